```python
import jax, jax.numpy as jnp
from jax import lax
import numpy as np

D_MODEL = 2048
BATCH = 8
SEQ = 4096
DEPTH = 4

ATTN_WIDTH = D_MODEL // 2
HEAD_DIM_ATTN = 128
N_HEADS_ATTN = ATTN_WIDTH // HEAD_DIM_ATTN
DILATION_PATTERNS = ((128, 1), (512, 4), (2048, 16))
RET_WIDTH = D_MODEL // 2
N_HEADS_RET = 4
RET_V_DIM = RET_WIDTH // N_HEADS_RET
RET_QK_DIM = RET_V_DIM // 2
RET_CHUNK = 128
MIX_WIDTH = ATTN_WIDTH + RET_WIDTH
IN_SPLIT_SIZES = (ATTN_WIDTH, ATTN_WIDTH, ATTN_WIDTH, ATTN_WIDTH,
                  N_HEADS_RET * RET_QK_DIM, N_HEADS_RET * RET_QK_DIM, RET_WIDTH, RET_WIDTH)
IN_PROJ_WIDTH = sum(IN_SPLIT_SIZES)
IN_SPLIT_IDX = tuple(int(i) for i in np.cumsum(IN_SPLIT_SIZES)[:-1])
NORM_EPS = 1e-6
MASK_VALUE = -1e30

kernel_name = "hybrid_dilated_attn_retention_encoder"


def rms_norm(x, g):
    xf = x.astype(jnp.float32)
    y = xf * lax.rsqrt(jnp.mean(xf * xf, axis=-1, keepdims=True) + NORM_EPS)
    return (y * g.astype(jnp.float32)).astype(x.dtype)


def alibi_slopes(n_heads):
    return jnp.exp2(-8.0 * (jnp.arange(n_heads, dtype=jnp.float32) + 1.0) / n_heads)


def dilated_window_attention(q, k, v, window, dilation, slopes):
    b, s, h, e = q.shape
    radius = window // (2 * dilation)
    blk = radius
    sub_len = s // dilation
    n_blk = -(-sub_len // blk)
    pad_len = n_blk * blk
    def to_sub(t):
        return t.reshape(b, sub_len, dilation, h, e)
    qs = jnp.pad(to_sub(q), ((0, 0), (0, pad_len - sub_len), (0, 0), (0, 0), (0, 0)))
    qb = qs.reshape(b, n_blk, blk, dilation, h, e)
    def key_windows(t):
        tp = jnp.pad(to_sub(t), ((0, 0), (blk, blk + pad_len - sub_len), (0, 0), (0, 0), (0, 0)))
        tb = tp.reshape(b, n_blk + 2, blk, dilation, h, e)
        return jnp.concatenate([tb[:, :-2], tb[:, 1:-1], tb[:, 2:]], axis=2)
    kw = key_windows(k)
    vw = key_windows(v)
    scores = jnp.einsum('bnirhe,bnjrhe->bnrhij', qb, kw).astype(jnp.float32)
    i_idx = jnp.arange(blk)[:, None]
    j_idx = jnp.arange(3 * blk)[None, :]
    rel = j_idx - blk - i_idx
    key_pos = jnp.arange(n_blk)[:, None] * blk + jnp.arange(3 * blk)[None, :] - blk
    valid = (key_pos >= 0) & (key_pos < sub_len)
    mask = (jnp.abs(rel) <= radius)[None] & valid[:, None, :]
    dist = (jnp.abs(rel) * dilation).astype(jnp.float32)
    bias = -slopes[:, None, None] * dist[None]
    scores = jnp.where(mask[None, :, None, None], scores + bias[None, None, None], MASK_VALUE)
    m = jnp.max(scores, axis=-1, keepdims=True)
    p = jnp.exp(scores - m)
    den = jnp.sum(p, axis=-1)
    o = jnp.einsum('bnrhij,bnjrhe->bnirhe', p, vw.astype(jnp.float32))
    o = o / den.transpose(0, 1, 4, 2, 3)[..., None]
    lse = (m[..., 0] + jnp.log(den)).transpose(0, 1, 4, 2, 3)
    o = o.reshape(b, pad_len, dilation, h, e)[:, :sub_len].reshape(b, s, h, e)
    lse = lse.reshape(b, pad_len, dilation, h)[:, :sub_len].reshape(b, s, h)
    return o, lse


def dilated_attention_mixture(q, k, v):
    slopes = alibi_slopes(q.shape[2])
    outs, lses = [], []
    for window, dilation in DILATION_PATTERNS:
        o, lse = dilated_window_attention(q, k, v, window, dilation, slopes)
        outs.append(o)
        lses.append(lse)
    w = jax.nn.softmax(jnp.stack(lses, axis=0), axis=0)
    o = jnp.sum(w[..., None] * jnp.stack(outs, axis=0), axis=0)
    return o.astype(v.dtype)


def retention_one_direction(q, k, v, log_gamma):
    b, s, h, dk = q.shape
    dv = v.shape[-1]
    c = RET_CHUNK
    n = s // c
    def chunk(t):
        return t.reshape(b, n, c, h, t.shape[-1]).transpose(1, 0, 3, 2, 4)
    qc, kc, vc = chunk(q), chunk(k), chunk(v)
    idx = jnp.arange(c, dtype=jnp.float32)
    rel = idx[:, None] - idx[None, :]
    lg = log_gamma[:, None, None]
    decay = jnp.where(rel[None] >= 0, jnp.exp(jnp.maximum(rel, 0.0)[None] * lg), 0.0).astype(q.dtype)
    xi = jnp.exp((idx[None] + 1.0) * log_gamma[:, None]).astype(q.dtype)[..., None]
    zeta = jnp.exp((c - 1.0 - idx[None]) * log_gamma[:, None]).astype(q.dtype)[..., None]
    g_chunk = jnp.exp(c * log_gamma).astype(q.dtype)[:, None, None]

    def step(state, inp):
        qi, ki, vi = inp
        inner = jnp.einsum('bhid,bhjd->bhij', qi, ki) * decay
        o = jnp.einsum('bhij,bhje->bhie', inner, vi) + jnp.einsum('bhid,bhde->bhie', qi, state) * xi
        state = state * g_chunk + jnp.einsum('bhjd,bhje->bhde', ki * zeta, vi)
        return state, o

    state0 = jnp.zeros((b, h, dk, dv), dtype=q.dtype)
    _, oc = lax.scan(step, state0, (qc, kc, vc))
    return oc.transpose(1, 0, 3, 2, 4).reshape(b, s, h, dv)


def bidirectional_retention(q, k, v, decay_logit_f, decay_logit_b):
    lg_f = jax.nn.log_sigmoid(decay_logit_f.astype(jnp.float32))
    lg_b = jax.nn.log_sigmoid(decay_logit_b.astype(jnp.float32))
    o_f = retention_one_direction(q, k, v, lg_f)
    o_b = jnp.flip(retention_one_direction(jnp.flip(q, 1), jnp.flip(k, 1), jnp.flip(v, 1), lg_b), 1)
    o = (o_f + o_b).astype(jnp.float32)
    o = o * lax.rsqrt(jnp.mean(o * o, axis=-1, keepdims=True) + NORM_EPS)
    return o.astype(v.dtype)


def hybrid_layer(x, c_act, g, w_ada, b_ada, w_in, w_out, dec_f, dec_b):
    b, s, _ = x.shape
    mod = c_act @ w_ada + b_ada
    shift, scale, gate = jnp.split(mod, 3, axis=-1)
    h = rms_norm(x, g) * (1.0 + scale[:, None]) + shift[:, None]
    proj = jnp.einsum('bsd,df->bsf', h, w_in)
    q_a, k_a, v_a, z_a, q_r, k_r, v_r, z_r = jnp.split(proj, IN_SPLIT_IDX, axis=-1)
    q_a = q_a.reshape(b, s, N_HEADS_ATTN, HEAD_DIM_ATTN) * (HEAD_DIM_ATTN ** -0.5)
    k_a = k_a.reshape(b, s, N_HEADS_ATTN, HEAD_DIM_ATTN)
    v_a = v_a.reshape(b, s, N_HEADS_ATTN, HEAD_DIM_ATTN)
    y_a = dilated_attention_mixture(q_a, k_a, v_a).reshape(b, s, ATTN_WIDTH)
    q_r = q_r.reshape(b, s, N_HEADS_RET, RET_QK_DIM)
    k_r = k_r.reshape(b, s, N_HEADS_RET, RET_QK_DIM) * (RET_QK_DIM ** -0.5)
    v_r = v_r.reshape(b, s, N_HEADS_RET, RET_V_DIM)
    y_r = bidirectional_retention(q_r, k_r, v_r, dec_f, dec_b).reshape(b, s, RET_WIDTH)
    y = jnp.concatenate([y_a * jax.nn.silu(z_a), y_r * jax.nn.silu(z_r)], axis=-1)
    out = jnp.einsum('bsf,fd->bsd', y, w_out)
    return x + gate[:, None] * out


def _fwd_setup_inputs(seed: int = 0) -> dict:
    key = jax.random.key(seed)
    ks = jax.random.split(key, 10)
    f32 = jnp.float32
    x = jax.random.normal(ks[0], (BATCH, SEQ, D_MODEL), f32)
    c = jax.random.normal(ks[1], (BATCH, D_MODEL), f32)
    norm_gain = 1.0 + 0.02 * jax.random.normal(ks[2], (DEPTH, D_MODEL), f32)
    w_ada = 0.5 * D_MODEL ** -0.5 * jax.random.normal(ks[3], (DEPTH, D_MODEL, 3 * D_MODEL), f32)
    b_ada = 0.02 * jax.random.normal(ks[4], (DEPTH, 3 * D_MODEL), f32)
    w_in = D_MODEL ** -0.5 * jax.random.normal(ks[5], (DEPTH, D_MODEL, IN_PROJ_WIDTH), f32)
    w_out = MIX_WIDTH ** -0.5 * jax.random.normal(ks[6], (DEPTH, MIX_WIDTH, D_MODEL), f32)
    gamma = 1.0 - jnp.exp2(-5.0 - jnp.arange(N_HEADS_RET, dtype=f32))
    base_logit = jnp.log(gamma) - jnp.log1p(-gamma)
    ret_decay_logit_f = base_logit[None] + 0.1 * jax.random.normal(ks[7], (DEPTH, N_HEADS_RET), f32)
    ret_decay_logit_b = base_logit[None] + 0.1 * jax.random.normal(ks[8], (DEPTH, N_HEADS_RET), f32)
    final_gain = 1.0 + 0.02 * jax.random.normal(ks[9], (D_MODEL,), f32)
    return {"x": x, "c": c, "norm_gain": norm_gain, "w_ada": w_ada, "b_ada": b_ada,
            "w_in": w_in, "w_out": w_out, "ret_decay_logit_f": ret_decay_logit_f,
            "ret_decay_logit_b": ret_decay_logit_b, "final_gain": final_gain}


def _fwd_reference(x, c, norm_gain, w_ada, b_ada, w_in, w_out, ret_decay_logit_f, ret_decay_logit_b, final_gain):
    c_act = jax.nn.silu(c)
    h = x
    for layer in range(DEPTH):
        h = hybrid_layer(h, c_act, norm_gain[layer], w_ada[layer], b_ada[layer], w_in[layer],
                         w_out[layer], ret_decay_logit_f[layer], ret_decay_logit_b[layer])
    return rms_norm(h, final_gain)


import jax as _jax
import jax.numpy as _jnp

TWIN_FORMAT = 'train_step'
FWD_PARAMS = ['x', 'c', 'norm_gain', 'w_ada', 'b_ada', 'w_in', 'w_out', 'ret_decay_logit_f', 'ret_decay_logit_b', 'final_gain']
TWIN_WEIGHTS = ['norm_gain', 'w_ada', 'b_ada', 'w_in', 'w_out', 'ret_decay_logit_f', 'ret_decay_logit_b', 'final_gain']
TWIN_DIFF_INPUT = 'x'
TWIN_INPUTS = ['x', 'c', 'norm_gain', 'w_ada', 'b_ada', 'w_in', 'w_out', 'ret_decay_logit_f', 'ret_decay_logit_b', 'final_gain', 'loss_target', 'm_norm_gain', 'm_w_ada', 'm_b_ada', 'm_w_in', 'm_w_out', 'm_ret_decay_logit_f', 'm_ret_decay_logit_b', 'm_final_gain', 'v_norm_gain', 'v_w_ada', 'v_b_ada', 'v_w_in', 'v_w_out', 'v_ret_decay_logit_f', 'v_ret_decay_logit_b', 'v_final_gain']
TWIN_OUTPUTS = ['loss', 'grad_x', 'grad_norm_gain', 'grad_w_ada', 'grad_b_ada', 'grad_w_in', 'grad_w_out', 'grad_ret_decay_logit_f', 'grad_ret_decay_logit_b', 'grad_final_gain', 'delta_norm_gain', 'delta_w_ada', 'delta_b_ada', 'delta_w_in', 'delta_w_out', 'delta_ret_decay_logit_f', 'delta_ret_decay_logit_b', 'delta_final_gain', 'new_m_norm_gain', 'new_m_w_ada', 'new_m_b_ada', 'new_m_w_in', 'new_m_w_out', 'new_m_ret_decay_logit_f', 'new_m_ret_decay_logit_b', 'new_m_final_gain', 'new_v_norm_gain', 'new_v_w_ada', 'new_v_b_ada', 'new_v_w_in', 'new_v_w_out', 'new_v_ret_decay_logit_f', 'new_v_ret_decay_logit_b', 'new_v_final_gain']
TWIN_LEAF_KINDS = {'loss': 'loss', 'grad_x': 'grad_x', 'grad_norm_gain': 'grad_w', 'grad_w_ada': 'grad_w', 'grad_b_ada': 'grad_w', 'grad_w_in': 'grad_w', 'grad_w_out': 'grad_w', 'grad_ret_decay_logit_f': 'grad_w', 'grad_ret_decay_logit_b': 'grad_w', 'grad_final_gain': 'grad_w', 'delta_norm_gain': 'delta_w', 'delta_w_ada': 'delta_w', 'delta_b_ada': 'delta_w', 'delta_w_in': 'delta_w', 'delta_w_out': 'delta_w', 'delta_ret_decay_logit_f': 'delta_w', 'delta_ret_decay_logit_b': 'delta_w', 'delta_final_gain': 'delta_w', 'new_m_norm_gain': 'new_m', 'new_m_w_ada': 'new_m', 'new_m_b_ada': 'new_m', 'new_m_w_in': 'new_m', 'new_m_w_out': 'new_m', 'new_m_ret_decay_logit_f': 'new_m', 'new_m_ret_decay_logit_b': 'new_m', 'new_m_final_gain': 'new_m', 'new_v_norm_gain': 'new_v', 'new_v_w_ada': 'new_v', 'new_v_b_ada': 'new_v', 'new_v_w_in': 'new_v', 'new_v_w_out': 'new_v', 'new_v_ret_decay_logit_f': 'new_v', 'new_v_ret_decay_logit_b': 'new_v', 'new_v_final_gain': 'new_v'}


def _forward(args):
    return _fwd_reference(*[args[k] for k in FWD_PARAMS])


def _output_shape():
    def fwd():
        inp = _fwd_setup_inputs(0)
        return _fwd_reference(*[inp[k] for k in FWD_PARAMS])
    out = _jax.eval_shape(fwd)
    return out.shape, out.dtype

N_MICROBATCH = 1
ADAM_LR = 0.001
ADAM_B1 = 0.9
ADAM_B2 = 0.999
ADAM_EPS = 1e-08
ADAM_WD = 0.01
ADAM_STEP = 10
PER_EXAMPLE_BATCH_AXIS = {'x': 0, 'c': 0, 'loss_target': 0}
SHARED_INPUTS = []
_WEIGHT_DTYPES = {'norm_gain': _jnp.float32, 'w_ada': _jnp.float32, 'b_ada': _jnp.float32, 'w_in': _jnp.float32, 'w_out': _jnp.float32, 'ret_decay_logit_f': _jnp.float32, 'ret_decay_logit_b': _jnp.float32, 'final_gain': _jnp.float32}
MOMENT_SCALE = {'norm_gain': 2.472391e-02, 'w_ada': 2.561428e-02, 'b_ada': 4.294208e-02, 'w_in': 1.541706e-02, 'w_out': 1.359968e-02, 'ret_decay_logit_f': 1.250783e-01, 'ret_decay_logit_b': 9.625845e-02, 'final_gain': 1.598942e+01}


def _to_microbatches(a, axis):
    t = _jnp.moveaxis(a, axis, 0)
    t = t.reshape((N_MICROBATCH, t.shape[0] // N_MICROBATCH) + t.shape[1:])
    return _jnp.moveaxis(t, 1, axis + 1)


def setup_inputs(seed: int = 0) -> dict:
    inp = _fwd_setup_inputs(seed)
    key = _jax.random.fold_in(_jax.random.key(seed), 7919)
    shape, _ = _output_shape()
    out = dict(inp)
    out["loss_target"] = _jax.random.normal(_jax.random.fold_in(key, 0), shape, _jnp.float32)
    for i, name in enumerate(TWIN_WEIGHTS):
        w = inp[name].astype(_jnp.float32)
        if MOMENT_SCALE is None:
            s = _jnp.sqrt(_jnp.mean(_jnp.square(w)) + 1e-30)
        else:
            s = MOMENT_SCALE[name]
        km, kv = _jax.random.split(_jax.random.fold_in(key, i + 1))
        out[name] = w
        out["m_" + name] = s * _jax.random.normal(km, w.shape, _jnp.float32)
        out["v_" + name] = (s * s) * _jax.random.uniform(kv, w.shape, _jnp.float32, 0.5, 1.5)
    if N_MICROBATCH > 1:
        for name, axis in PER_EXAMPLE_BATCH_AXIS.items():
            out[name] = _to_microbatches(out[name], axis)
    return {'x': out['x'], 'c': out['c'], 'norm_gain': out['norm_gain'], 'w_ada': out['w_ada'], 'b_ada': out['b_ada'], 'w_in': out['w_in'], 'w_out': out['w_out'], 'ret_decay_logit_f': out['ret_decay_logit_f'], 'ret_decay_logit_b': out['ret_decay_logit_b'], 'final_gain': out['final_gain'], 'loss_target': out['loss_target'], 'm_norm_gain': out['m_norm_gain'], 'm_w_ada': out['m_w_ada'], 'm_b_ada': out['m_b_ada'], 'm_w_in': out['m_w_in'], 'm_w_out': out['m_w_out'], 'm_ret_decay_logit_f': out['m_ret_decay_logit_f'], 'm_ret_decay_logit_b': out['m_ret_decay_logit_b'], 'm_final_gain': out['m_final_gain'], 'v_norm_gain': out['v_norm_gain'], 'v_w_ada': out['v_w_ada'], 'v_b_ada': out['v_b_ada'], 'v_w_in': out['v_w_in'], 'v_w_out': out['v_w_out'], 'v_ret_decay_logit_f': out['v_ret_decay_logit_f'], 'v_ret_decay_logit_b': out['v_ret_decay_logit_b'], 'v_final_gain': out['v_final_gain']}


def _loss(weights, diff, rest, loss_target):
    with _jax.named_scope("forward"):
        args = {**rest, TWIN_DIFF_INPUT: diff, **{k: w.astype(_WEIGHT_DTYPES[k]) for k, w in weights.items()}}
        y = _forward(args)
    with _jax.named_scope("loss_head"):
        err = _jnp.square(y.astype(_jnp.float32) - loss_target)
        return 0.5 * _jnp.sum(_jnp.mean(err, axis=-1)) if err.ndim else 0.5 * err


def _adamw(w, g, m, v):
    m = ADAM_B1 * m + (1.0 - ADAM_B1) * g
    v = ADAM_B2 * v + (1.0 - ADAM_B2) * _jnp.square(g)
    m_hat = m / (1.0 - ADAM_B1 ** ADAM_STEP)
    v_hat = v / (1.0 - ADAM_B2 ** ADAM_STEP)
    delta = -ADAM_LR * (m_hat / (_jnp.sqrt(v_hat) + ADAM_EPS) + ADAM_WD * w)
    return delta, m, v


def reference(x, c, norm_gain, w_ada, b_ada, w_in, w_out, ret_decay_logit_f, ret_decay_logit_b, final_gain, loss_target, m_norm_gain, m_w_ada, m_b_ada, m_w_in, m_w_out, m_ret_decay_logit_f, m_ret_decay_logit_b, m_final_gain, v_norm_gain, v_w_ada, v_b_ada, v_w_in, v_w_out, v_ret_decay_logit_f, v_ret_decay_logit_b, v_final_gain):
    given = dict(x=x, c=c, norm_gain=norm_gain, w_ada=w_ada, b_ada=b_ada, w_in=w_in, w_out=w_out, ret_decay_logit_f=ret_decay_logit_f, ret_decay_logit_b=ret_decay_logit_b, final_gain=final_gain, loss_target=loss_target, m_norm_gain=m_norm_gain, m_w_ada=m_w_ada, m_b_ada=m_b_ada, m_w_in=m_w_in, m_w_out=m_w_out, m_ret_decay_logit_f=m_ret_decay_logit_f, m_ret_decay_logit_b=m_ret_decay_logit_b, m_final_gain=m_final_gain, v_norm_gain=v_norm_gain, v_w_ada=v_w_ada, v_b_ada=v_b_ada, v_w_in=v_w_in, v_w_out=v_w_out, v_ret_decay_logit_f=v_ret_decay_logit_f, v_ret_decay_logit_b=v_ret_decay_logit_b, v_final_gain=v_final_gain)
    weights = {n: given[n] for n in TWIN_WEIGHTS}
    shared = {n: given[n] for n in SHARED_INPUTS}
    per_example = {n: given[n] for n in ['x', 'c']}
    grad_fn = _jax.value_and_grad(_loss, argnums=(0, 1))

    def one_microbatch(ex, loss_target):
        ex = dict(ex)
        diff = ex.pop(TWIN_DIFF_INPUT)
        return grad_fn(weights, diff, {**shared, **ex}, loss_target)

    if N_MICROBATCH == 1:
        loss, (grad_w, grad_x) = one_microbatch(per_example, given["loss_target"])
    else:
        def body(carry, xs):
            loss_sum, grad_sum = carry
            l_k, (gw_k, gx_k) = one_microbatch(xs[0], xs[1])
            with _jax.named_scope("update"):
                return (loss_sum + l_k, _jax.tree.map(_jnp.add, grad_sum, gw_k)), gx_k

        init = (_jnp.zeros((), _jnp.float32), _jax.tree.map(_jnp.zeros_like, weights))
        (loss, grad_w), grad_x = _jax.lax.scan(body, init, (per_example, given["loss_target"]))
    with _jax.named_scope("update"):
        delta_w, new_m, new_v = {}, {}, {}
        for n in TWIN_WEIGHTS:
            delta_w[n], new_m[n], new_v[n] = _adamw(weights[n], grad_w[n], given["m_" + n], given["v_" + n])
    return (loss, grad_x, *[grad_w[n] for n in TWIN_WEIGHTS], *[delta_w[n] for n in TWIN_WEIGHTS],
            *[new_m[n] for n in TWIN_WEIGHTS], *[new_v[n] for n in TWIN_WEIGHTS])
```

```python
import functools

import jax
import jax.numpy as jnp
from jax import lax
from jax.experimental import pallas as pl
from jax.experimental.pallas import tpu as pltpu

F32 = jnp.float32
BF16 = jnp.bfloat16

D_MODEL = 2048
SEQ = 4096
DEPTH = 4
HEAD_DIM = 128
DILATIONS = (1, 4, 16)
RADIUS = 64
N_HEADS_RET = 4
RET_QK = 128
RET_V = 256
RET_CHUNK = 128
NORM_EPS = 1e-6
MASK_VALUE = -1e30
N_DEV = 8
N_CHIP = 4
LANES = 128
VMEM_LIMIT = 56 * 1024 * 1024

ADAM_LR = 0.001
ADAM_B1 = 0.9
ADAM_B2 = 0.999
ADAM_EPS = 1e-08
ADAM_WD = 0.01
ADAM_STEP = 10

MESH = pl.DeviceIdType.MESH


class _Cfg:
    def __init__(self):
        self.d = D_MODEL
        self.s = SEQ
        self.aw = D_MODEL // 2
        self.ha = self.aw // HEAD_DIM
        self.rw = D_MODEL // 2
        self.hr = N_HEADS_RET
        self.rqk = self.hr * RET_QK
        self.f = 4 * self.aw + 2 * self.rqk + 2 * self.rw
        self.qa, self.ka, self.va, self.za = 0, self.aw, 2 * self.aw, 3 * self.aw
        self.qr = 4 * self.aw
        self.kr = self.qr + self.rqk
        self.vr = self.kr + self.rqk
        self.zr = self.vr + self.rw
        assert self.rw == self.hr * RET_V


def _tile(n, pref):
    t = min(n, pref)
    while n % t or t % LANES:
        t -= LANES
    return t


def _params(dims=None):
    return pltpu.CompilerParams(dimension_semantics=dims, vmem_limit_bytes=VMEM_LIMIT)


def _silu(z):
    return z * jax.nn.sigmoid(z)


def _dsilu(z):
    sg = jax.nn.sigmoid(z)
    return sg * (1.0 + z * (1.0 - sg))


_DN = {"nn": (((1,), (0,)), ((), ())), "nt": (((1,), (1,)), ((), ())), "tn": (((0,), (0,)), ((), ()))}


def _matmul(a, b, mode, out_dtype, name, tm=1024, tn=1024, tk=2048, layer=None):
    if mode == "tn":
        kk, m = a.shape
    else:
        m, kk = a.shape
    bshape = b.shape if layer is None else b.shape[1:]
    n = bshape[0] if mode == "nt" else bshape[1]
    tm, tn, tk = _tile(m, tm), _tile(n, tn), _tile(kk, tk)
    nk = kk // tk
    a_spec = (pl.BlockSpec((tk, tm), lambda i, j, k: (k, i)) if mode == "tn"
              else pl.BlockSpec((tm, tk), lambda i, j, k: (i, k)))
    if layer is None:
        b_spec = (pl.BlockSpec((tn, tk), lambda i, j, k: (j, k)) if mode == "nt"
                  else pl.BlockSpec((tk, tn), lambda i, j, k: (k, j)))
    else:
        b_spec = (pl.BlockSpec((None, tn, tk), lambda i, j, k: (layer, j, k)) if mode == "nt"
                  else pl.BlockSpec((None, tk, tn), lambda i, j, k: (layer, k, j)))
    dn = _DN[mode]

    def body(a_ref, b_ref, o_ref, acc_ref):
        k = pl.program_id(2)
        p = lax.dot_general(a_ref[...], b_ref[...], dn, preferred_element_type=F32)

        @pl.when(k == 0)
        def _():
            acc_ref[...] = p

        @pl.when(k > 0)
        def _():
            acc_ref[...] += p

        @pl.when(k == nk - 1)
        def _():
            o_ref[...] = acc_ref[...].astype(out_dtype)

    return pl.pallas_call(
        body, name=name, grid=(m // tm, n // tn, nk),
        in_specs=[a_spec, b_spec],
        out_specs=pl.BlockSpec((tm, tn), lambda i, j, k: (i, j)),
        out_shape=jax.ShapeDtypeStruct((m, n), out_dtype),
        scratch_shapes=[pltpu.VMEM((tm, tn), F32)],
        compiler_params=_params(("parallel", "parallel", "arbitrary")),
    )(a, b)


def _out_proj_fwd(y, w_out, layer, x, mod3, b3, name):
    s, kk = y.shape
    d = w_out.shape[2]
    tm, tn = _tile(s, 512), _tile(d, 1024)

    def body(y_ref, w_ref, x_ref, m_ref, b_ref, xn_ref, o_ref):
        out = jnp.dot(y_ref[...], w_ref[...], preferred_element_type=F32)
        gate = m_ref[2:3, :] + b_ref[2:3, :]
        xn_ref[...] = x_ref[...] + gate * out
        o_ref[...] = out.astype(BF16)

    vec = pl.BlockSpec((8, tn), lambda i, j: (0, j))
    return pl.pallas_call(
        body, name=name, grid=(s // tm, d // tn),
        in_specs=[pl.BlockSpec((tm, kk), lambda i, j: (i, 0)), pl.BlockSpec((None, kk, tn), lambda i, j: (layer, 0, j)),
                  pl.BlockSpec((tm, tn), lambda i, j: (i, j)), vec, vec],
        out_specs=[pl.BlockSpec((tm, tn), lambda i, j: (i, j)), pl.BlockSpec((tm, tn), lambda i, j: (i, j))],
        out_shape=[jax.ShapeDtypeStruct((s, d), F32), jax.ShapeDtypeStruct((s, d), BF16)],
        compiler_params=_params(("parallel", "parallel")),
    )(y, w_out, x, mod3, b3)


def _norm_mod_fwd(x, g8, mod3, b3, name):
    s, d = x.shape
    tr = _tile(s, 512)

    def body(x_ref, g_ref, m_ref, b_ref, h_ref):
        xv = x_ref[...]
        r = lax.rsqrt(jnp.mean(xv * xv, axis=-1, keepdims=True) + NORM_EPS)
        shift = m_ref[0:1, :] + b_ref[0:1, :]
        scale = m_ref[1:2, :] + b_ref[1:2, :]
        h_ref[...] = ((xv * r * g_ref[0:1, :]) * (1.0 + scale) + shift).astype(BF16)

    vec = pl.BlockSpec((8, d), lambda i: (0, 0))
    return pl.pallas_call(
        body, name=name, grid=(s // tr,),
        in_specs=[pl.BlockSpec((tr, d), lambda i: (i, 0)), vec, vec, vec],
        out_specs=pl.BlockSpec((tr, d), lambda i: (i, 0)),
        out_shape=jax.ShapeDtypeStruct((s, d), BF16),
        compiler_params=_params(("parallel",)),
    )(x, g8, mod3, b3)


def _norm_mod_bwd(dh, x, dx_next, g8, mod3, b3, name):
    s, d = x.shape
    tr = _tile(s, 256)

    def body(dh_ref, x_ref, dn_ref, g_ref, m_ref, b_ref, dx_ref, acc_ref):
        @pl.when(pl.program_id(0) == 0)
        def _():
            acc_ref[...] = jnp.zeros_like(acc_ref)

        xv = x_ref[...]
        dh_v = dh_ref[...]
        g = g_ref[0:1, :]
        r = lax.rsqrt(jnp.mean(xv * xv, axis=-1, keepdims=True) + NORM_EPS)
        xn = xv * r
        scale1 = 1.0 + m_ref[1:2, :] + b_ref[1:2, :]
        dhs = dh_v * scale1
        dxn = dhs * g
        dx_ref[...] = dn_ref[...] + r * (dxn - xn * jnp.mean(dxn * xn, axis=-1, keepdims=True))
        acc_ref[0:1, :] += jnp.sum(dh_v, axis=0, keepdims=True)
        acc_ref[1:2, :] += jnp.sum(dh_v * (xn * g), axis=0, keepdims=True)
        acc_ref[2:3, :] += jnp.sum(dhs * xn, axis=0, keepdims=True)

    vec = pl.BlockSpec((8, d), lambda i: (0, 0))
    row = pl.BlockSpec((tr, d), lambda i: (i, 0))
    return pl.pallas_call(
        body, name=name, grid=(s // tr,),
        in_specs=[row, row, row, vec, vec, vec],
        out_specs=[row, vec],
        out_shape=[jax.ShapeDtypeStruct((s, d), F32), jax.ShapeDtypeStruct((8, d), F32)],
        compiler_params=_params(("arbitrary",)),
    )(dh, x, dx_next, g8, mod3, b3)


def _final_loss(x, tgt, g8, name):
    s, d = x.shape
    tr = _tile(s, 256)

    def body(x_ref, t_ref, g_ref, dx_ref, loss_ref, acc_ref):
        @pl.when(pl.program_id(0) == 0)
        def _():
            acc_ref[...] = jnp.zeros_like(acc_ref)
            loss_ref[...] = jnp.zeros_like(loss_ref)

        xv = x_ref[...]
        g = g_ref[0:1, :]
        r = lax.rsqrt(jnp.mean(xv * xv, axis=-1, keepdims=True) + NORM_EPS)
        xn = xv * r
        err = xn * g - t_ref[...]
        loss_ref[...] += 0.5 * jnp.sum(jnp.mean(err * err, axis=-1, keepdims=True), axis=0, keepdims=True)
        dy = err * (1.0 / d)
        acc_ref[0:1, :] += jnp.sum(dy * xn, axis=0, keepdims=True)
        dxn = dy * g
        dx_ref[...] = r * (dxn - xn * jnp.mean(dxn * xn, axis=-1, keepdims=True))

    vec = pl.BlockSpec((8, d), lambda i: (0, 0))
    row = pl.BlockSpec((tr, d), lambda i: (i, 0))
    return pl.pallas_call(
        body, name=name, grid=(s // tr,),
        in_specs=[row, row, vec],
        out_specs=[row, pl.BlockSpec((8, LANES), lambda i: (0, 0)), vec],
        out_shape=[jax.ShapeDtypeStruct((s, d), F32), jax.ShapeDtypeStruct((8, LANES), F32),
                   jax.ShapeDtypeStruct((8, d), F32)],
        compiler_params=_params(("arbitrary",)),
    )(x, tgt, g8)


def _gate_fwd(cfg, os_, ls_, proj, oret, name):
    s = cfg.s
    aw, rw = cfg.aw, cfg.rw
    tr = _tile(s, 256)

    def body(o1, o2, o3, l1, l2, l3, za_ref, or_ref, zr_ref, y_ref, oa_ref, lse_ref):
        la, lb, lc = l1[...], l2[...], l3[...]
        m = jnp.maximum(jnp.maximum(la, lb), lc)
        ea, eb, ec = jnp.exp(la - m), jnp.exp(lb - m), jnp.exp(lc - m)
        den = ea + eb + ec
        o = (ea * o1[...].astype(F32) + eb * o2[...].astype(F32) + ec * o3[...].astype(F32)) / den
        lse_ref[...] = m + jnp.log(den)
        oa_ref[...] = o.astype(BF16)
        y_ref[:, 0:aw] = (o * _silu(za_ref[...].astype(F32))).astype(BF16)
        for h in range(cfg.hr):
            cols = slice(h * RET_V, (h + 1) * RET_V)
            oh = or_ref[:, cols]
            rr = lax.rsqrt(jnp.mean(oh * oh, axis=-1, keepdims=True) + NORM_EPS)
            y_ref[:, aw + h * RET_V:aw + (h + 1) * RET_V] = (
                oh * rr * _silu(zr_ref[:, cols].astype(F32))).astype(BF16)

    ra = pl.BlockSpec((tr, aw), lambda i: (i, 0))
    return pl.pallas_call(
        body, name=name, grid=(s // tr,),
        in_specs=[ra, ra, ra, ra, ra, ra,
                  pl.BlockSpec((tr, aw), lambda i: (i, cfg.za // aw)),
                  pl.BlockSpec((tr, rw), lambda i: (i, 0)),
                  pl.BlockSpec((tr, rw), lambda i: (i, cfg.zr // rw))],
        out_specs=[pl.BlockSpec((tr, cfg.d), lambda i: (i, 0)), ra, ra],
        out_shape=[jax.ShapeDtypeStruct((s, cfg.d), BF16), jax.ShapeDtypeStruct((s, aw), BF16),
                   jax.ShapeDtypeStruct((s, aw), F32)],
        compiler_params=_params(("parallel",)),
    )(*os_, *ls_, proj, oret, proj)


def _gate_bwd(cfg, dy, oa, proj, oret, name):
    s = cfg.s
    aw, rw = cfg.aw, cfg.rw
    tr = _tile(s, 256)

    def body(dy_ref, oa_ref, za_ref, or_ref, zr_ref, do_ref, dl_ref, dza_ref, dor_ref, dzr_ref):
        dya = dy_ref[:, 0:aw].astype(F32)
        za = za_ref[...].astype(F32)
        o = oa_ref[...].astype(F32)
        do = dya * _silu(za)
        do_ref[...] = do.astype(BF16)
        dza_ref[...] = (dya * o * _dsilu(za)).astype(BF16)
        prod = do * o
        for h in range(cfg.ha):
            cols = slice(h * HEAD_DIM, (h + 1) * HEAD_DIM)
            dl_ref[:, cols] = jnp.broadcast_to(jnp.sum(prod[:, cols], axis=-1, keepdims=True), (tr, HEAD_DIM))
        for h in range(cfg.hr):
            cols = slice(h * RET_V, (h + 1) * RET_V)
            oh = or_ref[:, cols]
            zr = zr_ref[:, cols].astype(F32)
            dyr = dy_ref[:, aw + h * RET_V:aw + (h + 1) * RET_V].astype(F32)
            rr = lax.rsqrt(jnp.mean(oh * oh, axis=-1, keepdims=True) + NORM_EPS)
            yn = oh * rr
            dyn = dyr * _silu(zr)
            dzr_ref[:, cols] = (dyr * yn * _dsilu(zr)).astype(BF16)
            dor_ref[:, cols] = (rr * (dyn - yn * jnp.mean(dyn * yn, axis=-1, keepdims=True))).astype(BF16)

    ra = pl.BlockSpec((tr, aw), lambda i: (i, 0))
    rr_ = pl.BlockSpec((tr, rw), lambda i: (i, 0))
    return pl.pallas_call(
        body, name=name, grid=(s // tr,),
        in_specs=[pl.BlockSpec((tr, cfg.d), lambda i: (i, 0)), ra,
                  pl.BlockSpec((tr, aw), lambda i: (i, cfg.za // aw)), rr_,
                  pl.BlockSpec((tr, rw), lambda i: (i, cfg.zr // rw))],
        out_specs=[ra, ra, ra, rr_, rr_],
        out_shape=[jax.ShapeDtypeStruct((s, aw), BF16), jax.ShapeDtypeStruct((s, aw), F32),
                   jax.ShapeDtypeStruct((s, aw), BF16), jax.ShapeDtypeStruct((s, rw), BF16),
                   jax.ShapeDtypeStruct((s, rw), BF16)],
        compiler_params=_params(("parallel",)),
    )(dy, oa, proj, oret, proj)


def _out_proj_bwd_prep(dxn, out, mod3, b3, name):
    s, d = dxn.shape
    tr = _tile(s, 512)

    def body(dx_ref, o_ref, m_ref, b_ref, do_ref, acc_ref):
        @pl.when(pl.program_id(0) == 0)
        def _():
            acc_ref[...] = jnp.zeros_like(acc_ref)

        dxv = dx_ref[...]
        gate = m_ref[2:3, :] + b_ref[2:3, :]
        do_ref[...] = (gate * dxv).astype(BF16)
        acc_ref[0:1, :] += jnp.sum(dxv * o_ref[...].astype(F32), axis=0, keepdims=True)

    vec = pl.BlockSpec((8, d), lambda i: (0, 0))
    row = pl.BlockSpec((tr, d), lambda i: (i, 0))
    return pl.pallas_call(
        body, name=name, grid=(s // tr,),
        in_specs=[row, row, vec, vec],
        out_specs=[row, vec],
        out_shape=[jax.ShapeDtypeStruct((s, d), BF16), jax.ShapeDtypeStruct((8, d), F32)],
        compiler_params=_params(("arbitrary",)),
    )(dxn, out, mod3, b3)


def _assemble_dproj(cfg, dqs, dks, dvs, dza, rdq, rdk, rdv, dzr, name):
    s = cfg.s
    aw, rw, rqk = cfg.aw, cfg.rw, cfg.rqk
    tr = _tile(s, 256)

    def body(q1, q2, q3, k1, k2, k3, v1, v2, v3, dza_ref, rq, rk, rv, dzr_ref, o_ref):
        def add3(a, b, c):
            return (a[...].astype(F32) + b[...].astype(F32) + c[...].astype(F32)).astype(BF16)

        o_ref[:, cfg.qa:cfg.qa + aw] = add3(q1, q2, q3)
        o_ref[:, cfg.ka:cfg.ka + aw] = add3(k1, k2, k3)
        o_ref[:, cfg.va:cfg.va + aw] = add3(v1, v2, v3)
        o_ref[:, cfg.za:cfg.za + aw] = dza_ref[...]
        o_ref[:, cfg.qr:cfg.qr + rqk] = rq[...].astype(BF16)
        o_ref[:, cfg.kr:cfg.kr + rqk] = rk[...].astype(BF16)
        o_ref[:, cfg.vr:cfg.vr + rw] = rv[...].astype(BF16)
        o_ref[:, cfg.zr:cfg.zr + rw] = dzr_ref[...]

    ra = pl.BlockSpec((tr, aw), lambda i: (i, 0))
    rq_ = pl.BlockSpec((tr, rqk), lambda i: (i, 0))
    rv_ = pl.BlockSpec((tr, rw), lambda i: (i, 0))
    return pl.pallas_call(
        body, name=name, grid=(s // tr,),
        in_specs=[ra] * 10 + [rq_, rq_, rv_, rv_],
        out_specs=pl.BlockSpec((tr, cfg.f), lambda i: (i, 0)),
        out_shape=jax.ShapeDtypeStruct((s, cfg.f), BF16),
        compiler_params=_params(("parallel",)),
    )(*dqs, *dks, *dvs, dza, rdq, rdk, rdv, dzr)


def _attn_geometry(cfg, dil):
    sub = cfg.s // dil
    bq = min(128, sub)
    win = min(256, sub)
    return sub, bq, win


def _attn_scores(q_ref, k_ref, slope, dil, i, sub, bq, win):
    qs = pl.multiple_of(i * bq, bq)
    ws = pl.multiple_of(jnp.clip(i * bq - (win - bq) // 2, 0, sub - win), 64)
    q = (q_ref[pl.ds(qs, bq), :].astype(F32) * (HEAD_DIM ** -0.5)).astype(BF16)
    kw = k_ref[pl.ds(ws, win), :]
    sc = lax.dot_general(q, kw, _DN["nt"], preferred_element_type=F32)
    rel = (ws - qs) + lax.broadcasted_iota(jnp.int32, (bq, win), 1) - lax.broadcasted_iota(jnp.int32, (bq, win), 0)
    arel = jnp.abs(rel)
    sc = jnp.where(arel <= RADIUS, sc - (slope * dil) * arel.astype(F32), MASK_VALUE)
    return qs, ws, q, kw, sc


def _attn_fwd(cfg, proj, slopes, dil, name):
    sub, bq, win = _attn_geometry(cfg, dil)
    nb = cfg.f // HEAD_DIM
    ha = cfg.ha
    view = proj.reshape(sub, dil * cfg.f)

    def body(sl_ref, q_ref, k_ref, v_ref, o_ref, l_ref):
        slope = sl_ref[pl.program_id(0)]

        def blk(i, carry):
            qs, ws, _, _, sc = _attn_scores(q_ref, k_ref, slope, dil, i, sub, bq, win)
            m = jnp.max(sc, axis=-1, keepdims=True)
            p = jnp.exp(sc - m)
            den = jnp.sum(p, axis=-1, keepdims=True)
            o = jnp.dot(p.astype(BF16), v_ref[pl.ds(ws, win), :], preferred_element_type=F32) / den
            o_ref[pl.ds(qs, bq), :] = o.astype(BF16)
            l_ref[pl.ds(qs, bq), :] = jnp.broadcast_to(m + jnp.log(den), (bq, HEAD_DIM))
            return carry

        lax.fori_loop(0, sub // bq, blk, 0)

    def col(off):
        return pl.BlockSpec((sub, HEAD_DIM), lambda h, r: (0, r * nb + off // HEAD_DIM + h))

    out_spec = pl.BlockSpec((sub, HEAD_DIM), lambda h, r: (0, r * ha + h))
    o, lse = pl.pallas_call(
        body, name=name, grid=(ha, dil),
        in_specs=[pl.BlockSpec(memory_space=pltpu.SMEM), col(cfg.qa), col(cfg.ka), col(cfg.va)],
        out_specs=[out_spec, out_spec],
        out_shape=[jax.ShapeDtypeStruct((sub, dil * cfg.aw), BF16), jax.ShapeDtypeStruct((sub, dil * cfg.aw), F32)],
        compiler_params=_params(("parallel", "parallel")),
    )(slopes, view, view, view)
    return o.reshape(cfg.s, cfg.aw), lse.reshape(cfg.s, cfg.aw)


def _attn_bwd(cfg, proj, slopes, do, lse, dlt, dil, name):
    sub, bq, win = _attn_geometry(cfg, dil)
    nb = cfg.f // HEAD_DIM
    ha = cfg.ha
    view = proj.reshape(sub, dil * cfg.f)
    dov = do.reshape(sub, dil * cfg.aw)
    lsev = lse.reshape(sub, dil * cfg.aw)
    dltv = dlt.reshape(sub, dil * cfg.aw)
    rep = win // HEAD_DIM

    def body(sl_ref, q_ref, k_ref, v_ref, do_ref, l_ref, dl_ref, dq_ref, dk_ref, dv_ref, dk_acc, dv_acc):
        slope = sl_ref[pl.program_id(0)]
        dk_acc[...] = jnp.zeros_like(dk_acc)
        dv_acc[...] = jnp.zeros_like(dv_acc)

        def blk(i, carry):
            qs, ws, q, kw, sc = _attn_scores(q_ref, k_ref, slope, dil, i, sub, bq, win)
            dob = do_ref[pl.ds(qs, bq), :]
            p = jnp.exp(sc - jnp.tile(l_ref[pl.ds(qs, bq), :], (1, rep)))
            dp = lax.dot_general(dob, v_ref[pl.ds(ws, win), :], _DN["nt"], preferred_element_type=F32)
            ds = (p * (dp - jnp.tile(dl_ref[pl.ds(qs, bq), :], (1, rep)))).astype(BF16)
            dq = jnp.dot(ds, kw, preferred_element_type=F32) * (HEAD_DIM ** -0.5)
            dq_ref[pl.ds(qs, bq), :] = dq.astype(BF16)
            dk_acc[pl.ds(ws, win), :] += lax.dot_general(ds, q, _DN["tn"], preferred_element_type=F32)
            dv_acc[pl.ds(ws, win), :] += lax.dot_general(p.astype(BF16), dob, _DN["tn"], preferred_element_type=F32)
            return carry

        lax.fori_loop(0, sub // bq, blk, 0)
        dk_ref[...] = dk_acc[...].astype(BF16)
        dv_ref[...] = dv_acc[...].astype(BF16)

    def col(off):
        return pl.BlockSpec((sub, HEAD_DIM), lambda h, r: (0, r * nb + off // HEAD_DIM + h))

    hs = pl.BlockSpec((sub, HEAD_DIM), lambda h, r: (0, r * ha + h))
    shp = jax.ShapeDtypeStruct((sub, dil * cfg.aw), BF16)
    dq, dk, dv = pl.pallas_call(
        body, name=name, grid=(ha, dil),
        in_specs=[pl.BlockSpec(memory_space=pltpu.SMEM), col(cfg.qa), col(cfg.ka), col(cfg.va), hs, hs, hs],
        out_specs=[hs, hs, hs],
        out_shape=[shp, shp, shp],
        scratch_shapes=[pltpu.VMEM((sub, HEAD_DIM), F32), pltpu.VMEM((sub, HEAD_DIM), F32)],
        compiler_params=_params(("parallel", "parallel")),
    )(slopes, view, view, view, dov, lsev, dltv)
    return tuple(t.reshape(cfg.s, cfg.aw) for t in (dq, dk, dv))


def _decay_tables(lg, backward):
    c = RET_CHUNK
    a = lax.broadcasted_iota(jnp.int32, (c, c), 0)
    b = lax.broadcasted_iota(jnp.int32, (c, c), 1)
    idx = lax.broadcasted_iota(jnp.int32, (c, 1), 0).astype(F32)
    if backward:
        rel = (b - a).astype(F32)
        ex_xi = c - idx
        ex_zeta = idx
    else:
        rel = (a - b).astype(F32)
        ex_xi = idx + 1.0
        ex_zeta = c - 1.0 - idx
    relc = jnp.maximum(rel, 0.0)
    dm = jnp.where(rel >= 0, jnp.exp(relc * lg), 0.0)
    xi = jnp.exp(ex_xi * lg)
    zeta = jnp.exp(ex_zeta * lg)
    gch = jnp.exp(jnp.full((1, 1), c, F32) * lg)
    return relc, dm, xi, zeta, ex_xi, ex_zeta, gch


def _ret_fwd(cfg, proj, lgs, name):
    s = cfg.s
    c = RET_CHUNK
    n = s // c
    kscale = RET_QK ** -0.5

    def body(lg_ref, q_ref, k_ref, v_ref, o_ref, st_ref):
        h = pl.program_id(0)
        for dirn in range(2):
            lg = lg_ref[dirn, h]
            _, dm, xi, zeta, _, _, gch = _decay_tables(lg, dirn == 1)
            st_ref[...] = jnp.zeros_like(st_ref)

            def step(t, carry):
                i = (n - 1 - t) if dirn == 1 else t
                rows = pl.ds(pl.multiple_of(i * c, c), c)
                qi = q_ref[rows, :]
                ks = k_ref[rows, :].astype(F32) * kscale
                vi = v_ref[rows, :]
                inner = lax.dot_general(qi, ks.astype(BF16), _DN["nt"], preferred_element_type=F32) * dm
                st = st_ref[...]
                o = (jnp.dot(inner.astype(BF16), vi, preferred_element_type=F32)
                     + jnp.dot(qi, st.astype(BF16), preferred_element_type=F32) * xi)
                if dirn == 0:
                    o_ref[rows, :] = o
                else:
                    o_ref[rows, :] += o
                st_ref[...] = st * gch + lax.dot_general((ks * zeta).astype(BF16), vi, _DN["tn"],
                                                         preferred_element_type=F32)
                return carry

            lax.fori_loop(0, n, step, 0)

    return pl.pallas_call(
        body, name=name, grid=(cfg.hr,),
        in_specs=[pl.BlockSpec(memory_space=pltpu.SMEM),
                  pl.BlockSpec((s, RET_QK), lambda h: (0, cfg.qr // RET_QK + h)),
                  pl.BlockSpec((s, RET_QK), lambda h: (0, cfg.kr // RET_QK + h)),
                  pl.BlockSpec((s, RET_V), lambda h: (0, cfg.vr // RET_V + h))],
        out_specs=pl.BlockSpec((s, RET_V), lambda h: (0, h)),
        out_shape=jax.ShapeDtypeStruct((s, cfg.rw), F32),
        scratch_shapes=[pltpu.VMEM((RET_QK, RET_V), F32)],
        compiler_params=_params(("parallel",)),
    )(lgs, proj, proj, proj)


def _ret_bwd(cfg, proj, lgs, do, name):
    s = cfg.s
    c = RET_CHUNK
    n = s // c
    kscale = RET_QK ** -0.5

    def body(lg_ref, q_ref, k_ref, v_ref, do_ref, dq_ref, dk_ref, dv_ref, dlg_ref, states, t_ref):
        h = pl.program_id(0)
        dlg_ref[...] = jnp.zeros_like(dlg_ref)
        for dirn in range(2):
            lg = lg_ref[dirn, h]
            relc, dm, xi, zeta, ex_xi, ex_zeta, gch = _decay_tables(lg, dirn == 1)

            def chunk_rows(t):
                i = (n - 1 - t) if dirn == 1 else t
                return pl.ds(pl.multiple_of(i * c, c), c)

            t_ref[...] = jnp.zeros_like(t_ref)

            def fwd_step(t, carry):
                rows = chunk_rows(t)
                st = t_ref[...]
                states[t] = st
                ks = k_ref[rows, :].astype(F32) * kscale
                t_ref[...] = st * gch + lax.dot_general((ks * zeta).astype(BF16), v_ref[rows, :], _DN["tn"],
                                                        preferred_element_type=F32)
                return carry

            lax.fori_loop(0, n, fwd_step, 0)
            t_ref[...] = jnp.zeros_like(t_ref)

            def bwd_step(u, acc):
                t = n - 1 - u
                rows = chunk_rows(t)
                qi = q_ref[rows, :]
                ks = k_ref[rows, :].astype(F32) * kscale
                ksb = ks.astype(BF16)
                vi = v_ref[rows, :]
                doi = do_ref[rows, :]
                sn = states[t].astype(BF16)
                tt = t_ref[...]
                ttb = tt.astype(BF16)
                a_mat = lax.dot_general(qi, ksb, _DN["nt"], preferred_element_type=F32) * dm
                dov = lax.dot_general(doi, vi, _DN["nt"], preferred_element_type=F32)
                b_mat = (dov * dm).astype(BF16)
                kz = (ks * zeta).astype(BF16)
                d_v = (jnp.dot(kz, ttb, preferred_element_type=F32)
                       + lax.dot_general(a_mat.astype(BF16), doi, _DN["tn"], preferred_element_type=F32))
                dk_inter = lax.dot_general(vi, ttb, _DN["nt"], preferred_element_type=F32) * zeta
                d_k = lax.dot_general(b_mat, qi, _DN["tn"], preferred_element_type=F32) + dk_inter
                o_inter = jnp.dot(qi, sn, preferred_element_type=F32) * xi
                d_q = (jnp.dot(b_mat, ksb, preferred_element_type=F32)
                       + lax.dot_general(doi, sn, _DN["nt"], preferred_element_type=F32) * xi)
                part = (jnp.sum(relc * a_mat * dov, keepdims=True)
                        + jnp.sum(ex_xi * jnp.sum(doi.astype(F32) * o_inter, axis=-1, keepdims=True), keepdims=True)
                        + jnp.sum(ex_zeta * jnp.sum(ks * dk_inter, axis=-1, keepdims=True), keepdims=True)
                        + (c * gch) * jnp.sum(tt * states[t], keepdims=True))
                t_ref[...] = tt * gch + lax.dot_general((qi.astype(F32) * xi).astype(BF16), doi, _DN["tn"],
                                                        preferred_element_type=F32)
                if dirn == 0:
                    dq_ref[rows, :] = d_q
                    dk_ref[rows, :] = d_k * kscale
                    dv_ref[rows, :] = d_v
                else:
                    dq_ref[rows, :] += d_q
                    dk_ref[rows, :] += d_k * kscale
                    dv_ref[rows, :] += d_v
                return acc + part

            total = lax.fori_loop(0, n, bwd_step, jnp.zeros((1, 1), F32))
            dlg_ref[0, dirn:dirn + 1, :] = jnp.broadcast_to(total, (1, LANES))

    return pl.pallas_call(
        body, name=name, grid=(cfg.hr,),
        in_specs=[pl.BlockSpec(memory_space=pltpu.SMEM),
                  pl.BlockSpec((s, RET_QK), lambda h: (0, cfg.qr // RET_QK + h)),
                  pl.BlockSpec((s, RET_QK), lambda h: (0, cfg.kr // RET_QK + h)),
                  pl.BlockSpec((s, RET_V), lambda h: (0, cfg.vr // RET_V + h)),
                  pl.BlockSpec((s, RET_V), lambda h: (0, h))],
        out_specs=[pl.BlockSpec((s, RET_QK), lambda h: (0, h)), pl.BlockSpec((s, RET_QK), lambda h: (0, h)),
                   pl.BlockSpec((s, RET_V), lambda h: (0, h)), pl.BlockSpec((1, 8, LANES), lambda h: (h, 0, 0))],
        out_shape=[jax.ShapeDtypeStruct((s, cfg.rqk), F32), jax.ShapeDtypeStruct((s, cfg.rqk), F32),
                   jax.ShapeDtypeStruct((s, cfg.rw), F32), jax.ShapeDtypeStruct((cfg.hr, 8, LANES), F32)],
        scratch_shapes=[pltpu.VMEM((n, RET_QK, RET_V), F32), pltpu.VMEM((RET_QK, RET_V), F32)],
        compiler_params=_params(("parallel",)),
    )(lgs, proj, proj, proj, do)


def _ada_fwd(cact16, w_ada, name):
    depth, d, n = w_ada.shape
    tn = _tile(n, 768)

    def body(c_ref, w_ref, o_ref):
        o_ref[0] = jnp.dot(c_ref[...], w_ref[0].astype(BF16), preferred_element_type=F32)

    return pl.pallas_call(
        body, name=name, grid=(depth, n // tn),
        in_specs=[pl.BlockSpec((16, d), lambda l, j: (0, 0)), pl.BlockSpec((1, d, tn), lambda l, j: (l, 0, j))],
        out_specs=pl.BlockSpec((1, 16, tn), lambda l, j: (l, 0, j)),
        out_shape=jax.ShapeDtypeStruct((depth, 16, n), F32),
        compiler_params=_params(("parallel", "parallel")),
    )(cact16, w_ada)


def _adam_math(w, g, m, v):
    m2 = ADAM_B1 * m + (1.0 - ADAM_B1) * g
    v2 = ADAM_B2 * v + (1.0 - ADAM_B2) * (g * g)
    m_hat = m2 / (1.0 - ADAM_B1 ** ADAM_STEP)
    v_hat = v2 / (1.0 - ADAM_B2 ** ADAM_STEP)
    delta = -ADAM_LR * (m_hat / (jnp.sqrt(v_hat) + ADAM_EPS) + ADAM_WD * w)
    return delta, m2, v2


def _adamw_big(w, g, m, v, name):
    depth, r, c = w.shape
    tr, tc = _tile(r, 512), _tile(c, 1024)

    def body(w_ref, g_ref, m_ref, v_ref, go_ref, d_ref, mo_ref, vo_ref):
        gv = g_ref[...]
        delta, m2, v2 = _adam_math(w_ref[...], gv, m_ref[...], v_ref[...])
        go_ref[...] = gv
        d_ref[...] = delta
        mo_ref[...] = m2
        vo_ref[...] = v2

    spec = pl.BlockSpec((1, tr, tc), lambda l, i, j: (l, i, j))
    shp = jax.ShapeDtypeStruct(w.shape, F32)
    return pl.pallas_call(
        body, name=name, grid=(depth, r // tr, c // tc),
        in_specs=[spec] * 4, out_specs=[spec] * 4, out_shape=[shp] * 4,
        compiler_params=_params(("parallel", "parallel", "parallel")),
    )(w, g, m, v)


def _adamw_ada(w, m, v, cact128, dmod128, name):
    depth, r, c = w.shape
    tr, tc = _tile(r, 512), _tile(c, 768)

    def body(w_ref, m_ref, v_ref, c_ref, dm_ref, go_ref, d_ref, mo_ref, vo_ref):
        gv = lax.dot_general(c_ref[...], dm_ref[0], _DN["tn"], preferred_element_type=F32)
        delta, m2, v2 = _adam_math(w_ref[0], gv, m_ref[0], v_ref[0])
        go_ref[0] = gv
        d_ref[0] = delta
        mo_ref[0] = m2
        vo_ref[0] = v2

    spec = pl.BlockSpec((1, tr, tc), lambda l, i, j: (l, i, j))
    shp = jax.ShapeDtypeStruct(w.shape, F32)
    return pl.pallas_call(
        body, name=name, grid=(depth, r // tr, c // tc),
        in_specs=[spec] * 3 + [pl.BlockSpec((128, tr), lambda l, i, j: (0, i)),
                               pl.BlockSpec((1, 128, tc), lambda l, i, j: (l, 0, j))],
        out_specs=[spec] * 4, out_shape=[shp] * 4,
        compiler_params=_params(("parallel", "parallel", "parallel")),
    )(w, m, v, cact128, dmod128)


def _adamw_small(w, g, m, v, name):
    def body(w_ref, g_ref, m_ref, v_ref, d_ref, mo_ref, vo_ref):
        delta, m2, v2 = _adam_math(w_ref[...], g_ref[...], m_ref[...], v_ref[...])
        d_ref[...] = delta
        mo_ref[...] = m2
        vo_ref[...] = v2

    shp = jax.ShapeDtypeStruct(w.shape, F32)
    return pl.pallas_call(body, name=name, out_shape=[shp] * 3)(w, g, m, v)


def _sum_gathered(parts, name):
    nd, r, c = parts.shape

    def body(p_ref, o_ref):
        acc = p_ref[0]
        for e in range(1, nd):
            acc = acc + p_ref[e]
        o_ref[...] = acc

    return pl.pallas_call(body, name=name, out_shape=jax.ShapeDtypeStruct((r, c), F32))(parts)


def _flip(v, bit):
    return 1 - v if bit else v


def _all_gather_small(x, name):
    r, c = x.shape

    def body(x_ref, out_ref, send_sems, recv_sems, local_sem):
        mx, my, mc = lax.axis_index("x"), lax.axis_index("y"), lax.axis_index("c")
        me = 4 * mx + 2 * my + mc
        mine = pltpu.make_async_copy(x_ref, out_ref.at[me], local_sem)
        mine.start()
        sends = []
        for k in range(1, N_DEV):
            peer = (_flip(mx, k & 4), _flip(my, k & 2), _flip(mc, k & 1))
            cp = pltpu.make_async_remote_copy(src_ref=x_ref, dst_ref=out_ref.at[me], send_sem=send_sems.at[k - 1],
                                              recv_sem=recv_sems.at[k - 1], device_id=peer, device_id_type=MESH)
            cp.start()
            sends.append(cp)
        for k in range(1, N_DEV):
            peer = (_flip(mx, k & 4), _flip(my, k & 2), _flip(mc, k & 1))
            src = 4 * peer[0] + 2 * peer[1] + peer[2]
            pltpu.make_async_remote_copy(src_ref=x_ref, dst_ref=out_ref.at[src], send_sem=send_sems.at[k - 1],
                                         recv_sem=recv_sems.at[k - 1], device_id=peer,
                                         device_id_type=MESH).wait_recv()
        for cp in sends:
            cp.wait_send()
        mine.wait()

    return pl.pallas_call(
        body, name=name,
        out_shape=jax.ShapeDtypeStruct((N_DEV, r, c), x.dtype),
        in_specs=[pl.BlockSpec(memory_space=pltpu.VMEM)],
        out_specs=pl.BlockSpec(memory_space=pltpu.VMEM),
        scratch_shapes=[pltpu.SemaphoreType.DMA((N_DEV - 1,)), pltpu.SemaphoreType.DMA((N_DEV - 1,)),
                        pltpu.SemaphoreType.DMA],
        compiler_params=pltpu.CompilerParams(vmem_limit_bytes=VMEM_LIMIT),
    )(x)


def _gather_weights(w_in_b, w_out_b, name):
    depth, d, fc = w_in_b.shape
    rc = w_out_b.shape[1]

    def body(wi_ref, wo_ref, fi_ref, fo_ref, send_sems, recv_sems, local_sems):
        mx, my, mc = lax.axis_index("x"), lax.axis_index("y"), lax.axis_index("c")
        chip = 2 * mx + my

        def dst_in(k):
            return fi_ref.at[:, :, pl.ds(pl.multiple_of(k * fc, LANES), fc)]

        def dst_out(k):
            return fo_ref.at[:, pl.ds(pl.multiple_of(k * rc, 8), rc), :]

        own = [pltpu.make_async_copy(wi_ref, dst_in(chip), local_sems.at[0]),
               pltpu.make_async_copy(wo_ref, dst_out(chip), local_sems.at[1])]
        for cp in own:
            cp.start()
        sends = []
        for j in range(1, N_CHIP):
            peer = (_flip(mx, j & 2), _flip(my, j & 1), mc)
            for t, (src, dst) in enumerate(((wi_ref, dst_in(chip)), (wo_ref, dst_out(chip)))):
                cp = pltpu.make_async_remote_copy(src_ref=src, dst_ref=dst, send_sem=send_sems.at[2 * (j - 1) + t],
                                                  recv_sem=recv_sems.at[2 * (j - 1) + t], device_id=peer,
                                                  device_id_type=MESH)
                cp.start()
                sends.append(cp)
        for j in range(1, N_CHIP):
            peer = (_flip(mx, j & 2), _flip(my, j & 1), mc)
            pchip = 2 * peer[0] + peer[1]
            for t, (src, dst) in enumerate(((wi_ref, dst_in(pchip)), (wo_ref, dst_out(pchip)))):
                pltpu.make_async_remote_copy(src_ref=src, dst_ref=dst, send_sem=send_sems.at[2 * (j - 1) + t],
                                             recv_sem=recv_sems.at[2 * (j - 1) + t], device_id=peer,
                                             device_id_type=MESH).wait_recv()
        for cp in sends:
            cp.wait_send()
        for cp in own:
            cp.wait()

    any_spec = pl.BlockSpec(memory_space=pl.ANY)
    ns = 2 * (N_CHIP - 1)
    return pl.pallas_call(
        body, name=name,
        out_shape=[jax.ShapeDtypeStruct((depth, d, N_CHIP * fc), BF16),
                   jax.ShapeDtypeStruct((depth, N_CHIP * rc, w_out_b.shape[2]), BF16)],
        in_specs=[any_spec, any_spec], out_specs=[any_spec, any_spec],
        scratch_shapes=[pltpu.SemaphoreType.DMA((ns,)), pltpu.SemaphoreType.DMA((ns,)),
                        pltpu.SemaphoreType.DMA((2,))],
    )(w_in_b, w_out_b)


def _scatter_grads(g_in, g_out, name):
    depth = len(g_in)
    d, f = g_in[0].shape
    fc, hr = f // N_CHIP, d // 2
    ro = d // N_DEV
    dd = g_out[0].shape[1]

    def body(*refs):
        gi = refs[:depth]
        go = refs[depth:2 * depth]
        ri_ref, ro_ref, send_sems, recv_sems, local_sems = refs[2 * depth:]
        mx, my, mc = lax.axis_index("x"), lax.axis_index("y"), lax.axis_index("c")
        me = 4 * mx + 2 * my + mc

        def src_in(l, owner):
            ochip = 2 * owner[0] + owner[1]
            return gi[l].at[pl.ds(pl.multiple_of(owner[2] * hr, 8), hr), pl.ds(pl.multiple_of(ochip * fc, LANES), fc)]

        def src_out(l, owner):
            oid = 4 * owner[0] + 2 * owner[1] + owner[2]
            return go[l].at[pl.ds(pl.multiple_of(oid * ro, 8), ro), :]

        mine = (mx, my, mc)
        own = []
        for l in range(depth):
            own.append(pltpu.make_async_copy(src_in(l, mine), ri_ref.at[me, l], local_sems.at[2 * l]))
            own.append(pltpu.make_async_copy(src_out(l, mine), ro_ref.at[me, l], local_sems.at[2 * l + 1]))
        for cp in own:
            cp.start()
        sends = []
        for k in range(1, N_DEV):
            peer = (_flip(mx, k & 4), _flip(my, k & 2), _flip(mc, k & 1))
            for l in range(depth):
                for t, (src, dst) in enumerate(((src_in(l, peer), ri_ref.at[me, l]), (src_out(l, peer), ro_ref.at[me, l]))):
                    idx = ((k - 1) * depth + l) * 2 + t
                    cp = pltpu.make_async_remote_copy(src_ref=src, dst_ref=dst, send_sem=send_sems.at[idx],
                                                      recv_sem=recv_sems.at[idx], device_id=peer, device_id_type=MESH)
                    cp.start()
                    sends.append(cp)
        for k in range(1, N_DEV):
            peer = (_flip(mx, k & 4), _flip(my, k & 2), _flip(mc, k & 1))
            pid = 4 * peer[0] + 2 * peer[1] + peer[2]
            for l in range(depth):
                for t, (src, dst) in enumerate(((src_in(l, mine), ri_ref.at[pid, l]), (src_out(l, mine), ro_ref.at[pid, l]))):
                    idx = ((k - 1) * depth + l) * 2 + t
                    pltpu.make_async_remote_copy(src_ref=src, dst_ref=dst, send_sem=send_sems.at[idx],
                                                 recv_sem=recv_sems.at[idx], device_id=peer,
                                                 device_id_type=MESH).wait_recv()
        for cp in sends:
            cp.wait_send()
        for cp in own:
            cp.wait()

    any_spec = pl.BlockSpec(memory_space=pl.ANY)
    ns = (N_DEV - 1) * depth * 2
    return pl.pallas_call(
        body, name=name,
        out_shape=[jax.ShapeDtypeStruct((N_DEV, depth, hr, fc), BF16),
                   jax.ShapeDtypeStruct((N_DEV, depth, ro, dd), BF16)],
        in_specs=[any_spec] * (2 * depth), out_specs=[any_spec, any_spec],
        scratch_shapes=[pltpu.SemaphoreType.DMA((ns,)), pltpu.SemaphoreType.DMA((ns,)),
                        pltpu.SemaphoreType.DMA((2 * depth,))],
    )(*g_in, *g_out)


def _sum_partials(parts, name):
    nd, depth, r, c = parts.shape
    tr, tc = _tile(r, 256), _tile(c, 1024)

    def body(p_ref, o_ref):
        acc = p_ref[0, 0].astype(F32)
        for e in range(1, nd):
            acc = acc + p_ref[e, 0].astype(F32)
        o_ref[0] = acc

    return pl.pallas_call(
        body, name=name, grid=(depth, r // tr, c // tc),
        in_specs=[pl.BlockSpec((nd, 1, tr, tc), lambda l, i, j: (0, l, i, j))],
        out_specs=pl.BlockSpec((1, tr, tc), lambda l, i, j: (l, i, j)),
        out_shape=jax.ShapeDtypeStruct((depth, r, c), F32),
        compiler_params=_params(("parallel", "parallel", "parallel")),
    )(parts)


def _exchange_halves(h_in, h_out, name):
    depth, hr, fc = h_in.shape
    ro, dd = h_out.shape[1:]

    def body(hi_ref, ho_ref, fi_ref, fo_ref, send_sems, recv_sems, local_sems):
        mx, my, mc = lax.axis_index("x"), lax.axis_index("y"), lax.axis_index("c")
        sib = (mx, my, 1 - mc)

        def dst_in(half):
            return fi_ref.at[:, pl.ds(pl.multiple_of(half * hr, 8), hr), :]

        def dst_out(half):
            return fo_ref.at[:, pl.ds(pl.multiple_of(half * ro, 8), ro), :]

        own = [pltpu.make_async_copy(hi_ref, dst_in(mc), local_sems.at[0]),
               pltpu.make_async_copy(ho_ref, dst_out(mc), local_sems.at[1])]
        for cp in own:
            cp.start()
        sends = []
        for t, (src, dst) in enumerate(((hi_ref, dst_in(mc)), (ho_ref, dst_out(mc)))):
            cp = pltpu.make_async_remote_copy(src_ref=src, dst_ref=dst, send_sem=send_sems.at[t],
                                              recv_sem=recv_sems.at[t], device_id=sib, device_id_type=MESH)
            cp.start()
            sends.append(cp)
        for t, (src, dst) in enumerate(((hi_ref, dst_in(1 - mc)), (ho_ref, dst_out(1 - mc)))):
            pltpu.make_async_remote_copy(src_ref=src, dst_ref=dst, send_sem=send_sems.at[t],
                                         recv_sem=recv_sems.at[t], device_id=sib, device_id_type=MESH).wait_recv()
        for cp in sends:
            cp.wait_send()
        for cp in own:
            cp.wait()

    any_spec = pl.BlockSpec(memory_space=pl.ANY)
    return pl.pallas_call(
        body, name=name,
        out_shape=[jax.ShapeDtypeStruct((depth, 2 * hr, fc), F32), jax.ShapeDtypeStruct((depth, 2 * ro, dd), F32)],
        in_specs=[any_spec, any_spec], out_specs=[any_spec, any_spec],
        scratch_shapes=[pltpu.SemaphoreType.DMA((2,)), pltpu.SemaphoreType.DMA((2,)), pltpu.SemaphoreType.DMA((2,))],
    )(h_in, h_out)


def _rows8(v):
    return jnp.pad(v, ((0, 8 - v.shape[0]), (0, 0)))


def kernel(x, c, norm_gain, w_ada, b_ada, w_in, w_out, ret_decay_logit_f, ret_decay_logit_b, final_gain, loss_target, m_norm_gain, m_w_ada, m_b_ada, m_w_in, m_w_out, m_ret_decay_logit_f, m_ret_decay_logit_b, m_final_gain, v_norm_gain, v_w_ada, v_b_ada, v_w_in, v_w_out, v_ret_decay_logit_f, v_ret_decay_logit_b, v_final_gain):
    cfg = _Cfg()
    depth, d = norm_gain.shape
    mx, my, mc = lax.axis_index("x"), lax.axis_index("y"), lax.axis_index("c")
    me = 4 * mx + 2 * my + mc
    chip = 2 * mx + my
    x0 = x[0]
    tgt = loss_target[0]
    ada_cols = w_ada.shape[2]

    c_all = _all_gather_small(_rows8(c), "gather_c")[:, 0, :]
    cact = _silu(c_all)
    mod_part = _ada_fwd(jnp.pad(cact, ((0, 8), (0, 0))).astype(BF16), w_ada, "ada_fwd")
    mod_all = _all_gather_small(mod_part.reshape(depth * 16, ada_cols), "gather_mod")
    mod_all = mod_all.reshape(N_CHIP, 2, depth, 16, ada_cols)[:, 0]
    mod_mine = lax.dynamic_index_in_dim(mod_all, me, axis=2, keepdims=False)
    mod = jnp.transpose(mod_mine, (1, 0, 2)).reshape(depth, 3, d)
    bias = b_ada.reshape(depth, 3, d)

    w_in_full, w_out_full = _gather_weights(w_in.astype(BF16), w_out.astype(BF16), "gather_weights")

    slopes = jnp.exp2(-8.0 * (jnp.arange(cfg.ha, dtype=F32) + 1.0) / cfg.ha)
    lg_f = jax.nn.log_sigmoid(ret_decay_logit_f)
    lg_b = jax.nn.log_sigmoid(ret_decay_logit_b)

    saved = []
    h_x = x0
    for l in range(depth):
        g8 = _rows8(norm_gain[l:l + 1])
        mod3, b3 = _rows8(mod[l]), _rows8(bias[l])
        hb = _norm_mod_fwd(h_x, g8, mod3, b3, f"norm_mod_fwd_{l}")
        proj = _matmul(hb, w_in_full, "nn", BF16, f"in_proj_{l}", layer=l)
        os_, ls_ = [], []
        for dil in DILATIONS:
            o_p, l_p = _attn_fwd(cfg, proj, slopes, dil, f"attn_fwd_d{dil}_{l}")
            os_.append(o_p)
            ls_.append(l_p)
        lgs = jnp.stack([lg_f[l], lg_b[l]])
        oret = _ret_fwd(cfg, proj, lgs, f"ret_fwd_{l}")
        y, oa, lse = _gate_fwd(cfg, os_, ls_, proj, oret, f"gate_fwd_{l}")
        x_next, out = _out_proj_fwd(y, w_out_full, l, h_x, mod3, b3, f"out_proj_{l}")
        saved.append((h_x, hb, proj, oret, y, oa, lse, out, g8, mod3, b3, lgs))
        h_x = x_next

    dx, loss8, fin_acc = _final_loss(h_x, tgt, _rows8(final_gain[None]), "final_loss")

    g_in, g_out = [None] * depth, [None] * depth
    d_mod, d_gain, d_lg = [None] * depth, [None] * depth, [None] * depth
    for l in reversed(range(depth)):
        x_l, hb, proj, oret, y, oa, lse, out, g8, mod3, b3, lgs = saved[l]
        douts, gate_acc = _out_proj_bwd_prep(dx, out, mod3, b3, f"out_proj_bwd_prep_{l}")
        dy = _matmul(douts, w_out_full, "nt", BF16, f"out_proj_dy_{l}", layer=l)
        g_out[l] = _matmul(y, douts, "tn", BF16, f"out_proj_dw_{l}", tk=1024)
        do_a, dlt, dza, do_r, dzr = _gate_bwd(cfg, dy, oa, proj, oret, f"gate_bwd_{l}")
        dqs, dks, dvs = [], [], []
        for dil in DILATIONS:
            dq_p, dk_p, dv_p = _attn_bwd(cfg, proj, slopes, do_a, lse, dlt, dil, f"attn_bwd_d{dil}_{l}")
            dqs.append(dq_p)
            dks.append(dk_p)
            dvs.append(dv_p)
        rdq, rdk, rdv, dlg = _ret_bwd(cfg, proj, lgs, do_r, f"ret_bwd_{l}")
        dproj = _assemble_dproj(cfg, dqs, dks, dvs, dza, rdq, rdk, rdv, dzr, f"assemble_dproj_{l}")
        dh = _matmul(dproj, w_in_full, "nt", F32, f"in_proj_dh_{l}", tk=1792, layer=l)
        g_in[l] = _matmul(hb, dproj, "tn", BF16, f"in_proj_dw_{l}", tk=1024)
        dx, nm_acc = _norm_mod_bwd(dh, x_l, dx, g8, mod3, b3, f"norm_mod_bwd_{l}")
        d_mod[l] = jnp.concatenate([nm_acc[0], nm_acc[1], gate_acc[0]])
        d_gain[l] = nm_acc[2]
        d_lg[l] = dlg[:, 0:2, 0]

    parts_in, parts_out = _scatter_grads(g_in, g_out, "scatter_grads")
    half_in = _sum_partials(parts_in, "sum_partials_in")
    half_out = _sum_partials(parts_out, "sum_partials_out")
    gw_in_t, gw_out = _exchange_halves(half_in, half_out, "exchange_halves")
    grad_w_in, delta_w_in, new_m_w_in, new_v_w_in = _adamw_big(w_in, gw_in_t, m_w_in, v_w_in, "adamw_w_in")
    grad_w_out, delta_w_out, new_m_w_out, new_v_w_out = _adamw_big(w_out, gw_out, m_w_out, v_w_out, "adamw_w_out")

    dmod_mine = jnp.stack(d_mod)
    dmod_gathered = _all_gather_small(_rows8(dmod_mine), "gather_dmod")
    dmod_all = dmod_gathered[:, :depth, :]
    grad_b_ada = _sum_gathered(dmod_gathered, "sum_b_ada")[:depth]
    dmod_cols = lax.dynamic_slice_in_dim(dmod_all, chip * ada_cols, ada_cols, axis=2)
    dmod128 = jnp.pad(jnp.transpose(dmod_cols, (1, 0, 2)), ((0, 0), (0, 120), (0, 0))).astype(BF16)
    cact128 = jnp.pad(cact, ((0, 120), (0, 0))).astype(BF16)
    grad_w_ada, delta_w_ada, new_m_w_ada, new_v_w_ada = _adamw_ada(w_ada, m_w_ada, v_w_ada, cact128, dmod128,
                                                                  "adamw_w_ada")

    dlg_all = jnp.stack(d_lg)
    sig_f = jax.nn.sigmoid(-ret_decay_logit_f)
    sig_b = jax.nn.sigmoid(-ret_decay_logit_b)
    nlg = depth * cfg.hr
    pack = jnp.zeros((8, d), F32)
    for l in range(depth):
        pack = pack.at[l].set(d_gain[l])
    pack = pack.at[depth].set(fin_acc[0])
    pack = pack.at[depth + 1, 0].set(loss8[0, 0])
    pack = pack.at[depth + 1, LANES:LANES + nlg].set((dlg_all[:, :, 0] * sig_f).reshape(-1))
    pack = pack.at[depth + 1, 2 * LANES:2 * LANES + nlg].set((dlg_all[:, :, 1] * sig_b).reshape(-1))
    tot = _sum_gathered(_all_gather_small(pack, "gather_small"), "sum_small")
    grad_norm_gain = tot[:depth]
    grad_final_gain = tot[depth]
    loss = tot[depth + 1, 0]
    grad_lf = tot[depth + 1, LANES:LANES + nlg].reshape(depth, cfg.hr)
    grad_lb = tot[depth + 1, 2 * LANES:2 * LANES + nlg].reshape(depth, cfg.hr)

    d_ng, m_ng, v_ng = _adamw_small(norm_gain, grad_norm_gain, m_norm_gain, v_norm_gain, "adamw_norm_gain")
    d_ba, m_ba, v_ba = _adamw_small(b_ada, grad_b_ada, m_b_ada, v_b_ada, "adamw_b_ada")
    d_lf, m_lf, v_lf = _adamw_small(ret_decay_logit_f, grad_lf, m_ret_decay_logit_f, v_ret_decay_logit_f, "adamw_lf")
    d_lb, m_lb, v_lb = _adamw_small(ret_decay_logit_b, grad_lb, m_ret_decay_logit_b, v_ret_decay_logit_b, "adamw_lb")
    d_fg, m_fg, v_fg = _adamw_small(final_gain[None], grad_final_gain[None], m_final_gain[None], v_final_gain[None],
                                    "adamw_final_gain")

    return (loss, dx[None],
            grad_norm_gain, grad_w_ada, grad_b_ada, grad_w_in, grad_w_out, grad_lf, grad_lb, grad_final_gain,
            d_ng, delta_w_ada, d_ba, delta_w_in, delta_w_out, d_lf, d_lb, d_fg[0],
            m_ng, new_m_w_ada, m_ba, new_m_w_in, new_m_w_out, m_lf, m_lb, m_fg[0],
            v_ng, new_v_w_ada, v_ba, new_v_w_in, new_v_w_out, v_lf, v_lb, v_fg[0])
```

```python
import functools

import jax
import jax.numpy as jnp
from jax import lax
from jax.experimental import pallas as pl
from jax.experimental.pallas import tpu as pltpu

F32 = jnp.float32
BF16 = jnp.bfloat16

D_MODEL = 2048
SEQ = 4096
DEPTH = 4
HEAD_DIM = 128
DILATIONS = (1, 4, 16)
RADIUS = 64
N_HEADS_RET = 4
RET_QK = 128
RET_V = 256
RET_CHUNK = 128
NORM_EPS = 1e-6
MASK_VALUE = -1e30
N_DEV = 8
N_CHIP = 4
LANES = 128
VMEM_LIMIT = 56 * 1024 * 1024

ADAM_LR = 0.001
ADAM_B1 = 0.9
ADAM_B2 = 0.999
ADAM_EPS = 1e-08
ADAM_WD = 0.01
ADAM_STEP = 10

MESH = pl.DeviceIdType.MESH


class _Cfg:
    def __init__(self):
        self.d = D_MODEL
        self.s = SEQ
        self.aw = D_MODEL // 2
        self.ha = self.aw // HEAD_DIM
        self.rw = D_MODEL // 2
        self.hr = N_HEADS_RET
        self.rqk = self.hr * RET_QK
        self.f = 4 * self.aw + 2 * self.rqk + 2 * self.rw
        self.qa, self.ka, self.va, self.za = 0, self.aw, 2 * self.aw, 3 * self.aw
        self.qr = 4 * self.aw
        self.kr = self.qr + self.rqk
        self.vr = self.kr + self.rqk
        self.zr = self.vr + self.rw
        assert self.rw == self.hr * RET_V


def _tile(n, pref):
    t = min(n, pref)
    while n % t or t % LANES:
        t -= LANES
    return t


def _params(dims=None):
    return pltpu.CompilerParams(dimension_semantics=dims, vmem_limit_bytes=VMEM_LIMIT)


def _silu(z):
    return z * jax.nn.sigmoid(z)


def _dsilu(z):
    sg = jax.nn.sigmoid(z)
    return sg * (1.0 + z * (1.0 - sg))


_DN = {"nn": (((1,), (0,)), ((), ())), "nt": (((1,), (1,)), ((), ())), "tn": (((0,), (0,)), ((), ()))}


def _matmul(a, b, mode, out_dtype, name, tm=1024, tn=1024, tk=2048):
    if mode == "tn":
        kk, m = a.shape
    else:
        m, kk = a.shape
    n = b.shape[0] if mode == "nt" else b.shape[1]
    tm, tn, tk = _tile(m, tm), _tile(n, tn), _tile(kk, tk)
    nk = kk // tk
    a_spec = (pl.BlockSpec((tk, tm), lambda i, j, k: (k, i)) if mode == "tn"
              else pl.BlockSpec((tm, tk), lambda i, j, k: (i, k)))
    b_spec = (pl.BlockSpec((tn, tk), lambda i, j, k: (j, k)) if mode == "nt"
              else pl.BlockSpec((tk, tn), lambda i, j, k: (k, j)))
    dn = _DN[mode]

    def body(a_ref, b_ref, o_ref, acc_ref):
        k = pl.program_id(2)
        p = lax.dot_general(a_ref[...], b_ref[...], dn, preferred_element_type=F32)

        @pl.when(k == 0)
        def _():
            acc_ref[...] = p

        @pl.when(k > 0)
        def _():
            acc_ref[...] += p

        @pl.when(k == nk - 1)
        def _():
            o_ref[...] = acc_ref[...].astype(out_dtype)

    return pl.pallas_call(
        body, name=name, grid=(m // tm, n // tn, nk),
        in_specs=[a_spec, b_spec],
        out_specs=pl.BlockSpec((tm, tn), lambda i, j, k: (i, j)),
        out_shape=jax.ShapeDtypeStruct((m, n), out_dtype),
        scratch_shapes=[pltpu.VMEM((tm, tn), F32)],
        compiler_params=_params(("parallel", "parallel", "arbitrary")),
    )(a, b)


def _out_proj_fwd(y, w_out, x, mod3, b3, name):
    s, kk = y.shape
    d = w_out.shape[1]
    tm, tn = _tile(s, 512), _tile(d, 1024)

    def body(y_ref, w_ref, x_ref, m_ref, b_ref, xn_ref, o_ref):
        out = jnp.dot(y_ref[...], w_ref[...], preferred_element_type=F32)
        gate = m_ref[2:3, :] + b_ref[2:3, :]
        xn_ref[...] = x_ref[...] + gate * out
        o_ref[...] = out.astype(BF16)

    vec = pl.BlockSpec((8, tn), lambda i, j: (0, j))
    return pl.pallas_call(
        body, name=name, grid=(s // tm, d // tn),
        in_specs=[pl.BlockSpec((tm, kk), lambda i, j: (i, 0)), pl.BlockSpec((kk, tn), lambda i, j: (0, j)),
                  pl.BlockSpec((tm, tn), lambda i, j: (i, j)), vec, vec],
        out_specs=[pl.BlockSpec((tm, tn), lambda i, j: (i, j)), pl.BlockSpec((tm, tn), lambda i, j: (i, j))],
        out_shape=[jax.ShapeDtypeStruct((s, d), F32), jax.ShapeDtypeStruct((s, d), BF16)],
        compiler_params=_params(("parallel", "parallel")),
    )(y, w_out, x, mod3, b3)


def _norm_mod_fwd(x, g8, mod3, b3, name):
    s, d = x.shape
    tr = _tile(s, 512)

    def body(x_ref, g_ref, m_ref, b_ref, h_ref):
        xv = x_ref[...]
        r = lax.rsqrt(jnp.mean(xv * xv, axis=-1, keepdims=True) + NORM_EPS)
        shift = m_ref[0:1, :] + b_ref[0:1, :]
        scale = m_ref[1:2, :] + b_ref[1:2, :]
        h_ref[...] = ((xv * r * g_ref[0:1, :]) * (1.0 + scale) + shift).astype(BF16)

    vec = pl.BlockSpec((8, d), lambda i: (0, 0))
    return pl.pallas_call(
        body, name=name, grid=(s // tr,),
        in_specs=[pl.BlockSpec((tr, d), lambda i: (i, 0)), vec, vec, vec],
        out_specs=pl.BlockSpec((tr, d), lambda i: (i, 0)),
        out_shape=jax.ShapeDtypeStruct((s, d), BF16),
        compiler_params=_params(("parallel",)),
    )(x, g8, mod3, b3)


def _norm_mod_bwd(dh, x, dx_next, g8, mod3, b3, name):
    s, d = x.shape
    tr = _tile(s, 256)

    def body(dh_ref, x_ref, dn_ref, g_ref, m_ref, b_ref, dx_ref, acc_ref):
        @pl.when(pl.program_id(0) == 0)
        def _():
            acc_ref[...] = jnp.zeros_like(acc_ref)

        xv = x_ref[...]
        dh_v = dh_ref[...]
        g = g_ref[0:1, :]
        r = lax.rsqrt(jnp.mean(xv * xv, axis=-1, keepdims=True) + NORM_EPS)
        xn = xv * r
        scale1 = 1.0 + m_ref[1:2, :] + b_ref[1:2, :]
        dhs = dh_v * scale1
        dxn = dhs * g
        dx_ref[...] = dn_ref[...] + r * (dxn - xn * jnp.mean(dxn * xn, axis=-1, keepdims=True))
        acc_ref[0:1, :] += jnp.sum(dh_v, axis=0, keepdims=True)
        acc_ref[1:2, :] += jnp.sum(dh_v * (xn * g), axis=0, keepdims=True)
        acc_ref[2:3, :] += jnp.sum(dhs * xn, axis=0, keepdims=True)

    vec = pl.BlockSpec((8, d), lambda i: (0, 0))
    row = pl.BlockSpec((tr, d), lambda i: (i, 0))
    return pl.pallas_call(
        body, name=name, grid=(s // tr,),
        in_specs=[row, row, row, vec, vec, vec],
        out_specs=[row, vec],
        out_shape=[jax.ShapeDtypeStruct((s, d), F32), jax.ShapeDtypeStruct((8, d), F32)],
        compiler_params=_params(("arbitrary",)),
    )(dh, x, dx_next, g8, mod3, b3)


def _final_loss(x, tgt, g8, name):
    s, d = x.shape
    tr = _tile(s, 256)

    def body(x_ref, t_ref, g_ref, dx_ref, loss_ref, acc_ref):
        @pl.when(pl.program_id(0) == 0)
        def _():
            acc_ref[...] = jnp.zeros_like(acc_ref)
            loss_ref[...] = jnp.zeros_like(loss_ref)

        xv = x_ref[...]
        g = g_ref[0:1, :]
        r = lax.rsqrt(jnp.mean(xv * xv, axis=-1, keepdims=True) + NORM_EPS)
        xn = xv * r
        err = xn * g - t_ref[...]
        loss_ref[...] += 0.5 * jnp.sum(jnp.mean(err * err, axis=-1, keepdims=True), axis=0, keepdims=True)
        dy = err * (1.0 / d)
        acc_ref[0:1, :] += jnp.sum(dy * xn, axis=0, keepdims=True)
        dxn = dy * g
        dx_ref[...] = r * (dxn - xn * jnp.mean(dxn * xn, axis=-1, keepdims=True))

    vec = pl.BlockSpec((8, d), lambda i: (0, 0))
    row = pl.BlockSpec((tr, d), lambda i: (i, 0))
    return pl.pallas_call(
        body, name=name, grid=(s // tr,),
        in_specs=[row, row, vec],
        out_specs=[row, pl.BlockSpec((8, LANES), lambda i: (0, 0)), vec],
        out_shape=[jax.ShapeDtypeStruct((s, d), F32), jax.ShapeDtypeStruct((8, LANES), F32),
                   jax.ShapeDtypeStruct((8, d), F32)],
        compiler_params=_params(("arbitrary",)),
    )(x, tgt, g8)


def _gate_fwd(cfg, os_, ls_, proj, oret, name):
    s = cfg.s
    aw, rw = cfg.aw, cfg.rw
    tr = _tile(s, 256)

    def body(o1, o2, o3, l1, l2, l3, za_ref, or_ref, zr_ref, y_ref, oa_ref, lse_ref):
        la, lb, lc = l1[...], l2[...], l3[...]
        m = jnp.maximum(jnp.maximum(la, lb), lc)
        ea, eb, ec = jnp.exp(la - m), jnp.exp(lb - m), jnp.exp(lc - m)
        den = ea + eb + ec
        o = (ea * o1[...].astype(F32) + eb * o2[...].astype(F32) + ec * o3[...].astype(F32)) / den
        lse_ref[...] = m + jnp.log(den)
        oa_ref[...] = o.astype(BF16)
        y_ref[:, 0:aw] = (o * _silu(za_ref[...].astype(F32))).astype(BF16)
        for h in range(cfg.hr):
            cols = slice(h * RET_V, (h + 1) * RET_V)
            oh = or_ref[:, cols]
            rr = lax.rsqrt(jnp.mean(oh * oh, axis=-1, keepdims=True) + NORM_EPS)
            y_ref[:, aw + h * RET_V:aw + (h + 1) * RET_V] = (
                oh * rr * _silu(zr_ref[:, cols].astype(F32))).astype(BF16)

    ra = pl.BlockSpec((tr, aw), lambda i: (i, 0))
    return pl.pallas_call(
        body, name=name, grid=(s // tr,),
        in_specs=[ra, ra, ra, ra, ra, ra,
                  pl.BlockSpec((tr, aw), lambda i: (i, cfg.za // aw)),
                  pl.BlockSpec((tr, rw), lambda i: (i, 0)),
                  pl.BlockSpec((tr, rw), lambda i: (i, cfg.zr // rw))],
        out_specs=[pl.BlockSpec((tr, cfg.d), lambda i: (i, 0)), ra, ra],
        out_shape=[jax.ShapeDtypeStruct((s, cfg.d), BF16), jax.ShapeDtypeStruct((s, aw), BF16),
                   jax.ShapeDtypeStruct((s, aw), F32)],
        compiler_params=_params(("parallel",)),
    )(*os_, *ls_, proj, oret, proj)


def _gate_bwd(cfg, dy, oa, proj, oret, name):
    s = cfg.s
    aw, rw = cfg.aw, cfg.rw
    tr = _tile(s, 256)

    def body(dy_ref, oa_ref, za_ref, or_ref, zr_ref, do_ref, dl_ref, dza_ref, dor_ref, dzr_ref):
        dya = dy_ref[:, 0:aw].astype(F32)
        za = za_ref[...].astype(F32)
        o = oa_ref[...].astype(F32)
        do = dya * _silu(za)
        do_ref[...] = do.astype(BF16)
        dza_ref[...] = (dya * o * _dsilu(za)).astype(BF16)
        prod = do * o
        for h in range(cfg.ha):
            cols = slice(h * HEAD_DIM, (h + 1) * HEAD_DIM)
            dl_ref[:, cols] = jnp.broadcast_to(jnp.sum(prod[:, cols], axis=-1, keepdims=True), (tr, HEAD_DIM))
        for h in range(cfg.hr):
            cols = slice(h * RET_V, (h + 1) * RET_V)
            oh = or_ref[:, cols]
            zr = zr_ref[:, cols].astype(F32)
            dyr = dy_ref[:, aw + h * RET_V:aw + (h + 1) * RET_V].astype(F32)
            rr = lax.rsqrt(jnp.mean(oh * oh, axis=-1, keepdims=True) + NORM_EPS)
            yn = oh * rr
            dyn = dyr * _silu(zr)
            dzr_ref[:, cols] = (dyr * yn * _dsilu(zr)).astype(BF16)
            dor_ref[:, cols] = (rr * (dyn - yn * jnp.mean(dyn * yn, axis=-1, keepdims=True))).astype(BF16)

    ra = pl.BlockSpec((tr, aw), lambda i: (i, 0))
    rr_ = pl.BlockSpec((tr, rw), lambda i: (i, 0))
    return pl.pallas_call(
        body, name=name, grid=(s // tr,),
        in_specs=[pl.BlockSpec((tr, cfg.d), lambda i: (i, 0)), ra,
                  pl.BlockSpec((tr, aw), lambda i: (i, cfg.za // aw)), rr_,
                  pl.BlockSpec((tr, rw), lambda i: (i, cfg.zr // rw))],
        out_specs=[ra, ra, ra, rr_, rr_],
        out_shape=[jax.ShapeDtypeStruct((s, aw), BF16), jax.ShapeDtypeStruct((s, aw), F32),
                   jax.ShapeDtypeStruct((s, aw), BF16), jax.ShapeDtypeStruct((s, rw), BF16),
                   jax.ShapeDtypeStruct((s, rw), BF16)],
        compiler_params=_params(("parallel",)),
    )(dy, oa, proj, oret, proj)


def _out_proj_bwd_prep(dxn, out, mod3, b3, name):
    s, d = dxn.shape
    tr = _tile(s, 512)

    def body(dx_ref, o_ref, m_ref, b_ref, do_ref, acc_ref):
        @pl.when(pl.program_id(0) == 0)
        def _():
            acc_ref[...] = jnp.zeros_like(acc_ref)

        dxv = dx_ref[...]
        gate = m_ref[2:3, :] + b_ref[2:3, :]
        do_ref[...] = (gate * dxv).astype(BF16)
        acc_ref[0:1, :] += jnp.sum(dxv * o_ref[...].astype(F32), axis=0, keepdims=True)

    vec = pl.BlockSpec((8, d), lambda i: (0, 0))
    row = pl.BlockSpec((tr, d), lambda i: (i, 0))
    return pl.pallas_call(
        body, name=name, grid=(s // tr,),
        in_specs=[row, row, vec, vec],
        out_specs=[row, vec],
        out_shape=[jax.ShapeDtypeStruct((s, d), BF16), jax.ShapeDtypeStruct((8, d), F32)],
        compiler_params=_params(("arbitrary",)),
    )(dxn, out, mod3, b3)


def _assemble_dproj(cfg, dqs, dks, dvs, dza, rdq, rdk, rdv, dzr, name):
    s = cfg.s
    aw, rw, rqk = cfg.aw, cfg.rw, cfg.rqk
    tr = _tile(s, 256)

    def body(q1, q2, q3, k1, k2, k3, v1, v2, v3, dza_ref, rq, rk, rv, dzr_ref, o_ref):
        def add3(a, b, c):
            return (a[...].astype(F32) + b[...].astype(F32) + c[...].astype(F32)).astype(BF16)

        o_ref[:, cfg.qa:cfg.qa + aw] = add3(q1, q2, q3)
        o_ref[:, cfg.ka:cfg.ka + aw] = add3(k1, k2, k3)
        o_ref[:, cfg.va:cfg.va + aw] = add3(v1, v2, v3)
        o_ref[:, cfg.za:cfg.za + aw] = dza_ref[...]
        o_ref[:, cfg.qr:cfg.qr + rqk] = rq[...].astype(BF16)
        o_ref[:, cfg.kr:cfg.kr + rqk] = rk[...].astype(BF16)
        o_ref[:, cfg.vr:cfg.vr + rw] = rv[...].astype(BF16)
        o_ref[:, cfg.zr:cfg.zr + rw] = dzr_ref[...]

    ra = pl.BlockSpec((tr, aw), lambda i: (i, 0))
    rq_ = pl.BlockSpec((tr, rqk), lambda i: (i, 0))
    rv_ = pl.BlockSpec((tr, rw), lambda i: (i, 0))
    return pl.pallas_call(
        body, name=name, grid=(s // tr,),
        in_specs=[ra] * 10 + [rq_, rq_, rv_, rv_],
        out_specs=pl.BlockSpec((tr, cfg.f), lambda i: (i, 0)),
        out_shape=jax.ShapeDtypeStruct((s, cfg.f), BF16),
        compiler_params=_params(("parallel",)),
    )(*dqs, *dks, *dvs, dza, rdq, rdk, rdv, dzr)


def _attn_geometry(cfg, dil):
    sub = cfg.s // dil
    bq = min(128, sub)
    win = min(256, sub)
    return sub, bq, win


def _attn_scores(q_ref, k_ref, slope, dil, i, sub, bq, win):
    qs = pl.multiple_of(i * bq, bq)
    ws = pl.multiple_of(jnp.clip(i * bq - (win - bq) // 2, 0, sub - win), 64)
    q = (q_ref[pl.ds(qs, bq), :].astype(F32) * (HEAD_DIM ** -0.5)).astype(BF16)
    kw = k_ref[pl.ds(ws, win), :]
    sc = lax.dot_general(q, kw, _DN["nt"], preferred_element_type=F32)
    rel = (ws - qs) + lax.broadcasted_iota(jnp.int32, (bq, win), 1) - lax.broadcasted_iota(jnp.int32, (bq, win), 0)
    arel = jnp.abs(rel)
    sc = jnp.where(arel <= RADIUS, sc - (slope * dil) * arel.astype(F32), MASK_VALUE)
    return qs, ws, q, kw, sc


def _attn_fwd(cfg, proj, slopes, dil, name):
    sub, bq, win = _attn_geometry(cfg, dil)
    nb = cfg.f // HEAD_DIM
    ha = cfg.ha
    view = proj.reshape(sub, dil * cfg.f)

    def body(sl_ref, q_ref, k_ref, v_ref, o_ref, l_ref):
        slope = sl_ref[pl.program_id(0)]

        def blk(i, carry):
            qs, ws, _, _, sc = _attn_scores(q_ref, k_ref, slope, dil, i, sub, bq, win)
            m = jnp.max(sc, axis=-1, keepdims=True)
            p = jnp.exp(sc - m)
            den = jnp.sum(p, axis=-1, keepdims=True)
            o = jnp.dot(p.astype(BF16), v_ref[pl.ds(ws, win), :], preferred_element_type=F32) / den
            o_ref[pl.ds(qs, bq), :] = o.astype(BF16)
            l_ref[pl.ds(qs, bq), :] = jnp.broadcast_to(m + jnp.log(den), (bq, HEAD_DIM))
            return carry

        lax.fori_loop(0, sub // bq, blk, 0)

    def col(off):
        return pl.BlockSpec((sub, HEAD_DIM), lambda h, r: (0, r * nb + off // HEAD_DIM + h))

    out_spec = pl.BlockSpec((sub, HEAD_DIM), lambda h, r: (0, r * ha + h))
    o, lse = pl.pallas_call(
        body, name=name, grid=(ha, dil),
        in_specs=[pl.BlockSpec(memory_space=pltpu.SMEM), col(cfg.qa), col(cfg.ka), col(cfg.va)],
        out_specs=[out_spec, out_spec],
        out_shape=[jax.ShapeDtypeStruct((sub, dil * cfg.aw), BF16), jax.ShapeDtypeStruct((sub, dil * cfg.aw), F32)],
        compiler_params=_params(("parallel", "parallel")),
    )(slopes, view, view, view)
    return o.reshape(cfg.s, cfg.aw), lse.reshape(cfg.s, cfg.aw)


def _attn_bwd(cfg, proj, slopes, do, lse, dlt, dil, name):
    sub, bq, win = _attn_geometry(cfg, dil)
    nb = cfg.f // HEAD_DIM
    ha = cfg.ha
    view = proj.reshape(sub, dil * cfg.f)
    dov = do.reshape(sub, dil * cfg.aw)
    lsev = lse.reshape(sub, dil * cfg.aw)
    dltv = dlt.reshape(sub, dil * cfg.aw)
    rep = win // HEAD_DIM

    def body(sl_ref, q_ref, k_ref, v_ref, do_ref, l_ref, dl_ref, dq_ref, dk_ref, dv_ref, dk_acc, dv_acc):
        slope = sl_ref[pl.program_id(0)]
        dk_acc[...] = jnp.zeros_like(dk_acc)
        dv_acc[...] = jnp.zeros_like(dv_acc)

        def blk(i, carry):
            qs, ws, q, kw, sc = _attn_scores(q_ref, k_ref, slope, dil, i, sub, bq, win)
            dob = do_ref[pl.ds(qs, bq), :]
            p = jnp.exp(sc - jnp.tile(l_ref[pl.ds(qs, bq), :], (1, rep)))
            dp = lax.dot_general(dob, v_ref[pl.ds(ws, win), :], _DN["nt"], preferred_element_type=F32)
            ds = (p * (dp - jnp.tile(dl_ref[pl.ds(qs, bq), :], (1, rep)))).astype(BF16)
            dq = jnp.dot(ds, kw, preferred_element_type=F32) * (HEAD_DIM ** -0.5)
            dq_ref[pl.ds(qs, bq), :] = dq.astype(BF16)
            dk_acc[pl.ds(ws, win), :] += lax.dot_general(ds, q, _DN["tn"], preferred_element_type=F32)
            dv_acc[pl.ds(ws, win), :] += lax.dot_general(p.astype(BF16), dob, _DN["tn"], preferred_element_type=F32)
            return carry

        lax.fori_loop(0, sub // bq, blk, 0)
        dk_ref[...] = dk_acc[...].astype(BF16)
        dv_ref[...] = dv_acc[...].astype(BF16)

    def col(off):
        return pl.BlockSpec((sub, HEAD_DIM), lambda h, r: (0, r * nb + off // HEAD_DIM + h))

    hs = pl.BlockSpec((sub, HEAD_DIM), lambda h, r: (0, r * ha + h))
    shp = jax.ShapeDtypeStruct((sub, dil * cfg.aw), BF16)
    dq, dk, dv = pl.pallas_call(
        body, name=name, grid=(ha, dil),
        in_specs=[pl.BlockSpec(memory_space=pltpu.SMEM), col(cfg.qa), col(cfg.ka), col(cfg.va), hs, hs, hs],
        out_specs=[hs, hs, hs],
        out_shape=[shp, shp, shp],
        scratch_shapes=[pltpu.VMEM((sub, HEAD_DIM), F32), pltpu.VMEM((sub, HEAD_DIM), F32)],
        compiler_params=_params(("parallel", "parallel")),
    )(slopes, view, view, view, dov, lsev, dltv)
    return tuple(t.reshape(cfg.s, cfg.aw) for t in (dq, dk, dv))


def _decay_tables(lg, backward):
    c = RET_CHUNK
    a = lax.broadcasted_iota(jnp.int32, (c, c), 0)
    b = lax.broadcasted_iota(jnp.int32, (c, c), 1)
    idx = lax.broadcasted_iota(jnp.int32, (c, 1), 0).astype(F32)
    if backward:
        rel = (b - a).astype(F32)
        ex_xi = c - idx
        ex_zeta = idx
    else:
        rel = (a - b).astype(F32)
        ex_xi = idx + 1.0
        ex_zeta = c - 1.0 - idx
    relc = jnp.maximum(rel, 0.0)
    dm = jnp.where(rel >= 0, jnp.exp(relc * lg), 0.0)
    xi = jnp.exp(ex_xi * lg)
    zeta = jnp.exp(ex_zeta * lg)
    gch = jnp.exp(jnp.full((1, 1), c, F32) * lg)
    return relc, dm, xi, zeta, ex_xi, ex_zeta, gch


def _ret_fwd(cfg, proj, lgs, name):
    s = cfg.s
    c = RET_CHUNK
    n = s // c
    kscale = RET_QK ** -0.5

    def body(lg_ref, q_ref, k_ref, v_ref, o_ref, st_ref):
        h = pl.program_id(0)
        tabs = [_decay_tables(lg_ref[dirn, h], dirn == 1) for dirn in range(2)]
        st_ref[...] = jnp.zeros_like(st_ref)
        o_ref[...] = jnp.zeros_like(o_ref)

        def step(t, carry):
            for dirn in range(2):
                _, dm, xi, zeta, _, _, gch = tabs[dirn]
                i = (n - 1 - t) if dirn == 1 else t
                rows = pl.ds(pl.multiple_of(i * c, c), c)
                qi = q_ref[rows, :]
                ks = k_ref[rows, :].astype(F32) * kscale
                vi = v_ref[rows, :]
                inner = lax.dot_general(qi, ks.astype(BF16), _DN["nt"], preferred_element_type=F32) * dm
                st = st_ref[dirn]
                o_ref[rows, :] += (jnp.dot(inner.astype(BF16), vi, preferred_element_type=F32)
                                   + jnp.dot(qi, st.astype(BF16), preferred_element_type=F32) * xi)
                st_ref[dirn] = st * gch + lax.dot_general((ks * zeta).astype(BF16), vi, _DN["tn"],
                                                          preferred_element_type=F32)
            return carry

        lax.fori_loop(0, n, step, 0)

    return pl.pallas_call(
        body, name=name, grid=(cfg.hr,),
        in_specs=[pl.BlockSpec(memory_space=pltpu.SMEM),
                  pl.BlockSpec((s, RET_QK), lambda h: (0, cfg.qr // RET_QK + h)),
                  pl.BlockSpec((s, RET_QK), lambda h: (0, cfg.kr // RET_QK + h)),
                  pl.BlockSpec((s, RET_V), lambda h: (0, cfg.vr // RET_V + h))],
        out_specs=pl.BlockSpec((s, RET_V), lambda h: (0, h)),
        out_shape=jax.ShapeDtypeStruct((s, cfg.rw), F32),
        scratch_shapes=[pltpu.VMEM((2, RET_QK, RET_V), F32)],
        compiler_params=_params(("parallel",)),
    )(lgs, proj, proj, proj)


def _ret_bwd(cfg, proj, lgs, do, name):
    s = cfg.s
    c = RET_CHUNK
    n = s // c
    kscale = RET_QK ** -0.5

    def body(lg_ref, q_ref, k_ref, v_ref, do_ref, dq_ref, dk_ref, dv_ref, dlg_ref, states, t_ref):
        h = pl.program_id(0)
        tabs = [_decay_tables(lg_ref[dirn, h], dirn == 1) for dirn in range(2)]
        dlg_ref[...] = jnp.zeros_like(dlg_ref)
        dq_ref[...] = jnp.zeros_like(dq_ref)
        dk_ref[...] = jnp.zeros_like(dk_ref)
        dv_ref[...] = jnp.zeros_like(dv_ref)

        def chunk_rows(dirn, t):
            i = (n - 1 - t) if dirn == 1 else t
            return pl.ds(pl.multiple_of(i * c, c), c)

        t_ref[...] = jnp.zeros_like(t_ref)

        def fwd_step(t, carry):
            for dirn in range(2):
                _, _, _, zeta, _, _, gch = tabs[dirn]
                rows = chunk_rows(dirn, t)
                st = t_ref[dirn]
                states[dirn, t] = st
                ks = k_ref[rows, :].astype(F32) * kscale
                t_ref[dirn] = st * gch + lax.dot_general((ks * zeta).astype(BF16), v_ref[rows, :], _DN["tn"],
                                                         preferred_element_type=F32)
            return carry

        lax.fori_loop(0, n, fwd_step, 0)
        t_ref[...] = jnp.zeros_like(t_ref)

        def bwd_step(u, accs):
            t = n - 1 - u
            new = []
            for dirn in range(2):
                relc, dm, xi, zeta, ex_xi, ex_zeta, gch = tabs[dirn]
                rows = chunk_rows(dirn, t)
                qi = q_ref[rows, :]
                ks = k_ref[rows, :].astype(F32) * kscale
                ksb = ks.astype(BF16)
                vi = v_ref[rows, :]
                doi = do_ref[rows, :]
                sn_f = states[dirn, t]
                sn = sn_f.astype(BF16)
                tt = t_ref[dirn]
                ttb = tt.astype(BF16)
                a_mat = lax.dot_general(qi, ksb, _DN["nt"], preferred_element_type=F32) * dm
                dov = lax.dot_general(doi, vi, _DN["nt"], preferred_element_type=F32)
                b_mat = (dov * dm).astype(BF16)
                kz = (ks * zeta).astype(BF16)
                d_v = (jnp.dot(kz, ttb, preferred_element_type=F32)
                       + lax.dot_general(a_mat.astype(BF16), doi, _DN["tn"], preferred_element_type=F32))
                dk_inter = lax.dot_general(vi, ttb, _DN["nt"], preferred_element_type=F32) * zeta
                d_k = lax.dot_general(b_mat, qi, _DN["tn"], preferred_element_type=F32) + dk_inter
                o_inter = jnp.dot(qi, sn, preferred_element_type=F32) * xi
                d_q = (jnp.dot(b_mat, ksb, preferred_element_type=F32)
                       + lax.dot_general(doi, sn, _DN["nt"], preferred_element_type=F32) * xi)
                part = (jnp.sum(relc * a_mat * dov, keepdims=True)
                        + jnp.sum(ex_xi * jnp.sum(doi.astype(F32) * o_inter, axis=-1, keepdims=True), keepdims=True)
                        + jnp.sum(ex_zeta * jnp.sum(ks * dk_inter, axis=-1, keepdims=True), keepdims=True)
                        + (c * gch) * jnp.sum(tt * sn_f, keepdims=True))
                t_ref[dirn] = tt * gch + lax.dot_general((qi.astype(F32) * xi).astype(BF16), doi, _DN["tn"],
                                                         preferred_element_type=F32)
                dq_ref[rows, :] += d_q
                dk_ref[rows, :] += d_k * kscale
                dv_ref[rows, :] += d_v
                new.append(accs[dirn] + part)
            return tuple(new)

        totals = lax.fori_loop(0, n, bwd_step, (jnp.zeros((1, 1), F32), jnp.zeros((1, 1), F32)))
        for dirn in range(2):
            dlg_ref[0, dirn:dirn + 1, :] = jnp.broadcast_to(totals[dirn], (1, LANES))

    return pl.pallas_call(
        body, name=name, grid=(cfg.hr,),
        in_specs=[pl.BlockSpec(memory_space=pltpu.SMEM),
                  pl.BlockSpec((s, RET_QK), lambda h: (0, cfg.qr // RET_QK + h)),
                  pl.BlockSpec((s, RET_QK), lambda h: (0, cfg.kr // RET_QK + h)),
                  pl.BlockSpec((s, RET_V), lambda h: (0, cfg.vr // RET_V + h)),
                  pl.BlockSpec((s, RET_V), lambda h: (0, h))],
        out_specs=[pl.BlockSpec((s, RET_QK), lambda h: (0, h)), pl.BlockSpec((s, RET_QK), lambda h: (0, h)),
                   pl.BlockSpec((s, RET_V), lambda h: (0, h)), pl.BlockSpec((1, 8, LANES), lambda h: (h, 0, 0))],
        out_shape=[jax.ShapeDtypeStruct((s, cfg.rqk), F32), jax.ShapeDtypeStruct((s, cfg.rqk), F32),
                   jax.ShapeDtypeStruct((s, cfg.rw), F32), jax.ShapeDtypeStruct((cfg.hr, 8, LANES), F32)],
        scratch_shapes=[pltpu.VMEM((2, n, RET_QK, RET_V), F32), pltpu.VMEM((2, RET_QK, RET_V), F32)],
        compiler_params=_params(("parallel",)),
    )(lgs, proj, proj, proj, do)


def _ada_fwd(cact16, w_ada, name):
    depth, d, n = w_ada.shape
    tn = _tile(n, 768)

    def body(c_ref, w_ref, o_ref):
        o_ref[0] = jnp.dot(c_ref[...], w_ref[0].astype(BF16), preferred_element_type=F32)

    return pl.pallas_call(
        body, name=name, grid=(depth, n // tn),
        in_specs=[pl.BlockSpec((16, d), lambda l, j: (0, 0)), pl.BlockSpec((1, d, tn), lambda l, j: (l, 0, j))],
        out_specs=pl.BlockSpec((1, 16, tn), lambda l, j: (l, 0, j)),
        out_shape=jax.ShapeDtypeStruct((depth, 16, n), F32),
        compiler_params=_params(("parallel", "parallel")),
    )(cact16, w_ada)


def _adam_math(w, g, m, v):
    m2 = ADAM_B1 * m + (1.0 - ADAM_B1) * g
    v2 = ADAM_B2 * v + (1.0 - ADAM_B2) * (g * g)
    m_hat = m2 / (1.0 - ADAM_B1 ** ADAM_STEP)
    v_hat = v2 / (1.0 - ADAM_B2 ** ADAM_STEP)
    delta = -ADAM_LR * (m_hat / (jnp.sqrt(v_hat) + ADAM_EPS) + ADAM_WD * w)
    return delta, m2, v2


def _adamw_big(w, g, m, v, name):
    depth, r, c = w.shape
    tr, tc = _tile(r, 512), _tile(c, 1024)

    def body(w_ref, g_ref, m_ref, v_ref, go_ref, d_ref, mo_ref, vo_ref):
        gv = g_ref[...]
        delta, m2, v2 = _adam_math(w_ref[...], gv, m_ref[...], v_ref[...])
        go_ref[...] = gv
        d_ref[...] = delta
        mo_ref[...] = m2
        vo_ref[...] = v2

    spec = pl.BlockSpec((1, tr, tc), lambda l, i, j: (l, i, j))
    shp = jax.ShapeDtypeStruct(w.shape, F32)
    return pl.pallas_call(
        body, name=name, grid=(depth, r // tr, c // tc),
        in_specs=[spec] * 4, out_specs=[spec] * 4, out_shape=[shp] * 4,
        compiler_params=_params(("parallel", "parallel", "parallel")),
    )(w, g, m, v)


def _adamw_ada(w, m, v, cact128, dmod128, name):
    depth, r, c = w.shape
    tr, tc = _tile(r, 512), _tile(c, 768)

    def body(w_ref, m_ref, v_ref, c_ref, dm_ref, go_ref, d_ref, mo_ref, vo_ref):
        gv = lax.dot_general(c_ref[...], dm_ref[0], _DN["tn"], preferred_element_type=F32)
        delta, m2, v2 = _adam_math(w_ref[0], gv, m_ref[0], v_ref[0])
        go_ref[0] = gv
        d_ref[0] = delta
        mo_ref[0] = m2
        vo_ref[0] = v2

    spec = pl.BlockSpec((1, tr, tc), lambda l, i, j: (l, i, j))
    shp = jax.ShapeDtypeStruct(w.shape, F32)
    return pl.pallas_call(
        body, name=name, grid=(depth, r // tr, c // tc),
        in_specs=[spec] * 3 + [pl.BlockSpec((128, tr), lambda l, i, j: (0, i)),
                               pl.BlockSpec((1, 128, tc), lambda l, i, j: (l, 0, j))],
        out_specs=[spec] * 4, out_shape=[shp] * 4,
        compiler_params=_params(("parallel", "parallel", "parallel")),
    )(w, m, v, cact128, dmod128)


def _adamw_small(w, g, m, v, name):
    def body(w_ref, g_ref, m_ref, v_ref, d_ref, mo_ref, vo_ref):
        delta, m2, v2 = _adam_math(w_ref[...], g_ref[...], m_ref[...], v_ref[...])
        d_ref[...] = delta
        mo_ref[...] = m2
        vo_ref[...] = v2

    shp = jax.ShapeDtypeStruct(w.shape, F32)
    return pl.pallas_call(body, name=name, out_shape=[shp] * 3)(w, g, m, v)


def _sum_gathered(parts, name):
    nd, r, c = parts.shape

    def body(p_ref, o_ref):
        acc = p_ref[0]
        for e in range(1, nd):
            acc = acc + p_ref[e]
        o_ref[...] = acc

    return pl.pallas_call(body, name=name, out_shape=jax.ShapeDtypeStruct((r, c), F32))(parts)


def _flip(v, bit):
    return 1 - v if bit else v


def _all_gather_small(x, name):
    r, c = x.shape

    def body(x_ref, out_ref, send_sems, recv_sems, local_sem):
        mx, my, mc = lax.axis_index("x"), lax.axis_index("y"), lax.axis_index("c")
        me = 4 * mx + 2 * my + mc
        mine = pltpu.make_async_copy(x_ref, out_ref.at[me], local_sem)
        mine.start()
        sends = []
        for k in range(1, N_DEV):
            peer = (_flip(mx, k & 4), _flip(my, k & 2), _flip(mc, k & 1))
            cp = pltpu.make_async_remote_copy(src_ref=x_ref, dst_ref=out_ref.at[me], send_sem=send_sems.at[k - 1],
                                              recv_sem=recv_sems.at[k - 1], device_id=peer, device_id_type=MESH)
            cp.start()
            sends.append(cp)
        for k in range(1, N_DEV):
            peer = (_flip(mx, k & 4), _flip(my, k & 2), _flip(mc, k & 1))
            src = 4 * peer[0] + 2 * peer[1] + peer[2]
            pltpu.make_async_remote_copy(src_ref=x_ref, dst_ref=out_ref.at[src], send_sem=send_sems.at[k - 1],
                                         recv_sem=recv_sems.at[k - 1], device_id=peer,
                                         device_id_type=MESH).wait_recv()
        for cp in sends:
            cp.wait_send()
        mine.wait()

    return pl.pallas_call(
        body, name=name,
        out_shape=jax.ShapeDtypeStruct((N_DEV, r, c), x.dtype),
        in_specs=[pl.BlockSpec(memory_space=pltpu.VMEM)],
        out_specs=pl.BlockSpec(memory_space=pltpu.VMEM),
        scratch_shapes=[pltpu.SemaphoreType.DMA((N_DEV - 1,)), pltpu.SemaphoreType.DMA((N_DEV - 1,)),
                        pltpu.SemaphoreType.DMA],
        compiler_params=pltpu.CompilerParams(vmem_limit_bytes=VMEM_LIMIT),
    )(x)


_HBM = pl.BlockSpec(memory_space=pltpu.HBM)
_SEM = pl.BlockSpec(memory_space=pltpu.SEMAPHORE)
_ANY = pl.BlockSpec(memory_space=pl.ANY)
_EFFECT = pltpu.SideEffectType.DATAFLOW_SIDE_EFFECTING


def _in_hbm(a):
    return pltpu.with_memory_space_constraint(a, pltpu.HBM)


def _place_w_in(w, layer, chip1, name):
    _, d, fc = w.shape
    tr = _tile(d, 512)

    def body(c_ref, w_ref, o_ref):
        o_ref[...] = w_ref[...].astype(BF16)

    return pl.pallas_call(
        body, name=name,
        grid_spec=pltpu.PrefetchScalarGridSpec(
            num_scalar_prefetch=1, grid=(d // tr,),
            in_specs=[pl.BlockSpec((None, tr, fc), lambda i, c: (layer, i, 0))],
            out_specs=pl.BlockSpec((tr, fc), lambda i, c: (i, c[0]))),
        out_shape=jax.ShapeDtypeStruct((d, N_CHIP * fc), BF16),
        compiler_params=_params(("parallel",)),
    )(chip1, w)


def _place_w_out(w, layer, chip1, name):
    _, rc, dd = w.shape
    tc = _tile(dd, 1024)

    def body(c_ref, w_ref, o_ref):
        o_ref[...] = w_ref[...].astype(BF16)

    return pl.pallas_call(
        body, name=name,
        grid_spec=pltpu.PrefetchScalarGridSpec(
            num_scalar_prefetch=1, grid=(dd // tc,),
            in_specs=[pl.BlockSpec((None, rc, tc), lambda j, c: (layer, 0, j))],
            out_specs=pl.BlockSpec((rc, tc), lambda j, c: (c[0], j))),
        out_shape=jax.ShapeDtypeStruct((N_CHIP * rc, dd), BF16),
        compiler_params=_params(("parallel",)),
    )(chip1, w)


def _weight_regions(fi_ref, fo_ref, chip):
    fc = fi_ref.shape[1] // N_CHIP
    rc = fo_ref.shape[0] // N_CHIP
    return (fi_ref.at[:, pl.ds(pl.multiple_of(chip * fc, LANES), fc)],
            fo_ref.at[pl.ds(pl.multiple_of(chip * rc, 8), rc), :])


def _gather_copies(fi_ref, fo_ref, send_sems, recv_sems):
    mx, my, mc = lax.axis_index("x"), lax.axis_index("y"), lax.axis_index("c")
    mine = _weight_regions(fi_ref, fo_ref, 2 * mx + my)
    sends, waits = [], []
    for j in range(1, N_CHIP):
        peer = (_flip(mx, j & 2), _flip(my, j & 1), mc)
        theirs = _weight_regions(fi_ref, fo_ref, 2 * peer[0] + peer[1])
        for t in range(2):
            idx = 2 * (j - 1) + t
            kw = dict(send_sem=send_sems.at[idx], recv_sem=recv_sems.at[idx], device_id=peer, device_id_type=MESH)
            sends.append(pltpu.make_async_remote_copy(src_ref=mine[t], dst_ref=mine[t], **kw))
            waits.append(pltpu.make_async_remote_copy(src_ref=mine[t], dst_ref=theirs[t], **kw))
    return sends, waits


def _gather_start(fi, fo, dep, name):
    ns = 2 * (N_CHIP - 1)

    def body(fi_ref, fo_ref, dep_ref, send_sems, recv_sems, fi_thru, fo_thru, token):
        sends, _ = _gather_copies(fi_ref, fo_ref, send_sems, recv_sems)
        for cp in sends:
            cp.start()
        token[...] = jnp.zeros_like(token)

    return pl.pallas_call(
        body, name=name,
        out_shape=(pltpu.SemaphoreType.DMA((ns,)), pltpu.SemaphoreType.DMA((ns,)),
                   pltpu.HBM(fi.shape, fi.dtype), pltpu.HBM(fo.shape, fo.dtype),
                   jax.ShapeDtypeStruct((8, LANES), F32)),
        in_specs=(_HBM, _HBM, _ANY),
        out_specs=(_SEM, _SEM, _HBM, _HBM, pl.BlockSpec(memory_space=pltpu.VMEM)),
        input_output_aliases={0: 2, 1: 3},
        compiler_params=pltpu.CompilerParams(has_side_effects=_EFFECT),
    )(_in_hbm(fi), _in_hbm(fo), dep)


def _gather_wait(send_sems, recv_sems, fi, fo, after, name):
    def body(fi_ref, fo_ref, send_sems, recv_sems, after_ref, fi_out, fo_out):
        _, waits = _gather_copies(fi_ref, fo_ref, send_sems, recv_sems)
        for cp in waits:
            cp.wait_send()
            cp.wait_recv()

    return pl.pallas_call(
        body, name=name,
        out_shape=(pltpu.HBM(fi.shape, fi.dtype), pltpu.HBM(fo.shape, fo.dtype)),
        in_specs=(_HBM, _HBM, _SEM, _SEM, _ANY), out_specs=(_HBM, _HBM),
        input_output_aliases={0: 0, 1: 1},
        compiler_params=pltpu.CompilerParams(has_side_effects=_EFFECT),
    )(fi, fo, send_sems, recv_sems, after)


def _scatter_copies(gi_ref, go_ref, pi_ref, po_ref, send_sems, recv_sems):
    mx, my, mc = lax.axis_index("x"), lax.axis_index("y"), lax.axis_index("c")
    hr, fc = pi_ref.shape[1:]
    ro = po_ref.shape[1]
    copies = []
    for k in range(1, N_DEV):
        peer = (_flip(mx, k & 4), _flip(my, k & 2), _flip(mc, k & 1))
        pchip = 2 * peer[0] + peer[1]
        src = (gi_ref.at[pl.ds(pl.multiple_of(peer[2] * hr, 8), hr), pl.ds(pl.multiple_of(pchip * fc, LANES), fc)],
               go_ref.at[pl.ds(pl.multiple_of((2 * pchip + peer[2]) * ro, 8), ro), :])
        dst = (pi_ref.at[k - 1], po_ref.at[k - 1])
        for t in range(2):
            idx = 2 * (k - 1) + t
            copies.append(pltpu.make_async_remote_copy(src_ref=src[t], dst_ref=dst[t], send_sem=send_sems.at[idx],
                                                       recv_sem=recv_sems.at[idx], device_id=peer,
                                                       device_id_type=MESH))
    return copies


def _scatter_start(gi, go, name):
    d, f = gi.shape
    dd = go.shape[1]
    ns = 2 * (N_DEV - 1)
    pi = lax.empty((N_DEV - 1, d // 2, f // N_CHIP), BF16)
    po = lax.empty((N_DEV - 1, d // N_DEV, dd), BF16)

    def body(gi_ref, go_ref, pi_ref, po_ref, send_sems, recv_sems, gi_thru, go_thru, pi_thru, po_thru, token):
        for cp in _scatter_copies(gi_ref, go_ref, pi_ref, po_ref, send_sems, recv_sems):
            cp.start()
        token[...] = jnp.zeros_like(token)

    return pl.pallas_call(
        body, name=name,
        out_shape=(pltpu.SemaphoreType.DMA((ns,)), pltpu.SemaphoreType.DMA((ns,)),
                   pltpu.HBM(gi.shape, gi.dtype), pltpu.HBM(go.shape, go.dtype),
                   pltpu.HBM(pi.shape, pi.dtype), pltpu.HBM(po.shape, po.dtype),
                   jax.ShapeDtypeStruct((8, LANES), F32)),
        in_specs=(_HBM, _HBM, _HBM, _HBM),
        out_specs=(_SEM, _SEM, _HBM, _HBM, _HBM, _HBM, pl.BlockSpec(memory_space=pltpu.VMEM)),
        input_output_aliases={0: 2, 1: 3, 2: 4, 3: 5},
        compiler_params=pltpu.CompilerParams(has_side_effects=_EFFECT),
    )(_in_hbm(gi), _in_hbm(go), _in_hbm(pi), _in_hbm(po))


def _scatter_wait(send_sems, recv_sems, gi, go, pi, po, after, name):
    def body(gi_ref, go_ref, pi_ref, po_ref, send_sems, recv_sems, after_ref, gi_out, go_out, pi_out, po_out):
        for cp in _scatter_copies(gi_ref, go_ref, pi_ref, po_ref, send_sems, recv_sems):
            cp.wait_send()
            cp.wait_recv()

    return pl.pallas_call(
        body, name=name,
        out_shape=tuple(pltpu.HBM(a.shape, a.dtype) for a in (gi, go, pi, po)),
        in_specs=(_HBM, _HBM, _HBM, _HBM, _SEM, _SEM, _ANY), out_specs=(_HBM, _HBM, _HBM, _HBM),
        input_output_aliases={0: 0, 1: 1, 2: 2, 3: 3},
        compiler_params=pltpu.CompilerParams(has_side_effects=_EFFECT),
    )(gi, go, pi, po, send_sems, recv_sems, after)


def _sum_into(buf, g, parts, where2, layer, row_blocks, name):
    depth, r2, c = buf.shape
    r = r2 // 2
    tr, tc = _tile(r, 256), _tile(c, 1024)
    nr, nc = r // tr, c // tc
    col_blocks = (g.shape[1] // c) > 1

    def body(w_ref, buf_ref, g_ref, p_ref, o_ref):
        acc = g_ref[...].astype(F32)
        for e in range(N_DEV - 1):
            acc = acc + p_ref[e].astype(F32)
        o_ref[...] = acc

    return pl.pallas_call(
        body, name=name,
        grid_spec=pltpu.PrefetchScalarGridSpec(
            num_scalar_prefetch=1, grid=(nr, nc),
            in_specs=[_ANY,
                      pl.BlockSpec((tr, tc), lambda i, j, w: (row_blocks(w, nr) + i, (w[1] * nc if col_blocks else 0) + j)),
                      pl.BlockSpec((N_DEV - 1, tr, tc), lambda i, j, w: (0, i, j))],
            out_specs=pl.BlockSpec((None, tr, tc), lambda i, j, w: (layer, w[0] * nr + i, j))),
        out_shape=jax.ShapeDtypeStruct(buf.shape, F32),
        input_output_aliases={1: 0},
        compiler_params=_params(("parallel", "parallel")),
    )(where2, buf, g, parts)


def _exchange_halves(b_in, b_out, name):
    depth, r_in = b_in.shape[:2]
    r_out = b_out.shape[1]

    def body(bi_ref, bo_ref, oi_ref, oo_ref, send_sems, recv_sems):
        mx, my, mc = lax.axis_index("x"), lax.axis_index("y"), lax.axis_index("c")
        sib = (mx, my, 1 - mc)

        def half(ref, l, rows, which):
            return ref.at[l, pl.ds(pl.multiple_of(which * (rows // 2), 8), rows // 2), :]

        copies = []
        for l in range(depth):
            for t, (src, dst, rows) in enumerate(((bi_ref, oi_ref, r_in), (bo_ref, oo_ref, r_out))):
                idx = 2 * l + t
                kw = dict(send_sem=send_sems.at[idx], recv_sem=recv_sems.at[idx], device_id=sib, device_id_type=MESH)
                cp = pltpu.make_async_remote_copy(src_ref=half(src, l, rows, mc), dst_ref=half(dst, l, rows, mc), **kw)
                cp.start()
                copies.append((cp, pltpu.make_async_remote_copy(src_ref=half(src, l, rows, mc),
                                                                dst_ref=half(dst, l, rows, 1 - mc), **kw)))
        for cp, landed in copies:
            landed.wait_recv()
        for cp, landed in copies:
            cp.wait_send()

    ns = 2 * depth
    return pl.pallas_call(
        body, name=name,
        out_shape=[jax.ShapeDtypeStruct(b_in.shape, F32), jax.ShapeDtypeStruct(b_out.shape, F32)],
        in_specs=[_ANY, _ANY], out_specs=[_ANY, _ANY],
        input_output_aliases={0: 0, 1: 1},
        scratch_shapes=[pltpu.SemaphoreType.DMA((ns,)), pltpu.SemaphoreType.DMA((ns,))],
    )(b_in, b_out)


def _rows8(v):
    return jnp.pad(v, ((0, 8 - v.shape[0]), (0, 0)))


def kernel(x, c, norm_gain, w_ada, b_ada, w_in, w_out, ret_decay_logit_f, ret_decay_logit_b, final_gain, loss_target, m_norm_gain, m_w_ada, m_b_ada, m_w_in, m_w_out, m_ret_decay_logit_f, m_ret_decay_logit_b, m_final_gain, v_norm_gain, v_w_ada, v_b_ada, v_w_in, v_w_out, v_ret_decay_logit_f, v_ret_decay_logit_b, v_final_gain):
    cfg = _Cfg()
    depth, d = norm_gain.shape
    mx, my, mc = lax.axis_index("x"), lax.axis_index("y"), lax.axis_index("c")
    me = 4 * mx + 2 * my + mc
    chip = 2 * mx + my
    x0 = x[0]
    tgt = loss_target[0]
    ada_cols = w_ada.shape[2]

    c_all = _all_gather_small(_rows8(c), "gather_c")[:, 0, :]
    cact = _silu(c_all)
    mod_part = _ada_fwd(jnp.pad(cact, ((0, 8), (0, 0))).astype(BF16), w_ada, "ada_fwd")
    mod_all = _all_gather_small(mod_part.reshape(depth * 16, ada_cols), "gather_mod")
    mod_all = mod_all.reshape(N_CHIP, 2, depth, 16, ada_cols)[:, 0]
    mod_mine = lax.dynamic_index_in_dim(mod_all, me, axis=2, keepdims=False)
    mod = jnp.transpose(mod_mine, (1, 0, 2)).reshape(depth, 3, d)
    bias = b_ada.reshape(depth, 3, d)

    chip1 = jnp.reshape(chip, (1,)).astype(jnp.int32)
    where2 = jnp.stack([mc, chip]).astype(jnp.int32)

    def start_gather(l, dep):
        return _gather_start(_place_w_in(w_in, l, chip1, f"place_w_in_{l}"),
                             _place_w_out(w_out, l, chip1, f"place_w_out_{l}"), dep, f"gather_start_{l}")

    slopes = jnp.exp2(-8.0 * (jnp.arange(cfg.ha, dtype=F32) + 1.0) / cfg.ha)
    lg_f = jax.nn.log_sigmoid(ret_decay_logit_f)
    lg_b = jax.nn.log_sigmoid(ret_decay_logit_b)

    saved = []
    w_full = []
    h_x = x0
    pending = start_gather(0, c)
    for l in range(depth):
        send_sems, recv_sems, fi, fo, _ = pending
        w_in_l, w_out_l = _gather_wait(send_sems, recv_sems, fi, fo, h_x, f"gather_wait_{l}")
        w_full.append((w_in_l, w_out_l))
        g8 = _rows8(norm_gain[l:l + 1])
        if l + 1 < depth:
            pending = start_gather(l + 1, w_in_l)
            g8 = g8 + pending[4][0:1, 0:1]
        mod3, b3 = _rows8(mod[l]), _rows8(bias[l])
        hb = _norm_mod_fwd(h_x, g8, mod3, b3, f"norm_mod_fwd_{l}")
        proj = _matmul(hb, w_in_l, "nn", BF16, f"in_proj_{l}")
        os_, ls_ = [], []
        for dil in DILATIONS:
            o_p, l_p = _attn_fwd(cfg, proj, slopes, dil, f"attn_fwd_d{dil}_{l}")
            os_.append(o_p)
            ls_.append(l_p)
        lgs = jnp.stack([lg_f[l], lg_b[l]])
        oret = _ret_fwd(cfg, proj, lgs, f"ret_fwd_{l}")
        y, oa, lse = _gate_fwd(cfg, os_, ls_, proj, oret, f"gate_fwd_{l}")
        x_next, out = _out_proj_fwd(y, w_out_l, h_x, mod3, b3, f"out_proj_{l}")
        saved.append((h_x, hb, proj, oret, y, oa, lse, out, g8, mod3, b3, lgs))
        h_x = x_next

    dx, loss8, fin_acc = _final_loss(h_x, tgt, _rows8(final_gain[None]), "final_loss")

    landed = [None] * depth
    d_mod, d_gain, d_lg = [None] * depth, [None] * depth, [None] * depth
    in_flight = None
    for l in reversed(range(depth)):
        x_l, hb, proj, oret, y, oa, lse, out, g8, mod3, b3, lgs = saved[l]
        w_in_l, w_out_l = w_full[l]
        if in_flight is not None:
            mod3 = mod3 + in_flight[-1][0:1, 0:1]
        douts, gate_acc = _out_proj_bwd_prep(dx, out, mod3, b3, f"out_proj_bwd_prep_{l}")
        dy = _matmul(douts, w_out_l, "nt", BF16, f"out_proj_dy_{l}")
        g_out_l = _matmul(y, douts, "tn", BF16, f"out_proj_dw_{l}", tk=1024)
        do_a, dlt, dza, do_r, dzr = _gate_bwd(cfg, dy, oa, proj, oret, f"gate_bwd_{l}")
        dqs, dks, dvs = [], [], []
        for dil in DILATIONS:
            dq_p, dk_p, dv_p = _attn_bwd(cfg, proj, slopes, do_a, lse, dlt, dil, f"attn_bwd_d{dil}_{l}")
            dqs.append(dq_p)
            dks.append(dk_p)
            dvs.append(dv_p)
        rdq, rdk, rdv, dlg = _ret_bwd(cfg, proj, lgs, do_r, f"ret_bwd_{l}")
        dproj = _assemble_dproj(cfg, dqs, dks, dvs, dza, rdq, rdk, rdv, dzr, f"assemble_dproj_{l}")
        dh = _matmul(dproj, w_in_l, "nt", F32, f"in_proj_dh_{l}", tk=1792)
        g_in_l = _matmul(hb, dproj, "tn", BF16, f"in_proj_dw_{l}", tk=1024)
        dx, nm_acc = _norm_mod_bwd(dh, x_l, dx, g8, mod3, b3, f"norm_mod_bwd_{l}")
        d_mod[l] = jnp.concatenate([nm_acc[0], nm_acc[1], gate_acc[0]])
        d_gain[l] = nm_acc[2]
        d_lg[l] = dlg[:, 0:2, 0]
        if in_flight is not None:
            landed[l + 1] = _scatter_wait(*in_flight[:-1], dx, f"scatter_wait_{l + 1}")
        in_flight = _scatter_start(g_in_l, g_out_l, f"scatter_start_{l}")
    landed[0] = _scatter_wait(*in_flight[:-1], dx, "scatter_wait_0")

    gw_in = lax.empty(w_in.shape, F32)
    gw_out = lax.empty(w_out.shape, F32)
    for l in range(depth):
        gi, go, pi, po = landed[l]
        gw_in = _sum_into(gw_in, gi, pi, where2, l, lambda w, nr: w[0] * nr, f"sum_w_in_{l}")
        gw_out = _sum_into(gw_out, go, po, where2, l, lambda w, nr: (2 * w[1] + w[0]) * nr, f"sum_w_out_{l}")
    gw_in, gw_out = _exchange_halves(gw_in, gw_out, "exchange_halves")
    grad_w_in, delta_w_in, new_m_w_in, new_v_w_in = _adamw_big(w_in, gw_in, m_w_in, v_w_in, "adamw_w_in")
    grad_w_out, delta_w_out, new_m_w_out, new_v_w_out = _adamw_big(w_out, gw_out, m_w_out, v_w_out, "adamw_w_out")

    dmod_mine = jnp.stack(d_mod)
    dmod_gathered = _all_gather_small(_rows8(dmod_mine), "gather_dmod")
    dmod_all = dmod_gathered[:, :depth, :]
    grad_b_ada = _sum_gathered(dmod_gathered, "sum_b_ada")[:depth]
    dmod_cols = lax.dynamic_slice_in_dim(dmod_all, chip * ada_cols, ada_cols, axis=2)
    dmod128 = jnp.pad(jnp.transpose(dmod_cols, (1, 0, 2)), ((0, 0), (0, 120), (0, 0))).astype(BF16)
    cact128 = jnp.pad(cact, ((0, 120), (0, 0))).astype(BF16)
    grad_w_ada, delta_w_ada, new_m_w_ada, new_v_w_ada = _adamw_ada(w_ada, m_w_ada, v_w_ada, cact128, dmod128,
                                                                  "adamw_w_ada")

    dlg_all = jnp.stack(d_lg)
    sig_f = jax.nn.sigmoid(-ret_decay_logit_f)
    sig_b = jax.nn.sigmoid(-ret_decay_logit_b)
    nlg = depth * cfg.hr
    pack = jnp.zeros((8, d), F32)
    for l in range(depth):
        pack = pack.at[l].set(d_gain[l])
    pack = pack.at[depth].set(fin_acc[0])
    pack = pack.at[depth + 1, 0].set(loss8[0, 0])
    pack = pack.at[depth + 1, LANES:LANES + nlg].set((dlg_all[:, :, 0] * sig_f).reshape(-1))
    pack = pack.at[depth + 1, 2 * LANES:2 * LANES + nlg].set((dlg_all[:, :, 1] * sig_b).reshape(-1))
    tot = _sum_gathered(_all_gather_small(pack, "gather_small"), "sum_small")
    grad_norm_gain = tot[:depth]
    grad_final_gain = tot[depth]
    loss = tot[depth + 1, 0]
    grad_lf = tot[depth + 1, LANES:LANES + nlg].reshape(depth, cfg.hr)
    grad_lb = tot[depth + 1, 2 * LANES:2 * LANES + nlg].reshape(depth, cfg.hr)

    d_ng, m_ng, v_ng = _adamw_small(norm_gain, grad_norm_gain, m_norm_gain, v_norm_gain, "adamw_norm_gain")
    d_ba, m_ba, v_ba = _adamw_small(b_ada, grad_b_ada, m_b_ada, v_b_ada, "adamw_b_ada")
    d_lf, m_lf, v_lf = _adamw_small(ret_decay_logit_f, grad_lf, m_ret_decay_logit_f, v_ret_decay_logit_f, "adamw_lf")
    d_lb, m_lb, v_lb = _adamw_small(ret_decay_logit_b, grad_lb, m_ret_decay_logit_b, v_ret_decay_logit_b, "adamw_lb")
    d_fg, m_fg, v_fg = _adamw_small(final_gain[None], grad_final_gain[None], m_final_gain[None], v_final_gain[None],
                                    "adamw_final_gain")

    return (loss, dx[None],
            grad_norm_gain, grad_w_ada, grad_b_ada, grad_w_in, grad_w_out, grad_lf, grad_lb, grad_final_gain,
            d_ng, delta_w_ada, d_ba, delta_w_in, delta_w_out, d_lf, d_lb, d_fg[0],
            m_ng, new_m_w_ada, m_ba, new_m_w_in, new_m_w_out, m_lf, m_lb, m_fg[0],
            v_ng, new_v_w_ada, v_ba, new_v_w_in, new_v_w_out, v_lf, v_lb, v_fg[0])
```

```python
import functools

import jax
import jax.numpy as jnp
from jax import lax
from jax.experimental import pallas as pl
from jax.experimental.pallas import tpu as pltpu

F32 = jnp.float32
BF16 = jnp.bfloat16

D_MODEL = 2048
SEQ = 4096
DEPTH = 4
HEAD_DIM = 128
DILATIONS = (1, 4, 16)
RADIUS = 64
N_HEADS_RET = 4
RET_QK = 128
RET_V = 256
RET_CHUNK = 128
NORM_EPS = 1e-6
MASK_VALUE = -1e30
N_DEV = 8
N_CHIP = 4
LANES = 128
VMEM_LIMIT = 56 * 1024 * 1024

ADAM_LR = 0.001
ADAM_B1 = 0.9
ADAM_B2 = 0.999
ADAM_EPS = 1e-08
ADAM_WD = 0.01
ADAM_STEP = 10

MESH = pl.DeviceIdType.MESH


class _Cfg:
    def __init__(self):
        self.d = D_MODEL
        self.s = SEQ
        self.aw = D_MODEL // 2
        self.ha = self.aw // HEAD_DIM
        self.rw = D_MODEL // 2
        self.hr = N_HEADS_RET
        self.rqk = self.hr * RET_QK
        self.f = 4 * self.aw + 2 * self.rqk + 2 * self.rw
        self.qa, self.ka, self.va, self.za = 0, self.aw, 2 * self.aw, 3 * self.aw
        self.qr = 4 * self.aw
        self.kr = self.qr + self.rqk
        self.vr = self.kr + self.rqk
        self.zr = self.vr + self.rw
        assert self.rw == self.hr * RET_V


def _tile(n, pref):
    t = min(n, pref)
    while n % t or t % LANES:
        t -= LANES
    return t


def _params(dims=None):
    return pltpu.CompilerParams(dimension_semantics=dims, vmem_limit_bytes=VMEM_LIMIT)


def _silu(z):
    return z * jax.nn.sigmoid(z)


def _dsilu(z):
    sg = jax.nn.sigmoid(z)
    return sg * (1.0 + z * (1.0 - sg))


_DN = {"nn": (((1,), (0,)), ((), ())), "nt": (((1,), (1,)), ((), ())), "tn": (((0,), (0,)), ((), ()))}


def _matmul(a, b, mode, out_dtype, name, tm=1024, tn=1024, tk=2048):
    if mode == "tn":
        kk, m = a.shape
    else:
        m, kk = a.shape
    n = b.shape[0] if mode == "nt" else b.shape[1]
    tm, tn, tk = _tile(m, tm), _tile(n, tn), _tile(kk, tk)
    nk = kk // tk
    a_spec = (pl.BlockSpec((tk, tm), lambda i, j, k: (k, i)) if mode == "tn"
              else pl.BlockSpec((tm, tk), lambda i, j, k: (i, k)))
    b_spec = (pl.BlockSpec((tn, tk), lambda i, j, k: (j, k)) if mode == "nt"
              else pl.BlockSpec((tk, tn), lambda i, j, k: (k, j)))
    dn = _DN[mode]

    def body(a_ref, b_ref, o_ref, acc_ref):
        k = pl.program_id(2)
        p = lax.dot_general(a_ref[...], b_ref[...], dn, preferred_element_type=F32)

        @pl.when(k == 0)
        def _():
            acc_ref[...] = p

        @pl.when(k > 0)
        def _():
            acc_ref[...] += p

        @pl.when(k == nk - 1)
        def _():
            o_ref[...] = acc_ref[...].astype(out_dtype)

    return pl.pallas_call(
        body, name=name, grid=(m // tm, n // tn, nk),
        in_specs=[a_spec, b_spec],
        out_specs=pl.BlockSpec((tm, tn), lambda i, j, k: (i, j)),
        out_shape=jax.ShapeDtypeStruct((m, n), out_dtype),
        scratch_shapes=[pltpu.VMEM((tm, tn), F32)],
        compiler_params=_params(("parallel", "parallel", "arbitrary")),
    )(a, b)


def _out_proj_fwd(y, w_out, x, mod3, b3, name):
    s, kk = y.shape
    d = w_out.shape[1]
    tm, tn = _tile(s, 512), _tile(d, 1024)

    def body(y_ref, w_ref, x_ref, m_ref, b_ref, xn_ref, o_ref):
        out = jnp.dot(y_ref[...], w_ref[...], preferred_element_type=F32)
        gate = m_ref[2:3, :] + b_ref[2:3, :]
        xn_ref[...] = x_ref[...] + gate * out
        o_ref[...] = out.astype(BF16)

    vec = pl.BlockSpec((8, tn), lambda i, j: (0, j))
    return pl.pallas_call(
        body, name=name, grid=(s // tm, d // tn),
        in_specs=[pl.BlockSpec((tm, kk), lambda i, j: (i, 0)), pl.BlockSpec((kk, tn), lambda i, j: (0, j)),
                  pl.BlockSpec((tm, tn), lambda i, j: (i, j)), vec, vec],
        out_specs=[pl.BlockSpec((tm, tn), lambda i, j: (i, j)), pl.BlockSpec((tm, tn), lambda i, j: (i, j))],
        out_shape=[jax.ShapeDtypeStruct((s, d), F32), jax.ShapeDtypeStruct((s, d), BF16)],
        compiler_params=_params(("parallel", "parallel")),
    )(y, w_out, x, mod3, b3)


def _norm_mod_fwd(x, g8, mod3, b3, name):
    s, d = x.shape
    tr = _tile(s, 512)

    def body(x_ref, g_ref, m_ref, b_ref, h_ref):
        xv = x_ref[...]
        r = lax.rsqrt(jnp.mean(xv * xv, axis=-1, keepdims=True) + NORM_EPS)
        shift = m_ref[0:1, :] + b_ref[0:1, :]
        scale = m_ref[1:2, :] + b_ref[1:2, :]
        h_ref[...] = ((xv * r * g_ref[0:1, :]) * (1.0 + scale) + shift).astype(BF16)

    vec = pl.BlockSpec((8, d), lambda i: (0, 0))
    return pl.pallas_call(
        body, name=name, grid=(s // tr,),
        in_specs=[pl.BlockSpec((tr, d), lambda i: (i, 0)), vec, vec, vec],
        out_specs=pl.BlockSpec((tr, d), lambda i: (i, 0)),
        out_shape=jax.ShapeDtypeStruct((s, d), BF16),
        compiler_params=_params(("parallel",)),
    )(x, g8, mod3, b3)


def _norm_mod_bwd(dh, x, dx_next, g8, mod3, b3, name):
    s, d = x.shape
    tr = _tile(s, 256)

    def body(dh_ref, x_ref, dn_ref, g_ref, m_ref, b_ref, dx_ref, acc_ref):
        @pl.when(pl.program_id(0) == 0)
        def _():
            acc_ref[...] = jnp.zeros_like(acc_ref)

        xv = x_ref[...]
        dh_v = dh_ref[...]
        g = g_ref[0:1, :]
        r = lax.rsqrt(jnp.mean(xv * xv, axis=-1, keepdims=True) + NORM_EPS)
        xn = xv * r
        scale1 = 1.0 + m_ref[1:2, :] + b_ref[1:2, :]
        dhs = dh_v * scale1
        dxn = dhs * g
        dx_ref[...] = dn_ref[...] + r * (dxn - xn * jnp.mean(dxn * xn, axis=-1, keepdims=True))
        acc_ref[0:1, :] += jnp.sum(dh_v, axis=0, keepdims=True)
        acc_ref[1:2, :] += jnp.sum(dh_v * (xn * g), axis=0, keepdims=True)
        acc_ref[2:3, :] += jnp.sum(dhs * xn, axis=0, keepdims=True)

    vec = pl.BlockSpec((8, d), lambda i: (0, 0))
    row = pl.BlockSpec((tr, d), lambda i: (i, 0))
    return pl.pallas_call(
        body, name=name, grid=(s // tr,),
        in_specs=[row, row, row, vec, vec, vec],
        out_specs=[row, vec],
        out_shape=[jax.ShapeDtypeStruct((s, d), F32), jax.ShapeDtypeStruct((8, d), F32)],
        compiler_params=_params(("arbitrary",)),
    )(dh, x, dx_next, g8, mod3, b3)


def _final_loss(x, tgt, g8, name):
    s, d = x.shape
    tr = _tile(s, 256)

    def body(x_ref, t_ref, g_ref, dx_ref, loss_ref, acc_ref):
        @pl.when(pl.program_id(0) == 0)
        def _():
            acc_ref[...] = jnp.zeros_like(acc_ref)
            loss_ref[...] = jnp.zeros_like(loss_ref)

        xv = x_ref[...]
        g = g_ref[0:1, :]
        r = lax.rsqrt(jnp.mean(xv * xv, axis=-1, keepdims=True) + NORM_EPS)
        xn = xv * r
        err = xn * g - t_ref[...]
        loss_ref[...] += 0.5 * jnp.sum(jnp.mean(err * err, axis=-1, keepdims=True), axis=0, keepdims=True)
        dy = err * (1.0 / d)
        acc_ref[0:1, :] += jnp.sum(dy * xn, axis=0, keepdims=True)
        dxn = dy * g
        dx_ref[...] = r * (dxn - xn * jnp.mean(dxn * xn, axis=-1, keepdims=True))

    vec = pl.BlockSpec((8, d), lambda i: (0, 0))
    row = pl.BlockSpec((tr, d), lambda i: (i, 0))
    return pl.pallas_call(
        body, name=name, grid=(s // tr,),
        in_specs=[row, row, vec],
        out_specs=[row, pl.BlockSpec((8, LANES), lambda i: (0, 0)), vec],
        out_shape=[jax.ShapeDtypeStruct((s, d), F32), jax.ShapeDtypeStruct((8, LANES), F32),
                   jax.ShapeDtypeStruct((8, d), F32)],
        compiler_params=_params(("arbitrary",)),
    )(x, tgt, g8)


def _gate_fwd(cfg, oa, proj, oret, name):
    s = cfg.s
    aw, rw = cfg.aw, cfg.rw
    tr = _tile(s, 256)

    def body(oa_ref, za_ref, or_ref, zr_ref, y_ref):
        y_ref[:, 0:aw] = (oa_ref[...].astype(F32) * _silu(za_ref[...].astype(F32))).astype(BF16)
        for h in range(cfg.hr):
            cols = slice(h * RET_V, (h + 1) * RET_V)
            oh = or_ref[:, cols]
            rr = lax.rsqrt(jnp.mean(oh * oh, axis=-1, keepdims=True) + NORM_EPS)
            y_ref[:, aw + h * RET_V:aw + (h + 1) * RET_V] = (
                oh * rr * _silu(zr_ref[:, cols].astype(F32))).astype(BF16)

    ra = pl.BlockSpec((tr, aw), lambda i: (i, 0))
    return pl.pallas_call(
        body, name=name, grid=(s // tr,),
        in_specs=[ra,
                  pl.BlockSpec((tr, aw), lambda i: (i, cfg.za // aw)),
                  pl.BlockSpec((tr, rw), lambda i: (i, 0)),
                  pl.BlockSpec((tr, rw), lambda i: (i, cfg.zr // rw))],
        out_specs=pl.BlockSpec((tr, cfg.d), lambda i: (i, 0)),
        out_shape=jax.ShapeDtypeStruct((s, cfg.d), BF16),
        compiler_params=_params(("parallel",)),
    )(oa, proj, oret, proj)


def _gate_bwd(cfg, dy, oa, proj, oret, name):
    s = cfg.s
    aw, rw = cfg.aw, cfg.rw
    tr = _tile(s, 256)

    def body(dy_ref, oa_ref, za_ref, or_ref, zr_ref, do_ref, dza_ref, dor_ref, dzr_ref):
        dya = dy_ref[:, 0:aw].astype(F32)
        za = za_ref[...].astype(F32)
        do_ref[...] = (dya * _silu(za)).astype(BF16)
        dza_ref[...] = (dya * oa_ref[...].astype(F32) * _dsilu(za)).astype(BF16)
        for h in range(cfg.hr):
            cols = slice(h * RET_V, (h + 1) * RET_V)
            oh = or_ref[:, cols]
            zr = zr_ref[:, cols].astype(F32)
            dyr = dy_ref[:, aw + h * RET_V:aw + (h + 1) * RET_V].astype(F32)
            rr = lax.rsqrt(jnp.mean(oh * oh, axis=-1, keepdims=True) + NORM_EPS)
            yn = oh * rr
            dyn = dyr * _silu(zr)
            dzr_ref[:, cols] = (dyr * yn * _dsilu(zr)).astype(BF16)
            dor_ref[:, cols] = (rr * (dyn - yn * jnp.mean(dyn * yn, axis=-1, keepdims=True))).astype(BF16)

    ra = pl.BlockSpec((tr, aw), lambda i: (i, 0))
    rr_ = pl.BlockSpec((tr, rw), lambda i: (i, 0))
    return pl.pallas_call(
        body, name=name, grid=(s // tr,),
        in_specs=[pl.BlockSpec((tr, cfg.d), lambda i: (i, 0)), ra,
                  pl.BlockSpec((tr, aw), lambda i: (i, cfg.za // aw)), rr_,
                  pl.BlockSpec((tr, rw), lambda i: (i, cfg.zr // rw))],
        out_specs=[ra, ra, rr_, rr_],
        out_shape=[jax.ShapeDtypeStruct((s, aw), BF16), jax.ShapeDtypeStruct((s, aw), BF16),
                   jax.ShapeDtypeStruct((s, rw), BF16), jax.ShapeDtypeStruct((s, rw), BF16)],
        compiler_params=_params(("parallel",)),
    )(dy, oa, proj, oret, proj)


def _out_proj_bwd_prep(dxn, out, mod3, b3, name):
    s, d = dxn.shape
    tr = _tile(s, 512)

    def body(dx_ref, o_ref, m_ref, b_ref, do_ref, acc_ref):
        @pl.when(pl.program_id(0) == 0)
        def _():
            acc_ref[...] = jnp.zeros_like(acc_ref)

        dxv = dx_ref[...]
        gate = m_ref[2:3, :] + b_ref[2:3, :]
        do_ref[...] = (gate * dxv).astype(BF16)
        acc_ref[0:1, :] += jnp.sum(dxv * o_ref[...].astype(F32), axis=0, keepdims=True)

    vec = pl.BlockSpec((8, d), lambda i: (0, 0))
    row = pl.BlockSpec((tr, d), lambda i: (i, 0))
    return pl.pallas_call(
        body, name=name, grid=(s // tr,),
        in_specs=[row, row, vec, vec],
        out_specs=[row, vec],
        out_shape=[jax.ShapeDtypeStruct((s, d), BF16), jax.ShapeDtypeStruct((8, d), F32)],
        compiler_params=_params(("arbitrary",)),
    )(dxn, out, mod3, b3)


def _assemble_dproj(cfg, dq, dk, dv, dza, rdq, rdk, rdv, dzr, name):
    s = cfg.s
    aw, rw, rqk = cfg.aw, cfg.rw, cfg.rqk
    tr = _tile(s, 256)

    def body(dq_ref, dk_ref, dv_ref, dza_ref, rq, rk, rv, dzr_ref, o_ref):
        o_ref[:, cfg.qa:cfg.qa + aw] = dq_ref[...]
        o_ref[:, cfg.ka:cfg.ka + aw] = dk_ref[...]
        o_ref[:, cfg.va:cfg.va + aw] = dv_ref[...]
        o_ref[:, cfg.za:cfg.za + aw] = dza_ref[...]
        o_ref[:, cfg.qr:cfg.qr + rqk] = rq[...].astype(BF16)
        o_ref[:, cfg.kr:cfg.kr + rqk] = rk[...].astype(BF16)
        o_ref[:, cfg.vr:cfg.vr + rw] = rv[...].astype(BF16)
        o_ref[:, cfg.zr:cfg.zr + rw] = dzr_ref[...]

    ra = pl.BlockSpec((tr, aw), lambda i: (i, 0))
    rq_ = pl.BlockSpec((tr, rqk), lambda i: (i, 0))
    rv_ = pl.BlockSpec((tr, rw), lambda i: (i, 0))
    return pl.pallas_call(
        body, name=name, grid=(s // tr,),
        in_specs=[ra] * 4 + [rq_, rq_, rv_, rv_],
        out_specs=pl.BlockSpec((tr, cfg.f), lambda i: (i, 0)),
        out_shape=jax.ShapeDtypeStruct((s, cfg.f), BF16),
        compiler_params=_params(("parallel",)),
    )(dq, dk, dv, dza, rdq, rdk, rdv, dzr)


_COPY_ROWS = 512


def _attn_geometry(cfg, dil):
    sub = cfg.s // dil
    bq = min(128, sub)
    win = min(256, sub)
    return sub, bq, win


_N_SHIFTS = 3


def _attn_rows(dil, r, i, sub, bq, win):
    margin = (win - bq) // 2
    ws = jnp.clip(i * bq - margin, 0, sub - win)
    shift = (i * bq - ws) // margin if margin else 0
    if dil == 1:
        return (shift, pl.ds(pl.multiple_of(i * bq, bq), bq), pl.ds(pl.multiple_of(ws, 64), win))
    return (shift, pl.ds(r + i * (bq * dil), bq, stride=dil), pl.ds(r + ws * dil, win, stride=dil))


def _fill_bias_tables(cfg, bias_ref, slope):
    for pattern, dil in enumerate(DILATIONS):
        _, bq, win = _attn_geometry(cfg, dil)
        margin = (win - bq) // 2
        rel0 = lax.broadcasted_iota(jnp.int32, (bq, win), 1) - lax.broadcasted_iota(jnp.int32, (bq, win), 0)
        for shift in range(_N_SHIFTS if margin else 1):
            arel = jnp.abs(rel0 - shift * margin)
            bias_ref[pattern * _N_SHIFTS + shift, 0:bq, 0:win] = jnp.where(
                arel <= RADIUS, -(slope * dil) * arel.astype(F32), MASK_VALUE)


def _bias_scratch():
    return pltpu.VMEM((len(DILATIONS) * _N_SHIFTS, 128, 256), F32)


def _attn_scores(cfg, bias_ref, q, kw, pattern, shift):
    bq, win = q.shape[0], kw.shape[0]
    return (lax.dot_general(q, kw, _DN["nt"], preferred_element_type=F32)
            + bias_ref[pattern * _N_SHIFTS + shift, 0:bq, 0:win])


_ATTN_GROUP = 4


def _for_each_group(cfg, group):
    for pattern, dil in enumerate(DILATIONS):
        sub, bq, _ = _attn_geometry(cfg, dil)
        nblk = sub // bq
        per = min(_ATTN_GROUP, nblk)
        for r in range(dil):
            if nblk == per:
                group(pattern, dil, [(r, i) for i in range(nblk)])
            else:
                def step(g, carry, pattern=pattern, dil=dil, r=r, per=per):
                    group(pattern, dil, [(r, g * per + j) for j in range(per)])
                    return carry

                lax.fori_loop(0, nblk // per, step, 0)


def _attn_fwd(cfg, proj, slopes, name):
    s = cfg.s
    scale = HEAD_DIM ** -0.5

    def body(sl_ref, q_ref, k_ref, v_ref, o_ref, lse_ref, qf, kf, vf, acc, m_s, l_s, bias_s):
        slope = sl_ref[pl.program_id(0)]

        def to_f32(i, carry):
            rows = pl.ds(pl.multiple_of(i * _COPY_ROWS, _COPY_ROWS), _COPY_ROWS)
            qf[rows, :] = q_ref[rows, :].astype(F32) * scale
            kf[rows, :] = k_ref[rows, :].astype(F32)
            vf[rows, :] = v_ref[rows, :].astype(F32)
            return carry

        lax.fori_loop(0, s // _COPY_ROWS, to_f32, 0)

        _fill_bias_tables(cfg, bias_s, slope)

        def group(pattern, dil, blocks):
            sub, bq, win = _attn_geometry(cfg, dil)
            rep = win // HEAD_DIM
            first = pattern == 0
            work = []
            for r, i in blocks:
                shift, qrows, krows = _attn_rows(dil, r, i, sub, bq, win)
                old = None if first else (m_s[qrows, :], l_s[qrows, :], acc[qrows, :])
                work.append((shift, qrows, qf[qrows, :].astype(BF16), kf[krows, :].astype(BF16),
                             vf[krows, :].astype(BF16), old))
            new = []
            for shift, qrows, q, kw, vw, old in work:
                sc = _attn_scores(cfg, bias_s, q, kw, pattern, shift)
                m_blk = jnp.max(sc, axis=-1, keepdims=True)
                if first:
                    m_new = jnp.broadcast_to(m_blk, (bq, HEAD_DIM))
                    p = jnp.exp(sc - m_blk)
                    l_new = jnp.broadcast_to(jnp.sum(p, axis=-1, keepdims=True), (bq, HEAD_DIM))
                    a_new = jnp.dot(p.astype(BF16), vw, preferred_element_type=F32)
                else:
                    m_old, l_old, a_old = old
                    m_new = jnp.maximum(m_old, m_blk)
                    alpha = jnp.exp(m_old - m_new)
                    p = jnp.exp(sc - jnp.tile(m_new, (1, rep)))
                    l_new = alpha * l_old + jnp.sum(p, axis=-1, keepdims=True)
                    a_new = alpha * a_old + jnp.dot(p.astype(BF16), vw, preferred_element_type=F32)
                new.append((qrows, m_new, l_new, a_new))
            for qrows, m_new, l_new, a_new in new:
                m_s[qrows, :] = m_new
                l_s[qrows, :] = l_new
                acc[qrows, :] = a_new

        _for_each_group(cfg, group)

        def finish(i, carry):
            rows = pl.ds(pl.multiple_of(i * _COPY_ROWS, _COPY_ROWS), _COPY_ROWS)
            den = l_s[rows, :]
            o_ref[rows, :] = (acc[rows, :] / den).astype(BF16)
            lse_ref[rows, :] = m_s[rows, :] + jnp.log(den)
            return carry

        lax.fori_loop(0, s // _COPY_ROWS, finish, 0)

    def col(off):
        return pl.BlockSpec((s, HEAD_DIM), lambda h: (0, off // HEAD_DIM + h))

    head = pl.BlockSpec((s, HEAD_DIM), lambda h: (0, h))
    return pl.pallas_call(
        body, name=name, grid=(cfg.ha,),
        in_specs=[pl.BlockSpec(memory_space=pltpu.SMEM), col(cfg.qa), col(cfg.ka), col(cfg.va)],
        out_specs=[head, head],
        out_shape=[jax.ShapeDtypeStruct((s, cfg.aw), BF16), jax.ShapeDtypeStruct((s, cfg.aw), F32)],
        scratch_shapes=[pltpu.VMEM((s, HEAD_DIM), F32)] * 6 + [_bias_scratch()],
        compiler_params=_params(("parallel",)),
    )(slopes, proj, proj, proj)


def _attn_bwd(cfg, proj, slopes, do, oa, lse, name):
    s = cfg.s
    scale = HEAD_DIM ** -0.5

    def body(sl_ref, q_ref, k_ref, v_ref, do_ref, o_ref, lse_ref, dq_ref, dk_ref, dv_ref,
             qf, kf, vf, dof, dlt, dqa, dka, dva, bias_s):
        slope = sl_ref[pl.program_id(0)]

        def to_f32(i, carry):
            rows = pl.ds(pl.multiple_of(i * _COPY_ROWS, _COPY_ROWS), _COPY_ROWS)
            qf[rows, :] = q_ref[rows, :].astype(F32) * scale
            kf[rows, :] = k_ref[rows, :].astype(F32)
            vf[rows, :] = v_ref[rows, :].astype(F32)
            dov = do_ref[rows, :].astype(F32)
            dof[rows, :] = dov
            dlt[rows, :] = jnp.broadcast_to(jnp.sum(dov * o_ref[rows, :].astype(F32), axis=-1, keepdims=True),
                                            (_COPY_ROWS, HEAD_DIM))
            zero = jnp.zeros((_COPY_ROWS, HEAD_DIM), F32)
            dqa[rows, :] = zero
            dka[rows, :] = zero
            dva[rows, :] = zero
            return carry

        lax.fori_loop(0, s // _COPY_ROWS, to_f32, 0)

        _fill_bias_tables(cfg, bias_s, slope)

        def group(pattern, dil, blocks):
            sub, bq, win = _attn_geometry(cfg, dil)
            rep = win // HEAD_DIM
            work = []
            for r, i in blocks:
                shift, qrows, krows = _attn_rows(dil, r, i, sub, bq, win)
                work.append((shift, qrows, krows, qf[qrows, :].astype(BF16), kf[krows, :].astype(BF16),
                             vf[krows, :].astype(BF16), dof[qrows, :].astype(BF16),
                             lse_ref[qrows, :], dlt[qrows, :]))
            new = []
            for shift, qrows, krows, q, kw, vw, dob, lse_b, dlt_b in work:
                sc = _attn_scores(cfg, bias_s, q, kw, pattern, shift)
                p = jnp.exp(sc - jnp.tile(lse_b, (1, rep)))
                dp = lax.dot_general(dob, vw, _DN["nt"], preferred_element_type=F32)
                ds = (p * (dp - jnp.tile(dlt_b, (1, rep)))).astype(BF16)
                new.append((qrows, krows,
                            jnp.dot(ds, kw, preferred_element_type=F32) * scale,
                            lax.dot_general(ds, q, _DN["tn"], preferred_element_type=F32),
                            lax.dot_general(p.astype(BF16), dob, _DN["tn"], preferred_element_type=F32)))
            for qrows, krows, dq_b, dk_b, dv_b in new:
                dqa[qrows, :] += dq_b
                dka[krows, :] += dk_b
                dva[krows, :] += dv_b

        _for_each_group(cfg, group)

        def finish(i, carry):
            rows = pl.ds(pl.multiple_of(i * _COPY_ROWS, _COPY_ROWS), _COPY_ROWS)
            dq_ref[rows, :] = dqa[rows, :].astype(BF16)
            dk_ref[rows, :] = dka[rows, :].astype(BF16)
            dv_ref[rows, :] = dva[rows, :].astype(BF16)
            return carry

        lax.fori_loop(0, s // _COPY_ROWS, finish, 0)

    def col(off):
        return pl.BlockSpec((s, HEAD_DIM), lambda h: (0, off // HEAD_DIM + h))

    head = pl.BlockSpec((s, HEAD_DIM), lambda h: (0, h))
    shp = jax.ShapeDtypeStruct((s, cfg.aw), BF16)
    return pl.pallas_call(
        body, name=name, grid=(cfg.ha,),
        in_specs=[pl.BlockSpec(memory_space=pltpu.SMEM), col(cfg.qa), col(cfg.ka), col(cfg.va), head, head, head],
        out_specs=[head, head, head],
        out_shape=[shp, shp, shp],
        scratch_shapes=[pltpu.VMEM((s, HEAD_DIM), F32)] * 8 + [_bias_scratch()],
        compiler_params=_params(("parallel",)),
    )(slopes, proj, proj, proj, do, oa, lse)


def _decay_tables(lg, backward):
    c = RET_CHUNK
    a = lax.broadcasted_iota(jnp.int32, (c, c), 0)
    b = lax.broadcasted_iota(jnp.int32, (c, c), 1)
    idx = lax.broadcasted_iota(jnp.int32, (c, 1), 0).astype(F32)
    if backward:
        rel = (b - a).astype(F32)
        ex_xi = c - idx
        ex_zeta = idx
    else:
        rel = (a - b).astype(F32)
        ex_xi = idx + 1.0
        ex_zeta = c - 1.0 - idx
    relc = jnp.maximum(rel, 0.0)
    dm = jnp.where(rel >= 0, jnp.exp(relc * lg), 0.0)
    xi = jnp.exp(ex_xi * lg)
    zeta = jnp.exp(ex_zeta * lg)
    gch = jnp.exp(jnp.full((1, 1), c, F32) * lg)
    return relc, dm, xi, zeta, ex_xi, ex_zeta, gch


def _ret_fwd(cfg, proj, lgs, name):
    s = cfg.s
    c = RET_CHUNK
    n = s // c
    kscale = RET_QK ** -0.5

    def body(lg_ref, q_ref, k_ref, v_ref, o_ref, st_ref):
        h = pl.program_id(0)
        tabs = [_decay_tables(lg_ref[dirn, h], dirn == 1) for dirn in range(2)]
        st_ref[...] = jnp.zeros_like(st_ref)
        o_ref[...] = jnp.zeros_like(o_ref)

        def step(t, carry):
            for dirn in range(2):
                _, dm, xi, zeta, _, _, gch = tabs[dirn]
                i = (n - 1 - t) if dirn == 1 else t
                rows = pl.ds(pl.multiple_of(i * c, c), c)
                qi = q_ref[rows, :]
                ks = k_ref[rows, :].astype(F32) * kscale
                vi = v_ref[rows, :]
                inner = lax.dot_general(qi, ks.astype(BF16), _DN["nt"], preferred_element_type=F32) * dm
                st = st_ref[dirn]
                o_ref[rows, :] += (jnp.dot(inner.astype(BF16), vi, preferred_element_type=F32)
                                   + jnp.dot(qi, st.astype(BF16), preferred_element_type=F32) * xi)
                st_ref[dirn] = st * gch + lax.dot_general((ks * zeta).astype(BF16), vi, _DN["tn"],
                                                          preferred_element_type=F32)
            return carry

        lax.fori_loop(0, n, step, 0)

    return pl.pallas_call(
        body, name=name, grid=(cfg.hr,),
        in_specs=[pl.BlockSpec(memory_space=pltpu.SMEM),
                  pl.BlockSpec((s, RET_QK), lambda h: (0, cfg.qr // RET_QK + h)),
                  pl.BlockSpec((s, RET_QK), lambda h: (0, cfg.kr // RET_QK + h)),
                  pl.BlockSpec((s, RET_V), lambda h: (0, cfg.vr // RET_V + h))],
        out_specs=pl.BlockSpec((s, RET_V), lambda h: (0, h)),
        out_shape=jax.ShapeDtypeStruct((s, cfg.rw), F32),
        scratch_shapes=[pltpu.VMEM((2, RET_QK, RET_V), F32)],
        compiler_params=_params(("parallel",)),
    )(lgs, proj, proj, proj)


def _ret_bwd(cfg, proj, lgs, do, name):
    s = cfg.s
    c = RET_CHUNK
    n = s // c
    kscale = RET_QK ** -0.5

    def body(lg_ref, q_ref, k_ref, v_ref, do_ref, dq_ref, dk_ref, dv_ref, dlg_ref, states, t_ref):
        h = pl.program_id(0)
        tabs = [_decay_tables(lg_ref[dirn, h], dirn == 1) for dirn in range(2)]
        dlg_ref[...] = jnp.zeros_like(dlg_ref)
        dq_ref[...] = jnp.zeros_like(dq_ref)
        dk_ref[...] = jnp.zeros_like(dk_ref)
        dv_ref[...] = jnp.zeros_like(dv_ref)

        def chunk_rows(dirn, t):
            i = (n - 1 - t) if dirn == 1 else t
            return pl.ds(pl.multiple_of(i * c, c), c)

        t_ref[...] = jnp.zeros_like(t_ref)

        def fwd_step(t, carry):
            for dirn in range(2):
                _, _, _, zeta, _, _, gch = tabs[dirn]
                rows = chunk_rows(dirn, t)
                st = t_ref[dirn]
                states[dirn, t] = st
                ks = k_ref[rows, :].astype(F32) * kscale
                t_ref[dirn] = st * gch + lax.dot_general((ks * zeta).astype(BF16), v_ref[rows, :], _DN["tn"],
                                                         preferred_element_type=F32)
            return carry

        lax.fori_loop(0, n, fwd_step, 0)
        t_ref[...] = jnp.zeros_like(t_ref)

        def bwd_step(u, accs):
            t = n - 1 - u
            new = []
            for dirn in range(2):
                relc, dm, xi, zeta, ex_xi, ex_zeta, gch = tabs[dirn]
                rows = chunk_rows(dirn, t)
                qi = q_ref[rows, :]
                ks = k_ref[rows, :].astype(F32) * kscale
                ksb = ks.astype(BF16)
                vi = v_ref[rows, :]
                doi = do_ref[rows, :]
                sn_f = states[dirn, t]
                sn = sn_f.astype(BF16)
                tt = t_ref[dirn]
                ttb = tt.astype(BF16)
                a_mat = lax.dot_general(qi, ksb, _DN["nt"], preferred_element_type=F32) * dm
                dov = lax.dot_general(doi, vi, _DN["nt"], preferred_element_type=F32)
                b_mat = (dov * dm).astype(BF16)
                kz = (ks * zeta).astype(BF16)
                d_v = (jnp.dot(kz, ttb, preferred_element_type=F32)
                       + lax.dot_general(a_mat.astype(BF16), doi, _DN["tn"], preferred_element_type=F32))
                dk_inter = lax.dot_general(vi, ttb, _DN["nt"], preferred_element_type=F32) * zeta
                d_k = lax.dot_general(b_mat, qi, _DN["tn"], preferred_element_type=F32) + dk_inter
                o_inter = jnp.dot(qi, sn, preferred_element_type=F32) * xi
                d_q = (jnp.dot(b_mat, ksb, preferred_element_type=F32)
                       + lax.dot_general(doi, sn, _DN["nt"], preferred_element_type=F32) * xi)
                part = (jnp.sum(relc * a_mat * dov, keepdims=True)
                        + jnp.sum(ex_xi * jnp.sum(doi.astype(F32) * o_inter, axis=-1, keepdims=True), keepdims=True)
                        + jnp.sum(ex_zeta * jnp.sum(ks * dk_inter, axis=-1, keepdims=True), keepdims=True)
                        + (c * gch) * jnp.sum(tt * sn_f, keepdims=True))
                t_ref[dirn] = tt * gch + lax.dot_general((qi.astype(F32) * xi).astype(BF16), doi, _DN["tn"],
                                                         preferred_element_type=F32)
                dq_ref[rows, :] += d_q
                dk_ref[rows, :] += d_k * kscale
                dv_ref[rows, :] += d_v
                new.append(accs[dirn] + part)
            return tuple(new)

        totals = lax.fori_loop(0, n, bwd_step, (jnp.zeros((1, 1), F32), jnp.zeros((1, 1), F32)))
        for dirn in range(2):
            dlg_ref[0, dirn:dirn + 1, :] = jnp.broadcast_to(totals[dirn], (1, LANES))

    return pl.pallas_call(
        body, name=name, grid=(cfg.hr,),
        in_specs=[pl.BlockSpec(memory_space=pltpu.SMEM),
                  pl.BlockSpec((s, RET_QK), lambda h: (0, cfg.qr // RET_QK + h)),
                  pl.BlockSpec((s, RET_QK), lambda h: (0, cfg.kr // RET_QK + h)),
                  pl.BlockSpec((s, RET_V), lambda h: (0, cfg.vr // RET_V + h)),
                  pl.BlockSpec((s, RET_V), lambda h: (0, h))],
        out_specs=[pl.BlockSpec((s, RET_QK), lambda h: (0, h)), pl.BlockSpec((s, RET_QK), lambda h: (0, h)),
                   pl.BlockSpec((s, RET_V), lambda h: (0, h)), pl.BlockSpec((1, 8, LANES), lambda h: (h, 0, 0))],
        out_shape=[jax.ShapeDtypeStruct((s, cfg.rqk), F32), jax.ShapeDtypeStruct((s, cfg.rqk), F32),
                   jax.ShapeDtypeStruct((s, cfg.rw), F32), jax.ShapeDtypeStruct((cfg.hr, 8, LANES), F32)],
        scratch_shapes=[pltpu.VMEM((2, n, RET_QK, RET_V), F32), pltpu.VMEM((2, RET_QK, RET_V), F32)],
        compiler_params=_params(("parallel",)),
    )(lgs, proj, proj, proj, do)


def _ada_fwd(cact16, w_ada, name):
    depth, d, n = w_ada.shape
    tn = _tile(n, 768)

    def body(c_ref, w_ref, o_ref):
        o_ref[0] = jnp.dot(c_ref[...], w_ref[0].astype(BF16), preferred_element_type=F32)

    return pl.pallas_call(
        body, name=name, grid=(depth, n // tn),
        in_specs=[pl.BlockSpec((16, d), lambda l, j: (0, 0)), pl.BlockSpec((1, d, tn), lambda l, j: (l, 0, j))],
        out_specs=pl.BlockSpec((1, 16, tn), lambda l, j: (l, 0, j)),
        out_shape=jax.ShapeDtypeStruct((depth, 16, n), F32),
        compiler_params=_params(("parallel", "parallel")),
    )(cact16, w_ada)


def _adam_math(w, g, m, v):
    m2 = ADAM_B1 * m + (1.0 - ADAM_B1) * g
    v2 = ADAM_B2 * v + (1.0 - ADAM_B2) * (g * g)
    m_hat = m2 / (1.0 - ADAM_B1 ** ADAM_STEP)
    v_hat = v2 / (1.0 - ADAM_B2 ** ADAM_STEP)
    delta = -ADAM_LR * (m_hat / (jnp.sqrt(v_hat) + ADAM_EPS) + ADAM_WD * w)
    return delta, m2, v2


def _adamw_big(w, g, m, v, name):
    depth, r, c = w.shape
    tr, tc = _tile(r, 512), _tile(c, 1024)

    def body(w_ref, g_ref, m_ref, v_ref, go_ref, d_ref, mo_ref, vo_ref):
        gv = g_ref[...]
        delta, m2, v2 = _adam_math(w_ref[...], gv, m_ref[...], v_ref[...])
        go_ref[...] = gv
        d_ref[...] = delta
        mo_ref[...] = m2
        vo_ref[...] = v2

    spec = pl.BlockSpec((1, tr, tc), lambda l, i, j: (l, i, j))
    shp = jax.ShapeDtypeStruct(w.shape, F32)
    return pl.pallas_call(
        body, name=name, grid=(depth, r // tr, c // tc),
        in_specs=[spec] * 4, out_specs=[spec] * 4, out_shape=[shp] * 4,
        compiler_params=_params(("parallel", "parallel", "parallel")),
    )(w, g, m, v)


def _adamw_ada(w, m, v, cact128, dmod128, name):
    depth, r, c = w.shape
    tr, tc = _tile(r, 512), _tile(c, 768)

    def body(w_ref, m_ref, v_ref, c_ref, dm_ref, go_ref, d_ref, mo_ref, vo_ref):
        gv = lax.dot_general(c_ref[...], dm_ref[0], _DN["tn"], preferred_element_type=F32)
        delta, m2, v2 = _adam_math(w_ref[0], gv, m_ref[0], v_ref[0])
        go_ref[0] = gv
        d_ref[0] = delta
        mo_ref[0] = m2
        vo_ref[0] = v2

    spec = pl.BlockSpec((1, tr, tc), lambda l, i, j: (l, i, j))
    shp = jax.ShapeDtypeStruct(w.shape, F32)
    return pl.pallas_call(
        body, name=name, grid=(depth, r // tr, c // tc),
        in_specs=[spec] * 3 + [pl.BlockSpec((128, tr), lambda l, i, j: (0, i)),
                               pl.BlockSpec((1, 128, tc), lambda l, i, j: (l, 0, j))],
        out_specs=[spec] * 4, out_shape=[shp] * 4,
        compiler_params=_params(("parallel", "parallel", "parallel")),
    )(w, m, v, cact128, dmod128)


def _adamw_small(w, g, m, v, name):
    def body(w_ref, g_ref, m_ref, v_ref, d_ref, mo_ref, vo_ref):
        delta, m2, v2 = _adam_math(w_ref[...], g_ref[...], m_ref[...], v_ref[...])
        d_ref[...] = delta
        mo_ref[...] = m2
        vo_ref[...] = v2

    shp = jax.ShapeDtypeStruct(w.shape, F32)
    return pl.pallas_call(body, name=name, out_shape=[shp] * 3)(w, g, m, v)


def _sum_gathered(parts, name):
    nd, r, c = parts.shape

    def body(p_ref, o_ref):
        acc = p_ref[0]
        for e in range(1, nd):
            acc = acc + p_ref[e]
        o_ref[...] = acc

    return pl.pallas_call(body, name=name, out_shape=jax.ShapeDtypeStruct((r, c), F32))(parts)


def _flip(v, bit):
    return 1 - v if bit else v


def _all_gather_small(x, name):
    r, c = x.shape

    def body(x_ref, out_ref, send_sems, recv_sems, local_sem):
        mx, my, mc = lax.axis_index("x"), lax.axis_index("y"), lax.axis_index("c")
        me = 4 * mx + 2 * my + mc
        mine = pltpu.make_async_copy(x_ref, out_ref.at[me], local_sem)
        mine.start()
        sends = []
        for k in range(1, N_DEV):
            peer = (_flip(mx, k & 4), _flip(my, k & 2), _flip(mc, k & 1))
            cp = pltpu.make_async_remote_copy(src_ref=x_ref, dst_ref=out_ref.at[me], send_sem=send_sems.at[k - 1],
                                              recv_sem=recv_sems.at[k - 1], device_id=peer, device_id_type=MESH)
            cp.start()
            sends.append(cp)
        for k in range(1, N_DEV):
            peer = (_flip(mx, k & 4), _flip(my, k & 2), _flip(mc, k & 1))
            src = 4 * peer[0] + 2 * peer[1] + peer[2]
            pltpu.make_async_remote_copy(src_ref=x_ref, dst_ref=out_ref.at[src], send_sem=send_sems.at[k - 1],
                                         recv_sem=recv_sems.at[k - 1], device_id=peer,
                                         device_id_type=MESH).wait_recv()
        for cp in sends:
            cp.wait_send()
        mine.wait()

    return pl.pallas_call(
        body, name=name,
        out_shape=jax.ShapeDtypeStruct((N_DEV, r, c), x.dtype),
        in_specs=[pl.BlockSpec(memory_space=pltpu.VMEM)],
        out_specs=pl.BlockSpec(memory_space=pltpu.VMEM),
        scratch_shapes=[pltpu.SemaphoreType.DMA((N_DEV - 1,)), pltpu.SemaphoreType.DMA((N_DEV - 1,)),
                        pltpu.SemaphoreType.DMA],
        compiler_params=pltpu.CompilerParams(vmem_limit_bytes=VMEM_LIMIT),
    )(x)


_HBM = pl.BlockSpec(memory_space=pltpu.HBM)
_SEM = pl.BlockSpec(memory_space=pltpu.SEMAPHORE)
_ANY = pl.BlockSpec(memory_space=pl.ANY)
_EFFECT = pltpu.SideEffectType.DATAFLOW_SIDE_EFFECTING


def _in_hbm(a):
    return pltpu.with_memory_space_constraint(a, pltpu.HBM)


def _place_w_in(w, layer, chip1, name):
    _, d, fc = w.shape
    tr = _tile(d, 512)

    def body(c_ref, w_ref, o_ref):
        o_ref[...] = w_ref[...].astype(BF16)

    return pl.pallas_call(
        body, name=name,
        grid_spec=pltpu.PrefetchScalarGridSpec(
            num_scalar_prefetch=1, grid=(d // tr,),
            in_specs=[pl.BlockSpec((None, tr, fc), lambda i, c: (layer, i, 0))],
            out_specs=pl.BlockSpec((tr, fc), lambda i, c: (i, c[0]))),
        out_shape=jax.ShapeDtypeStruct((d, N_CHIP * fc), BF16),
        compiler_params=_params(("parallel",)),
    )(chip1, w)


def _place_w_out(w, layer, chip1, name):
    _, rc, dd = w.shape
    tc = _tile(dd, 1024)

    def body(c_ref, w_ref, o_ref):
        o_ref[...] = w_ref[...].astype(BF16)

    return pl.pallas_call(
        body, name=name,
        grid_spec=pltpu.PrefetchScalarGridSpec(
            num_scalar_prefetch=1, grid=(dd // tc,),
            in_specs=[pl.BlockSpec((None, rc, tc), lambda j, c: (layer, 0, j))],
            out_specs=pl.BlockSpec((rc, tc), lambda j, c: (c[0], j))),
        out_shape=jax.ShapeDtypeStruct((N_CHIP * rc, dd), BF16),
        compiler_params=_params(("parallel",)),
    )(chip1, w)


def _weight_regions(fi_ref, fo_ref, chip):
    fc = fi_ref.shape[1] // N_CHIP
    rc = fo_ref.shape[0] // N_CHIP
    return (fi_ref.at[:, pl.ds(pl.multiple_of(chip * fc, LANES), fc)],
            fo_ref.at[pl.ds(pl.multiple_of(chip * rc, 8), rc), :])


def _gather_copies(fi_ref, fo_ref, send_sems, recv_sems, landing):
    mx, my, mc = lax.axis_index("x"), lax.axis_index("y"), lax.axis_index("c")
    mine = _weight_regions(fi_ref, fo_ref, 2 * mx + my)
    copies = []
    for j in range(1, N_CHIP):
        peer = (_flip(mx, j & 2), _flip(my, j & 1), mc)
        dst = _weight_regions(fi_ref, fo_ref, 2 * peer[0] + peer[1]) if landing else mine
        for t in range(2):
            idx = 2 * (j - 1) + t
            copies.append(pltpu.make_async_remote_copy(src_ref=mine[t], dst_ref=dst[t], send_sem=send_sems.at[idx],
                                                       recv_sem=recv_sems.at[idx], device_id=peer,
                                                       device_id_type=MESH))
    return copies


def _gather_start(fi, fo, dep, name):
    ns = 2 * (N_CHIP - 1)

    def body(fi_ref, fo_ref, dep_ref, send_sems, recv_sems, fi_thru, fo_thru, token):
        for cp in _gather_copies(fi_ref, fo_ref, send_sems, recv_sems, landing=False):
            cp.start()
        token[...] = jnp.zeros_like(token)

    return pl.pallas_call(
        body, name=name,
        out_shape=(pltpu.SemaphoreType.DMA((ns,)), pltpu.SemaphoreType.DMA((ns,)),
                   pltpu.HBM(fi.shape, fi.dtype), pltpu.HBM(fo.shape, fo.dtype),
                   jax.ShapeDtypeStruct((8, LANES), F32)),
        in_specs=(_HBM, _HBM, _ANY),
        out_specs=(_SEM, _SEM, _HBM, _HBM, pl.BlockSpec(memory_space=pltpu.VMEM)),
        input_output_aliases={0: 2, 1: 3},
        compiler_params=pltpu.CompilerParams(has_side_effects=_EFFECT),
    )(_in_hbm(fi), _in_hbm(fo), dep)


def _gather_wait(send_sems, recv_sems, fi, fo, after, name):
    def body(fi_ref, fo_ref, send_sems, recv_sems, after_ref, fi_out, fo_out):
        for cp in _gather_copies(fi_ref, fo_ref, send_sems, recv_sems, landing=True):
            cp.wait_send()
            cp.wait_recv()

    return pl.pallas_call(
        body, name=name,
        out_shape=(pltpu.HBM(fi.shape, fi.dtype), pltpu.HBM(fo.shape, fo.dtype)),
        in_specs=(_HBM, _HBM, _SEM, _SEM, _ANY), out_specs=(_HBM, _HBM),
        input_output_aliases={0: 0, 1: 1},
        compiler_params=pltpu.CompilerParams(has_side_effects=_EFFECT),
    )(fi, fo, send_sems, recv_sems, after)


def _scatter_copies(gi_ref, go_ref, pi_ref, po_ref, send_sems, recv_sems):
    mx, my, mc = lax.axis_index("x"), lax.axis_index("y"), lax.axis_index("c")
    hr, fc = pi_ref.shape[1:]
    ro = po_ref.shape[1]
    copies = []
    for k in range(1, N_DEV):
        peer = (_flip(mx, k & 4), _flip(my, k & 2), _flip(mc, k & 1))
        pchip = 2 * peer[0] + peer[1]
        src = (gi_ref.at[pl.ds(pl.multiple_of(peer[2] * hr, 8), hr), pl.ds(pl.multiple_of(pchip * fc, LANES), fc)],
               go_ref.at[pl.ds(pl.multiple_of((2 * pchip + peer[2]) * ro, 8), ro), :])
        dst = (pi_ref.at[k - 1], po_ref.at[k - 1])
        for t in range(2):
            idx = 2 * (k - 1) + t
            copies.append(pltpu.make_async_remote_copy(src_ref=src[t], dst_ref=dst[t], send_sem=send_sems.at[idx],
                                                       recv_sem=recv_sems.at[idx], device_id=peer,
                                                       device_id_type=MESH))
    return copies


def _scatter_start(gi, go, name):
    d, f = gi.shape
    dd = go.shape[1]
    ns = 2 * (N_DEV - 1)
    pi = lax.empty((N_DEV - 1, d // 2, f // N_CHIP), BF16)
    po = lax.empty((N_DEV - 1, d // N_DEV, dd), BF16)

    def body(gi_ref, go_ref, pi_ref, po_ref, send_sems, recv_sems, gi_thru, go_thru, pi_thru, po_thru, token):
        for cp in _scatter_copies(gi_ref, go_ref, pi_ref, po_ref, send_sems, recv_sems):
            cp.start()
        token[...] = jnp.zeros_like(token)

    return pl.pallas_call(
        body, name=name,
        out_shape=(pltpu.SemaphoreType.DMA((ns,)), pltpu.SemaphoreType.DMA((ns,)),
                   pltpu.HBM(gi.shape, gi.dtype), pltpu.HBM(go.shape, go.dtype),
                   pltpu.HBM(pi.shape, pi.dtype), pltpu.HBM(po.shape, po.dtype),
                   jax.ShapeDtypeStruct((8, LANES), F32)),
        in_specs=(_HBM, _HBM, _HBM, _HBM),
        out_specs=(_SEM, _SEM, _HBM, _HBM, _HBM, _HBM, pl.BlockSpec(memory_space=pltpu.VMEM)),
        input_output_aliases={0: 2, 1: 3, 2: 4, 3: 5},
        compiler_params=pltpu.CompilerParams(has_side_effects=_EFFECT),
    )(_in_hbm(gi), _in_hbm(go), _in_hbm(pi), _in_hbm(po))


def _scatter_wait(send_sems, recv_sems, gi, go, pi, po, after, name):
    def body(gi_ref, go_ref, pi_ref, po_ref, send_sems, recv_sems, after_ref, gi_out, go_out, pi_out, po_out):
        for cp in _scatter_copies(gi_ref, go_ref, pi_ref, po_ref, send_sems, recv_sems):
            cp.wait_send()
            cp.wait_recv()

    return pl.pallas_call(
        body, name=name,
        out_shape=tuple(pltpu.HBM(a.shape, a.dtype) for a in (gi, go, pi, po)),
        in_specs=(_HBM, _HBM, _HBM, _HBM, _SEM, _SEM, _ANY), out_specs=(_HBM, _HBM, _HBM, _HBM),
        input_output_aliases={0: 0, 1: 1, 2: 2, 3: 3},
        compiler_params=pltpu.CompilerParams(has_side_effects=_EFFECT),
    )(gi, go, pi, po, send_sems, recv_sems, after)


def _sum_into(buf, g, parts, where2, layer, row_blocks, name):
    depth, r2, c = buf.shape
    r = r2 // 2
    tr, tc = _tile(r, 256), _tile(c, 1024)
    nr, nc = r // tr, c // tc
    col_blocks = (g.shape[1] // c) > 1

    def body(w_ref, buf_ref, g_ref, p_ref, o_ref):
        acc = g_ref[...].astype(F32)
        for e in range(N_DEV - 1):
            acc = acc + p_ref[e].astype(F32)
        o_ref[...] = acc

    return pl.pallas_call(
        body, name=name,
        grid_spec=pltpu.PrefetchScalarGridSpec(
            num_scalar_prefetch=1, grid=(nr, nc),
            in_specs=[_ANY,
                      pl.BlockSpec((tr, tc), lambda i, j, w: (row_blocks(w, nr) + i, (w[1] * nc if col_blocks else 0) + j)),
                      pl.BlockSpec((N_DEV - 1, tr, tc), lambda i, j, w: (0, i, j))],
            out_specs=pl.BlockSpec((None, tr, tc), lambda i, j, w: (layer, w[0] * nr + i, j))),
        out_shape=jax.ShapeDtypeStruct(buf.shape, F32),
        input_output_aliases={1: 0},
        compiler_params=_params(("parallel", "parallel")),
    )(where2, buf, g, parts)


def _exchange_halves(b_in, b_out, name):
    depth, r_in = b_in.shape[:2]
    r_out = b_out.shape[1]

    def body(bi_ref, bo_ref, oi_ref, oo_ref, send_sems, recv_sems):
        mx, my, mc = lax.axis_index("x"), lax.axis_index("y"), lax.axis_index("c")
        sib = (mx, my, 1 - mc)

        def half(ref, l, rows, which):
            return ref.at[l, pl.ds(pl.multiple_of(which * (rows // 2), 8), rows // 2), :]

        copies = []
        for l in range(depth):
            for t, (src, dst, rows) in enumerate(((bi_ref, oi_ref, r_in), (bo_ref, oo_ref, r_out))):
                idx = 2 * l + t
                kw = dict(send_sem=send_sems.at[idx], recv_sem=recv_sems.at[idx], device_id=sib, device_id_type=MESH)
                cp = pltpu.make_async_remote_copy(src_ref=half(src, l, rows, mc), dst_ref=half(dst, l, rows, mc), **kw)
                cp.start()
                copies.append((cp, pltpu.make_async_remote_copy(src_ref=half(src, l, rows, mc),
                                                                dst_ref=half(dst, l, rows, 1 - mc), **kw)))
        for cp, landed in copies:
            landed.wait_recv()
        for cp, landed in copies:
            cp.wait_send()

    ns = 2 * depth
    return pl.pallas_call(
        body, name=name,
        out_shape=[jax.ShapeDtypeStruct(b_in.shape, F32), jax.ShapeDtypeStruct(b_out.shape, F32)],
        in_specs=[_ANY, _ANY], out_specs=[_ANY, _ANY],
        input_output_aliases={0: 0, 1: 1},
        scratch_shapes=[pltpu.SemaphoreType.DMA((ns,)), pltpu.SemaphoreType.DMA((ns,))],
    )(b_in, b_out)


def _rows8(v):
    return jnp.pad(v, ((0, 8 - v.shape[0]), (0, 0)))


def kernel(x, c, norm_gain, w_ada, b_ada, w_in, w_out, ret_decay_logit_f, ret_decay_logit_b, final_gain, loss_target, m_norm_gain, m_w_ada, m_b_ada, m_w_in, m_w_out, m_ret_decay_logit_f, m_ret_decay_logit_b, m_final_gain, v_norm_gain, v_w_ada, v_b_ada, v_w_in, v_w_out, v_ret_decay_logit_f, v_ret_decay_logit_b, v_final_gain):
    cfg = _Cfg()
    depth, d = norm_gain.shape
    mx, my, mc = lax.axis_index("x"), lax.axis_index("y"), lax.axis_index("c")
    me = 4 * mx + 2 * my + mc
    chip = 2 * mx + my
    x0 = x[0]
    tgt = loss_target[0]
    ada_cols = w_ada.shape[2]

    c_all = _all_gather_small(_rows8(c), "gather_c")[:, 0, :]
    cact = _silu(c_all)
    mod_part = _ada_fwd(jnp.pad(cact, ((0, 8), (0, 0))).astype(BF16), w_ada, "ada_fwd")
    mod_all = _all_gather_small(mod_part.reshape(depth * 16, ada_cols), "gather_mod")
    mod_all = mod_all.reshape(N_CHIP, 2, depth, 16, ada_cols)[:, 0]
    mod_mine = lax.dynamic_index_in_dim(mod_all, me, axis=2, keepdims=False)
    mod = jnp.transpose(mod_mine, (1, 0, 2)).reshape(depth, 3, d)
    bias = b_ada.reshape(depth, 3, d)

    chip1 = jnp.reshape(chip, (1,)).astype(jnp.int32)
    where2 = jnp.stack([mc, chip]).astype(jnp.int32)

    def start_gather(l, dep):
        return _gather_start(_place_w_in(w_in, l, chip1, f"place_w_in_{l}"),
                             _place_w_out(w_out, l, chip1, f"place_w_out_{l}"), dep, f"gather_start_{l}")

    slopes = jnp.exp2(-8.0 * (jnp.arange(cfg.ha, dtype=F32) + 1.0) / cfg.ha)
    lg_f = jax.nn.log_sigmoid(ret_decay_logit_f)
    lg_b = jax.nn.log_sigmoid(ret_decay_logit_b)

    saved = []
    w_full = []
    h_x = x0
    pending = start_gather(0, c)
    for l in range(depth):
        send_sems, recv_sems, fi, fo, _ = pending
        w_in_l, w_out_l = _gather_wait(send_sems, recv_sems, fi, fo, mod if l == 0 else h_x, f"gather_wait_{l}")
        w_full.append((w_in_l, w_out_l))
        g8 = _rows8(norm_gain[l:l + 1])
        if l + 1 < depth:
            pending = start_gather(l + 1, w_in_l)
            g8 = g8 + pending[4][0:1, 0:1]
        mod3, b3 = _rows8(mod[l]), _rows8(bias[l])
        hb = _norm_mod_fwd(h_x, g8, mod3, b3, f"norm_mod_fwd_{l}")
        proj = _matmul(hb, w_in_l, "nn", BF16, f"in_proj_{l}")
        oa, lse = _attn_fwd(cfg, proj, slopes, f"attn_fwd_{l}")
        lgs = jnp.stack([lg_f[l], lg_b[l]])
        oret = _ret_fwd(cfg, proj, lgs, f"ret_fwd_{l}")
        y = _gate_fwd(cfg, oa, proj, oret, f"gate_fwd_{l}")
        x_next, out = _out_proj_fwd(y, w_out_l, h_x, mod3, b3, f"out_proj_{l}")
        saved.append((h_x, hb, proj, oret, y, oa, lse, out, g8, mod3, b3, lgs))
        h_x = x_next

    dx, loss8, fin_acc = _final_loss(h_x, tgt, _rows8(final_gain[None]), "final_loss")

    landed = [None] * depth
    d_mod, d_gain, d_lg = [None] * depth, [None] * depth, [None] * depth
    in_flight = None
    for l in reversed(range(depth)):
        x_l, hb, proj, oret, y, oa, lse, out, g8, mod3, b3, lgs = saved[l]
        w_in_l, w_out_l = w_full[l]
        if in_flight is not None:
            mod3 = mod3 + in_flight[-1][0:1, 0:1]
        douts, gate_acc = _out_proj_bwd_prep(dx, out, mod3, b3, f"out_proj_bwd_prep_{l}")
        dy = _matmul(douts, w_out_l, "nt", BF16, f"out_proj_dy_{l}")
        g_out_l = _matmul(y, douts, "tn", BF16, f"out_proj_dw_{l}", tk=1024)
        do_a, dza, do_r, dzr = _gate_bwd(cfg, dy, oa, proj, oret, f"gate_bwd_{l}")
        dq_a, dk_a, dv_a = _attn_bwd(cfg, proj, slopes, do_a, oa, lse, f"attn_bwd_{l}")
        rdq, rdk, rdv, dlg = _ret_bwd(cfg, proj, lgs, do_r, f"ret_bwd_{l}")
        dproj = _assemble_dproj(cfg, dq_a, dk_a, dv_a, dza, rdq, rdk, rdv, dzr, f"assemble_dproj_{l}")
        dh = _matmul(dproj, w_in_l, "nt", F32, f"in_proj_dh_{l}", tk=1792)
        g_in_l = _matmul(hb, dproj, "tn", BF16, f"in_proj_dw_{l}", tk=1024)
        dx, nm_acc = _norm_mod_bwd(dh, x_l, dx, g8, mod3, b3, f"norm_mod_bwd_{l}")
        d_mod[l] = jnp.concatenate([nm_acc[0], nm_acc[1], gate_acc[0]])
        d_gain[l] = nm_acc[2]
        d_lg[l] = dlg[:, 0:2, 0]
        if in_flight is not None:
            landed[l + 1] = _scatter_wait(*in_flight[:-1], dx, f"scatter_wait_{l + 1}")
        in_flight = _scatter_start(g_in_l, g_out_l, f"scatter_start_{l}")
    landed[0] = _scatter_wait(*in_flight[:-1], dx, "scatter_wait_0")

    gw_in = lax.empty(w_in.shape, F32)
    gw_out = lax.empty(w_out.shape, F32)
    for l in range(depth):
        gi, go, pi, po = landed[l]
        gw_in = _sum_into(gw_in, gi, pi, where2, l, lambda w, nr: w[0] * nr, f"sum_w_in_{l}")
        gw_out = _sum_into(gw_out, go, po, where2, l, lambda w, nr: (2 * w[1] + w[0]) * nr, f"sum_w_out_{l}")
    gw_in, gw_out = _exchange_halves(gw_in, gw_out, "exchange_halves")
    grad_w_in, delta_w_in, new_m_w_in, new_v_w_in = _adamw_big(w_in, gw_in, m_w_in, v_w_in, "adamw_w_in")
    grad_w_out, delta_w_out, new_m_w_out, new_v_w_out = _adamw_big(w_out, gw_out, m_w_out, v_w_out, "adamw_w_out")

    dmod_mine = jnp.stack(d_mod)
    dmod_gathered = _all_gather_small(_rows8(dmod_mine), "gather_dmod")
    dmod_all = dmod_gathered[:, :depth, :]
    grad_b_ada = _sum_gathered(dmod_gathered, "sum_b_ada")[:depth]
    dmod_cols = lax.dynamic_slice_in_dim(dmod_all, chip * ada_cols, ada_cols, axis=2)
    dmod128 = jnp.pad(jnp.transpose(dmod_cols, (1, 0, 2)), ((0, 0), (0, 120), (0, 0))).astype(BF16)
    cact128 = jnp.pad(cact, ((0, 120), (0, 0))).astype(BF16)
    grad_w_ada, delta_w_ada, new_m_w_ada, new_v_w_ada = _adamw_ada(w_ada, m_w_ada, v_w_ada, cact128, dmod128,
                                                                  "adamw_w_ada")

    dlg_all = jnp.stack(d_lg)
    sig_f = jax.nn.sigmoid(-ret_decay_logit_f)
    sig_b = jax.nn.sigmoid(-ret_decay_logit_b)
    nlg = depth * cfg.hr
    pack = jnp.zeros((8, d), F32)
    for l in range(depth):
        pack = pack.at[l].set(d_gain[l])
    pack = pack.at[depth].set(fin_acc[0])
    pack = pack.at[depth + 1, 0].set(loss8[0, 0])
    pack = pack.at[depth + 1, LANES:LANES + nlg].set((dlg_all[:, :, 0] * sig_f).reshape(-1))
    pack = pack.at[depth + 1, 2 * LANES:2 * LANES + nlg].set((dlg_all[:, :, 1] * sig_b).reshape(-1))
    tot = _sum_gathered(_all_gather_small(pack, "gather_small"), "sum_small")
    grad_norm_gain = tot[:depth]
    grad_final_gain = tot[depth]
    loss = tot[depth + 1, 0]
    grad_lf = tot[depth + 1, LANES:LANES + nlg].reshape(depth, cfg.hr)
    grad_lb = tot[depth + 1, 2 * LANES:2 * LANES + nlg].reshape(depth, cfg.hr)

    d_ng, m_ng, v_ng = _adamw_small(norm_gain, grad_norm_gain, m_norm_gain, v_norm_gain, "adamw_norm_gain")
    d_ba, m_ba, v_ba = _adamw_small(b_ada, grad_b_ada, m_b_ada, v_b_ada, "adamw_b_ada")
    d_lf, m_lf, v_lf = _adamw_small(ret_decay_logit_f, grad_lf, m_ret_decay_logit_f, v_ret_decay_logit_f, "adamw_lf")
    d_lb, m_lb, v_lb = _adamw_small(ret_decay_logit_b, grad_lb, m_ret_decay_logit_b, v_ret_decay_logit_b, "adamw_lb")
    d_fg, m_fg, v_fg = _adamw_small(final_gain[None], grad_final_gain[None], m_final_gain[None], v_final_gain[None],
                                    "adamw_final_gain")

    return (loss, dx[None],
            grad_norm_gain, grad_w_ada, grad_b_ada, grad_w_in, grad_w_out, grad_lf, grad_lb, grad_final_gain,
            d_ng, delta_w_ada, d_ba, delta_w_in, delta_w_out, d_lf, d_lb, d_fg[0],
            m_ng, new_m_w_ada, m_ba, new_m_w_in, new_m_w_out, m_lf, m_lb, m_fg[0],
            v_ng, new_v_w_ada, v_ba, new_v_w_in, new_v_w_out, v_lf, v_lb, v_fg[0])
```

```python
import functools

import jax
import jax.numpy as jnp
from jax import lax
from jax.experimental import pallas as pl
from jax.experimental.pallas import tpu as pltpu

F32 = jnp.float32
BF16 = jnp.bfloat16

D_MODEL = 2048
SEQ = 4096
DEPTH = 4
HEAD_DIM = 128
DILATIONS = (1, 4, 16)
RADIUS = 64
N_HEADS_RET = 4
RET_QK = 128
RET_V = 256
RET_CHUNK = 256
NORM_EPS = 1e-6
MASK_VALUE = -1e30
N_DEV = 8
N_CHIP = 4
LANES = 128
VMEM_LIMIT = 56 * 1024 * 1024

ADAM_LR = 0.001
ADAM_B1 = 0.9
ADAM_B2 = 0.999
ADAM_EPS = 1e-08
ADAM_WD = 0.01
ADAM_STEP = 10

MESH = pl.DeviceIdType.MESH


class _Cfg:
    def __init__(self):
        self.d = D_MODEL
        self.s = SEQ
        self.aw = D_MODEL // 2
        self.ha = self.aw // HEAD_DIM
        self.rw = D_MODEL // 2
        self.hr = N_HEADS_RET
        self.rqk = self.hr * RET_QK
        self.f = 4 * self.aw + 2 * self.rqk + 2 * self.rw
        self.qa, self.ka, self.va, self.za = 0, self.aw, 2 * self.aw, 3 * self.aw
        self.qr = 4 * self.aw
        self.kr = self.qr + self.rqk
        self.vr = self.kr + self.rqk
        self.zr = self.vr + self.rw
        assert self.rw == self.hr * RET_V


def _tile(n, pref):
    t = min(n, pref)
    while n % t or t % LANES:
        t -= LANES
    return t


def _params(dims=None):
    return pltpu.CompilerParams(dimension_semantics=dims, vmem_limit_bytes=VMEM_LIMIT)


def _silu(z):
    return z * jax.nn.sigmoid(z)


def _dsilu(z):
    sg = jax.nn.sigmoid(z)
    return sg * (1.0 + z * (1.0 - sg))


_DN = {"nn": (((1,), (0,)), ((), ())), "nt": (((1,), (1,)), ((), ())), "tn": (((0,), (0,)), ((), ()))}


def _matmul(a, b, mode, out_dtype, name, tm=1024, tn=1024, tk=2048):
    if mode == "tn":
        kk, m = a.shape
    else:
        m, kk = a.shape
    n = b.shape[0] if mode == "nt" else b.shape[1]
    tm, tn, tk = _tile(m, tm), _tile(n, tn), _tile(kk, tk)
    nk = kk // tk
    a_spec = (pl.BlockSpec((tk, tm), lambda i, j, k: (k, i)) if mode == "tn"
              else pl.BlockSpec((tm, tk), lambda i, j, k: (i, k)))
    b_spec = (pl.BlockSpec((tn, tk), lambda i, j, k: (j, k)) if mode == "nt"
              else pl.BlockSpec((tk, tn), lambda i, j, k: (k, j)))
    dn = _DN[mode]

    def body(a_ref, b_ref, o_ref, acc_ref):
        k = pl.program_id(2)
        p = lax.dot_general(a_ref[...], b_ref[...], dn, preferred_element_type=F32)

        @pl.when(k == 0)
        def _():
            acc_ref[...] = p

        @pl.when(k > 0)
        def _():
            acc_ref[...] += p

        @pl.when(k == nk - 1)
        def _():
            o_ref[...] = acc_ref[...].astype(out_dtype)

    return pl.pallas_call(
        body, name=name, grid=(m // tm, n // tn, nk),
        in_specs=[a_spec, b_spec],
        out_specs=pl.BlockSpec((tm, tn), lambda i, j, k: (i, j)),
        out_shape=jax.ShapeDtypeStruct((m, n), out_dtype),
        scratch_shapes=[pltpu.VMEM((tm, tn), F32)],
        compiler_params=_params(("parallel", "parallel", "arbitrary")),
    )(a, b)


def _out_proj_fwd(y, w_out, x, mod3, b3, name):
    s, kk = y.shape
    d = w_out.shape[1]
    tm, tn = _tile(s, 256), d

    def body(y_ref, w_ref, x_ref, m_ref, b_ref, xn_ref, o_ref):
        out = jnp.dot(y_ref[...], w_ref[...], preferred_element_type=F32)
        gate = m_ref[2:3, :] + b_ref[2:3, :]
        xn_ref[...] = x_ref[...] + gate * out
        o_ref[...] = out.astype(BF16)

    vec = pl.BlockSpec((8, tn), lambda i, j: (0, j))
    return pl.pallas_call(
        body, name=name, grid=(s // tm, d // tn),
        in_specs=[pl.BlockSpec((tm, kk), lambda i, j: (i, 0)), pl.BlockSpec((kk, tn), lambda i, j: (0, j)),
                  pl.BlockSpec((tm, tn), lambda i, j: (i, j)), vec, vec],
        out_specs=[pl.BlockSpec((tm, tn), lambda i, j: (i, j)), pl.BlockSpec((tm, tn), lambda i, j: (i, j))],
        out_shape=[jax.ShapeDtypeStruct((s, d), F32), jax.ShapeDtypeStruct((s, d), BF16)],
        compiler_params=_params(("parallel", "parallel")),
    )(y, w_out, x, mod3, b3)


def _norm_mod_fwd(x, g8, mod3, b3, name):
    s, d = x.shape
    tr = _tile(s, 512)

    def body(x_ref, g_ref, m_ref, b_ref, h_ref):
        xv = x_ref[...]
        r = lax.rsqrt(jnp.mean(xv * xv, axis=-1, keepdims=True) + NORM_EPS)
        shift = m_ref[0:1, :] + b_ref[0:1, :]
        scale = m_ref[1:2, :] + b_ref[1:2, :]
        h_ref[...] = ((xv * r * g_ref[0:1, :]) * (1.0 + scale) + shift).astype(BF16)

    vec = pl.BlockSpec((8, d), lambda i: (0, 0))
    return pl.pallas_call(
        body, name=name, grid=(s // tr,),
        in_specs=[pl.BlockSpec((tr, d), lambda i: (i, 0)), vec, vec, vec],
        out_specs=pl.BlockSpec((tr, d), lambda i: (i, 0)),
        out_shape=jax.ShapeDtypeStruct((s, d), BF16),
        compiler_params=_params(("parallel",)),
    )(x, g8, mod3, b3)


def _norm_mod_bwd(dh, x, dx_next, g8, mod3, b3, name):
    s, d = x.shape
    tr = _tile(s, 256)

    def body(dh_ref, x_ref, dn_ref, g_ref, m_ref, b_ref, dx_ref, acc_ref):
        @pl.when(pl.program_id(0) == 0)
        def _():
            acc_ref[...] = jnp.zeros_like(acc_ref)

        xv = x_ref[...]
        dh_v = dh_ref[...]
        g = g_ref[0:1, :]
        r = lax.rsqrt(jnp.mean(xv * xv, axis=-1, keepdims=True) + NORM_EPS)
        xn = xv * r
        scale1 = 1.0 + m_ref[1:2, :] + b_ref[1:2, :]
        dhs = dh_v * scale1
        dxn = dhs * g
        dx_ref[...] = dn_ref[...] + r * (dxn - xn * jnp.mean(dxn * xn, axis=-1, keepdims=True))
        acc_ref[0:1, :] += jnp.sum(dh_v, axis=0, keepdims=True)
        acc_ref[1:2, :] += jnp.sum(dh_v * (xn * g), axis=0, keepdims=True)
        acc_ref[2:3, :] += jnp.sum(dhs * xn, axis=0, keepdims=True)

    vec = pl.BlockSpec((8, d), lambda i: (0, 0))
    row = pl.BlockSpec((tr, d), lambda i: (i, 0))
    return pl.pallas_call(
        body, name=name, grid=(s // tr,),
        in_specs=[row, row, row, vec, vec, vec],
        out_specs=[row, vec],
        out_shape=[jax.ShapeDtypeStruct((s, d), F32), jax.ShapeDtypeStruct((8, d), F32)],
        compiler_params=_params(("arbitrary",)),
    )(dh, x, dx_next, g8, mod3, b3)


def _final_loss(x, tgt, g8, name):
    s, d = x.shape
    tr = _tile(s, 256)

    def body(x_ref, t_ref, g_ref, dx_ref, loss_ref, acc_ref):
        @pl.when(pl.program_id(0) == 0)
        def _():
            acc_ref[...] = jnp.zeros_like(acc_ref)
            loss_ref[...] = jnp.zeros_like(loss_ref)

        xv = x_ref[...]
        g = g_ref[0:1, :]
        r = lax.rsqrt(jnp.mean(xv * xv, axis=-1, keepdims=True) + NORM_EPS)
        xn = xv * r
        err = xn * g - t_ref[...]
        loss_ref[...] += 0.5 * jnp.sum(jnp.mean(err * err, axis=-1, keepdims=True), axis=0, keepdims=True)
        dy = err * (1.0 / d)
        acc_ref[0:1, :] += jnp.sum(dy * xn, axis=0, keepdims=True)
        dxn = dy * g
        dx_ref[...] = r * (dxn - xn * jnp.mean(dxn * xn, axis=-1, keepdims=True))

    vec = pl.BlockSpec((8, d), lambda i: (0, 0))
    row = pl.BlockSpec((tr, d), lambda i: (i, 0))
    return pl.pallas_call(
        body, name=name, grid=(s // tr,),
        in_specs=[row, row, vec],
        out_specs=[row, pl.BlockSpec((8, LANES), lambda i: (0, 0)), vec],
        out_shape=[jax.ShapeDtypeStruct((s, d), F32), jax.ShapeDtypeStruct((8, LANES), F32),
                   jax.ShapeDtypeStruct((8, d), F32)],
        compiler_params=_params(("arbitrary",)),
    )(x, tgt, g8)


def _gate_fwd(cfg, oa, proj, oret, name):
    s = cfg.s
    aw, rw = cfg.aw, cfg.rw
    tr = _tile(s, 256)

    def body(oa_ref, za_ref, or_ref, zr_ref, y_ref):
        y_ref[:, 0:aw] = (oa_ref[...].astype(F32) * _silu(za_ref[...].astype(F32))).astype(BF16)
        for h in range(cfg.hr):
            cols = slice(h * RET_V, (h + 1) * RET_V)
            oh = or_ref[:, cols]
            rr = lax.rsqrt(jnp.mean(oh * oh, axis=-1, keepdims=True) + NORM_EPS)
            y_ref[:, aw + h * RET_V:aw + (h + 1) * RET_V] = (
                oh * rr * _silu(zr_ref[:, cols].astype(F32))).astype(BF16)

    ra = pl.BlockSpec((tr, aw), lambda i: (i, 0))
    return pl.pallas_call(
        body, name=name, grid=(s // tr,),
        in_specs=[ra,
                  pl.BlockSpec((tr, aw), lambda i: (i, cfg.za // aw)),
                  pl.BlockSpec((tr, rw), lambda i: (i, 0)),
                  pl.BlockSpec((tr, rw), lambda i: (i, cfg.zr // rw))],
        out_specs=pl.BlockSpec((tr, cfg.d), lambda i: (i, 0)),
        out_shape=jax.ShapeDtypeStruct((s, cfg.d), BF16),
        compiler_params=_params(("parallel",)),
    )(oa, proj, oret, proj)


def _gate_bwd(cfg, dy, oa, proj, oret, name):
    s = cfg.s
    aw, rw = cfg.aw, cfg.rw
    tr = _tile(s, 256)

    def body(dy_ref, oa_ref, za_ref, or_ref, zr_ref, do_ref, dza_ref, dor_ref, dzr_ref):
        dya = dy_ref[:, 0:aw].astype(F32)
        za = za_ref[...].astype(F32)
        do_ref[...] = (dya * _silu(za)).astype(BF16)
        dza_ref[...] = (dya * oa_ref[...].astype(F32) * _dsilu(za)).astype(BF16)
        for h in range(cfg.hr):
            cols = slice(h * RET_V, (h + 1) * RET_V)
            oh = or_ref[:, cols]
            zr = zr_ref[:, cols].astype(F32)
            dyr = dy_ref[:, aw + h * RET_V:aw + (h + 1) * RET_V].astype(F32)
            rr = lax.rsqrt(jnp.mean(oh * oh, axis=-1, keepdims=True) + NORM_EPS)
            yn = oh * rr
            dyn = dyr * _silu(zr)
            dzr_ref[:, cols] = (dyr * yn * _dsilu(zr)).astype(BF16)
            dor_ref[:, cols] = (rr * (dyn - yn * jnp.mean(dyn * yn, axis=-1, keepdims=True))).astype(BF16)

    ra = pl.BlockSpec((tr, aw), lambda i: (i, 0))
    rr_ = pl.BlockSpec((tr, rw), lambda i: (i, 0))
    return pl.pallas_call(
        body, name=name, grid=(s // tr,),
        in_specs=[pl.BlockSpec((tr, cfg.d), lambda i: (i, 0)), ra,
                  pl.BlockSpec((tr, aw), lambda i: (i, cfg.za // aw)), rr_,
                  pl.BlockSpec((tr, rw), lambda i: (i, cfg.zr // rw))],
        out_specs=[ra, ra, rr_, rr_],
        out_shape=[jax.ShapeDtypeStruct((s, aw), BF16), jax.ShapeDtypeStruct((s, aw), BF16),
                   jax.ShapeDtypeStruct((s, rw), BF16), jax.ShapeDtypeStruct((s, rw), BF16)],
        compiler_params=_params(("parallel",)),
    )(dy, oa, proj, oret, proj)


def _out_proj_bwd_prep(dxn, out, mod3, b3, name):
    s, d = dxn.shape
    tr = _tile(s, 512)

    def body(dx_ref, o_ref, m_ref, b_ref, do_ref, acc_ref):
        @pl.when(pl.program_id(0) == 0)
        def _():
            acc_ref[...] = jnp.zeros_like(acc_ref)

        dxv = dx_ref[...]
        gate = m_ref[2:3, :] + b_ref[2:3, :]
        do_ref[...] = (gate * dxv).astype(BF16)
        acc_ref[0:1, :] += jnp.sum(dxv * o_ref[...].astype(F32), axis=0, keepdims=True)

    vec = pl.BlockSpec((8, d), lambda i: (0, 0))
    row = pl.BlockSpec((tr, d), lambda i: (i, 0))
    return pl.pallas_call(
        body, name=name, grid=(s // tr,),
        in_specs=[row, row, vec, vec],
        out_specs=[row, vec],
        out_shape=[jax.ShapeDtypeStruct((s, d), BF16), jax.ShapeDtypeStruct((8, d), F32)],
        compiler_params=_params(("arbitrary",)),
    )(dxn, out, mod3, b3)


def _assemble_dproj(cfg, dq, dk, dv, dza, rdq, rdk, rdv, dzr, name):
    s = cfg.s
    aw, rw, rqk = cfg.aw, cfg.rw, cfg.rqk
    tr = _tile(s, 256)

    def body(dq_ref, dk_ref, dv_ref, dza_ref, rq, rk, rv, dzr_ref, o_ref):
        o_ref[:, cfg.qa:cfg.qa + aw] = dq_ref[...]
        o_ref[:, cfg.ka:cfg.ka + aw] = dk_ref[...]
        o_ref[:, cfg.va:cfg.va + aw] = dv_ref[...]
        o_ref[:, cfg.za:cfg.za + aw] = dza_ref[...]
        o_ref[:, cfg.qr:cfg.qr + rqk] = rq[...].astype(BF16)
        o_ref[:, cfg.kr:cfg.kr + rqk] = rk[...].astype(BF16)
        o_ref[:, cfg.vr:cfg.vr + rw] = rv[...].astype(BF16)
        o_ref[:, cfg.zr:cfg.zr + rw] = dzr_ref[...]

    ra = pl.BlockSpec((tr, aw), lambda i: (i, 0))
    rq_ = pl.BlockSpec((tr, rqk), lambda i: (i, 0))
    rv_ = pl.BlockSpec((tr, rw), lambda i: (i, 0))
    return pl.pallas_call(
        body, name=name, grid=(s // tr,),
        in_specs=[ra] * 4 + [rq_, rq_, rv_, rv_],
        out_specs=pl.BlockSpec((tr, cfg.f), lambda i: (i, 0)),
        out_shape=jax.ShapeDtypeStruct((s, cfg.f), BF16),
        compiler_params=_params(("parallel",)),
    )(dq, dk, dv, dza, rdq, rdk, rdv, dzr)


_COPY_ROWS = 512


def _attn_geometry(cfg, dil):
    sub = cfg.s // dil
    bq = min(128, sub)
    win = min(256, sub)
    return sub, bq, win


_N_SHIFTS = 3


def _attn_rows(dil, r, i, sub, bq, win):
    margin = (win - bq) // 2
    ws = jnp.clip(i * bq - margin, 0, sub - win)
    shift = (i * bq - ws) // margin if margin else 0
    if dil == 1:
        return (shift, pl.ds(pl.multiple_of(i * bq, bq), bq), pl.ds(pl.multiple_of(ws, 64), win))
    return (shift, pl.ds(r + i * (bq * dil), bq, stride=dil), pl.ds(r + ws * dil, win, stride=dil))


def _fill_bias_tables(cfg, bias_ref, slope):
    for pattern, dil in enumerate(DILATIONS):
        _, bq, win = _attn_geometry(cfg, dil)
        margin = (win - bq) // 2
        rel0 = lax.broadcasted_iota(jnp.int32, (bq, win), 1) - lax.broadcasted_iota(jnp.int32, (bq, win), 0)
        for shift in range(_N_SHIFTS if margin else 1):
            arel = jnp.abs(rel0 - shift * margin)
            bias_ref[pattern * _N_SHIFTS + shift, 0:bq, 0:win] = jnp.where(
                arel <= RADIUS, -(slope * dil) * arel.astype(F32), MASK_VALUE)


def _bias_scratch():
    return pltpu.VMEM((len(DILATIONS) * _N_SHIFTS, 128, 256), F32)


def _attn_scores(cfg, bias_ref, q, kw, pattern, shift):
    bq, win = q.shape[0], kw.shape[0]
    return (lax.dot_general(q, kw, _DN["nt"], preferred_element_type=F32)
            + bias_ref[pattern * _N_SHIFTS + shift, 0:bq, 0:win])


_ATTN_GROUP = 4


def _for_each_group(cfg, group):
    for pattern, dil in enumerate(DILATIONS):
        sub, bq, _ = _attn_geometry(cfg, dil)
        nblk = sub // bq
        per = min(_ATTN_GROUP, nblk)
        for r in range(dil):
            if nblk == per:
                group(pattern, dil, [(r, i) for i in range(nblk)])
            else:
                def step(g, carry, pattern=pattern, dil=dil, r=r, per=per):
                    group(pattern, dil, [(r, g * per + j) for j in range(per)])
                    return carry

                lax.fori_loop(0, nblk // per, step, 0)


def _attn_fwd(cfg, proj, slopes, name):
    s = cfg.s
    scale = HEAD_DIM ** -0.5

    def body(sl_ref, q_ref, k_ref, v_ref, o_ref, lse_ref, qf, kf, vf, acc, m_s, l_s, bias_s):
        slope = sl_ref[pl.program_id(0)]

        def to_f32(i, carry):
            rows = pl.ds(pl.multiple_of(i * _COPY_ROWS, _COPY_ROWS), _COPY_ROWS)
            qf[rows, :] = q_ref[rows, :].astype(F32) * scale
            kf[rows, :] = k_ref[rows, :].astype(F32)
            vf[rows, :] = v_ref[rows, :].astype(F32)
            return carry

        lax.fori_loop(0, s // _COPY_ROWS, to_f32, 0)

        _fill_bias_tables(cfg, bias_s, slope)

        def group(pattern, dil, blocks):
            sub, bq, win = _attn_geometry(cfg, dil)
            rep = win // HEAD_DIM
            first = pattern == 0
            work = []
            for r, i in blocks:
                shift, qrows, krows = _attn_rows(dil, r, i, sub, bq, win)
                old = None if first else (m_s[qrows, :], l_s[qrows, :], acc[qrows, :])
                work.append((shift, qrows, qf[qrows, :].astype(BF16), kf[krows, :].astype(BF16),
                             vf[krows, :].astype(BF16), old))
            new = []
            for shift, qrows, q, kw, vw, old in work:
                sc = _attn_scores(cfg, bias_s, q, kw, pattern, shift)
                m_blk = jnp.max(sc, axis=-1, keepdims=True)
                if first:
                    m_new = jnp.broadcast_to(m_blk, (bq, HEAD_DIM))
                    p = jnp.exp(sc - m_blk)
                    l_new = jnp.broadcast_to(jnp.sum(p, axis=-1, keepdims=True), (bq, HEAD_DIM))
                    a_new = jnp.dot(p.astype(BF16), vw, preferred_element_type=F32)
                else:
                    m_old, l_old, a_old = old
                    m_new = jnp.maximum(m_old, m_blk)
                    alpha = jnp.exp(m_old - m_new)
                    p = jnp.exp(sc - jnp.tile(m_new, (1, rep)))
                    l_new = alpha * l_old + jnp.sum(p, axis=-1, keepdims=True)
                    a_new = alpha * a_old + jnp.dot(p.astype(BF16), vw, preferred_element_type=F32)
                new.append((qrows, m_new, l_new, a_new))
            for qrows, m_new, l_new, a_new in new:
                m_s[qrows, :] = m_new
                l_s[qrows, :] = l_new
                acc[qrows, :] = a_new

        _for_each_group(cfg, group)

        def finish(i, carry):
            rows = pl.ds(pl.multiple_of(i * _COPY_ROWS, _COPY_ROWS), _COPY_ROWS)
            den = l_s[rows, :]
            o_ref[rows, :] = (acc[rows, :] / den).astype(BF16)
            lse_ref[rows, :] = m_s[rows, :] + jnp.log(den)
            return carry

        lax.fori_loop(0, s // _COPY_ROWS, finish, 0)

    def col(off):
        return pl.BlockSpec((s, HEAD_DIM), lambda h: (0, off // HEAD_DIM + h))

    head = pl.BlockSpec((s, HEAD_DIM), lambda h: (0, h))
    return pl.pallas_call(
        body, name=name, grid=(cfg.ha,),
        in_specs=[pl.BlockSpec(memory_space=pltpu.SMEM), col(cfg.qa), col(cfg.ka), col(cfg.va)],
        out_specs=[head, head],
        out_shape=[jax.ShapeDtypeStruct((s, cfg.aw), BF16), jax.ShapeDtypeStruct((s, cfg.aw), F32)],
        scratch_shapes=[pltpu.VMEM((s, HEAD_DIM), F32)] * 6 + [_bias_scratch()],
        compiler_params=_params(("parallel",)),
    )(slopes, proj, proj, proj)


def _attn_bwd(cfg, proj, slopes, do, oa, lse, name):
    s = cfg.s
    scale = HEAD_DIM ** -0.5

    def body(sl_ref, q_ref, k_ref, v_ref, do_ref, o_ref, lse_ref, dq_ref, dk_ref, dv_ref,
             qf, kf, vf, dof, dlt, dqa, dka, dva, bias_s):
        slope = sl_ref[pl.program_id(0)]

        def to_f32(i, carry):
            rows = pl.ds(pl.multiple_of(i * _COPY_ROWS, _COPY_ROWS), _COPY_ROWS)
            qf[rows, :] = q_ref[rows, :].astype(F32) * scale
            kf[rows, :] = k_ref[rows, :].astype(F32)
            vf[rows, :] = v_ref[rows, :].astype(F32)
            dov = do_ref[rows, :].astype(F32)
            dof[rows, :] = dov
            dlt[rows, :] = jnp.broadcast_to(jnp.sum(dov * o_ref[rows, :].astype(F32), axis=-1, keepdims=True),
                                            (_COPY_ROWS, HEAD_DIM))
            zero = jnp.zeros((_COPY_ROWS, HEAD_DIM), F32)
            dqa[rows, :] = zero
            dka[rows, :] = zero
            dva[rows, :] = zero
            return carry

        lax.fori_loop(0, s // _COPY_ROWS, to_f32, 0)

        _fill_bias_tables(cfg, bias_s, slope)

        def group(pattern, dil, blocks):
            sub, bq, win = _attn_geometry(cfg, dil)
            rep = win // HEAD_DIM
            work = []
            for r, i in blocks:
                shift, qrows, krows = _attn_rows(dil, r, i, sub, bq, win)
                work.append((shift, qrows, krows, qf[qrows, :].astype(BF16), kf[krows, :].astype(BF16),
                             vf[krows, :].astype(BF16), dof[qrows, :].astype(BF16),
                             lse_ref[qrows, :], dlt[qrows, :]))
            new = []
            for shift, qrows, krows, q, kw, vw, dob, lse_b, dlt_b in work:
                sc = _attn_scores(cfg, bias_s, q, kw, pattern, shift)
                p = jnp.exp(sc - jnp.tile(lse_b, (1, rep)))
                dp = lax.dot_general(dob, vw, _DN["nt"], preferred_element_type=F32)
                ds = (p * (dp - jnp.tile(dlt_b, (1, rep)))).astype(BF16)
                new.append((qrows, krows,
                            jnp.dot(ds, kw, preferred_element_type=F32) * scale,
                            lax.dot_general(ds, q, _DN["tn"], preferred_element_type=F32),
                            lax.dot_general(p.astype(BF16), dob, _DN["tn"], preferred_element_type=F32)))
            for qrows, krows, dq_b, dk_b, dv_b in new:
                dqa[qrows, :] += dq_b
                dka[krows, :] += dk_b
                dva[krows, :] += dv_b

        _for_each_group(cfg, group)

        def finish(i, carry):
            rows = pl.ds(pl.multiple_of(i * _COPY_ROWS, _COPY_ROWS), _COPY_ROWS)
            dq_ref[rows, :] = dqa[rows, :].astype(BF16)
            dk_ref[rows, :] = dka[rows, :].astype(BF16)
            dv_ref[rows, :] = dva[rows, :].astype(BF16)
            return carry

        lax.fori_loop(0, s // _COPY_ROWS, finish, 0)

    def col(off):
        return pl.BlockSpec((s, HEAD_DIM), lambda h: (0, off // HEAD_DIM + h))

    head = pl.BlockSpec((s, HEAD_DIM), lambda h: (0, h))
    shp = jax.ShapeDtypeStruct((s, cfg.aw), BF16)
    return pl.pallas_call(
        body, name=name, grid=(cfg.ha,),
        in_specs=[pl.BlockSpec(memory_space=pltpu.SMEM), col(cfg.qa), col(cfg.ka), col(cfg.va), head, head, head],
        out_specs=[head, head, head],
        out_shape=[shp, shp, shp],
        scratch_shapes=[pltpu.VMEM((s, HEAD_DIM), F32)] * 8 + [_bias_scratch()],
        compiler_params=_params(("parallel",)),
    )(slopes, proj, proj, proj, do, oa, lse)


def _decay_tables(lg, backward):
    c = RET_CHUNK
    a = lax.broadcasted_iota(jnp.int32, (c, c), 0)
    b = lax.broadcasted_iota(jnp.int32, (c, c), 1)
    idx = lax.broadcasted_iota(jnp.int32, (c, 1), 0).astype(F32)
    if backward:
        rel = (b - a).astype(F32)
        ex_xi = c - idx
        ex_zeta = idx
    else:
        rel = (a - b).astype(F32)
        ex_xi = idx + 1.0
        ex_zeta = c - 1.0 - idx
    relc = jnp.maximum(rel, 0.0)
    dm = jnp.where(rel >= 0, jnp.exp(relc * lg), 0.0)
    xi = jnp.exp(ex_xi * lg)
    zeta = jnp.exp(ex_zeta * lg)
    gch = jnp.exp(jnp.full((1, 1), c, F32) * lg)
    return relc, dm, xi, zeta, ex_xi, ex_zeta, gch


def _ret_fwd(cfg, proj, lgs, name):
    s = cfg.s
    c = RET_CHUNK
    n = s // c
    kscale = RET_QK ** -0.5

    def body(lg_ref, q_ref, k_ref, v_ref, o_ref, st_ref):
        h = pl.program_id(0)
        tabs = [_decay_tables(lg_ref[dirn, h], dirn == 1) for dirn in range(2)]
        st_ref[...] = jnp.zeros_like(st_ref)
        o_ref[...] = jnp.zeros_like(o_ref)

        def step(t, carry):
            for dirn in range(2):
                _, dm, xi, zeta, _, _, gch = tabs[dirn]
                i = (n - 1 - t) if dirn == 1 else t
                rows = pl.ds(pl.multiple_of(i * c, c), c)
                qi = q_ref[rows, :]
                ks = k_ref[rows, :].astype(F32) * kscale
                vi = v_ref[rows, :]
                inner = lax.dot_general(qi, ks.astype(BF16), _DN["nt"], preferred_element_type=F32) * dm
                st = st_ref[dirn]
                o_ref[rows, :] += (jnp.dot(inner.astype(BF16), vi, preferred_element_type=F32)
                                   + jnp.dot(qi, st.astype(BF16), preferred_element_type=F32) * xi)
                st_ref[dirn] = st * gch + lax.dot_general((ks * zeta).astype(BF16), vi, _DN["tn"],
                                                          preferred_element_type=F32)
            return carry

        lax.fori_loop(0, n, step, 0, unroll=2)

    return pl.pallas_call(
        body, name=name, grid=(cfg.hr,),
        in_specs=[pl.BlockSpec(memory_space=pltpu.SMEM),
                  pl.BlockSpec((s, RET_QK), lambda h: (0, cfg.qr // RET_QK + h)),
                  pl.BlockSpec((s, RET_QK), lambda h: (0, cfg.kr // RET_QK + h)),
                  pl.BlockSpec((s, RET_V), lambda h: (0, cfg.vr // RET_V + h))],
        out_specs=pl.BlockSpec((s, RET_V), lambda h: (0, h)),
        out_shape=jax.ShapeDtypeStruct((s, cfg.rw), F32),
        scratch_shapes=[pltpu.VMEM((2, RET_QK, RET_V), F32)],
        compiler_params=_params(("parallel",)),
    )(lgs, proj, proj, proj)


def _ret_bwd(cfg, proj, lgs, do, name):
    s = cfg.s
    c = RET_CHUNK
    n = s // c
    kscale = RET_QK ** -0.5

    def body(lg_ref, q_ref, k_ref, v_ref, do_ref, dq_ref, dk_ref, dv_ref, dlg_ref, states, t_ref,
             e_dm, e_xi, e_zeta, e_g):
        h = pl.program_id(0)
        tabs = [_decay_tables(lg_ref[dirn, h], dirn == 1) for dirn in range(2)]
        dlg_ref[...] = jnp.zeros_like(dlg_ref)
        dq_ref[...] = jnp.zeros_like(dq_ref)
        dk_ref[...] = jnp.zeros_like(dk_ref)
        dv_ref[...] = jnp.zeros_like(dv_ref)

        def chunk_rows(dirn, t):
            i = (n - 1 - t) if dirn == 1 else t
            return pl.ds(pl.multiple_of(i * c, c), c)

        t_ref[...] = jnp.zeros_like(t_ref)

        def fwd_step(t, carry):
            for dirn in range(2):
                _, _, _, zeta, _, _, gch = tabs[dirn]
                rows = chunk_rows(dirn, t)
                st = t_ref[dirn]
                states[dirn, t] = st
                ks = k_ref[rows, :].astype(F32) * kscale
                t_ref[dirn] = st * gch + lax.dot_general((ks * zeta).astype(BF16), v_ref[rows, :], _DN["tn"],
                                                         preferred_element_type=F32)
            return carry

        lax.fori_loop(0, n, fwd_step, 0, unroll=2)
        t_ref[...] = jnp.zeros_like(t_ref)

        for ref in (e_dm, e_xi, e_zeta, e_g):
            ref[...] = jnp.zeros_like(ref)

        def bwd_step(u, carry):
            t = n - 1 - u
            for dirn in range(2):
                relc, dm, xi, zeta, ex_xi, ex_zeta, gch = tabs[dirn]
                rows = chunk_rows(dirn, t)
                qi = q_ref[rows, :]
                ks = k_ref[rows, :].astype(F32) * kscale
                ksb = ks.astype(BF16)
                vi = v_ref[rows, :]
                doi = do_ref[rows, :]
                sn_f = states[dirn, t]
                sn = sn_f.astype(BF16)
                tt = t_ref[dirn]
                ttb = tt.astype(BF16)
                a_mat = lax.dot_general(qi, ksb, _DN["nt"], preferred_element_type=F32) * dm
                dov = lax.dot_general(doi, vi, _DN["nt"], preferred_element_type=F32)
                b_mat = (dov * dm).astype(BF16)
                kz = (ks * zeta).astype(BF16)
                d_v = (jnp.dot(kz, ttb, preferred_element_type=F32)
                       + lax.dot_general(a_mat.astype(BF16), doi, _DN["tn"], preferred_element_type=F32))
                dk_inter = lax.dot_general(vi, ttb, _DN["nt"], preferred_element_type=F32) * zeta
                d_k = lax.dot_general(b_mat, qi, _DN["tn"], preferred_element_type=F32) + dk_inter
                o_inter = jnp.dot(qi, sn, preferred_element_type=F32) * xi
                d_q = (jnp.dot(b_mat, ksb, preferred_element_type=F32)
                       + lax.dot_general(doi, sn, _DN["nt"], preferred_element_type=F32) * xi)
                e_dm[dirn] += relc * a_mat * dov
                e_xi[dirn] += ex_xi * (doi.astype(F32) * o_inter)
                e_zeta[dirn] += ex_zeta * (ks * dk_inter)
                e_g[dirn] += tt * sn_f
                t_ref[dirn] = tt * gch + lax.dot_general((qi.astype(F32) * xi).astype(BF16), doi, _DN["tn"],
                                                         preferred_element_type=F32)
                dq_ref[rows, :] += d_q
                dk_ref[rows, :] += d_k * kscale
                dv_ref[rows, :] += d_v
            return carry

        lax.fori_loop(0, n, bwd_step, 0, unroll=2)
        for dirn in range(2):
            total = (jnp.sum(e_dm[dirn], keepdims=True) + jnp.sum(e_xi[dirn], keepdims=True)
                     + jnp.sum(e_zeta[dirn], keepdims=True) + (c * tabs[dirn][6]) * jnp.sum(e_g[dirn], keepdims=True))
            dlg_ref[0, dirn:dirn + 1, :] = jnp.broadcast_to(total, (1, LANES))

    return pl.pallas_call(
        body, name=name, grid=(cfg.hr,),
        in_specs=[pl.BlockSpec(memory_space=pltpu.SMEM),
                  pl.BlockSpec((s, RET_QK), lambda h: (0, cfg.qr // RET_QK + h)),
                  pl.BlockSpec((s, RET_QK), lambda h: (0, cfg.kr // RET_QK + h)),
                  pl.BlockSpec((s, RET_V), lambda h: (0, cfg.vr // RET_V + h)),
                  pl.BlockSpec((s, RET_V), lambda h: (0, h))],
        out_specs=[pl.BlockSpec((s, RET_QK), lambda h: (0, h)), pl.BlockSpec((s, RET_QK), lambda h: (0, h)),
                   pl.BlockSpec((s, RET_V), lambda h: (0, h)), pl.BlockSpec((1, 8, LANES), lambda h: (h, 0, 0))],
        out_shape=[jax.ShapeDtypeStruct((s, cfg.rqk), F32), jax.ShapeDtypeStruct((s, cfg.rqk), F32),
                   jax.ShapeDtypeStruct((s, cfg.rw), F32), jax.ShapeDtypeStruct((cfg.hr, 8, LANES), F32)],
        scratch_shapes=[pltpu.VMEM((2, n, RET_QK, RET_V), F32), pltpu.VMEM((2, RET_QK, RET_V), F32),
                        pltpu.VMEM((2, c, c), F32), pltpu.VMEM((2, c, RET_V), F32),
                        pltpu.VMEM((2, c, RET_QK), F32), pltpu.VMEM((2, RET_QK, RET_V), F32)],
        compiler_params=_params(("parallel",)),
    )(lgs, proj, proj, proj, do)


def _ada_fwd(cact16, w_ada, name):
    depth, d, n = w_ada.shape
    tn = _tile(n, 768)

    def body(c_ref, w_ref, o_ref):
        o_ref[0] = jnp.dot(c_ref[...], w_ref[0].astype(BF16), preferred_element_type=F32)

    return pl.pallas_call(
        body, name=name, grid=(depth, n // tn),
        in_specs=[pl.BlockSpec((16, d), lambda l, j: (0, 0)), pl.BlockSpec((1, d, tn), lambda l, j: (l, 0, j))],
        out_specs=pl.BlockSpec((1, 16, tn), lambda l, j: (l, 0, j)),
        out_shape=jax.ShapeDtypeStruct((depth, 16, n), F32),
        compiler_params=_params(("parallel", "parallel")),
    )(cact16, w_ada)


def _adam_math(w, g, m, v):
    m2 = ADAM_B1 * m + (1.0 - ADAM_B1) * g
    v2 = ADAM_B2 * v + (1.0 - ADAM_B2) * (g * g)
    m_hat = m2 / (1.0 - ADAM_B1 ** ADAM_STEP)
    v_hat = v2 / (1.0 - ADAM_B2 ** ADAM_STEP)
    delta = -ADAM_LR * (m_hat / (jnp.sqrt(v_hat) + ADAM_EPS) + ADAM_WD * w)
    return delta, m2, v2


def _adamw_big(w, g, m, v, first, count, prev, name):
    _, r, c = w.shape
    tr, tc = _tile(r, 512), _tile(c, 1024)

    def body(w_ref, g_ref, m_ref, v_ref, *rest):
        go_ref, d_ref, mo_ref, vo_ref = rest[-4:]
        gv = g_ref[...]
        delta, m2, v2 = _adam_math(w_ref[...], gv, m_ref[...], v_ref[...])
        go_ref[...] = gv
        d_ref[...] = delta
        mo_ref[...] = m2
        vo_ref[...] = v2

    spec = pl.BlockSpec((1, tr, tc), lambda l, i, j: (first + l, i, j))
    shp = jax.ShapeDtypeStruct(w.shape, F32)
    carried = [] if prev is None else list(prev)
    return pl.pallas_call(
        body, name=name, grid=(count, r // tr, c // tc),
        in_specs=[spec] * 4 + [_ANY] * len(carried), out_specs=[spec] * 4, out_shape=[shp] * 4,
        input_output_aliases={4 + k: k for k in range(len(carried))},
        compiler_params=_params(("parallel", "parallel", "parallel")),
    )(w, g, m, v, *carried)


def _adamw_ada(w, m, v, cact128, dmod128, name):
    depth, r, c = w.shape
    tr, tc = _tile(r, 512), _tile(c, 768)

    def body(w_ref, m_ref, v_ref, c_ref, dm_ref, go_ref, d_ref, mo_ref, vo_ref):
        gv = lax.dot_general(c_ref[...], dm_ref[0], _DN["tn"], preferred_element_type=F32)
        delta, m2, v2 = _adam_math(w_ref[0], gv, m_ref[0], v_ref[0])
        go_ref[0] = gv
        d_ref[0] = delta
        mo_ref[0] = m2
        vo_ref[0] = v2

    spec = pl.BlockSpec((1, tr, tc), lambda l, i, j: (l, i, j))
    shp = jax.ShapeDtypeStruct(w.shape, F32)
    return pl.pallas_call(
        body, name=name, grid=(depth, r // tr, c // tc),
        in_specs=[spec] * 3 + [pl.BlockSpec((128, tr), lambda l, i, j: (0, i)),
                               pl.BlockSpec((1, 128, tc), lambda l, i, j: (l, 0, j))],
        out_specs=[spec] * 4, out_shape=[shp] * 4,
        compiler_params=_params(("parallel", "parallel", "parallel")),
    )(w, m, v, cact128, dmod128)


def _adamw_small(w, g, m, v, name):
    def body(w_ref, g_ref, m_ref, v_ref, d_ref, mo_ref, vo_ref):
        delta, m2, v2 = _adam_math(w_ref[...], g_ref[...], m_ref[...], v_ref[...])
        d_ref[...] = delta
        mo_ref[...] = m2
        vo_ref[...] = v2

    shp = jax.ShapeDtypeStruct(w.shape, F32)
    return pl.pallas_call(body, name=name, out_shape=[shp] * 3)(w, g, m, v)


def _sum_gathered(parts, name):
    nd, r, c = parts.shape

    def body(p_ref, o_ref):
        acc = p_ref[0]
        for e in range(1, nd):
            acc = acc + p_ref[e]
        o_ref[...] = acc

    return pl.pallas_call(body, name=name, out_shape=jax.ShapeDtypeStruct((r, c), F32))(parts)


def _flip(v, bit):
    return 1 - v if bit else v


def _all_gather_small(x, name):
    r, c = x.shape

    def body(x_ref, out_ref, send_sems, recv_sems, local_sem):
        mx, my, mc = lax.axis_index("x"), lax.axis_index("y"), lax.axis_index("c")
        me = 4 * mx + 2 * my + mc
        mine = pltpu.make_async_copy(x_ref, out_ref.at[me], local_sem)
        mine.start()
        sends = []
        for k in range(1, N_DEV):
            peer = (_flip(mx, k & 4), _flip(my, k & 2), _flip(mc, k & 1))
            cp = pltpu.make_async_remote_copy(src_ref=x_ref, dst_ref=out_ref.at[me], send_sem=send_sems.at[k - 1],
                                              recv_sem=recv_sems.at[k - 1], device_id=peer, device_id_type=MESH)
            cp.start()
            sends.append(cp)
        for k in range(1, N_DEV):
            peer = (_flip(mx, k & 4), _flip(my, k & 2), _flip(mc, k & 1))
            src = 4 * peer[0] + 2 * peer[1] + peer[2]
            pltpu.make_async_remote_copy(src_ref=x_ref, dst_ref=out_ref.at[src], send_sem=send_sems.at[k - 1],
                                         recv_sem=recv_sems.at[k - 1], device_id=peer,
                                         device_id_type=MESH).wait_recv()
        for cp in sends:
            cp.wait_send()
        mine.wait()

    return pl.pallas_call(
        body, name=name,
        out_shape=jax.ShapeDtypeStruct((N_DEV, r, c), x.dtype),
        in_specs=[pl.BlockSpec(memory_space=pltpu.VMEM)],
        out_specs=pl.BlockSpec(memory_space=pltpu.VMEM),
        scratch_shapes=[pltpu.SemaphoreType.DMA((N_DEV - 1,)), pltpu.SemaphoreType.DMA((N_DEV - 1,)),
                        pltpu.SemaphoreType.DMA],
        compiler_params=pltpu.CompilerParams(vmem_limit_bytes=VMEM_LIMIT),
    )(x)


_HBM = pl.BlockSpec(memory_space=pltpu.HBM)
_SEM = pl.BlockSpec(memory_space=pltpu.SEMAPHORE)
_ANY = pl.BlockSpec(memory_space=pl.ANY)
_EFFECT = pltpu.SideEffectType.DATAFLOW_SIDE_EFFECTING


def _in_hbm(a):
    return pltpu.with_memory_space_constraint(a, pltpu.HBM)


def _place_w_in(w, layer, chip1, name):
    _, d, fc = w.shape
    tr = _tile(d, 512)

    def body(c_ref, w_ref, o_ref):
        o_ref[...] = w_ref[...].astype(BF16)

    return pl.pallas_call(
        body, name=name,
        grid_spec=pltpu.PrefetchScalarGridSpec(
            num_scalar_prefetch=1, grid=(d // tr,),
            in_specs=[pl.BlockSpec((None, tr, fc), lambda i, c: (layer, i, 0))],
            out_specs=pl.BlockSpec((tr, fc), lambda i, c: (i, c[0]))),
        out_shape=jax.ShapeDtypeStruct((d, N_CHIP * fc), BF16),
        compiler_params=_params(("parallel",)),
    )(chip1, w)


def _place_w_out(w, layer, chip1, name):
    _, rc, dd = w.shape
    tc = _tile(dd, 1024)

    def body(c_ref, w_ref, o_ref):
        o_ref[...] = w_ref[...].astype(BF16)

    return pl.pallas_call(
        body, name=name,
        grid_spec=pltpu.PrefetchScalarGridSpec(
            num_scalar_prefetch=1, grid=(dd // tc,),
            in_specs=[pl.BlockSpec((None, rc, tc), lambda j, c: (layer, 0, j))],
            out_specs=pl.BlockSpec((rc, tc), lambda j, c: (c[0], j))),
        out_shape=jax.ShapeDtypeStruct((N_CHIP * rc, dd), BF16),
        compiler_params=_params(("parallel",)),
    )(chip1, w)


def _weight_region(ref, tensor, chip):
    if tensor == 0:
        fc = ref.shape[1] // N_CHIP
        return ref.at[:, pl.ds(pl.multiple_of(chip * fc, LANES), fc)]
    rc = ref.shape[0] // N_CHIP
    return ref.at[pl.ds(pl.multiple_of(chip * rc, 8), rc), :]


def _gather_copies(ref, tensor, send_sems, recv_sems, landing):
    mx, my, mc = lax.axis_index("x"), lax.axis_index("y"), lax.axis_index("c")
    mine = _weight_region(ref, tensor, 2 * mx + my)
    copies = []
    for j in range(1, N_CHIP):
        peer = (_flip(mx, j & 2), _flip(my, j & 1), mc)
        dst = _weight_region(ref, tensor, 2 * peer[0] + peer[1]) if landing else mine
        idx = 2 * (j - 1) + tensor
        copies.append(pltpu.make_async_remote_copy(src_ref=mine, dst_ref=dst, send_sem=send_sems.at[idx],
                                                   recv_sem=recv_sems.at[idx], device_id=peer, device_id_type=MESH))
    return copies


def _gather_start(fi, fo, dep, name):
    ns = 2 * (N_CHIP - 1)

    def body(fi_ref, fo_ref, dep_ref, send_sems, recv_sems, fi_thru, fo_thru, token):
        for tensor, ref in enumerate((fi_ref, fo_ref)):
            for cp in _gather_copies(ref, tensor, send_sems, recv_sems, landing=False):
                cp.start()
        token[...] = jnp.zeros_like(token)

    return pl.pallas_call(
        body, name=name,
        out_shape=(pltpu.SemaphoreType.DMA((ns,)), pltpu.SemaphoreType.DMA((ns,)),
                   pltpu.HBM(fi.shape, fi.dtype), pltpu.HBM(fo.shape, fo.dtype),
                   jax.ShapeDtypeStruct((8, LANES), F32)),
        in_specs=(_HBM, _HBM, _ANY),
        out_specs=(_SEM, _SEM, _HBM, _HBM, pl.BlockSpec(memory_space=pltpu.VMEM)),
        input_output_aliases={0: 2, 1: 3},
        compiler_params=pltpu.CompilerParams(has_side_effects=_EFFECT),
    )(_in_hbm(fi), _in_hbm(fo), dep)


def _gather_wait(send_sems, recv_sems, buf, tensor, after, name):
    def body(buf_ref, send_sems, recv_sems, after_ref, buf_out):
        for cp in _gather_copies(buf_ref, tensor, send_sems, recv_sems, landing=True):
            cp.wait_send()
            cp.wait_recv()

    return pl.pallas_call(
        body, name=name,
        out_shape=pltpu.HBM(buf.shape, buf.dtype),
        in_specs=(_HBM, _SEM, _SEM, _ANY), out_specs=_HBM,
        input_output_aliases={0: 0},
        compiler_params=pltpu.CompilerParams(has_side_effects=_EFFECT),
    )(buf, send_sems, recv_sems, after)


def _scatter_copies(gi_ref, go_ref, pi_ref, po_ref, send_sems, recv_sems):
    mx, my, mc = lax.axis_index("x"), lax.axis_index("y"), lax.axis_index("c")
    hr, fc = pi_ref.shape[1:]
    ro = po_ref.shape[1]
    copies = []
    for k in range(1, N_DEV):
        peer = (_flip(mx, k & 4), _flip(my, k & 2), _flip(mc, k & 1))
        pchip = 2 * peer[0] + peer[1]
        src = (gi_ref.at[pl.ds(pl.multiple_of(peer[2] * hr, 8), hr), pl.ds(pl.multiple_of(pchip * fc, LANES), fc)],
               go_ref.at[pl.ds(pl.multiple_of((2 * pchip + peer[2]) * ro, 8), ro), :])
        dst = (pi_ref.at[k - 1], po_ref.at[k - 1])
        for t in range(2):
            idx = 2 * (k - 1) + t
            copies.append(pltpu.make_async_remote_copy(src_ref=src[t], dst_ref=dst[t], send_sem=send_sems.at[idx],
                                                       recv_sem=recv_sems.at[idx], device_id=peer,
                                                       device_id_type=MESH))
    return copies


def _scatter_start(gi, go, name):
    d, f = gi.shape
    dd = go.shape[1]
    ns = 2 * (N_DEV - 1)
    pi = lax.empty((N_DEV - 1, d // 2, f // N_CHIP), BF16)
    po = lax.empty((N_DEV - 1, d // N_DEV, dd), BF16)

    def body(gi_ref, go_ref, pi_ref, po_ref, send_sems, recv_sems, gi_thru, go_thru, pi_thru, po_thru, token):
        for cp in _scatter_copies(gi_ref, go_ref, pi_ref, po_ref, send_sems, recv_sems):
            cp.start()
        token[...] = jnp.zeros_like(token)

    return pl.pallas_call(
        body, name=name,
        out_shape=(pltpu.SemaphoreType.DMA((ns,)), pltpu.SemaphoreType.DMA((ns,)),
                   pltpu.HBM(gi.shape, gi.dtype), pltpu.HBM(go.shape, go.dtype),
                   pltpu.HBM(pi.shape, pi.dtype), pltpu.HBM(po.shape, po.dtype),
                   jax.ShapeDtypeStruct((8, LANES), F32)),
        in_specs=(_HBM, _HBM, _HBM, _HBM),
        out_specs=(_SEM, _SEM, _HBM, _HBM, _HBM, _HBM, pl.BlockSpec(memory_space=pltpu.VMEM)),
        input_output_aliases={0: 2, 1: 3, 2: 4, 3: 5},
        compiler_params=pltpu.CompilerParams(has_side_effects=_EFFECT),
    )(_in_hbm(gi), _in_hbm(go), _in_hbm(pi), _in_hbm(po))


def _scatter_wait(send_sems, recv_sems, gi, go, pi, po, after, name):
    def body(gi_ref, go_ref, pi_ref, po_ref, send_sems, recv_sems, *rest):
        for cp in _scatter_copies(gi_ref, go_ref, pi_ref, po_ref, send_sems, recv_sems):
            cp.wait_send()
            cp.wait_recv()

    return pl.pallas_call(
        body, name=name,
        out_shape=tuple(pltpu.HBM(a.shape, a.dtype) for a in (gi, go, pi, po)),
        in_specs=(_HBM, _HBM, _HBM, _HBM, _SEM, _SEM) + (_ANY,) * len(after), out_specs=(_HBM, _HBM, _HBM, _HBM),
        input_output_aliases={0: 0, 1: 1, 2: 2, 3: 3},
        compiler_params=pltpu.CompilerParams(has_side_effects=_EFFECT),
    )(gi, go, pi, po, send_sems, recv_sems, *after)


def _sum_into(buf, g, parts, where2, layer, row_blocks, name):
    depth, r2, c = buf.shape
    r = r2 // 2
    tr, tc = _tile(r, 256), _tile(c, 1024)
    nr, nc = r // tr, c // tc
    col_blocks = (g.shape[1] // c) > 1

    def body(w_ref, buf_ref, g_ref, p_ref, o_ref):
        acc = g_ref[...].astype(F32)
        for e in range(N_DEV - 1):
            acc = acc + p_ref[e].astype(F32)
        o_ref[...] = acc

    return pl.pallas_call(
        body, name=name,
        grid_spec=pltpu.PrefetchScalarGridSpec(
            num_scalar_prefetch=1, grid=(nr, nc),
            in_specs=[_ANY,
                      pl.BlockSpec((tr, tc), lambda i, j, w: (row_blocks(w, nr) + i, (w[1] * nc if col_blocks else 0) + j)),
                      pl.BlockSpec((N_DEV - 1, tr, tc), lambda i, j, w: (0, i, j))],
            out_specs=pl.BlockSpec((None, tr, tc), lambda i, j, w: (layer, w[0] * nr + i, j))),
        out_shape=jax.ShapeDtypeStruct(buf.shape, F32),
        input_output_aliases={1: 0},
        compiler_params=_params(("parallel", "parallel")),
    )(where2, buf, g, parts)


def _exchange_halves(b_in, b_out, first, count, name):
    r_in = b_in.shape[1]
    r_out = b_out.shape[1]

    def body(bi_ref, bo_ref, oi_ref, oo_ref, send_sems, recv_sems):
        mx, my, mc = lax.axis_index("x"), lax.axis_index("y"), lax.axis_index("c")
        sib = (mx, my, 1 - mc)

        def half(ref, l, rows, which):
            return ref.at[l, pl.ds(pl.multiple_of(which * (rows // 2), 8), rows // 2), :]

        copies = []
        for l in range(first, first + count):
            for t, (src, dst, rows) in enumerate(((bi_ref, oi_ref, r_in), (bo_ref, oo_ref, r_out))):
                idx = 2 * (l - first) + t
                kw = dict(send_sem=send_sems.at[idx], recv_sem=recv_sems.at[idx], device_id=sib, device_id_type=MESH)
                cp = pltpu.make_async_remote_copy(src_ref=half(src, l, rows, mc), dst_ref=half(dst, l, rows, mc), **kw)
                cp.start()
                copies.append((cp, pltpu.make_async_remote_copy(src_ref=half(src, l, rows, mc),
                                                                dst_ref=half(dst, l, rows, 1 - mc), **kw)))
        for cp, landed in copies:
            landed.wait_recv()
        for cp, landed in copies:
            cp.wait_send()

    ns = 2 * count
    return pl.pallas_call(
        body, name=name,
        out_shape=[jax.ShapeDtypeStruct(b_in.shape, F32), jax.ShapeDtypeStruct(b_out.shape, F32)],
        in_specs=[_ANY, _ANY], out_specs=[_ANY, _ANY],
        input_output_aliases={0: 0, 1: 1},
        scratch_shapes=[pltpu.SemaphoreType.DMA((ns,)), pltpu.SemaphoreType.DMA((ns,))],
    )(b_in, b_out)


def _rows8(v):
    return jnp.pad(v, ((0, 8 - v.shape[0]), (0, 0)))


def kernel(x, c, norm_gain, w_ada, b_ada, w_in, w_out, ret_decay_logit_f, ret_decay_logit_b, final_gain, loss_target, m_norm_gain, m_w_ada, m_b_ada, m_w_in, m_w_out, m_ret_decay_logit_f, m_ret_decay_logit_b, m_final_gain, v_norm_gain, v_w_ada, v_b_ada, v_w_in, v_w_out, v_ret_decay_logit_f, v_ret_decay_logit_b, v_final_gain):
    cfg = _Cfg()
    depth, d = norm_gain.shape
    mx, my, mc = lax.axis_index("x"), lax.axis_index("y"), lax.axis_index("c")
    me = 4 * mx + 2 * my + mc
    chip = 2 * mx + my
    x0 = x[0]
    tgt = loss_target[0]
    ada_cols = w_ada.shape[2]

    c_all = _all_gather_small(_rows8(c), "gather_c")[:, 0, :]
    cact = _silu(c_all)
    mod_part = _ada_fwd(jnp.pad(cact, ((0, 8), (0, 0))).astype(BF16), w_ada, "ada_fwd")
    mod_all = _all_gather_small(mod_part.reshape(depth * 16, ada_cols), "gather_mod")
    mod_all = mod_all.reshape(N_CHIP, 2, depth, 16, ada_cols)[:, 0]
    mod_mine = lax.dynamic_index_in_dim(mod_all, me, axis=2, keepdims=False)
    mod = jnp.transpose(mod_mine, (1, 0, 2)).reshape(depth, 3, d)
    bias = b_ada.reshape(depth, 3, d)

    chip1 = jnp.reshape(chip, (1,)).astype(jnp.int32)
    where2 = jnp.stack([mc, chip]).astype(jnp.int32)

    def start_gather(l, dep):
        return _gather_start(_place_w_in(w_in, l, chip1, f"place_w_in_{l}"),
                             _place_w_out(w_out, l, chip1, f"place_w_out_{l}"), dep, f"gather_start_{l}")

    slopes = jnp.exp2(-8.0 * (jnp.arange(cfg.ha, dtype=F32) + 1.0) / cfg.ha)
    lg_f = jax.nn.log_sigmoid(ret_decay_logit_f)
    lg_b = jax.nn.log_sigmoid(ret_decay_logit_b)

    saved = []
    w_full = []
    h_x = x0
    pending = start_gather(0, c)
    for l in range(depth):
        send_sems, recv_sems, fi, fo, _ = pending
        w_in_l = _gather_wait(send_sems, recv_sems, fi, 0, mod if l == 0 else h_x, f"gather_wait_in_{l}")
        g8 = _rows8(norm_gain[l:l + 1])
        if l + 1 < depth:
            pending = start_gather(l + 1, w_in_l)
            g8 = g8 + pending[4][0:1, 0:1]
        mod3, b3 = _rows8(mod[l]), _rows8(bias[l])
        hb = _norm_mod_fwd(h_x, g8, mod3, b3, f"norm_mod_fwd_{l}")
        proj = _matmul(hb, w_in_l, "nn", BF16, f"in_proj_{l}")
        oa, lse = _attn_fwd(cfg, proj, slopes, f"attn_fwd_{l}")
        lgs = jnp.stack([lg_f[l], lg_b[l]])
        oret = _ret_fwd(cfg, proj, lgs, f"ret_fwd_{l}")
        y = _gate_fwd(cfg, oa, proj, oret, f"gate_fwd_{l}")
        w_out_l = _gather_wait(send_sems, recv_sems, fo, 1, y, f"gather_wait_out_{l}")
        w_full.append((w_in_l, w_out_l))
        x_next, out = _out_proj_fwd(y, w_out_l, h_x, mod3, b3, f"out_proj_{l}")
        saved.append((h_x, hb, proj, oret, y, oa, lse, out, g8, mod3, b3, lgs))
        h_x = x_next

    dx, loss8, fin_acc = _final_loss(h_x, tgt, _rows8(final_gain[None]), "final_loss")

    landed = [None] * depth
    d_mod, d_gain, d_lg = [None] * depth, [None] * depth, [None] * depth
    in_flight = None
    for l in reversed(range(depth)):
        x_l, hb, proj, oret, y, oa, lse, out, g8, mod3, b3, lgs = saved[l]
        w_in_l, w_out_l = w_full[l]
        if in_flight is not None:
            mod3 = mod3 + in_flight[-1][0:1, 0:1]
        douts, gate_acc = _out_proj_bwd_prep(dx, out, mod3, b3, f"out_proj_bwd_prep_{l}")
        dy = _matmul(douts, w_out_l, "nt", BF16, f"out_proj_dy_{l}")
        g_out_l = _matmul(y, douts, "tn", BF16, f"out_proj_dw_{l}", tk=1024)
        do_a, dza, do_r, dzr = _gate_bwd(cfg, dy, oa, proj, oret, f"gate_bwd_{l}")
        dq_a, dk_a, dv_a = _attn_bwd(cfg, proj, slopes, do_a, oa, lse, f"attn_bwd_{l}")
        rdq, rdk, rdv, dlg = _ret_bwd(cfg, proj, lgs, do_r, f"ret_bwd_{l}")
        dproj = _assemble_dproj(cfg, dq_a, dk_a, dv_a, dza, rdq, rdk, rdv, dzr, f"assemble_dproj_{l}")
        dh = _matmul(dproj, w_in_l, "nt", F32, f"in_proj_dh_{l}", tk=1792)
        g_in_l = _matmul(hb, dproj, "tn", BF16, f"in_proj_dw_{l}", tk=1024)
        dx, nm_acc = _norm_mod_bwd(dh, x_l, dx, g8, mod3, b3, f"norm_mod_bwd_{l}")
        d_mod[l] = jnp.concatenate([nm_acc[0], nm_acc[1], gate_acc[0]])
        d_gain[l] = nm_acc[2]
        d_lg[l] = dlg[:, 0:2, 0]
        if in_flight is not None:
            landed[l + 1] = _scatter_wait(*in_flight[:-1], (dx,), f"scatter_wait_{l + 1}")
        in_flight = _scatter_start(g_in_l, g_out_l, f"scatter_start_{l}")

    gw_in = lax.empty(w_in.shape, F32)
    gw_out = lax.empty(w_out.shape, F32)
    res_in = res_out = None
    for first, count in ((1, depth - 1), (0, 1)):
        if first == 0:
            after = (dx,) if res_out is None else (res_in[1], res_out[1])
            landed[0] = _scatter_wait(*in_flight[:-1], after, "scatter_wait_0")
        if count == 0:
            continue
        for l in range(first, first + count):
            gi, go, pi, po = landed[l]
            gw_in = _sum_into(gw_in, gi, pi, where2, l, lambda w, nr: w[0] * nr, f"sum_w_in_{l}")
            gw_out = _sum_into(gw_out, go, po, where2, l, lambda w, nr: (2 * w[1] + w[0]) * nr, f"sum_w_out_{l}")
        gw_in, gw_out = _exchange_halves(gw_in, gw_out, first, count, f"exchange_halves_{first}")
        res_in = _adamw_big(w_in, gw_in, m_w_in, v_w_in, first, count, res_in, f"adamw_w_in_{first}")
        res_out = _adamw_big(w_out, gw_out, m_w_out, v_w_out, first, count, res_out, f"adamw_w_out_{first}")
    grad_w_in, delta_w_in, new_m_w_in, new_v_w_in = res_in
    grad_w_out, delta_w_out, new_m_w_out, new_v_w_out = res_out

    dmod_mine = jnp.stack(d_mod)
    dmod_gathered = _all_gather_small(_rows8(dmod_mine), "gather_dmod")
    dmod_all = dmod_gathered[:, :depth, :]
    grad_b_ada = _sum_gathered(dmod_gathered, "sum_b_ada")[:depth]
    dmod_cols = lax.dynamic_slice_in_dim(dmod_all, chip * ada_cols, ada_cols, axis=2)
    dmod128 = jnp.pad(jnp.transpose(dmod_cols, (1, 0, 2)), ((0, 0), (0, 120), (0, 0))).astype(BF16)
    cact128 = jnp.pad(cact, ((0, 120), (0, 0))).astype(BF16)
    grad_w_ada, delta_w_ada, new_m_w_ada, new_v_w_ada = _adamw_ada(w_ada, m_w_ada, v_w_ada, cact128, dmod128,
                                                                  "adamw_w_ada")

    dlg_all = jnp.stack(d_lg)
    sig_f = jax.nn.sigmoid(-ret_decay_logit_f)
    sig_b = jax.nn.sigmoid(-ret_decay_logit_b)
    nlg = depth * cfg.hr
    pack = jnp.zeros((8, d), F32)
    for l in range(depth):
        pack = pack.at[l].set(d_gain[l])
    pack = pack.at[depth].set(fin_acc[0])
    pack = pack.at[depth + 1, 0].set(loss8[0, 0])
    pack = pack.at[depth + 1, LANES:LANES + nlg].set((dlg_all[:, :, 0] * sig_f).reshape(-1))
    pack = pack.at[depth + 1, 2 * LANES:2 * LANES + nlg].set((dlg_all[:, :, 1] * sig_b).reshape(-1))
    tot = _sum_gathered(_all_gather_small(pack, "gather_small"), "sum_small")
    grad_norm_gain = tot[:depth]
    grad_final_gain = tot[depth]
    loss = tot[depth + 1, 0]
    grad_lf = tot[depth + 1, LANES:LANES + nlg].reshape(depth, cfg.hr)
    grad_lb = tot[depth + 1, 2 * LANES:2 * LANES + nlg].reshape(depth, cfg.hr)

    d_ng, m_ng, v_ng = _adamw_small(norm_gain, grad_norm_gain, m_norm_gain, v_norm_gain, "adamw_norm_gain")
    d_ba, m_ba, v_ba = _adamw_small(b_ada, grad_b_ada, m_b_ada, v_b_ada, "adamw_b_ada")
    d_lf, m_lf, v_lf = _adamw_small(ret_decay_logit_f, grad_lf, m_ret_decay_logit_f, v_ret_decay_logit_f, "adamw_lf")
    d_lb, m_lb, v_lb = _adamw_small(ret_decay_logit_b, grad_lb, m_ret_decay_logit_b, v_ret_decay_logit_b, "adamw_lb")
    d_fg, m_fg, v_fg = _adamw_small(final_gain[None], grad_final_gain[None], m_final_gain[None], v_final_gain[None],
                                    "adamw_final_gain")

    return (loss, dx[None],
            grad_norm_gain, grad_w_ada, grad_b_ada, grad_w_in, grad_w_out, grad_lf, grad_lb, grad_final_gain,
            d_ng, delta_w_ada, d_ba, delta_w_in, delta_w_out, d_lf, d_lb, d_fg[0],
            m_ng, new_m_w_ada, m_ba, new_m_w_in, new_m_w_out, m_lf, m_lb, m_fg[0],
            v_ng, new_v_w_ada, v_ba, new_v_w_in, new_v_w_out, v_lf, v_lb, v_fg[0])
```

```python
import functools

import jax
import jax.numpy as jnp
from jax import lax
from jax.experimental import pallas as pl
from jax.experimental.pallas import tpu as pltpu

F32 = jnp.float32
BF16 = jnp.bfloat16

D_MODEL = 2048
SEQ = 4096
DEPTH = 4
HEAD_DIM = 128
DILATIONS = (1, 4, 16)
RADIUS = 64
N_HEADS_RET = 4
RET_QK = 128
RET_V = 256
RET_CHUNK = 256
NORM_EPS = 1e-6
MASK_VALUE = -1e30
N_DEV = 8
N_CHIP = 4
LANES = 128
VMEM_LIMIT = 56 * 1024 * 1024

ADAM_LR = 0.001
ADAM_B1 = 0.9
ADAM_B2 = 0.999
ADAM_EPS = 1e-08
ADAM_WD = 0.01
ADAM_STEP = 10

MESH = pl.DeviceIdType.MESH


class _Cfg:
    def __init__(self):
        self.d = D_MODEL
        self.s = SEQ
        self.aw = D_MODEL // 2
        self.ha = self.aw // HEAD_DIM
        self.rw = D_MODEL // 2
        self.hr = N_HEADS_RET
        self.rqk = self.hr * RET_QK
        self.f = 4 * self.aw + 2 * self.rqk + 2 * self.rw
        self.qa, self.ka, self.va, self.za = 0, self.aw, 2 * self.aw, 3 * self.aw
        self.qr = 4 * self.aw
        self.kr = self.qr + self.rqk
        self.vr = self.kr + self.rqk
        self.zr = self.vr + self.rw
        assert self.rw == self.hr * RET_V


def _tile(n, pref):
    t = min(n, pref)
    while n % t or t % LANES:
        t -= LANES
    return t


def _params(dims=None):
    return pltpu.CompilerParams(dimension_semantics=dims, vmem_limit_bytes=VMEM_LIMIT)


def _silu(z):
    return z * jax.nn.sigmoid(z)


def _dsilu(z):
    sg = jax.nn.sigmoid(z)
    return sg * (1.0 + z * (1.0 - sg))


_DN = {"nn": (((1,), (0,)), ((), ())), "nt": (((1,), (1,)), ((), ())), "tn": (((0,), (0,)), ((), ()))}


def _matmul(a, b, mode, out_dtype, name, tm=1024, tn=1024, tk=2048):
    if mode == "tn":
        kk, m = a.shape
    else:
        m, kk = a.shape
    n = b.shape[0] if mode == "nt" else b.shape[1]
    tm, tn, tk = _tile(m, tm), _tile(n, tn), _tile(kk, tk)
    nk = kk // tk
    a_spec = (pl.BlockSpec((tk, tm), lambda i, j, k: (k, i)) if mode == "tn"
              else pl.BlockSpec((tm, tk), lambda i, j, k: (i, k)))
    b_spec = (pl.BlockSpec((tn, tk), lambda i, j, k: (j, k)) if mode == "nt"
              else pl.BlockSpec((tk, tn), lambda i, j, k: (k, j)))
    dn = _DN[mode]

    def body(a_ref, b_ref, o_ref, acc_ref):
        k = pl.program_id(2)
        p = lax.dot_general(a_ref[...], b_ref[...], dn, preferred_element_type=F32)

        @pl.when(k == 0)
        def _():
            acc_ref[...] = p

        @pl.when(k > 0)
        def _():
            acc_ref[...] += p

        @pl.when(k == nk - 1)
        def _():
            o_ref[...] = acc_ref[...].astype(out_dtype)

    return pl.pallas_call(
        body, name=name, grid=(m // tm, n // tn, nk),
        in_specs=[a_spec, b_spec],
        out_specs=pl.BlockSpec((tm, tn), lambda i, j, k: (i, j)),
        out_shape=jax.ShapeDtypeStruct((m, n), out_dtype),
        scratch_shapes=[pltpu.VMEM((tm, tn), F32)],
        compiler_params=_params(("parallel", "parallel", "arbitrary")),
    )(a, b)


def _out_proj_fwd(y, w_out, x, mod3, b3, name):
    s, kk = y.shape
    d = w_out.shape[1]
    tm, tn = _tile(s, 256), d

    def body(y_ref, w_ref, x_ref, m_ref, b_ref, xn_ref, o_ref):
        out = jnp.dot(y_ref[...], w_ref[...], preferred_element_type=F32)
        gate = m_ref[2:3, :] + b_ref[2:3, :]
        xn_ref[...] = x_ref[...] + gate * out
        o_ref[...] = out.astype(BF16)

    vec = pl.BlockSpec((8, tn), lambda i, j: (0, j))
    return pl.pallas_call(
        body, name=name, grid=(s // tm, d // tn),
        in_specs=[pl.BlockSpec((tm, kk), lambda i, j: (i, 0)), pl.BlockSpec((kk, tn), lambda i, j: (0, j)),
                  pl.BlockSpec((tm, tn), lambda i, j: (i, j)), vec, vec],
        out_specs=[pl.BlockSpec((tm, tn), lambda i, j: (i, j)), pl.BlockSpec((tm, tn), lambda i, j: (i, j))],
        out_shape=[jax.ShapeDtypeStruct((s, d), F32), jax.ShapeDtypeStruct((s, d), BF16)],
        compiler_params=_params(("parallel", "parallel")),
    )(y, w_out, x, mod3, b3)


def _norm_mod_fwd(x, g8, mod3, b3, name):
    s, d = x.shape
    tr = _tile(s, 512)

    def body(x_ref, g_ref, m_ref, b_ref, h_ref):
        xv = x_ref[...]
        r = lax.rsqrt(jnp.mean(xv * xv, axis=-1, keepdims=True) + NORM_EPS)
        shift = m_ref[0:1, :] + b_ref[0:1, :]
        scale = m_ref[1:2, :] + b_ref[1:2, :]
        h_ref[...] = ((xv * r * g_ref[0:1, :]) * (1.0 + scale) + shift).astype(BF16)

    vec = pl.BlockSpec((8, d), lambda i: (0, 0))
    return pl.pallas_call(
        body, name=name, grid=(s // tr,),
        in_specs=[pl.BlockSpec((tr, d), lambda i: (i, 0)), vec, vec, vec],
        out_specs=pl.BlockSpec((tr, d), lambda i: (i, 0)),
        out_shape=jax.ShapeDtypeStruct((s, d), BF16),
        compiler_params=_params(("parallel",)),
    )(x, g8, mod3, b3)


def _norm_mod_bwd(dh, x, dx_next, g8, mod3, b3, name):
    s, d = x.shape
    tr = _tile(s, 256)

    def body(dh_ref, x_ref, dn_ref, g_ref, m_ref, b_ref, dx_ref, acc_ref):
        @pl.when(pl.program_id(0) == 0)
        def _():
            acc_ref[...] = jnp.zeros_like(acc_ref)

        xv = x_ref[...]
        dh_v = dh_ref[...]
        g = g_ref[0:1, :]
        r = lax.rsqrt(jnp.mean(xv * xv, axis=-1, keepdims=True) + NORM_EPS)
        xn = xv * r
        scale1 = 1.0 + m_ref[1:2, :] + b_ref[1:2, :]
        dhs = dh_v * scale1
        dxn = dhs * g
        dx_ref[...] = dn_ref[...] + r * (dxn - xn * jnp.mean(dxn * xn, axis=-1, keepdims=True))
        acc_ref[0:1, :] += jnp.sum(dh_v, axis=0, keepdims=True)
        acc_ref[1:2, :] += jnp.sum(dh_v * (xn * g), axis=0, keepdims=True)
        acc_ref[2:3, :] += jnp.sum(dhs * xn, axis=0, keepdims=True)

    vec = pl.BlockSpec((8, d), lambda i: (0, 0))
    row = pl.BlockSpec((tr, d), lambda i: (i, 0))
    return pl.pallas_call(
        body, name=name, grid=(s // tr,),
        in_specs=[row, row, row, vec, vec, vec],
        out_specs=[row, vec],
        out_shape=[jax.ShapeDtypeStruct((s, d), F32), jax.ShapeDtypeStruct((8, d), F32)],
        compiler_params=_params(("arbitrary",)),
    )(dh, x, dx_next, g8, mod3, b3)


def _final_loss(x, tgt, g8, name):
    s, d = x.shape
    tr = _tile(s, 256)

    def body(x_ref, t_ref, g_ref, dx_ref, loss_ref, acc_ref):
        @pl.when(pl.program_id(0) == 0)
        def _():
            acc_ref[...] = jnp.zeros_like(acc_ref)
            loss_ref[...] = jnp.zeros_like(loss_ref)

        xv = x_ref[...]
        g = g_ref[0:1, :]
        r = lax.rsqrt(jnp.mean(xv * xv, axis=-1, keepdims=True) + NORM_EPS)
        xn = xv * r
        err = xn * g - t_ref[...]
        loss_ref[...] += 0.5 * jnp.sum(jnp.mean(err * err, axis=-1, keepdims=True), axis=0, keepdims=True)
        dy = err * (1.0 / d)
        acc_ref[0:1, :] += jnp.sum(dy * xn, axis=0, keepdims=True)
        dxn = dy * g
        dx_ref[...] = r * (dxn - xn * jnp.mean(dxn * xn, axis=-1, keepdims=True))

    vec = pl.BlockSpec((8, d), lambda i: (0, 0))
    row = pl.BlockSpec((tr, d), lambda i: (i, 0))
    return pl.pallas_call(
        body, name=name, grid=(s // tr,),
        in_specs=[row, row, vec],
        out_specs=[row, pl.BlockSpec((8, LANES), lambda i: (0, 0)), vec],
        out_shape=[jax.ShapeDtypeStruct((s, d), F32), jax.ShapeDtypeStruct((8, LANES), F32),
                   jax.ShapeDtypeStruct((8, d), F32)],
        compiler_params=_params(("arbitrary",)),
    )(x, tgt, g8)


def _gate_fwd(cfg, oa, proj, oret, name):
    s = cfg.s
    aw, rw = cfg.aw, cfg.rw
    tr = _tile(s, 256)

    def body(oa_ref, za_ref, or_ref, zr_ref, y_ref):
        y_ref[:, 0:aw] = (oa_ref[...].astype(F32) * _silu(za_ref[...].astype(F32))).astype(BF16)
        for h in range(cfg.hr):
            cols = slice(h * RET_V, (h + 1) * RET_V)
            oh = or_ref[:, cols]
            rr = lax.rsqrt(jnp.mean(oh * oh, axis=-1, keepdims=True) + NORM_EPS)
            y_ref[:, aw + h * RET_V:aw + (h + 1) * RET_V] = (
                oh * rr * _silu(zr_ref[:, cols].astype(F32))).astype(BF16)

    ra = pl.BlockSpec((tr, aw), lambda i: (i, 0))
    return pl.pallas_call(
        body, name=name, grid=(s // tr,),
        in_specs=[ra,
                  pl.BlockSpec((tr, aw), lambda i: (i, cfg.za // aw)),
                  pl.BlockSpec((tr, rw), lambda i: (i, 0)),
                  pl.BlockSpec((tr, rw), lambda i: (i, cfg.zr // rw))],
        out_specs=pl.BlockSpec((tr, cfg.d), lambda i: (i, 0)),
        out_shape=jax.ShapeDtypeStruct((s, cfg.d), BF16),
        compiler_params=_params(("parallel",)),
    )(oa, proj, oret, proj)


def _gate_bwd(cfg, dy, oa, proj, oret, name):
    s = cfg.s
    aw, rw = cfg.aw, cfg.rw
    assert aw == rw and cfg.za % aw == 0 and cfg.zr % aw == 0
    tr = _tile(s, 256)

    def body(dy_ref, oa_ref, za_ref, or_ref, zr_ref, do_ref, dor_ref, dz_ref):
        @pl.when(pl.program_id(1) == 0)
        def _():
            dya = dy_ref[:, 0:aw].astype(F32)
            za = za_ref[...].astype(F32)
            do_ref[...] = (dya * _silu(za)).astype(BF16)
            dz_ref[...] = (dya * oa_ref[...].astype(F32) * _dsilu(za)).astype(BF16)

        @pl.when(pl.program_id(1) == 1)
        def _():
            for h in range(cfg.hr):
                cols = slice(h * RET_V, (h + 1) * RET_V)
                oh = or_ref[:, cols]
                zr = zr_ref[:, cols].astype(F32)
                dyr = dy_ref[:, aw + h * RET_V:aw + (h + 1) * RET_V].astype(F32)
                rr = lax.rsqrt(jnp.mean(oh * oh, axis=-1, keepdims=True) + NORM_EPS)
                yn = oh * rr
                dyn = dyr * _silu(zr)
                dz_ref[:, cols] = (dyr * yn * _dsilu(zr)).astype(BF16)
                dor_ref[:, cols] = (rr * (dyn - yn * jnp.mean(dyn * yn, axis=-1, keepdims=True))).astype(BF16)

    ra = pl.BlockSpec((tr, aw), lambda i, j: (i, 0))
    za_blk, zr_blk = cfg.za // aw, cfg.zr // aw
    return pl.pallas_call(
        body, name=name, grid=(s // tr, 2),
        in_specs=[pl.BlockSpec((tr, cfg.d), lambda i, j: (i, 0)), ra,
                  pl.BlockSpec((tr, aw), lambda i, j: (i, za_blk)), ra,
                  pl.BlockSpec((tr, rw), lambda i, j: (i, zr_blk))],
        out_specs=[ra, ra, pl.BlockSpec((tr, aw), lambda i, j: (i, za_blk + j * (zr_blk - za_blk)))],
        out_shape=[jax.ShapeDtypeStruct((s, aw), BF16), jax.ShapeDtypeStruct((s, rw), BF16),
                   jax.ShapeDtypeStruct((s, cfg.f), BF16)],
        compiler_params=_params(("parallel", "arbitrary")),
    )(dy, oa, proj, oret, proj)


def _out_proj_bwd_prep(dxn, out, mod3, b3, name):
    s, d = dxn.shape
    tr = _tile(s, 512)

    def body(dx_ref, o_ref, m_ref, b_ref, do_ref, acc_ref):
        @pl.when(pl.program_id(0) == 0)
        def _():
            acc_ref[...] = jnp.zeros_like(acc_ref)

        dxv = dx_ref[...]
        gate = m_ref[2:3, :] + b_ref[2:3, :]
        do_ref[...] = (gate * dxv).astype(BF16)
        acc_ref[0:1, :] += jnp.sum(dxv * o_ref[...].astype(F32), axis=0, keepdims=True)

    vec = pl.BlockSpec((8, d), lambda i: (0, 0))
    row = pl.BlockSpec((tr, d), lambda i: (i, 0))
    return pl.pallas_call(
        body, name=name, grid=(s // tr,),
        in_specs=[row, row, vec, vec],
        out_specs=[row, vec],
        out_shape=[jax.ShapeDtypeStruct((s, d), BF16), jax.ShapeDtypeStruct((8, d), F32)],
        compiler_params=_params(("arbitrary",)),
    )(dxn, out, mod3, b3)


_COPY_ROWS = 512


def _attn_geometry(cfg, dil):
    sub = cfg.s // dil
    bq = min(128, sub)
    win = min(256, sub)
    return sub, bq, win


_N_SHIFTS = 3


def _attn_rows(dil, r, i, sub, bq, win):
    margin = (win - bq) // 2
    ws = jnp.clip(i * bq - margin, 0, sub - win)
    shift = (i * bq - ws) // margin if margin else 0
    if dil == 1:
        return (shift, pl.ds(pl.multiple_of(i * bq, bq), bq), pl.ds(pl.multiple_of(ws, 64), win))
    return (shift, pl.ds(r + i * (bq * dil), bq, stride=dil), pl.ds(r + ws * dil, win, stride=dil))


def _fill_bias_tables(cfg, bias_ref, slope):
    for pattern, dil in enumerate(DILATIONS):
        _, bq, win = _attn_geometry(cfg, dil)
        margin = (win - bq) // 2
        rel0 = lax.broadcasted_iota(jnp.int32, (bq, win), 1) - lax.broadcasted_iota(jnp.int32, (bq, win), 0)
        for shift in range(_N_SHIFTS if margin else 1):
            arel = jnp.abs(rel0 - shift * margin)
            bias_ref[pattern * _N_SHIFTS + shift, 0:bq, 0:win] = jnp.where(
                arel <= RADIUS, -(slope * dil) * arel.astype(F32), MASK_VALUE)


def _bias_scratch():
    return pltpu.VMEM((len(DILATIONS) * _N_SHIFTS, 128, 256), F32)


def _attn_scores(cfg, bias_ref, q, kw, pattern, shift):
    bq, win = q.shape[0], kw.shape[0]
    return (lax.dot_general(q, kw, _DN["nt"], preferred_element_type=F32)
            + bias_ref[pattern * _N_SHIFTS + shift, 0:bq, 0:win])


_ATTN_GROUP = 4


def _for_each_group(cfg, group):
    for pattern, dil in enumerate(DILATIONS):
        sub, bq, _ = _attn_geometry(cfg, dil)
        nblk = sub // bq
        per = min(_ATTN_GROUP, nblk)
        for r in range(dil):
            if nblk == per:
                group(pattern, dil, [(r, i) for i in range(nblk)])
            else:
                def step(g, carry, pattern=pattern, dil=dil, r=r, per=per):
                    group(pattern, dil, [(r, g * per + j) for j in range(per)])
                    return carry

                lax.fori_loop(0, nblk // per, step, 0)


def _attn_fwd(cfg, proj, slopes, name):
    s = cfg.s
    scale = HEAD_DIM ** -0.5

    def body(sl_ref, q_ref, k_ref, v_ref, o_ref, lse_ref, qf, kf, vf, acc, m_s, l_s, bias_s):
        slope = sl_ref[pl.program_id(0)]

        def to_f32(i, carry):
            rows = pl.ds(pl.multiple_of(i * _COPY_ROWS, _COPY_ROWS), _COPY_ROWS)
            qf[rows, :] = q_ref[rows, :].astype(F32) * scale
            kf[rows, :] = k_ref[rows, :].astype(F32)
            vf[rows, :] = v_ref[rows, :].astype(F32)
            return carry

        lax.fori_loop(0, s // _COPY_ROWS, to_f32, 0)

        _fill_bias_tables(cfg, bias_s, slope)

        def group(pattern, dil, blocks):
            sub, bq, win = _attn_geometry(cfg, dil)
            rep = win // HEAD_DIM
            first = pattern == 0
            work = []
            for r, i in blocks:
                shift, qrows, krows = _attn_rows(dil, r, i, sub, bq, win)
                old = None if first else (m_s[qrows, :], l_s[qrows, :], acc[qrows, :])
                work.append((shift, qrows, qf[qrows, :].astype(BF16), kf[krows, :].astype(BF16),
                             vf[krows, :].astype(BF16), old))
            new = []
            for shift, qrows, q, kw, vw, old in work:
                sc = _attn_scores(cfg, bias_s, q, kw, pattern, shift)
                m_blk = jnp.max(sc, axis=-1, keepdims=True)
                if first:
                    m_new = jnp.broadcast_to(m_blk, (bq, HEAD_DIM))
                    p = jnp.exp(sc - m_blk)
                    l_new = jnp.broadcast_to(jnp.sum(p, axis=-1, keepdims=True), (bq, HEAD_DIM))
                    a_new = jnp.dot(p.astype(BF16), vw, preferred_element_type=F32)
                else:
                    m_old, l_old, a_old = old
                    m_new = jnp.maximum(m_old, m_blk)
                    alpha = jnp.exp(m_old - m_new)
                    p = jnp.exp(sc - jnp.tile(m_new, (1, rep)))
                    l_new = alpha * l_old + jnp.sum(p, axis=-1, keepdims=True)
                    a_new = alpha * a_old + jnp.dot(p.astype(BF16), vw, preferred_element_type=F32)
                new.append((qrows, m_new, l_new, a_new))
            for qrows, m_new, l_new, a_new in new:
                m_s[qrows, :] = m_new
                l_s[qrows, :] = l_new
                acc[qrows, :] = a_new

        _for_each_group(cfg, group)

        def finish(i, carry):
            rows = pl.ds(pl.multiple_of(i * _COPY_ROWS, _COPY_ROWS), _COPY_ROWS)
            den = l_s[rows, :]
            o_ref[rows, :] = (acc[rows, :] / den).astype(BF16)
            lse_ref[rows, :] = m_s[rows, :] + jnp.log(den)
            return carry

        lax.fori_loop(0, s // _COPY_ROWS, finish, 0)

    def col(off):
        return pl.BlockSpec((s, HEAD_DIM), lambda h: (0, off // HEAD_DIM + h))

    head = pl.BlockSpec((s, HEAD_DIM), lambda h: (0, h))
    return pl.pallas_call(
        body, name=name, grid=(cfg.ha,),
        in_specs=[pl.BlockSpec(memory_space=pltpu.SMEM), col(cfg.qa), col(cfg.ka), col(cfg.va)],
        out_specs=[head, head],
        out_shape=[jax.ShapeDtypeStruct((s, cfg.aw), BF16), jax.ShapeDtypeStruct((s, cfg.aw), F32)],
        scratch_shapes=[pltpu.VMEM((s, HEAD_DIM), F32)] * 6 + [_bias_scratch()],
        compiler_params=_params(("parallel",)),
    )(slopes, proj, proj, proj)


def _attn_bwd(cfg, proj, slopes, do, oa, lse, dproj, name):
    s = cfg.s
    scale = HEAD_DIM ** -0.5

    def accumulate(sl_ref, q_ref, k_ref, v_ref, do_ref, o_ref, lse_ref, qf, kf, vf, dof, dlt, dqa, dka, dva, bias_s):
        slope = sl_ref[pl.program_id(0)]

        def to_f32(i, carry):
            rows = pl.ds(pl.multiple_of(i * _COPY_ROWS, _COPY_ROWS), _COPY_ROWS)
            qf[rows, :] = q_ref[rows, :].astype(F32) * scale
            kf[rows, :] = k_ref[rows, :].astype(F32)
            vf[rows, :] = v_ref[rows, :].astype(F32)
            dov = do_ref[rows, :].astype(F32)
            dof[rows, :] = dov
            dlt[rows, :] = jnp.broadcast_to(jnp.sum(dov * o_ref[rows, :].astype(F32), axis=-1, keepdims=True),
                                            (_COPY_ROWS, HEAD_DIM))
            zero = jnp.zeros((_COPY_ROWS, HEAD_DIM), F32)
            dqa[rows, :] = zero
            dka[rows, :] = zero
            dva[rows, :] = zero
            return carry

        lax.fori_loop(0, s // _COPY_ROWS, to_f32, 0)

        _fill_bias_tables(cfg, bias_s, slope)

        def group(pattern, dil, blocks):
            sub, bq, win = _attn_geometry(cfg, dil)
            rep = win // HEAD_DIM
            work = []
            for r, i in blocks:
                shift, qrows, krows = _attn_rows(dil, r, i, sub, bq, win)
                work.append((shift, qrows, krows, qf[qrows, :].astype(BF16), kf[krows, :].astype(BF16),
                             vf[krows, :].astype(BF16), dof[qrows, :].astype(BF16),
                             lse_ref[qrows, :], dlt[qrows, :]))
            new = []
            for shift, qrows, krows, q, kw, vw, dob, lse_b, dlt_b in work:
                sc = _attn_scores(cfg, bias_s, q, kw, pattern, shift)
                p = jnp.exp(sc - jnp.tile(lse_b, (1, rep)))
                dp = lax.dot_general(dob, vw, _DN["nt"], preferred_element_type=F32)
                ds = (p * (dp - jnp.tile(dlt_b, (1, rep)))).astype(BF16)
                new.append((qrows, krows,
                            jnp.dot(ds, kw, preferred_element_type=F32) * scale,
                            lax.dot_general(ds, q, _DN["tn"], preferred_element_type=F32),
                            lax.dot_general(p.astype(BF16), dob, _DN["tn"], preferred_element_type=F32)))
            for qrows, krows, dq_b, dk_b, dv_b in new:
                dqa[qrows, :] += dq_b
                dka[krows, :] += dk_b
                dva[krows, :] += dv_b

        _for_each_group(cfg, group)

    def body(sl_ref, q_ref, k_ref, v_ref, do_ref, o_ref, lse_ref, dproj_in, out_ref, *scratch):
        part = pl.program_id(1)

        @pl.when(part == 0)
        def _():
            accumulate(sl_ref, q_ref, k_ref, v_ref, do_ref, o_ref, lse_ref, *scratch)

        for which, acc in enumerate(scratch[5:8]):
            @pl.when(part == which)
            def _(acc=acc):
                def emit(i, carry):
                    rows = pl.ds(pl.multiple_of(i * _COPY_ROWS, _COPY_ROWS), _COPY_ROWS)
                    out_ref[rows, :] = acc[rows, :].astype(BF16)
                    return carry

                lax.fori_loop(0, s // _COPY_ROWS, emit, 0)

    def col(off):
        return pl.BlockSpec((s, HEAD_DIM), lambda h, p: (0, off // HEAD_DIM + h))

    assert cfg.ka - cfg.qa == cfg.aw and cfg.va - cfg.ka == cfg.aw
    head = pl.BlockSpec((s, HEAD_DIM), lambda h, p: (0, h))
    return pl.pallas_call(
        body, name=name, grid=(cfg.ha, 3),
        in_specs=[pl.BlockSpec(memory_space=pltpu.SMEM), col(cfg.qa), col(cfg.ka), col(cfg.va), head, head, head,
                  _ANY],
        out_specs=pl.BlockSpec((s, HEAD_DIM), lambda h, p: (0, cfg.qa // HEAD_DIM + p * cfg.ha + h)),
        out_shape=jax.ShapeDtypeStruct(dproj.shape, BF16),
        input_output_aliases={7: 0},
        scratch_shapes=[pltpu.VMEM((s, HEAD_DIM), F32)] * 8 + [_bias_scratch()],
        compiler_params=_params(("parallel", "arbitrary")),
    )(slopes, proj, proj, proj, do, oa, lse, dproj)


def _decay_tables(lg, backward):
    c = RET_CHUNK
    a = lax.broadcasted_iota(jnp.int32, (c, c), 0)
    b = lax.broadcasted_iota(jnp.int32, (c, c), 1)
    idx = lax.broadcasted_iota(jnp.int32, (c, 1), 0).astype(F32)
    if backward:
        rel = (b - a).astype(F32)
        ex_xi = c - idx
        ex_zeta = idx
    else:
        rel = (a - b).astype(F32)
        ex_xi = idx + 1.0
        ex_zeta = c - 1.0 - idx
    relc = jnp.maximum(rel, 0.0)
    dm = jnp.where(rel >= 0, jnp.exp(relc * lg), 0.0)
    xi = jnp.exp(ex_xi * lg)
    zeta = jnp.exp(ex_zeta * lg)
    gch = jnp.exp(jnp.full((1, 1), c, F32) * lg)
    return relc, dm, xi, zeta, ex_xi, ex_zeta, gch


def _ret_fwd(cfg, proj, lgs, name):
    s = cfg.s
    c = RET_CHUNK
    n = s // c
    kscale = RET_QK ** -0.5

    def body(lg_ref, q_ref, k_ref, v_ref, o_ref, st_ref):
        h = pl.program_id(0)
        tabs = [_decay_tables(lg_ref[dirn, h], dirn == 1) for dirn in range(2)]
        st_ref[...] = jnp.zeros_like(st_ref)
        o_ref[...] = jnp.zeros_like(o_ref)

        def step(t, carry):
            for dirn in range(2):
                _, dm, xi, zeta, _, _, gch = tabs[dirn]
                i = (n - 1 - t) if dirn == 1 else t
                rows = pl.ds(pl.multiple_of(i * c, c), c)
                qi = q_ref[rows, :]
                ks = k_ref[rows, :].astype(F32) * kscale
                vi = v_ref[rows, :]
                inner = lax.dot_general(qi, ks.astype(BF16), _DN["nt"], preferred_element_type=F32) * dm
                st = st_ref[dirn]
                o_ref[rows, :] += (jnp.dot(inner.astype(BF16), vi, preferred_element_type=F32)
                                   + jnp.dot(qi, st.astype(BF16), preferred_element_type=F32) * xi)
                st_ref[dirn] = st * gch + lax.dot_general((ks * zeta).astype(BF16), vi, _DN["tn"],
                                                          preferred_element_type=F32)
            return carry

        lax.fori_loop(0, n, step, 0, unroll=2)

    return pl.pallas_call(
        body, name=name, grid=(cfg.hr,),
        in_specs=[pl.BlockSpec(memory_space=pltpu.SMEM),
                  pl.BlockSpec((s, RET_QK), lambda h: (0, cfg.qr // RET_QK + h)),
                  pl.BlockSpec((s, RET_QK), lambda h: (0, cfg.kr // RET_QK + h)),
                  pl.BlockSpec((s, RET_V), lambda h: (0, cfg.vr // RET_V + h))],
        out_specs=pl.BlockSpec((s, RET_V), lambda h: (0, h)),
        out_shape=jax.ShapeDtypeStruct((s, cfg.rw), F32),
        scratch_shapes=[pltpu.VMEM((2, RET_QK, RET_V), F32)],
        compiler_params=_params(("parallel",)),
    )(lgs, proj, proj, proj)


def _ret_bwd(cfg, proj, lgs, do, dproj, name):
    s = cfg.s
    c = RET_CHUNK
    n = s // c
    kscale = RET_QK ** -0.5

    def accumulate(lg_ref, q_ref, k_ref, v_ref, do_ref, dq_ref, dk_ref, dv_ref, dlg_ref, states, t_ref,
                   e_dm, e_xi, e_zeta, e_g):
        h = pl.program_id(0)
        tabs = [_decay_tables(lg_ref[dirn, h], dirn == 1) for dirn in range(2)]
        dlg_ref[...] = jnp.zeros_like(dlg_ref)
        dq_ref[...] = jnp.zeros_like(dq_ref)
        dk_ref[...] = jnp.zeros_like(dk_ref)
        dv_ref[...] = jnp.zeros_like(dv_ref)

        def chunk_rows(dirn, t):
            i = (n - 1 - t) if dirn == 1 else t
            return pl.ds(pl.multiple_of(i * c, c), c)

        t_ref[...] = jnp.zeros_like(t_ref)

        def fwd_step(t, carry):
            for dirn in range(2):
                _, _, _, zeta, _, _, gch = tabs[dirn]
                rows = chunk_rows(dirn, t)
                st = t_ref[dirn]
                states[dirn, t] = st
                ks = k_ref[rows, :].astype(F32) * kscale
                t_ref[dirn] = st * gch + lax.dot_general((ks * zeta).astype(BF16), v_ref[rows, :], _DN["tn"],
                                                         preferred_element_type=F32)
            return carry

        lax.fori_loop(0, n, fwd_step, 0, unroll=2)
        t_ref[...] = jnp.zeros_like(t_ref)

        for ref in (e_dm, e_xi, e_zeta, e_g):
            ref[...] = jnp.zeros_like(ref)

        def bwd_step(u, carry):
            t = n - 1 - u
            for dirn in range(2):
                relc, dm, xi, zeta, ex_xi, ex_zeta, gch = tabs[dirn]
                rows = chunk_rows(dirn, t)
                qi = q_ref[rows, :]
                ks = k_ref[rows, :].astype(F32) * kscale
                ksb = ks.astype(BF16)
                vi = v_ref[rows, :]
                doi = do_ref[rows, :]
                sn_f = states[dirn, t]
                sn = sn_f.astype(BF16)
                tt = t_ref[dirn]
                ttb = tt.astype(BF16)
                a_mat = lax.dot_general(qi, ksb, _DN["nt"], preferred_element_type=F32) * dm
                dov = lax.dot_general(doi, vi, _DN["nt"], preferred_element_type=F32)
                b_mat = (dov * dm).astype(BF16)
                kz = (ks * zeta).astype(BF16)
                d_v = (jnp.dot(kz, ttb, preferred_element_type=F32)
                       + lax.dot_general(a_mat.astype(BF16), doi, _DN["tn"], preferred_element_type=F32))
                dk_inter = lax.dot_general(vi, ttb, _DN["nt"], preferred_element_type=F32) * zeta
                d_k = lax.dot_general(b_mat, qi, _DN["tn"], preferred_element_type=F32) + dk_inter
                o_inter = jnp.dot(qi, sn, preferred_element_type=F32) * xi
                d_q = (jnp.dot(b_mat, ksb, preferred_element_type=F32)
                       + lax.dot_general(doi, sn, _DN["nt"], preferred_element_type=F32) * xi)
                e_dm[dirn] += relc * a_mat * dov
                e_xi[dirn] += ex_xi * (doi.astype(F32) * o_inter)
                e_zeta[dirn] += ex_zeta * (ks * dk_inter)
                e_g[dirn] += tt * sn_f
                t_ref[dirn] = tt * gch + lax.dot_general((qi.astype(F32) * xi).astype(BF16), doi, _DN["tn"],
                                                         preferred_element_type=F32)
                dq_ref[rows, :] += d_q
                dk_ref[rows, :] += d_k * kscale
                dv_ref[rows, :] += d_v
            return carry

        lax.fori_loop(0, n, bwd_step, 0, unroll=2)
        for dirn in range(2):
            total = (jnp.sum(e_dm[dirn], keepdims=True) + jnp.sum(e_xi[dirn], keepdims=True)
                     + jnp.sum(e_zeta[dirn], keepdims=True) + (c * tabs[dirn][6]) * jnp.sum(e_g[dirn], keepdims=True))
            dlg_ref[0, dirn:dirn + 1, :] = jnp.broadcast_to(total, (1, LANES))

    def body(lg_ref, q_ref, k_ref, v_ref, do_ref, dproj_in, out_ref, dlg_ref, dq_s, dk_s, dv_s, *scratch):
        part = pl.program_id(1)

        @pl.when(part == 0)
        def _():
            accumulate(lg_ref, q_ref, k_ref, v_ref, do_ref, dq_s, dk_s, dv_s, dlg_ref, *scratch)

        pieces = ((dq_s, 0), (dk_s, 0), (dv_s, 0), (dv_s, LANES))
        for which, (acc, lane0) in enumerate(pieces):
            @pl.when(part == which)
            def _(acc=acc, lane0=lane0):
                def emit(i, carry):
                    rows = pl.ds(pl.multiple_of(i * _COPY_ROWS, _COPY_ROWS), _COPY_ROWS)
                    out_ref[rows, :] = acc[rows, lane0:lane0 + LANES].astype(BF16)
                    return carry

                lax.fori_loop(0, s // _COPY_ROWS, emit, 0)

    assert RET_QK == LANES and RET_V == 2 * LANES
    qr_blk, kr_blk, vr_blk = cfg.qr // LANES, cfg.kr // LANES, cfg.vr // LANES

    def out_col(h, p):
        return jnp.where(p == 0, qr_blk + h, jnp.where(p == 1, kr_blk + h, vr_blk + 2 * h + p - 2))

    return pl.pallas_call(
        body, name=name, grid=(cfg.hr, 4),
        in_specs=[pl.BlockSpec(memory_space=pltpu.SMEM),
                  pl.BlockSpec((s, RET_QK), lambda h, p: (0, cfg.qr // RET_QK + h)),
                  pl.BlockSpec((s, RET_QK), lambda h, p: (0, cfg.kr // RET_QK + h)),
                  pl.BlockSpec((s, RET_V), lambda h, p: (0, cfg.vr // RET_V + h)),
                  pl.BlockSpec((s, RET_V), lambda h, p: (0, h)), _ANY],
        out_specs=[pl.BlockSpec((s, LANES), lambda h, p: (0, out_col(h, p))),
                   pl.BlockSpec((1, 8, LANES), lambda h, p: (h, 0, 0))],
        out_shape=[jax.ShapeDtypeStruct(dproj.shape, BF16), jax.ShapeDtypeStruct((cfg.hr, 8, LANES), F32)],
        input_output_aliases={5: 0},
        scratch_shapes=[pltpu.VMEM((s, RET_QK), F32), pltpu.VMEM((s, RET_QK), F32), pltpu.VMEM((s, RET_V), F32),
                        pltpu.VMEM((2, n, RET_QK, RET_V), F32), pltpu.VMEM((2, RET_QK, RET_V), F32),
                        pltpu.VMEM((2, c, c), F32), pltpu.VMEM((2, c, RET_V), F32),
                        pltpu.VMEM((2, c, RET_QK), F32), pltpu.VMEM((2, RET_QK, RET_V), F32)],
        compiler_params=_params(("parallel", "arbitrary")),
    )(lgs, proj, proj, proj, do, dproj)


def _ada_fwd(cact16, w_ada, name):
    depth, d, n = w_ada.shape
    tn = _tile(n, 768)

    def body(c_ref, w_ref, o_ref):
        o_ref[0] = jnp.dot(c_ref[...], w_ref[0].astype(BF16), preferred_element_type=F32)

    return pl.pallas_call(
        body, name=name, grid=(depth, n // tn),
        in_specs=[pl.BlockSpec((16, d), lambda l, j: (0, 0)), pl.BlockSpec((1, d, tn), lambda l, j: (l, 0, j))],
        out_specs=pl.BlockSpec((1, 16, tn), lambda l, j: (l, 0, j)),
        out_shape=jax.ShapeDtypeStruct((depth, 16, n), F32),
        compiler_params=_params(("parallel", "parallel")),
    )(cact16, w_ada)


def _adam_math(w, g, m, v):
    m2 = ADAM_B1 * m + (1.0 - ADAM_B1) * g
    v2 = ADAM_B2 * v + (1.0 - ADAM_B2) * (g * g)
    m_hat = m2 / (1.0 - ADAM_B1 ** ADAM_STEP)
    v_hat = v2 / (1.0 - ADAM_B2 ** ADAM_STEP)
    delta = -ADAM_LR * (m_hat / (jnp.sqrt(v_hat) + ADAM_EPS) + ADAM_WD * w)
    return delta, m2, v2


def _adamw_big(w, g, m, v, first, count, prev, name):
    _, r, c = w.shape
    tr, tc = _tile(r, 512), _tile(c, 1024)

    def body(w_ref, g_ref, m_ref, v_ref, *rest):
        go_ref, d_ref, mo_ref, vo_ref = rest[-4:]
        gv = g_ref[...]
        delta, m2, v2 = _adam_math(w_ref[...], gv, m_ref[...], v_ref[...])
        go_ref[...] = gv
        d_ref[...] = delta
        mo_ref[...] = m2
        vo_ref[...] = v2

    spec = pl.BlockSpec((1, tr, tc), lambda l, i, j: (first + l, i, j))
    shp = jax.ShapeDtypeStruct(w.shape, F32)
    carried = [] if prev is None else list(prev)
    return pl.pallas_call(
        body, name=name, grid=(count, r // tr, c // tc),
        in_specs=[spec] * 4 + [_ANY] * len(carried), out_specs=[spec] * 4, out_shape=[shp] * 4,
        input_output_aliases={4 + k: k for k in range(len(carried))},
        compiler_params=_params(("parallel", "parallel", "parallel")),
    )(w, g, m, v, *carried)


def _adamw_ada(w, m, v, cact128, dmod128, name):
    depth, r, c = w.shape
    tr, tc = _tile(r, 512), _tile(c, 768)

    def body(w_ref, m_ref, v_ref, c_ref, dm_ref, go_ref, d_ref, mo_ref, vo_ref):
        gv = lax.dot_general(c_ref[...], dm_ref[0], _DN["tn"], preferred_element_type=F32)
        delta, m2, v2 = _adam_math(w_ref[0], gv, m_ref[0], v_ref[0])
        go_ref[0] = gv
        d_ref[0] = delta
        mo_ref[0] = m2
        vo_ref[0] = v2

    spec = pl.BlockSpec((1, tr, tc), lambda l, i, j: (l, i, j))
    shp = jax.ShapeDtypeStruct(w.shape, F32)
    return pl.pallas_call(
        body, name=name, grid=(depth, r // tr, c // tc),
        in_specs=[spec] * 3 + [pl.BlockSpec((128, tr), lambda l, i, j: (0, i)),
                               pl.BlockSpec((1, 128, tc), lambda l, i, j: (l, 0, j))],
        out_specs=[spec] * 4, out_shape=[shp] * 4,
        compiler_params=_params(("parallel", "parallel", "parallel")),
    )(w, m, v, cact128, dmod128)


def _adamw_small(w, g, m, v, name):
    def body(w_ref, g_ref, m_ref, v_ref, d_ref, mo_ref, vo_ref):
        delta, m2, v2 = _adam_math(w_ref[...], g_ref[...], m_ref[...], v_ref[...])
        d_ref[...] = delta
        mo_ref[...] = m2
        vo_ref[...] = v2

    shp = jax.ShapeDtypeStruct(w.shape, F32)
    return pl.pallas_call(body, name=name, out_shape=[shp] * 3)(w, g, m, v)


def _sum_gathered(parts, name):
    nd, r, c = parts.shape

    def body(p_ref, o_ref):
        acc = p_ref[0]
        for e in range(1, nd):
            acc = acc + p_ref[e]
        o_ref[...] = acc

    return pl.pallas_call(body, name=name, out_shape=jax.ShapeDtypeStruct((r, c), F32))(parts)


def _flip(v, bit):
    return 1 - v if bit else v


def _all_gather_small(x, name):
    r, c = x.shape

    def body(x_ref, out_ref, send_sems, recv_sems, local_sem):
        mx, my, mc = lax.axis_index("x"), lax.axis_index("y"), lax.axis_index("c")
        me = 4 * mx + 2 * my + mc
        mine = pltpu.make_async_copy(x_ref, out_ref.at[me], local_sem)
        mine.start()
        sends = []
        for k in range(1, N_DEV):
            peer = (_flip(mx, k & 4), _flip(my, k & 2), _flip(mc, k & 1))
            cp = pltpu.make_async_remote_copy(src_ref=x_ref, dst_ref=out_ref.at[me], send_sem=send_sems.at[k - 1],
                                              recv_sem=recv_sems.at[k - 1], device_id=peer, device_id_type=MESH)
            cp.start()
            sends.append(cp)
        for k in range(1, N_DEV):
            peer = (_flip(mx, k & 4), _flip(my, k & 2), _flip(mc, k & 1))
            src = 4 * peer[0] + 2 * peer[1] + peer[2]
            pltpu.make_async_remote_copy(src_ref=x_ref, dst_ref=out_ref.at[src], send_sem=send_sems.at[k - 1],
                                         recv_sem=recv_sems.at[k - 1], device_id=peer,
                                         device_id_type=MESH).wait_recv()
        for cp in sends:
            cp.wait_send()
        mine.wait()

    return pl.pallas_call(
        body, name=name,
        out_shape=jax.ShapeDtypeStruct((N_DEV, r, c), x.dtype),
        in_specs=[pl.BlockSpec(memory_space=pltpu.VMEM)],
        out_specs=pl.BlockSpec(memory_space=pltpu.VMEM),
        scratch_shapes=[pltpu.SemaphoreType.DMA((N_DEV - 1,)), pltpu.SemaphoreType.DMA((N_DEV - 1,)),
                        pltpu.SemaphoreType.DMA],
        compiler_params=pltpu.CompilerParams(vmem_limit_bytes=VMEM_LIMIT),
    )(x)


_HBM = pl.BlockSpec(memory_space=pltpu.HBM)
_SEM = pl.BlockSpec(memory_space=pltpu.SEMAPHORE)
_ANY = pl.BlockSpec(memory_space=pl.ANY)
_EFFECT = pltpu.SideEffectType.DATAFLOW_SIDE_EFFECTING


def _in_hbm(a):
    return pltpu.with_memory_space_constraint(a, pltpu.HBM)


def _place_w_in(w, layer, chip1, name):
    _, d, fc = w.shape
    tr = _tile(d, 512)

    def body(c_ref, w_ref, o_ref):
        o_ref[...] = w_ref[...].astype(BF16)

    return pl.pallas_call(
        body, name=name,
        grid_spec=pltpu.PrefetchScalarGridSpec(
            num_scalar_prefetch=1, grid=(d // tr,),
            in_specs=[pl.BlockSpec((None, tr, fc), lambda i, c: (layer, i, 0))],
            out_specs=pl.BlockSpec((tr, fc), lambda i, c: (i, c[0]))),
        out_shape=jax.ShapeDtypeStruct((d, N_CHIP * fc), BF16),
        compiler_params=_params(("parallel",)),
    )(chip1, w)


def _place_w_out(w, layer, chip1, name):
    _, rc, dd = w.shape
    tc = _tile(dd, 1024)

    def body(c_ref, w_ref, o_ref):
        o_ref[...] = w_ref[...].astype(BF16)

    return pl.pallas_call(
        body, name=name,
        grid_spec=pltpu.PrefetchScalarGridSpec(
            num_scalar_prefetch=1, grid=(dd // tc,),
            in_specs=[pl.BlockSpec((None, rc, tc), lambda j, c: (layer, 0, j))],
            out_specs=pl.BlockSpec((rc, tc), lambda j, c: (c[0], j))),
        out_shape=jax.ShapeDtypeStruct((N_CHIP * rc, dd), BF16),
        compiler_params=_params(("parallel",)),
    )(chip1, w)


def _weight_region(ref, tensor, chip):
    if tensor == 0:
        fc = ref.shape[1] // N_CHIP
        return ref.at[:, pl.ds(pl.multiple_of(chip * fc, LANES), fc)]
    rc = ref.shape[0] // N_CHIP
    return ref.at[pl.ds(pl.multiple_of(chip * rc, 8), rc), :]


def _gather_copies(ref, tensor, send_sems, recv_sems, landing):
    mx, my, mc = lax.axis_index("x"), lax.axis_index("y"), lax.axis_index("c")
    mine = _weight_region(ref, tensor, 2 * mx + my)
    copies = []
    for j in range(1, N_CHIP):
        peer = (_flip(mx, j & 2), _flip(my, j & 1), mc)
        dst = _weight_region(ref, tensor, 2 * peer[0] + peer[1]) if landing else mine
        idx = 2 * (j - 1) + tensor
        copies.append(pltpu.make_async_remote_copy(src_ref=mine, dst_ref=dst, send_sem=send_sems.at[idx],
                                                   recv_sem=recv_sems.at[idx], device_id=peer, device_id_type=MESH))
    return copies


def _gather_start(fi, fo, dep, name):
    ns = 2 * (N_CHIP - 1)

    def body(fi_ref, fo_ref, dep_ref, send_sems, recv_sems, fi_thru, fo_thru, token):
        for tensor, ref in enumerate((fi_ref, fo_ref)):
            for cp in _gather_copies(ref, tensor, send_sems, recv_sems, landing=False):
                cp.start()
        token[...] = jnp.zeros_like(token)

    return pl.pallas_call(
        body, name=name,
        out_shape=(pltpu.SemaphoreType.DMA((ns,)), pltpu.SemaphoreType.DMA((ns,)),
                   pltpu.HBM(fi.shape, fi.dtype), pltpu.HBM(fo.shape, fo.dtype),
                   jax.ShapeDtypeStruct((8, LANES), F32)),
        in_specs=(_HBM, _HBM, _ANY),
        out_specs=(_SEM, _SEM, _HBM, _HBM, pl.BlockSpec(memory_space=pltpu.VMEM)),
        input_output_aliases={0: 2, 1: 3},
        compiler_params=pltpu.CompilerParams(has_side_effects=_EFFECT),
    )(_in_hbm(fi), _in_hbm(fo), dep)


def _gather_wait(send_sems, recv_sems, buf, tensor, after, name):
    def body(buf_ref, send_sems, recv_sems, after_ref, buf_out):
        for cp in _gather_copies(buf_ref, tensor, send_sems, recv_sems, landing=True):
            cp.wait_send()
            cp.wait_recv()

    return pl.pallas_call(
        body, name=name,
        out_shape=pltpu.HBM(buf.shape, buf.dtype),
        in_specs=(_HBM, _SEM, _SEM, _ANY), out_specs=_HBM,
        input_output_aliases={0: 0},
        compiler_params=pltpu.CompilerParams(has_side_effects=_EFFECT),
    )(buf, send_sems, recv_sems, after)


def _scatter_copies(gi_ref, go_ref, pi_ref, po_ref, send_sems, recv_sems):
    mx, my, mc = lax.axis_index("x"), lax.axis_index("y"), lax.axis_index("c")
    hr, fc = pi_ref.shape[1:]
    ro = po_ref.shape[1]
    copies = []
    for k in range(1, N_DEV):
        peer = (_flip(mx, k & 4), _flip(my, k & 2), _flip(mc, k & 1))
        pchip = 2 * peer[0] + peer[1]
        src = (gi_ref.at[pl.ds(pl.multiple_of(peer[2] * hr, 8), hr), pl.ds(pl.multiple_of(pchip * fc, LANES), fc)],
               go_ref.at[pl.ds(pl.multiple_of((2 * pchip + peer[2]) * ro, 8), ro), :])
        dst = (pi_ref.at[k - 1], po_ref.at[k - 1])
        for t in range(2):
            idx = 2 * (k - 1) + t
            copies.append(pltpu.make_async_remote_copy(src_ref=src[t], dst_ref=dst[t], send_sem=send_sems.at[idx],
                                                       recv_sem=recv_sems.at[idx], device_id=peer,
                                                       device_id_type=MESH))
    return copies


def _scatter_start(gi, go, name):
    d, f = gi.shape
    dd = go.shape[1]
    ns = 2 * (N_DEV - 1)
    pi = lax.empty((N_DEV - 1, d // 2, f // N_CHIP), BF16)
    po = lax.empty((N_DEV - 1, d // N_DEV, dd), BF16)

    def body(gi_ref, go_ref, pi_ref, po_ref, send_sems, recv_sems, gi_thru, go_thru, pi_thru, po_thru, token):
        for cp in _scatter_copies(gi_ref, go_ref, pi_ref, po_ref, send_sems, recv_sems):
            cp.start()
        token[...] = jnp.zeros_like(token)

    return pl.pallas_call(
        body, name=name,
        out_shape=(pltpu.SemaphoreType.DMA((ns,)), pltpu.SemaphoreType.DMA((ns,)),
                   pltpu.HBM(gi.shape, gi.dtype), pltpu.HBM(go.shape, go.dtype),
                   pltpu.HBM(pi.shape, pi.dtype), pltpu.HBM(po.shape, po.dtype),
                   jax.ShapeDtypeStruct((8, LANES), F32)),
        in_specs=(_HBM, _HBM, _HBM, _HBM),
        out_specs=(_SEM, _SEM, _HBM, _HBM, _HBM, _HBM, pl.BlockSpec(memory_space=pltpu.VMEM)),
        input_output_aliases={0: 2, 1: 3, 2: 4, 3: 5},
        compiler_params=pltpu.CompilerParams(has_side_effects=_EFFECT),
    )(_in_hbm(gi), _in_hbm(go), _in_hbm(pi), _in_hbm(po))


def _scatter_wait(send_sems, recv_sems, gi, go, pi, po, after, name):
    def body(gi_ref, go_ref, pi_ref, po_ref, send_sems, recv_sems, *rest):
        for cp in _scatter_copies(gi_ref, go_ref, pi_ref, po_ref, send_sems, recv_sems):
            cp.wait_send()
            cp.wait_recv()

    return pl.pallas_call(
        body, name=name,
        out_shape=tuple(pltpu.HBM(a.shape, a.dtype) for a in (gi, go, pi, po)),
        in_specs=(_HBM, _HBM, _HBM, _HBM, _SEM, _SEM) + (_ANY,) * len(after), out_specs=(_HBM, _HBM, _HBM, _HBM),
        input_output_aliases={0: 0, 1: 1, 2: 2, 3: 3},
        compiler_params=pltpu.CompilerParams(has_side_effects=_EFFECT),
    )(gi, go, pi, po, send_sems, recv_sems, *after)


def _sum_into(buf, g, parts, where2, layer, row_blocks, dep, name):
    depth, r2, c = buf.shape
    r = r2 // 2
    tr, tc = _tile(r, 256), _tile(c, 1024)
    nr, nc = r // tr, c // tc
    col_blocks = (g.shape[1] // c) > 1

    def body(w_ref, buf_ref, g_ref, p_ref, dep_ref, o_ref):
        acc = g_ref[...].astype(F32)
        for e in range(N_DEV - 1):
            acc = acc + p_ref[e].astype(F32)
        o_ref[...] = acc

    return pl.pallas_call(
        body, name=name,
        grid_spec=pltpu.PrefetchScalarGridSpec(
            num_scalar_prefetch=1, grid=(nr, nc),
            in_specs=[_ANY,
                      pl.BlockSpec((tr, tc), lambda i, j, w: (row_blocks(w, nr) + i, (w[1] * nc if col_blocks else 0) + j)),
                      pl.BlockSpec((N_DEV - 1, tr, tc), lambda i, j, w: (0, i, j)), _ANY],
            out_specs=pl.BlockSpec((None, tr, tc), lambda i, j, w: (layer, w[0] * nr + i, j))),
        out_shape=jax.ShapeDtypeStruct(buf.shape, F32),
        input_output_aliases={1: 0},
        compiler_params=_params(("parallel", "parallel")),
    )(where2, buf, g, parts, dep)


def _exchange_halves(b_in, b_out, first, count, name):
    r_in = b_in.shape[1]
    r_out = b_out.shape[1]

    def body(bi_ref, bo_ref, oi_ref, oo_ref, send_sems, recv_sems):
        mx, my, mc = lax.axis_index("x"), lax.axis_index("y"), lax.axis_index("c")
        sib = (mx, my, 1 - mc)

        def half(ref, l, rows, which):
            return ref.at[l, pl.ds(pl.multiple_of(which * (rows // 2), 8), rows // 2), :]

        copies = []
        for l in range(first, first + count):
            for t, (src, dst, rows) in enumerate(((bi_ref, oi_ref, r_in), (bo_ref, oo_ref, r_out))):
                idx = 2 * (l - first) + t
                kw = dict(send_sem=send_sems.at[idx], recv_sem=recv_sems.at[idx], device_id=sib, device_id_type=MESH)
                cp = pltpu.make_async_remote_copy(src_ref=half(src, l, rows, mc), dst_ref=half(dst, l, rows, mc), **kw)
                cp.start()
                copies.append((cp, pltpu.make_async_remote_copy(src_ref=half(src, l, rows, mc),
                                                                dst_ref=half(dst, l, rows, 1 - mc), **kw)))
        for cp, landed in copies:
            landed.wait_recv()
        for cp, landed in copies:
            cp.wait_send()

    ns = 2 * count
    return pl.pallas_call(
        body, name=name,
        out_shape=[jax.ShapeDtypeStruct(b_in.shape, F32), jax.ShapeDtypeStruct(b_out.shape, F32)],
        in_specs=[_ANY, _ANY], out_specs=[_ANY, _ANY],
        input_output_aliases={0: 0, 1: 1},
        scratch_shapes=[pltpu.SemaphoreType.DMA((ns,)), pltpu.SemaphoreType.DMA((ns,))],
    )(b_in, b_out)


def _rows8(v):
    return jnp.pad(v, ((0, 8 - v.shape[0]), (0, 0)))


def kernel(x, c, norm_gain, w_ada, b_ada, w_in, w_out, ret_decay_logit_f, ret_decay_logit_b, final_gain, loss_target, m_norm_gain, m_w_ada, m_b_ada, m_w_in, m_w_out, m_ret_decay_logit_f, m_ret_decay_logit_b, m_final_gain, v_norm_gain, v_w_ada, v_b_ada, v_w_in, v_w_out, v_ret_decay_logit_f, v_ret_decay_logit_b, v_final_gain):
    cfg = _Cfg()
    depth, d = norm_gain.shape
    mx, my, mc = lax.axis_index("x"), lax.axis_index("y"), lax.axis_index("c")
    me = 4 * mx + 2 * my + mc
    chip = 2 * mx + my
    x0 = x[0]
    tgt = loss_target[0]
    ada_cols = w_ada.shape[2]

    c_all = _all_gather_small(_rows8(c), "gather_c")[:, 0, :]
    cact = _silu(c_all)
    mod_part = _ada_fwd(jnp.pad(cact, ((0, 8), (0, 0))).astype(BF16), w_ada, "ada_fwd")
    mod_all = _all_gather_small(mod_part.reshape(depth * 16, ada_cols), "gather_mod")
    mod_all = mod_all.reshape(N_CHIP, 2, depth, 16, ada_cols)[:, 0]
    mod_mine = lax.dynamic_index_in_dim(mod_all, me, axis=2, keepdims=False)
    mod = jnp.transpose(mod_mine, (1, 0, 2)).reshape(depth, 3, d)
    bias = b_ada.reshape(depth, 3, d)

    chip1 = jnp.reshape(chip, (1,)).astype(jnp.int32)
    where2 = jnp.stack([mc, chip]).astype(jnp.int32)

    def start_gather(l, dep):
        return _gather_start(_place_w_in(w_in, l, chip1, f"place_w_in_{l}"),
                             _place_w_out(w_out, l, chip1, f"place_w_out_{l}"), dep, f"gather_start_{l}")

    slopes = jnp.exp2(-8.0 * (jnp.arange(cfg.ha, dtype=F32) + 1.0) / cfg.ha)
    lg_f = jax.nn.log_sigmoid(ret_decay_logit_f)
    lg_b = jax.nn.log_sigmoid(ret_decay_logit_b)

    saved = []
    w_full = []
    h_x = x0
    pending = start_gather(0, c)
    for l in range(depth):
        send_sems, recv_sems, fi, fo, _ = pending
        w_in_l = _gather_wait(send_sems, recv_sems, fi, 0, mod if l == 0 else h_x, f"gather_wait_in_{l}")
        g8 = _rows8(norm_gain[l:l + 1])
        if l + 1 < depth:
            pending = start_gather(l + 1, w_in_l)
            g8 = g8 + pending[4][0:1, 0:1]
        mod3, b3 = _rows8(mod[l]), _rows8(bias[l])
        hb = _norm_mod_fwd(h_x, g8, mod3, b3, f"norm_mod_fwd_{l}")
        proj = _matmul(hb, w_in_l, "nn", BF16, f"in_proj_{l}")
        oa, lse = _attn_fwd(cfg, proj, slopes, f"attn_fwd_{l}")
        lgs = jnp.stack([lg_f[l], lg_b[l]])
        oret = _ret_fwd(cfg, proj, lgs, f"ret_fwd_{l}")
        y = _gate_fwd(cfg, oa, proj, oret, f"gate_fwd_{l}")
        w_out_l = _gather_wait(send_sems, recv_sems, fo, 1, y, f"gather_wait_out_{l}")
        w_full.append((w_in_l, w_out_l))
        x_next, out = _out_proj_fwd(y, w_out_l, h_x, mod3, b3, f"out_proj_{l}")
        saved.append((h_x, hb, proj, oret, y, oa, lse, out, g8, mod3, b3, lgs))
        h_x = x_next

    dx, loss8, fin_acc = _final_loss(h_x, tgt, _rows8(final_gain[None]), "final_loss")

    landed = [None] * depth
    d_mod, d_gain, d_lg = [None] * depth, [None] * depth, [None] * depth
    in_flight = None
    for l in reversed(range(depth)):
        x_l, hb, proj, oret, y, oa, lse, out, g8, mod3, b3, lgs = saved[l]
        w_in_l, w_out_l = w_full[l]
        douts, gate_acc = _out_proj_bwd_prep(dx, out, mod3, b3, f"out_proj_bwd_prep_{l}")
        dy = _matmul(douts, w_out_l, "nt", BF16, f"out_proj_dy_{l}")
        g_out_l = _matmul(y, douts, "tn", BF16, f"out_proj_dw_{l}", tk=1024)
        do_a, do_r, dproj = _gate_bwd(cfg, dy, oa, proj, oret, f"gate_bwd_{l}")
        dproj = _attn_bwd(cfg, proj, slopes, do_a, oa, lse, dproj, f"attn_bwd_{l}")
        dproj, dlg = _ret_bwd(cfg, proj, lgs, do_r, dproj, f"ret_bwd_{l}")
        g_in_l = _matmul(hb, dproj, "tn", BF16, f"in_proj_dw_{l}", tk=1024)
        started = _scatter_start(g_in_l, g_out_l, f"scatter_start_{l}")
        dh = _matmul(dproj, w_in_l, "nt", F32, f"in_proj_dh_{l}", tk=1792)
        g8 = g8 + started[-1][0:1, 0:1]
        dx, nm_acc = _norm_mod_bwd(dh, x_l, dx, g8, mod3, b3, f"norm_mod_bwd_{l}")
        d_mod[l] = jnp.concatenate([nm_acc[0], nm_acc[1], gate_acc[0]])
        d_gain[l] = nm_acc[2]
        d_lg[l] = dlg[:, 0:2, 0]
        if in_flight is not None:
            landed[l + 1] = _scatter_wait(*in_flight[:-1], (dx,), f"scatter_wait_{l + 1}")
        in_flight = started

    gw_in = lax.empty(w_in.shape, F32)
    gw_out = lax.empty(w_out.shape, F32)
    res_in = res_out = None
    for first, count in ((1, depth - 1), (0, 1)):
        if first == 0:
            after = (dx,) if res_out is None else (res_in[1], res_out[1])
            landed[0] = _scatter_wait(*in_flight[:-1], after, "scatter_wait_0")
        if count == 0:
            continue
        dep = in_flight[-1]
        for l in range(first, first + count):
            gi, go, pi, po = landed[l]
            gw_in = _sum_into(gw_in, gi, pi, where2, l, lambda w, nr: w[0] * nr, dep, f"sum_w_in_{l}")
            gw_out = _sum_into(gw_out, go, po, where2, l, lambda w, nr: (2 * w[1] + w[0]) * nr, dep,
                               f"sum_w_out_{l}")
        gw_in, gw_out = _exchange_halves(gw_in, gw_out, first, count, f"exchange_halves_{first}")
        res_in = _adamw_big(w_in, gw_in, m_w_in, v_w_in, first, count, res_in, f"adamw_w_in_{first}")
        res_out = _adamw_big(w_out, gw_out, m_w_out, v_w_out, first, count, res_out, f"adamw_w_out_{first}")
    grad_w_in, delta_w_in, new_m_w_in, new_v_w_in = res_in
    grad_w_out, delta_w_out, new_m_w_out, new_v_w_out = res_out

    dmod_mine = jnp.stack(d_mod)
    dmod_gathered = _all_gather_small(_rows8(dmod_mine), "gather_dmod")
    dmod_all = dmod_gathered[:, :depth, :]
    grad_b_ada = _sum_gathered(dmod_gathered, "sum_b_ada")[:depth]
    dmod_cols = lax.dynamic_slice_in_dim(dmod_all, chip * ada_cols, ada_cols, axis=2)
    dmod128 = jnp.pad(jnp.transpose(dmod_cols, (1, 0, 2)), ((0, 0), (0, 120), (0, 0))).astype(BF16)
    cact128 = jnp.pad(cact, ((0, 120), (0, 0))).astype(BF16)
    grad_w_ada, delta_w_ada, new_m_w_ada, new_v_w_ada = _adamw_ada(w_ada, m_w_ada, v_w_ada, cact128, dmod128,
                                                                  "adamw_w_ada")

    dlg_all = jnp.stack(d_lg)
    sig_f = jax.nn.sigmoid(-ret_decay_logit_f)
    sig_b = jax.nn.sigmoid(-ret_decay_logit_b)
    nlg = depth * cfg.hr
    pack = jnp.zeros((8, d), F32)
    for l in range(depth):
        pack = pack.at[l].set(d_gain[l])
    pack = pack.at[depth].set(fin_acc[0])
    pack = pack.at[depth + 1, 0].set(loss8[0, 0])
    pack = pack.at[depth + 1, LANES:LANES + nlg].set((dlg_all[:, :, 0] * sig_f).reshape(-1))
    pack = pack.at[depth + 1, 2 * LANES:2 * LANES + nlg].set((dlg_all[:, :, 1] * sig_b).reshape(-1))
    tot = _sum_gathered(_all_gather_small(pack, "gather_small"), "sum_small")
    grad_norm_gain = tot[:depth]
    grad_final_gain = tot[depth]
    loss = tot[depth + 1, 0]
    grad_lf = tot[depth + 1, LANES:LANES + nlg].reshape(depth, cfg.hr)
    grad_lb = tot[depth + 1, 2 * LANES:2 * LANES + nlg].reshape(depth, cfg.hr)

    d_ng, m_ng, v_ng = _adamw_small(norm_gain, grad_norm_gain, m_norm_gain, v_norm_gain, "adamw_norm_gain")
    d_ba, m_ba, v_ba = _adamw_small(b_ada, grad_b_ada, m_b_ada, v_b_ada, "adamw_b_ada")
    d_lf, m_lf, v_lf = _adamw_small(ret_decay_logit_f, grad_lf, m_ret_decay_logit_f, v_ret_decay_logit_f, "adamw_lf")
    d_lb, m_lb, v_lb = _adamw_small(ret_decay_logit_b, grad_lb, m_ret_decay_logit_b, v_ret_decay_logit_b, "adamw_lb")
    d_fg, m_fg, v_fg = _adamw_small(final_gain[None], grad_final_gain[None], m_final_gain[None], v_final_gain[None],
                                    "adamw_final_gain")

    return (loss, dx[None],
            grad_norm_gain, grad_w_ada, grad_b_ada, grad_w_in, grad_w_out, grad_lf, grad_lb, grad_final_gain,
            d_ng, delta_w_ada, d_ba, delta_w_in, delta_w_out, d_lf, d_lb, d_fg[0],
            m_ng, new_m_w_ada, m_ba, new_m_w_in, new_m_w_out, m_lf, m_lb, m_fg[0],
            v_ng, new_v_w_ada, v_ba, new_v_w_in, new_v_w_out, v_lf, v_lb, v_fg[0])
```

```python
import functools

import jax
import jax.numpy as jnp
from jax import lax
from jax.experimental import pallas as pl
from jax.experimental.pallas import tpu as pltpu

F32 = jnp.float32
BF16 = jnp.bfloat16

D_MODEL = 2048
SEQ = 4096
DEPTH = 4
HEAD_DIM = 128
DILATIONS = (1, 4, 16)
RADIUS = 64
N_HEADS_RET = 4
RET_QK = 128
RET_V = 256
RET_CHUNK = 256
NORM_EPS = 1e-6
MASK_VALUE = -1e30
N_DEV = 8
N_CHIP = 4
LANES = 128
VMEM_LIMIT = 56 * 1024 * 1024

ADAM_LR = 0.001
ADAM_B1 = 0.9
ADAM_B2 = 0.999
ADAM_EPS = 1e-08
ADAM_WD = 0.01
ADAM_STEP = 10

MESH = pl.DeviceIdType.MESH


class _Cfg:
    def __init__(self):
        self.d = D_MODEL
        self.s = SEQ
        self.aw = D_MODEL // 2
        self.ha = self.aw // HEAD_DIM
        self.rw = D_MODEL // 2
        self.hr = N_HEADS_RET
        self.rqk = self.hr * RET_QK
        self.f = 4 * self.aw + 2 * self.rqk + 2 * self.rw
        self.qa, self.ka, self.va, self.za = 0, self.aw, 2 * self.aw, 3 * self.aw
        self.qr = 4 * self.aw
        self.kr = self.qr + self.rqk
        self.vr = self.kr + self.rqk
        self.zr = self.vr + self.rw
        assert self.rw == self.hr * RET_V


def _tile(n, pref):
    t = min(n, pref)
    while n % t or t % LANES:
        t -= LANES
    return t


def _params(dims=None):
    return pltpu.CompilerParams(dimension_semantics=dims, vmem_limit_bytes=VMEM_LIMIT)


def _silu(z):
    return z * jax.nn.sigmoid(z)


def _dsilu(z):
    sg = jax.nn.sigmoid(z)
    return sg * (1.0 + z * (1.0 - sg))


_DN = {"nn": (((1,), (0,)), ((), ())), "nt": (((1,), (1,)), ((), ())), "tn": (((0,), (0,)), ((), ()))}


def _matmul(a, b, mode, out_dtype, name, tm=1024, tn=1024, tk=2048):
    if mode == "tn":
        kk, m = a.shape
    else:
        m, kk = a.shape
    n = b.shape[0] if mode == "nt" else b.shape[1]
    tm, tn, tk = _tile(m, tm), _tile(n, tn), _tile(kk, tk)
    nk = kk // tk
    a_spec = (pl.BlockSpec((tk, tm), lambda i, j, k: (k, i)) if mode == "tn"
              else pl.BlockSpec((tm, tk), lambda i, j, k: (i, k)))
    b_spec = (pl.BlockSpec((tn, tk), lambda i, j, k: (j, k)) if mode == "nt"
              else pl.BlockSpec((tk, tn), lambda i, j, k: (k, j)))
    dn = _DN[mode]

    def body(a_ref, b_ref, o_ref, *acc):
        p = lax.dot_general(a_ref[...], b_ref[...], dn, preferred_element_type=F32)
        if nk == 1:
            o_ref[...] = p.astype(out_dtype)
            return
        acc_ref, = acc
        k = pl.program_id(2)

        @pl.when(k == 0)
        def _():
            acc_ref[...] = p

        @pl.when(k > 0)
        def _():
            acc_ref[...] += p

        @pl.when(k == nk - 1)
        def _():
            o_ref[...] = acc_ref[...].astype(out_dtype)

    return pl.pallas_call(
        body, name=name, grid=(m // tm, n // tn, nk),
        in_specs=[a_spec, b_spec],
        out_specs=pl.BlockSpec((tm, tn), lambda i, j, k: (i, j)),
        out_shape=jax.ShapeDtypeStruct((m, n), out_dtype),
        scratch_shapes=[pltpu.VMEM((tm, tn), F32)] if nk > 1 else [],
        compiler_params=_params(("parallel", "parallel", "arbitrary")),
    )(a, b)


def _out_proj_fwd(y, w_out, x, mod3, b3, name):
    s, kk = y.shape
    d = w_out.shape[1]
    tm, tn = _tile(s, 256), d

    def body(y_ref, w_ref, x_ref, m_ref, b_ref, xn_ref, o_ref):
        out = jnp.dot(y_ref[...], w_ref[...], preferred_element_type=F32)
        gate = m_ref[2:3, :] + b_ref[2:3, :]
        xn_ref[...] = x_ref[...] + gate * out
        o_ref[...] = out.astype(BF16)

    vec = pl.BlockSpec((8, tn), lambda i, j: (0, j))
    return pl.pallas_call(
        body, name=name, grid=(s // tm, d // tn),
        in_specs=[pl.BlockSpec((tm, kk), lambda i, j: (i, 0)), pl.BlockSpec((kk, tn), lambda i, j: (0, j)),
                  pl.BlockSpec((tm, tn), lambda i, j: (i, j)), vec, vec],
        out_specs=[pl.BlockSpec((tm, tn), lambda i, j: (i, j)), pl.BlockSpec((tm, tn), lambda i, j: (i, j))],
        out_shape=[jax.ShapeDtypeStruct((s, d), F32), jax.ShapeDtypeStruct((s, d), BF16)],
        compiler_params=_params(("parallel", "parallel")),
    )(y, w_out, x, mod3, b3)


def _norm_mod_fwd(x, g8, mod3, b3, name):
    s, d = x.shape
    tr = _tile(s, 512)

    def body(x_ref, g_ref, m_ref, b_ref, h_ref):
        xv = x_ref[...]
        r = lax.rsqrt(jnp.mean(xv * xv, axis=-1, keepdims=True) + NORM_EPS)
        shift = m_ref[0:1, :] + b_ref[0:1, :]
        scale = m_ref[1:2, :] + b_ref[1:2, :]
        h_ref[...] = ((xv * r * g_ref[0:1, :]) * (1.0 + scale) + shift).astype(BF16)

    vec = pl.BlockSpec((8, d), lambda i: (0, 0))
    return pl.pallas_call(
        body, name=name, grid=(s // tr,),
        in_specs=[pl.BlockSpec((tr, d), lambda i: (i, 0)), vec, vec, vec],
        out_specs=pl.BlockSpec((tr, d), lambda i: (i, 0)),
        out_shape=jax.ShapeDtypeStruct((s, d), BF16),
        compiler_params=_params(("parallel",)),
    )(x, g8, mod3, b3)


def _norm_mod_bwd(dh, x, dx_next, g8, mod3, b3, name):
    s, d = x.shape
    tr = _tile(s, 256)

    def body(dh_ref, x_ref, dn_ref, g_ref, m_ref, b_ref, dx_ref, acc_ref):
        @pl.when(pl.program_id(0) == 0)
        def _():
            acc_ref[...] = jnp.zeros_like(acc_ref)

        xv = x_ref[...]
        dh_v = dh_ref[...]
        g = g_ref[0:1, :]
        r = lax.rsqrt(jnp.mean(xv * xv, axis=-1, keepdims=True) + NORM_EPS)
        xn = xv * r
        scale1 = 1.0 + m_ref[1:2, :] + b_ref[1:2, :]
        dhs = dh_v * scale1
        dxn = dhs * g
        dx_ref[...] = dn_ref[...] + r * (dxn - xn * jnp.mean(dxn * xn, axis=-1, keepdims=True))
        acc_ref[0:1, :] += jnp.sum(dh_v, axis=0, keepdims=True)
        acc_ref[1:2, :] += jnp.sum(dh_v * (xn * g), axis=0, keepdims=True)
        acc_ref[2:3, :] += jnp.sum(dhs * xn, axis=0, keepdims=True)

    vec = pl.BlockSpec((8, d), lambda i: (0, 0))
    row = pl.BlockSpec((tr, d), lambda i: (i, 0))
    return pl.pallas_call(
        body, name=name, grid=(s // tr,),
        in_specs=[row, row, row, vec, vec, vec],
        out_specs=[row, vec],
        out_shape=[jax.ShapeDtypeStruct((s, d), F32), jax.ShapeDtypeStruct((8, d), F32)],
        compiler_params=_params(("arbitrary",)),
    )(dh, x, dx_next, g8, mod3, b3)


def _final_loss(x, tgt, g8, name):
    s, d = x.shape
    tr = _tile(s, 256)

    def body(x_ref, t_ref, g_ref, dx_ref, loss_ref, acc_ref):
        @pl.when(pl.program_id(0) == 0)
        def _():
            acc_ref[...] = jnp.zeros_like(acc_ref)
            loss_ref[...] = jnp.zeros_like(loss_ref)

        xv = x_ref[...]
        g = g_ref[0:1, :]
        r = lax.rsqrt(jnp.mean(xv * xv, axis=-1, keepdims=True) + NORM_EPS)
        xn = xv * r
        err = xn * g - t_ref[...]
        loss_ref[...] += 0.5 * jnp.sum(jnp.mean(err * err, axis=-1, keepdims=True), axis=0, keepdims=True)
        dy = err * (1.0 / d)
        acc_ref[0:1, :] += jnp.sum(dy * xn, axis=0, keepdims=True)
        dxn = dy * g
        dx_ref[...] = r * (dxn - xn * jnp.mean(dxn * xn, axis=-1, keepdims=True))

    vec = pl.BlockSpec((8, d), lambda i: (0, 0))
    row = pl.BlockSpec((tr, d), lambda i: (i, 0))
    return pl.pallas_call(
        body, name=name, grid=(s // tr,),
        in_specs=[row, row, vec],
        out_specs=[row, pl.BlockSpec((8, LANES), lambda i: (0, 0)), vec],
        out_shape=[jax.ShapeDtypeStruct((s, d), F32), jax.ShapeDtypeStruct((8, LANES), F32),
                   jax.ShapeDtypeStruct((8, d), F32)],
        compiler_params=_params(("arbitrary",)),
    )(x, tgt, g8)


def _gate_fwd(cfg, oa, proj, oret, name):
    s = cfg.s
    aw, rw = cfg.aw, cfg.rw
    tr = _tile(s, 256)

    def body(oa_ref, za_ref, or_ref, zr_ref, y_ref):
        y_ref[:, 0:aw] = (oa_ref[...].astype(F32) * _silu(za_ref[...].astype(F32))).astype(BF16)
        for h in range(cfg.hr):
            cols = slice(h * RET_V, (h + 1) * RET_V)
            oh = or_ref[:, cols]
            rr = lax.rsqrt(jnp.mean(oh * oh, axis=-1, keepdims=True) + NORM_EPS)
            y_ref[:, aw + h * RET_V:aw + (h + 1) * RET_V] = (
                oh * rr * _silu(zr_ref[:, cols].astype(F32))).astype(BF16)

    ra = pl.BlockSpec((tr, aw), lambda i: (i, 0))
    return pl.pallas_call(
        body, name=name, grid=(s // tr,),
        in_specs=[ra,
                  pl.BlockSpec((tr, aw), lambda i: (i, cfg.za // aw)),
                  pl.BlockSpec((tr, rw), lambda i: (i, 0)),
                  pl.BlockSpec((tr, rw), lambda i: (i, cfg.zr // rw))],
        out_specs=pl.BlockSpec((tr, cfg.d), lambda i: (i, 0)),
        out_shape=jax.ShapeDtypeStruct((s, cfg.d), BF16),
        compiler_params=_params(("parallel",)),
    )(oa, proj, oret, proj)


def _out_proj_bwd_prep(dxn, out, mod3, b3, name):
    s, d = dxn.shape
    tr = _tile(s, 512)

    def body(dx_ref, o_ref, m_ref, b_ref, do_ref, acc_ref):
        @pl.when(pl.program_id(0) == 0)
        def _():
            acc_ref[...] = jnp.zeros_like(acc_ref)

        dxv = dx_ref[...]
        gate = m_ref[2:3, :] + b_ref[2:3, :]
        do_ref[...] = (gate * dxv).astype(BF16)
        acc_ref[0:1, :] += jnp.sum(dxv * o_ref[...].astype(F32), axis=0, keepdims=True)

    vec = pl.BlockSpec((8, d), lambda i: (0, 0))
    row = pl.BlockSpec((tr, d), lambda i: (i, 0))
    return pl.pallas_call(
        body, name=name, grid=(s // tr,),
        in_specs=[row, row, vec, vec],
        out_specs=[row, vec],
        out_shape=[jax.ShapeDtypeStruct((s, d), BF16), jax.ShapeDtypeStruct((8, d), F32)],
        compiler_params=_params(("arbitrary",)),
    )(dxn, out, mod3, b3)


_COPY_ROWS = 512


def _attn_geometry(cfg, dil):
    sub = cfg.s // dil
    bq = min(128, sub)
    win = min(256, sub)
    return sub, bq, win


_N_SHIFTS = 3


def _attn_rows(dil, r, i, sub, bq, win):
    margin = (win - bq) // 2
    ws = jnp.clip(i * bq - margin, 0, sub - win)
    shift = (i * bq - ws) // margin if margin else 0
    if dil == 1:
        return (shift, pl.ds(pl.multiple_of(i * bq, bq), bq), pl.ds(pl.multiple_of(ws, 64), win))
    return (shift, pl.ds(r + i * (bq * dil), bq, stride=dil), pl.ds(r + ws * dil, win, stride=dil))


def _fill_bias_tables(cfg, bias_ref, slope):
    for pattern, dil in enumerate(DILATIONS):
        _, bq, win = _attn_geometry(cfg, dil)
        margin = (win - bq) // 2
        rel0 = lax.broadcasted_iota(jnp.int32, (bq, win), 1) - lax.broadcasted_iota(jnp.int32, (bq, win), 0)
        for shift in range(_N_SHIFTS if margin else 1):
            arel = jnp.abs(rel0 - shift * margin)
            bias_ref[pattern * _N_SHIFTS + shift, 0:bq, 0:win] = jnp.where(
                arel <= RADIUS, -(slope * dil) * arel.astype(F32), MASK_VALUE)


def _bias_scratch():
    return pltpu.VMEM((len(DILATIONS) * _N_SHIFTS, 128, 256), F32)


def _attn_scores(cfg, bias_ref, q, kw, pattern, shift):
    bq, win = q.shape[0], kw.shape[0]
    return (lax.dot_general(q, kw, _DN["nt"], preferred_element_type=F32)
            + bias_ref[pattern * _N_SHIFTS + shift, 0:bq, 0:win])


_ATTN_GROUP = 4


def _for_each_group(cfg, group):
    for pattern, dil in enumerate(DILATIONS):
        sub, bq, _ = _attn_geometry(cfg, dil)
        nblk = sub // bq
        per = min(_ATTN_GROUP, nblk)
        for r in range(dil):
            if nblk == per:
                group(pattern, dil, [(r, i) for i in range(nblk)])
            else:
                def step(g, carry, pattern=pattern, dil=dil, r=r, per=per):
                    group(pattern, dil, [(r, g * per + j) for j in range(per)])
                    return carry

                lax.fori_loop(0, nblk // per, step, 0)


def _attn_fwd(cfg, proj, slopes, name):
    s = cfg.s
    scale = HEAD_DIM ** -0.5

    def body(sl_ref, q_ref, k_ref, v_ref, o_ref, lse_ref, qf, kf, vf, acc, m_s, l_s, bias_s):
        slope = sl_ref[pl.program_id(0)]

        def to_f32(i, carry):
            rows = pl.ds(pl.multiple_of(i * _COPY_ROWS, _COPY_ROWS), _COPY_ROWS)
            qf[rows, :] = q_ref[rows, :].astype(F32) * scale
            kf[rows, :] = k_ref[rows, :].astype(F32)
            vf[rows, :] = v_ref[rows, :].astype(F32)
            return carry

        lax.fori_loop(0, s // _COPY_ROWS, to_f32, 0)

        _fill_bias_tables(cfg, bias_s, slope)

        def group(pattern, dil, blocks):
            sub, bq, win = _attn_geometry(cfg, dil)
            rep = win // HEAD_DIM
            first = pattern == 0
            work = []
            for r, i in blocks:
                shift, qrows, krows = _attn_rows(dil, r, i, sub, bq, win)
                old = None if first else (m_s[qrows, :], l_s[qrows, :], acc[qrows, :])
                work.append((shift, qrows, qf[qrows, :].astype(BF16), kf[krows, :].astype(BF16),
                             vf[krows, :].astype(BF16), old))
            new = []
            for shift, qrows, q, kw, vw, old in work:
                sc = _attn_scores(cfg, bias_s, q, kw, pattern, shift)
                m_blk = jnp.max(sc, axis=-1, keepdims=True)
                if first:
                    m_new = jnp.broadcast_to(m_blk, (bq, HEAD_DIM))
                    p = jnp.exp(sc - m_blk)
                    l_new = jnp.broadcast_to(jnp.sum(p, axis=-1, keepdims=True), (bq, HEAD_DIM))
                    a_new = jnp.dot(p.astype(BF16), vw, preferred_element_type=F32)
                else:
                    m_old, l_old, a_old = old
                    m_new = jnp.maximum(m_old, m_blk)
                    alpha = jnp.exp(m_old - m_new)
                    p = jnp.exp(sc - jnp.tile(m_new, (1, rep)))
                    l_new = alpha * l_old + jnp.sum(p, axis=-1, keepdims=True)
                    a_new = alpha * a_old + jnp.dot(p.astype(BF16), vw, preferred_element_type=F32)
                new.append((qrows, m_new, l_new, a_new))
            for qrows, m_new, l_new, a_new in new:
                m_s[qrows, :] = m_new
                l_s[qrows, :] = l_new
                acc[qrows, :] = a_new

        _for_each_group(cfg, group)

        def finish(i, carry):
            rows = pl.ds(pl.multiple_of(i * _COPY_ROWS, _COPY_ROWS), _COPY_ROWS)
            den = l_s[rows, :]
            o_ref[rows, :] = (acc[rows, :] / den).astype(BF16)
            lse_ref[rows, :] = m_s[rows, :] + jnp.log(den)
            return carry

        lax.fori_loop(0, s // _COPY_ROWS, finish, 0)

    def col(off):
        return pl.BlockSpec((s, HEAD_DIM), lambda h: (0, off // HEAD_DIM + h))

    head = pl.BlockSpec((s, HEAD_DIM), lambda h: (0, h))
    return pl.pallas_call(
        body, name=name, grid=(cfg.ha,),
        in_specs=[pl.BlockSpec(memory_space=pltpu.SMEM), col(cfg.qa), col(cfg.ka), col(cfg.va)],
        out_specs=[head, head],
        out_shape=[jax.ShapeDtypeStruct((s, cfg.aw), BF16), jax.ShapeDtypeStruct((s, cfg.aw), F32)],
        scratch_shapes=[pltpu.VMEM((s, HEAD_DIM), F32)] * 6 + [_bias_scratch()],
        compiler_params=_params(("parallel",)),
    )(slopes, proj, proj, proj)


def _attn_bwd(cfg, proj, slopes, dy, oa, lse, name):
    s = cfg.s
    scale = HEAD_DIM ** -0.5
    dst_blocks = [c0 // HEAD_DIM for c0 in (cfg.za, cfg.qa, cfg.ka, cfg.va)]

    def body(sl_ref, q_ref, k_ref, v_ref, z_ref, dy_ref, o_ref, lse_ref, dproj_ref,
             qf, kf, vf, dof, dlt, dqa, dka, dva, bias_s, stage, sems):
        head = pl.program_id(0)
        slope = sl_ref[head]

        def out_copy(slot):
            cols = pl.ds(pl.multiple_of((dst_blocks[slot] + head) * HEAD_DIM, HEAD_DIM), HEAD_DIM)
            return pltpu.make_async_copy(stage.at[slot], dproj_ref.at[:, cols], sems.at[slot])

        @pl.when(head > 0)
        def _():
            out_copy(0).wait()

        def to_f32(i, carry):
            rows = pl.ds(pl.multiple_of(i * _COPY_ROWS, _COPY_ROWS), _COPY_ROWS)
            qf[rows, :] = q_ref[rows, :].astype(F32) * scale
            kf[rows, :] = k_ref[rows, :].astype(F32)
            vf[rows, :] = v_ref[rows, :].astype(F32)
            dyv = dy_ref[rows, :].astype(F32)
            zv = z_ref[rows, :].astype(F32)
            ov = o_ref[rows, :].astype(F32)
            dov = dyv * _silu(zv)
            dof[rows, :] = dov
            dlt[rows, :] = jnp.broadcast_to(jnp.sum(dov * ov, axis=-1, keepdims=True), (_COPY_ROWS, HEAD_DIM))
            stage[0, rows, :] = (dyv * ov * _dsilu(zv)).astype(BF16)
            zero = jnp.zeros((_COPY_ROWS, HEAD_DIM), F32)
            dqa[rows, :] = zero
            dka[rows, :] = zero
            dva[rows, :] = zero
            return carry

        lax.fori_loop(0, s // _COPY_ROWS, to_f32, 0)
        out_copy(0).start()

        _fill_bias_tables(cfg, bias_s, slope)

        def group(pattern, dil, blocks):
            sub, bq, win = _attn_geometry(cfg, dil)
            rep = win // HEAD_DIM
            work = []
            for r, i in blocks:
                shift, qrows, krows = _attn_rows(dil, r, i, sub, bq, win)
                work.append((shift, qrows, krows, qf[qrows, :].astype(BF16), kf[krows, :].astype(BF16),
                             vf[krows, :].astype(BF16), dof[qrows, :].astype(BF16),
                             lse_ref[qrows, :], dlt[qrows, :]))
            new = []
            for shift, qrows, krows, q, kw, vw, dob, lse_b, dlt_b in work:
                sc = _attn_scores(cfg, bias_s, q, kw, pattern, shift)
                p = jnp.exp(sc - jnp.tile(lse_b, (1, rep)))
                dp = lax.dot_general(dob, vw, _DN["nt"], preferred_element_type=F32)
                ds = (p * (dp - jnp.tile(dlt_b, (1, rep)))).astype(BF16)
                new.append((qrows, krows,
                            jnp.dot(ds, kw, preferred_element_type=F32) * scale,
                            lax.dot_general(ds, q, _DN["tn"], preferred_element_type=F32),
                            lax.dot_general(p.astype(BF16), dob, _DN["tn"], preferred_element_type=F32)))
            for qrows, krows, dq_b, dk_b, dv_b in new:
                dqa[qrows, :] += dq_b
                dka[krows, :] += dk_b
                dva[krows, :] += dv_b

        _for_each_group(cfg, group)

        @pl.when(head > 0)
        def _():
            for slot in (1, 2, 3):
                out_copy(slot).wait()

        def emit(i, carry):
            rows = pl.ds(pl.multiple_of(i * _COPY_ROWS, _COPY_ROWS), _COPY_ROWS)
            stage[1, rows, :] = dqa[rows, :].astype(BF16)
            stage[2, rows, :] = dka[rows, :].astype(BF16)
            stage[3, rows, :] = dva[rows, :].astype(BF16)
            return carry

        lax.fori_loop(0, s // _COPY_ROWS, emit, 0)
        for slot in (1, 2, 3):
            out_copy(slot).start()

        @pl.when(head == pl.num_programs(0) - 1)
        def _():
            for slot in range(4):
                out_copy(slot).wait()

    def col(off):
        return pl.BlockSpec((s, HEAD_DIM), lambda h: (0, off // HEAD_DIM + h))

    head_cols = pl.BlockSpec((s, HEAD_DIM), lambda h: (0, h))
    return pl.pallas_call(
        body, name=name, grid=(cfg.ha,),
        in_specs=[pl.BlockSpec(memory_space=pltpu.SMEM), col(cfg.qa), col(cfg.ka), col(cfg.va), col(cfg.za),
                  head_cols, head_cols, head_cols],
        out_specs=_ANY,
        out_shape=jax.ShapeDtypeStruct((s, cfg.f), BF16),
        scratch_shapes=[pltpu.VMEM((s, HEAD_DIM), F32)] * 8
        + [_bias_scratch(), pltpu.VMEM((4, s, HEAD_DIM), BF16), pltpu.SemaphoreType.DMA((4,))],
        compiler_params=_params(("arbitrary",)),
    )(slopes, proj, proj, proj, proj, dy, oa, lse)


def _decay_tables(lg, backward):
    c = RET_CHUNK
    a = lax.broadcasted_iota(jnp.int32, (c, c), 0)
    b = lax.broadcasted_iota(jnp.int32, (c, c), 1)
    idx = lax.broadcasted_iota(jnp.int32, (c, 1), 0).astype(F32)
    if backward:
        rel = (b - a).astype(F32)
        ex_xi = c - idx
        ex_zeta = idx
    else:
        rel = (a - b).astype(F32)
        ex_xi = idx + 1.0
        ex_zeta = c - 1.0 - idx
    relc = jnp.maximum(rel, 0.0)
    dm = jnp.where(rel >= 0, jnp.exp(relc * lg), 0.0)
    xi = jnp.exp(ex_xi * lg)
    zeta = jnp.exp(ex_zeta * lg)
    gch = jnp.exp(jnp.full((1, 1), c, F32) * lg)
    return relc, dm, xi, zeta, ex_xi, ex_zeta, gch


def _ret_fwd(cfg, proj, lgs, name):
    s = cfg.s
    c = RET_CHUNK
    n = s // c
    kscale = RET_QK ** -0.5

    def body(lg_ref, q_ref, k_ref, v_ref, o_ref, st_ref):
        h = pl.program_id(0)
        tabs = [_decay_tables(lg_ref[dirn, h], dirn == 1) for dirn in range(2)]
        st_ref[...] = jnp.zeros_like(st_ref)
        o_ref[...] = jnp.zeros_like(o_ref)

        def step(t, carry):
            for dirn in range(2):
                _, dm, xi, zeta, _, _, gch = tabs[dirn]
                i = (n - 1 - t) if dirn == 1 else t
                rows = pl.ds(pl.multiple_of(i * c, c), c)
                qi = q_ref[rows, :]
                ks = k_ref[rows, :].astype(F32) * kscale
                vi = v_ref[rows, :]
                inner = lax.dot_general(qi, ks.astype(BF16), _DN["nt"], preferred_element_type=F32) * dm
                st = st_ref[dirn]
                o_ref[rows, :] += (jnp.dot(inner.astype(BF16), vi, preferred_element_type=F32)
                                   + jnp.dot(qi, st.astype(BF16), preferred_element_type=F32) * xi)
                st_ref[dirn] = st * gch + lax.dot_general((ks * zeta).astype(BF16), vi, _DN["tn"],
                                                          preferred_element_type=F32)
            return carry

        lax.fori_loop(0, n, step, 0, unroll=2)

    return pl.pallas_call(
        body, name=name, grid=(cfg.hr,),
        in_specs=[pl.BlockSpec(memory_space=pltpu.SMEM),
                  pl.BlockSpec((s, RET_QK), lambda h: (0, cfg.qr // RET_QK + h)),
                  pl.BlockSpec((s, RET_QK), lambda h: (0, cfg.kr // RET_QK + h)),
                  pl.BlockSpec((s, RET_V), lambda h: (0, cfg.vr // RET_V + h))],
        out_specs=pl.BlockSpec((s, RET_V), lambda h: (0, h)),
        out_shape=jax.ShapeDtypeStruct((s, cfg.rw), F32),
        scratch_shapes=[pltpu.VMEM((2, RET_QK, RET_V), F32)],
        compiler_params=_params(("parallel",)),
    )(lgs, proj, proj, proj)


def _ret_bwd(cfg, proj, lgs, dy, oret, dproj, name):
    s = cfg.s
    c = RET_CHUNK
    n = s // c
    kscale = RET_QK ** -0.5

    def accumulate(lg_ref, q_ref, k_ref, v_ref, do_ref, dq_ref, dk_ref, dv_ref, dlg_ref, states, t_ref,
                   e_dm, e_xi, e_zeta, e_g):
        h = pl.program_id(0)
        tabs = [_decay_tables(lg_ref[dirn, h], dirn == 1) for dirn in range(2)]
        dlg_ref[...] = jnp.zeros_like(dlg_ref)
        dq_ref[...] = jnp.zeros_like(dq_ref)
        dk_ref[...] = jnp.zeros_like(dk_ref)
        dv_ref[...] = jnp.zeros_like(dv_ref)

        def chunk_rows(dirn, t):
            i = (n - 1 - t) if dirn == 1 else t
            return pl.ds(pl.multiple_of(i * c, c), c)

        t_ref[...] = jnp.zeros_like(t_ref)

        def fwd_step(t, carry):
            for dirn in range(2):
                _, _, _, zeta, _, _, gch = tabs[dirn]
                rows = chunk_rows(dirn, t)
                st = t_ref[dirn]
                states[dirn, t] = st
                ks = k_ref[rows, :].astype(F32) * kscale
                t_ref[dirn] = st * gch + lax.dot_general((ks * zeta).astype(BF16), v_ref[rows, :], _DN["tn"],
                                                         preferred_element_type=F32)
            return carry

        lax.fori_loop(0, n, fwd_step, 0, unroll=2)
        t_ref[...] = jnp.zeros_like(t_ref)

        for ref in (e_dm, e_xi, e_zeta, e_g):
            ref[...] = jnp.zeros_like(ref)

        def bwd_step(u, carry):
            t = n - 1 - u
            for dirn in range(2):
                relc, dm, xi, zeta, ex_xi, ex_zeta, gch = tabs[dirn]
                rows = chunk_rows(dirn, t)
                qi = q_ref[rows, :]
                ks = k_ref[rows, :].astype(F32) * kscale
                ksb = ks.astype(BF16)
                vi = v_ref[rows, :]
                doi = do_ref[rows, :]
                sn_f = states[dirn, t]
                sn = sn_f.astype(BF16)
                tt = t_ref[dirn]
                ttb = tt.astype(BF16)
                a_mat = lax.dot_general(qi, ksb, _DN["nt"], preferred_element_type=F32) * dm
                dov = lax.dot_general(doi, vi, _DN["nt"], preferred_element_type=F32)
                b_mat = (dov * dm).astype(BF16)
                kz = (ks * zeta).astype(BF16)
                d_v = (jnp.dot(kz, ttb, preferred_element_type=F32)
                       + lax.dot_general(a_mat.astype(BF16), doi, _DN["tn"], preferred_element_type=F32))
                dk_inter = lax.dot_general(vi, ttb, _DN["nt"], preferred_element_type=F32) * zeta
                d_k = lax.dot_general(b_mat, qi, _DN["tn"], preferred_element_type=F32) + dk_inter
                o_inter = jnp.dot(qi, sn, preferred_element_type=F32) * xi
                d_q = (jnp.dot(b_mat, ksb, preferred_element_type=F32)
                       + lax.dot_general(doi, sn, _DN["nt"], preferred_element_type=F32) * xi)
                e_dm[dirn] += relc * a_mat * dov
                e_xi[dirn] += ex_xi * (doi.astype(F32) * o_inter)
                e_zeta[dirn] += ex_zeta * (ks * dk_inter)
                e_g[dirn] += tt * sn_f
                t_ref[dirn] = tt * gch + lax.dot_general((qi.astype(F32) * xi).astype(BF16), doi, _DN["tn"],
                                                         preferred_element_type=F32)
                dq_ref[rows, :] += d_q
                dk_ref[rows, :] += d_k * kscale
                dv_ref[rows, :] += d_v
            return carry

        lax.fori_loop(0, n, bwd_step, 0, unroll=2)
        for dirn in range(2):
            total = (jnp.sum(e_dm[dirn], keepdims=True) + jnp.sum(e_xi[dirn], keepdims=True)
                     + jnp.sum(e_zeta[dirn], keepdims=True) + (c * tabs[dirn][6]) * jnp.sum(e_g[dirn], keepdims=True))
            dlg_ref[0, dirn:dirn + 1, :] = jnp.broadcast_to(total, (1, LANES))

    def body(lg_ref, q_ref, k_ref, v_ref, z_ref, dy_ref, or_ref, dproj_in, dproj_ref, dlg_ref,
             do_s, dq_s, dk_s, dv_s, stage_z, stage_q, stage_k, stage_v, sems, *scratch):
        head = pl.program_id(0)
        stages = (stage_z, stage_q, stage_k, stage_v)
        firsts = (cfg.zr, cfg.qr, cfg.kr, cfg.vr)

        def out_copy(slot):
            width = stages[slot].shape[1]
            cols = pl.ds(pl.multiple_of(firsts[slot] + head * width, LANES), width)
            return pltpu.make_async_copy(stages[slot], dproj_ref.at[:, cols], sems.at[slot])

        @pl.when(head > 0)
        def _():
            out_copy(0).wait()

        def gate_norm_bwd(i, carry):
            rows = pl.ds(pl.multiple_of(i * _COPY_ROWS, _COPY_ROWS), _COPY_ROWS)
            oh = or_ref[rows, :]
            zr = z_ref[rows, :].astype(F32)
            dyr = dy_ref[rows, :].astype(F32)
            rr = lax.rsqrt(jnp.mean(oh * oh, axis=-1, keepdims=True) + NORM_EPS)
            yn = oh * rr
            dyn = dyr * _silu(zr)
            stage_z[rows, :] = (dyr * yn * _dsilu(zr)).astype(BF16)
            do_s[rows, :] = (rr * (dyn - yn * jnp.mean(dyn * yn, axis=-1, keepdims=True))).astype(BF16)
            return carry

        lax.fori_loop(0, s // _COPY_ROWS, gate_norm_bwd, 0)
        out_copy(0).start()

        accumulate(lg_ref, q_ref, k_ref, v_ref, do_s, dq_s, dk_s, dv_s, dlg_ref, *scratch)

        @pl.when(head > 0)
        def _():
            for slot in (1, 2, 3):
                out_copy(slot).wait()

        def emit(i, carry):
            rows = pl.ds(pl.multiple_of(i * _COPY_ROWS, _COPY_ROWS), _COPY_ROWS)
            stage_q[rows, :] = dq_s[rows, :].astype(BF16)
            stage_k[rows, :] = dk_s[rows, :].astype(BF16)
            stage_v[rows, :] = dv_s[rows, :].astype(BF16)
            return carry

        lax.fori_loop(0, s // _COPY_ROWS, emit, 0)
        for slot in (1, 2, 3):
            out_copy(slot).start()

        @pl.when(head == pl.num_programs(0) - 1)
        def _():
            for slot in range(4):
                out_copy(slot).wait()

    return pl.pallas_call(
        body, name=name, grid=(cfg.hr,),
        in_specs=[pl.BlockSpec(memory_space=pltpu.SMEM),
                  pl.BlockSpec((s, RET_QK), lambda h: (0, cfg.qr // RET_QK + h)),
                  pl.BlockSpec((s, RET_QK), lambda h: (0, cfg.kr // RET_QK + h)),
                  pl.BlockSpec((s, RET_V), lambda h: (0, cfg.vr // RET_V + h)),
                  pl.BlockSpec((s, RET_V), lambda h: (0, cfg.zr // RET_V + h)),
                  pl.BlockSpec((s, RET_V), lambda h: (0, cfg.aw // RET_V + h)),
                  pl.BlockSpec((s, RET_V), lambda h: (0, h)), _ANY],
        out_specs=[_ANY, pl.BlockSpec((1, 8, LANES), lambda h: (h, 0, 0))],
        out_shape=[jax.ShapeDtypeStruct(dproj.shape, BF16), jax.ShapeDtypeStruct((cfg.hr, 8, LANES), F32)],
        input_output_aliases={7: 0},
        scratch_shapes=[pltpu.VMEM((s, RET_V), BF16),
                        pltpu.VMEM((s, RET_QK), F32), pltpu.VMEM((s, RET_QK), F32), pltpu.VMEM((s, RET_V), F32),
                        pltpu.VMEM((s, RET_V), BF16), pltpu.VMEM((s, RET_QK), BF16), pltpu.VMEM((s, RET_QK), BF16),
                        pltpu.VMEM((s, RET_V), BF16), pltpu.SemaphoreType.DMA((4,)),
                        pltpu.VMEM((2, n, RET_QK, RET_V), F32), pltpu.VMEM((2, RET_QK, RET_V), F32),
                        pltpu.VMEM((2, c, c), F32), pltpu.VMEM((2, c, RET_V), F32),
                        pltpu.VMEM((2, c, RET_QK), F32), pltpu.VMEM((2, RET_QK, RET_V), F32)],
        compiler_params=_params(("arbitrary",)),
    )(lgs, proj, proj, proj, proj, dy, oret, dproj)


def _ada_fwd(cact16, w_ada, name):
    depth, d, n = w_ada.shape
    tn = _tile(n, 768)

    def body(c_ref, w_ref, o_ref):
        o_ref[0] = jnp.dot(c_ref[...], w_ref[0].astype(BF16), preferred_element_type=F32)

    return pl.pallas_call(
        body, name=name, grid=(depth, n // tn),
        in_specs=[pl.BlockSpec((16, d), lambda l, j: (0, 0)), pl.BlockSpec((1, d, tn), lambda l, j: (l, 0, j))],
        out_specs=pl.BlockSpec((1, 16, tn), lambda l, j: (l, 0, j)),
        out_shape=jax.ShapeDtypeStruct((depth, 16, n), F32),
        compiler_params=_params(("parallel", "parallel")),
    )(cact16, w_ada)


def _adam_math(w, g, m, v):
    m2 = ADAM_B1 * m + (1.0 - ADAM_B1) * g
    v2 = ADAM_B2 * v + (1.0 - ADAM_B2) * (g * g)
    m_hat = m2 / (1.0 - ADAM_B1 ** ADAM_STEP)
    v_hat = v2 / (1.0 - ADAM_B2 ** ADAM_STEP)
    delta = -ADAM_LR * (m_hat / (jnp.sqrt(v_hat) + ADAM_EPS) + ADAM_WD * w)
    return delta, m2, v2


def _adamw_big(w, g, m, v, first, count, prev, name):
    _, r, c = w.shape
    tr, tc = _tile(r, 512), _tile(c, 1024)

    def body(w_ref, g_ref, m_ref, v_ref, *rest):
        go_ref, d_ref, mo_ref, vo_ref = rest[-4:]
        gv = g_ref[...]
        delta, m2, v2 = _adam_math(w_ref[...], gv, m_ref[...], v_ref[...])
        go_ref[...] = gv
        d_ref[...] = delta
        mo_ref[...] = m2
        vo_ref[...] = v2

    spec = pl.BlockSpec((1, tr, tc), lambda l, i, j: (first + l, i, j))
    shp = jax.ShapeDtypeStruct(w.shape, F32)
    carried = [] if prev is None else list(prev)
    return pl.pallas_call(
        body, name=name, grid=(count, r // tr, c // tc),
        in_specs=[spec] * 4 + [_ANY] * len(carried), out_specs=[spec] * 4, out_shape=[shp] * 4,
        input_output_aliases={4 + k: k for k in range(len(carried))},
        compiler_params=_params(("parallel", "parallel", "parallel")),
    )(w, g, m, v, *carried)


def _adamw_ada(w, m, v, cact128, dmod128, name):
    depth, r, c = w.shape
    tr, tc = _tile(r, 512), _tile(c, 768)

    def body(w_ref, m_ref, v_ref, c_ref, dm_ref, go_ref, d_ref, mo_ref, vo_ref):
        gv = lax.dot_general(c_ref[...], dm_ref[0], _DN["tn"], preferred_element_type=F32)
        delta, m2, v2 = _adam_math(w_ref[0], gv, m_ref[0], v_ref[0])
        go_ref[0] = gv
        d_ref[0] = delta
        mo_ref[0] = m2
        vo_ref[0] = v2

    spec = pl.BlockSpec((1, tr, tc), lambda l, i, j: (l, i, j))
    shp = jax.ShapeDtypeStruct(w.shape, F32)
    return pl.pallas_call(
        body, name=name, grid=(depth, r // tr, c // tc),
        in_specs=[spec] * 3 + [pl.BlockSpec((128, tr), lambda l, i, j: (0, i)),
                               pl.BlockSpec((1, 128, tc), lambda l, i, j: (l, 0, j))],
        out_specs=[spec] * 4, out_shape=[shp] * 4,
        compiler_params=_params(("parallel", "parallel", "parallel")),
    )(w, m, v, cact128, dmod128)


def _adamw_small(w, g, m, v, name):
    def body(w_ref, g_ref, m_ref, v_ref, d_ref, mo_ref, vo_ref):
        delta, m2, v2 = _adam_math(w_ref[...], g_ref[...], m_ref[...], v_ref[...])
        d_ref[...] = delta
        mo_ref[...] = m2
        vo_ref[...] = v2

    shp = jax.ShapeDtypeStruct(w.shape, F32)
    return pl.pallas_call(body, name=name, out_shape=[shp] * 3)(w, g, m, v)


def _sum_gathered(parts, name):
    nd, r, c = parts.shape

    def body(p_ref, o_ref):
        acc = p_ref[0]
        for e in range(1, nd):
            acc = acc + p_ref[e]
        o_ref[...] = acc

    return pl.pallas_call(body, name=name, out_shape=jax.ShapeDtypeStruct((r, c), F32))(parts)


def _flip(v, bit):
    return 1 - v if bit else v


def _all_gather_small(x, name):
    r, c = x.shape

    def body(x_ref, out_ref, send_sems, recv_sems, local_sem):
        mx, my, mc = lax.axis_index("x"), lax.axis_index("y"), lax.axis_index("c")
        me = 4 * mx + 2 * my + mc
        mine = pltpu.make_async_copy(x_ref, out_ref.at[me], local_sem)
        mine.start()
        sends = []
        for k in range(1, N_DEV):
            peer = (_flip(mx, k & 4), _flip(my, k & 2), _flip(mc, k & 1))
            cp = pltpu.make_async_remote_copy(src_ref=x_ref, dst_ref=out_ref.at[me], send_sem=send_sems.at[k - 1],
                                              recv_sem=recv_sems.at[k - 1], device_id=peer, device_id_type=MESH)
            cp.start()
            sends.append(cp)
        for k in range(1, N_DEV):
            peer = (_flip(mx, k & 4), _flip(my, k & 2), _flip(mc, k & 1))
            src = 4 * peer[0] + 2 * peer[1] + peer[2]
            pltpu.make_async_remote_copy(src_ref=x_ref, dst_ref=out_ref.at[src], send_sem=send_sems.at[k - 1],
                                         recv_sem=recv_sems.at[k - 1], device_id=peer,
                                         device_id_type=MESH).wait_recv()
        for cp in sends:
            cp.wait_send()
        mine.wait()

    return pl.pallas_call(
        body, name=name,
        out_shape=jax.ShapeDtypeStruct((N_DEV, r, c), x.dtype),
        in_specs=[pl.BlockSpec(memory_space=pltpu.VMEM)],
        out_specs=pl.BlockSpec(memory_space=pltpu.VMEM),
        scratch_shapes=[pltpu.SemaphoreType.DMA((N_DEV - 1,)), pltpu.SemaphoreType.DMA((N_DEV - 1,)),
                        pltpu.SemaphoreType.DMA],
        compiler_params=pltpu.CompilerParams(vmem_limit_bytes=VMEM_LIMIT),
    )(x)


_HBM = pl.BlockSpec(memory_space=pltpu.HBM)
_SEM = pl.BlockSpec(memory_space=pltpu.SEMAPHORE)
_ANY = pl.BlockSpec(memory_space=pl.ANY)
_EFFECT = pltpu.SideEffectType.DATAFLOW_SIDE_EFFECTING


def _in_hbm(a):
    return pltpu.with_memory_space_constraint(a, pltpu.HBM)


def _place_w_in(w, layer, chip1, name):
    _, d, fc = w.shape
    tr = _tile(d, 512)

    def body(c_ref, w_ref, o_ref):
        o_ref[...] = w_ref[...].astype(BF16)

    return pl.pallas_call(
        body, name=name,
        grid_spec=pltpu.PrefetchScalarGridSpec(
            num_scalar_prefetch=1, grid=(d // tr,),
            in_specs=[pl.BlockSpec((None, tr, fc), lambda i, c: (layer, i, 0))],
            out_specs=pl.BlockSpec((tr, fc), lambda i, c: (i, c[0]))),
        out_shape=jax.ShapeDtypeStruct((d, N_CHIP * fc), BF16),
        compiler_params=_params(("parallel",)),
    )(chip1, w)


def _place_w_out(w, layer, chip1, name):
    _, rc, dd = w.shape
    tc = _tile(dd, 1024)

    def body(c_ref, w_ref, o_ref):
        o_ref[...] = w_ref[...].astype(BF16)

    return pl.pallas_call(
        body, name=name,
        grid_spec=pltpu.PrefetchScalarGridSpec(
            num_scalar_prefetch=1, grid=(dd // tc,),
            in_specs=[pl.BlockSpec((None, rc, tc), lambda j, c: (layer, 0, j))],
            out_specs=pl.BlockSpec((rc, tc), lambda j, c: (c[0], j))),
        out_shape=jax.ShapeDtypeStruct((N_CHIP * rc, dd), BF16),
        compiler_params=_params(("parallel",)),
    )(chip1, w)


def _weight_region(ref, tensor, chip):
    if tensor == 0:
        fc = ref.shape[1] // N_CHIP
        return ref.at[:, pl.ds(pl.multiple_of(chip * fc, LANES), fc)]
    rc = ref.shape[0] // N_CHIP
    return ref.at[pl.ds(pl.multiple_of(chip * rc, 8), rc), :]


def _gather_copies(ref, tensor, send_sems, recv_sems, landing):
    mx, my, mc = lax.axis_index("x"), lax.axis_index("y"), lax.axis_index("c")
    mine = _weight_region(ref, tensor, 2 * mx + my)
    copies = []
    for j in range(1, N_CHIP):
        peer = (_flip(mx, j & 2), _flip(my, j & 1), mc)
        dst = _weight_region(ref, tensor, 2 * peer[0] + peer[1]) if landing else mine
        idx = 2 * (j - 1) + tensor
        copies.append(pltpu.make_async_remote_copy(src_ref=mine, dst_ref=dst, send_sem=send_sems.at[idx],
                                                   recv_sem=recv_sems.at[idx], device_id=peer, device_id_type=MESH))
    return copies


def _gather_start(fi, fo, dep, name):
    ns = 2 * (N_CHIP - 1)

    def body(fi_ref, fo_ref, dep_ref, send_sems, recv_sems, fi_thru, fo_thru, token):
        for tensor, ref in enumerate((fi_ref, fo_ref)):
            for cp in _gather_copies(ref, tensor, send_sems, recv_sems, landing=False):
                cp.start()
        token[...] = jnp.zeros_like(token)

    return pl.pallas_call(
        body, name=name,
        out_shape=(pltpu.SemaphoreType.DMA((ns,)), pltpu.SemaphoreType.DMA((ns,)),
                   pltpu.HBM(fi.shape, fi.dtype), pltpu.HBM(fo.shape, fo.dtype),
                   jax.ShapeDtypeStruct((8, LANES), F32)),
        in_specs=(_HBM, _HBM, _ANY),
        out_specs=(_SEM, _SEM, _HBM, _HBM, pl.BlockSpec(memory_space=pltpu.VMEM)),
        input_output_aliases={0: 2, 1: 3},
        compiler_params=pltpu.CompilerParams(has_side_effects=_EFFECT),
    )(_in_hbm(fi), _in_hbm(fo), dep)


def _gather_wait(send_sems, recv_sems, buf, tensor, after, name):
    def body(buf_ref, send_sems, recv_sems, after_ref, buf_out):
        for cp in _gather_copies(buf_ref, tensor, send_sems, recv_sems, landing=True):
            cp.wait_send()
            cp.wait_recv()

    return pl.pallas_call(
        body, name=name,
        out_shape=pltpu.HBM(buf.shape, buf.dtype),
        in_specs=(_HBM, _SEM, _SEM, _ANY), out_specs=_HBM,
        input_output_aliases={0: 0},
        compiler_params=pltpu.CompilerParams(has_side_effects=_EFFECT),
    )(buf, send_sems, recv_sems, after)


def _scatter_copies(gi_ref, go_ref, pi_ref, po_ref, send_sems, recv_sems):
    mx, my, mc = lax.axis_index("x"), lax.axis_index("y"), lax.axis_index("c")
    hr, fc = pi_ref.shape[1:]
    ro = po_ref.shape[1]
    copies = []
    for k in range(1, N_DEV):
        peer = (_flip(mx, k & 4), _flip(my, k & 2), _flip(mc, k & 1))
        pchip = 2 * peer[0] + peer[1]
        src = (gi_ref.at[pl.ds(pl.multiple_of(peer[2] * hr, 8), hr), pl.ds(pl.multiple_of(pchip * fc, LANES), fc)],
               go_ref.at[pl.ds(pl.multiple_of((2 * pchip + peer[2]) * ro, 8), ro), :])
        dst = (pi_ref.at[k - 1], po_ref.at[k - 1])
        for t in range(2):
            idx = 2 * (k - 1) + t
            copies.append(pltpu.make_async_remote_copy(src_ref=src[t], dst_ref=dst[t], send_sem=send_sems.at[idx],
                                                       recv_sem=recv_sems.at[idx], device_id=peer,
                                                       device_id_type=MESH))
    return copies


def _scatter_start(gi, go, name):
    d, f = gi.shape
    dd = go.shape[1]
    ns = 2 * (N_DEV - 1)
    pi = lax.empty((N_DEV - 1, d // 2, f // N_CHIP), BF16)
    po = lax.empty((N_DEV - 1, d // N_DEV, dd), BF16)

    def body(gi_ref, go_ref, pi_ref, po_ref, send_sems, recv_sems, gi_thru, go_thru, pi_thru, po_thru, token):
        for cp in _scatter_copies(gi_ref, go_ref, pi_ref, po_ref, send_sems, recv_sems):
            cp.start()
        token[...] = jnp.zeros_like(token)

    return pl.pallas_call(
        body, name=name,
        out_shape=(pltpu.SemaphoreType.DMA((ns,)), pltpu.SemaphoreType.DMA((ns,)),
                   pltpu.HBM(gi.shape, gi.dtype), pltpu.HBM(go.shape, go.dtype),
                   pltpu.HBM(pi.shape, pi.dtype), pltpu.HBM(po.shape, po.dtype),
                   jax.ShapeDtypeStruct((8, LANES), F32)),
        in_specs=(_HBM, _HBM, _HBM, _HBM),
        out_specs=(_SEM, _SEM, _HBM, _HBM, _HBM, _HBM, pl.BlockSpec(memory_space=pltpu.VMEM)),
        input_output_aliases={0: 2, 1: 3, 2: 4, 3: 5},
        compiler_params=pltpu.CompilerParams(has_side_effects=_EFFECT),
    )(_in_hbm(gi), _in_hbm(go), _in_hbm(pi), _in_hbm(po))


def _scatter_wait(send_sems, recv_sems, gi, go, pi, po, after, name):
    def body(gi_ref, go_ref, pi_ref, po_ref, send_sems, recv_sems, *rest):
        for cp in _scatter_copies(gi_ref, go_ref, pi_ref, po_ref, send_sems, recv_sems):
            cp.wait_send()
            cp.wait_recv()

    return pl.pallas_call(
        body, name=name,
        out_shape=tuple(pltpu.HBM(a.shape, a.dtype) for a in (gi, go, pi, po)),
        in_specs=(_HBM, _HBM, _HBM, _HBM, _SEM, _SEM) + (_ANY,) * len(after), out_specs=(_HBM, _HBM, _HBM, _HBM),
        input_output_aliases={0: 0, 1: 1, 2: 2, 3: 3},
        compiler_params=pltpu.CompilerParams(has_side_effects=_EFFECT),
    )(gi, go, pi, po, send_sems, recv_sems, *after)


def _sum_into(buf, g, parts, where2, layer, row_blocks, dep, name):
    depth, r2, c = buf.shape
    r = r2 // 2
    tr, tc = _tile(r, 256), _tile(c, 1024)
    nr, nc = r // tr, c // tc
    col_blocks = (g.shape[1] // c) > 1

    def body(w_ref, buf_ref, g_ref, p_ref, dep_ref, o_ref):
        acc = g_ref[...].astype(F32)
        for e in range(N_DEV - 1):
            acc = acc + p_ref[e].astype(F32)
        o_ref[...] = acc

    return pl.pallas_call(
        body, name=name,
        grid_spec=pltpu.PrefetchScalarGridSpec(
            num_scalar_prefetch=1, grid=(nr, nc),
            in_specs=[_ANY,
                      pl.BlockSpec((tr, tc), lambda i, j, w: (row_blocks(w, nr) + i, (w[1] * nc if col_blocks else 0) + j)),
                      pl.BlockSpec((N_DEV - 1, tr, tc), lambda i, j, w: (0, i, j)), _ANY],
            out_specs=pl.BlockSpec((None, tr, tc), lambda i, j, w: (layer, w[0] * nr + i, j))),
        out_shape=jax.ShapeDtypeStruct(buf.shape, F32),
        input_output_aliases={1: 0},
        compiler_params=_params(("parallel", "parallel")),
    )(where2, buf, g, parts, dep)


def _exchange_halves(b_in, b_out, first, count, name):
    r_in = b_in.shape[1]
    r_out = b_out.shape[1]

    def body(bi_ref, bo_ref, oi_ref, oo_ref, send_sems, recv_sems):
        mx, my, mc = lax.axis_index("x"), lax.axis_index("y"), lax.axis_index("c")
        sib = (mx, my, 1 - mc)

        def half(ref, l, rows, which):
            return ref.at[l, pl.ds(pl.multiple_of(which * (rows // 2), 8), rows // 2), :]

        copies = []
        for l in range(first, first + count):
            for t, (src, dst, rows) in enumerate(((bi_ref, oi_ref, r_in), (bo_ref, oo_ref, r_out))):
                idx = 2 * (l - first) + t
                kw = dict(send_sem=send_sems.at[idx], recv_sem=recv_sems.at[idx], device_id=sib, device_id_type=MESH)
                cp = pltpu.make_async_remote_copy(src_ref=half(src, l, rows, mc), dst_ref=half(dst, l, rows, mc), **kw)
                cp.start()
                copies.append((cp, pltpu.make_async_remote_copy(src_ref=half(src, l, rows, mc),
                                                                dst_ref=half(dst, l, rows, 1 - mc), **kw)))
        for cp, landed in copies:
            landed.wait_recv()
        for cp, landed in copies:
            cp.wait_send()

    ns = 2 * count
    return pl.pallas_call(
        body, name=name,
        out_shape=[jax.ShapeDtypeStruct(b_in.shape, F32), jax.ShapeDtypeStruct(b_out.shape, F32)],
        in_specs=[_ANY, _ANY], out_specs=[_ANY, _ANY],
        input_output_aliases={0: 0, 1: 1},
        scratch_shapes=[pltpu.SemaphoreType.DMA((ns,)), pltpu.SemaphoreType.DMA((ns,))],
    )(b_in, b_out)


def _rows8(v):
    return jnp.pad(v, ((0, 8 - v.shape[0]), (0, 0)))


def kernel(x, c, norm_gain, w_ada, b_ada, w_in, w_out, ret_decay_logit_f, ret_decay_logit_b, final_gain, loss_target, m_norm_gain, m_w_ada, m_b_ada, m_w_in, m_w_out, m_ret_decay_logit_f, m_ret_decay_logit_b, m_final_gain, v_norm_gain, v_w_ada, v_b_ada, v_w_in, v_w_out, v_ret_decay_logit_f, v_ret_decay_logit_b, v_final_gain):
    cfg = _Cfg()
    depth, d = norm_gain.shape
    mx, my, mc = lax.axis_index("x"), lax.axis_index("y"), lax.axis_index("c")
    me = 4 * mx + 2 * my + mc
    chip = 2 * mx + my
    x0 = x[0]
    tgt = loss_target[0]
    ada_cols = w_ada.shape[2]

    c_all = _all_gather_small(_rows8(c), "gather_c")[:, 0, :]
    cact = _silu(c_all)
    mod_part = _ada_fwd(jnp.pad(cact, ((0, 8), (0, 0))).astype(BF16), w_ada, "ada_fwd")
    mod_all = _all_gather_small(mod_part.reshape(depth * 16, ada_cols), "gather_mod")
    mod_all = mod_all.reshape(N_CHIP, 2, depth, 16, ada_cols)[:, 0]
    mod_mine = lax.dynamic_index_in_dim(mod_all, me, axis=2, keepdims=False)
    mod = jnp.transpose(mod_mine, (1, 0, 2)).reshape(depth, 3, d)
    bias = b_ada.reshape(depth, 3, d)

    chip1 = jnp.reshape(chip, (1,)).astype(jnp.int32)
    where2 = jnp.stack([mc, chip]).astype(jnp.int32)

    def start_gather(l, dep):
        return _gather_start(_place_w_in(w_in, l, chip1, f"place_w_in_{l}"),
                             _place_w_out(w_out, l, chip1, f"place_w_out_{l}"), dep, f"gather_start_{l}")

    slopes = jnp.exp2(-8.0 * (jnp.arange(cfg.ha, dtype=F32) + 1.0) / cfg.ha)
    lg_f = jax.nn.log_sigmoid(ret_decay_logit_f)
    lg_b = jax.nn.log_sigmoid(ret_decay_logit_b)

    saved = []
    w_full = []
    h_x = x0
    pending = start_gather(0, c)
    for l in range(depth):
        send_sems, recv_sems, fi, fo, _ = pending
        w_in_l = _gather_wait(send_sems, recv_sems, fi, 0, mod if l == 0 else h_x, f"gather_wait_in_{l}")
        g8 = _rows8(norm_gain[l:l + 1])
        if l + 1 < depth:
            pending = start_gather(l + 1, w_in_l)
            g8 = g8 + pending[4][0:1, 0:1]
        mod3, b3 = _rows8(mod[l]), _rows8(bias[l])
        hb = _norm_mod_fwd(h_x, g8, mod3, b3, f"norm_mod_fwd_{l}")
        proj = _matmul(hb, w_in_l, "nn", BF16, f"in_proj_{l}")
        oa, lse = _attn_fwd(cfg, proj, slopes, f"attn_fwd_{l}")
        lgs = jnp.stack([lg_f[l], lg_b[l]])
        oret = _ret_fwd(cfg, proj, lgs, f"ret_fwd_{l}")
        y = _gate_fwd(cfg, oa, proj, oret, f"gate_fwd_{l}")
        w_out_l = _gather_wait(send_sems, recv_sems, fo, 1, y, f"gather_wait_out_{l}")
        w_full.append((w_in_l, w_out_l))
        x_next, out = _out_proj_fwd(y, w_out_l, h_x, mod3, b3, f"out_proj_{l}")
        saved.append((h_x, hb, proj, oret, y, oa, lse, out, g8, mod3, b3, lgs))
        h_x = x_next

    dx, loss8, fin_acc = _final_loss(h_x, tgt, _rows8(final_gain[None]), "final_loss")

    landed = [None] * depth
    d_mod, d_gain, d_lg = [None] * depth, [None] * depth, [None] * depth
    in_flight = None
    for l in reversed(range(depth)):
        x_l, hb, proj, oret, y, oa, lse, out, g8, mod3, b3, lgs = saved[l]
        w_in_l, w_out_l = w_full[l]
        douts, gate_acc = _out_proj_bwd_prep(dx, out, mod3, b3, f"out_proj_bwd_prep_{l}")
        dy = _matmul(douts, w_out_l, "nt", BF16, f"out_proj_dy_{l}")
        g_out_l = _matmul(y, douts, "tn", BF16, f"out_proj_dw_{l}", tk=4096)
        dproj = _attn_bwd(cfg, proj, slopes, dy, oa, lse, f"attn_bwd_{l}")
        dproj, dlg = _ret_bwd(cfg, proj, lgs, dy, oret, dproj, f"ret_bwd_{l}")
        g_in_l = _matmul(hb, dproj, "tn", BF16, f"in_proj_dw_{l}", tk=4096)
        started = _scatter_start(g_in_l, g_out_l, f"scatter_start_{l}")
        dh = _matmul(dproj, w_in_l, "nt", F32, f"in_proj_dh_{l}", tk=3584)
        g8 = g8 + started[-1][0:1, 0:1]
        dx, nm_acc = _norm_mod_bwd(dh, x_l, dx, g8, mod3, b3, f"norm_mod_bwd_{l}")
        d_mod[l] = jnp.concatenate([nm_acc[0], nm_acc[1], gate_acc[0]])
        d_gain[l] = nm_acc[2]
        d_lg[l] = dlg[:, 0:2, 0]
        if in_flight is not None:
            landed[l + 1] = _scatter_wait(*in_flight[:-1], (dx,), f"scatter_wait_{l + 1}")
        in_flight = started

    gw_in = lax.empty(w_in.shape, F32)
    gw_out = lax.empty(w_out.shape, F32)
    res_in = res_out = None
    for first, count in ((1, depth - 1), (0, 1)):
        if first == 0:
            after = (dx,) if res_out is None else (res_in[1], res_out[1])
            landed[0] = _scatter_wait(*in_flight[:-1], after, "scatter_wait_0")
        if count == 0:
            continue
        dep = in_flight[-1]
        for l in range(first, first + count):
            gi, go, pi, po = landed[l]
            gw_in = _sum_into(gw_in, gi, pi, where2, l, lambda w, nr: w[0] * nr, dep, f"sum_w_in_{l}")
            gw_out = _sum_into(gw_out, go, po, where2, l, lambda w, nr: (2 * w[1] + w[0]) * nr, dep,
                               f"sum_w_out_{l}")
        gw_in, gw_out = _exchange_halves(gw_in, gw_out, first, count, f"exchange_halves_{first}")
        res_in = _adamw_big(w_in, gw_in, m_w_in, v_w_in, first, count, res_in, f"adamw_w_in_{first}")
        res_out = _adamw_big(w_out, gw_out, m_w_out, v_w_out, first, count, res_out, f"adamw_w_out_{first}")
    grad_w_in, delta_w_in, new_m_w_in, new_v_w_in = res_in
    grad_w_out, delta_w_out, new_m_w_out, new_v_w_out = res_out

    dmod_mine = jnp.stack(d_mod)
    dmod_gathered = _all_gather_small(_rows8(dmod_mine), "gather_dmod")
    dmod_all = dmod_gathered[:, :depth, :]
    grad_b_ada = _sum_gathered(dmod_gathered, "sum_b_ada")[:depth]
    dmod_cols = lax.dynamic_slice_in_dim(dmod_all, chip * ada_cols, ada_cols, axis=2)
    dmod128 = jnp.pad(jnp.transpose(dmod_cols, (1, 0, 2)), ((0, 0), (0, 120), (0, 0))).astype(BF16)
    cact128 = jnp.pad(cact, ((0, 120), (0, 0))).astype(BF16)
    grad_w_ada, delta_w_ada, new_m_w_ada, new_v_w_ada = _adamw_ada(w_ada, m_w_ada, v_w_ada, cact128, dmod128,
                                                                  "adamw_w_ada")

    dlg_all = jnp.stack(d_lg)
    sig_f = jax.nn.sigmoid(-ret_decay_logit_f)
    sig_b = jax.nn.sigmoid(-ret_decay_logit_b)
    nlg = depth * cfg.hr
    pack = jnp.zeros((8, d), F32)
    for l in range(depth):
        pack = pack.at[l].set(d_gain[l])
    pack = pack.at[depth].set(fin_acc[0])
    pack = pack.at[depth + 1, 0].set(loss8[0, 0])
    pack = pack.at[depth + 1, LANES:LANES + nlg].set((dlg_all[:, :, 0] * sig_f).reshape(-1))
    pack = pack.at[depth + 1, 2 * LANES:2 * LANES + nlg].set((dlg_all[:, :, 1] * sig_b).reshape(-1))
    tot = _sum_gathered(_all_gather_small(pack, "gather_small"), "sum_small")
    grad_norm_gain = tot[:depth]
    grad_final_gain = tot[depth]
    loss = tot[depth + 1, 0]
    grad_lf = tot[depth + 1, LANES:LANES + nlg].reshape(depth, cfg.hr)
    grad_lb = tot[depth + 1, 2 * LANES:2 * LANES + nlg].reshape(depth, cfg.hr)

    d_ng, m_ng, v_ng = _adamw_small(norm_gain, grad_norm_gain, m_norm_gain, v_norm_gain, "adamw_norm_gain")
    d_ba, m_ba, v_ba = _adamw_small(b_ada, grad_b_ada, m_b_ada, v_b_ada, "adamw_b_ada")
    d_lf, m_lf, v_lf = _adamw_small(ret_decay_logit_f, grad_lf, m_ret_decay_logit_f, v_ret_decay_logit_f, "adamw_lf")
    d_lb, m_lb, v_lb = _adamw_small(ret_decay_logit_b, grad_lb, m_ret_decay_logit_b, v_ret_decay_logit_b, "adamw_lb")
    d_fg, m_fg, v_fg = _adamw_small(final_gain[None], grad_final_gain[None], m_final_gain[None], v_final_gain[None],
                                    "adamw_final_gain")

    return (loss, dx[None],
            grad_norm_gain, grad_w_ada, grad_b_ada, grad_w_in, grad_w_out, grad_lf, grad_lb, grad_final_gain,
            d_ng, delta_w_ada, d_ba, delta_w_in, delta_w_out, d_lf, d_lb, d_fg[0],
            m_ng, new_m_w_ada, m_ba, new_m_w_in, new_m_w_out, m_lf, m_lb, m_fg[0],
            v_ng, new_v_w_ada, v_ba, new_v_w_in, new_v_w_out, v_lf, v_lb, v_fg[0])
```

```python
import functools

import jax
import jax.numpy as jnp
from jax import lax
from jax.experimental import pallas as pl
from jax.experimental.pallas import tpu as pltpu

F32 = jnp.float32
BF16 = jnp.bfloat16

D_MODEL = 2048
SEQ = 4096
DEPTH = 4
HEAD_DIM = 128
DILATIONS = (1, 4, 16)
RADIUS = 64
N_HEADS_RET = 4
RET_QK = 128
RET_V = 256
RET_CHUNK = 256
NORM_EPS = 1e-6
MASK_VALUE = -1e30
N_DEV = 8
N_CHIP = 4
LANES = 128
VMEM_LIMIT = 56 * 1024 * 1024

ADAM_LR = 0.001
ADAM_B1 = 0.9
ADAM_B2 = 0.999
ADAM_EPS = 1e-08
ADAM_WD = 0.01
ADAM_STEP = 10

MESH = pl.DeviceIdType.MESH


class _Cfg:
    def __init__(self):
        self.d = D_MODEL
        self.s = SEQ
        self.aw = D_MODEL // 2
        self.ha = self.aw // HEAD_DIM
        self.rw = D_MODEL // 2
        self.hr = N_HEADS_RET
        self.rqk = self.hr * RET_QK
        self.f = 4 * self.aw + 2 * self.rqk + 2 * self.rw
        self.qa, self.ka, self.va, self.za = 0, self.aw, 2 * self.aw, 3 * self.aw
        self.qr = 4 * self.aw
        self.kr = self.qr + self.rqk
        self.vr = self.kr + self.rqk
        self.zr = self.vr + self.rw
        assert self.rw == self.hr * RET_V


def _tile(n, pref):
    t = min(n, pref)
    while n % t or t % LANES:
        t -= LANES
    return t


def _params(dims=None):
    return pltpu.CompilerParams(dimension_semantics=dims, vmem_limit_bytes=VMEM_LIMIT)


def _silu(z):
    return z * jax.nn.sigmoid(z)


def _dsilu(z):
    sg = jax.nn.sigmoid(z)
    return sg * (1.0 + z * (1.0 - sg))


_DN = {"nn": (((1,), (0,)), ((), ())), "nt": (((1,), (1,)), ((), ())), "tn": (((0,), (0,)), ((), ()))}


def _matmul(a, b, mode, out_dtype, name, tm=1024, tn=1024, tk=2048):
    if mode == "tn":
        kk, m = a.shape
    else:
        m, kk = a.shape
    n = b.shape[0] if mode == "nt" else b.shape[1]
    tm, tn, tk = _tile(m, tm), _tile(n, tn), _tile(kk, tk)
    nk = kk // tk
    a_spec = (pl.BlockSpec((tk, tm), lambda i, j, k: (k, i)) if mode == "tn"
              else pl.BlockSpec((tm, tk), lambda i, j, k: (i, k)))
    b_spec = (pl.BlockSpec((tn, tk), lambda i, j, k: (j, k)) if mode == "nt"
              else pl.BlockSpec((tk, tn), lambda i, j, k: (k, j)))
    dn = _DN[mode]

    def body(a_ref, b_ref, o_ref, *acc):
        p = lax.dot_general(a_ref[...], b_ref[...], dn, preferred_element_type=F32)
        if nk == 1:
            o_ref[...] = p.astype(out_dtype)
            return
        acc_ref, = acc
        k = pl.program_id(2)

        @pl.when(k == 0)
        def _():
            acc_ref[...] = p

        @pl.when(k > 0)
        def _():
            acc_ref[...] += p

        @pl.when(k == nk - 1)
        def _():
            o_ref[...] = acc_ref[...].astype(out_dtype)

    return pl.pallas_call(
        body, name=name, grid=(m // tm, n // tn, nk),
        in_specs=[a_spec, b_spec],
        out_specs=pl.BlockSpec((tm, tn), lambda i, j, k: (i, j)),
        out_shape=jax.ShapeDtypeStruct((m, n), out_dtype),
        scratch_shapes=[pltpu.VMEM((tm, tn), F32)] if nk > 1 else [],
        compiler_params=_params(("parallel", "parallel", "arbitrary")),
    )(a, b)


def _out_proj_fwd(y, w_out, x, mod3, b3, name):
    s, kk = y.shape
    d = w_out.shape[1]
    tm, tn = _tile(s, 256), d

    def body(y_ref, w_ref, x_ref, m_ref, b_ref, xn_ref, o_ref):
        out = jnp.dot(y_ref[...], w_ref[...], preferred_element_type=F32)
        gate = m_ref[2:3, :] + b_ref[2:3, :]
        xn_ref[...] = x_ref[...] + gate * out
        o_ref[...] = out.astype(BF16)

    vec = pl.BlockSpec((8, tn), lambda i, j: (0, j))
    return pl.pallas_call(
        body, name=name, grid=(s // tm, d // tn),
        in_specs=[pl.BlockSpec((tm, kk), lambda i, j: (i, 0)), pl.BlockSpec((kk, tn), lambda i, j: (0, j)),
                  pl.BlockSpec((tm, tn), lambda i, j: (i, j)), vec, vec],
        out_specs=[pl.BlockSpec((tm, tn), lambda i, j: (i, j)), pl.BlockSpec((tm, tn), lambda i, j: (i, j))],
        out_shape=[jax.ShapeDtypeStruct((s, d), F32), jax.ShapeDtypeStruct((s, d), BF16)],
        compiler_params=_params(("parallel", "parallel")),
    )(y, w_out, x, mod3, b3)


def _norm_mod_fwd(x, g8, mod3, b3, name):
    s, d = x.shape
    tr = _tile(s, 512)

    def body(x_ref, g_ref, m_ref, b_ref, h_ref):
        xv = x_ref[...]
        r = lax.rsqrt(jnp.mean(xv * xv, axis=-1, keepdims=True) + NORM_EPS)
        shift = m_ref[0:1, :] + b_ref[0:1, :]
        scale = m_ref[1:2, :] + b_ref[1:2, :]
        h_ref[...] = ((xv * r * g_ref[0:1, :]) * (1.0 + scale) + shift).astype(BF16)

    vec = pl.BlockSpec((8, d), lambda i: (0, 0))
    return pl.pallas_call(
        body, name=name, grid=(s // tr,),
        in_specs=[pl.BlockSpec((tr, d), lambda i: (i, 0)), vec, vec, vec],
        out_specs=pl.BlockSpec((tr, d), lambda i: (i, 0)),
        out_shape=jax.ShapeDtypeStruct((s, d), BF16),
        compiler_params=_params(("parallel",)),
    )(x, g8, mod3, b3)


def _norm_mod_bwd(dh, x, dx_next, g8, mod3, b3, name):
    s, d = x.shape
    tr = _tile(s, 256)

    def body(dh_ref, x_ref, dn_ref, g_ref, m_ref, b_ref, dx_ref, acc_ref):
        @pl.when(pl.program_id(0) == 0)
        def _():
            acc_ref[...] = jnp.zeros_like(acc_ref)

        xv = x_ref[...]
        dh_v = dh_ref[...]
        g = g_ref[0:1, :]
        r = lax.rsqrt(jnp.mean(xv * xv, axis=-1, keepdims=True) + NORM_EPS)
        xn = xv * r
        scale1 = 1.0 + m_ref[1:2, :] + b_ref[1:2, :]
        dhs = dh_v * scale1
        dxn = dhs * g
        dx_ref[...] = dn_ref[...] + r * (dxn - xn * jnp.mean(dxn * xn, axis=-1, keepdims=True))
        acc_ref[0:1, :] += jnp.sum(dh_v, axis=0, keepdims=True)
        acc_ref[1:2, :] += jnp.sum(dh_v * (xn * g), axis=0, keepdims=True)
        acc_ref[2:3, :] += jnp.sum(dhs * xn, axis=0, keepdims=True)

    vec = pl.BlockSpec((8, d), lambda i: (0, 0))
    row = pl.BlockSpec((tr, d), lambda i: (i, 0))
    return pl.pallas_call(
        body, name=name, grid=(s // tr,),
        in_specs=[row, row, row, vec, vec, vec],
        out_specs=[row, vec],
        out_shape=[jax.ShapeDtypeStruct((s, d), F32), jax.ShapeDtypeStruct((8, d), F32)],
        compiler_params=_params(("arbitrary",)),
    )(dh, x, dx_next, g8, mod3, b3)


def _final_loss(x, tgt, g8, name):
    s, d = x.shape
    tr = _tile(s, 256)

    def body(x_ref, t_ref, g_ref, dx_ref, loss_ref, acc_ref):
        @pl.when(pl.program_id(0) == 0)
        def _():
            acc_ref[...] = jnp.zeros_like(acc_ref)
            loss_ref[...] = jnp.zeros_like(loss_ref)

        xv = x_ref[...]
        g = g_ref[0:1, :]
        r = lax.rsqrt(jnp.mean(xv * xv, axis=-1, keepdims=True) + NORM_EPS)
        xn = xv * r
        err = xn * g - t_ref[...]
        loss_ref[...] += 0.5 * jnp.sum(jnp.mean(err * err, axis=-1, keepdims=True), axis=0, keepdims=True)
        dy = err * (1.0 / d)
        acc_ref[0:1, :] += jnp.sum(dy * xn, axis=0, keepdims=True)
        dxn = dy * g
        dx_ref[...] = r * (dxn - xn * jnp.mean(dxn * xn, axis=-1, keepdims=True))

    vec = pl.BlockSpec((8, d), lambda i: (0, 0))
    row = pl.BlockSpec((tr, d), lambda i: (i, 0))
    return pl.pallas_call(
        body, name=name, grid=(s // tr,),
        in_specs=[row, row, vec],
        out_specs=[row, pl.BlockSpec((8, LANES), lambda i: (0, 0)), vec],
        out_shape=[jax.ShapeDtypeStruct((s, d), F32), jax.ShapeDtypeStruct((8, LANES), F32),
                   jax.ShapeDtypeStruct((8, d), F32)],
        compiler_params=_params(("arbitrary",)),
    )(x, tgt, g8)


def _out_proj_bwd_prep(dxn, out, mod3, b3, name):
    s, d = dxn.shape
    tr = _tile(s, 512)

    def body(dx_ref, o_ref, m_ref, b_ref, do_ref, acc_ref):
        @pl.when(pl.program_id(0) == 0)
        def _():
            acc_ref[...] = jnp.zeros_like(acc_ref)

        dxv = dx_ref[...]
        gate = m_ref[2:3, :] + b_ref[2:3, :]
        do_ref[...] = (gate * dxv).astype(BF16)
        acc_ref[0:1, :] += jnp.sum(dxv * o_ref[...].astype(F32), axis=0, keepdims=True)

    vec = pl.BlockSpec((8, d), lambda i: (0, 0))
    row = pl.BlockSpec((tr, d), lambda i: (i, 0))
    return pl.pallas_call(
        body, name=name, grid=(s // tr,),
        in_specs=[row, row, vec, vec],
        out_specs=[row, vec],
        out_shape=[jax.ShapeDtypeStruct((s, d), BF16), jax.ShapeDtypeStruct((8, d), F32)],
        compiler_params=_params(("arbitrary",)),
    )(dxn, out, mod3, b3)


_COPY_ROWS = 512


_ATTN_WIN = 256


def _attn_geometry(cfg, dil):
    sub = cfg.s // dil
    win = min(_ATTN_WIN, sub)
    bq = win if sub == win else win // 2
    return sub, bq, win


_N_SHIFTS = 3


def _attn_rows(dil, r, i, sub, bq, win):
    margin = (win - bq) // 2
    ws = jnp.clip(i * bq - margin, 0, sub - win)
    shift = (i * bq - ws) // margin if margin else 0
    if dil == 1:
        return (shift, pl.ds(pl.multiple_of(i * bq, bq), bq), pl.ds(pl.multiple_of(ws, 64), win))
    return (shift, pl.ds(r + i * (bq * dil), bq, stride=dil), pl.ds(r + ws * dil, win, stride=dil))


def _fill_bias_tables(cfg, bias_ref, slope):
    for pattern, dil in enumerate(DILATIONS):
        _, bq, win = _attn_geometry(cfg, dil)
        margin = (win - bq) // 2
        rel0 = lax.broadcasted_iota(jnp.int32, (bq, win), 1) - lax.broadcasted_iota(jnp.int32, (bq, win), 0)
        for shift in range(_N_SHIFTS if margin else 1):
            arel = jnp.abs(rel0 - shift * margin)
            bias_ref[pattern * _N_SHIFTS + shift, 0:bq, 0:win] = jnp.where(
                arel <= RADIUS, -(slope * dil) * arel.astype(F32), MASK_VALUE)


def _bias_scratch():
    return pltpu.VMEM((len(DILATIONS) * _N_SHIFTS, _ATTN_WIN, _ATTN_WIN), F32)


def _attn_scores(cfg, bias_ref, q, kw, pattern, shift):
    bq, win = q.shape[0], kw.shape[0]
    return (lax.dot_general(q, kw, _DN["nt"], preferred_element_type=F32)
            + bias_ref[pattern * _N_SHIFTS + shift, 0:bq, 0:win])


_ATTN_GROUP = 4


def _for_each_group(cfg, group):
    for pattern, dil in enumerate(DILATIONS):
        sub, bq, _ = _attn_geometry(cfg, dil)
        nblk = sub // bq
        per = min(_ATTN_GROUP, nblk)
        for r in range(dil):
            if nblk == per:
                group(pattern, dil, [(r, i) for i in range(nblk)])
            else:
                def step(g, carry, pattern=pattern, dil=dil, r=r, per=per):
                    group(pattern, dil, [(r, g * per + j) for j in range(per)])
                    return carry

                lax.fori_loop(0, nblk // per, step, 0)


def _attn_fwd(cfg, proj, slopes, name):
    s = cfg.s
    scale = HEAD_DIM ** -0.5

    def body(sl_ref, q_ref, k_ref, v_ref, z_ref, o_ref, lse_ref, y_ref, qf, kf, vf, acc, m_s, l_s, bias_s):
        slope = sl_ref[pl.program_id(0)]

        def to_f32(i, carry):
            rows = pl.ds(pl.multiple_of(i * _COPY_ROWS, _COPY_ROWS), _COPY_ROWS)
            qf[rows, :] = q_ref[rows, :].astype(F32) * scale
            kf[rows, :] = k_ref[rows, :].astype(F32)
            vf[rows, :] = v_ref[rows, :].astype(F32)
            return carry

        lax.fori_loop(0, s // _COPY_ROWS, to_f32, 0)

        _fill_bias_tables(cfg, bias_s, slope)

        def group(pattern, dil, blocks):
            sub, bq, win = _attn_geometry(cfg, dil)
            rep = win // HEAD_DIM
            first = pattern == 0
            work = []
            for r, i in blocks:
                shift, qrows, krows = _attn_rows(dil, r, i, sub, bq, win)
                old = None if first else (m_s[qrows, :], l_s[qrows, :], acc[qrows, :])
                work.append((shift, qrows, qf[qrows, :].astype(BF16), kf[krows, :].astype(BF16),
                             vf[krows, :].astype(BF16), old))
            new = []
            for shift, qrows, q, kw, vw, old in work:
                sc = _attn_scores(cfg, bias_s, q, kw, pattern, shift)
                m_blk = jnp.max(sc, axis=-1, keepdims=True)
                if first:
                    m_new = jnp.broadcast_to(m_blk, (bq, HEAD_DIM))
                    p = jnp.exp(sc - m_blk)
                    l_new = jnp.broadcast_to(jnp.sum(p, axis=-1, keepdims=True), (bq, HEAD_DIM))
                    a_new = jnp.dot(p.astype(BF16), vw, preferred_element_type=F32)
                else:
                    m_old, l_old, a_old = old
                    m_new = jnp.maximum(m_old, m_blk)
                    alpha = jnp.exp(m_old - m_new)
                    p = jnp.exp(sc - jnp.tile(m_new, (1, rep)))
                    l_new = alpha * l_old + jnp.sum(p, axis=-1, keepdims=True)
                    a_new = alpha * a_old + jnp.dot(p.astype(BF16), vw, preferred_element_type=F32)
                new.append((qrows, m_new, l_new, a_new))
            for qrows, m_new, l_new, a_new in new:
                m_s[qrows, :] = m_new
                l_s[qrows, :] = l_new
                acc[qrows, :] = a_new

        _for_each_group(cfg, group)

        def finish(i, carry):
            rows = pl.ds(pl.multiple_of(i * _COPY_ROWS, _COPY_ROWS), _COPY_ROWS)
            den = l_s[rows, :]
            o = (acc[rows, :] / den).astype(BF16)
            o_ref[rows, :] = o
            lse_ref[rows, :] = m_s[rows, :] + jnp.log(den)
            y_ref[rows, :] = (o.astype(F32) * _silu(z_ref[rows, :].astype(F32))).astype(BF16)
            return carry

        lax.fori_loop(0, s // _COPY_ROWS, finish, 0)

    def col(off):
        return pl.BlockSpec((s, HEAD_DIM), lambda h: (0, off // HEAD_DIM + h))

    head = pl.BlockSpec((s, HEAD_DIM), lambda h: (0, h))
    return pl.pallas_call(
        body, name=name, grid=(cfg.ha,),
        in_specs=[pl.BlockSpec(memory_space=pltpu.SMEM), col(cfg.qa), col(cfg.ka), col(cfg.va), col(cfg.za)],
        out_specs=[head, head, head],
        out_shape=[jax.ShapeDtypeStruct((s, cfg.aw), BF16), jax.ShapeDtypeStruct((s, cfg.aw), F32),
                   jax.ShapeDtypeStruct((s, cfg.d), BF16)],
        scratch_shapes=[pltpu.VMEM((s, HEAD_DIM), F32)] * 6 + [_bias_scratch()],
        compiler_params=_params(("parallel",)),
    )(slopes, proj, proj, proj, proj)


def _attn_bwd(cfg, proj, slopes, dy, oa, lse, name):
    s = cfg.s
    scale = HEAD_DIM ** -0.5
    dst_blocks = [c0 // HEAD_DIM for c0 in (cfg.za, cfg.qa, cfg.ka, cfg.va)]

    def body(sl_ref, q_ref, k_ref, v_ref, z_ref, dy_ref, o_ref, lse_ref, dproj_ref,
             qf, kf, vf, dof, dlt, dqa, dka, dva, bias_s, stage, sems):
        head = pl.program_id(0)
        slope = sl_ref[head]

        def out_copy(slot):
            cols = pl.ds(pl.multiple_of((dst_blocks[slot] + head) * HEAD_DIM, HEAD_DIM), HEAD_DIM)
            return pltpu.make_async_copy(stage.at[slot], dproj_ref.at[:, cols], sems.at[slot])

        @pl.when(head > 0)
        def _():
            out_copy(0).wait()

        def to_f32(i, carry):
            rows = pl.ds(pl.multiple_of(i * _COPY_ROWS, _COPY_ROWS), _COPY_ROWS)
            qf[rows, :] = q_ref[rows, :].astype(F32) * scale
            kf[rows, :] = k_ref[rows, :].astype(F32)
            vf[rows, :] = v_ref[rows, :].astype(F32)
            dyv = dy_ref[rows, :].astype(F32)
            zv = z_ref[rows, :].astype(F32)
            ov = o_ref[rows, :].astype(F32)
            dov = dyv * _silu(zv)
            dof[rows, :] = dov
            dlt[rows, :] = jnp.broadcast_to(jnp.sum(dov * ov, axis=-1, keepdims=True), (_COPY_ROWS, HEAD_DIM))
            stage[0, rows, :] = (dyv * ov * _dsilu(zv)).astype(BF16)
            zero = jnp.zeros((_COPY_ROWS, HEAD_DIM), F32)
            dqa[rows, :] = zero
            dka[rows, :] = zero
            dva[rows, :] = zero
            return carry

        lax.fori_loop(0, s // _COPY_ROWS, to_f32, 0)
        out_copy(0).start()

        _fill_bias_tables(cfg, bias_s, slope)

        def group(pattern, dil, blocks):
            sub, bq, win = _attn_geometry(cfg, dil)
            rep = win // HEAD_DIM
            work = []
            for r, i in blocks:
                shift, qrows, krows = _attn_rows(dil, r, i, sub, bq, win)
                work.append((shift, qrows, krows, qf[qrows, :].astype(BF16), kf[krows, :].astype(BF16),
                             vf[krows, :].astype(BF16), dof[qrows, :].astype(BF16),
                             lse_ref[qrows, :], dlt[qrows, :]))
            new = []
            for shift, qrows, krows, q, kw, vw, dob, lse_b, dlt_b in work:
                sc = _attn_scores(cfg, bias_s, q, kw, pattern, shift)
                p = jnp.exp(sc - jnp.tile(lse_b, (1, rep)))
                dp = lax.dot_general(dob, vw, _DN["nt"], preferred_element_type=F32)
                ds = (p * (dp - jnp.tile(dlt_b, (1, rep)))).astype(BF16)
                new.append((qrows, krows,
                            jnp.dot(ds, kw, preferred_element_type=F32) * scale,
                            lax.dot_general(ds, q, _DN["tn"], preferred_element_type=F32),
                            lax.dot_general(p.astype(BF16), dob, _DN["tn"], preferred_element_type=F32)))
            for qrows, krows, dq_b, dk_b, dv_b in new:
                dqa[qrows, :] += dq_b
                dka[krows, :] += dk_b
                dva[krows, :] += dv_b

        _for_each_group(cfg, group)

        @pl.when(head > 0)
        def _():
            for slot in (1, 2, 3):
                out_copy(slot).wait()

        def emit(i, carry):
            rows = pl.ds(pl.multiple_of(i * _COPY_ROWS, _COPY_ROWS), _COPY_ROWS)
            stage[1, rows, :] = dqa[rows, :].astype(BF16)
            stage[2, rows, :] = dka[rows, :].astype(BF16)
            stage[3, rows, :] = dva[rows, :].astype(BF16)
            return carry

        lax.fori_loop(0, s // _COPY_ROWS, emit, 0)
        for slot in (1, 2, 3):
            out_copy(slot).start()

        @pl.when(head == pl.num_programs(0) - 1)
        def _():
            for slot in range(4):
                out_copy(slot).wait()

    def col(off):
        return pl.BlockSpec((s, HEAD_DIM), lambda h: (0, off // HEAD_DIM + h))

    head_cols = pl.BlockSpec((s, HEAD_DIM), lambda h: (0, h))
    return pl.pallas_call(
        body, name=name, grid=(cfg.ha,),
        in_specs=[pl.BlockSpec(memory_space=pltpu.SMEM), col(cfg.qa), col(cfg.ka), col(cfg.va), col(cfg.za),
                  head_cols, head_cols, head_cols],
        out_specs=_ANY,
        out_shape=jax.ShapeDtypeStruct((s, cfg.f), BF16),
        scratch_shapes=[pltpu.VMEM((s, HEAD_DIM), F32)] * 8
        + [_bias_scratch(), pltpu.VMEM((4, s, HEAD_DIM), BF16), pltpu.SemaphoreType.DMA((4,))],
        compiler_params=_params(("arbitrary",)),
    )(slopes, proj, proj, proj, proj, dy, oa, lse)


def _decay_tables(lg, backward):
    c = RET_CHUNK
    a = lax.broadcasted_iota(jnp.int32, (c, c), 0)
    b = lax.broadcasted_iota(jnp.int32, (c, c), 1)
    idx = lax.broadcasted_iota(jnp.int32, (c, 1), 0).astype(F32)
    if backward:
        rel = (b - a).astype(F32)
        ex_xi = c - idx
        ex_zeta = idx
    else:
        rel = (a - b).astype(F32)
        ex_xi = idx + 1.0
        ex_zeta = c - 1.0 - idx
    relc = jnp.maximum(rel, 0.0)
    dm = jnp.where(rel >= 0, jnp.exp(relc * lg), 0.0)
    xi = jnp.exp(ex_xi * lg)
    zeta = jnp.exp(ex_zeta * lg)
    gch = jnp.exp(jnp.full((1, 1), c, F32) * lg)
    return relc, dm, xi, zeta, ex_xi, ex_zeta, gch


def _ret_fwd(cfg, proj, lgs, y, name):
    s = cfg.s
    c = RET_CHUNK
    n = s // c
    kscale = RET_QK ** -0.5

    def body(lg_ref, q_ref, k_ref, v_ref, z_ref, y_in, o_ref, y_ref, st_ref):
        h = pl.program_id(0)
        tabs = [_decay_tables(lg_ref[dirn, h], dirn == 1) for dirn in range(2)]
        st_ref[...] = jnp.zeros_like(st_ref)
        o_ref[...] = jnp.zeros_like(o_ref)

        def step(t, carry):
            for dirn in range(2):
                _, dm, xi, zeta, _, _, gch = tabs[dirn]
                i = (n - 1 - t) if dirn == 1 else t
                rows = pl.ds(pl.multiple_of(i * c, c), c)
                qi = q_ref[rows, :]
                ks = k_ref[rows, :].astype(F32) * kscale
                vi = v_ref[rows, :]
                inner = lax.dot_general(qi, ks.astype(BF16), _DN["nt"], preferred_element_type=F32) * dm
                st = st_ref[dirn]
                o_ref[rows, :] += (jnp.dot(inner.astype(BF16), vi, preferred_element_type=F32)
                                   + jnp.dot(qi, st.astype(BF16), preferred_element_type=F32) * xi)
                st_ref[dirn] = st * gch + lax.dot_general((ks * zeta).astype(BF16), vi, _DN["tn"],
                                                          preferred_element_type=F32)
            return carry

        lax.fori_loop(0, n, step, 0, unroll=2)

        def gate(i, carry):
            rows = pl.ds(pl.multiple_of(i * _COPY_ROWS, _COPY_ROWS), _COPY_ROWS)
            oh = o_ref[rows, :]
            rr = lax.rsqrt(jnp.mean(oh * oh, axis=-1, keepdims=True) + NORM_EPS)
            y_ref[rows, :] = (oh * rr * _silu(z_ref[rows, :].astype(F32))).astype(BF16)
            return carry

        lax.fori_loop(0, s // _COPY_ROWS, gate, 0)

    return pl.pallas_call(
        body, name=name, grid=(cfg.hr,),
        in_specs=[pl.BlockSpec(memory_space=pltpu.SMEM),
                  pl.BlockSpec((s, RET_QK), lambda h: (0, cfg.qr // RET_QK + h)),
                  pl.BlockSpec((s, RET_QK), lambda h: (0, cfg.kr // RET_QK + h)),
                  pl.BlockSpec((s, RET_V), lambda h: (0, cfg.vr // RET_V + h)),
                  pl.BlockSpec((s, RET_V), lambda h: (0, cfg.zr // RET_V + h)), _ANY],
        out_specs=[pl.BlockSpec((s, RET_V), lambda h: (0, h)),
                   pl.BlockSpec((s, RET_V), lambda h: (0, cfg.aw // RET_V + h))],
        out_shape=[jax.ShapeDtypeStruct((s, cfg.rw), F32), jax.ShapeDtypeStruct(y.shape, BF16)],
        input_output_aliases={5: 1},
        scratch_shapes=[pltpu.VMEM((2, RET_QK, RET_V), F32)],
        compiler_params=_params(("parallel",)),
    )(lgs, proj, proj, proj, proj, y)


def _ret_bwd(cfg, proj, lgs, dy, oret, dproj, name):
    s = cfg.s
    c = RET_CHUNK
    n = s // c
    kscale = RET_QK ** -0.5

    def accumulate(lg_ref, q_ref, k_ref, v_ref, do_ref, dq_ref, dk_ref, dv_ref, dlg_ref, states, t_ref,
                   e_dm, e_xi, e_zeta, e_g):
        h = pl.program_id(0)
        tabs = [_decay_tables(lg_ref[dirn, h], dirn == 1) for dirn in range(2)]
        dlg_ref[...] = jnp.zeros_like(dlg_ref)
        dq_ref[...] = jnp.zeros_like(dq_ref)
        dk_ref[...] = jnp.zeros_like(dk_ref)
        dv_ref[...] = jnp.zeros_like(dv_ref)

        def chunk_rows(dirn, t):
            i = (n - 1 - t) if dirn == 1 else t
            return pl.ds(pl.multiple_of(i * c, c), c)

        t_ref[...] = jnp.zeros_like(t_ref)

        def fwd_step(t, carry):
            for dirn in range(2):
                _, _, _, zeta, _, _, gch = tabs[dirn]
                rows = chunk_rows(dirn, t)
                st = t_ref[dirn]
                states[dirn, t] = st
                ks = k_ref[rows, :].astype(F32) * kscale
                t_ref[dirn] = st * gch + lax.dot_general((ks * zeta).astype(BF16), v_ref[rows, :], _DN["tn"],
                                                         preferred_element_type=F32)
            return carry

        lax.fori_loop(0, n, fwd_step, 0, unroll=2)
        t_ref[...] = jnp.zeros_like(t_ref)

        for ref in (e_dm, e_xi, e_zeta, e_g):
            ref[...] = jnp.zeros_like(ref)

        def bwd_step(u, carry):
            t = n - 1 - u
            for dirn in range(2):
                relc, dm, xi, zeta, ex_xi, ex_zeta, gch = tabs[dirn]
                rows = chunk_rows(dirn, t)
                qi = q_ref[rows, :]
                ks = k_ref[rows, :].astype(F32) * kscale
                ksb = ks.astype(BF16)
                vi = v_ref[rows, :]
                doi = do_ref[rows, :]
                sn_f = states[dirn, t]
                sn = sn_f.astype(BF16)
                tt = t_ref[dirn]
                ttb = tt.astype(BF16)
                a_mat = lax.dot_general(qi, ksb, _DN["nt"], preferred_element_type=F32) * dm
                dov = lax.dot_general(doi, vi, _DN["nt"], preferred_element_type=F32)
                b_mat = (dov * dm).astype(BF16)
                kz = (ks * zeta).astype(BF16)
                d_v = (jnp.dot(kz, ttb, preferred_element_type=F32)
                       + lax.dot_general(a_mat.astype(BF16), doi, _DN["tn"], preferred_element_type=F32))
                dk_inter = lax.dot_general(vi, ttb, _DN["nt"], preferred_element_type=F32) * zeta
                d_k = lax.dot_general(b_mat, qi, _DN["tn"], preferred_element_type=F32) + dk_inter
                o_inter = jnp.dot(qi, sn, preferred_element_type=F32) * xi
                d_q = (jnp.dot(b_mat, ksb, preferred_element_type=F32)
                       + lax.dot_general(doi, sn, _DN["nt"], preferred_element_type=F32) * xi)
                e_dm[dirn] += relc * a_mat * dov
                e_xi[dirn] += ex_xi * (doi.astype(F32) * o_inter)
                e_zeta[dirn] += ex_zeta * (ks * dk_inter)
                e_g[dirn] += tt * sn_f
                t_ref[dirn] = tt * gch + lax.dot_general((qi.astype(F32) * xi).astype(BF16), doi, _DN["tn"],
                                                         preferred_element_type=F32)
                dq_ref[rows, :] += d_q
                dk_ref[rows, :] += d_k * kscale
                dv_ref[rows, :] += d_v
            return carry

        lax.fori_loop(0, n, bwd_step, 0, unroll=2)
        for dirn in range(2):
            total = (jnp.sum(e_dm[dirn], keepdims=True) + jnp.sum(e_xi[dirn], keepdims=True)
                     + jnp.sum(e_zeta[dirn], keepdims=True) + (c * tabs[dirn][6]) * jnp.sum(e_g[dirn], keepdims=True))
            dlg_ref[0, dirn:dirn + 1, :] = jnp.broadcast_to(total, (1, LANES))

    def body(lg_ref, q_ref, k_ref, v_ref, z_ref, dy_ref, or_ref, dproj_in, dproj_ref, dlg_ref,
             do_s, dq_s, dk_s, dv_s, stage_z, stage_q, stage_k, stage_v, sems, *scratch):
        head = pl.program_id(0)
        stages = (stage_z, stage_q, stage_k, stage_v)
        firsts = (cfg.zr, cfg.qr, cfg.kr, cfg.vr)

        def out_copy(slot):
            width = stages[slot].shape[1]
            cols = pl.ds(pl.multiple_of(firsts[slot] + head * width, LANES), width)
            return pltpu.make_async_copy(stages[slot], dproj_ref.at[:, cols], sems.at[slot])

        @pl.when(head > 0)
        def _():
            out_copy(0).wait()

        def gate_norm_bwd(i, carry):
            rows = pl.ds(pl.multiple_of(i * _COPY_ROWS, _COPY_ROWS), _COPY_ROWS)
            oh = or_ref[rows, :]
            zr = z_ref[rows, :].astype(F32)
            dyr = dy_ref[rows, :].astype(F32)
            rr = lax.rsqrt(jnp.mean(oh * oh, axis=-1, keepdims=True) + NORM_EPS)
            yn = oh * rr
            dyn = dyr * _silu(zr)
            stage_z[rows, :] = (dyr * yn * _dsilu(zr)).astype(BF16)
            do_s[rows, :] = (rr * (dyn - yn * jnp.mean(dyn * yn, axis=-1, keepdims=True))).astype(BF16)
            return carry

        lax.fori_loop(0, s // _COPY_ROWS, gate_norm_bwd, 0)
        out_copy(0).start()

        accumulate(lg_ref, q_ref, k_ref, v_ref, do_s, dq_s, dk_s, dv_s, dlg_ref, *scratch)

        @pl.when(head > 0)
        def _():
            for slot in (1, 2, 3):
                out_copy(slot).wait()

        def emit(i, carry):
            rows = pl.ds(pl.multiple_of(i * _COPY_ROWS, _COPY_ROWS), _COPY_ROWS)
            stage_q[rows, :] = dq_s[rows, :].astype(BF16)
            stage_k[rows, :] = dk_s[rows, :].astype(BF16)
            stage_v[rows, :] = dv_s[rows, :].astype(BF16)
            return carry

        lax.fori_loop(0, s // _COPY_ROWS, emit, 0)
        for slot in (1, 2, 3):
            out_copy(slot).start()

        @pl.when(head == pl.num_programs(0) - 1)
        def _():
            for slot in range(4):
                out_copy(slot).wait()

    return pl.pallas_call(
        body, name=name, grid=(cfg.hr,),
        in_specs=[pl.BlockSpec(memory_space=pltpu.SMEM),
                  pl.BlockSpec((s, RET_QK), lambda h: (0, cfg.qr // RET_QK + h)),
                  pl.BlockSpec((s, RET_QK), lambda h: (0, cfg.kr // RET_QK + h)),
                  pl.BlockSpec((s, RET_V), lambda h: (0, cfg.vr // RET_V + h)),
                  pl.BlockSpec((s, RET_V), lambda h: (0, cfg.zr // RET_V + h)),
                  pl.BlockSpec((s, RET_V), lambda h: (0, cfg.aw // RET_V + h)),
                  pl.BlockSpec((s, RET_V), lambda h: (0, h)), _ANY],
        out_specs=[_ANY, pl.BlockSpec((1, 8, LANES), lambda h: (h, 0, 0))],
        out_shape=[jax.ShapeDtypeStruct(dproj.shape, BF16), jax.ShapeDtypeStruct((cfg.hr, 8, LANES), F32)],
        input_output_aliases={7: 0},
        scratch_shapes=[pltpu.VMEM((s, RET_V), BF16),
                        pltpu.VMEM((s, RET_QK), F32), pltpu.VMEM((s, RET_QK), F32), pltpu.VMEM((s, RET_V), F32),
                        pltpu.VMEM((s, RET_V), BF16), pltpu.VMEM((s, RET_QK), BF16), pltpu.VMEM((s, RET_QK), BF16),
                        pltpu.VMEM((s, RET_V), BF16), pltpu.SemaphoreType.DMA((4,)),
                        pltpu.VMEM((2, n, RET_QK, RET_V), F32), pltpu.VMEM((2, RET_QK, RET_V), F32),
                        pltpu.VMEM((2, c, c), F32), pltpu.VMEM((2, c, RET_V), F32),
                        pltpu.VMEM((2, c, RET_QK), F32), pltpu.VMEM((2, RET_QK, RET_V), F32)],
        compiler_params=_params(("arbitrary",)),
    )(lgs, proj, proj, proj, proj, dy, oret, dproj)


def _ada_fwd(cact16, w_ada, name):
    depth, d, n = w_ada.shape
    tn = _tile(n, 768)

    def body(c_ref, w_ref, o_ref):
        o_ref[0] = jnp.dot(c_ref[...], w_ref[0].astype(BF16), preferred_element_type=F32)

    return pl.pallas_call(
        body, name=name, grid=(depth, n // tn),
        in_specs=[pl.BlockSpec((16, d), lambda l, j: (0, 0)), pl.BlockSpec((1, d, tn), lambda l, j: (l, 0, j))],
        out_specs=pl.BlockSpec((1, 16, tn), lambda l, j: (l, 0, j)),
        out_shape=jax.ShapeDtypeStruct((depth, 16, n), F32),
        compiler_params=_params(("parallel", "parallel")),
    )(cact16, w_ada)


def _adam_math(w, g, m, v):
    m2 = ADAM_B1 * m + (1.0 - ADAM_B1) * g
    v2 = ADAM_B2 * v + (1.0 - ADAM_B2) * (g * g)
    m_hat = m2 / (1.0 - ADAM_B1 ** ADAM_STEP)
    v_hat = v2 / (1.0 - ADAM_B2 ** ADAM_STEP)
    delta = -ADAM_LR * (m_hat / (jnp.sqrt(v_hat) + ADAM_EPS) + ADAM_WD * w)
    return delta, m2, v2


def _adamw_big(w, g, m, v, first, count, prev, name):
    _, r, c = w.shape
    tr, tc = _tile(r, 512), _tile(c, 1024)

    def body(w_ref, g_ref, m_ref, v_ref, *rest):
        go_ref, d_ref, mo_ref, vo_ref = rest[-4:]
        gv = g_ref[...]
        delta, m2, v2 = _adam_math(w_ref[...], gv, m_ref[...], v_ref[...])
        go_ref[...] = gv
        d_ref[...] = delta
        mo_ref[...] = m2
        vo_ref[...] = v2

    spec = pl.BlockSpec((1, tr, tc), lambda l, i, j: (first + l, i, j))
    shp = jax.ShapeDtypeStruct(w.shape, F32)
    carried = [] if prev is None else list(prev)
    return pl.pallas_call(
        body, name=name, grid=(count, r // tr, c // tc),
        in_specs=[spec] * 4 + [_ANY] * len(carried), out_specs=[spec] * 4, out_shape=[shp] * 4,
        input_output_aliases={4 + k: k for k in range(len(carried))},
        compiler_params=_params(("parallel", "parallel", "parallel")),
    )(w, g, m, v, *carried)


def _adamw_ada(w, m, v, cact128, dmod128, name):
    depth, r, c = w.shape
    tr, tc = _tile(r, 512), _tile(c, 768)

    def body(w_ref, m_ref, v_ref, c_ref, dm_ref, go_ref, d_ref, mo_ref, vo_ref):
        gv = lax.dot_general(c_ref[...], dm_ref[0], _DN["tn"], preferred_element_type=F32)
        delta, m2, v2 = _adam_math(w_ref[0], gv, m_ref[0], v_ref[0])
        go_ref[0] = gv
        d_ref[0] = delta
        mo_ref[0] = m2
        vo_ref[0] = v2

    spec = pl.BlockSpec((1, tr, tc), lambda l, i, j: (l, i, j))
    shp = jax.ShapeDtypeStruct(w.shape, F32)
    return pl.pallas_call(
        body, name=name, grid=(depth, r // tr, c // tc),
        in_specs=[spec] * 3 + [pl.BlockSpec((128, tr), lambda l, i, j: (0, i)),
                               pl.BlockSpec((1, 128, tc), lambda l, i, j: (l, 0, j))],
        out_specs=[spec] * 4, out_shape=[shp] * 4,
        compiler_params=_params(("parallel", "parallel", "parallel")),
    )(w, m, v, cact128, dmod128)


def _adamw_small(w, g, m, v, name):
    def body(w_ref, g_ref, m_ref, v_ref, d_ref, mo_ref, vo_ref):
        delta, m2, v2 = _adam_math(w_ref[...], g_ref[...], m_ref[...], v_ref[...])
        d_ref[...] = delta
        mo_ref[...] = m2
        vo_ref[...] = v2

    shp = jax.ShapeDtypeStruct(w.shape, F32)
    return pl.pallas_call(body, name=name, out_shape=[shp] * 3)(w, g, m, v)


def _sum_gathered(parts, name):
    nd, r, c = parts.shape

    def body(p_ref, o_ref):
        acc = p_ref[0]
        for e in range(1, nd):
            acc = acc + p_ref[e]
        o_ref[...] = acc

    return pl.pallas_call(body, name=name, out_shape=jax.ShapeDtypeStruct((r, c), F32))(parts)


def _flip(v, bit):
    return 1 - v if bit else v


def _all_gather_small(x, name):
    r, c = x.shape

    def body(x_ref, out_ref, send_sems, recv_sems, local_sem):
        mx, my, mc = lax.axis_index("x"), lax.axis_index("y"), lax.axis_index("c")
        me = 4 * mx + 2 * my + mc
        mine = pltpu.make_async_copy(x_ref, out_ref.at[me], local_sem)
        mine.start()
        sends = []
        for k in range(1, N_DEV):
            peer = (_flip(mx, k & 4), _flip(my, k & 2), _flip(mc, k & 1))
            cp = pltpu.make_async_remote_copy(src_ref=x_ref, dst_ref=out_ref.at[me], send_sem=send_sems.at[k - 1],
                                              recv_sem=recv_sems.at[k - 1], device_id=peer, device_id_type=MESH)
            cp.start()
            sends.append(cp)
        for k in range(1, N_DEV):
            peer = (_flip(mx, k & 4), _flip(my, k & 2), _flip(mc, k & 1))
            src = 4 * peer[0] + 2 * peer[1] + peer[2]
            pltpu.make_async_remote_copy(src_ref=x_ref, dst_ref=out_ref.at[src], send_sem=send_sems.at[k - 1],
                                         recv_sem=recv_sems.at[k - 1], device_id=peer,
                                         device_id_type=MESH).wait_recv()
        for cp in sends:
            cp.wait_send()
        mine.wait()

    return pl.pallas_call(
        body, name=name,
        out_shape=jax.ShapeDtypeStruct((N_DEV, r, c), x.dtype),
        in_specs=[pl.BlockSpec(memory_space=pltpu.VMEM)],
        out_specs=pl.BlockSpec(memory_space=pltpu.VMEM),
        scratch_shapes=[pltpu.SemaphoreType.DMA((N_DEV - 1,)), pltpu.SemaphoreType.DMA((N_DEV - 1,)),
                        pltpu.SemaphoreType.DMA],
        compiler_params=pltpu.CompilerParams(vmem_limit_bytes=VMEM_LIMIT),
    )(x)


_HBM = pl.BlockSpec(memory_space=pltpu.HBM)
_SEM = pl.BlockSpec(memory_space=pltpu.SEMAPHORE)
_ANY = pl.BlockSpec(memory_space=pl.ANY)
_EFFECT = pltpu.SideEffectType.DATAFLOW_SIDE_EFFECTING


def _in_hbm(a):
    return pltpu.with_memory_space_constraint(a, pltpu.HBM)


def _place_w_in(w, layer, chip1, name):
    _, d, fc = w.shape
    tr = _tile(d, 512)

    def body(c_ref, w_ref, o_ref):
        o_ref[...] = w_ref[...].astype(BF16)

    return pl.pallas_call(
        body, name=name,
        grid_spec=pltpu.PrefetchScalarGridSpec(
            num_scalar_prefetch=1, grid=(d // tr,),
            in_specs=[pl.BlockSpec((None, tr, fc), lambda i, c: (layer, i, 0))],
            out_specs=pl.BlockSpec((tr, fc), lambda i, c: (i, c[0]))),
        out_shape=jax.ShapeDtypeStruct((d, N_CHIP * fc), BF16),
        compiler_params=_params(("parallel",)),
    )(chip1, w)


def _place_w_out(w, layer, chip1, name):
    _, rc, dd = w.shape
    tc = _tile(dd, 1024)

    def body(c_ref, w_ref, o_ref):
        o_ref[...] = w_ref[...].astype(BF16)

    return pl.pallas_call(
        body, name=name,
        grid_spec=pltpu.PrefetchScalarGridSpec(
            num_scalar_prefetch=1, grid=(dd // tc,),
            in_specs=[pl.BlockSpec((None, rc, tc), lambda j, c: (layer, 0, j))],
            out_specs=pl.BlockSpec((rc, tc), lambda j, c: (c[0], j))),
        out_shape=jax.ShapeDtypeStruct((N_CHIP * rc, dd), BF16),
        compiler_params=_params(("parallel",)),
    )(chip1, w)


def _weight_region(ref, tensor, chip):
    if tensor == 0:
        fc = ref.shape[1] // N_CHIP
        return ref.at[:, pl.ds(pl.multiple_of(chip * fc, LANES), fc)]
    rc = ref.shape[0] // N_CHIP
    return ref.at[pl.ds(pl.multiple_of(chip * rc, 8), rc), :]


def _gather_copies(ref, tensor, send_sems, recv_sems, landing):
    mx, my, mc = lax.axis_index("x"), lax.axis_index("y"), lax.axis_index("c")
    mine = _weight_region(ref, tensor, 2 * mx + my)
    copies = []
    for j in range(1, N_CHIP):
        peer = (_flip(mx, j & 2), _flip(my, j & 1), mc)
        dst = _weight_region(ref, tensor, 2 * peer[0] + peer[1]) if landing else mine
        idx = 2 * (j - 1) + tensor
        copies.append(pltpu.make_async_remote_copy(src_ref=mine, dst_ref=dst, send_sem=send_sems.at[idx],
                                                   recv_sem=recv_sems.at[idx], device_id=peer, device_id_type=MESH))
    return copies


def _gather_start(fi, fo, dep, name):
    ns = 2 * (N_CHIP - 1)

    def body(fi_ref, fo_ref, dep_ref, send_sems, recv_sems, fi_thru, fo_thru, token):
        for tensor, ref in enumerate((fi_ref, fo_ref)):
            for cp in _gather_copies(ref, tensor, send_sems, recv_sems, landing=False):
                cp.start()
        token[...] = jnp.zeros_like(token)

    return pl.pallas_call(
        body, name=name,
        out_shape=(pltpu.SemaphoreType.DMA((ns,)), pltpu.SemaphoreType.DMA((ns,)),
                   pltpu.HBM(fi.shape, fi.dtype), pltpu.HBM(fo.shape, fo.dtype),
                   jax.ShapeDtypeStruct((8, LANES), F32)),
        in_specs=(_HBM, _HBM, _ANY),
        out_specs=(_SEM, _SEM, _HBM, _HBM, pl.BlockSpec(memory_space=pltpu.VMEM)),
        input_output_aliases={0: 2, 1: 3},
        compiler_params=pltpu.CompilerParams(has_side_effects=_EFFECT),
    )(_in_hbm(fi), _in_hbm(fo), dep)


def _gather_wait(send_sems, recv_sems, buf, tensor, after, name):
    def body(buf_ref, send_sems, recv_sems, after_ref, buf_out):
        for cp in _gather_copies(buf_ref, tensor, send_sems, recv_sems, landing=True):
            cp.wait_send()
            cp.wait_recv()

    return pl.pallas_call(
        body, name=name,
        out_shape=pltpu.HBM(buf.shape, buf.dtype),
        in_specs=(_HBM, _SEM, _SEM, _ANY), out_specs=_HBM,
        input_output_aliases={0: 0},
        compiler_params=pltpu.CompilerParams(has_side_effects=_EFFECT),
    )(buf, send_sems, recv_sems, after)


def _scatter_copies(gi_ref, go_ref, pi_ref, po_ref, send_sems, recv_sems):
    mx, my, mc = lax.axis_index("x"), lax.axis_index("y"), lax.axis_index("c")
    hr, fc = pi_ref.shape[1:]
    ro = po_ref.shape[1]
    copies = []
    for k in range(1, N_DEV):
        peer = (_flip(mx, k & 4), _flip(my, k & 2), _flip(mc, k & 1))
        pchip = 2 * peer[0] + peer[1]
        src = (gi_ref.at[pl.ds(pl.multiple_of(peer[2] * hr, 8), hr), pl.ds(pl.multiple_of(pchip * fc, LANES), fc)],
               go_ref.at[pl.ds(pl.multiple_of((2 * pchip + peer[2]) * ro, 8), ro), :])
        dst = (pi_ref.at[k - 1], po_ref.at[k - 1])
        for t in range(2):
            idx = 2 * (k - 1) + t
            copies.append(pltpu.make_async_remote_copy(src_ref=src[t], dst_ref=dst[t], send_sem=send_sems.at[idx],
                                                       recv_sem=recv_sems.at[idx], device_id=peer,
                                                       device_id_type=MESH))
    return copies


def _scatter_start(gi, go, name):
    d, f = gi.shape
    dd = go.shape[1]
    ns = 2 * (N_DEV - 1)
    pi = lax.empty((N_DEV - 1, d // 2, f // N_CHIP), BF16)
    po = lax.empty((N_DEV - 1, d // N_DEV, dd), BF16)

    def body(gi_ref, go_ref, pi_ref, po_ref, send_sems, recv_sems, gi_thru, go_thru, pi_thru, po_thru, token):
        for cp in _scatter_copies(gi_ref, go_ref, pi_ref, po_ref, send_sems, recv_sems):
            cp.start()
        token[...] = jnp.zeros_like(token)

    return pl.pallas_call(
        body, name=name,
        out_shape=(pltpu.SemaphoreType.DMA((ns,)), pltpu.SemaphoreType.DMA((ns,)),
                   pltpu.HBM(gi.shape, gi.dtype), pltpu.HBM(go.shape, go.dtype),
                   pltpu.HBM(pi.shape, pi.dtype), pltpu.HBM(po.shape, po.dtype),
                   jax.ShapeDtypeStruct((8, LANES), F32)),
        in_specs=(_HBM, _HBM, _HBM, _HBM),
        out_specs=(_SEM, _SEM, _HBM, _HBM, _HBM, _HBM, pl.BlockSpec(memory_space=pltpu.VMEM)),
        input_output_aliases={0: 2, 1: 3, 2: 4, 3: 5},
        compiler_params=pltpu.CompilerParams(has_side_effects=_EFFECT),
    )(_in_hbm(gi), _in_hbm(go), _in_hbm(pi), _in_hbm(po))


def _scatter_wait(send_sems, recv_sems, gi, go, pi, po, after, name):
    def body(gi_ref, go_ref, pi_ref, po_ref, send_sems, recv_sems, *rest):
        for cp in _scatter_copies(gi_ref, go_ref, pi_ref, po_ref, send_sems, recv_sems):
            cp.wait_send()
            cp.wait_recv()

    return pl.pallas_call(
        body, name=name,
        out_shape=tuple(pltpu.HBM(a.shape, a.dtype) for a in (gi, go, pi, po)),
        in_specs=(_HBM, _HBM, _HBM, _HBM, _SEM, _SEM) + (_ANY,) * len(after), out_specs=(_HBM, _HBM, _HBM, _HBM),
        input_output_aliases={0: 0, 1: 1, 2: 2, 3: 3},
        compiler_params=pltpu.CompilerParams(has_side_effects=_EFFECT),
    )(gi, go, pi, po, send_sems, recv_sems, *after)


def _sum_into(buf, g, parts, where2, layer, row_blocks, dep, name):
    depth, r2, c = buf.shape
    r = r2 // 2
    tr, tc = _tile(r, 256), _tile(c, 1024)
    nr, nc = r // tr, c // tc
    col_blocks = (g.shape[1] // c) > 1

    def body(w_ref, buf_ref, g_ref, p_ref, dep_ref, o_ref):
        acc = g_ref[...].astype(F32)
        for e in range(N_DEV - 1):
            acc = acc + p_ref[e].astype(F32)
        o_ref[...] = acc

    return pl.pallas_call(
        body, name=name,
        grid_spec=pltpu.PrefetchScalarGridSpec(
            num_scalar_prefetch=1, grid=(nr, nc),
            in_specs=[_ANY,
                      pl.BlockSpec((tr, tc), lambda i, j, w: (row_blocks(w, nr) + i, (w[1] * nc if col_blocks else 0) + j)),
                      pl.BlockSpec((N_DEV - 1, tr, tc), lambda i, j, w: (0, i, j)), _ANY],
            out_specs=pl.BlockSpec((None, tr, tc), lambda i, j, w: (layer, w[0] * nr + i, j))),
        out_shape=jax.ShapeDtypeStruct(buf.shape, F32),
        input_output_aliases={1: 0},
        compiler_params=_params(("parallel", "parallel")),
    )(where2, buf, g, parts, dep)


def _exchange_halves(b_in, b_out, first, count, name):
    r_in = b_in.shape[1]
    r_out = b_out.shape[1]

    def body(bi_ref, bo_ref, oi_ref, oo_ref, send_sems, recv_sems):
        mx, my, mc = lax.axis_index("x"), lax.axis_index("y"), lax.axis_index("c")
        sib = (mx, my, 1 - mc)

        def half(ref, l, rows, which):
            return ref.at[l, pl.ds(pl.multiple_of(which * (rows // 2), 8), rows // 2), :]

        copies = []
        for l in range(first, first + count):
            for t, (src, dst, rows) in enumerate(((bi_ref, oi_ref, r_in), (bo_ref, oo_ref, r_out))):
                idx = 2 * (l - first) + t
                kw = dict(send_sem=send_sems.at[idx], recv_sem=recv_sems.at[idx], device_id=sib, device_id_type=MESH)
                cp = pltpu.make_async_remote_copy(src_ref=half(src, l, rows, mc), dst_ref=half(dst, l, rows, mc), **kw)
                cp.start()
                copies.append((cp, pltpu.make_async_remote_copy(src_ref=half(src, l, rows, mc),
                                                                dst_ref=half(dst, l, rows, 1 - mc), **kw)))
        for cp, landed in copies:
            landed.wait_recv()
        for cp, landed in copies:
            cp.wait_send()

    ns = 2 * count
    return pl.pallas_call(
        body, name=name,
        out_shape=[jax.ShapeDtypeStruct(b_in.shape, F32), jax.ShapeDtypeStruct(b_out.shape, F32)],
        in_specs=[_ANY, _ANY], out_specs=[_ANY, _ANY],
        input_output_aliases={0: 0, 1: 1},
        scratch_shapes=[pltpu.SemaphoreType.DMA((ns,)), pltpu.SemaphoreType.DMA((ns,))],
    )(b_in, b_out)


def _rows8(v):
    return jnp.pad(v, ((0, 8 - v.shape[0]), (0, 0)))


def kernel(x, c, norm_gain, w_ada, b_ada, w_in, w_out, ret_decay_logit_f, ret_decay_logit_b, final_gain, loss_target, m_norm_gain, m_w_ada, m_b_ada, m_w_in, m_w_out, m_ret_decay_logit_f, m_ret_decay_logit_b, m_final_gain, v_norm_gain, v_w_ada, v_b_ada, v_w_in, v_w_out, v_ret_decay_logit_f, v_ret_decay_logit_b, v_final_gain):
    cfg = _Cfg()
    depth, d = norm_gain.shape
    mx, my, mc = lax.axis_index("x"), lax.axis_index("y"), lax.axis_index("c")
    me = 4 * mx + 2 * my + mc
    chip = 2 * mx + my
    x0 = x[0]
    tgt = loss_target[0]
    ada_cols = w_ada.shape[2]

    c_all = _all_gather_small(_rows8(c), "gather_c")[:, 0, :]
    cact = _silu(c_all)
    mod_part = _ada_fwd(jnp.pad(cact, ((0, 8), (0, 0))).astype(BF16), w_ada, "ada_fwd")
    mod_all = _all_gather_small(mod_part.reshape(depth * 16, ada_cols), "gather_mod")
    mod_all = mod_all.reshape(N_CHIP, 2, depth, 16, ada_cols)[:, 0]
    mod_mine = lax.dynamic_index_in_dim(mod_all, me, axis=2, keepdims=False)
    mod = jnp.transpose(mod_mine, (1, 0, 2)).reshape(depth, 3, d)
    bias = b_ada.reshape(depth, 3, d)

    chip1 = jnp.reshape(chip, (1,)).astype(jnp.int32)
    where2 = jnp.stack([mc, chip]).astype(jnp.int32)

    def start_gather(l, dep):
        return _gather_start(_place_w_in(w_in, l, chip1, f"place_w_in_{l}"),
                             _place_w_out(w_out, l, chip1, f"place_w_out_{l}"), dep, f"gather_start_{l}")

    slopes = jnp.exp2(-8.0 * (jnp.arange(cfg.ha, dtype=F32) + 1.0) / cfg.ha)
    lg_f = jax.nn.log_sigmoid(ret_decay_logit_f)
    lg_b = jax.nn.log_sigmoid(ret_decay_logit_b)

    saved = []
    w_full = []
    h_x = x0
    pending = start_gather(0, c)
    for l in range(depth):
        send_sems, recv_sems, fi, fo, _ = pending
        w_in_l = _gather_wait(send_sems, recv_sems, fi, 0, mod if l == 0 else h_x, f"gather_wait_in_{l}")
        g8 = _rows8(norm_gain[l:l + 1])
        if l + 1 < depth:
            pending = start_gather(l + 1, w_in_l)
            g8 = g8 + pending[4][0:1, 0:1]
        mod3, b3 = _rows8(mod[l]), _rows8(bias[l])
        hb = _norm_mod_fwd(h_x, g8, mod3, b3, f"norm_mod_fwd_{l}")
        proj = _matmul(hb, w_in_l, "nn", BF16, f"in_proj_{l}")
        oa, lse, y = _attn_fwd(cfg, proj, slopes, f"attn_fwd_{l}")
        lgs = jnp.stack([lg_f[l], lg_b[l]])
        oret, y = _ret_fwd(cfg, proj, lgs, y, f"ret_fwd_{l}")
        w_out_l = _gather_wait(send_sems, recv_sems, fo, 1, y, f"gather_wait_out_{l}")
        w_full.append((w_in_l, w_out_l))
        x_next, out = _out_proj_fwd(y, w_out_l, h_x, mod3, b3, f"out_proj_{l}")
        saved.append((h_x, hb, proj, oret, y, oa, lse, out, g8, mod3, b3, lgs))
        h_x = x_next

    dx, loss8, fin_acc = _final_loss(h_x, tgt, _rows8(final_gain[None]), "final_loss")

    landed = [None] * depth
    d_mod, d_gain, d_lg = [None] * depth, [None] * depth, [None] * depth
    in_flight = None
    for l in reversed(range(depth)):
        x_l, hb, proj, oret, y, oa, lse, out, g8, mod3, b3, lgs = saved[l]
        w_in_l, w_out_l = w_full[l]
        douts, gate_acc = _out_proj_bwd_prep(dx, out, mod3, b3, f"out_proj_bwd_prep_{l}")
        dy = _matmul(douts, w_out_l, "nt", BF16, f"out_proj_dy_{l}")
        g_out_l = _matmul(y, douts, "tn", BF16, f"out_proj_dw_{l}", tk=4096)
        dproj = _attn_bwd(cfg, proj, slopes, dy, oa, lse, f"attn_bwd_{l}")
        dproj, dlg = _ret_bwd(cfg, proj, lgs, dy, oret, dproj, f"ret_bwd_{l}")
        g_in_l = _matmul(hb, dproj, "tn", BF16, f"in_proj_dw_{l}", tk=4096)
        started = _scatter_start(g_in_l, g_out_l, f"scatter_start_{l}")
        dh = _matmul(dproj, w_in_l, "nt", F32, f"in_proj_dh_{l}", tk=3584)
        g8 = g8 + started[-1][0:1, 0:1]
        dx, nm_acc = _norm_mod_bwd(dh, x_l, dx, g8, mod3, b3, f"norm_mod_bwd_{l}")
        d_mod[l] = jnp.concatenate([nm_acc[0], nm_acc[1], gate_acc[0]])
        d_gain[l] = nm_acc[2]
        d_lg[l] = dlg[:, 0:2, 0]
        if in_flight is not None:
            landed[l + 1] = _scatter_wait(*in_flight[:-1], (dx,), f"scatter_wait_{l + 1}")
        in_flight = started

    gw_in = lax.empty(w_in.shape, F32)
    gw_out = lax.empty(w_out.shape, F32)
    res_in = res_out = None
    for first, count in ((1, depth - 1), (0, 1)):
        if first == 0:
            after = (dx,) if res_out is None else (res_in[1], res_out[1])
            landed[0] = _scatter_wait(*in_flight[:-1], after, "scatter_wait_0")
        if count == 0:
            continue
        dep = in_flight[-1]
        for l in range(first, first + count):
            gi, go, pi, po = landed[l]
            gw_in = _sum_into(gw_in, gi, pi, where2, l, lambda w, nr: w[0] * nr, dep, f"sum_w_in_{l}")
            gw_out = _sum_into(gw_out, go, po, where2, l, lambda w, nr: (2 * w[1] + w[0]) * nr, dep,
                               f"sum_w_out_{l}")
        gw_in, gw_out = _exchange_halves(gw_in, gw_out, first, count, f"exchange_halves_{first}")
        res_in = _adamw_big(w_in, gw_in, m_w_in, v_w_in, first, count, res_in, f"adamw_w_in_{first}")
        res_out = _adamw_big(w_out, gw_out, m_w_out, v_w_out, first, count, res_out, f"adamw_w_out_{first}")
    grad_w_in, delta_w_in, new_m_w_in, new_v_w_in = res_in
    grad_w_out, delta_w_out, new_m_w_out, new_v_w_out = res_out

    dmod_mine = jnp.stack(d_mod)
    dmod_gathered = _all_gather_small(_rows8(dmod_mine), "gather_dmod")
    dmod_all = dmod_gathered[:, :depth, :]
    grad_b_ada = _sum_gathered(dmod_gathered, "sum_b_ada")[:depth]
    dmod_cols = lax.dynamic_slice_in_dim(dmod_all, chip * ada_cols, ada_cols, axis=2)
    dmod128 = jnp.pad(jnp.transpose(dmod_cols, (1, 0, 2)), ((0, 0), (0, 120), (0, 0))).astype(BF16)
    cact128 = jnp.pad(cact, ((0, 120), (0, 0))).astype(BF16)
    grad_w_ada, delta_w_ada, new_m_w_ada, new_v_w_ada = _adamw_ada(w_ada, m_w_ada, v_w_ada, cact128, dmod128,
                                                                  "adamw_w_ada")

    dlg_all = jnp.stack(d_lg)
    sig_f = jax.nn.sigmoid(-ret_decay_logit_f)
    sig_b = jax.nn.sigmoid(-ret_decay_logit_b)
    nlg = depth * cfg.hr
    pack = jnp.zeros((8, d), F32)
    for l in range(depth):
        pack = pack.at[l].set(d_gain[l])
    pack = pack.at[depth].set(fin_acc[0])
    pack = pack.at[depth + 1, 0].set(loss8[0, 0])
    pack = pack.at[depth + 1, LANES:LANES + nlg].set((dlg_all[:, :, 0] * sig_f).reshape(-1))
    pack = pack.at[depth + 1, 2 * LANES:2 * LANES + nlg].set((dlg_all[:, :, 1] * sig_b).reshape(-1))
    tot = _sum_gathered(_all_gather_small(pack, "gather_small"), "sum_small")
    grad_norm_gain = tot[:depth]
    grad_final_gain = tot[depth]
    loss = tot[depth + 1, 0]
    grad_lf = tot[depth + 1, LANES:LANES + nlg].reshape(depth, cfg.hr)
    grad_lb = tot[depth + 1, 2 * LANES:2 * LANES + nlg].reshape(depth, cfg.hr)

    d_ng, m_ng, v_ng = _adamw_small(norm_gain, grad_norm_gain, m_norm_gain, v_norm_gain, "adamw_norm_gain")
    d_ba, m_ba, v_ba = _adamw_small(b_ada, grad_b_ada, m_b_ada, v_b_ada, "adamw_b_ada")
    d_lf, m_lf, v_lf = _adamw_small(ret_decay_logit_f, grad_lf, m_ret_decay_logit_f, v_ret_decay_logit_f, "adamw_lf")
    d_lb, m_lb, v_lb = _adamw_small(ret_decay_logit_b, grad_lb, m_ret_decay_logit_b, v_ret_decay_logit_b, "adamw_lb")
    d_fg, m_fg, v_fg = _adamw_small(final_gain[None], grad_final_gain[None], m_final_gain[None], v_final_gain[None],
                                    "adamw_final_gain")

    return (loss, dx[None],
            grad_norm_gain, grad_w_ada, grad_b_ada, grad_w_in, grad_w_out, grad_lf, grad_lb, grad_final_gain,
            d_ng, delta_w_ada, d_ba, delta_w_in, delta_w_out, d_lf, d_lb, d_fg[0],
            m_ng, new_m_w_ada, m_ba, new_m_w_in, new_m_w_out, m_lf, m_lb, m_fg[0],
            v_ng, new_v_w_ada, v_ba, new_v_w_in, new_v_w_out, v_lf, v_lb, v_fg[0])
```

```python
import functools

import jax
import jax.numpy as jnp
from jax import lax
from jax.experimental import pallas as pl
from jax.experimental.pallas import tpu as pltpu

F32 = jnp.float32
BF16 = jnp.bfloat16

D_MODEL = 2048
SEQ = 4096
DEPTH = 4
HEAD_DIM = 128
DILATIONS = (1, 4, 16)
RADIUS = 64
N_HEADS_RET = 4
RET_QK = 128
RET_V = 256
RET_CHUNK = 256
NORM_EPS = 1e-6
MASK_VALUE = -1e30
N_DEV = 8
N_CHIP = 4
LANES = 128
VMEM_LIMIT = 56 * 1024 * 1024

ADAM_LR = 0.001
ADAM_B1 = 0.9
ADAM_B2 = 0.999
ADAM_EPS = 1e-08
ADAM_WD = 0.01
ADAM_STEP = 10

MESH = pl.DeviceIdType.MESH


class _Cfg:
    def __init__(self):
        self.d = D_MODEL
        self.s = SEQ
        self.aw = D_MODEL // 2
        self.ha = self.aw // HEAD_DIM
        self.rw = D_MODEL // 2
        self.hr = N_HEADS_RET
        self.rqk = self.hr * RET_QK
        self.f = 4 * self.aw + 2 * self.rqk + 2 * self.rw
        self.qa, self.ka, self.va, self.za = 0, self.aw, 2 * self.aw, 3 * self.aw
        self.qr = 4 * self.aw
        self.kr = self.qr + self.rqk
        self.vr = self.kr + self.rqk
        self.zr = self.vr + self.rw
        assert self.rw == self.hr * RET_V


def _tile(n, pref):
    t = min(n, pref)
    while n % t or t % LANES:
        t -= LANES
    return t


def _params(dims=None):
    return pltpu.CompilerParams(dimension_semantics=dims, vmem_limit_bytes=VMEM_LIMIT)


def _silu(z):
    return z * jax.nn.sigmoid(z)


def _dsilu(z):
    sg = jax.nn.sigmoid(z)
    return sg * (1.0 + z * (1.0 - sg))


_DN = {"nn": (((1,), (0,)), ((), ())), "nt": (((1,), (1,)), ((), ())), "tn": (((0,), (0,)), ((), ()))}


def _matmul(a, b, mode, out_dtype, name, tm=1024, tn=1024, tk=2048):
    if mode == "tn":
        kk, m = a.shape
    else:
        m, kk = a.shape
    n = b.shape[0] if mode == "nt" else b.shape[1]
    tm, tn, tk = _tile(m, tm), _tile(n, tn), _tile(kk, tk)
    nk = kk // tk
    a_spec = (pl.BlockSpec((tk, tm), lambda i, j, k: (k, i)) if mode == "tn"
              else pl.BlockSpec((tm, tk), lambda i, j, k: (i, k)))
    b_spec = (pl.BlockSpec((tn, tk), lambda i, j, k: (j, k)) if mode == "nt"
              else pl.BlockSpec((tk, tn), lambda i, j, k: (k, j)))
    dn = _DN[mode]

    def body(a_ref, b_ref, o_ref, *acc):
        p = lax.dot_general(a_ref[...], b_ref[...], dn, preferred_element_type=F32)
        if nk == 1:
            o_ref[...] = p.astype(out_dtype)
            return
        acc_ref, = acc
        k = pl.program_id(2)

        @pl.when(k == 0)
        def _():
            acc_ref[...] = p

        @pl.when(k > 0)
        def _():
            acc_ref[...] += p

        @pl.when(k == nk - 1)
        def _():
            o_ref[...] = acc_ref[...].astype(out_dtype)

    return pl.pallas_call(
        body, name=name, grid=(m // tm, n // tn, nk),
        in_specs=[a_spec, b_spec],
        out_specs=pl.BlockSpec((tm, tn), lambda i, j, k: (i, j)),
        out_shape=jax.ShapeDtypeStruct((m, n), out_dtype),
        scratch_shapes=[pltpu.VMEM((tm, tn), F32)] if nk > 1 else [],
        compiler_params=_params(("parallel", "parallel", "arbitrary")),
    )(a, b)


def _out_proj_fwd(y, w_out, x, mod3, b3, name):
    s, kk = y.shape
    d = w_out.shape[1]
    tm, tn = _tile(s, 256), d

    def body(y_ref, w_ref, x_ref, m_ref, b_ref, xn_ref, o_ref):
        out = jnp.dot(y_ref[...], w_ref[...], preferred_element_type=F32)
        gate = m_ref[2:3, :] + b_ref[2:3, :]
        xn_ref[...] = x_ref[...] + gate * out
        o_ref[...] = out.astype(BF16)

    vec = pl.BlockSpec((8, tn), lambda i, j: (0, j))
    return pl.pallas_call(
        body, name=name, grid=(s // tm, d // tn),
        in_specs=[pl.BlockSpec((tm, kk), lambda i, j: (i, 0)), pl.BlockSpec((kk, tn), lambda i, j: (0, j)),
                  pl.BlockSpec((tm, tn), lambda i, j: (i, j)), vec, vec],
        out_specs=[pl.BlockSpec((tm, tn), lambda i, j: (i, j)), pl.BlockSpec((tm, tn), lambda i, j: (i, j))],
        out_shape=[jax.ShapeDtypeStruct((s, d), F32), jax.ShapeDtypeStruct((s, d), BF16)],
        compiler_params=_params(("parallel", "parallel")),
    )(y, w_out, x, mod3, b3)


def _norm_mod_fwd(x, g8, mod3, b3, name):
    s, d = x.shape
    tr = _tile(s, 512)

    def body(x_ref, g_ref, m_ref, b_ref, h_ref):
        xv = x_ref[...]
        r = lax.rsqrt(jnp.mean(xv * xv, axis=-1, keepdims=True) + NORM_EPS)
        shift = m_ref[0:1, :] + b_ref[0:1, :]
        scale = m_ref[1:2, :] + b_ref[1:2, :]
        h_ref[...] = ((xv * r * g_ref[0:1, :]) * (1.0 + scale) + shift).astype(BF16)

    vec = pl.BlockSpec((8, d), lambda i: (0, 0))
    return pl.pallas_call(
        body, name=name, grid=(s // tr,),
        in_specs=[pl.BlockSpec((tr, d), lambda i: (i, 0)), vec, vec, vec],
        out_specs=pl.BlockSpec((tr, d), lambda i: (i, 0)),
        out_shape=jax.ShapeDtypeStruct((s, d), BF16),
        compiler_params=_params(("parallel",)),
    )(x, g8, mod3, b3)


def _norm_mod_bwd(dh, x, dx_next, g8, mod3, b3, name):
    s, d = x.shape
    tr = _tile(s, 256)

    def body(dh_ref, x_ref, dn_ref, g_ref, m_ref, b_ref, dx_ref, acc_ref):
        @pl.when(pl.program_id(0) == 0)
        def _():
            acc_ref[...] = jnp.zeros_like(acc_ref)

        xv = x_ref[...]
        dh_v = dh_ref[...]
        g = g_ref[0:1, :]
        r = lax.rsqrt(jnp.mean(xv * xv, axis=-1, keepdims=True) + NORM_EPS)
        xn = xv * r
        scale1 = 1.0 + m_ref[1:2, :] + b_ref[1:2, :]
        dhs = dh_v * scale1
        dxn = dhs * g
        dx_ref[...] = dn_ref[...] + r * (dxn - xn * jnp.mean(dxn * xn, axis=-1, keepdims=True))
        acc_ref[0:1, :] += jnp.sum(dh_v, axis=0, keepdims=True)
        acc_ref[1:2, :] += jnp.sum(dh_v * (xn * g), axis=0, keepdims=True)
        acc_ref[2:3, :] += jnp.sum(dhs * xn, axis=0, keepdims=True)

    vec = pl.BlockSpec((8, d), lambda i: (0, 0))
    row = pl.BlockSpec((tr, d), lambda i: (i, 0))
    return pl.pallas_call(
        body, name=name, grid=(s // tr,),
        in_specs=[row, row, row, vec, vec, vec],
        out_specs=[row, vec],
        out_shape=[jax.ShapeDtypeStruct((s, d), F32), jax.ShapeDtypeStruct((8, d), F32)],
        compiler_params=_params(("arbitrary",)),
    )(dh, x, dx_next, g8, mod3, b3)


def _final_loss(x, tgt, g8, name):
    s, d = x.shape
    tr = _tile(s, 256)

    def body(x_ref, t_ref, g_ref, dx_ref, loss_ref, acc_ref):
        @pl.when(pl.program_id(0) == 0)
        def _():
            acc_ref[...] = jnp.zeros_like(acc_ref)
            loss_ref[...] = jnp.zeros_like(loss_ref)

        xv = x_ref[...]
        g = g_ref[0:1, :]
        r = lax.rsqrt(jnp.mean(xv * xv, axis=-1, keepdims=True) + NORM_EPS)
        xn = xv * r
        err = xn * g - t_ref[...]
        loss_ref[...] += 0.5 * jnp.sum(jnp.mean(err * err, axis=-1, keepdims=True), axis=0, keepdims=True)
        dy = err * (1.0 / d)
        acc_ref[0:1, :] += jnp.sum(dy * xn, axis=0, keepdims=True)
        dxn = dy * g
        dx_ref[...] = r * (dxn - xn * jnp.mean(dxn * xn, axis=-1, keepdims=True))

    vec = pl.BlockSpec((8, d), lambda i: (0, 0))
    row = pl.BlockSpec((tr, d), lambda i: (i, 0))
    return pl.pallas_call(
        body, name=name, grid=(s // tr,),
        in_specs=[row, row, vec],
        out_specs=[row, pl.BlockSpec((8, LANES), lambda i: (0, 0)), vec],
        out_shape=[jax.ShapeDtypeStruct((s, d), F32), jax.ShapeDtypeStruct((8, LANES), F32),
                   jax.ShapeDtypeStruct((8, d), F32)],
        compiler_params=_params(("arbitrary",)),
    )(x, tgt, g8)


def _out_proj_bwd_prep(dxn, out, mod3, b3, name):
    s, d = dxn.shape
    tr = _tile(s, 512)

    def body(dx_ref, o_ref, m_ref, b_ref, do_ref, acc_ref):
        @pl.when(pl.program_id(0) == 0)
        def _():
            acc_ref[...] = jnp.zeros_like(acc_ref)

        dxv = dx_ref[...]
        gate = m_ref[2:3, :] + b_ref[2:3, :]
        do_ref[...] = (gate * dxv).astype(BF16)
        acc_ref[0:1, :] += jnp.sum(dxv * o_ref[...].astype(F32), axis=0, keepdims=True)

    vec = pl.BlockSpec((8, d), lambda i: (0, 0))
    row = pl.BlockSpec((tr, d), lambda i: (i, 0))
    return pl.pallas_call(
        body, name=name, grid=(s // tr,),
        in_specs=[row, row, vec, vec],
        out_specs=[row, vec],
        out_shape=[jax.ShapeDtypeStruct((s, d), BF16), jax.ShapeDtypeStruct((8, d), F32)],
        compiler_params=_params(("arbitrary",)),
    )(dxn, out, mod3, b3)


_COPY_ROWS = 512


_ATTN_WIN = 384


def _attn_geometry(cfg, dil):
    sub = cfg.s // dil
    if sub <= _ATTN_WIN:
        return sub, sub, sub
    return sub, _ATTN_WIN - 2 * RADIUS, _ATTN_WIN


_N_SHIFTS = 3


def _attn_rows(dil, r, i, sub, bq, win):
    margin = (win - bq) // 2
    ws = jnp.clip(i * bq - margin, 0, sub - win)
    shift = (i * bq - ws) // margin if margin else 0
    if dil == 1:
        return (shift, pl.ds(pl.multiple_of(i * bq, bq), bq), pl.ds(pl.multiple_of(ws, 64), win))
    return (shift, pl.ds(r + i * (bq * dil), bq, stride=dil), pl.ds(r + ws * dil, win, stride=dil))


def _fill_bias_tables(cfg, bias_ref, slope):
    for pattern, dil in enumerate(DILATIONS):
        _, bq, win = _attn_geometry(cfg, dil)
        margin = (win - bq) // 2
        rel0 = lax.broadcasted_iota(jnp.int32, (bq, win), 1) - lax.broadcasted_iota(jnp.int32, (bq, win), 0)
        for shift in range(_N_SHIFTS if margin else 1):
            arel = jnp.abs(rel0 - shift * margin)
            bias_ref[pattern * _N_SHIFTS + shift, 0:bq, 0:win] = jnp.where(
                arel <= RADIUS, -(slope * dil) * arel.astype(F32), MASK_VALUE)


def _bias_scratch():
    return pltpu.VMEM((len(DILATIONS) * _N_SHIFTS, _ATTN_WIN - 2 * RADIUS, _ATTN_WIN), F32)


def _attn_scores(cfg, bias_ref, q, kw, pattern, shift):
    bq, win = q.shape[0], kw.shape[0]
    return (lax.dot_general(q, kw, _DN["nt"], preferred_element_type=F32)
            + bias_ref[pattern * _N_SHIFTS + shift, 0:bq, 0:win])


def _for_each_group(cfg, group, size):
    for pattern, dil in enumerate(DILATIONS):
        sub, bq, _ = _attn_geometry(cfg, dil)
        nblk = sub // bq
        per = min(size, nblk)
        for r in range(dil):
            if nblk == per:
                group(pattern, dil, [(r, i) for i in range(nblk)])
            else:
                def step(g, carry, pattern=pattern, dil=dil, r=r, per=per):
                    group(pattern, dil, [(r, g * per + j) for j in range(per)])
                    return carry

                lax.fori_loop(0, nblk // per, step, 0)


def _attn_fwd(cfg, proj, slopes, name):
    s = cfg.s
    scale = HEAD_DIM ** -0.5

    def body(sl_ref, q_ref, k_ref, v_ref, z_ref, o_ref, lse_ref, y_ref, qf, kf, vf, acc, m_s, l_s, bias_s):
        slope = sl_ref[pl.program_id(0)]

        def to_f32(i, carry):
            rows = pl.ds(pl.multiple_of(i * _COPY_ROWS, _COPY_ROWS), _COPY_ROWS)
            qf[rows, :] = q_ref[rows, :].astype(F32) * scale
            kf[rows, :] = k_ref[rows, :].astype(F32)
            vf[rows, :] = v_ref[rows, :].astype(F32)
            return carry

        lax.fori_loop(0, s // _COPY_ROWS, to_f32, 0)

        _fill_bias_tables(cfg, bias_s, slope)

        def group(pattern, dil, blocks):
            sub, bq, win = _attn_geometry(cfg, dil)
            rep = win // HEAD_DIM
            first = pattern == 0
            work = []
            for r, i in blocks:
                shift, qrows, krows = _attn_rows(dil, r, i, sub, bq, win)
                old = None if first else (m_s[qrows, :], l_s[qrows, :], acc[qrows, :])
                work.append((shift, qrows, qf[qrows, :].astype(BF16), kf[krows, :].astype(BF16),
                             vf[krows, :].astype(BF16), old))
            new = []
            for shift, qrows, q, kw, vw, old in work:
                sc = _attn_scores(cfg, bias_s, q, kw, pattern, shift)
                m_blk = jnp.max(sc, axis=-1, keepdims=True)
                if first:
                    m_new = jnp.broadcast_to(m_blk, (bq, HEAD_DIM))
                    p = jnp.exp(sc - m_blk)
                    l_new = jnp.broadcast_to(jnp.sum(p, axis=-1, keepdims=True), (bq, HEAD_DIM))
                    a_new = jnp.dot(p.astype(BF16), vw, preferred_element_type=F32)
                else:
                    m_old, l_old, a_old = old
                    m_new = jnp.maximum(m_old, m_blk)
                    alpha = jnp.exp(m_old - m_new)
                    p = jnp.exp(sc - jnp.tile(m_new, (1, rep)))
                    l_new = alpha * l_old + jnp.sum(p, axis=-1, keepdims=True)
                    a_new = alpha * a_old + jnp.dot(p.astype(BF16), vw, preferred_element_type=F32)
                new.append((qrows, m_new, l_new, a_new))
            for qrows, m_new, l_new, a_new in new:
                m_s[qrows, :] = m_new
                l_s[qrows, :] = l_new
                acc[qrows, :] = a_new

        _for_each_group(cfg, group, size=4)

        def finish(i, carry):
            rows = pl.ds(pl.multiple_of(i * _COPY_ROWS, _COPY_ROWS), _COPY_ROWS)
            den = l_s[rows, :]
            o = (acc[rows, :] / den).astype(BF16)
            o_ref[rows, :] = o
            lse_ref[rows, :] = m_s[rows, :] + jnp.log(den)
            y_ref[rows, :] = (o.astype(F32) * _silu(z_ref[rows, :].astype(F32))).astype(BF16)
            return carry

        lax.fori_loop(0, s // _COPY_ROWS, finish, 0)

    def col(off):
        return pl.BlockSpec((s, HEAD_DIM), lambda h: (0, off // HEAD_DIM + h))

    head = pl.BlockSpec((s, HEAD_DIM), lambda h: (0, h))
    return pl.pallas_call(
        body, name=name, grid=(cfg.ha,),
        in_specs=[pl.BlockSpec(memory_space=pltpu.SMEM), col(cfg.qa), col(cfg.ka), col(cfg.va), col(cfg.za)],
        out_specs=[head, head, head],
        out_shape=[jax.ShapeDtypeStruct((s, cfg.aw), BF16), jax.ShapeDtypeStruct((s, cfg.aw), F32),
                   jax.ShapeDtypeStruct((s, cfg.d), BF16)],
        scratch_shapes=[pltpu.VMEM((s, HEAD_DIM), F32)] * 6 + [_bias_scratch()],
        compiler_params=_params(("parallel",)),
    )(slopes, proj, proj, proj, proj)


def _attn_bwd(cfg, proj, slopes, dy, oa, lse, name):
    s = cfg.s
    scale = HEAD_DIM ** -0.5
    dst_blocks = [c0 // HEAD_DIM for c0 in (cfg.za, cfg.qa, cfg.ka, cfg.va)]

    def body(sl_ref, q_ref, k_ref, v_ref, z_ref, dy_ref, o_ref, lse_ref, dproj_ref,
             qf, kf, vf, dof, dlt, dqa, dka, dva, bias_s, stage, sems):
        head = pl.program_id(0)
        slope = sl_ref[head]

        def out_copy(slot):
            cols = pl.ds(pl.multiple_of((dst_blocks[slot] + head) * HEAD_DIM, HEAD_DIM), HEAD_DIM)
            return pltpu.make_async_copy(stage.at[slot], dproj_ref.at[:, cols], sems.at[slot])

        @pl.when(head > 0)
        def _():
            out_copy(0).wait()

        def to_f32(i, carry):
            rows = pl.ds(pl.multiple_of(i * _COPY_ROWS, _COPY_ROWS), _COPY_ROWS)
            qf[rows, :] = q_ref[rows, :].astype(F32) * scale
            kf[rows, :] = k_ref[rows, :].astype(F32)
            vf[rows, :] = v_ref[rows, :].astype(F32)
            dyv = dy_ref[rows, :].astype(F32)
            zv = z_ref[rows, :].astype(F32)
            ov = o_ref[rows, :].astype(F32)
            dov = dyv * _silu(zv)
            dof[rows, :] = dov
            dlt[rows, :] = jnp.broadcast_to(jnp.sum(dov * ov, axis=-1, keepdims=True), (_COPY_ROWS, HEAD_DIM))
            stage[0, rows, :] = (dyv * ov * _dsilu(zv)).astype(BF16)
            zero = jnp.zeros((_COPY_ROWS, HEAD_DIM), F32)
            dqa[rows, :] = zero
            dka[rows, :] = zero
            dva[rows, :] = zero
            return carry

        lax.fori_loop(0, s // _COPY_ROWS, to_f32, 0)
        out_copy(0).start()

        _fill_bias_tables(cfg, bias_s, slope)

        def group(pattern, dil, blocks):
            sub, bq, win = _attn_geometry(cfg, dil)
            rep = win // HEAD_DIM
            work = []
            for r, i in blocks:
                shift, qrows, krows = _attn_rows(dil, r, i, sub, bq, win)
                work.append((shift, qrows, krows, qf[qrows, :].astype(BF16), kf[krows, :].astype(BF16),
                             vf[krows, :].astype(BF16), dof[qrows, :].astype(BF16),
                             lse_ref[qrows, :], dlt[qrows, :]))
            new = []
            for shift, qrows, krows, q, kw, vw, dob, lse_b, dlt_b in work:
                sc = _attn_scores(cfg, bias_s, q, kw, pattern, shift)
                p = jnp.exp(sc - jnp.tile(lse_b, (1, rep)))
                dp = lax.dot_general(dob, vw, _DN["nt"], preferred_element_type=F32)
                ds = (p * (dp - jnp.tile(dlt_b, (1, rep)))).astype(BF16)
                new.append((qrows, krows,
                            jnp.dot(ds, kw, preferred_element_type=F32) * scale,
                            lax.dot_general(ds, q, _DN["tn"], preferred_element_type=F32),
                            lax.dot_general(p.astype(BF16), dob, _DN["tn"], preferred_element_type=F32)))
            for qrows, krows, dq_b, dk_b, dv_b in new:
                dqa[qrows, :] += dq_b
                dka[krows, :] += dk_b
                dva[krows, :] += dv_b

        _for_each_group(cfg, group, size=2)

        @pl.when(head > 0)
        def _():
            for slot in (1, 2, 3):
                out_copy(slot).wait()

        def emit(i, carry):
            rows = pl.ds(pl.multiple_of(i * _COPY_ROWS, _COPY_ROWS), _COPY_ROWS)
            stage[1, rows, :] = dqa[rows, :].astype(BF16)
            stage[2, rows, :] = dka[rows, :].astype(BF16)
            stage[3, rows, :] = dva[rows, :].astype(BF16)
            return carry

        lax.fori_loop(0, s // _COPY_ROWS, emit, 0)
        for slot in (1, 2, 3):
            out_copy(slot).start()

        @pl.when(head == pl.num_programs(0) - 1)
        def _():
            for slot in range(4):
                out_copy(slot).wait()

    def col(off):
        return pl.BlockSpec((s, HEAD_DIM), lambda h: (0, off // HEAD_DIM + h))

    head_cols = pl.BlockSpec((s, HEAD_DIM), lambda h: (0, h))
    return pl.pallas_call(
        body, name=name, grid=(cfg.ha,),
        in_specs=[pl.BlockSpec(memory_space=pltpu.SMEM), col(cfg.qa), col(cfg.ka), col(cfg.va), col(cfg.za),
                  head_cols, head_cols, head_cols],
        out_specs=_ANY,
        out_shape=jax.ShapeDtypeStruct((s, cfg.f), BF16),
        scratch_shapes=[pltpu.VMEM((s, HEAD_DIM), F32)] * 8
        + [_bias_scratch(), pltpu.VMEM((4, s, HEAD_DIM), BF16), pltpu.SemaphoreType.DMA((4,))],
        compiler_params=_params(("arbitrary",)),
    )(slopes, proj, proj, proj, proj, dy, oa, lse)


def _decay_tables(lg, backward):
    c = RET_CHUNK
    a = lax.broadcasted_iota(jnp.int32, (c, c), 0)
    b = lax.broadcasted_iota(jnp.int32, (c, c), 1)
    idx = lax.broadcasted_iota(jnp.int32, (c, 1), 0).astype(F32)
    if backward:
        rel = (b - a).astype(F32)
        ex_xi = c - idx
        ex_zeta = idx
    else:
        rel = (a - b).astype(F32)
        ex_xi = idx + 1.0
        ex_zeta = c - 1.0 - idx
    relc = jnp.maximum(rel, 0.0)
    dm = jnp.where(rel >= 0, jnp.exp(relc * lg), 0.0)
    xi = jnp.exp(ex_xi * lg)
    zeta = jnp.exp(ex_zeta * lg)
    gch = jnp.exp(jnp.full((1, 1), c, F32) * lg)
    return relc, dm, xi, zeta, ex_xi, ex_zeta, gch


def _ret_fwd(cfg, proj, lgs, y, name):
    s = cfg.s
    c = RET_CHUNK
    n = s // c
    kscale = RET_QK ** -0.5

    def body(lg_ref, q_ref, k_ref, v_ref, z_ref, y_in, o_ref, y_ref, st_ref):
        h = pl.program_id(0)
        tabs = [_decay_tables(lg_ref[dirn, h], dirn == 1) for dirn in range(2)]
        st_ref[...] = jnp.zeros_like(st_ref)
        o_ref[...] = jnp.zeros_like(o_ref)

        def step(t, carry):
            for dirn in range(2):
                _, dm, xi, zeta, _, _, gch = tabs[dirn]
                i = (n - 1 - t) if dirn == 1 else t
                rows = pl.ds(pl.multiple_of(i * c, c), c)
                qi = q_ref[rows, :]
                ks = k_ref[rows, :].astype(F32) * kscale
                vi = v_ref[rows, :]
                inner = lax.dot_general(qi, ks.astype(BF16), _DN["nt"], preferred_element_type=F32) * dm
                st = st_ref[dirn]
                o_ref[rows, :] += (jnp.dot(inner.astype(BF16), vi, preferred_element_type=F32)
                                   + jnp.dot(qi, st.astype(BF16), preferred_element_type=F32) * xi)
                st_ref[dirn] = st * gch + lax.dot_general((ks * zeta).astype(BF16), vi, _DN["tn"],
                                                          preferred_element_type=F32)
            return carry

        lax.fori_loop(0, n, step, 0, unroll=2)

        def gate(i, carry):
            rows = pl.ds(pl.multiple_of(i * _COPY_ROWS, _COPY_ROWS), _COPY_ROWS)
            oh = o_ref[rows, :]
            rr = lax.rsqrt(jnp.mean(oh * oh, axis=-1, keepdims=True) + NORM_EPS)
            y_ref[rows, :] = (oh * rr * _silu(z_ref[rows, :].astype(F32))).astype(BF16)
            return carry

        lax.fori_loop(0, s // _COPY_ROWS, gate, 0)

    return pl.pallas_call(
        body, name=name, grid=(cfg.hr,),
        in_specs=[pl.BlockSpec(memory_space=pltpu.SMEM),
                  pl.BlockSpec((s, RET_QK), lambda h: (0, cfg.qr // RET_QK + h)),
                  pl.BlockSpec((s, RET_QK), lambda h: (0, cfg.kr // RET_QK + h)),
                  pl.BlockSpec((s, RET_V), lambda h: (0, cfg.vr // RET_V + h)),
                  pl.BlockSpec((s, RET_V), lambda h: (0, cfg.zr // RET_V + h)), _ANY],
        out_specs=[pl.BlockSpec((s, RET_V), lambda h: (0, h)),
                   pl.BlockSpec((s, RET_V), lambda h: (0, cfg.aw // RET_V + h))],
        out_shape=[jax.ShapeDtypeStruct((s, cfg.rw), F32), jax.ShapeDtypeStruct(y.shape, BF16)],
        input_output_aliases={5: 1},
        scratch_shapes=[pltpu.VMEM((2, RET_QK, RET_V), F32)],
        compiler_params=_params(("parallel",)),
    )(lgs, proj, proj, proj, proj, y)


def _ret_bwd(cfg, proj, lgs, dy, oret, dproj, name):
    s = cfg.s
    c = RET_CHUNK
    n = s // c
    kscale = RET_QK ** -0.5

    def accumulate(lg_ref, q_ref, k_ref, v_ref, do_ref, dq_ref, dk_ref, dv_ref, dlg_ref, states, t_ref,
                   e_dm, e_xi, e_zeta, e_g):
        h = pl.program_id(0)
        tabs = [_decay_tables(lg_ref[dirn, h], dirn == 1) for dirn in range(2)]
        dlg_ref[...] = jnp.zeros_like(dlg_ref)
        dq_ref[...] = jnp.zeros_like(dq_ref)
        dk_ref[...] = jnp.zeros_like(dk_ref)
        dv_ref[...] = jnp.zeros_like(dv_ref)

        def chunk_rows(dirn, t):
            i = (n - 1 - t) if dirn == 1 else t
            return pl.ds(pl.multiple_of(i * c, c), c)

        t_ref[...] = jnp.zeros_like(t_ref)

        def fwd_step(t, carry):
            for dirn in range(2):
                _, _, _, zeta, _, _, gch = tabs[dirn]
                rows = chunk_rows(dirn, t)
                st = t_ref[dirn]
                states[dirn, t] = st
                ks = k_ref[rows, :].astype(F32) * kscale
                t_ref[dirn] = st * gch + lax.dot_general((ks * zeta).astype(BF16), v_ref[rows, :], _DN["tn"],
                                                         preferred_element_type=F32)
            return carry

        lax.fori_loop(0, n, fwd_step, 0, unroll=2)
        t_ref[...] = jnp.zeros_like(t_ref)

        for ref in (e_dm, e_xi, e_zeta, e_g):
            ref[...] = jnp.zeros_like(ref)

        def bwd_step(u, carry):
            t = n - 1 - u
            for dirn in range(2):
                relc, dm, xi, zeta, ex_xi, ex_zeta, gch = tabs[dirn]
                rows = chunk_rows(dirn, t)
                qi = q_ref[rows, :]
                ks = k_ref[rows, :].astype(F32) * kscale
                ksb = ks.astype(BF16)
                vi = v_ref[rows, :]
                doi = do_ref[rows, :]
                sn_f = states[dirn, t]
                sn = sn_f.astype(BF16)
                tt = t_ref[dirn]
                ttb = tt.astype(BF16)
                a_mat = lax.dot_general(qi, ksb, _DN["nt"], preferred_element_type=F32) * dm
                dov = lax.dot_general(doi, vi, _DN["nt"], preferred_element_type=F32)
                b_mat = (dov * dm).astype(BF16)
                kz = (ks * zeta).astype(BF16)
                d_v = (jnp.dot(kz, ttb, preferred_element_type=F32)
                       + lax.dot_general(a_mat.astype(BF16), doi, _DN["tn"], preferred_element_type=F32))
                dk_inter = lax.dot_general(vi, ttb, _DN["nt"], preferred_element_type=F32) * zeta
                d_k = lax.dot_general(b_mat, qi, _DN["tn"], preferred_element_type=F32) + dk_inter
                o_inter = jnp.dot(qi, sn, preferred_element_type=F32) * xi
                d_q = (jnp.dot(b_mat, ksb, preferred_element_type=F32)
                       + lax.dot_general(doi, sn, _DN["nt"], preferred_element_type=F32) * xi)
                e_dm[dirn] += relc * a_mat * dov
                e_xi[dirn] += ex_xi * (doi.astype(F32) * o_inter)
                e_zeta[dirn] += ex_zeta * (ks * dk_inter)
                e_g[dirn] += tt * sn_f
                t_ref[dirn] = tt * gch + lax.dot_general((qi.astype(F32) * xi).astype(BF16), doi, _DN["tn"],
                                                         preferred_element_type=F32)
                dq_ref[rows, :] += d_q
                dk_ref[rows, :] += d_k * kscale
                dv_ref[rows, :] += d_v
            return carry

        lax.fori_loop(0, n, bwd_step, 0, unroll=2)
        for dirn in range(2):
            total = (jnp.sum(e_dm[dirn], keepdims=True) + jnp.sum(e_xi[dirn], keepdims=True)
                     + jnp.sum(e_zeta[dirn], keepdims=True) + (c * tabs[dirn][6]) * jnp.sum(e_g[dirn], keepdims=True))
            dlg_ref[0, dirn:dirn + 1, :] = jnp.broadcast_to(total, (1, LANES))

    def body(lg_ref, q_ref, k_ref, v_ref, z_ref, dy_ref, or_ref, dproj_in, dproj_ref, dlg_ref,
             do_s, dq_s, dk_s, dv_s, stage_z, stage_q, stage_k, stage_v, sems, *scratch):
        head = pl.program_id(0)
        stages = (stage_z, stage_q, stage_k, stage_v)
        firsts = (cfg.zr, cfg.qr, cfg.kr, cfg.vr)

        def out_copy(slot):
            width = stages[slot].shape[1]
            cols = pl.ds(pl.multiple_of(firsts[slot] + head * width, LANES), width)
            return pltpu.make_async_copy(stages[slot], dproj_ref.at[:, cols], sems.at[slot])

        @pl.when(head > 0)
        def _():
            out_copy(0).wait()

        def gate_norm_bwd(i, carry):
            rows = pl.ds(pl.multiple_of(i * _COPY_ROWS, _COPY_ROWS), _COPY_ROWS)
            oh = or_ref[rows, :]
            zr = z_ref[rows, :].astype(F32)
            dyr = dy_ref[rows, :].astype(F32)
            rr = lax.rsqrt(jnp.mean(oh * oh, axis=-1, keepdims=True) + NORM_EPS)
            yn = oh * rr
            dyn = dyr * _silu(zr)
            stage_z[rows, :] = (dyr * yn * _dsilu(zr)).astype(BF16)
            do_s[rows, :] = (rr * (dyn - yn * jnp.mean(dyn * yn, axis=-1, keepdims=True))).astype(BF16)
            return carry

        lax.fori_loop(0, s // _COPY_ROWS, gate_norm_bwd, 0)
        out_copy(0).start()

        accumulate(lg_ref, q_ref, k_ref, v_ref, do_s, dq_s, dk_s, dv_s, dlg_ref, *scratch)

        @pl.when(head > 0)
        def _():
            for slot in (1, 2, 3):
                out_copy(slot).wait()

        def emit(i, carry):
            rows = pl.ds(pl.multiple_of(i * _COPY_ROWS, _COPY_ROWS), _COPY_ROWS)
            stage_q[rows, :] = dq_s[rows, :].astype(BF16)
            stage_k[rows, :] = dk_s[rows, :].astype(BF16)
            stage_v[rows, :] = dv_s[rows, :].astype(BF16)
            return carry

        lax.fori_loop(0, s // _COPY_ROWS, emit, 0)
        for slot in (1, 2, 3):
            out_copy(slot).start()

        @pl.when(head == pl.num_programs(0) - 1)
        def _():
            for slot in range(4):
                out_copy(slot).wait()

    return pl.pallas_call(
        body, name=name, grid=(cfg.hr,),
        in_specs=[pl.BlockSpec(memory_space=pltpu.SMEM),
                  pl.BlockSpec((s, RET_QK), lambda h: (0, cfg.qr // RET_QK + h)),
                  pl.BlockSpec((s, RET_QK), lambda h: (0, cfg.kr // RET_QK + h)),
                  pl.BlockSpec((s, RET_V), lambda h: (0, cfg.vr // RET_V + h)),
                  pl.BlockSpec((s, RET_V), lambda h: (0, cfg.zr // RET_V + h)),
                  pl.BlockSpec((s, RET_V), lambda h: (0, cfg.aw // RET_V + h)),
                  pl.BlockSpec((s, RET_V), lambda h: (0, h)), _ANY],
        out_specs=[_ANY, pl.BlockSpec((1, 8, LANES), lambda h: (h, 0, 0))],
        out_shape=[jax.ShapeDtypeStruct(dproj.shape, BF16), jax.ShapeDtypeStruct((cfg.hr, 8, LANES), F32)],
        input_output_aliases={7: 0},
        scratch_shapes=[pltpu.VMEM((s, RET_V), BF16),
                        pltpu.VMEM((s, RET_QK), F32), pltpu.VMEM((s, RET_QK), F32), pltpu.VMEM((s, RET_V), F32),
                        pltpu.VMEM((s, RET_V), BF16), pltpu.VMEM((s, RET_QK), BF16), pltpu.VMEM((s, RET_QK), BF16),
                        pltpu.VMEM((s, RET_V), BF16), pltpu.SemaphoreType.DMA((4,)),
                        pltpu.VMEM((2, n, RET_QK, RET_V), F32), pltpu.VMEM((2, RET_QK, RET_V), F32),
                        pltpu.VMEM((2, c, c), F32), pltpu.VMEM((2, c, RET_V), F32),
                        pltpu.VMEM((2, c, RET_QK), F32), pltpu.VMEM((2, RET_QK, RET_V), F32)],
        compiler_params=_params(("arbitrary",)),
    )(lgs, proj, proj, proj, proj, dy, oret, dproj)


def _ada_fwd(cact16, w_ada, name):
    depth, d, n = w_ada.shape
    tn = _tile(n, 768)

    def body(c_ref, w_ref, o_ref):
        o_ref[0] = jnp.dot(c_ref[...], w_ref[0].astype(BF16), preferred_element_type=F32)

    return pl.pallas_call(
        body, name=name, grid=(depth, n // tn),
        in_specs=[pl.BlockSpec((16, d), lambda l, j: (0, 0)), pl.BlockSpec((1, d, tn), lambda l, j: (l, 0, j))],
        out_specs=pl.BlockSpec((1, 16, tn), lambda l, j: (l, 0, j)),
        out_shape=jax.ShapeDtypeStruct((depth, 16, n), F32),
        compiler_params=_params(("parallel", "parallel")),
    )(cact16, w_ada)


def _adam_math(w, g, m, v):
    m2 = ADAM_B1 * m + (1.0 - ADAM_B1) * g
    v2 = ADAM_B2 * v + (1.0 - ADAM_B2) * (g * g)
    m_hat = m2 / (1.0 - ADAM_B1 ** ADAM_STEP)
    v_hat = v2 / (1.0 - ADAM_B2 ** ADAM_STEP)
    delta = -ADAM_LR * (m_hat / (jnp.sqrt(v_hat) + ADAM_EPS) + ADAM_WD * w)
    return delta, m2, v2


def _adamw_big(w, g, m, v, first, count, prev, name):
    _, r, c = w.shape
    tr, tc = _tile(r, 512), _tile(c, 1024)

    def body(w_ref, g_ref, m_ref, v_ref, *rest):
        go_ref, d_ref, mo_ref, vo_ref = rest[-4:]
        gv = g_ref[...]
        delta, m2, v2 = _adam_math(w_ref[...], gv, m_ref[...], v_ref[...])
        go_ref[...] = gv
        d_ref[...] = delta
        mo_ref[...] = m2
        vo_ref[...] = v2

    spec = pl.BlockSpec((1, tr, tc), lambda l, i, j: (first + l, i, j))
    shp = jax.ShapeDtypeStruct(w.shape, F32)
    carried = [] if prev is None else list(prev)
    return pl.pallas_call(
        body, name=name, grid=(count, r // tr, c // tc),
        in_specs=[spec] * 4 + [_ANY] * len(carried), out_specs=[spec] * 4, out_shape=[shp] * 4,
        input_output_aliases={4 + k: k for k in range(len(carried))},
        compiler_params=_params(("parallel", "parallel", "parallel")),
    )(w, g, m, v, *carried)


def _adamw_ada(w, m, v, cact128, dmod128, name):
    depth, r, c = w.shape
    tr, tc = _tile(r, 512), _tile(c, 768)

    def body(w_ref, m_ref, v_ref, c_ref, dm_ref, go_ref, d_ref, mo_ref, vo_ref):
        gv = lax.dot_general(c_ref[...], dm_ref[0], _DN["tn"], preferred_element_type=F32)
        delta, m2, v2 = _adam_math(w_ref[0], gv, m_ref[0], v_ref[0])
        go_ref[0] = gv
        d_ref[0] = delta
        mo_ref[0] = m2
        vo_ref[0] = v2

    spec = pl.BlockSpec((1, tr, tc), lambda l, i, j: (l, i, j))
    shp = jax.ShapeDtypeStruct(w.shape, F32)
    return pl.pallas_call(
        body, name=name, grid=(depth, r // tr, c // tc),
        in_specs=[spec] * 3 + [pl.BlockSpec((128, tr), lambda l, i, j: (0, i)),
                               pl.BlockSpec((1, 128, tc), lambda l, i, j: (l, 0, j))],
        out_specs=[spec] * 4, out_shape=[shp] * 4,
        compiler_params=_params(("parallel", "parallel", "parallel")),
    )(w, m, v, cact128, dmod128)


def _adamw_small(w, g, m, v, name):
    def body(w_ref, g_ref, m_ref, v_ref, d_ref, mo_ref, vo_ref):
        delta, m2, v2 = _adam_math(w_ref[...], g_ref[...], m_ref[...], v_ref[...])
        d_ref[...] = delta
        mo_ref[...] = m2
        vo_ref[...] = v2

    shp = jax.ShapeDtypeStruct(w.shape, F32)
    return pl.pallas_call(body, name=name, out_shape=[shp] * 3)(w, g, m, v)


def _sum_gathered(parts, name):
    nd, r, c = parts.shape

    def body(p_ref, o_ref):
        acc = p_ref[0]
        for e in range(1, nd):
            acc = acc + p_ref[e]
        o_ref[...] = acc

    return pl.pallas_call(body, name=name, out_shape=jax.ShapeDtypeStruct((r, c), F32))(parts)


def _flip(v, bit):
    return 1 - v if bit else v


def _all_gather_small(x, name):
    r, c = x.shape

    def body(x_ref, out_ref, send_sems, recv_sems, local_sem):
        mx, my, mc = lax.axis_index("x"), lax.axis_index("y"), lax.axis_index("c")
        me = 4 * mx + 2 * my + mc
        mine = pltpu.make_async_copy(x_ref, out_ref.at[me], local_sem)
        mine.start()
        sends = []
        for k in range(1, N_DEV):
            peer = (_flip(mx, k & 4), _flip(my, k & 2), _flip(mc, k & 1))
            cp = pltpu.make_async_remote_copy(src_ref=x_ref, dst_ref=out_ref.at[me], send_sem=send_sems.at[k - 1],
                                              recv_sem=recv_sems.at[k - 1], device_id=peer, device_id_type=MESH)
            cp.start()
            sends.append(cp)
        for k in range(1, N_DEV):
            peer = (_flip(mx, k & 4), _flip(my, k & 2), _flip(mc, k & 1))
            src = 4 * peer[0] + 2 * peer[1] + peer[2]
            pltpu.make_async_remote_copy(src_ref=x_ref, dst_ref=out_ref.at[src], send_sem=send_sems.at[k - 1],
                                         recv_sem=recv_sems.at[k - 1], device_id=peer,
                                         device_id_type=MESH).wait_recv()
        for cp in sends:
            cp.wait_send()
        mine.wait()

    return pl.pallas_call(
        body, name=name,
        out_shape=jax.ShapeDtypeStruct((N_DEV, r, c), x.dtype),
        in_specs=[pl.BlockSpec(memory_space=pltpu.VMEM)],
        out_specs=pl.BlockSpec(memory_space=pltpu.VMEM),
        scratch_shapes=[pltpu.SemaphoreType.DMA((N_DEV - 1,)), pltpu.SemaphoreType.DMA((N_DEV - 1,)),
                        pltpu.SemaphoreType.DMA],
        compiler_params=pltpu.CompilerParams(vmem_limit_bytes=VMEM_LIMIT),
    )(x)


_HBM = pl.BlockSpec(memory_space=pltpu.HBM)
_SEM = pl.BlockSpec(memory_space=pltpu.SEMAPHORE)
_ANY = pl.BlockSpec(memory_space=pl.ANY)
_EFFECT = pltpu.SideEffectType.DATAFLOW_SIDE_EFFECTING


def _in_hbm(a):
    return pltpu.with_memory_space_constraint(a, pltpu.HBM)


def _place_w_in(w, layer, chip1, name):
    _, d, fc = w.shape
    tr = _tile(d, 512)

    def body(c_ref, w_ref, o_ref):
        o_ref[...] = w_ref[...].astype(BF16)

    return pl.pallas_call(
        body, name=name,
        grid_spec=pltpu.PrefetchScalarGridSpec(
            num_scalar_prefetch=1, grid=(d // tr,),
            in_specs=[pl.BlockSpec((None, tr, fc), lambda i, c: (layer, i, 0))],
            out_specs=pl.BlockSpec((tr, fc), lambda i, c: (i, c[0]))),
        out_shape=jax.ShapeDtypeStruct((d, N_CHIP * fc), BF16),
        compiler_params=_params(("parallel",)),
    )(chip1, w)


def _place_w_out(w, layer, chip1, name):
    _, rc, dd = w.shape
    tc = _tile(dd, 1024)

    def body(c_ref, w_ref, o_ref):
        o_ref[...] = w_ref[...].astype(BF16)

    return pl.pallas_call(
        body, name=name,
        grid_spec=pltpu.PrefetchScalarGridSpec(
            num_scalar_prefetch=1, grid=(dd // tc,),
            in_specs=[pl.BlockSpec((None, rc, tc), lambda j, c: (layer, 0, j))],
            out_specs=pl.BlockSpec((rc, tc), lambda j, c: (c[0], j))),
        out_shape=jax.ShapeDtypeStruct((N_CHIP * rc, dd), BF16),
        compiler_params=_params(("parallel",)),
    )(chip1, w)


def _weight_region(ref, tensor, chip):
    if tensor == 0:
        fc = ref.shape[1] // N_CHIP
        return ref.at[:, pl.ds(pl.multiple_of(chip * fc, LANES), fc)]
    rc = ref.shape[0] // N_CHIP
    return ref.at[pl.ds(pl.multiple_of(chip * rc, 8), rc), :]


def _gather_copies(ref, tensor, send_sems, recv_sems, landing):
    mx, my, mc = lax.axis_index("x"), lax.axis_index("y"), lax.axis_index("c")
    mine = _weight_region(ref, tensor, 2 * mx + my)
    copies = []
    for j in range(1, N_CHIP):
        peer = (_flip(mx, j & 2), _flip(my, j & 1), mc)
        dst = _weight_region(ref, tensor, 2 * peer[0] + peer[1]) if landing else mine
        idx = 2 * (j - 1) + tensor
        copies.append(pltpu.make_async_remote_copy(src_ref=mine, dst_ref=dst, send_sem=send_sems.at[idx],
                                                   recv_sem=recv_sems.at[idx], device_id=peer, device_id_type=MESH))
    return copies


def _gather_start(fi, fo, dep, name):
    ns = 2 * (N_CHIP - 1)

    def body(fi_ref, fo_ref, dep_ref, send_sems, recv_sems, fi_thru, fo_thru, token):
        for tensor, ref in enumerate((fi_ref, fo_ref)):
            for cp in _gather_copies(ref, tensor, send_sems, recv_sems, landing=False):
                cp.start()
        token[...] = jnp.zeros_like(token)

    return pl.pallas_call(
        body, name=name,
        out_shape=(pltpu.SemaphoreType.DMA((ns,)), pltpu.SemaphoreType.DMA((ns,)),
                   pltpu.HBM(fi.shape, fi.dtype), pltpu.HBM(fo.shape, fo.dtype),
                   jax.ShapeDtypeStruct((8, LANES), F32)),
        in_specs=(_HBM, _HBM, _ANY),
        out_specs=(_SEM, _SEM, _HBM, _HBM, pl.BlockSpec(memory_space=pltpu.VMEM)),
        input_output_aliases={0: 2, 1: 3},
        compiler_params=pltpu.CompilerParams(has_side_effects=_EFFECT),
    )(_in_hbm(fi), _in_hbm(fo), dep)


def _gather_wait(send_sems, recv_sems, buf, tensor, after, name):
    def body(buf_ref, send_sems, recv_sems, after_ref, buf_out):
        for cp in _gather_copies(buf_ref, tensor, send_sems, recv_sems, landing=True):
            cp.wait_send()
            cp.wait_recv()

    return pl.pallas_call(
        body, name=name,
        out_shape=pltpu.HBM(buf.shape, buf.dtype),
        in_specs=(_HBM, _SEM, _SEM, _ANY), out_specs=_HBM,
        input_output_aliases={0: 0},
        compiler_params=pltpu.CompilerParams(has_side_effects=_EFFECT),
    )(buf, send_sems, recv_sems, after)


def _scatter_copies(gi_ref, go_ref, pi_ref, po_ref, send_sems, recv_sems):
    mx, my, mc = lax.axis_index("x"), lax.axis_index("y"), lax.axis_index("c")
    hr, fc = pi_ref.shape[1:]
    ro = po_ref.shape[1]
    copies = []
    for k in range(1, N_DEV):
        peer = (_flip(mx, k & 4), _flip(my, k & 2), _flip(mc, k & 1))
        pchip = 2 * peer[0] + peer[1]
        src = (gi_ref.at[pl.ds(pl.multiple_of(peer[2] * hr, 8), hr), pl.ds(pl.multiple_of(pchip * fc, LANES), fc)],
               go_ref.at[pl.ds(pl.multiple_of((2 * pchip + peer[2]) * ro, 8), ro), :])
        dst = (pi_ref.at[k - 1], po_ref.at[k - 1])
        for t in range(2):
            idx = 2 * (k - 1) + t
            copies.append(pltpu.make_async_remote_copy(src_ref=src[t], dst_ref=dst[t], send_sem=send_sems.at[idx],
                                                       recv_sem=recv_sems.at[idx], device_id=peer,
                                                       device_id_type=MESH))
    return copies


def _scatter_start(gi, go, name):
    d, f = gi.shape
    dd = go.shape[1]
    ns = 2 * (N_DEV - 1)
    pi = lax.empty((N_DEV - 1, d // 2, f // N_CHIP), BF16)
    po = lax.empty((N_DEV - 1, d // N_DEV, dd), BF16)

    def body(gi_ref, go_ref, pi_ref, po_ref, send_sems, recv_sems, gi_thru, go_thru, pi_thru, po_thru, token):
        for cp in _scatter_copies(gi_ref, go_ref, pi_ref, po_ref, send_sems, recv_sems):
            cp.start()
        token[...] = jnp.zeros_like(token)

    return pl.pallas_call(
        body, name=name,
        out_shape=(pltpu.SemaphoreType.DMA((ns,)), pltpu.SemaphoreType.DMA((ns,)),
                   pltpu.HBM(gi.shape, gi.dtype), pltpu.HBM(go.shape, go.dtype),
                   pltpu.HBM(pi.shape, pi.dtype), pltpu.HBM(po.shape, po.dtype),
                   jax.ShapeDtypeStruct((8, LANES), F32)),
        in_specs=(_HBM, _HBM, _HBM, _HBM),
        out_specs=(_SEM, _SEM, _HBM, _HBM, _HBM, _HBM, pl.BlockSpec(memory_space=pltpu.VMEM)),
        input_output_aliases={0: 2, 1: 3, 2: 4, 3: 5},
        compiler_params=pltpu.CompilerParams(has_side_effects=_EFFECT),
    )(_in_hbm(gi), _in_hbm(go), _in_hbm(pi), _in_hbm(po))


def _scatter_wait(send_sems, recv_sems, gi, go, pi, po, after, name):
    def body(gi_ref, go_ref, pi_ref, po_ref, send_sems, recv_sems, *rest):
        for cp in _scatter_copies(gi_ref, go_ref, pi_ref, po_ref, send_sems, recv_sems):
            cp.wait_send()
            cp.wait_recv()

    return pl.pallas_call(
        body, name=name,
        out_shape=tuple(pltpu.HBM(a.shape, a.dtype) for a in (gi, go, pi, po)),
        in_specs=(_HBM, _HBM, _HBM, _HBM, _SEM, _SEM) + (_ANY,) * len(after), out_specs=(_HBM, _HBM, _HBM, _HBM),
        input_output_aliases={0: 0, 1: 1, 2: 2, 3: 3},
        compiler_params=pltpu.CompilerParams(has_side_effects=_EFFECT),
    )(gi, go, pi, po, send_sems, recv_sems, *after)


def _sum_into(buf, g, parts, where2, layer, row_blocks, dep, name):
    depth, r2, c = buf.shape
    r = r2 // 2
    tr, tc = _tile(r, 256), _tile(c, 1024)
    nr, nc = r // tr, c // tc
    col_blocks = (g.shape[1] // c) > 1

    def body(w_ref, buf_ref, g_ref, p_ref, dep_ref, o_ref):
        acc = g_ref[...].astype(F32)
        for e in range(N_DEV - 1):
            acc = acc + p_ref[e].astype(F32)
        o_ref[...] = acc

    return pl.pallas_call(
        body, name=name,
        grid_spec=pltpu.PrefetchScalarGridSpec(
            num_scalar_prefetch=1, grid=(nr, nc),
            in_specs=[_ANY,
                      pl.BlockSpec((tr, tc), lambda i, j, w: (row_blocks(w, nr) + i, (w[1] * nc if col_blocks else 0) + j)),
                      pl.BlockSpec((N_DEV - 1, tr, tc), lambda i, j, w: (0, i, j)), _ANY],
            out_specs=pl.BlockSpec((None, tr, tc), lambda i, j, w: (layer, w[0] * nr + i, j))),
        out_shape=jax.ShapeDtypeStruct(buf.shape, F32),
        input_output_aliases={1: 0},
        compiler_params=_params(("parallel", "parallel")),
    )(where2, buf, g, parts, dep)


def _exchange_halves(b_in, b_out, first, count, name):
    r_in = b_in.shape[1]
    r_out = b_out.shape[1]

    def body(bi_ref, bo_ref, oi_ref, oo_ref, send_sems, recv_sems):
        mx, my, mc = lax.axis_index("x"), lax.axis_index("y"), lax.axis_index("c")
        sib = (mx, my, 1 - mc)

        def half(ref, l, rows, which):
            return ref.at[l, pl.ds(pl.multiple_of(which * (rows // 2), 8), rows // 2), :]

        copies = []
        for l in range(first, first + count):
            for t, (src, dst, rows) in enumerate(((bi_ref, oi_ref, r_in), (bo_ref, oo_ref, r_out))):
                idx = 2 * (l - first) + t
                kw = dict(send_sem=send_sems.at[idx], recv_sem=recv_sems.at[idx], device_id=sib, device_id_type=MESH)
                cp = pltpu.make_async_remote_copy(src_ref=half(src, l, rows, mc), dst_ref=half(dst, l, rows, mc), **kw)
                cp.start()
                copies.append((cp, pltpu.make_async_remote_copy(src_ref=half(src, l, rows, mc),
                                                                dst_ref=half(dst, l, rows, 1 - mc), **kw)))
        for cp, landed in copies:
            landed.wait_recv()
        for cp, landed in copies:
            cp.wait_send()

    ns = 2 * count
    return pl.pallas_call(
        body, name=name,
        out_shape=[jax.ShapeDtypeStruct(b_in.shape, F32), jax.ShapeDtypeStruct(b_out.shape, F32)],
        in_specs=[_ANY, _ANY], out_specs=[_ANY, _ANY],
        input_output_aliases={0: 0, 1: 1},
        scratch_shapes=[pltpu.SemaphoreType.DMA((ns,)), pltpu.SemaphoreType.DMA((ns,))],
    )(b_in, b_out)


def _rows8(v):
    return jnp.pad(v, ((0, 8 - v.shape[0]), (0, 0)))


def kernel(x, c, norm_gain, w_ada, b_ada, w_in, w_out, ret_decay_logit_f, ret_decay_logit_b, final_gain, loss_target, m_norm_gain, m_w_ada, m_b_ada, m_w_in, m_w_out, m_ret_decay_logit_f, m_ret_decay_logit_b, m_final_gain, v_norm_gain, v_w_ada, v_b_ada, v_w_in, v_w_out, v_ret_decay_logit_f, v_ret_decay_logit_b, v_final_gain):
    cfg = _Cfg()
    depth, d = norm_gain.shape
    mx, my, mc = lax.axis_index("x"), lax.axis_index("y"), lax.axis_index("c")
    me = 4 * mx + 2 * my + mc
    chip = 2 * mx + my
    x0 = x[0]
    tgt = loss_target[0]
    ada_cols = w_ada.shape[2]

    c_all = _all_gather_small(_rows8(c), "gather_c")[:, 0, :]
    cact = _silu(c_all)
    mod_part = _ada_fwd(jnp.pad(cact, ((0, 8), (0, 0))).astype(BF16), w_ada, "ada_fwd")
    mod_all = _all_gather_small(mod_part.reshape(depth * 16, ada_cols), "gather_mod")
    mod_all = mod_all.reshape(N_CHIP, 2, depth, 16, ada_cols)[:, 0]
    mod_mine = lax.dynamic_index_in_dim(mod_all, me, axis=2, keepdims=False)
    mod = jnp.transpose(mod_mine, (1, 0, 2)).reshape(depth, 3, d)
    bias = b_ada.reshape(depth, 3, d)

    chip1 = jnp.reshape(chip, (1,)).astype(jnp.int32)
    where2 = jnp.stack([mc, chip]).astype(jnp.int32)

    def start_gather(l, dep):
        return _gather_start(_place_w_in(w_in, l, chip1, f"place_w_in_{l}"),
                             _place_w_out(w_out, l, chip1, f"place_w_out_{l}"), dep, f"gather_start_{l}")

    slopes = jnp.exp2(-8.0 * (jnp.arange(cfg.ha, dtype=F32) + 1.0) / cfg.ha)
    lg_f = jax.nn.log_sigmoid(ret_decay_logit_f)
    lg_b = jax.nn.log_sigmoid(ret_decay_logit_b)

    saved = []
    w_full = []
    h_x = x0
    pending = start_gather(0, c)
    for l in range(depth):
        send_sems, recv_sems, fi, fo, _ = pending
        w_in_l = _gather_wait(send_sems, recv_sems, fi, 0, mod if l == 0 else h_x, f"gather_wait_in_{l}")
        g8 = _rows8(norm_gain[l:l + 1])
        if l + 1 < depth:
            pending = start_gather(l + 1, w_in_l)
            g8 = g8 + pending[4][0:1, 0:1]
        mod3, b3 = _rows8(mod[l]), _rows8(bias[l])
        hb = _norm_mod_fwd(h_x, g8, mod3, b3, f"norm_mod_fwd_{l}")
        proj = _matmul(hb, w_in_l, "nn", BF16, f"in_proj_{l}")
        oa, lse, y = _attn_fwd(cfg, proj, slopes, f"attn_fwd_{l}")
        lgs = jnp.stack([lg_f[l], lg_b[l]])
        oret, y = _ret_fwd(cfg, proj, lgs, y, f"ret_fwd_{l}")
        w_out_l = _gather_wait(send_sems, recv_sems, fo, 1, y, f"gather_wait_out_{l}")
        w_full.append((w_in_l, w_out_l))
        x_next, out = _out_proj_fwd(y, w_out_l, h_x, mod3, b3, f"out_proj_{l}")
        saved.append((h_x, hb, proj, oret, y, oa, lse, out, g8, mod3, b3, lgs))
        h_x = x_next

    dx, loss8, fin_acc = _final_loss(h_x, tgt, _rows8(final_gain[None]), "final_loss")

    landed = [None] * depth
    d_mod, d_gain, d_lg = [None] * depth, [None] * depth, [None] * depth
    in_flight = None
    for l in reversed(range(depth)):
        x_l, hb, proj, oret, y, oa, lse, out, g8, mod3, b3, lgs = saved[l]
        w_in_l, w_out_l = w_full[l]
        douts, gate_acc = _out_proj_bwd_prep(dx, out, mod3, b3, f"out_proj_bwd_prep_{l}")
        dy = _matmul(douts, w_out_l, "nt", BF16, f"out_proj_dy_{l}")
        g_out_l = _matmul(y, douts, "tn", BF16, f"out_proj_dw_{l}", tk=4096)
        dproj = _attn_bwd(cfg, proj, slopes, dy, oa, lse, f"attn_bwd_{l}")
        dproj, dlg = _ret_bwd(cfg, proj, lgs, dy, oret, dproj, f"ret_bwd_{l}")
        g_in_l = _matmul(hb, dproj, "tn", BF16, f"in_proj_dw_{l}", tk=4096)
        started = _scatter_start(g_in_l, g_out_l, f"scatter_start_{l}")
        dh = _matmul(dproj, w_in_l, "nt", F32, f"in_proj_dh_{l}", tk=3584)
        g8 = g8 + started[-1][0:1, 0:1]
        dx, nm_acc = _norm_mod_bwd(dh, x_l, dx, g8, mod3, b3, f"norm_mod_bwd_{l}")
        d_mod[l] = jnp.concatenate([nm_acc[0], nm_acc[1], gate_acc[0]])
        d_gain[l] = nm_acc[2]
        d_lg[l] = dlg[:, 0:2, 0]
        if in_flight is not None:
            landed[l + 1] = _scatter_wait(*in_flight[:-1], (dx,), f"scatter_wait_{l + 1}")
        in_flight = started

    gw_in = lax.empty(w_in.shape, F32)
    gw_out = lax.empty(w_out.shape, F32)
    res_in = res_out = None
    for first, count in ((1, depth - 1), (0, 1)):
        if first == 0:
            after = (dx,) if res_out is None else (res_in[1], res_out[1])
            landed[0] = _scatter_wait(*in_flight[:-1], after, "scatter_wait_0")
        if count == 0:
            continue
        dep = in_flight[-1]
        for l in range(first, first + count):
            gi, go, pi, po = landed[l]
            gw_in = _sum_into(gw_in, gi, pi, where2, l, lambda w, nr: w[0] * nr, dep, f"sum_w_in_{l}")
            gw_out = _sum_into(gw_out, go, po, where2, l, lambda w, nr: (2 * w[1] + w[0]) * nr, dep,
                               f"sum_w_out_{l}")
        gw_in, gw_out = _exchange_halves(gw_in, gw_out, first, count, f"exchange_halves_{first}")
        res_in = _adamw_big(w_in, gw_in, m_w_in, v_w_in, first, count, res_in, f"adamw_w_in_{first}")
        res_out = _adamw_big(w_out, gw_out, m_w_out, v_w_out, first, count, res_out, f"adamw_w_out_{first}")
    grad_w_in, delta_w_in, new_m_w_in, new_v_w_in = res_in
    grad_w_out, delta_w_out, new_m_w_out, new_v_w_out = res_out

    dmod_mine = jnp.stack(d_mod)
    dmod_gathered = _all_gather_small(_rows8(dmod_mine), "gather_dmod")
    dmod_all = dmod_gathered[:, :depth, :]
    grad_b_ada = _sum_gathered(dmod_gathered, "sum_b_ada")[:depth]
    dmod_cols = lax.dynamic_slice_in_dim(dmod_all, chip * ada_cols, ada_cols, axis=2)
    dmod128 = jnp.pad(jnp.transpose(dmod_cols, (1, 0, 2)), ((0, 0), (0, 120), (0, 0))).astype(BF16)
    cact128 = jnp.pad(cact, ((0, 120), (0, 0))).astype(BF16)
    grad_w_ada, delta_w_ada, new_m_w_ada, new_v_w_ada = _adamw_ada(w_ada, m_w_ada, v_w_ada, cact128, dmod128,
                                                                  "adamw_w_ada")

    dlg_all = jnp.stack(d_lg)
    sig_f = jax.nn.sigmoid(-ret_decay_logit_f)
    sig_b = jax.nn.sigmoid(-ret_decay_logit_b)
    nlg = depth * cfg.hr
    pack = jnp.zeros((8, d), F32)
    for l in range(depth):
        pack = pack.at[l].set(d_gain[l])
    pack = pack.at[depth].set(fin_acc[0])
    pack = pack.at[depth + 1, 0].set(loss8[0, 0])
    pack = pack.at[depth + 1, LANES:LANES + nlg].set((dlg_all[:, :, 0] * sig_f).reshape(-1))
    pack = pack.at[depth + 1, 2 * LANES:2 * LANES + nlg].set((dlg_all[:, :, 1] * sig_b).reshape(-1))
    tot = _sum_gathered(_all_gather_small(pack, "gather_small"), "sum_small")
    grad_norm_gain = tot[:depth]
    grad_final_gain = tot[depth]
    loss = tot[depth + 1, 0]
    grad_lf = tot[depth + 1, LANES:LANES + nlg].reshape(depth, cfg.hr)
    grad_lb = tot[depth + 1, 2 * LANES:2 * LANES + nlg].reshape(depth, cfg.hr)

    d_ng, m_ng, v_ng = _adamw_small(norm_gain, grad_norm_gain, m_norm_gain, v_norm_gain, "adamw_norm_gain")
    d_ba, m_ba, v_ba = _adamw_small(b_ada, grad_b_ada, m_b_ada, v_b_ada, "adamw_b_ada")
    d_lf, m_lf, v_lf = _adamw_small(ret_decay_logit_f, grad_lf, m_ret_decay_logit_f, v_ret_decay_logit_f, "adamw_lf")
    d_lb, m_lb, v_lb = _adamw_small(ret_decay_logit_b, grad_lb, m_ret_decay_logit_b, v_ret_decay_logit_b, "adamw_lb")
    d_fg, m_fg, v_fg = _adamw_small(final_gain[None], grad_final_gain[None], m_final_gain[None], v_final_gain[None],
                                    "adamw_final_gain")

    return (loss, dx[None],
            grad_norm_gain, grad_w_ada, grad_b_ada, grad_w_in, grad_w_out, grad_lf, grad_lb, grad_final_gain,
            d_ng, delta_w_ada, d_ba, delta_w_in, delta_w_out, d_lf, d_lb, d_fg[0],
            m_ng, new_m_w_ada, m_ba, new_m_w_in, new_m_w_out, m_lf, m_lb, m_fg[0],
            v_ng, new_v_w_ada, v_ba, new_v_w_in, new_v_w_out, v_lf, v_lb, v_fg[0])
```

```python
import functools

import jax
import jax.numpy as jnp
from jax import lax
from jax.experimental import pallas as pl
from jax.experimental.pallas import tpu as pltpu

F32 = jnp.float32
BF16 = jnp.bfloat16

D_MODEL = 2048
SEQ = 4096
DEPTH = 4
HEAD_DIM = 128
DILATIONS = (1, 4, 16)
RADIUS = 64
N_HEADS_RET = 4
RET_QK = 128
RET_V = 256
RET_CHUNK = 256
NORM_EPS = 1e-6
MASK_VALUE = -1e30
N_DEV = 8
N_CHIP = 4
LANES = 128
VMEM_LIMIT = 56 * 1024 * 1024

ADAM_LR = 0.001
ADAM_B1 = 0.9
ADAM_B2 = 0.999
ADAM_EPS = 1e-08
ADAM_WD = 0.01
ADAM_STEP = 10

MESH = pl.DeviceIdType.MESH


class _Cfg:
    def __init__(self):
        self.d = D_MODEL
        self.s = SEQ
        self.aw = D_MODEL // 2
        self.ha = self.aw // HEAD_DIM
        self.rw = D_MODEL // 2
        self.hr = N_HEADS_RET
        self.rqk = self.hr * RET_QK
        self.f = 4 * self.aw + 2 * self.rqk + 2 * self.rw
        self.qa, self.ka, self.va, self.za = 0, self.aw, 2 * self.aw, 3 * self.aw
        self.qr = 4 * self.aw
        self.kr = self.qr + self.rqk
        self.vr = self.kr + self.rqk
        self.zr = self.vr + self.rw
        assert self.rw == self.hr * RET_V


def _tile(n, pref):
    t = min(n, pref)
    while n % t or t % LANES:
        t -= LANES
    return t


def _params(dims=None):
    return pltpu.CompilerParams(dimension_semantics=dims, vmem_limit_bytes=VMEM_LIMIT)


def _silu(z):
    return z * jax.nn.sigmoid(z)


def _dsilu(z):
    sg = jax.nn.sigmoid(z)
    return sg * (1.0 + z * (1.0 - sg))


_DN = {"nn": (((1,), (0,)), ((), ())), "nt": (((1,), (1,)), ((), ())), "tn": (((0,), (0,)), ((), ()))}


def _matmul(a, b, mode, out_dtype, name, tm=1024, tn=1024, tk=2048):
    if mode == "tn":
        kk, m = a.shape
    else:
        m, kk = a.shape
    n = b.shape[0] if mode == "nt" else b.shape[1]
    tm, tn, tk = _tile(m, tm), _tile(n, tn), _tile(kk, tk)
    nk = kk // tk
    a_spec = (pl.BlockSpec((tk, tm), lambda i, j, k: (k, i)) if mode == "tn"
              else pl.BlockSpec((tm, tk), lambda i, j, k: (i, k)))
    b_spec = (pl.BlockSpec((tn, tk), lambda i, j, k: (j, k)) if mode == "nt"
              else pl.BlockSpec((tk, tn), lambda i, j, k: (k, j)))
    dn = _DN[mode]

    def body(a_ref, b_ref, o_ref, *acc):
        p = lax.dot_general(a_ref[...], b_ref[...], dn, preferred_element_type=F32)
        if nk == 1:
            o_ref[...] = p.astype(out_dtype)
            return
        acc_ref, = acc
        k = pl.program_id(2)

        @pl.when(k == 0)
        def _():
            acc_ref[...] = p

        @pl.when(k > 0)
        def _():
            acc_ref[...] += p

        @pl.when(k == nk - 1)
        def _():
            o_ref[...] = acc_ref[...].astype(out_dtype)

    return pl.pallas_call(
        body, name=name, grid=(m // tm, n // tn, nk),
        in_specs=[a_spec, b_spec],
        out_specs=pl.BlockSpec((tm, tn), lambda i, j, k: (i, j)),
        out_shape=jax.ShapeDtypeStruct((m, n), out_dtype),
        scratch_shapes=[pltpu.VMEM((tm, tn), F32)] if nk > 1 else [],
        compiler_params=_params(("parallel", "parallel", "arbitrary")),
    )(a, b)


def _out_proj_fwd(y, w_out, x, mod3, b3, name):
    s, kk = y.shape
    d = w_out.shape[1]
    tm, tn = _tile(s, 256), d

    def body(y_ref, w_ref, x_ref, m_ref, b_ref, xn_ref, o_ref):
        out = jnp.dot(y_ref[...], w_ref[...], preferred_element_type=F32)
        gate = m_ref[2:3, :] + b_ref[2:3, :]
        xn_ref[...] = x_ref[...] + gate * out
        o_ref[...] = out.astype(BF16)

    vec = pl.BlockSpec((8, tn), lambda i, j: (0, j))
    return pl.pallas_call(
        body, name=name, grid=(s // tm, d // tn),
        in_specs=[pl.BlockSpec((tm, kk), lambda i, j: (i, 0)), pl.BlockSpec((kk, tn), lambda i, j: (0, j)),
                  pl.BlockSpec((tm, tn), lambda i, j: (i, j)), vec, vec],
        out_specs=[pl.BlockSpec((tm, tn), lambda i, j: (i, j)), pl.BlockSpec((tm, tn), lambda i, j: (i, j))],
        out_shape=[jax.ShapeDtypeStruct((s, d), F32), jax.ShapeDtypeStruct((s, d), BF16)],
        compiler_params=_params(("parallel", "parallel")),
    )(y, w_out, x, mod3, b3)


def _norm_mod_fwd(x, g8, mod3, b3, name):
    s, d = x.shape
    tr = _tile(s, 512)

    def body(x_ref, g_ref, m_ref, b_ref, h_ref):
        xv = x_ref[...]
        r = lax.rsqrt(jnp.mean(xv * xv, axis=-1, keepdims=True) + NORM_EPS)
        shift = m_ref[0:1, :] + b_ref[0:1, :]
        scale = m_ref[1:2, :] + b_ref[1:2, :]
        h_ref[...] = ((xv * r * g_ref[0:1, :]) * (1.0 + scale) + shift).astype(BF16)

    vec = pl.BlockSpec((8, d), lambda i: (0, 0))
    return pl.pallas_call(
        body, name=name, grid=(s // tr,),
        in_specs=[pl.BlockSpec((tr, d), lambda i: (i, 0)), vec, vec, vec],
        out_specs=pl.BlockSpec((tr, d), lambda i: (i, 0)),
        out_shape=jax.ShapeDtypeStruct((s, d), BF16),
        compiler_params=_params(("parallel",)),
    )(x, g8, mod3, b3)


def _norm_mod_bwd(dh, x, dx_next, g8, mod3, b3, name):
    s, d = x.shape
    tr = _tile(s, 256)

    def body(dh_ref, x_ref, dn_ref, g_ref, m_ref, b_ref, dx_ref, acc_ref):
        @pl.when(pl.program_id(0) == 0)
        def _():
            acc_ref[...] = jnp.zeros_like(acc_ref)

        xv = x_ref[...]
        dh_v = dh_ref[...]
        g = g_ref[0:1, :]
        r = lax.rsqrt(jnp.mean(xv * xv, axis=-1, keepdims=True) + NORM_EPS)
        xn = xv * r
        scale1 = 1.0 + m_ref[1:2, :] + b_ref[1:2, :]
        dhs = dh_v * scale1
        dxn = dhs * g
        dx_ref[...] = dn_ref[...] + r * (dxn - xn * jnp.mean(dxn * xn, axis=-1, keepdims=True))
        acc_ref[0:1, :] += jnp.sum(dh_v, axis=0, keepdims=True)
        acc_ref[1:2, :] += jnp.sum(dh_v * (xn * g), axis=0, keepdims=True)
        acc_ref[2:3, :] += jnp.sum(dhs * xn, axis=0, keepdims=True)

    vec = pl.BlockSpec((8, d), lambda i: (0, 0))
    row = pl.BlockSpec((tr, d), lambda i: (i, 0))
    return pl.pallas_call(
        body, name=name, grid=(s // tr,),
        in_specs=[row, row, row, vec, vec, vec],
        out_specs=[row, vec],
        out_shape=[jax.ShapeDtypeStruct((s, d), F32), jax.ShapeDtypeStruct((8, d), F32)],
        compiler_params=_params(("arbitrary",)),
    )(dh, x, dx_next, g8, mod3, b3)


def _final_loss(x, tgt, g8, name):
    s, d = x.shape
    tr = _tile(s, 256)

    def body(x_ref, t_ref, g_ref, dx_ref, loss_ref, acc_ref):
        @pl.when(pl.program_id(0) == 0)
        def _():
            acc_ref[...] = jnp.zeros_like(acc_ref)
            loss_ref[...] = jnp.zeros_like(loss_ref)

        xv = x_ref[...]
        g = g_ref[0:1, :]
        r = lax.rsqrt(jnp.mean(xv * xv, axis=-1, keepdims=True) + NORM_EPS)
        xn = xv * r
        err = xn * g - t_ref[...]
        loss_ref[...] += 0.5 * jnp.sum(jnp.mean(err * err, axis=-1, keepdims=True), axis=0, keepdims=True)
        dy = err * (1.0 / d)
        acc_ref[0:1, :] += jnp.sum(dy * xn, axis=0, keepdims=True)
        dxn = dy * g
        dx_ref[...] = r * (dxn - xn * jnp.mean(dxn * xn, axis=-1, keepdims=True))

    vec = pl.BlockSpec((8, d), lambda i: (0, 0))
    row = pl.BlockSpec((tr, d), lambda i: (i, 0))
    return pl.pallas_call(
        body, name=name, grid=(s // tr,),
        in_specs=[row, row, vec],
        out_specs=[row, pl.BlockSpec((8, LANES), lambda i: (0, 0)), vec],
        out_shape=[jax.ShapeDtypeStruct((s, d), F32), jax.ShapeDtypeStruct((8, LANES), F32),
                   jax.ShapeDtypeStruct((8, d), F32)],
        compiler_params=_params(("arbitrary",)),
    )(x, tgt, g8)


def _out_proj_bwd_prep(dxn, out, mod3, b3, name):
    s, d = dxn.shape
    tr = _tile(s, 512)

    def body(dx_ref, o_ref, m_ref, b_ref, do_ref, acc_ref):
        @pl.when(pl.program_id(0) == 0)
        def _():
            acc_ref[...] = jnp.zeros_like(acc_ref)

        dxv = dx_ref[...]
        gate = m_ref[2:3, :] + b_ref[2:3, :]
        do_ref[...] = (gate * dxv).astype(BF16)
        acc_ref[0:1, :] += jnp.sum(dxv * o_ref[...].astype(F32), axis=0, keepdims=True)

    vec = pl.BlockSpec((8, d), lambda i: (0, 0))
    row = pl.BlockSpec((tr, d), lambda i: (i, 0))
    return pl.pallas_call(
        body, name=name, grid=(s // tr,),
        in_specs=[row, row, vec, vec],
        out_specs=[row, vec],
        out_shape=[jax.ShapeDtypeStruct((s, d), BF16), jax.ShapeDtypeStruct((8, d), F32)],
        compiler_params=_params(("arbitrary",)),
    )(dxn, out, mod3, b3)


_COPY_ROWS = 512


_ATTN_WIN = 384


def _attn_geometry(cfg, dil):
    sub = cfg.s // dil
    if sub <= _ATTN_WIN:
        return sub, sub, sub
    return sub, _ATTN_WIN - 2 * RADIUS, _ATTN_WIN


_N_SHIFTS = 3


def _attn_rows(dil, r, i, sub, bq, win):
    margin = (win - bq) // 2
    ws = jnp.clip(i * bq - margin, 0, sub - win)
    shift = (i * bq - ws) // margin if margin else 0
    if dil == 1:
        return (shift, pl.ds(pl.multiple_of(i * bq, bq), bq), pl.ds(pl.multiple_of(ws, 64), win))
    return (shift, pl.ds(r + i * (bq * dil), bq, stride=dil), pl.ds(r + ws * dil, win, stride=dil))


def _fill_bias_tables(cfg, bias_ref, slope):
    for pattern, dil in enumerate(DILATIONS):
        _, bq, win = _attn_geometry(cfg, dil)
        margin = (win - bq) // 2
        rel0 = lax.broadcasted_iota(jnp.int32, (bq, win), 1) - lax.broadcasted_iota(jnp.int32, (bq, win), 0)
        for shift in range(_N_SHIFTS if margin else 1):
            arel = jnp.abs(rel0 - shift * margin)
            bias_ref[pattern * _N_SHIFTS + shift, 0:bq, 0:win] = jnp.where(
                arel <= RADIUS, -(slope * dil) * arel.astype(F32), MASK_VALUE)


def _bias_scratch():
    return pltpu.VMEM((len(DILATIONS) * _N_SHIFTS, _ATTN_WIN - 2 * RADIUS, _ATTN_WIN), F32)


def _attn_scores(cfg, bias_ref, q, kw, pattern, shift):
    bq, win = q.shape[0], kw.shape[0]
    return (lax.dot_general(q, kw, _DN["nt"], preferred_element_type=F32)
            + bias_ref[pattern * _N_SHIFTS + shift, 0:bq, 0:win])


def _for_each_group(cfg, group, size):
    for pattern, dil in enumerate(DILATIONS):
        sub, bq, _ = _attn_geometry(cfg, dil)
        nblk = sub // bq
        per = min(size, nblk)
        for r in range(dil):
            if nblk == per:
                group(pattern, dil, [(r, i) for i in range(nblk)])
            else:
                def step(g, carry, pattern=pattern, dil=dil, r=r, per=per):
                    group(pattern, dil, [(r, g * per + j) for j in range(per)])
                    return carry

                lax.fori_loop(0, nblk // per, step, 0)


def _attn_fwd(cfg, proj, slopes, name):
    s = cfg.s
    scale = HEAD_DIM ** -0.5

    def body(sl_ref, q_ref, k_ref, v_ref, z_ref, o_ref, lse_ref, y_ref, qf, kf, vf, acc, m_s, l_s, bias_s):
        slope = sl_ref[pl.program_id(0)]

        def to_f32(i, carry):
            rows = pl.ds(pl.multiple_of(i * _COPY_ROWS, _COPY_ROWS), _COPY_ROWS)
            qf[rows, :] = q_ref[rows, :].astype(F32) * scale
            kf[rows, :] = k_ref[rows, :].astype(F32)
            vf[rows, :] = v_ref[rows, :].astype(F32)
            return carry

        lax.fori_loop(0, s // _COPY_ROWS, to_f32, 0)

        _fill_bias_tables(cfg, bias_s, slope)

        def group(pattern, dil, blocks):
            sub, bq, win = _attn_geometry(cfg, dil)
            rep = win // HEAD_DIM
            first = pattern == 0
            work = []
            for r, i in blocks:
                shift, qrows, krows = _attn_rows(dil, r, i, sub, bq, win)
                old = None if first else (m_s[qrows, :], l_s[qrows, :], acc[qrows, :])
                work.append((shift, qrows, qf[qrows, :].astype(BF16), kf[krows, :].astype(BF16),
                             vf[krows, :].astype(BF16), old))
            new = []
            for shift, qrows, q, kw, vw, old in work:
                sc = _attn_scores(cfg, bias_s, q, kw, pattern, shift)
                m_blk = jnp.max(sc, axis=-1, keepdims=True)
                if first:
                    m_new = jnp.broadcast_to(m_blk, (bq, HEAD_DIM))
                    p = jnp.exp(sc - m_blk)
                    l_new = jnp.broadcast_to(jnp.sum(p, axis=-1, keepdims=True), (bq, HEAD_DIM))
                    a_new = jnp.dot(p.astype(BF16), vw, preferred_element_type=F32)
                else:
                    m_old, l_old, a_old = old
                    m_new = jnp.maximum(m_old, m_blk)
                    alpha = jnp.exp(m_old - m_new)
                    p = jnp.exp(sc - jnp.tile(m_new, (1, rep)))
                    l_new = alpha * l_old + jnp.sum(p, axis=-1, keepdims=True)
                    a_new = alpha * a_old + jnp.dot(p.astype(BF16), vw, preferred_element_type=F32)
                new.append((qrows, m_new, l_new, a_new))
            for qrows, m_new, l_new, a_new in new:
                m_s[qrows, :] = m_new
                l_s[qrows, :] = l_new
                acc[qrows, :] = a_new

        _for_each_group(cfg, group, size=4)

        def finish(i, carry):
            rows = pl.ds(pl.multiple_of(i * _COPY_ROWS, _COPY_ROWS), _COPY_ROWS)
            den = l_s[rows, :]
            o = (acc[rows, :] / den).astype(BF16)
            o_ref[rows, :] = o
            lse_ref[rows, :] = m_s[rows, :] + jnp.log(den)
            y_ref[rows, :] = (o.astype(F32) * _silu(z_ref[rows, :].astype(F32))).astype(BF16)
            return carry

        lax.fori_loop(0, s // _COPY_ROWS, finish, 0)

    def col(off):
        return pl.BlockSpec((s, HEAD_DIM), lambda h: (0, off // HEAD_DIM + h))

    head = pl.BlockSpec((s, HEAD_DIM), lambda h: (0, h))
    return pl.pallas_call(
        body, name=name, grid=(cfg.ha,),
        in_specs=[pl.BlockSpec(memory_space=pltpu.SMEM), col(cfg.qa), col(cfg.ka), col(cfg.va), col(cfg.za)],
        out_specs=[head, head, head],
        out_shape=[jax.ShapeDtypeStruct((s, cfg.aw), BF16), jax.ShapeDtypeStruct((s, cfg.aw), F32),
                   jax.ShapeDtypeStruct((s, cfg.d), BF16)],
        scratch_shapes=[pltpu.VMEM((s, HEAD_DIM), F32)] * 6 + [_bias_scratch()],
        compiler_params=_params(("parallel",)),
    )(slopes, proj, proj, proj, proj)


def _attn_bwd(cfg, proj, slopes, dy, oa, lse, name):
    s = cfg.s
    scale = HEAD_DIM ** -0.5
    dst_blocks = [c0 // HEAD_DIM for c0 in (cfg.za, cfg.qa, cfg.ka, cfg.va)]

    def body(sl_ref, q_ref, k_ref, v_ref, z_ref, dy_ref, o_ref, lse_ref, dproj_ref,
             qf, kf, vf, dof, dlt, dqa, dka, dva, bias_s, stage, sems):
        head = pl.program_id(0)
        slope = sl_ref[head]

        def out_copy(slot):
            cols = pl.ds(pl.multiple_of((dst_blocks[slot] + head) * HEAD_DIM, HEAD_DIM), HEAD_DIM)
            return pltpu.make_async_copy(stage.at[slot], dproj_ref.at[:, cols], sems.at[slot])

        @pl.when(head > 0)
        def _():
            out_copy(0).wait()

        def to_f32(i, carry):
            rows = pl.ds(pl.multiple_of(i * _COPY_ROWS, _COPY_ROWS), _COPY_ROWS)
            qf[rows, :] = q_ref[rows, :].astype(F32) * scale
            kf[rows, :] = k_ref[rows, :].astype(F32)
            vf[rows, :] = v_ref[rows, :].astype(F32)
            dyv = dy_ref[rows, :].astype(F32)
            zv = z_ref[rows, :].astype(F32)
            ov = o_ref[rows, :].astype(F32)
            dov = dyv * _silu(zv)
            dof[rows, :] = dov
            dlt[rows, :] = jnp.broadcast_to(jnp.sum(dov * ov, axis=-1, keepdims=True), (_COPY_ROWS, HEAD_DIM))
            stage[0, rows, :] = (dyv * ov * _dsilu(zv)).astype(BF16)
            zero = jnp.zeros((_COPY_ROWS, HEAD_DIM), F32)
            dqa[rows, :] = zero
            dka[rows, :] = zero
            dva[rows, :] = zero
            return carry

        lax.fori_loop(0, s // _COPY_ROWS, to_f32, 0)
        out_copy(0).start()

        _fill_bias_tables(cfg, bias_s, slope)

        def group(pattern, dil, blocks):
            sub, bq, win = _attn_geometry(cfg, dil)
            rep = win // HEAD_DIM
            work = []
            for r, i in blocks:
                shift, qrows, krows = _attn_rows(dil, r, i, sub, bq, win)
                work.append((shift, qrows, krows, qf[qrows, :].astype(BF16), kf[krows, :].astype(BF16),
                             vf[krows, :].astype(BF16), dof[qrows, :].astype(BF16),
                             lse_ref[qrows, :], dlt[qrows, :]))
            new = []
            for shift, qrows, krows, q, kw, vw, dob, lse_b, dlt_b in work:
                sc = _attn_scores(cfg, bias_s, q, kw, pattern, shift)
                p = jnp.exp(sc - jnp.tile(lse_b, (1, rep)))
                dp = lax.dot_general(dob, vw, _DN["nt"], preferred_element_type=F32)
                ds = (p * (dp - jnp.tile(dlt_b, (1, rep)))).astype(BF16)
                new.append((qrows, krows,
                            jnp.dot(ds, kw, preferred_element_type=F32) * scale,
                            lax.dot_general(ds, q, _DN["tn"], preferred_element_type=F32),
                            lax.dot_general(p.astype(BF16), dob, _DN["tn"], preferred_element_type=F32)))
            for qrows, krows, dq_b, dk_b, dv_b in new:
                dqa[qrows, :] += dq_b
                dka[krows, :] += dk_b
                dva[krows, :] += dv_b

        _for_each_group(cfg, group, size=2)

        @pl.when(head > 0)
        def _():
            for slot in (1, 2, 3):
                out_copy(slot).wait()

        def emit(i, carry):
            rows = pl.ds(pl.multiple_of(i * _COPY_ROWS, _COPY_ROWS), _COPY_ROWS)
            stage[1, rows, :] = dqa[rows, :].astype(BF16)
            stage[2, rows, :] = dka[rows, :].astype(BF16)
            stage[3, rows, :] = dva[rows, :].astype(BF16)
            return carry

        lax.fori_loop(0, s // _COPY_ROWS, emit, 0)
        for slot in (1, 2, 3):
            out_copy(slot).start()

        @pl.when(head == pl.num_programs(0) - 1)
        def _():
            for slot in range(4):
                out_copy(slot).wait()

    def col(off):
        return pl.BlockSpec((s, HEAD_DIM), lambda h: (0, off // HEAD_DIM + h))

    head_cols = pl.BlockSpec((s, HEAD_DIM), lambda h: (0, h))
    return pl.pallas_call(
        body, name=name, grid=(cfg.ha,),
        in_specs=[pl.BlockSpec(memory_space=pltpu.SMEM), col(cfg.qa), col(cfg.ka), col(cfg.va), col(cfg.za),
                  head_cols, head_cols, head_cols],
        out_specs=_ANY,
        out_shape=jax.ShapeDtypeStruct((s, cfg.f), BF16),
        scratch_shapes=[pltpu.VMEM((s, HEAD_DIM), F32)] * 8
        + [_bias_scratch(), pltpu.VMEM((4, s, HEAD_DIM), BF16), pltpu.SemaphoreType.DMA((4,))],
        compiler_params=_params(("arbitrary",)),
    )(slopes, proj, proj, proj, proj, dy, oa, lse)


def _decay_tables(lg, backward):
    c = RET_CHUNK
    a = lax.broadcasted_iota(jnp.int32, (c, c), 0)
    b = lax.broadcasted_iota(jnp.int32, (c, c), 1)
    idx = lax.broadcasted_iota(jnp.int32, (c, 1), 0).astype(F32)
    if backward:
        rel = (b - a).astype(F32)
        ex_xi = c - idx
        ex_zeta = idx
    else:
        rel = (a - b).astype(F32)
        ex_xi = idx + 1.0
        ex_zeta = c - 1.0 - idx
    relc = jnp.maximum(rel, 0.0)
    dm = jnp.where(rel >= 0, jnp.exp(relc * lg), 0.0)
    xi = jnp.exp(ex_xi * lg)
    zeta = jnp.exp(ex_zeta * lg)
    gch = jnp.exp(jnp.full((1, 1), c, F32) * lg)
    return relc, dm, xi, zeta, ex_xi, ex_zeta, gch


def _ret_fwd(cfg, proj, lgs, y, name):
    s = cfg.s
    c = RET_CHUNK
    n = s // c
    kscale = RET_QK ** -0.5

    def body(lg_ref, q_ref, k_ref, v_ref, z_ref, y_in, o_ref, y_ref, st_ref):
        h = pl.program_id(0)
        tabs = [_decay_tables(lg_ref[dirn, h], dirn == 1) for dirn in range(2)]
        st_ref[...] = jnp.zeros_like(st_ref)
        o_ref[...] = jnp.zeros_like(o_ref)

        def step(t, carry):
            for dirn in range(2):
                _, dm, xi, zeta, _, _, gch = tabs[dirn]
                i = (n - 1 - t) if dirn == 1 else t
                rows = pl.ds(pl.multiple_of(i * c, c), c)
                qi = q_ref[rows, :]
                ks = k_ref[rows, :].astype(F32) * kscale
                vi = v_ref[rows, :]
                inner = lax.dot_general(qi, ks.astype(BF16), _DN["nt"], preferred_element_type=F32) * dm
                st = st_ref[dirn]
                o_ref[rows, :] += (jnp.dot(inner.astype(BF16), vi, preferred_element_type=F32)
                                   + jnp.dot(qi, st.astype(BF16), preferred_element_type=F32) * xi)
                st_ref[dirn] = st * gch + lax.dot_general((ks * zeta).astype(BF16), vi, _DN["tn"],
                                                          preferred_element_type=F32)
            return carry

        lax.fori_loop(0, n, step, 0, unroll=2)

        def gate(i, carry):
            rows = pl.ds(pl.multiple_of(i * _COPY_ROWS, _COPY_ROWS), _COPY_ROWS)
            oh = o_ref[rows, :]
            rr = lax.rsqrt(jnp.mean(oh * oh, axis=-1, keepdims=True) + NORM_EPS)
            y_ref[rows, :] = (oh * rr * _silu(z_ref[rows, :].astype(F32))).astype(BF16)
            return carry

        lax.fori_loop(0, s // _COPY_ROWS, gate, 0)

    return pl.pallas_call(
        body, name=name, grid=(cfg.hr,),
        in_specs=[pl.BlockSpec(memory_space=pltpu.SMEM),
                  pl.BlockSpec((s, RET_QK), lambda h: (0, cfg.qr // RET_QK + h)),
                  pl.BlockSpec((s, RET_QK), lambda h: (0, cfg.kr // RET_QK + h)),
                  pl.BlockSpec((s, RET_V), lambda h: (0, cfg.vr // RET_V + h)),
                  pl.BlockSpec((s, RET_V), lambda h: (0, cfg.zr // RET_V + h)), _ANY],
        out_specs=[pl.BlockSpec((s, RET_V), lambda h: (0, h)),
                   pl.BlockSpec((s, RET_V), lambda h: (0, cfg.aw // RET_V + h))],
        out_shape=[jax.ShapeDtypeStruct((s, cfg.rw), F32), jax.ShapeDtypeStruct(y.shape, BF16)],
        input_output_aliases={5: 1},
        scratch_shapes=[pltpu.VMEM((2, RET_QK, RET_V), F32)],
        compiler_params=_params(("parallel",)),
    )(lgs, proj, proj, proj, proj, y)


def _ret_bwd(cfg, proj, lgs, dy, oret, dproj, name):
    s = cfg.s
    c = RET_CHUNK
    n = s // c
    kscale = RET_QK ** -0.5

    def accumulate(lg_ref, q_ref, k_ref, v_ref, do_ref, dq_ref, dk_ref, dv_ref, dlg_ref, states, t_ref,
                   e_dm, e_xi, e_zeta, e_g):
        h = pl.program_id(0)
        tabs = [_decay_tables(lg_ref[dirn, h], dirn == 1) for dirn in range(2)]
        dlg_ref[...] = jnp.zeros_like(dlg_ref)
        dq_ref[...] = jnp.zeros_like(dq_ref)
        dk_ref[...] = jnp.zeros_like(dk_ref)
        dv_ref[...] = jnp.zeros_like(dv_ref)

        def chunk_rows(dirn, t):
            i = (n - 1 - t) if dirn == 1 else t
            return pl.ds(pl.multiple_of(i * c, c), c)

        t_ref[...] = jnp.zeros_like(t_ref)

        def fwd_step(t, carry):
            for dirn in range(2):
                _, _, _, zeta, _, _, gch = tabs[dirn]
                rows = chunk_rows(dirn, t)
                st = t_ref[dirn]
                states[dirn, t] = st
                ks = k_ref[rows, :].astype(F32) * kscale
                t_ref[dirn] = st * gch + lax.dot_general((ks * zeta).astype(BF16), v_ref[rows, :], _DN["tn"],
                                                         preferred_element_type=F32)
            return carry

        lax.fori_loop(0, n, fwd_step, 0, unroll=2)
        t_ref[...] = jnp.zeros_like(t_ref)

        for ref in (e_dm, e_xi, e_zeta, e_g):
            ref[...] = jnp.zeros_like(ref)

        def bwd_step(u, carry):
            t = n - 1 - u
            for dirn in range(2):
                relc, dm, xi, zeta, ex_xi, ex_zeta, gch = tabs[dirn]
                rows = chunk_rows(dirn, t)
                qi = q_ref[rows, :]
                ks = k_ref[rows, :].astype(F32) * kscale
                ksb = ks.astype(BF16)
                vi = v_ref[rows, :]
                doi = do_ref[rows, :]
                sn_f = states[dirn, t]
                sn = sn_f.astype(BF16)
                tt = t_ref[dirn]
                ttb = tt.astype(BF16)
                a_mat = lax.dot_general(qi, ksb, _DN["nt"], preferred_element_type=F32) * dm
                dov = lax.dot_general(doi, vi, _DN["nt"], preferred_element_type=F32)
                b_mat = (dov * dm).astype(BF16)
                kz = (ks * zeta).astype(BF16)
                d_v = (jnp.dot(kz, ttb, preferred_element_type=F32)
                       + lax.dot_general(a_mat.astype(BF16), doi, _DN["tn"], preferred_element_type=F32))
                dk_inter = lax.dot_general(vi, ttb, _DN["nt"], preferred_element_type=F32) * zeta
                d_k = lax.dot_general(b_mat, qi, _DN["tn"], preferred_element_type=F32) + dk_inter
                o_inter = jnp.dot(qi, sn, preferred_element_type=F32) * xi
                d_q = (jnp.dot(b_mat, ksb, preferred_element_type=F32)
                       + lax.dot_general(doi, sn, _DN["nt"], preferred_element_type=F32) * xi)
                e_dm[dirn] += relc * a_mat * dov
                e_xi[dirn] += ex_xi * (doi.astype(F32) * o_inter)
                e_zeta[dirn] += ex_zeta * (ks * dk_inter)
                e_g[dirn] += tt * sn_f
                t_ref[dirn] = tt * gch + lax.dot_general((qi.astype(F32) * xi).astype(BF16), doi, _DN["tn"],
                                                         preferred_element_type=F32)
                dq_ref[rows, :] += d_q
                dk_ref[rows, :] += d_k * kscale
                dv_ref[rows, :] += d_v
            return carry

        lax.fori_loop(0, n, bwd_step, 0, unroll=2)
        for dirn in range(2):
            total = (jnp.sum(e_dm[dirn], keepdims=True) + jnp.sum(e_xi[dirn], keepdims=True)
                     + jnp.sum(e_zeta[dirn], keepdims=True) + (c * tabs[dirn][6]) * jnp.sum(e_g[dirn], keepdims=True))
            dlg_ref[0, dirn:dirn + 1, :] = jnp.broadcast_to(total, (1, LANES))

    def body(lg_ref, q_ref, k_ref, v_ref, z_ref, dy_ref, or_ref, dproj_in, dproj_ref, dlg_ref,
             do_s, dq_s, dk_s, dv_s, stage_z, stage_q, stage_k, stage_v, sems, *scratch):
        head = pl.program_id(0)
        stages = (stage_z, stage_q, stage_k, stage_v)
        firsts = (cfg.zr, cfg.qr, cfg.kr, cfg.vr)

        def out_copy(slot):
            width = stages[slot].shape[1]
            cols = pl.ds(pl.multiple_of(firsts[slot] + head * width, LANES), width)
            return pltpu.make_async_copy(stages[slot], dproj_ref.at[:, cols], sems.at[slot])

        @pl.when(head > 0)
        def _():
            out_copy(0).wait()

        def gate_norm_bwd(i, carry):
            rows = pl.ds(pl.multiple_of(i * _COPY_ROWS, _COPY_ROWS), _COPY_ROWS)
            oh = or_ref[rows, :]
            zr = z_ref[rows, :].astype(F32)
            dyr = dy_ref[rows, :].astype(F32)
            rr = lax.rsqrt(jnp.mean(oh * oh, axis=-1, keepdims=True) + NORM_EPS)
            yn = oh * rr
            dyn = dyr * _silu(zr)
            stage_z[rows, :] = (dyr * yn * _dsilu(zr)).astype(BF16)
            do_s[rows, :] = (rr * (dyn - yn * jnp.mean(dyn * yn, axis=-1, keepdims=True))).astype(BF16)
            return carry

        lax.fori_loop(0, s // _COPY_ROWS, gate_norm_bwd, 0)
        out_copy(0).start()

        accumulate(lg_ref, q_ref, k_ref, v_ref, do_s, dq_s, dk_s, dv_s, dlg_ref, *scratch)

        @pl.when(head > 0)
        def _():
            for slot in (1, 2, 3):
                out_copy(slot).wait()

        def emit(i, carry):
            rows = pl.ds(pl.multiple_of(i * _COPY_ROWS, _COPY_ROWS), _COPY_ROWS)
            stage_q[rows, :] = dq_s[rows, :].astype(BF16)
            stage_k[rows, :] = dk_s[rows, :].astype(BF16)
            stage_v[rows, :] = dv_s[rows, :].astype(BF16)
            return carry

        lax.fori_loop(0, s // _COPY_ROWS, emit, 0)
        for slot in (1, 2, 3):
            out_copy(slot).start()

        @pl.when(head == pl.num_programs(0) - 1)
        def _():
            for slot in range(4):
                out_copy(slot).wait()

    return pl.pallas_call(
        body, name=name, grid=(cfg.hr,),
        in_specs=[pl.BlockSpec(memory_space=pltpu.SMEM),
                  pl.BlockSpec((s, RET_QK), lambda h: (0, cfg.qr // RET_QK + h)),
                  pl.BlockSpec((s, RET_QK), lambda h: (0, cfg.kr // RET_QK + h)),
                  pl.BlockSpec((s, RET_V), lambda h: (0, cfg.vr // RET_V + h)),
                  pl.BlockSpec((s, RET_V), lambda h: (0, cfg.zr // RET_V + h)),
                  pl.BlockSpec((s, RET_V), lambda h: (0, cfg.aw // RET_V + h)),
                  pl.BlockSpec((s, RET_V), lambda h: (0, h)), _ANY],
        out_specs=[_ANY, pl.BlockSpec((1, 8, LANES), lambda h: (h, 0, 0))],
        out_shape=[jax.ShapeDtypeStruct(dproj.shape, BF16), jax.ShapeDtypeStruct((cfg.hr, 8, LANES), F32)],
        input_output_aliases={7: 0},
        scratch_shapes=[pltpu.VMEM((s, RET_V), BF16),
                        pltpu.VMEM((s, RET_QK), F32), pltpu.VMEM((s, RET_QK), F32), pltpu.VMEM((s, RET_V), F32),
                        pltpu.VMEM((s, RET_V), BF16), pltpu.VMEM((s, RET_QK), BF16), pltpu.VMEM((s, RET_QK), BF16),
                        pltpu.VMEM((s, RET_V), BF16), pltpu.SemaphoreType.DMA((4,)),
                        pltpu.VMEM((2, n, RET_QK, RET_V), F32), pltpu.VMEM((2, RET_QK, RET_V), F32),
                        pltpu.VMEM((2, c, c), F32), pltpu.VMEM((2, c, RET_V), F32),
                        pltpu.VMEM((2, c, RET_QK), F32), pltpu.VMEM((2, RET_QK, RET_V), F32)],
        compiler_params=_params(("arbitrary",)),
    )(lgs, proj, proj, proj, proj, dy, oret, dproj)


def _ada_fwd(cact16, w_ada, name):
    depth, d, n = w_ada.shape
    tn = _tile(n, 768)

    def body(c_ref, w_ref, o_ref):
        o_ref[0] = jnp.dot(c_ref[...], w_ref[0].astype(BF16), preferred_element_type=F32)

    return pl.pallas_call(
        body, name=name, grid=(depth, n // tn),
        in_specs=[pl.BlockSpec((16, d), lambda l, j: (0, 0)), pl.BlockSpec((1, d, tn), lambda l, j: (l, 0, j))],
        out_specs=pl.BlockSpec((1, 16, tn), lambda l, j: (l, 0, j)),
        out_shape=jax.ShapeDtypeStruct((depth, 16, n), F32),
        compiler_params=_params(("parallel", "parallel")),
    )(cact16, w_ada)


def _adam_math(w, g, m, v):
    m2 = ADAM_B1 * m + (1.0 - ADAM_B1) * g
    v2 = ADAM_B2 * v + (1.0 - ADAM_B2) * (g * g)
    m_hat = m2 / (1.0 - ADAM_B1 ** ADAM_STEP)
    v_hat = v2 / (1.0 - ADAM_B2 ** ADAM_STEP)
    delta = -ADAM_LR * (m_hat / (jnp.sqrt(v_hat) + ADAM_EPS) + ADAM_WD * w)
    return delta, m2, v2


def _adamw_big(w, g, m, v, first, count, prev, name):
    _, r, c = w.shape
    tr, tc = _tile(r, 512), _tile(c, 1024)

    def body(w_ref, g_ref, m_ref, v_ref, *rest):
        go_ref, d_ref, mo_ref, vo_ref = rest[-4:]
        gv = g_ref[...]
        delta, m2, v2 = _adam_math(w_ref[...], gv, m_ref[...], v_ref[...])
        go_ref[...] = gv
        d_ref[...] = delta
        mo_ref[...] = m2
        vo_ref[...] = v2

    spec = pl.BlockSpec((1, tr, tc), lambda l, i, j: (first + l, i, j))
    shp = jax.ShapeDtypeStruct(w.shape, F32)
    carried = [] if prev is None else list(prev)
    return pl.pallas_call(
        body, name=name, grid=(count, r // tr, c // tc),
        in_specs=[spec] * 4 + [_ANY] * len(carried), out_specs=[spec] * 4, out_shape=[shp] * 4,
        input_output_aliases={4 + k: k for k in range(len(carried))},
        compiler_params=_params(("parallel", "parallel", "parallel")),
    )(w, g, m, v, *carried)


def _adamw_ada(w, m, v, cact128, dmod128, name):
    depth, r, c = w.shape
    tr, tc = _tile(r, 512), _tile(c, 768)

    def body(w_ref, m_ref, v_ref, c_ref, dm_ref, go_ref, d_ref, mo_ref, vo_ref):
        gv = lax.dot_general(c_ref[...], dm_ref[0], _DN["tn"], preferred_element_type=F32)
        delta, m2, v2 = _adam_math(w_ref[0], gv, m_ref[0], v_ref[0])
        go_ref[0] = gv
        d_ref[0] = delta
        mo_ref[0] = m2
        vo_ref[0] = v2

    spec = pl.BlockSpec((1, tr, tc), lambda l, i, j: (l, i, j))
    shp = jax.ShapeDtypeStruct(w.shape, F32)
    return pl.pallas_call(
        body, name=name, grid=(depth, r // tr, c // tc),
        in_specs=[spec] * 3 + [pl.BlockSpec((128, tr), lambda l, i, j: (0, i)),
                               pl.BlockSpec((1, 128, tc), lambda l, i, j: (l, 0, j))],
        out_specs=[spec] * 4, out_shape=[shp] * 4,
        compiler_params=_params(("parallel", "parallel", "parallel")),
    )(w, m, v, cact128, dmod128)


def _adamw_small(w, g, m, v, name):
    def body(w_ref, g_ref, m_ref, v_ref, d_ref, mo_ref, vo_ref):
        delta, m2, v2 = _adam_math(w_ref[...], g_ref[...], m_ref[...], v_ref[...])
        d_ref[...] = delta
        mo_ref[...] = m2
        vo_ref[...] = v2

    shp = jax.ShapeDtypeStruct(w.shape, F32)
    return pl.pallas_call(body, name=name, out_shape=[shp] * 3)(w, g, m, v)


def _sum_gathered(parts, name):
    nd, r, c = parts.shape

    def body(p_ref, o_ref):
        acc = p_ref[0]
        for e in range(1, nd):
            acc = acc + p_ref[e]
        o_ref[...] = acc

    return pl.pallas_call(body, name=name, out_shape=jax.ShapeDtypeStruct((r, c), F32))(parts)


def _flip(v, bit):
    return 1 - v if bit else v


def _all_gather_small(x, name):
    r, c = x.shape

    def body(x_ref, out_ref, send_sems, recv_sems, local_sem):
        mx, my, mc = lax.axis_index("x"), lax.axis_index("y"), lax.axis_index("c")
        me = 4 * mx + 2 * my + mc
        mine = pltpu.make_async_copy(x_ref, out_ref.at[me], local_sem)
        mine.start()
        sends = []
        for k in range(1, N_DEV):
            peer = (_flip(mx, k & 4), _flip(my, k & 2), _flip(mc, k & 1))
            cp = pltpu.make_async_remote_copy(src_ref=x_ref, dst_ref=out_ref.at[me], send_sem=send_sems.at[k - 1],
                                              recv_sem=recv_sems.at[k - 1], device_id=peer, device_id_type=MESH)
            cp.start()
            sends.append(cp)
        for k in range(1, N_DEV):
            peer = (_flip(mx, k & 4), _flip(my, k & 2), _flip(mc, k & 1))
            src = 4 * peer[0] + 2 * peer[1] + peer[2]
            pltpu.make_async_remote_copy(src_ref=x_ref, dst_ref=out_ref.at[src], send_sem=send_sems.at[k - 1],
                                         recv_sem=recv_sems.at[k - 1], device_id=peer,
                                         device_id_type=MESH).wait_recv()
        for cp in sends:
            cp.wait_send()
        mine.wait()

    return pl.pallas_call(
        body, name=name,
        out_shape=jax.ShapeDtypeStruct((N_DEV, r, c), x.dtype),
        in_specs=[pl.BlockSpec(memory_space=pltpu.VMEM)],
        out_specs=pl.BlockSpec(memory_space=pltpu.VMEM),
        scratch_shapes=[pltpu.SemaphoreType.DMA((N_DEV - 1,)), pltpu.SemaphoreType.DMA((N_DEV - 1,)),
                        pltpu.SemaphoreType.DMA],
        compiler_params=pltpu.CompilerParams(vmem_limit_bytes=VMEM_LIMIT),
    )(x)


_HBM = pl.BlockSpec(memory_space=pltpu.HBM)
_SEM = pl.BlockSpec(memory_space=pltpu.SEMAPHORE)
_ANY = pl.BlockSpec(memory_space=pl.ANY)
_EFFECT = pltpu.SideEffectType.DATAFLOW_SIDE_EFFECTING


def _in_hbm(a):
    return pltpu.with_memory_space_constraint(a, pltpu.HBM)


def _place_w_in(w, layer, chip1, name):
    _, d, fc = w.shape
    tr = _tile(d, 512)

    def body(c_ref, w_ref, o_ref):
        o_ref[...] = w_ref[...].astype(BF16)

    return pl.pallas_call(
        body, name=name,
        grid_spec=pltpu.PrefetchScalarGridSpec(
            num_scalar_prefetch=1, grid=(d // tr,),
            in_specs=[pl.BlockSpec((None, tr, fc), lambda i, c: (layer, i, 0))],
            out_specs=pl.BlockSpec((tr, fc), lambda i, c: (i, c[0]))),
        out_shape=jax.ShapeDtypeStruct((d, N_CHIP * fc), BF16),
        compiler_params=_params(("parallel",)),
    )(chip1, w)


def _place_w_out(w, layer, chip1, name):
    _, rc, dd = w.shape
    tc = _tile(dd, 1024)

    def body(c_ref, w_ref, o_ref):
        o_ref[...] = w_ref[...].astype(BF16)

    return pl.pallas_call(
        body, name=name,
        grid_spec=pltpu.PrefetchScalarGridSpec(
            num_scalar_prefetch=1, grid=(dd // tc,),
            in_specs=[pl.BlockSpec((None, rc, tc), lambda j, c: (layer, 0, j))],
            out_specs=pl.BlockSpec((rc, tc), lambda j, c: (c[0], j))),
        out_shape=jax.ShapeDtypeStruct((N_CHIP * rc, dd), BF16),
        compiler_params=_params(("parallel",)),
    )(chip1, w)


def _weight_region(ref, tensor, chip):
    if tensor == 0:
        fc = ref.shape[1] // N_CHIP
        return ref.at[:, pl.ds(pl.multiple_of(chip * fc, LANES), fc)]
    rc = ref.shape[0] // N_CHIP
    return ref.at[pl.ds(pl.multiple_of(chip * rc, 8), rc), :]


def _gather_copies(ref, tensor, send_sems, recv_sems, landing):
    mx, my, mc = lax.axis_index("x"), lax.axis_index("y"), lax.axis_index("c")
    mine = _weight_region(ref, tensor, 2 * mx + my)
    copies = []
    for j in range(1, N_CHIP):
        peer = (_flip(mx, j & 2), _flip(my, j & 1), mc)
        dst = _weight_region(ref, tensor, 2 * peer[0] + peer[1]) if landing else mine
        idx = 2 * (j - 1) + tensor
        copies.append(pltpu.make_async_remote_copy(src_ref=mine, dst_ref=dst, send_sem=send_sems.at[idx],
                                                   recv_sem=recv_sems.at[idx], device_id=peer, device_id_type=MESH))
    return copies


def _gather_start(fi, fo, dep, name):
    ns = 2 * (N_CHIP - 1)

    def body(fi_ref, fo_ref, dep_ref, send_sems, recv_sems, fi_thru, fo_thru, token):
        for tensor, ref in enumerate((fi_ref, fo_ref)):
            for cp in _gather_copies(ref, tensor, send_sems, recv_sems, landing=False):
                cp.start()
        token[...] = jnp.zeros_like(token)

    return pl.pallas_call(
        body, name=name,
        out_shape=(pltpu.SemaphoreType.DMA((ns,)), pltpu.SemaphoreType.DMA((ns,)),
                   pltpu.HBM(fi.shape, fi.dtype), pltpu.HBM(fo.shape, fo.dtype),
                   jax.ShapeDtypeStruct((8, LANES), F32)),
        in_specs=(_HBM, _HBM, _ANY),
        out_specs=(_SEM, _SEM, _HBM, _HBM, pl.BlockSpec(memory_space=pltpu.VMEM)),
        input_output_aliases={0: 2, 1: 3},
        compiler_params=pltpu.CompilerParams(has_side_effects=_EFFECT),
    )(_in_hbm(fi), _in_hbm(fo), dep)


def _gather_wait(send_sems, recv_sems, buf, tensor, after, name):
    def body(buf_ref, send_sems, recv_sems, after_ref, buf_out):
        for cp in _gather_copies(buf_ref, tensor, send_sems, recv_sems, landing=True):
            cp.wait_send()
            cp.wait_recv()

    return pl.pallas_call(
        body, name=name,
        out_shape=pltpu.HBM(buf.shape, buf.dtype),
        in_specs=(_HBM, _SEM, _SEM, _ANY), out_specs=_HBM,
        input_output_aliases={0: 0},
        compiler_params=pltpu.CompilerParams(has_side_effects=_EFFECT),
    )(buf, send_sems, recv_sems, after)


def _w_in_half(ref, half, chip):
    hr = ref.shape[0] // 2
    fc = ref.shape[1] // N_CHIP
    return ref.at[pl.ds(pl.multiple_of(half * hr, 8), hr), pl.ds(pl.multiple_of(chip * fc, LANES), fc)]


def _half_copies_ici(ref, send_sems, recv_sems, landing):
    mx, my, mc = lax.axis_index("x"), lax.axis_index("y"), lax.axis_index("c")
    mine = _w_in_half(ref, mc, 2 * mx + my)
    copies = []
    for j in range(1, N_CHIP):
        peer = (_flip(mx, j & 2), _flip(my, j & 1), mc)
        dst = _w_in_half(ref, mc, 2 * peer[0] + peer[1]) if landing else mine
        copies.append(pltpu.make_async_remote_copy(src_ref=mine, dst_ref=dst, send_sem=send_sems.at[j - 1],
                                                   recv_sem=recv_sems.at[j - 1], device_id=peer, device_id_type=MESH))
    return copies


def _half_copies_d2d(ref, send_sems, recv_sems, landing):
    mx, my, mc = lax.axis_index("x"), lax.axis_index("y"), lax.axis_index("c")
    sib = (mx, my, 1 - mc)
    copies = []
    for j in range(1, N_CHIP):
        other = 2 * _flip(mx, j & 2) + _flip(my, j & 1)
        src = _w_in_half(ref, mc, other)
        dst = _w_in_half(ref, 1 - mc, other) if landing else src
        copies.append(pltpu.make_async_remote_copy(src_ref=src, dst_ref=dst, send_sem=send_sems.at[j - 1],
                                                   recv_sem=recv_sems.at[j - 1], device_id=sib, device_id_type=MESH))
    return copies


def _w_out_copies(ref, send_sems, recv_sems, landing):
    mx, my, mc = lax.axis_index("x"), lax.axis_index("y"), lax.axis_index("c")
    mine = _weight_region(ref, 1, 2 * mx + my)
    copies = []
    for j in range(1, N_CHIP):
        peer = (_flip(mx, j & 2), _flip(my, j & 1), mc)
        dst = _weight_region(ref, 1, 2 * peer[0] + peer[1]) if landing else mine
        copies.append(pltpu.make_async_remote_copy(src_ref=mine, dst_ref=dst, send_sem=send_sems.at[j - 1],
                                                   recv_sem=recv_sems.at[j - 1], device_id=peer, device_id_type=MESH))
    return copies


def _copies_start(buf, copies, dep, name):
    ns = N_CHIP - 1

    def body(buf_ref, dep_ref, send_sems, recv_sems, buf_thru, token):
        for cp in copies(buf_ref, send_sems, recv_sems, landing=False):
            cp.start()
        token[...] = jnp.zeros_like(token)

    return pl.pallas_call(
        body, name=name,
        out_shape=(pltpu.SemaphoreType.DMA((ns,)), pltpu.SemaphoreType.DMA((ns,)),
                   pltpu.HBM(buf.shape, buf.dtype), jax.ShapeDtypeStruct((8, LANES), F32)),
        in_specs=(_HBM, _ANY),
        out_specs=(_SEM, _SEM, _HBM, pl.BlockSpec(memory_space=pltpu.VMEM)),
        input_output_aliases={0: 2},
        compiler_params=pltpu.CompilerParams(has_side_effects=_EFFECT),
    )(_in_hbm(buf), dep)


def _copies_wait(send_sems, recv_sems, buf, copies, after, name):
    def body(buf_ref, send_sems, recv_sems, after_ref, buf_out):
        for cp in copies(buf_ref, send_sems, recv_sems, landing=True):
            cp.wait_send()
            cp.wait_recv()

    return pl.pallas_call(
        body, name=name,
        out_shape=pltpu.HBM(buf.shape, buf.dtype),
        in_specs=(_HBM, _SEM, _SEM, _ANY), out_specs=_HBM,
        input_output_aliases={0: 0},
        compiler_params=pltpu.CompilerParams(has_side_effects=_EFFECT),
    )(buf, send_sems, recv_sems, after)


def _scatter_copies(gi_ref, go_ref, pi_ref, po_ref, send_sems, recv_sems):
    mx, my, mc = lax.axis_index("x"), lax.axis_index("y"), lax.axis_index("c")
    hr, fc = pi_ref.shape[1:]
    ro = po_ref.shape[1]
    copies = []
    for k in range(1, N_DEV):
        peer = (_flip(mx, k & 4), _flip(my, k & 2), _flip(mc, k & 1))
        pchip = 2 * peer[0] + peer[1]
        src = (gi_ref.at[pl.ds(pl.multiple_of(peer[2] * hr, 8), hr), pl.ds(pl.multiple_of(pchip * fc, LANES), fc)],
               go_ref.at[pl.ds(pl.multiple_of((2 * pchip + peer[2]) * ro, 8), ro), :])
        dst = (pi_ref.at[k - 1], po_ref.at[k - 1])
        for t in range(2):
            idx = 2 * (k - 1) + t
            copies.append(pltpu.make_async_remote_copy(src_ref=src[t], dst_ref=dst[t], send_sem=send_sems.at[idx],
                                                       recv_sem=recv_sems.at[idx], device_id=peer,
                                                       device_id_type=MESH))
    return copies


def _scatter_start(gi, go, name):
    d, f = gi.shape
    dd = go.shape[1]
    ns = 2 * (N_DEV - 1)
    pi = lax.empty((N_DEV - 1, d // 2, f // N_CHIP), BF16)
    po = lax.empty((N_DEV - 1, d // N_DEV, dd), BF16)

    def body(gi_ref, go_ref, pi_ref, po_ref, send_sems, recv_sems, gi_thru, go_thru, pi_thru, po_thru, token):
        for cp in _scatter_copies(gi_ref, go_ref, pi_ref, po_ref, send_sems, recv_sems):
            cp.start()
        token[...] = jnp.zeros_like(token)

    return pl.pallas_call(
        body, name=name,
        out_shape=(pltpu.SemaphoreType.DMA((ns,)), pltpu.SemaphoreType.DMA((ns,)),
                   pltpu.HBM(gi.shape, gi.dtype), pltpu.HBM(go.shape, go.dtype),
                   pltpu.HBM(pi.shape, pi.dtype), pltpu.HBM(po.shape, po.dtype),
                   jax.ShapeDtypeStruct((8, LANES), F32)),
        in_specs=(_HBM, _HBM, _HBM, _HBM),
        out_specs=(_SEM, _SEM, _HBM, _HBM, _HBM, _HBM, pl.BlockSpec(memory_space=pltpu.VMEM)),
        input_output_aliases={0: 2, 1: 3, 2: 4, 3: 5},
        compiler_params=pltpu.CompilerParams(has_side_effects=_EFFECT),
    )(_in_hbm(gi), _in_hbm(go), _in_hbm(pi), _in_hbm(po))


def _scatter_wait(send_sems, recv_sems, gi, go, pi, po, after, name):
    def body(gi_ref, go_ref, pi_ref, po_ref, send_sems, recv_sems, *rest):
        for cp in _scatter_copies(gi_ref, go_ref, pi_ref, po_ref, send_sems, recv_sems):
            cp.wait_send()
            cp.wait_recv()

    return pl.pallas_call(
        body, name=name,
        out_shape=tuple(pltpu.HBM(a.shape, a.dtype) for a in (gi, go, pi, po)),
        in_specs=(_HBM, _HBM, _HBM, _HBM, _SEM, _SEM) + (_ANY,) * len(after), out_specs=(_HBM, _HBM, _HBM, _HBM),
        input_output_aliases={0: 0, 1: 1, 2: 2, 3: 3},
        compiler_params=pltpu.CompilerParams(has_side_effects=_EFFECT),
    )(gi, go, pi, po, send_sems, recv_sems, *after)


def _sum_into(buf, g, parts, where2, layer, row_blocks, dep, name):
    depth, r2, c = buf.shape
    r = r2 // 2
    tr, tc = _tile(r, 256), _tile(c, 1024)
    nr, nc = r // tr, c // tc
    col_blocks = (g.shape[1] // c) > 1

    def body(w_ref, buf_ref, g_ref, p_ref, dep_ref, o_ref):
        acc = g_ref[...].astype(F32)
        for e in range(N_DEV - 1):
            acc = acc + p_ref[e].astype(F32)
        o_ref[...] = acc

    return pl.pallas_call(
        body, name=name,
        grid_spec=pltpu.PrefetchScalarGridSpec(
            num_scalar_prefetch=1, grid=(nr, nc),
            in_specs=[_ANY,
                      pl.BlockSpec((tr, tc), lambda i, j, w: (row_blocks(w, nr) + i, (w[1] * nc if col_blocks else 0) + j)),
                      pl.BlockSpec((N_DEV - 1, tr, tc), lambda i, j, w: (0, i, j)), _ANY],
            out_specs=pl.BlockSpec((None, tr, tc), lambda i, j, w: (layer, w[0] * nr + i, j))),
        out_shape=jax.ShapeDtypeStruct(buf.shape, F32),
        input_output_aliases={1: 0},
        compiler_params=_params(("parallel", "parallel")),
    )(where2, buf, g, parts, dep)


def _exchange_halves(b_in, b_out, first, count, name):
    r_in = b_in.shape[1]
    r_out = b_out.shape[1]

    def body(bi_ref, bo_ref, oi_ref, oo_ref, send_sems, recv_sems):
        mx, my, mc = lax.axis_index("x"), lax.axis_index("y"), lax.axis_index("c")
        sib = (mx, my, 1 - mc)

        def half(ref, l, rows, which):
            return ref.at[l, pl.ds(pl.multiple_of(which * (rows // 2), 8), rows // 2), :]

        copies = []
        for l in range(first, first + count):
            for t, (src, dst, rows) in enumerate(((bi_ref, oi_ref, r_in), (bo_ref, oo_ref, r_out))):
                idx = 2 * (l - first) + t
                kw = dict(send_sem=send_sems.at[idx], recv_sem=recv_sems.at[idx], device_id=sib, device_id_type=MESH)
                cp = pltpu.make_async_remote_copy(src_ref=half(src, l, rows, mc), dst_ref=half(dst, l, rows, mc), **kw)
                cp.start()
                copies.append((cp, pltpu.make_async_remote_copy(src_ref=half(src, l, rows, mc),
                                                                dst_ref=half(dst, l, rows, 1 - mc), **kw)))
        for cp, landed in copies:
            landed.wait_recv()
        for cp, landed in copies:
            cp.wait_send()

    ns = 2 * count
    return pl.pallas_call(
        body, name=name,
        out_shape=[jax.ShapeDtypeStruct(b_in.shape, F32), jax.ShapeDtypeStruct(b_out.shape, F32)],
        in_specs=[_ANY, _ANY], out_specs=[_ANY, _ANY],
        input_output_aliases={0: 0, 1: 1},
        scratch_shapes=[pltpu.SemaphoreType.DMA((ns,)), pltpu.SemaphoreType.DMA((ns,))],
    )(b_in, b_out)


def _rows8(v):
    return jnp.pad(v, ((0, 8 - v.shape[0]), (0, 0)))


def kernel(x, c, norm_gain, w_ada, b_ada, w_in, w_out, ret_decay_logit_f, ret_decay_logit_b, final_gain, loss_target, m_norm_gain, m_w_ada, m_b_ada, m_w_in, m_w_out, m_ret_decay_logit_f, m_ret_decay_logit_b, m_final_gain, v_norm_gain, v_w_ada, v_b_ada, v_w_in, v_w_out, v_ret_decay_logit_f, v_ret_decay_logit_b, v_final_gain):
    cfg = _Cfg()
    depth, d = norm_gain.shape
    mx, my, mc = lax.axis_index("x"), lax.axis_index("y"), lax.axis_index("c")
    me = 4 * mx + 2 * my + mc
    chip = 2 * mx + my
    x0 = x[0]
    tgt = loss_target[0]
    ada_cols = w_ada.shape[2]

    chip1 = jnp.reshape(chip, (1,)).astype(jnp.int32)
    where2 = jnp.stack([mc, chip]).astype(jnp.int32)

    def start_gather(l, dep):
        return _gather_start(_place_w_in(w_in, l, chip1, f"place_w_in_{l}"),
                             _place_w_out(w_out, l, chip1, f"place_w_out_{l}"), dep, f"gather_start_{l}")

    c_all = _all_gather_small(_rows8(c), "gather_c")[:, 0, :]
    cact = _silu(c_all)
    ici = _copies_start(_place_w_in(w_in, 0, chip1, "place_w_in_0"), _half_copies_ici, c_all, "gather0_ici_start")
    cact16 = jnp.pad(cact, ((0, 8), (0, 0))) + ici[3][0:1, 0:1]
    mod_part = _ada_fwd(cact16.astype(BF16), w_ada, "ada_fwd")
    mod_all = _all_gather_small(mod_part.reshape(depth * 16, ada_cols), "gather_mod")
    mod_all = mod_all.reshape(N_CHIP, 2, depth, 16, ada_cols)[:, 0]
    mod_mine = lax.dynamic_index_in_dim(mod_all, me, axis=2, keepdims=False)
    mod = jnp.transpose(mod_mine, (1, 0, 2)).reshape(depth, 3, d)
    bias = b_ada.reshape(depth, 3, d)
    fi0 = _copies_wait(ici[0], ici[1], ici[2], _half_copies_ici, mod, "gather0_ici_wait")
    d2d = _copies_start(fi0, _half_copies_d2d, mod, "gather0_d2d_start")
    out0 = _copies_start(_place_w_out(w_out, 0, chip1, "place_w_out_0"), _w_out_copies, d2d[3], "gather0_out_start")
    w_in_0 = _copies_wait(d2d[0], d2d[1], d2d[2], _half_copies_d2d, out0[3], "gather0_d2d_wait")

    slopes = jnp.exp2(-8.0 * (jnp.arange(cfg.ha, dtype=F32) + 1.0) / cfg.ha)
    lg_f = jax.nn.log_sigmoid(ret_decay_logit_f)
    lg_b = jax.nn.log_sigmoid(ret_decay_logit_b)

    saved = []
    w_full = []
    h_x = x0
    pending = None
    for l in range(depth):
        if l == 0:
            w_in_l = w_in_0
        else:
            send_sems, recv_sems, fi, fo, _ = pending
            w_in_l = _gather_wait(send_sems, recv_sems, fi, 0, h_x, f"gather_wait_in_{l}")
        g8 = _rows8(norm_gain[l:l + 1])
        if l + 1 < depth:
            pending = start_gather(l + 1, w_in_l)
            g8 = g8 + pending[4][0:1, 0:1]
        mod3, b3 = _rows8(mod[l]), _rows8(bias[l])
        hb = _norm_mod_fwd(h_x, g8, mod3, b3, f"norm_mod_fwd_{l}")
        proj = _matmul(hb, w_in_l, "nn", BF16, f"in_proj_{l}")
        oa, lse, y = _attn_fwd(cfg, proj, slopes, f"attn_fwd_{l}")
        lgs = jnp.stack([lg_f[l], lg_b[l]])
        oret, y = _ret_fwd(cfg, proj, lgs, y, f"ret_fwd_{l}")
        if l == 0:
            w_out_l = _copies_wait(out0[0], out0[1], out0[2], _w_out_copies, y, "gather0_out_wait")
        else:
            w_out_l = _gather_wait(send_sems, recv_sems, fo, 1, y, f"gather_wait_out_{l}")
        w_full.append((w_in_l, w_out_l))
        x_next, out = _out_proj_fwd(y, w_out_l, h_x, mod3, b3, f"out_proj_{l}")
        saved.append((h_x, hb, proj, oret, y, oa, lse, out, g8, mod3, b3, lgs))
        h_x = x_next

    dx, loss8, fin_acc = _final_loss(h_x, tgt, _rows8(final_gain[None]), "final_loss")

    landed = [None] * depth
    d_mod, d_gain, d_lg = [None] * depth, [None] * depth, [None] * depth
    in_flight = None
    for l in reversed(range(depth)):
        x_l, hb, proj, oret, y, oa, lse, out, g8, mod3, b3, lgs = saved[l]
        w_in_l, w_out_l = w_full[l]
        douts, gate_acc = _out_proj_bwd_prep(dx, out, mod3, b3, f"out_proj_bwd_prep_{l}")
        dy = _matmul(douts, w_out_l, "nt", BF16, f"out_proj_dy_{l}")
        g_out_l = _matmul(y, douts, "tn", BF16, f"out_proj_dw_{l}", tk=4096)
        dproj = _attn_bwd(cfg, proj, slopes, dy, oa, lse, f"attn_bwd_{l}")
        dproj, dlg = _ret_bwd(cfg, proj, lgs, dy, oret, dproj, f"ret_bwd_{l}")
        g_in_l = _matmul(hb, dproj, "tn", BF16, f"in_proj_dw_{l}", tk=4096)
        started = _scatter_start(g_in_l, g_out_l, f"scatter_start_{l}")
        dh = _matmul(dproj, w_in_l, "nt", F32, f"in_proj_dh_{l}", tk=3584)
        g8 = g8 + started[-1][0:1, 0:1]
        dx, nm_acc = _norm_mod_bwd(dh, x_l, dx, g8, mod3, b3, f"norm_mod_bwd_{l}")
        d_mod[l] = jnp.concatenate([nm_acc[0], nm_acc[1], gate_acc[0]])
        d_gain[l] = nm_acc[2]
        d_lg[l] = dlg[:, 0:2, 0]
        if in_flight is not None:
            landed[l + 1] = _scatter_wait(*in_flight[:-1], (dx,), f"scatter_wait_{l + 1}")
        in_flight = started

    gw_in = lax.empty(w_in.shape, F32)
    gw_out = lax.empty(w_out.shape, F32)
    res_in = res_out = None
    for first, count in ((1, depth - 1), (0, 1)):
        if first == 0:
            after = (dx,) if res_out is None else (res_in[1], res_out[1])
            landed[0] = _scatter_wait(*in_flight[:-1], after, "scatter_wait_0")
        if count == 0:
            continue
        dep = in_flight[-1]
        for l in range(first, first + count):
            gi, go, pi, po = landed[l]
            gw_in = _sum_into(gw_in, gi, pi, where2, l, lambda w, nr: w[0] * nr, dep, f"sum_w_in_{l}")
            gw_out = _sum_into(gw_out, go, po, where2, l, lambda w, nr: (2 * w[1] + w[0]) * nr, dep,
                               f"sum_w_out_{l}")
        gw_in, gw_out = _exchange_halves(gw_in, gw_out, first, count, f"exchange_halves_{first}")
        res_in = _adamw_big(w_in, gw_in, m_w_in, v_w_in, first, count, res_in, f"adamw_w_in_{first}")
        res_out = _adamw_big(w_out, gw_out, m_w_out, v_w_out, first, count, res_out, f"adamw_w_out_{first}")
    grad_w_in, delta_w_in, new_m_w_in, new_v_w_in = res_in
    grad_w_out, delta_w_out, new_m_w_out, new_v_w_out = res_out

    dmod_mine = jnp.stack(d_mod)
    dmod_gathered = _all_gather_small(_rows8(dmod_mine), "gather_dmod")
    dmod_all = dmod_gathered[:, :depth, :]
    grad_b_ada = _sum_gathered(dmod_gathered, "sum_b_ada")[:depth]
    dmod_cols = lax.dynamic_slice_in_dim(dmod_all, chip * ada_cols, ada_cols, axis=2)
    dmod128 = jnp.pad(jnp.transpose(dmod_cols, (1, 0, 2)), ((0, 0), (0, 120), (0, 0))).astype(BF16)
    cact128 = jnp.pad(cact, ((0, 120), (0, 0))).astype(BF16)
    grad_w_ada, delta_w_ada, new_m_w_ada, new_v_w_ada = _adamw_ada(w_ada, m_w_ada, v_w_ada, cact128, dmod128,
                                                                  "adamw_w_ada")

    dlg_all = jnp.stack(d_lg)
    sig_f = jax.nn.sigmoid(-ret_decay_logit_f)
    sig_b = jax.nn.sigmoid(-ret_decay_logit_b)
    nlg = depth * cfg.hr
    pack = jnp.zeros((8, d), F32)
    for l in range(depth):
        pack = pack.at[l].set(d_gain[l])
    pack = pack.at[depth].set(fin_acc[0])
    pack = pack.at[depth + 1, 0].set(loss8[0, 0])
    pack = pack.at[depth + 1, LANES:LANES + nlg].set((dlg_all[:, :, 0] * sig_f).reshape(-1))
    pack = pack.at[depth + 1, 2 * LANES:2 * LANES + nlg].set((dlg_all[:, :, 1] * sig_b).reshape(-1))
    tot = _sum_gathered(_all_gather_small(pack, "gather_small"), "sum_small")
    grad_norm_gain = tot[:depth]
    grad_final_gain = tot[depth]
    loss = tot[depth + 1, 0]
    grad_lf = tot[depth + 1, LANES:LANES + nlg].reshape(depth, cfg.hr)
    grad_lb = tot[depth + 1, 2 * LANES:2 * LANES + nlg].reshape(depth, cfg.hr)

    d_ng, m_ng, v_ng = _adamw_small(norm_gain, grad_norm_gain, m_norm_gain, v_norm_gain, "adamw_norm_gain")
    d_ba, m_ba, v_ba = _adamw_small(b_ada, grad_b_ada, m_b_ada, v_b_ada, "adamw_b_ada")
    d_lf, m_lf, v_lf = _adamw_small(ret_decay_logit_f, grad_lf, m_ret_decay_logit_f, v_ret_decay_logit_f, "adamw_lf")
    d_lb, m_lb, v_lb = _adamw_small(ret_decay_logit_b, grad_lb, m_ret_decay_logit_b, v_ret_decay_logit_b, "adamw_lb")
    d_fg, m_fg, v_fg = _adamw_small(final_gain[None], grad_final_gain[None], m_final_gain[None], v_final_gain[None],
                                    "adamw_final_gain")

    return (loss, dx[None],
            grad_norm_gain, grad_w_ada, grad_b_ada, grad_w_in, grad_w_out, grad_lf, grad_lb, grad_final_gain,
            d_ng, delta_w_ada, d_ba, delta_w_in, delta_w_out, d_lf, d_lb, d_fg[0],
            m_ng, new_m_w_ada, m_ba, new_m_w_in, new_m_w_out, m_lf, m_lb, m_fg[0],
            v_ng, new_v_w_ada, v_ba, new_v_w_in, new_v_w_out, v_lf, v_lb, v_fg[0])
```

```python
import functools

import jax
import jax.numpy as jnp
from jax import lax
from jax.experimental import pallas as pl
from jax.experimental.pallas import tpu as pltpu

F32 = jnp.float32
BF16 = jnp.bfloat16

D_MODEL = 2048
SEQ = 4096
DEPTH = 4
HEAD_DIM = 128
DILATIONS = (1, 4, 16)
RADIUS = 64
N_HEADS_RET = 4
RET_QK = 128
RET_V = 256
RET_CHUNK = 256
NORM_EPS = 1e-6
MASK_VALUE = -1e30
N_DEV = 8
N_CHIP = 4
LANES = 128
VMEM_LIMIT = 56 * 1024 * 1024

ADAM_LR = 0.001
ADAM_B1 = 0.9
ADAM_B2 = 0.999
ADAM_EPS = 1e-08
ADAM_WD = 0.01
ADAM_STEP = 10

MESH = pl.DeviceIdType.MESH


class _Cfg:
    def __init__(self):
        self.d = D_MODEL
        self.s = SEQ
        self.aw = D_MODEL // 2
        self.ha = self.aw // HEAD_DIM
        self.rw = D_MODEL // 2
        self.hr = N_HEADS_RET
        self.rqk = self.hr * RET_QK
        self.f = 4 * self.aw + 2 * self.rqk + 2 * self.rw
        self.qa, self.ka, self.va, self.za = 0, self.aw, 2 * self.aw, 3 * self.aw
        self.qr = 4 * self.aw
        self.kr = self.qr + self.rqk
        self.vr = self.kr + self.rqk
        self.zr = self.vr + self.rw
        assert self.rw == self.hr * RET_V


def _tile(n, pref):
    t = min(n, pref)
    while n % t or t % LANES:
        t -= LANES
    return t


def _params(dims=None):
    return pltpu.CompilerParams(dimension_semantics=dims, vmem_limit_bytes=VMEM_LIMIT)


def _silu(z):
    return z * jax.nn.sigmoid(z)


def _dsilu(z):
    sg = jax.nn.sigmoid(z)
    return sg * (1.0 + z * (1.0 - sg))


_DN = {"nn": (((1,), (0,)), ((), ())), "nt": (((1,), (1,)), ((), ())), "tn": (((0,), (0,)), ((), ()))}


def _matmul(a, b, mode, out_dtype, name, tm=1024, tn=1024, tk=2048):
    if mode == "tn":
        kk, m = a.shape
    else:
        m, kk = a.shape
    n = b.shape[0] if mode == "nt" else b.shape[1]
    tm, tn, tk = _tile(m, tm), _tile(n, tn), _tile(kk, tk)
    nk = kk // tk
    a_spec = (pl.BlockSpec((tk, tm), lambda i, j, k: (k, i)) if mode == "tn"
              else pl.BlockSpec((tm, tk), lambda i, j, k: (i, k)))
    b_spec = (pl.BlockSpec((tn, tk), lambda i, j, k: (j, k)) if mode == "nt"
              else pl.BlockSpec((tk, tn), lambda i, j, k: (k, j)))
    dn = _DN[mode]

    def body(a_ref, b_ref, o_ref, *acc):
        p = lax.dot_general(a_ref[...], b_ref[...], dn, preferred_element_type=F32)
        if nk == 1:
            o_ref[...] = p.astype(out_dtype)
            return
        acc_ref, = acc
        k = pl.program_id(2)

        @pl.when(k == 0)
        def _():
            acc_ref[...] = p

        @pl.when(k > 0)
        def _():
            acc_ref[...] += p

        @pl.when(k == nk - 1)
        def _():
            o_ref[...] = acc_ref[...].astype(out_dtype)

    return pl.pallas_call(
        body, name=name, grid=(m // tm, n // tn, nk),
        in_specs=[a_spec, b_spec],
        out_specs=pl.BlockSpec((tm, tn), lambda i, j, k: (i, j)),
        out_shape=jax.ShapeDtypeStruct((m, n), out_dtype),
        scratch_shapes=[pltpu.VMEM((tm, tn), F32)] if nk > 1 else [],
        compiler_params=_params(("parallel", "parallel", "arbitrary")),
    )(a, b)


def _out_proj_fwd(y, w_out, x, mod3, b3, name):
    s, kk = y.shape
    d = w_out.shape[1]
    tm, tn = _tile(s, 256), d

    def body(y_ref, w_ref, x_ref, m_ref, b_ref, xn_ref, o_ref):
        out = jnp.dot(y_ref[...], w_ref[...], preferred_element_type=F32)
        gate = m_ref[2:3, :] + b_ref[2:3, :]
        xn_ref[...] = x_ref[...] + gate * out
        o_ref[...] = out.astype(BF16)

    vec = pl.BlockSpec((8, tn), lambda i, j: (0, j))
    return pl.pallas_call(
        body, name=name, grid=(s // tm, d // tn),
        in_specs=[pl.BlockSpec((tm, kk), lambda i, j: (i, 0)), pl.BlockSpec((kk, tn), lambda i, j: (0, j)),
                  pl.BlockSpec((tm, tn), lambda i, j: (i, j)), vec, vec],
        out_specs=[pl.BlockSpec((tm, tn), lambda i, j: (i, j)), pl.BlockSpec((tm, tn), lambda i, j: (i, j))],
        out_shape=[jax.ShapeDtypeStruct((s, d), F32), jax.ShapeDtypeStruct((s, d), BF16)],
        compiler_params=_params(("parallel", "parallel")),
    )(y, w_out, x, mod3, b3)


def _norm_mod_fwd(x, g8, mod3, b3, name):
    s, d = x.shape
    tr = _tile(s, 512)

    def body(x_ref, g_ref, m_ref, b_ref, h_ref):
        xv = x_ref[...]
        r = lax.rsqrt(jnp.mean(xv * xv, axis=-1, keepdims=True) + NORM_EPS)
        shift = m_ref[0:1, :] + b_ref[0:1, :]
        scale = m_ref[1:2, :] + b_ref[1:2, :]
        h_ref[...] = ((xv * r * g_ref[0:1, :]) * (1.0 + scale) + shift).astype(BF16)

    vec = pl.BlockSpec((8, d), lambda i: (0, 0))
    return pl.pallas_call(
        body, name=name, grid=(s // tr,),
        in_specs=[pl.BlockSpec((tr, d), lambda i: (i, 0)), vec, vec, vec],
        out_specs=pl.BlockSpec((tr, d), lambda i: (i, 0)),
        out_shape=jax.ShapeDtypeStruct((s, d), BF16),
        compiler_params=_params(("parallel",)),
    )(x, g8, mod3, b3)


def _norm_mod_bwd(dh, x, dx_next, g8, mod3, b3, name):
    s, d = x.shape
    tr = _tile(s, 256)

    def body(dh_ref, x_ref, dn_ref, g_ref, m_ref, b_ref, dx_ref, acc_ref):
        @pl.when(pl.program_id(0) == 0)
        def _():
            acc_ref[...] = jnp.zeros_like(acc_ref)

        xv = x_ref[...]
        dh_v = dh_ref[...]
        g = g_ref[0:1, :]
        r = lax.rsqrt(jnp.mean(xv * xv, axis=-1, keepdims=True) + NORM_EPS)
        xn = xv * r
        scale1 = 1.0 + m_ref[1:2, :] + b_ref[1:2, :]
        dhs = dh_v * scale1
        dxn = dhs * g
        dx_ref[...] = dn_ref[...] + r * (dxn - xn * jnp.mean(dxn * xn, axis=-1, keepdims=True))
        acc_ref[0:1, :] += jnp.sum(dh_v, axis=0, keepdims=True)
        acc_ref[1:2, :] += jnp.sum(dh_v * (xn * g), axis=0, keepdims=True)
        acc_ref[2:3, :] += jnp.sum(dhs * xn, axis=0, keepdims=True)

    vec = pl.BlockSpec((8, d), lambda i: (0, 0))
    row = pl.BlockSpec((tr, d), lambda i: (i, 0))
    return pl.pallas_call(
        body, name=name, grid=(s // tr,),
        in_specs=[row, row, row, vec, vec, vec],
        out_specs=[row, vec],
        out_shape=[jax.ShapeDtypeStruct((s, d), F32), jax.ShapeDtypeStruct((8, d), F32)],
        compiler_params=_params(("arbitrary",)),
    )(dh, x, dx_next, g8, mod3, b3)


def _final_loss(x, tgt, g8, name):
    s, d = x.shape
    tr = _tile(s, 256)

    def body(x_ref, t_ref, g_ref, dx_ref, loss_ref, acc_ref):
        @pl.when(pl.program_id(0) == 0)
        def _():
            acc_ref[...] = jnp.zeros_like(acc_ref)
            loss_ref[...] = jnp.zeros_like(loss_ref)

        xv = x_ref[...]
        g = g_ref[0:1, :]
        r = lax.rsqrt(jnp.mean(xv * xv, axis=-1, keepdims=True) + NORM_EPS)
        xn = xv * r
        err = xn * g - t_ref[...]
        loss_ref[...] += 0.5 * jnp.sum(jnp.mean(err * err, axis=-1, keepdims=True), axis=0, keepdims=True)
        dy = err * (1.0 / d)
        acc_ref[0:1, :] += jnp.sum(dy * xn, axis=0, keepdims=True)
        dxn = dy * g
        dx_ref[...] = r * (dxn - xn * jnp.mean(dxn * xn, axis=-1, keepdims=True))

    vec = pl.BlockSpec((8, d), lambda i: (0, 0))
    row = pl.BlockSpec((tr, d), lambda i: (i, 0))
    return pl.pallas_call(
        body, name=name, grid=(s // tr,),
        in_specs=[row, row, vec],
        out_specs=[row, pl.BlockSpec((8, LANES), lambda i: (0, 0)), vec],
        out_shape=[jax.ShapeDtypeStruct((s, d), F32), jax.ShapeDtypeStruct((8, LANES), F32),
                   jax.ShapeDtypeStruct((8, d), F32)],
        compiler_params=_params(("arbitrary",)),
    )(x, tgt, g8)


def _out_proj_bwd_prep(dxn, out, mod3, b3, name):
    s, d = dxn.shape
    tr = _tile(s, 512)

    def body(dx_ref, o_ref, m_ref, b_ref, do_ref, acc_ref):
        @pl.when(pl.program_id(0) == 0)
        def _():
            acc_ref[...] = jnp.zeros_like(acc_ref)

        dxv = dx_ref[...]
        gate = m_ref[2:3, :] + b_ref[2:3, :]
        do_ref[...] = (gate * dxv).astype(BF16)
        acc_ref[0:1, :] += jnp.sum(dxv * o_ref[...].astype(F32), axis=0, keepdims=True)

    vec = pl.BlockSpec((8, d), lambda i: (0, 0))
    row = pl.BlockSpec((tr, d), lambda i: (i, 0))
    return pl.pallas_call(
        body, name=name, grid=(s // tr,),
        in_specs=[row, row, vec, vec],
        out_specs=[row, vec],
        out_shape=[jax.ShapeDtypeStruct((s, d), BF16), jax.ShapeDtypeStruct((8, d), F32)],
        compiler_params=_params(("arbitrary",)),
    )(dxn, out, mod3, b3)


_COPY_ROWS = 512


_ATTN_WIN = 384


def _attn_geometry(cfg, dil):
    sub = cfg.s // dil
    if sub <= _ATTN_WIN:
        return sub, sub, sub
    return sub, _ATTN_WIN - 2 * RADIUS, _ATTN_WIN


_N_SHIFTS = 3


def _attn_rows(dil, r, i, sub, bq, win):
    margin = (win - bq) // 2
    ws = jnp.clip(i * bq - margin, 0, sub - win)
    shift = (i * bq - ws) // margin if margin else 0
    if dil == 1:
        return (shift, pl.ds(pl.multiple_of(i * bq, bq), bq), pl.ds(pl.multiple_of(ws, 64), win))
    return (shift, pl.ds(r + i * (bq * dil), bq, stride=dil), pl.ds(r + ws * dil, win, stride=dil))


def _fill_bias_tables(cfg, bias_ref, slope):
    for pattern, dil in enumerate(DILATIONS):
        _, bq, win = _attn_geometry(cfg, dil)
        margin = (win - bq) // 2
        rel0 = lax.broadcasted_iota(jnp.int32, (bq, win), 1) - lax.broadcasted_iota(jnp.int32, (bq, win), 0)
        for shift in range(_N_SHIFTS if margin else 1):
            arel = jnp.abs(rel0 - shift * margin)
            bias_ref[pattern * _N_SHIFTS + shift, 0:bq, 0:win] = jnp.where(
                arel <= RADIUS, -(slope * dil) * arel.astype(F32), MASK_VALUE)


def _bias_scratch():
    return pltpu.VMEM((len(DILATIONS) * _N_SHIFTS, _ATTN_WIN - 2 * RADIUS, _ATTN_WIN), F32)


def _attn_scores(cfg, bias_ref, q, kw, pattern, shift):
    bq, win = q.shape[0], kw.shape[0]
    return (lax.dot_general(q, kw, _DN["nt"], preferred_element_type=F32)
            + bias_ref[pattern * _N_SHIFTS + shift, 0:bq, 0:win])


def _for_each_group(cfg, group, size):
    for pattern, dil in reversed(list(enumerate(DILATIONS))):
        first = pattern == len(DILATIONS) - 1
        sub, bq, _ = _attn_geometry(cfg, dil)
        nblk = sub // bq
        per = min(size, nblk)
        for r in range(dil):
            if nblk == per:
                group(pattern, dil, first, [(r, i) for i in range(nblk)])
            else:
                def step(g, carry, pattern=pattern, dil=dil, first=first, r=r, per=per):
                    group(pattern, dil, first, [(r, g * per + j) for j in range(per)])
                    return carry

                lax.fori_loop(0, nblk // per, step, 0)


def _attn_fwd(cfg, proj, slopes, name):
    s = cfg.s
    scale = HEAD_DIM ** -0.5

    def body(sl_ref, q_ref, k_ref, v_ref, z_ref, o_ref, lse_ref, y_ref, qf, kf, vf, acc, m_s, l_s, bias_s):
        slope = sl_ref[pl.program_id(0)]

        def to_f32(i, carry):
            rows = pl.ds(pl.multiple_of(i * _COPY_ROWS, _COPY_ROWS), _COPY_ROWS)
            qf[rows, :] = q_ref[rows, :].astype(F32) * scale
            kf[rows, :] = k_ref[rows, :].astype(F32)
            vf[rows, :] = v_ref[rows, :].astype(F32)
            return carry

        lax.fori_loop(0, s // _COPY_ROWS, to_f32, 0)

        _fill_bias_tables(cfg, bias_s, slope)

        def group(pattern, dil, first, blocks):
            sub, bq, win = _attn_geometry(cfg, dil)
            rep = win // HEAD_DIM
            work = []
            for r, i in blocks:
                shift, qrows, krows = _attn_rows(dil, r, i, sub, bq, win)
                old = None if first else (m_s[qrows, :], l_s[qrows, :], acc[qrows, :])
                work.append((shift, qrows, qf[qrows, :].astype(BF16), kf[krows, :].astype(BF16),
                             vf[krows, :].astype(BF16), old))
            new = []
            for shift, qrows, q, kw, vw, old in work:
                sc = _attn_scores(cfg, bias_s, q, kw, pattern, shift)
                m_blk = jnp.max(sc, axis=-1, keepdims=True)
                if first:
                    m_new = jnp.broadcast_to(m_blk, (bq, HEAD_DIM))
                    p = jnp.exp(sc - m_blk)
                    l_new = jnp.broadcast_to(jnp.sum(p, axis=-1, keepdims=True), (bq, HEAD_DIM))
                    a_new = jnp.dot(p.astype(BF16), vw, preferred_element_type=F32)
                else:
                    m_old, l_old, a_old = old
                    m_new = jnp.maximum(m_old, m_blk)
                    alpha = jnp.exp(m_old - m_new)
                    p = jnp.exp(sc - jnp.tile(m_new, (1, rep)))
                    l_new = alpha * l_old + jnp.sum(p, axis=-1, keepdims=True)
                    a_new = alpha * a_old + jnp.dot(p.astype(BF16), vw, preferred_element_type=F32)
                new.append((qrows, m_new, l_new, a_new))
            for qrows, m_new, l_new, a_new in new:
                m_s[qrows, :] = m_new
                l_s[qrows, :] = l_new
                acc[qrows, :] = a_new

        _for_each_group(cfg, group, size=4)

        def finish(i, carry):
            rows = pl.ds(pl.multiple_of(i * _COPY_ROWS, _COPY_ROWS), _COPY_ROWS)
            den = l_s[rows, :]
            o = (acc[rows, :] / den).astype(BF16)
            o_ref[rows, :] = o
            lse_ref[rows, :] = m_s[rows, :] + jnp.log(den)
            y_ref[rows, :] = (o.astype(F32) * _silu(z_ref[rows, :].astype(F32))).astype(BF16)
            return carry

        lax.fori_loop(0, s // _COPY_ROWS, finish, 0)

    def col(off):
        return pl.BlockSpec((s, HEAD_DIM), lambda h: (0, off // HEAD_DIM + h))

    head = pl.BlockSpec((s, HEAD_DIM), lambda h: (0, h))
    return pl.pallas_call(
        body, name=name, grid=(cfg.ha,),
        in_specs=[pl.BlockSpec(memory_space=pltpu.SMEM), col(cfg.qa), col(cfg.ka), col(cfg.va), col(cfg.za)],
        out_specs=[head, head, head],
        out_shape=[jax.ShapeDtypeStruct((s, cfg.aw), BF16), jax.ShapeDtypeStruct((s, cfg.aw), F32),
                   jax.ShapeDtypeStruct((s, cfg.d), BF16)],
        scratch_shapes=[pltpu.VMEM((s, HEAD_DIM), F32)] * 6 + [_bias_scratch()],
        compiler_params=_params(("parallel",)),
    )(slopes, proj, proj, proj, proj)


def _attn_bwd(cfg, proj, slopes, dy, oa, lse, name):
    s = cfg.s
    scale = HEAD_DIM ** -0.5
    dst_blocks = [c0 // HEAD_DIM for c0 in (cfg.za, cfg.qa, cfg.ka, cfg.va)]
    sub0, _, win0 = _attn_geometry(cfg, DILATIONS[-1])
    assign_first = sub0 == win0

    def body(sl_ref, q_ref, k_ref, v_ref, z_ref, dy_ref, o_ref, lse_ref, dproj_ref,
             qf, kf, vf, dof, dlt, dqa, dka, dva, bias_s, stage, sems):
        head = pl.program_id(0)
        slope = sl_ref[head]

        def out_copy(slot):
            cols = pl.ds(pl.multiple_of((dst_blocks[slot] + head) * HEAD_DIM, HEAD_DIM), HEAD_DIM)
            return pltpu.make_async_copy(stage.at[slot], dproj_ref.at[:, cols], sems.at[slot])

        @pl.when(head > 0)
        def _():
            out_copy(0).wait()

        def to_f32(i, carry):
            rows = pl.ds(pl.multiple_of(i * _COPY_ROWS, _COPY_ROWS), _COPY_ROWS)
            qf[rows, :] = q_ref[rows, :].astype(F32) * scale
            kf[rows, :] = k_ref[rows, :].astype(F32)
            vf[rows, :] = v_ref[rows, :].astype(F32)
            dyv = dy_ref[rows, :].astype(F32)
            zv = z_ref[rows, :].astype(F32)
            ov = o_ref[rows, :].astype(F32)
            dov = dyv * _silu(zv)
            dof[rows, :] = dov
            dlt[rows, :] = jnp.broadcast_to(jnp.sum(dov * ov, axis=-1, keepdims=True), (_COPY_ROWS, HEAD_DIM))
            stage[0, rows, :] = (dyv * ov * _dsilu(zv)).astype(BF16)
            if not assign_first:
                zero = jnp.zeros((_COPY_ROWS, HEAD_DIM), F32)
                dqa[rows, :] = zero
                dka[rows, :] = zero
                dva[rows, :] = zero
            return carry

        lax.fori_loop(0, s // _COPY_ROWS, to_f32, 0)
        out_copy(0).start()

        _fill_bias_tables(cfg, bias_s, slope)

        def group(pattern, dil, first, blocks):
            sub, bq, win = _attn_geometry(cfg, dil)
            rep = win // HEAD_DIM
            assign = first and assign_first
            work = []
            for r, i in blocks:
                shift, qrows, krows = _attn_rows(dil, r, i, sub, bq, win)
                work.append((shift, qrows, krows, qf[qrows, :].astype(BF16), kf[krows, :].astype(BF16),
                             vf[krows, :].astype(BF16), dof[qrows, :].astype(BF16),
                             lse_ref[qrows, :], dlt[qrows, :]))
            new = []
            for shift, qrows, krows, q, kw, vw, dob, lse_b, dlt_b in work:
                sc = _attn_scores(cfg, bias_s, q, kw, pattern, shift)
                p = jnp.exp(sc - jnp.tile(lse_b, (1, rep)))
                dp = lax.dot_general(dob, vw, _DN["nt"], preferred_element_type=F32)
                ds = (p * (dp - jnp.tile(dlt_b, (1, rep)))).astype(BF16)
                new.append((qrows, krows,
                            jnp.dot(ds, kw, preferred_element_type=F32) * scale,
                            lax.dot_general(ds, q, _DN["tn"], preferred_element_type=F32),
                            lax.dot_general(p.astype(BF16), dob, _DN["tn"], preferred_element_type=F32)))
            for qrows, krows, dq_b, dk_b, dv_b in new:
                if assign:
                    dqa[qrows, :] = dq_b
                    dka[krows, :] = dk_b
                    dva[krows, :] = dv_b
                else:
                    dqa[qrows, :] += dq_b
                    dka[krows, :] += dk_b
                    dva[krows, :] += dv_b

        _for_each_group(cfg, group, size=2)

        @pl.when(head > 0)
        def _():
            for slot in (1, 2, 3):
                out_copy(slot).wait()

        def emit(i, carry):
            rows = pl.ds(pl.multiple_of(i * _COPY_ROWS, _COPY_ROWS), _COPY_ROWS)
            stage[1, rows, :] = dqa[rows, :].astype(BF16)
            stage[2, rows, :] = dka[rows, :].astype(BF16)
            stage[3, rows, :] = dva[rows, :].astype(BF16)
            return carry

        lax.fori_loop(0, s // _COPY_ROWS, emit, 0)
        for slot in (1, 2, 3):
            out_copy(slot).start()

        @pl.when(head == pl.num_programs(0) - 1)
        def _():
            for slot in range(4):
                out_copy(slot).wait()

    def col(off):
        return pl.BlockSpec((s, HEAD_DIM), lambda h: (0, off // HEAD_DIM + h))

    head_cols = pl.BlockSpec((s, HEAD_DIM), lambda h: (0, h))
    return pl.pallas_call(
        body, name=name, grid=(cfg.ha,),
        in_specs=[pl.BlockSpec(memory_space=pltpu.SMEM), col(cfg.qa), col(cfg.ka), col(cfg.va), col(cfg.za),
                  head_cols, head_cols, head_cols],
        out_specs=_ANY,
        out_shape=jax.ShapeDtypeStruct((s, cfg.f), BF16),
        scratch_shapes=[pltpu.VMEM((s, HEAD_DIM), F32)] * 8
        + [_bias_scratch(), pltpu.VMEM((4, s, HEAD_DIM), BF16), pltpu.SemaphoreType.DMA((4,))],
        compiler_params=_params(("arbitrary",)),
    )(slopes, proj, proj, proj, proj, dy, oa, lse)


def _decay_tables(lg, backward):
    c = RET_CHUNK
    a = lax.broadcasted_iota(jnp.int32, (c, c), 0)
    b = lax.broadcasted_iota(jnp.int32, (c, c), 1)
    idx = lax.broadcasted_iota(jnp.int32, (c, 1), 0).astype(F32)
    if backward:
        rel = (b - a).astype(F32)
        ex_xi = c - idx
        ex_zeta = idx
    else:
        rel = (a - b).astype(F32)
        ex_xi = idx + 1.0
        ex_zeta = c - 1.0 - idx
    relc = jnp.maximum(rel, 0.0)
    dm = jnp.where(rel >= 0, jnp.exp(relc * lg), 0.0)
    xi = jnp.exp(ex_xi * lg)
    zeta = jnp.exp(ex_zeta * lg)
    gch = jnp.exp(jnp.full((1, 1), c, F32) * lg)
    return relc, dm, xi, zeta, ex_xi, ex_zeta, gch


def _ret_fwd(cfg, proj, lgs, y, name):
    s = cfg.s
    c = RET_CHUNK
    n = s // c
    kscale = RET_QK ** -0.5

    def body(lg_ref, q_ref, k_ref, v_ref, z_ref, y_in, o_ref, y_ref, st_ref):
        h = pl.program_id(0)
        tabs = [_decay_tables(lg_ref[dirn, h], dirn == 1) for dirn in range(2)]
        st_ref[...] = jnp.zeros_like(st_ref)
        o_ref[...] = jnp.zeros_like(o_ref)

        def step(t, carry):
            for dirn in range(2):
                _, dm, xi, zeta, _, _, gch = tabs[dirn]
                i = (n - 1 - t) if dirn == 1 else t
                rows = pl.ds(pl.multiple_of(i * c, c), c)
                qi = q_ref[rows, :]
                ks = k_ref[rows, :].astype(F32) * kscale
                vi = v_ref[rows, :]
                inner = lax.dot_general(qi, ks.astype(BF16), _DN["nt"], preferred_element_type=F32) * dm
                st = st_ref[dirn]
                o_ref[rows, :] += (jnp.dot(inner.astype(BF16), vi, preferred_element_type=F32)
                                   + jnp.dot(qi, st.astype(BF16), preferred_element_type=F32) * xi)
                st_ref[dirn] = st * gch + lax.dot_general((ks * zeta).astype(BF16), vi, _DN["tn"],
                                                          preferred_element_type=F32)
            return carry

        lax.fori_loop(0, n, step, 0, unroll=2)

        def gate(i, carry):
            rows = pl.ds(pl.multiple_of(i * _COPY_ROWS, _COPY_ROWS), _COPY_ROWS)
            oh = o_ref[rows, :]
            rr = lax.rsqrt(jnp.mean(oh * oh, axis=-1, keepdims=True) + NORM_EPS)
            y_ref[rows, :] = (oh * rr * _silu(z_ref[rows, :].astype(F32))).astype(BF16)
            return carry

        lax.fori_loop(0, s // _COPY_ROWS, gate, 0)

    return pl.pallas_call(
        body, name=name, grid=(cfg.hr,),
        in_specs=[pl.BlockSpec(memory_space=pltpu.SMEM),
                  pl.BlockSpec((s, RET_QK), lambda h: (0, cfg.qr // RET_QK + h)),
                  pl.BlockSpec((s, RET_QK), lambda h: (0, cfg.kr // RET_QK + h)),
                  pl.BlockSpec((s, RET_V), lambda h: (0, cfg.vr // RET_V + h)),
                  pl.BlockSpec((s, RET_V), lambda h: (0, cfg.zr // RET_V + h)), _ANY],
        out_specs=[pl.BlockSpec((s, RET_V), lambda h: (0, h)),
                   pl.BlockSpec((s, RET_V), lambda h: (0, cfg.aw // RET_V + h))],
        out_shape=[jax.ShapeDtypeStruct((s, cfg.rw), F32), jax.ShapeDtypeStruct(y.shape, BF16)],
        input_output_aliases={5: 1},
        scratch_shapes=[pltpu.VMEM((2, RET_QK, RET_V), F32)],
        compiler_params=_params(("parallel",)),
    )(lgs, proj, proj, proj, proj, y)


def _ret_bwd(cfg, proj, lgs, dy, oret, dproj, name):
    s = cfg.s
    c = RET_CHUNK
    n = s // c
    kscale = RET_QK ** -0.5

    def accumulate(lg_ref, q_ref, k_ref, v_ref, do_ref, dq_ref, dk_ref, dv_ref, dlg_ref, states, t_ref,
                   e_dm, e_xi, e_zeta, e_g):
        h = pl.program_id(0)
        tabs = [_decay_tables(lg_ref[dirn, h], dirn == 1) for dirn in range(2)]
        dlg_ref[...] = jnp.zeros_like(dlg_ref)
        dq_ref[...] = jnp.zeros_like(dq_ref)
        dk_ref[...] = jnp.zeros_like(dk_ref)
        dv_ref[...] = jnp.zeros_like(dv_ref)

        def chunk_rows(dirn, t):
            i = (n - 1 - t) if dirn == 1 else t
            return pl.ds(pl.multiple_of(i * c, c), c)

        t_ref[...] = jnp.zeros_like(t_ref)

        def fwd_step(t, carry):
            for dirn in range(2):
                _, _, _, zeta, _, _, gch = tabs[dirn]
                rows = chunk_rows(dirn, t)
                st = t_ref[dirn]
                states[dirn, t] = st
                ks = k_ref[rows, :].astype(F32) * kscale
                t_ref[dirn] = st * gch + lax.dot_general((ks * zeta).astype(BF16), v_ref[rows, :], _DN["tn"],
                                                         preferred_element_type=F32)
            return carry

        lax.fori_loop(0, n, fwd_step, 0, unroll=2)
        t_ref[...] = jnp.zeros_like(t_ref)

        for ref in (e_dm, e_xi, e_zeta, e_g):
            ref[...] = jnp.zeros_like(ref)

        def bwd_step(u, carry):
            t = n - 1 - u
            for dirn in range(2):
                relc, dm, xi, zeta, ex_xi, ex_zeta, gch = tabs[dirn]
                rows = chunk_rows(dirn, t)
                qi = q_ref[rows, :]
                ks = k_ref[rows, :].astype(F32) * kscale
                ksb = ks.astype(BF16)
                vi = v_ref[rows, :]
                doi = do_ref[rows, :]
                sn_f = states[dirn, t]
                sn = sn_f.astype(BF16)
                tt = t_ref[dirn]
                ttb = tt.astype(BF16)
                a_mat = lax.dot_general(qi, ksb, _DN["nt"], preferred_element_type=F32) * dm
                dov = lax.dot_general(doi, vi, _DN["nt"], preferred_element_type=F32)
                b_mat = (dov * dm).astype(BF16)
                kz = (ks * zeta).astype(BF16)
                d_v = (jnp.dot(kz, ttb, preferred_element_type=F32)
                       + lax.dot_general(a_mat.astype(BF16), doi, _DN["tn"], preferred_element_type=F32))
                dk_inter = lax.dot_general(vi, ttb, _DN["nt"], preferred_element_type=F32) * zeta
                d_k = lax.dot_general(b_mat, qi, _DN["tn"], preferred_element_type=F32) + dk_inter
                o_inter = jnp.dot(qi, sn, preferred_element_type=F32) * xi
                d_q = (jnp.dot(b_mat, ksb, preferred_element_type=F32)
                       + lax.dot_general(doi, sn, _DN["nt"], preferred_element_type=F32) * xi)
                e_dm[dirn] += relc * a_mat * dov
                e_xi[dirn] += ex_xi * (doi.astype(F32) * o_inter)
                e_zeta[dirn] += ex_zeta * (ks * dk_inter)
                e_g[dirn] += tt * sn_f
                t_ref[dirn] = tt * gch + lax.dot_general((qi.astype(F32) * xi).astype(BF16), doi, _DN["tn"],
                                                         preferred_element_type=F32)
                dq_ref[rows, :] += d_q
                dk_ref[rows, :] += d_k * kscale
                dv_ref[rows, :] += d_v
            return carry

        lax.fori_loop(0, n, bwd_step, 0, unroll=2)
        for dirn in range(2):
            total = (jnp.sum(e_dm[dirn], keepdims=True) + jnp.sum(e_xi[dirn], keepdims=True)
                     + jnp.sum(e_zeta[dirn], keepdims=True) + (c * tabs[dirn][6]) * jnp.sum(e_g[dirn], keepdims=True))
            dlg_ref[0, dirn:dirn + 1, :] = jnp.broadcast_to(total, (1, LANES))

    def body(lg_ref, q_ref, k_ref, v_ref, z_ref, dy_ref, or_ref, dproj_in, dproj_ref, dlg_ref,
             do_s, dq_s, dk_s, dv_s, stage_z, stage_q, stage_k, stage_v, sems, *scratch):
        head = pl.program_id(0)
        stages = (stage_z, stage_q, stage_k, stage_v)
        firsts = (cfg.zr, cfg.qr, cfg.kr, cfg.vr)

        def out_copy(slot):
            width = stages[slot].shape[1]
            cols = pl.ds(pl.multiple_of(firsts[slot] + head * width, LANES), width)
            return pltpu.make_async_copy(stages[slot], dproj_ref.at[:, cols], sems.at[slot])

        @pl.when(head > 0)
        def _():
            out_copy(0).wait()

        def gate_norm_bwd(i, carry):
            rows = pl.ds(pl.multiple_of(i * _COPY_ROWS, _COPY_ROWS), _COPY_ROWS)
            oh = or_ref[rows, :]
            zr = z_ref[rows, :].astype(F32)
            dyr = dy_ref[rows, :].astype(F32)
            rr = lax.rsqrt(jnp.mean(oh * oh, axis=-1, keepdims=True) + NORM_EPS)
            yn = oh * rr
            dyn = dyr * _silu(zr)
            stage_z[rows, :] = (dyr * yn * _dsilu(zr)).astype(BF16)
            do_s[rows, :] = (rr * (dyn - yn * jnp.mean(dyn * yn, axis=-1, keepdims=True))).astype(BF16)
            return carry

        lax.fori_loop(0, s // _COPY_ROWS, gate_norm_bwd, 0)
        out_copy(0).start()

        accumulate(lg_ref, q_ref, k_ref, v_ref, do_s, dq_s, dk_s, dv_s, dlg_ref, *scratch)

        @pl.when(head > 0)
        def _():
            for slot in (1, 2, 3):
                out_copy(slot).wait()

        def emit(i, carry):
            rows = pl.ds(pl.multiple_of(i * _COPY_ROWS, _COPY_ROWS), _COPY_ROWS)
            stage_q[rows, :] = dq_s[rows, :].astype(BF16)
            stage_k[rows, :] = dk_s[rows, :].astype(BF16)
            stage_v[rows, :] = dv_s[rows, :].astype(BF16)
            return carry

        lax.fori_loop(0, s // _COPY_ROWS, emit, 0)
        for slot in (1, 2, 3):
            out_copy(slot).start()

        @pl.when(head == pl.num_programs(0) - 1)
        def _():
            for slot in range(4):
                out_copy(slot).wait()

    return pl.pallas_call(
        body, name=name, grid=(cfg.hr,),
        in_specs=[pl.BlockSpec(memory_space=pltpu.SMEM),
                  pl.BlockSpec((s, RET_QK), lambda h: (0, cfg.qr // RET_QK + h)),
                  pl.BlockSpec((s, RET_QK), lambda h: (0, cfg.kr // RET_QK + h)),
                  pl.BlockSpec((s, RET_V), lambda h: (0, cfg.vr // RET_V + h)),
                  pl.BlockSpec((s, RET_V), lambda h: (0, cfg.zr // RET_V + h)),
                  pl.BlockSpec((s, RET_V), lambda h: (0, cfg.aw // RET_V + h)),
                  pl.BlockSpec((s, RET_V), lambda h: (0, h)), _ANY],
        out_specs=[_ANY, pl.BlockSpec((1, 8, LANES), lambda h: (h, 0, 0))],
        out_shape=[jax.ShapeDtypeStruct(dproj.shape, BF16), jax.ShapeDtypeStruct((cfg.hr, 8, LANES), F32)],
        input_output_aliases={7: 0},
        scratch_shapes=[pltpu.VMEM((s, RET_V), BF16),
                        pltpu.VMEM((s, RET_QK), F32), pltpu.VMEM((s, RET_QK), F32), pltpu.VMEM((s, RET_V), F32),
                        pltpu.VMEM((s, RET_V), BF16), pltpu.VMEM((s, RET_QK), BF16), pltpu.VMEM((s, RET_QK), BF16),
                        pltpu.VMEM((s, RET_V), BF16), pltpu.SemaphoreType.DMA((4,)),
                        pltpu.VMEM((2, n, RET_QK, RET_V), F32), pltpu.VMEM((2, RET_QK, RET_V), F32),
                        pltpu.VMEM((2, c, c), F32), pltpu.VMEM((2, c, RET_V), F32),
                        pltpu.VMEM((2, c, RET_QK), F32), pltpu.VMEM((2, RET_QK, RET_V), F32)],
        compiler_params=_params(("arbitrary",)),
    )(lgs, proj, proj, proj, proj, dy, oret, dproj)


def _ada_fwd(cact16, w_ada, name):
    depth, d, n = w_ada.shape
    tn = _tile(n, 768)

    def body(c_ref, w_ref, o_ref):
        o_ref[0] = jnp.dot(c_ref[...], w_ref[0].astype(BF16), preferred_element_type=F32)

    return pl.pallas_call(
        body, name=name, grid=(depth, n // tn),
        in_specs=[pl.BlockSpec((16, d), lambda l, j: (0, 0)), pl.BlockSpec((1, d, tn), lambda l, j: (l, 0, j))],
        out_specs=pl.BlockSpec((1, 16, tn), lambda l, j: (l, 0, j)),
        out_shape=jax.ShapeDtypeStruct((depth, 16, n), F32),
        compiler_params=_params(("parallel", "parallel")),
    )(cact16, w_ada)


def _adam_math(w, g, m, v):
    m2 = ADAM_B1 * m + (1.0 - ADAM_B1) * g
    v2 = ADAM_B2 * v + (1.0 - ADAM_B2) * (g * g)
    m_hat = m2 / (1.0 - ADAM_B1 ** ADAM_STEP)
    v_hat = v2 / (1.0 - ADAM_B2 ** ADAM_STEP)
    delta = -ADAM_LR * (m_hat / (jnp.sqrt(v_hat) + ADAM_EPS) + ADAM_WD * w)
    return delta, m2, v2


def _adamw_big(w, g, m, v, first, count, prev, name):
    _, r, c = w.shape
    tr, tc = _tile(r, 512), _tile(c, 1024)

    def body(w_ref, g_ref, m_ref, v_ref, *rest):
        go_ref, d_ref, mo_ref, vo_ref = rest[-4:]
        gv = g_ref[...]
        delta, m2, v2 = _adam_math(w_ref[...], gv, m_ref[...], v_ref[...])
        go_ref[...] = gv
        d_ref[...] = delta
        mo_ref[...] = m2
        vo_ref[...] = v2

    spec = pl.BlockSpec((1, tr, tc), lambda l, i, j: (first + l, i, j))
    shp = jax.ShapeDtypeStruct(w.shape, F32)
    carried = [] if prev is None else list(prev)
    return pl.pallas_call(
        body, name=name, grid=(count, r // tr, c // tc),
        in_specs=[spec] * 4 + [_ANY] * len(carried), out_specs=[spec] * 4, out_shape=[shp] * 4,
        input_output_aliases={4 + k: k for k in range(len(carried))},
        compiler_params=_params(("parallel", "parallel", "parallel")),
    )(w, g, m, v, *carried)


def _adamw_ada(w, m, v, cact128, dmod128, name):
    depth, r, c = w.shape
    tr, tc = _tile(r, 512), _tile(c, 768)

    def body(w_ref, m_ref, v_ref, c_ref, dm_ref, go_ref, d_ref, mo_ref, vo_ref):
        gv = lax.dot_general(c_ref[...], dm_ref[0], _DN["tn"], preferred_element_type=F32)
        delta, m2, v2 = _adam_math(w_ref[0], gv, m_ref[0], v_ref[0])
        go_ref[0] = gv
        d_ref[0] = delta
        mo_ref[0] = m2
        vo_ref[0] = v2

    spec = pl.BlockSpec((1, tr, tc), lambda l, i, j: (l, i, j))
    shp = jax.ShapeDtypeStruct(w.shape, F32)
    return pl.pallas_call(
        body, name=name, grid=(depth, r // tr, c // tc),
        in_specs=[spec] * 3 + [pl.BlockSpec((128, tr), lambda l, i, j: (0, i)),
                               pl.BlockSpec((1, 128, tc), lambda l, i, j: (l, 0, j))],
        out_specs=[spec] * 4, out_shape=[shp] * 4,
        compiler_params=_params(("parallel", "parallel", "parallel")),
    )(w, m, v, cact128, dmod128)


def _adamw_small(w, g, m, v, name):
    def body(w_ref, g_ref, m_ref, v_ref, d_ref, mo_ref, vo_ref):
        delta, m2, v2 = _adam_math(w_ref[...], g_ref[...], m_ref[...], v_ref[...])
        d_ref[...] = delta
        mo_ref[...] = m2
        vo_ref[...] = v2

    shp = jax.ShapeDtypeStruct(w.shape, F32)
    return pl.pallas_call(body, name=name, out_shape=[shp] * 3)(w, g, m, v)


def _sum_gathered(parts, name):
    nd, r, c = parts.shape

    def body(p_ref, o_ref):
        acc = p_ref[0]
        for e in range(1, nd):
            acc = acc + p_ref[e]
        o_ref[...] = acc

    return pl.pallas_call(body, name=name, out_shape=jax.ShapeDtypeStruct((r, c), F32))(parts)


def _flip(v, bit):
    return 1 - v if bit else v


def _all_gather_small(x, name):
    r, c = x.shape

    def body(x_ref, out_ref, send_sems, recv_sems, local_sem):
        mx, my, mc = lax.axis_index("x"), lax.axis_index("y"), lax.axis_index("c")
        me = 4 * mx + 2 * my + mc
        mine = pltpu.make_async_copy(x_ref, out_ref.at[me], local_sem)
        mine.start()
        sends = []
        for k in range(1, N_DEV):
            peer = (_flip(mx, k & 4), _flip(my, k & 2), _flip(mc, k & 1))
            cp = pltpu.make_async_remote_copy(src_ref=x_ref, dst_ref=out_ref.at[me], send_sem=send_sems.at[k - 1],
                                              recv_sem=recv_sems.at[k - 1], device_id=peer, device_id_type=MESH)
            cp.start()
            sends.append(cp)
        for k in range(1, N_DEV):
            peer = (_flip(mx, k & 4), _flip(my, k & 2), _flip(mc, k & 1))
            src = 4 * peer[0] + 2 * peer[1] + peer[2]
            pltpu.make_async_remote_copy(src_ref=x_ref, dst_ref=out_ref.at[src], send_sem=send_sems.at[k - 1],
                                         recv_sem=recv_sems.at[k - 1], device_id=peer,
                                         device_id_type=MESH).wait_recv()
        for cp in sends:
            cp.wait_send()
        mine.wait()

    return pl.pallas_call(
        body, name=name,
        out_shape=jax.ShapeDtypeStruct((N_DEV, r, c), x.dtype),
        in_specs=[pl.BlockSpec(memory_space=pltpu.VMEM)],
        out_specs=pl.BlockSpec(memory_space=pltpu.VMEM),
        scratch_shapes=[pltpu.SemaphoreType.DMA((N_DEV - 1,)), pltpu.SemaphoreType.DMA((N_DEV - 1,)),
                        pltpu.SemaphoreType.DMA],
        compiler_params=pltpu.CompilerParams(vmem_limit_bytes=VMEM_LIMIT),
    )(x)


_HBM = pl.BlockSpec(memory_space=pltpu.HBM)
_SEM = pl.BlockSpec(memory_space=pltpu.SEMAPHORE)
_ANY = pl.BlockSpec(memory_space=pl.ANY)
_EFFECT = pltpu.SideEffectType.DATAFLOW_SIDE_EFFECTING


def _in_hbm(a):
    return pltpu.with_memory_space_constraint(a, pltpu.HBM)


def _place_w_in(w, layer, chip1, name):
    _, d, fc = w.shape
    tr = _tile(d, 512)

    def body(c_ref, w_ref, o_ref):
        o_ref[...] = w_ref[...].astype(BF16)

    return pl.pallas_call(
        body, name=name,
        grid_spec=pltpu.PrefetchScalarGridSpec(
            num_scalar_prefetch=1, grid=(d // tr,),
            in_specs=[pl.BlockSpec((None, tr, fc), lambda i, c: (layer, i, 0))],
            out_specs=pl.BlockSpec((tr, fc), lambda i, c: (i, c[0]))),
        out_shape=jax.ShapeDtypeStruct((d, N_CHIP * fc), BF16),
        compiler_params=_params(("parallel",)),
    )(chip1, w)


def _place_w_out(w, layer, chip1, name):
    _, rc, dd = w.shape
    tc = _tile(dd, 1024)

    def body(c_ref, w_ref, o_ref):
        o_ref[...] = w_ref[...].astype(BF16)

    return pl.pallas_call(
        body, name=name,
        grid_spec=pltpu.PrefetchScalarGridSpec(
            num_scalar_prefetch=1, grid=(dd // tc,),
            in_specs=[pl.BlockSpec((None, rc, tc), lambda j, c: (layer, 0, j))],
            out_specs=pl.BlockSpec((rc, tc), lambda j, c: (c[0], j))),
        out_shape=jax.ShapeDtypeStruct((N_CHIP * rc, dd), BF16),
        compiler_params=_params(("parallel",)),
    )(chip1, w)


def _weight_region(ref, tensor, chip):
    if tensor == 0:
        fc = ref.shape[1] // N_CHIP
        return ref.at[:, pl.ds(pl.multiple_of(chip * fc, LANES), fc)]
    rc = ref.shape[0] // N_CHIP
    return ref.at[pl.ds(pl.multiple_of(chip * rc, 8), rc), :]


def _gather_copies(ref, tensor, send_sems, recv_sems, landing):
    mx, my, mc = lax.axis_index("x"), lax.axis_index("y"), lax.axis_index("c")
    mine = _weight_region(ref, tensor, 2 * mx + my)
    copies = []
    for j in range(1, N_CHIP):
        peer = (_flip(mx, j & 2), _flip(my, j & 1), mc)
        dst = _weight_region(ref, tensor, 2 * peer[0] + peer[1]) if landing else mine
        idx = 2 * (j - 1) + tensor
        copies.append(pltpu.make_async_remote_copy(src_ref=mine, dst_ref=dst, send_sem=send_sems.at[idx],
                                                   recv_sem=recv_sems.at[idx], device_id=peer, device_id_type=MESH))
    return copies


def _gather_start(fi, fo, dep, name):
    ns = 2 * (N_CHIP - 1)

    def body(fi_ref, fo_ref, dep_ref, send_sems, recv_sems, fi_thru, fo_thru, token):
        for tensor, ref in enumerate((fi_ref, fo_ref)):
            for cp in _gather_copies(ref, tensor, send_sems, recv_sems, landing=False):
                cp.start()
        token[...] = jnp.zeros_like(token)

    return pl.pallas_call(
        body, name=name,
        out_shape=(pltpu.SemaphoreType.DMA((ns,)), pltpu.SemaphoreType.DMA((ns,)),
                   pltpu.HBM(fi.shape, fi.dtype), pltpu.HBM(fo.shape, fo.dtype),
                   jax.ShapeDtypeStruct((8, LANES), F32)),
        in_specs=(_HBM, _HBM, _ANY),
        out_specs=(_SEM, _SEM, _HBM, _HBM, pl.BlockSpec(memory_space=pltpu.VMEM)),
        input_output_aliases={0: 2, 1: 3},
        compiler_params=pltpu.CompilerParams(has_side_effects=_EFFECT),
    )(_in_hbm(fi), _in_hbm(fo), dep)


def _gather_wait(send_sems, recv_sems, buf, tensor, after, name):
    def body(buf_ref, send_sems, recv_sems, after_ref, buf_out):
        for cp in _gather_copies(buf_ref, tensor, send_sems, recv_sems, landing=True):
            cp.wait_send()
            cp.wait_recv()

    return pl.pallas_call(
        body, name=name,
        out_shape=pltpu.HBM(buf.shape, buf.dtype),
        in_specs=(_HBM, _SEM, _SEM, _ANY), out_specs=_HBM,
        input_output_aliases={0: 0},
        compiler_params=pltpu.CompilerParams(has_side_effects=_EFFECT),
    )(buf, send_sems, recv_sems, after)


def _w_in_half(ref, half, chip):
    hr = ref.shape[0] // 2
    fc = ref.shape[1] // N_CHIP
    return ref.at[pl.ds(pl.multiple_of(half * hr, 8), hr), pl.ds(pl.multiple_of(chip * fc, LANES), fc)]


def _half_copies_ici(ref, send_sems, recv_sems, landing):
    mx, my, mc = lax.axis_index("x"), lax.axis_index("y"), lax.axis_index("c")
    mine = _w_in_half(ref, mc, 2 * mx + my)
    copies = []
    for j in range(1, N_CHIP):
        peer = (_flip(mx, j & 2), _flip(my, j & 1), mc)
        dst = _w_in_half(ref, mc, 2 * peer[0] + peer[1]) if landing else mine
        copies.append(pltpu.make_async_remote_copy(src_ref=mine, dst_ref=dst, send_sem=send_sems.at[j - 1],
                                                   recv_sem=recv_sems.at[j - 1], device_id=peer, device_id_type=MESH))
    return copies


def _half_copies_d2d(ref, send_sems, recv_sems, landing):
    mx, my, mc = lax.axis_index("x"), lax.axis_index("y"), lax.axis_index("c")
    sib = (mx, my, 1 - mc)
    copies = []
    for j in range(1, N_CHIP):
        other = 2 * _flip(mx, j & 2) + _flip(my, j & 1)
        src = _w_in_half(ref, mc, other)
        dst = _w_in_half(ref, 1 - mc, other) if landing else src
        copies.append(pltpu.make_async_remote_copy(src_ref=src, dst_ref=dst, send_sem=send_sems.at[j - 1],
                                                   recv_sem=recv_sems.at[j - 1], device_id=sib, device_id_type=MESH))
    return copies


def _w_out_copies(ref, send_sems, recv_sems, landing):
    mx, my, mc = lax.axis_index("x"), lax.axis_index("y"), lax.axis_index("c")
    mine = _weight_region(ref, 1, 2 * mx + my)
    copies = []
    for j in range(1, N_CHIP):
        peer = (_flip(mx, j & 2), _flip(my, j & 1), mc)
        dst = _weight_region(ref, 1, 2 * peer[0] + peer[1]) if landing else mine
        copies.append(pltpu.make_async_remote_copy(src_ref=mine, dst_ref=dst, send_sem=send_sems.at[j - 1],
                                                   recv_sem=recv_sems.at[j - 1], device_id=peer, device_id_type=MESH))
    return copies


def _copies_start(buf, copies, dep, name):
    ns = N_CHIP - 1

    def body(buf_ref, dep_ref, send_sems, recv_sems, buf_thru, token):
        for cp in copies(buf_ref, send_sems, recv_sems, landing=False):
            cp.start()
        token[...] = jnp.zeros_like(token)

    return pl.pallas_call(
        body, name=name,
        out_shape=(pltpu.SemaphoreType.DMA((ns,)), pltpu.SemaphoreType.DMA((ns,)),
                   pltpu.HBM(buf.shape, buf.dtype), jax.ShapeDtypeStruct((8, LANES), F32)),
        in_specs=(_HBM, _ANY),
        out_specs=(_SEM, _SEM, _HBM, pl.BlockSpec(memory_space=pltpu.VMEM)),
        input_output_aliases={0: 2},
        compiler_params=pltpu.CompilerParams(has_side_effects=_EFFECT),
    )(_in_hbm(buf), dep)


def _copies_wait(send_sems, recv_sems, buf, copies, after, name):
    def body(buf_ref, send_sems, recv_sems, after_ref, buf_out):
        for cp in copies(buf_ref, send_sems, recv_sems, landing=True):
            cp.wait_send()
            cp.wait_recv()

    return pl.pallas_call(
        body, name=name,
        out_shape=pltpu.HBM(buf.shape, buf.dtype),
        in_specs=(_HBM, _SEM, _SEM, _ANY), out_specs=_HBM,
        input_output_aliases={0: 0},
        compiler_params=pltpu.CompilerParams(has_side_effects=_EFFECT),
    )(buf, send_sems, recv_sems, after)


def _scatter_copies(gi_ref, go_ref, pi_ref, po_ref, send_sems, recv_sems):
    mx, my, mc = lax.axis_index("x"), lax.axis_index("y"), lax.axis_index("c")
    hr, fc = pi_ref.shape[1:]
    ro = po_ref.shape[1]
    copies = []
    for k in range(1, N_DEV):
        peer = (_flip(mx, k & 4), _flip(my, k & 2), _flip(mc, k & 1))
        pchip = 2 * peer[0] + peer[1]
        src = (gi_ref.at[pl.ds(pl.multiple_of(peer[2] * hr, 8), hr), pl.ds(pl.multiple_of(pchip * fc, LANES), fc)],
               go_ref.at[pl.ds(pl.multiple_of((2 * pchip + peer[2]) * ro, 8), ro), :])
        dst = (pi_ref.at[k - 1], po_ref.at[k - 1])
        for t in range(2):
            idx = 2 * (k - 1) + t
            copies.append(pltpu.make_async_remote_copy(src_ref=src[t], dst_ref=dst[t], send_sem=send_sems.at[idx],
                                                       recv_sem=recv_sems.at[idx], device_id=peer,
                                                       device_id_type=MESH))
    return copies


def _scatter_start(gi, go, name):
    d, f = gi.shape
    dd = go.shape[1]
    ns = 2 * (N_DEV - 1)
    pi = lax.empty((N_DEV - 1, d // 2, f // N_CHIP), BF16)
    po = lax.empty((N_DEV - 1, d // N_DEV, dd), BF16)

    def body(gi_ref, go_ref, pi_ref, po_ref, send_sems, recv_sems, gi_thru, go_thru, pi_thru, po_thru, token):
        for cp in _scatter_copies(gi_ref, go_ref, pi_ref, po_ref, send_sems, recv_sems):
            cp.start()
        token[...] = jnp.zeros_like(token)

    return pl.pallas_call(
        body, name=name,
        out_shape=(pltpu.SemaphoreType.DMA((ns,)), pltpu.SemaphoreType.DMA((ns,)),
                   pltpu.HBM(gi.shape, gi.dtype), pltpu.HBM(go.shape, go.dtype),
                   pltpu.HBM(pi.shape, pi.dtype), pltpu.HBM(po.shape, po.dtype),
                   jax.ShapeDtypeStruct((8, LANES), F32)),
        in_specs=(_HBM, _HBM, _HBM, _HBM),
        out_specs=(_SEM, _SEM, _HBM, _HBM, _HBM, _HBM, pl.BlockSpec(memory_space=pltpu.VMEM)),
        input_output_aliases={0: 2, 1: 3, 2: 4, 3: 5},
        compiler_params=pltpu.CompilerParams(has_side_effects=_EFFECT),
    )(_in_hbm(gi), _in_hbm(go), _in_hbm(pi), _in_hbm(po))


def _scatter_wait(send_sems, recv_sems, gi, go, pi, po, after, name):
    def body(gi_ref, go_ref, pi_ref, po_ref, send_sems, recv_sems, *rest):
        for cp in _scatter_copies(gi_ref, go_ref, pi_ref, po_ref, send_sems, recv_sems):
            cp.wait_send()
            cp.wait_recv()

    return pl.pallas_call(
        body, name=name,
        out_shape=tuple(pltpu.HBM(a.shape, a.dtype) for a in (gi, go, pi, po)),
        in_specs=(_HBM, _HBM, _HBM, _HBM, _SEM, _SEM) + (_ANY,) * len(after), out_specs=(_HBM, _HBM, _HBM, _HBM),
        input_output_aliases={0: 0, 1: 1, 2: 2, 3: 3},
        compiler_params=pltpu.CompilerParams(has_side_effects=_EFFECT),
    )(gi, go, pi, po, send_sems, recv_sems, *after)


def _sum_into(buf, g, parts, where2, layer, row_blocks, dep, name):
    depth, r2, c = buf.shape
    r = r2 // 2
    tr, tc = _tile(r, 256), _tile(c, 1024)
    nr, nc = r // tr, c // tc
    col_blocks = (g.shape[1] // c) > 1

    def body(w_ref, buf_ref, g_ref, p_ref, dep_ref, o_ref):
        acc = g_ref[...].astype(F32)
        for e in range(N_DEV - 1):
            acc = acc + p_ref[e].astype(F32)
        o_ref[...] = acc

    return pl.pallas_call(
        body, name=name,
        grid_spec=pltpu.PrefetchScalarGridSpec(
            num_scalar_prefetch=1, grid=(nr, nc),
            in_specs=[_ANY,
                      pl.BlockSpec((tr, tc), lambda i, j, w: (row_blocks(w, nr) + i, (w[1] * nc if col_blocks else 0) + j)),
                      pl.BlockSpec((N_DEV - 1, tr, tc), lambda i, j, w: (0, i, j)), _ANY],
            out_specs=pl.BlockSpec((None, tr, tc), lambda i, j, w: (layer, w[0] * nr + i, j))),
        out_shape=jax.ShapeDtypeStruct(buf.shape, F32),
        input_output_aliases={1: 0},
        compiler_params=_params(("parallel", "parallel")),
    )(where2, buf, g, parts, dep)


def _exchange_halves(b_in, b_out, first, count, name):
    r_in = b_in.shape[1]
    r_out = b_out.shape[1]

    def body(bi_ref, bo_ref, oi_ref, oo_ref, send_sems, recv_sems):
        mx, my, mc = lax.axis_index("x"), lax.axis_index("y"), lax.axis_index("c")
        sib = (mx, my, 1 - mc)

        def half(ref, l, rows, which):
            return ref.at[l, pl.ds(pl.multiple_of(which * (rows // 2), 8), rows // 2), :]

        copies = []
        for l in range(first, first + count):
            for t, (src, dst, rows) in enumerate(((bi_ref, oi_ref, r_in), (bo_ref, oo_ref, r_out))):
                idx = 2 * (l - first) + t
                kw = dict(send_sem=send_sems.at[idx], recv_sem=recv_sems.at[idx], device_id=sib, device_id_type=MESH)
                cp = pltpu.make_async_remote_copy(src_ref=half(src, l, rows, mc), dst_ref=half(dst, l, rows, mc), **kw)
                cp.start()
                copies.append((cp, pltpu.make_async_remote_copy(src_ref=half(src, l, rows, mc),
                                                                dst_ref=half(dst, l, rows, 1 - mc), **kw)))
        for cp, landed in copies:
            landed.wait_recv()
        for cp, landed in copies:
            cp.wait_send()

    ns = 2 * count
    return pl.pallas_call(
        body, name=name,
        out_shape=[jax.ShapeDtypeStruct(b_in.shape, F32), jax.ShapeDtypeStruct(b_out.shape, F32)],
        in_specs=[_ANY, _ANY], out_specs=[_ANY, _ANY],
        input_output_aliases={0: 0, 1: 1},
        scratch_shapes=[pltpu.SemaphoreType.DMA((ns,)), pltpu.SemaphoreType.DMA((ns,))],
    )(b_in, b_out)


def _rows8(v):
    return jnp.pad(v, ((0, 8 - v.shape[0]), (0, 0)))


def kernel(x, c, norm_gain, w_ada, b_ada, w_in, w_out, ret_decay_logit_f, ret_decay_logit_b, final_gain, loss_target, m_norm_gain, m_w_ada, m_b_ada, m_w_in, m_w_out, m_ret_decay_logit_f, m_ret_decay_logit_b, m_final_gain, v_norm_gain, v_w_ada, v_b_ada, v_w_in, v_w_out, v_ret_decay_logit_f, v_ret_decay_logit_b, v_final_gain):
    cfg = _Cfg()
    depth, d = norm_gain.shape
    mx, my, mc = lax.axis_index("x"), lax.axis_index("y"), lax.axis_index("c")
    me = 4 * mx + 2 * my + mc
    chip = 2 * mx + my
    x0 = x[0]
    tgt = loss_target[0]
    ada_cols = w_ada.shape[2]

    chip1 = jnp.reshape(chip, (1,)).astype(jnp.int32)
    where2 = jnp.stack([mc, chip]).astype(jnp.int32)

    def start_gather(l, dep):
        return _gather_start(_place_w_in(w_in, l, chip1, f"place_w_in_{l}"),
                             _place_w_out(w_out, l, chip1, f"place_w_out_{l}"), dep, f"gather_start_{l}")

    c_all = _all_gather_small(_rows8(c), "gather_c")[:, 0, :]
    cact = _silu(c_all)
    ici = _copies_start(_place_w_in(w_in, 0, chip1, "place_w_in_0"), _half_copies_ici, c_all, "gather0_ici_start")
    cact16 = jnp.pad(cact, ((0, 8), (0, 0))) + ici[3][0:1, 0:1]
    mod_part = _ada_fwd(cact16.astype(BF16), w_ada, "ada_fwd")
    mod_all = _all_gather_small(mod_part.reshape(depth * 16, ada_cols), "gather_mod")
    mod_all = mod_all.reshape(N_CHIP, 2, depth, 16, ada_cols)[:, 0]
    mod_mine = lax.dynamic_index_in_dim(mod_all, me, axis=2, keepdims=False)
    mod = jnp.transpose(mod_mine, (1, 0, 2)).reshape(depth, 3, d)
    bias = b_ada.reshape(depth, 3, d)
    fi0 = _copies_wait(ici[0], ici[1], ici[2], _half_copies_ici, mod, "gather0_ici_wait")
    d2d = _copies_start(fi0, _half_copies_d2d, mod, "gather0_d2d_start")
    out0 = _copies_start(_place_w_out(w_out, 0, chip1, "place_w_out_0"), _w_out_copies, d2d[3], "gather0_out_start")
    w_in_0 = _copies_wait(d2d[0], d2d[1], d2d[2], _half_copies_d2d, out0[3], "gather0_d2d_wait")

    slopes = jnp.exp2(-8.0 * (jnp.arange(cfg.ha, dtype=F32) + 1.0) / cfg.ha)
    lg_f = jax.nn.log_sigmoid(ret_decay_logit_f)
    lg_b = jax.nn.log_sigmoid(ret_decay_logit_b)

    saved = []
    w_full = []
    h_x = x0
    pending = None
    for l in range(depth):
        if l == 0:
            w_in_l = w_in_0
        else:
            send_sems, recv_sems, fi, fo, _ = pending
            w_in_l = _gather_wait(send_sems, recv_sems, fi, 0, h_x, f"gather_wait_in_{l}")
        g8 = _rows8(norm_gain[l:l + 1])
        if l + 1 < depth:
            pending = start_gather(l + 1, w_in_l)
            g8 = g8 + pending[4][0:1, 0:1]
        mod3, b3 = _rows8(mod[l]), _rows8(bias[l])
        hb = _norm_mod_fwd(h_x, g8, mod3, b3, f"norm_mod_fwd_{l}")
        proj = _matmul(hb, w_in_l, "nn", BF16, f"in_proj_{l}")
        oa, lse, y = _attn_fwd(cfg, proj, slopes, f"attn_fwd_{l}")
        lgs = jnp.stack([lg_f[l], lg_b[l]])
        oret, y = _ret_fwd(cfg, proj, lgs, y, f"ret_fwd_{l}")
        if l == 0:
            w_out_l = _copies_wait(out0[0], out0[1], out0[2], _w_out_copies, y, "gather0_out_wait")
        else:
            w_out_l = _gather_wait(send_sems, recv_sems, fo, 1, y, f"gather_wait_out_{l}")
        w_full.append((w_in_l, w_out_l))
        x_next, out = _out_proj_fwd(y, w_out_l, h_x, mod3, b3, f"out_proj_{l}")
        saved.append((h_x, hb, proj, oret, y, oa, lse, out, g8, mod3, b3, lgs))
        h_x = x_next

    dx, loss8, fin_acc = _final_loss(h_x, tgt, _rows8(final_gain[None]), "final_loss")

    landed = [None] * depth
    d_mod, d_gain, d_lg = [None] * depth, [None] * depth, [None] * depth
    in_flight = None
    for l in reversed(range(depth)):
        x_l, hb, proj, oret, y, oa, lse, out, g8, mod3, b3, lgs = saved[l]
        w_in_l, w_out_l = w_full[l]
        douts, gate_acc = _out_proj_bwd_prep(dx, out, mod3, b3, f"out_proj_bwd_prep_{l}")
        dy = _matmul(douts, w_out_l, "nt", BF16, f"out_proj_dy_{l}")
        g_out_l = _matmul(y, douts, "tn", BF16, f"out_proj_dw_{l}", tk=4096)
        dproj = _attn_bwd(cfg, proj, slopes, dy, oa, lse, f"attn_bwd_{l}")
        dproj, dlg = _ret_bwd(cfg, proj, lgs, dy, oret, dproj, f"ret_bwd_{l}")
        g_in_l = _matmul(hb, dproj, "tn", BF16, f"in_proj_dw_{l}", tk=4096)
        started = _scatter_start(g_in_l, g_out_l, f"scatter_start_{l}")
        dh = _matmul(dproj, w_in_l, "nt", F32, f"in_proj_dh_{l}", tk=3584)
        g8 = g8 + started[-1][0:1, 0:1]
        dx, nm_acc = _norm_mod_bwd(dh, x_l, dx, g8, mod3, b3, f"norm_mod_bwd_{l}")
        d_mod[l] = jnp.concatenate([nm_acc[0], nm_acc[1], gate_acc[0]])
        d_gain[l] = nm_acc[2]
        d_lg[l] = dlg[:, 0:2, 0]
        if in_flight is not None:
            landed[l + 1] = _scatter_wait(*in_flight[:-1], (dx,), f"scatter_wait_{l + 1}")
        in_flight = started

    gw_in = lax.empty(w_in.shape, F32)
    gw_out = lax.empty(w_out.shape, F32)
    res_in = res_out = None
    for first, count in ((1, depth - 1), (0, 1)):
        if first == 0:
            after = (dx,) if res_out is None else (res_in[1], res_out[1])
            landed[0] = _scatter_wait(*in_flight[:-1], after, "scatter_wait_0")
        if count == 0:
            continue
        dep = in_flight[-1]
        for l in range(first, first + count):
            gi, go, pi, po = landed[l]
            gw_in = _sum_into(gw_in, gi, pi, where2, l, lambda w, nr: w[0] * nr, dep, f"sum_w_in_{l}")
            gw_out = _sum_into(gw_out, go, po, where2, l, lambda w, nr: (2 * w[1] + w[0]) * nr, dep,
                               f"sum_w_out_{l}")
        gw_in, gw_out = _exchange_halves(gw_in, gw_out, first, count, f"exchange_halves_{first}")
        res_in = _adamw_big(w_in, gw_in, m_w_in, v_w_in, first, count, res_in, f"adamw_w_in_{first}")
        res_out = _adamw_big(w_out, gw_out, m_w_out, v_w_out, first, count, res_out, f"adamw_w_out_{first}")
    grad_w_in, delta_w_in, new_m_w_in, new_v_w_in = res_in
    grad_w_out, delta_w_out, new_m_w_out, new_v_w_out = res_out

    dmod_mine = jnp.stack(d_mod)
    dmod_gathered = _all_gather_small(_rows8(dmod_mine), "gather_dmod")
    dmod_all = dmod_gathered[:, :depth, :]
    grad_b_ada = _sum_gathered(dmod_gathered, "sum_b_ada")[:depth]
    dmod_cols = lax.dynamic_slice_in_dim(dmod_all, chip * ada_cols, ada_cols, axis=2)
    dmod128 = jnp.pad(jnp.transpose(dmod_cols, (1, 0, 2)), ((0, 0), (0, 120), (0, 0))).astype(BF16)
    cact128 = jnp.pad(cact, ((0, 120), (0, 0))).astype(BF16)
    grad_w_ada, delta_w_ada, new_m_w_ada, new_v_w_ada = _adamw_ada(w_ada, m_w_ada, v_w_ada, cact128, dmod128,
                                                                  "adamw_w_ada")

    dlg_all = jnp.stack(d_lg)
    sig_f = jax.nn.sigmoid(-ret_decay_logit_f)
    sig_b = jax.nn.sigmoid(-ret_decay_logit_b)
    nlg = depth * cfg.hr
    pack = jnp.zeros((8, d), F32)
    for l in range(depth):
        pack = pack.at[l].set(d_gain[l])
    pack = pack.at[depth].set(fin_acc[0])
    pack = pack.at[depth + 1, 0].set(loss8[0, 0])
    pack = pack.at[depth + 1, LANES:LANES + nlg].set((dlg_all[:, :, 0] * sig_f).reshape(-1))
    pack = pack.at[depth + 1, 2 * LANES:2 * LANES + nlg].set((dlg_all[:, :, 1] * sig_b).reshape(-1))
    tot = _sum_gathered(_all_gather_small(pack, "gather_small"), "sum_small")
    grad_norm_gain = tot[:depth]
    grad_final_gain = tot[depth]
    loss = tot[depth + 1, 0]
    grad_lf = tot[depth + 1, LANES:LANES + nlg].reshape(depth, cfg.hr)
    grad_lb = tot[depth + 1, 2 * LANES:2 * LANES + nlg].reshape(depth, cfg.hr)

    d_ng, m_ng, v_ng = _adamw_small(norm_gain, grad_norm_gain, m_norm_gain, v_norm_gain, "adamw_norm_gain")
    d_ba, m_ba, v_ba = _adamw_small(b_ada, grad_b_ada, m_b_ada, v_b_ada, "adamw_b_ada")
    d_lf, m_lf, v_lf = _adamw_small(ret_decay_logit_f, grad_lf, m_ret_decay_logit_f, v_ret_decay_logit_f, "adamw_lf")
    d_lb, m_lb, v_lb = _adamw_small(ret_decay_logit_b, grad_lb, m_ret_decay_logit_b, v_ret_decay_logit_b, "adamw_lb")
    d_fg, m_fg, v_fg = _adamw_small(final_gain[None], grad_final_gain[None], m_final_gain[None], v_final_gain[None],
                                    "adamw_final_gain")

    return (loss, dx[None],
            grad_norm_gain, grad_w_ada, grad_b_ada, grad_w_in, grad_w_out, grad_lf, grad_lb, grad_final_gain,
            d_ng, delta_w_ada, d_ba, delta_w_in, delta_w_out, d_lf, d_lb, d_fg[0],
            m_ng, new_m_w_ada, m_ba, new_m_w_in, new_m_w_out, m_lf, m_lb, m_fg[0],
            v_ng, new_v_w_ada, v_ba, new_v_w_in, new_v_w_out, v_lf, v_lb, v_fg[0])
```

```python
import functools

import jax
import jax.numpy as jnp
from jax import lax
from jax.experimental import pallas as pl
from jax.experimental.pallas import tpu as pltpu

F32 = jnp.float32
BF16 = jnp.bfloat16

D_MODEL = 2048
SEQ = 4096
DEPTH = 4
HEAD_DIM = 128
DILATIONS = (1, 4, 16)
RADIUS = 64
N_HEADS_RET = 4
RET_QK = 128
RET_V = 256
RET_CHUNK = 256
NORM_EPS = 1e-6
MASK_VALUE = -1e30
N_DEV = 8
N_CHIP = 4
LANES = 128
VMEM_LIMIT = 56 * 1024 * 1024

ADAM_LR = 0.001
ADAM_B1 = 0.9
ADAM_B2 = 0.999
ADAM_EPS = 1e-08
ADAM_WD = 0.01
ADAM_STEP = 10

MESH = pl.DeviceIdType.MESH


class _Cfg:
    def __init__(self):
        self.d = D_MODEL
        self.s = SEQ
        self.aw = D_MODEL // 2
        self.ha = self.aw // HEAD_DIM
        self.rw = D_MODEL // 2
        self.hr = N_HEADS_RET
        self.rqk = self.hr * RET_QK
        self.f = 4 * self.aw + 2 * self.rqk + 2 * self.rw
        self.qa, self.ka, self.va, self.za = 0, self.aw, 2 * self.aw, 3 * self.aw
        self.qr = 4 * self.aw
        self.kr = self.qr + self.rqk
        self.vr = self.kr + self.rqk
        self.zr = self.vr + self.rw
        assert self.rw == self.hr * RET_V


def _tile(n, pref):
    t = min(n, pref)
    while n % t or t % LANES:
        t -= LANES
    return t


def _params(dims=None):
    return pltpu.CompilerParams(dimension_semantics=dims, vmem_limit_bytes=VMEM_LIMIT)


def _silu(z):
    return z * jax.nn.sigmoid(z)


def _dsilu(z):
    sg = jax.nn.sigmoid(z)
    return sg * (1.0 + z * (1.0 - sg))


_DN = {"nn": (((1,), (0,)), ((), ())), "nt": (((1,), (1,)), ((), ())), "tn": (((0,), (0,)), ((), ()))}


def _matmul(a, b, mode, out_dtype, name, tm=1024, tn=1024, tk=2048):
    if mode == "tn":
        kk, m = a.shape
    else:
        m, kk = a.shape
    n = b.shape[0] if mode == "nt" else b.shape[1]
    tm, tn, tk = _tile(m, tm), _tile(n, tn), _tile(kk, tk)
    nk = kk // tk
    a_spec = (pl.BlockSpec((tk, tm), lambda i, j, k: (k, i)) if mode == "tn"
              else pl.BlockSpec((tm, tk), lambda i, j, k: (i, k)))
    b_spec = (pl.BlockSpec((tn, tk), lambda i, j, k: (j, k)) if mode == "nt"
              else pl.BlockSpec((tk, tn), lambda i, j, k: (k, j)))
    dn = _DN[mode]

    def body(a_ref, b_ref, o_ref, *acc):
        p = lax.dot_general(a_ref[...], b_ref[...], dn, preferred_element_type=F32)
        if nk == 1:
            o_ref[...] = p.astype(out_dtype)
            return
        acc_ref, = acc
        k = pl.program_id(2)

        @pl.when(k == 0)
        def _():
            acc_ref[...] = p

        @pl.when(k > 0)
        def _():
            acc_ref[...] += p

        @pl.when(k == nk - 1)
        def _():
            o_ref[...] = acc_ref[...].astype(out_dtype)

    return pl.pallas_call(
        body, name=name, grid=(m // tm, n // tn, nk),
        in_specs=[a_spec, b_spec],
        out_specs=pl.BlockSpec((tm, tn), lambda i, j, k: (i, j)),
        out_shape=jax.ShapeDtypeStruct((m, n), out_dtype),
        scratch_shapes=[pltpu.VMEM((tm, tn), F32)] if nk > 1 else [],
        compiler_params=_params(("parallel", "parallel", "arbitrary")),
    )(a, b)


def _out_proj_fwd(y, w_out, x, mod3, b3, name):
    s, kk = y.shape
    d = w_out.shape[1]
    tm, tn = _tile(s, 256), d

    def body(y_ref, w_ref, x_ref, m_ref, b_ref, xn_ref, o_ref):
        out = jnp.dot(y_ref[...], w_ref[...], preferred_element_type=F32)
        gate = m_ref[2:3, :] + b_ref[2:3, :]
        xn_ref[...] = x_ref[...] + gate * out
        o_ref[...] = out.astype(BF16)

    vec = pl.BlockSpec((8, tn), lambda i, j: (0, j))
    return pl.pallas_call(
        body, name=name, grid=(s // tm, d // tn),
        in_specs=[pl.BlockSpec((tm, kk), lambda i, j: (i, 0)), pl.BlockSpec((kk, tn), lambda i, j: (0, j)),
                  pl.BlockSpec((tm, tn), lambda i, j: (i, j)), vec, vec],
        out_specs=[pl.BlockSpec((tm, tn), lambda i, j: (i, j)), pl.BlockSpec((tm, tn), lambda i, j: (i, j))],
        out_shape=[jax.ShapeDtypeStruct((s, d), F32), jax.ShapeDtypeStruct((s, d), BF16)],
        compiler_params=_params(("parallel", "parallel")),
    )(y, w_out, x, mod3, b3)


def _norm_mod_fwd(x, g8, mod3, b3, name):
    s, d = x.shape
    tr = _tile(s, 512)

    def body(x_ref, g_ref, m_ref, b_ref, h_ref):
        xv = x_ref[...]
        r = lax.rsqrt(jnp.mean(xv * xv, axis=-1, keepdims=True) + NORM_EPS)
        shift = m_ref[0:1, :] + b_ref[0:1, :]
        scale = m_ref[1:2, :] + b_ref[1:2, :]
        h_ref[...] = ((xv * r * g_ref[0:1, :]) * (1.0 + scale) + shift).astype(BF16)

    vec = pl.BlockSpec((8, d), lambda i: (0, 0))
    return pl.pallas_call(
        body, name=name, grid=(s // tr,),
        in_specs=[pl.BlockSpec((tr, d), lambda i: (i, 0)), vec, vec, vec],
        out_specs=pl.BlockSpec((tr, d), lambda i: (i, 0)),
        out_shape=jax.ShapeDtypeStruct((s, d), BF16),
        compiler_params=_params(("parallel",)),
    )(x, g8, mod3, b3)


def _norm_mod_bwd(dh, x, dx_next, g8, mod3, b3, name):
    s, d = x.shape
    tr = _tile(s, 256)

    def body(dh_ref, x_ref, dn_ref, g_ref, m_ref, b_ref, dx_ref, acc_ref):
        @pl.when(pl.program_id(0) == 0)
        def _():
            acc_ref[...] = jnp.zeros_like(acc_ref)

        xv = x_ref[...]
        dh_v = dh_ref[...]
        g = g_ref[0:1, :]
        r = lax.rsqrt(jnp.mean(xv * xv, axis=-1, keepdims=True) + NORM_EPS)
        xn = xv * r
        scale1 = 1.0 + m_ref[1:2, :] + b_ref[1:2, :]
        dhs = dh_v * scale1
        dxn = dhs * g
        dx_ref[...] = dn_ref[...] + r * (dxn - xn * jnp.mean(dxn * xn, axis=-1, keepdims=True))
        acc_ref[0:1, :] += jnp.sum(dh_v, axis=0, keepdims=True)
        acc_ref[1:2, :] += jnp.sum(dh_v * (xn * g), axis=0, keepdims=True)
        acc_ref[2:3, :] += jnp.sum(dhs * xn, axis=0, keepdims=True)

    vec = pl.BlockSpec((8, d), lambda i: (0, 0))
    row = pl.BlockSpec((tr, d), lambda i: (i, 0))
    return pl.pallas_call(
        body, name=name, grid=(s // tr,),
        in_specs=[row, row, row, vec, vec, vec],
        out_specs=[row, vec],
        out_shape=[jax.ShapeDtypeStruct((s, d), F32), jax.ShapeDtypeStruct((8, d), F32)],
        compiler_params=_params(("arbitrary",)),
    )(dh, x, dx_next, g8, mod3, b3)


def _final_loss(x, tgt, g8, name):
    s, d = x.shape
    tr = _tile(s, 256)

    def body(x_ref, t_ref, g_ref, dx_ref, loss_ref, acc_ref):
        @pl.when(pl.program_id(0) == 0)
        def _():
            acc_ref[...] = jnp.zeros_like(acc_ref)
            loss_ref[...] = jnp.zeros_like(loss_ref)

        xv = x_ref[...]
        g = g_ref[0:1, :]
        r = lax.rsqrt(jnp.mean(xv * xv, axis=-1, keepdims=True) + NORM_EPS)
        xn = xv * r
        err = xn * g - t_ref[...]
        loss_ref[...] += 0.5 * jnp.sum(jnp.mean(err * err, axis=-1, keepdims=True), axis=0, keepdims=True)
        dy = err * (1.0 / d)
        acc_ref[0:1, :] += jnp.sum(dy * xn, axis=0, keepdims=True)
        dxn = dy * g
        dx_ref[...] = r * (dxn - xn * jnp.mean(dxn * xn, axis=-1, keepdims=True))

    vec = pl.BlockSpec((8, d), lambda i: (0, 0))
    row = pl.BlockSpec((tr, d), lambda i: (i, 0))
    return pl.pallas_call(
        body, name=name, grid=(s // tr,),
        in_specs=[row, row, vec],
        out_specs=[row, pl.BlockSpec((8, LANES), lambda i: (0, 0)), vec],
        out_shape=[jax.ShapeDtypeStruct((s, d), F32), jax.ShapeDtypeStruct((8, LANES), F32),
                   jax.ShapeDtypeStruct((8, d), F32)],
        compiler_params=_params(("arbitrary",)),
    )(x, tgt, g8)


def _out_proj_bwd_prep(dxn, out, mod3, b3, name):
    s, d = dxn.shape
    tr = _tile(s, 512)

    def body(dx_ref, o_ref, m_ref, b_ref, do_ref, acc_ref):
        @pl.when(pl.program_id(0) == 0)
        def _():
            acc_ref[...] = jnp.zeros_like(acc_ref)

        dxv = dx_ref[...]
        gate = m_ref[2:3, :] + b_ref[2:3, :]
        do_ref[...] = (gate * dxv).astype(BF16)
        acc_ref[0:1, :] += jnp.sum(dxv * o_ref[...].astype(F32), axis=0, keepdims=True)

    vec = pl.BlockSpec((8, d), lambda i: (0, 0))
    row = pl.BlockSpec((tr, d), lambda i: (i, 0))
    return pl.pallas_call(
        body, name=name, grid=(s // tr,),
        in_specs=[row, row, vec, vec],
        out_specs=[row, vec],
        out_shape=[jax.ShapeDtypeStruct((s, d), BF16), jax.ShapeDtypeStruct((8, d), F32)],
        compiler_params=_params(("arbitrary",)),
    )(dxn, out, mod3, b3)


_COPY_ROWS = 512


_ATTN_WIN = 384


def _attn_geometry(cfg, dil):
    sub = cfg.s // dil
    if sub <= _ATTN_WIN:
        return sub, sub, sub
    return sub, _ATTN_WIN - 2 * RADIUS, _ATTN_WIN


_N_SHIFTS = 3


def _attn_rows(dil, r, i, sub, bq, win):
    margin = (win - bq) // 2
    ws = jnp.clip(i * bq - margin, 0, sub - win)
    shift = (i * bq - ws) // margin if margin else 0
    if dil == 1:
        return (shift, pl.ds(pl.multiple_of(i * bq, bq), bq), pl.ds(pl.multiple_of(ws, 64), win))
    return (shift, pl.ds(r + i * (bq * dil), bq, stride=dil), pl.ds(r + ws * dil, win, stride=dil))


def _fill_bias_tables(cfg, bias_ref, slope):
    for pattern, dil in enumerate(DILATIONS):
        _, bq, win = _attn_geometry(cfg, dil)
        margin = (win - bq) // 2
        rel0 = lax.broadcasted_iota(jnp.int32, (bq, win), 1) - lax.broadcasted_iota(jnp.int32, (bq, win), 0)
        for shift in range(_N_SHIFTS if margin else 1):
            arel = jnp.abs(rel0 - shift * margin)
            bias_ref[pattern * _N_SHIFTS + shift, 0:bq, 0:win] = jnp.where(
                arel <= RADIUS, -(slope * dil) * arel.astype(F32), MASK_VALUE)


def _bias_scratch():
    return pltpu.VMEM((len(DILATIONS) * _N_SHIFTS, _ATTN_WIN - 2 * RADIUS, _ATTN_WIN), F32)


def _attn_scores(cfg, bias_ref, q, kw, pattern, shift):
    bq, win = q.shape[0], kw.shape[0]
    return (lax.dot_general(q, kw, _DN["nt"], preferred_element_type=F32)
            + bias_ref[pattern * _N_SHIFTS + shift, 0:bq, 0:win])


def _for_each_group(cfg, group, size):
    for pattern, dil in reversed(list(enumerate(DILATIONS))):
        first = pattern == len(DILATIONS) - 1
        sub, bq, _ = _attn_geometry(cfg, dil)
        nblk = sub // bq
        per = min(size, nblk)
        for r in range(dil):
            if nblk == per:
                group(pattern, dil, first, [(r, i) for i in range(nblk)])
            else:
                def step(g, carry, pattern=pattern, dil=dil, first=first, r=r, per=per):
                    group(pattern, dil, first, [(r, g * per + j) for j in range(per)])
                    return carry

                lax.fori_loop(0, nblk // per, step, 0)


def _attn_fwd(cfg, proj, slopes, name):
    s = cfg.s
    scale = HEAD_DIM ** -0.5

    def body(sl_ref, q_ref, k_ref, v_ref, z_ref, o_ref, lse_ref, y_ref, qf, kf, vf, acc, m_s, l_s, bias_s):
        slope = sl_ref[pl.program_id(0)]

        def to_f32(i, carry):
            rows = pl.ds(pl.multiple_of(i * _COPY_ROWS, _COPY_ROWS), _COPY_ROWS)
            qf[rows, :] = q_ref[rows, :].astype(F32) * scale
            kf[rows, :] = k_ref[rows, :].astype(F32)
            vf[rows, :] = v_ref[rows, :].astype(F32)
            return carry

        lax.fori_loop(0, s // _COPY_ROWS, to_f32, 0)

        _fill_bias_tables(cfg, bias_s, slope)

        def group(pattern, dil, first, blocks):
            sub, bq, win = _attn_geometry(cfg, dil)
            rep = win // HEAD_DIM
            work = []
            for r, i in blocks:
                shift, qrows, krows = _attn_rows(dil, r, i, sub, bq, win)
                old = None if first else (m_s[qrows, :], l_s[qrows, :], acc[qrows, :])
                work.append((shift, qrows, qf[qrows, :].astype(BF16), kf[krows, :].astype(BF16),
                             vf[krows, :].astype(BF16), old))
            new = []
            for shift, qrows, q, kw, vw, old in work:
                sc = _attn_scores(cfg, bias_s, q, kw, pattern, shift)
                m_blk = jnp.max(sc, axis=-1, keepdims=True)
                if first:
                    m_new = jnp.broadcast_to(m_blk, (bq, HEAD_DIM))
                    p = jnp.exp(sc - m_blk)
                    l_new = jnp.broadcast_to(jnp.sum(p, axis=-1, keepdims=True), (bq, HEAD_DIM))
                    a_new = jnp.dot(p.astype(BF16), vw, preferred_element_type=F32)
                else:
                    m_old, l_old, a_old = old
                    m_new = jnp.maximum(m_old, m_blk)
                    alpha = jnp.exp(m_old - m_new)
                    p = jnp.exp(sc - jnp.tile(m_new, (1, rep)))
                    l_new = alpha * l_old + jnp.sum(p, axis=-1, keepdims=True)
                    a_new = alpha * a_old + jnp.dot(p.astype(BF16), vw, preferred_element_type=F32)
                new.append((qrows, m_new, l_new, a_new))
            for qrows, m_new, l_new, a_new in new:
                m_s[qrows, :] = m_new
                l_s[qrows, :] = l_new
                acc[qrows, :] = a_new

        _for_each_group(cfg, group, size=4)

        def finish(i, carry):
            rows = pl.ds(pl.multiple_of(i * _COPY_ROWS, _COPY_ROWS), _COPY_ROWS)
            den = l_s[rows, :]
            o = (acc[rows, :] / den).astype(BF16)
            o_ref[rows, :] = o
            lse_ref[rows, :] = m_s[rows, :] + jnp.log(den)
            y_ref[rows, :] = (o.astype(F32) * _silu(z_ref[rows, :].astype(F32))).astype(BF16)
            return carry

        lax.fori_loop(0, s // _COPY_ROWS, finish, 0)

    def col(off):
        return pl.BlockSpec((s, HEAD_DIM), lambda h: (0, off // HEAD_DIM + h))

    head = pl.BlockSpec((s, HEAD_DIM), lambda h: (0, h))
    return pl.pallas_call(
        body, name=name, grid=(cfg.ha,),
        in_specs=[pl.BlockSpec(memory_space=pltpu.SMEM), col(cfg.qa), col(cfg.ka), col(cfg.va), col(cfg.za)],
        out_specs=[head, head, head],
        out_shape=[jax.ShapeDtypeStruct((s, cfg.aw), BF16), jax.ShapeDtypeStruct((s, cfg.aw), F32),
                   jax.ShapeDtypeStruct((s, cfg.d), BF16)],
        scratch_shapes=[pltpu.VMEM((s, HEAD_DIM), F32)] * 6 + [_bias_scratch()],
        compiler_params=_params(("parallel",)),
    )(slopes, proj, proj, proj, proj)


def _attn_bwd(cfg, proj, slopes, dy, oa, lse, name):
    s = cfg.s
    scale = HEAD_DIM ** -0.5
    dst_blocks = [c0 // HEAD_DIM for c0 in (cfg.za, cfg.qa, cfg.ka, cfg.va)]
    sub0, _, win0 = _attn_geometry(cfg, DILATIONS[-1])
    assign_first = sub0 == win0

    def body(sl_ref, q_ref, k_ref, v_ref, z_ref, dy_ref, o_ref, lse_ref, dproj_ref,
             qf, kf, vf, dof, dlt, dqa, dka, dva, bias_s, stage, sems):
        head = pl.program_id(0)
        slope = sl_ref[head]

        def out_copy(slot):
            cols = pl.ds(pl.multiple_of((dst_blocks[slot] + head) * HEAD_DIM, HEAD_DIM), HEAD_DIM)
            return pltpu.make_async_copy(stage.at[slot], dproj_ref.at[:, cols], sems.at[slot])

        @pl.when(head > 0)
        def _():
            out_copy(0).wait()

        def to_f32(i, carry):
            rows = pl.ds(pl.multiple_of(i * _COPY_ROWS, _COPY_ROWS), _COPY_ROWS)
            qf[rows, :] = q_ref[rows, :].astype(F32) * scale
            kf[rows, :] = k_ref[rows, :].astype(F32)
            vf[rows, :] = v_ref[rows, :].astype(F32)
            dyv = dy_ref[rows, :].astype(F32)
            zv = z_ref[rows, :].astype(F32)
            ov = o_ref[rows, :].astype(F32)
            dov = dyv * _silu(zv)
            dof[rows, :] = dov
            dlt[rows, :] = jnp.broadcast_to(jnp.sum(dov * ov, axis=-1, keepdims=True), (_COPY_ROWS, HEAD_DIM))
            stage[0, rows, :] = (dyv * ov * _dsilu(zv)).astype(BF16)
            if not assign_first:
                zero = jnp.zeros((_COPY_ROWS, HEAD_DIM), F32)
                dqa[rows, :] = zero
                dka[rows, :] = zero
                dva[rows, :] = zero
            return carry

        lax.fori_loop(0, s // _COPY_ROWS, to_f32, 0)
        out_copy(0).start()

        _fill_bias_tables(cfg, bias_s, slope)

        def group(pattern, dil, first, blocks):
            sub, bq, win = _attn_geometry(cfg, dil)
            rep = win // HEAD_DIM
            assign = first and assign_first
            work = []
            for r, i in blocks:
                shift, qrows, krows = _attn_rows(dil, r, i, sub, bq, win)
                work.append((shift, qrows, krows, qf[qrows, :].astype(BF16), kf[krows, :].astype(BF16),
                             vf[krows, :].astype(BF16), dof[qrows, :].astype(BF16),
                             lse_ref[qrows, :], dlt[qrows, :]))
            new = []
            for shift, qrows, krows, q, kw, vw, dob, lse_b, dlt_b in work:
                sc = _attn_scores(cfg, bias_s, q, kw, pattern, shift)
                p = jnp.exp(sc - jnp.tile(lse_b, (1, rep)))
                dp = lax.dot_general(dob, vw, _DN["nt"], preferred_element_type=F32)
                ds = (p * (dp - jnp.tile(dlt_b, (1, rep)))).astype(BF16)
                new.append((qrows, krows,
                            jnp.dot(ds, kw, preferred_element_type=F32) * scale,
                            lax.dot_general(ds, q, _DN["tn"], preferred_element_type=F32),
                            lax.dot_general(p.astype(BF16), dob, _DN["tn"], preferred_element_type=F32)))
            for qrows, krows, dq_b, dk_b, dv_b in new:
                if assign:
                    dqa[qrows, :] = dq_b
                    dka[krows, :] = dk_b
                    dva[krows, :] = dv_b
                else:
                    dqa[qrows, :] += dq_b
                    dka[krows, :] += dk_b
                    dva[krows, :] += dv_b

        _for_each_group(cfg, group, size=4)

        @pl.when(head > 0)
        def _():
            for slot in (1, 2, 3):
                out_copy(slot).wait()

        def emit(i, carry):
            rows = pl.ds(pl.multiple_of(i * _COPY_ROWS, _COPY_ROWS), _COPY_ROWS)
            stage[1, rows, :] = dqa[rows, :].astype(BF16)
            stage[2, rows, :] = dka[rows, :].astype(BF16)
            stage[3, rows, :] = dva[rows, :].astype(BF16)
            return carry

        lax.fori_loop(0, s // _COPY_ROWS, emit, 0)
        for slot in (1, 2, 3):
            out_copy(slot).start()

        @pl.when(head == pl.num_programs(0) - 1)
        def _():
            for slot in range(4):
                out_copy(slot).wait()

    def col(off):
        return pl.BlockSpec((s, HEAD_DIM), lambda h: (0, off // HEAD_DIM + h))

    head_cols = pl.BlockSpec((s, HEAD_DIM), lambda h: (0, h))
    return pl.pallas_call(
        body, name=name, grid=(cfg.ha,),
        in_specs=[pl.BlockSpec(memory_space=pltpu.SMEM), col(cfg.qa), col(cfg.ka), col(cfg.va), col(cfg.za),
                  head_cols, head_cols, head_cols],
        out_specs=_ANY,
        out_shape=jax.ShapeDtypeStruct((s, cfg.f), BF16),
        scratch_shapes=[pltpu.VMEM((s, HEAD_DIM), F32)] * 8
        + [_bias_scratch(), pltpu.VMEM((4, s, HEAD_DIM), BF16), pltpu.SemaphoreType.DMA((4,))],
        compiler_params=_params(("arbitrary",)),
    )(slopes, proj, proj, proj, proj, dy, oa, lse)


def _decay_tables(lg, backward):
    c = RET_CHUNK
    a = lax.broadcasted_iota(jnp.int32, (c, c), 0)
    b = lax.broadcasted_iota(jnp.int32, (c, c), 1)
    idx = lax.broadcasted_iota(jnp.int32, (c, 1), 0).astype(F32)
    if backward:
        rel = (b - a).astype(F32)
        ex_xi = c - idx
        ex_zeta = idx
    else:
        rel = (a - b).astype(F32)
        ex_xi = idx + 1.0
        ex_zeta = c - 1.0 - idx
    relc = jnp.maximum(rel, 0.0)
    dm = jnp.where(rel >= 0, jnp.exp(relc * lg), 0.0)
    xi = jnp.exp(ex_xi * lg)
    zeta = jnp.exp(ex_zeta * lg)
    gch = jnp.exp(jnp.full((1, 1), c, F32) * lg)
    return relc, dm, xi, zeta, ex_xi, ex_zeta, gch


def _ret_fwd(cfg, proj, lgs, y, name):
    s = cfg.s
    c = RET_CHUNK
    n = s // c
    kscale = RET_QK ** -0.5

    def body(lg_ref, q_ref, k_ref, v_ref, z_ref, y_in, o_ref, y_ref, st_ref):
        h = pl.program_id(0)
        tabs = [_decay_tables(lg_ref[dirn, h], dirn == 1) for dirn in range(2)]
        st_ref[...] = jnp.zeros_like(st_ref)
        o_ref[...] = jnp.zeros_like(o_ref)

        def step(t, carry):
            for dirn in range(2):
                _, dm, xi, zeta, _, _, gch = tabs[dirn]
                i = (n - 1 - t) if dirn == 1 else t
                rows = pl.ds(pl.multiple_of(i * c, c), c)
                qi = q_ref[rows, :]
                ks = k_ref[rows, :].astype(F32) * kscale
                vi = v_ref[rows, :]
                inner = lax.dot_general(qi, ks.astype(BF16), _DN["nt"], preferred_element_type=F32) * dm
                st = st_ref[dirn]
                o_ref[rows, :] += (jnp.dot(inner.astype(BF16), vi, preferred_element_type=F32)
                                   + jnp.dot(qi, st.astype(BF16), preferred_element_type=F32) * xi)
                st_ref[dirn] = st * gch + lax.dot_general((ks * zeta).astype(BF16), vi, _DN["tn"],
                                                          preferred_element_type=F32)
            return carry

        lax.fori_loop(0, n, step, 0, unroll=2)

        def gate(i, carry):
            rows = pl.ds(pl.multiple_of(i * _COPY_ROWS, _COPY_ROWS), _COPY_ROWS)
            oh = o_ref[rows, :]
            rr = lax.rsqrt(jnp.mean(oh * oh, axis=-1, keepdims=True) + NORM_EPS)
            y_ref[rows, :] = (oh * rr * _silu(z_ref[rows, :].astype(F32))).astype(BF16)
            return carry

        lax.fori_loop(0, s // _COPY_ROWS, gate, 0)

    return pl.pallas_call(
        body, name=name, grid=(cfg.hr,),
        in_specs=[pl.BlockSpec(memory_space=pltpu.SMEM),
                  pl.BlockSpec((s, RET_QK), lambda h: (0, cfg.qr // RET_QK + h)),
                  pl.BlockSpec((s, RET_QK), lambda h: (0, cfg.kr // RET_QK + h)),
                  pl.BlockSpec((s, RET_V), lambda h: (0, cfg.vr // RET_V + h)),
                  pl.BlockSpec((s, RET_V), lambda h: (0, cfg.zr // RET_V + h)), _ANY],
        out_specs=[pl.BlockSpec((s, RET_V), lambda h: (0, h)),
                   pl.BlockSpec((s, RET_V), lambda h: (0, cfg.aw // RET_V + h))],
        out_shape=[jax.ShapeDtypeStruct((s, cfg.rw), F32), jax.ShapeDtypeStruct(y.shape, BF16)],
        input_output_aliases={5: 1},
        scratch_shapes=[pltpu.VMEM((2, RET_QK, RET_V), F32)],
        compiler_params=_params(("parallel",)),
    )(lgs, proj, proj, proj, proj, y)


def _ret_bwd(cfg, proj, lgs, dy, oret, dproj, name):
    s = cfg.s
    c = RET_CHUNK
    n = s // c
    kscale = RET_QK ** -0.5

    def accumulate(lg_ref, q_ref, k_ref, v_ref, do_ref, dq_ref, dk_ref, dv_ref, dlg_ref, states, t_ref,
                   e_dm, e_xi, e_zeta, e_g):
        h = pl.program_id(0)
        tabs = [_decay_tables(lg_ref[dirn, h], dirn == 1) for dirn in range(2)]
        dlg_ref[...] = jnp.zeros_like(dlg_ref)
        dq_ref[...] = jnp.zeros_like(dq_ref)
        dk_ref[...] = jnp.zeros_like(dk_ref)
        dv_ref[...] = jnp.zeros_like(dv_ref)

        def chunk_rows(dirn, t):
            i = (n - 1 - t) if dirn == 1 else t
            return pl.ds(pl.multiple_of(i * c, c), c)

        t_ref[...] = jnp.zeros_like(t_ref)

        def fwd_step(t, carry):
            for dirn in range(2):
                _, _, _, zeta, _, _, gch = tabs[dirn]
                rows = chunk_rows(dirn, t)
                st = t_ref[dirn]
                states[dirn, t] = st
                ks = k_ref[rows, :].astype(F32) * kscale
                t_ref[dirn] = st * gch + lax.dot_general((ks * zeta).astype(BF16), v_ref[rows, :], _DN["tn"],
                                                         preferred_element_type=F32)
            return carry

        lax.fori_loop(0, n, fwd_step, 0, unroll=2)
        t_ref[...] = jnp.zeros_like(t_ref)

        for ref in (e_dm, e_xi, e_zeta, e_g):
            ref[...] = jnp.zeros_like(ref)

        def bwd_step(u, carry):
            t = n - 1 - u
            for dirn in range(2):
                relc, dm, xi, zeta, ex_xi, ex_zeta, gch = tabs[dirn]
                rows = chunk_rows(dirn, t)
                qi = q_ref[rows, :]
                ks = k_ref[rows, :].astype(F32) * kscale
                ksb = ks.astype(BF16)
                vi = v_ref[rows, :]
                doi = do_ref[rows, :]
                sn_f = states[dirn, t]
                sn = sn_f.astype(BF16)
                tt = t_ref[dirn]
                ttb = tt.astype(BF16)
                a_mat = lax.dot_general(qi, ksb, _DN["nt"], preferred_element_type=F32) * dm
                dov = lax.dot_general(doi, vi, _DN["nt"], preferred_element_type=F32)
                b_mat = (dov * dm).astype(BF16)
                kz = (ks * zeta).astype(BF16)
                d_v = (jnp.dot(kz, ttb, preferred_element_type=F32)
                       + lax.dot_general(a_mat.astype(BF16), doi, _DN["tn"], preferred_element_type=F32))
                dk_inter = lax.dot_general(vi, ttb, _DN["nt"], preferred_element_type=F32) * zeta
                d_k = lax.dot_general(b_mat, qi, _DN["tn"], preferred_element_type=F32) + dk_inter
                o_inter = jnp.dot(qi, sn, preferred_element_type=F32) * xi
                d_q = (jnp.dot(b_mat, ksb, preferred_element_type=F32)
                       + lax.dot_general(doi, sn, _DN["nt"], preferred_element_type=F32) * xi)
                e_dm[dirn] += relc * a_mat * dov
                e_xi[dirn] += ex_xi * (doi.astype(F32) * o_inter)
                e_zeta[dirn] += ex_zeta * (ks * dk_inter)
                e_g[dirn] += tt * sn_f
                t_ref[dirn] = tt * gch + lax.dot_general((qi.astype(F32) * xi).astype(BF16), doi, _DN["tn"],
                                                         preferred_element_type=F32)
                dq_ref[rows, :] += d_q
                dk_ref[rows, :] += d_k * kscale
                dv_ref[rows, :] += d_v
            return carry

        lax.fori_loop(0, n, bwd_step, 0, unroll=2)
        for dirn in range(2):
            total = (jnp.sum(e_dm[dirn], keepdims=True) + jnp.sum(e_xi[dirn], keepdims=True)
                     + jnp.sum(e_zeta[dirn], keepdims=True) + (c * tabs[dirn][6]) * jnp.sum(e_g[dirn], keepdims=True))
            dlg_ref[0, dirn:dirn + 1, :] = jnp.broadcast_to(total, (1, LANES))

    def body(lg_ref, q_ref, k_ref, v_ref, z_ref, dy_ref, or_ref, dproj_in, dproj_ref, dlg_ref,
             do_s, dq_s, dk_s, dv_s, stage_z, stage_q, stage_k, stage_v, sems, *scratch):
        head = pl.program_id(0)
        stages = (stage_z, stage_q, stage_k, stage_v)
        firsts = (cfg.zr, cfg.qr, cfg.kr, cfg.vr)

        def out_copy(slot):
            width = stages[slot].shape[1]
            cols = pl.ds(pl.multiple_of(firsts[slot] + head * width, LANES), width)
            return pltpu.make_async_copy(stages[slot], dproj_ref.at[:, cols], sems.at[slot])

        @pl.when(head > 0)
        def _():
            out_copy(0).wait()

        def gate_norm_bwd(i, carry):
            rows = pl.ds(pl.multiple_of(i * _COPY_ROWS, _COPY_ROWS), _COPY_ROWS)
            oh = or_ref[rows, :]
            zr = z_ref[rows, :].astype(F32)
            dyr = dy_ref[rows, :].astype(F32)
            rr = lax.rsqrt(jnp.mean(oh * oh, axis=-1, keepdims=True) + NORM_EPS)
            yn = oh * rr
            dyn = dyr * _silu(zr)
            stage_z[rows, :] = (dyr * yn * _dsilu(zr)).astype(BF16)
            do_s[rows, :] = (rr * (dyn - yn * jnp.mean(dyn * yn, axis=-1, keepdims=True))).astype(BF16)
            return carry

        lax.fori_loop(0, s // _COPY_ROWS, gate_norm_bwd, 0)
        out_copy(0).start()

        accumulate(lg_ref, q_ref, k_ref, v_ref, do_s, dq_s, dk_s, dv_s, dlg_ref, *scratch)

        @pl.when(head > 0)
        def _():
            for slot in (1, 2, 3):
                out_copy(slot).wait()

        def emit(i, carry):
            rows = pl.ds(pl.multiple_of(i * _COPY_ROWS, _COPY_ROWS), _COPY_ROWS)
            stage_q[rows, :] = dq_s[rows, :].astype(BF16)
            stage_k[rows, :] = dk_s[rows, :].astype(BF16)
            stage_v[rows, :] = dv_s[rows, :].astype(BF16)
            return carry

        lax.fori_loop(0, s // _COPY_ROWS, emit, 0)
        for slot in (1, 2, 3):
            out_copy(slot).start()

        @pl.when(head == pl.num_programs(0) - 1)
        def _():
            for slot in range(4):
                out_copy(slot).wait()

    return pl.pallas_call(
        body, name=name, grid=(cfg.hr,),
        in_specs=[pl.BlockSpec(memory_space=pltpu.SMEM),
                  pl.BlockSpec((s, RET_QK), lambda h: (0, cfg.qr // RET_QK + h)),
                  pl.BlockSpec((s, RET_QK), lambda h: (0, cfg.kr // RET_QK + h)),
                  pl.BlockSpec((s, RET_V), lambda h: (0, cfg.vr // RET_V + h)),
                  pl.BlockSpec((s, RET_V), lambda h: (0, cfg.zr // RET_V + h)),
                  pl.BlockSpec((s, RET_V), lambda h: (0, cfg.aw // RET_V + h)),
                  pl.BlockSpec((s, RET_V), lambda h: (0, h)), _ANY],
        out_specs=[_ANY, pl.BlockSpec((1, 8, LANES), lambda h: (h, 0, 0))],
        out_shape=[jax.ShapeDtypeStruct(dproj.shape, BF16), jax.ShapeDtypeStruct((cfg.hr, 8, LANES), F32)],
        input_output_aliases={7: 0},
        scratch_shapes=[pltpu.VMEM((s, RET_V), BF16),
                        pltpu.VMEM((s, RET_QK), F32), pltpu.VMEM((s, RET_QK), F32), pltpu.VMEM((s, RET_V), F32),
                        pltpu.VMEM((s, RET_V), BF16), pltpu.VMEM((s, RET_QK), BF16), pltpu.VMEM((s, RET_QK), BF16),
                        pltpu.VMEM((s, RET_V), BF16), pltpu.SemaphoreType.DMA((4,)),
                        pltpu.VMEM((2, n, RET_QK, RET_V), F32), pltpu.VMEM((2, RET_QK, RET_V), F32),
                        pltpu.VMEM((2, c, c), F32), pltpu.VMEM((2, c, RET_V), F32),
                        pltpu.VMEM((2, c, RET_QK), F32), pltpu.VMEM((2, RET_QK, RET_V), F32)],
        compiler_params=_params(("arbitrary",)),
    )(lgs, proj, proj, proj, proj, dy, oret, dproj)


def _ada_fwd(cact16, w_ada, name):
    depth, d, n = w_ada.shape
    tn = _tile(n, 768)

    def body(c_ref, w_ref, o_ref):
        o_ref[0] = jnp.dot(c_ref[...], w_ref[0].astype(BF16), preferred_element_type=F32)

    return pl.pallas_call(
        body, name=name, grid=(depth, n // tn),
        in_specs=[pl.BlockSpec((16, d), lambda l, j: (0, 0)), pl.BlockSpec((1, d, tn), lambda l, j: (l, 0, j))],
        out_specs=pl.BlockSpec((1, 16, tn), lambda l, j: (l, 0, j)),
        out_shape=jax.ShapeDtypeStruct((depth, 16, n), F32),
        compiler_params=_params(("parallel", "parallel")),
    )(cact16, w_ada)


def _adam_math(w, g, m, v):
    m2 = ADAM_B1 * m + (1.0 - ADAM_B1) * g
    v2 = ADAM_B2 * v + (1.0 - ADAM_B2) * (g * g)
    m_hat = m2 / (1.0 - ADAM_B1 ** ADAM_STEP)
    v_hat = v2 / (1.0 - ADAM_B2 ** ADAM_STEP)
    delta = -ADAM_LR * (m_hat / (jnp.sqrt(v_hat) + ADAM_EPS) + ADAM_WD * w)
    return delta, m2, v2


def _adamw_big(w, g, m, v, first, count, prev, name):
    _, r, c = w.shape
    tr, tc = _tile(r, 512), _tile(c, 1024)

    def body(w_ref, g_ref, m_ref, v_ref, *rest):
        go_ref, d_ref, mo_ref, vo_ref = rest[-4:]
        gv = g_ref[...]
        delta, m2, v2 = _adam_math(w_ref[...], gv, m_ref[...], v_ref[...])
        go_ref[...] = gv
        d_ref[...] = delta
        mo_ref[...] = m2
        vo_ref[...] = v2

    spec = pl.BlockSpec((1, tr, tc), lambda l, i, j: (first + l, i, j))
    shp = jax.ShapeDtypeStruct(w.shape, F32)
    carried = [] if prev is None else list(prev)
    return pl.pallas_call(
        body, name=name, grid=(count, r // tr, c // tc),
        in_specs=[spec] * 4 + [_ANY] * len(carried), out_specs=[spec] * 4, out_shape=[shp] * 4,
        input_output_aliases={4 + k: k for k in range(len(carried))},
        compiler_params=_params(("parallel", "parallel", "parallel")),
    )(w, g, m, v, *carried)


def _adamw_ada(w, m, v, cact128, dmod128, name):
    depth, r, c = w.shape
    tr, tc = _tile(r, 512), _tile(c, 768)

    def body(w_ref, m_ref, v_ref, c_ref, dm_ref, go_ref, d_ref, mo_ref, vo_ref):
        gv = lax.dot_general(c_ref[...], dm_ref[0], _DN["tn"], preferred_element_type=F32)
        delta, m2, v2 = _adam_math(w_ref[0], gv, m_ref[0], v_ref[0])
        go_ref[0] = gv
        d_ref[0] = delta
        mo_ref[0] = m2
        vo_ref[0] = v2

    spec = pl.BlockSpec((1, tr, tc), lambda l, i, j: (l, i, j))
    shp = jax.ShapeDtypeStruct(w.shape, F32)
    return pl.pallas_call(
        body, name=name, grid=(depth, r // tr, c // tc),
        in_specs=[spec] * 3 + [pl.BlockSpec((128, tr), lambda l, i, j: (0, i)),
                               pl.BlockSpec((1, 128, tc), lambda l, i, j: (l, 0, j))],
        out_specs=[spec] * 4, out_shape=[shp] * 4,
        compiler_params=_params(("parallel", "parallel", "parallel")),
    )(w, m, v, cact128, dmod128)


def _adamw_small(w, g, m, v, name):
    def body(w_ref, g_ref, m_ref, v_ref, d_ref, mo_ref, vo_ref):
        delta, m2, v2 = _adam_math(w_ref[...], g_ref[...], m_ref[...], v_ref[...])
        d_ref[...] = delta
        mo_ref[...] = m2
        vo_ref[...] = v2

    shp = jax.ShapeDtypeStruct(w.shape, F32)
    return pl.pallas_call(body, name=name, out_shape=[shp] * 3)(w, g, m, v)


def _sum_gathered(parts, name):
    nd, r, c = parts.shape

    def body(p_ref, o_ref):
        acc = p_ref[0]
        for e in range(1, nd):
            acc = acc + p_ref[e]
        o_ref[...] = acc

    return pl.pallas_call(body, name=name, out_shape=jax.ShapeDtypeStruct((r, c), F32))(parts)


def _flip(v, bit):
    return 1 - v if bit else v


def _all_gather_small(x, name):
    r, c = x.shape

    def body(x_ref, out_ref, send_sems, recv_sems, local_sem):
        mx, my, mc = lax.axis_index("x"), lax.axis_index("y"), lax.axis_index("c")
        me = 4 * mx + 2 * my + mc
        mine = pltpu.make_async_copy(x_ref, out_ref.at[me], local_sem)
        mine.start()
        sends = []
        for k in range(1, N_DEV):
            peer = (_flip(mx, k & 4), _flip(my, k & 2), _flip(mc, k & 1))
            cp = pltpu.make_async_remote_copy(src_ref=x_ref, dst_ref=out_ref.at[me], send_sem=send_sems.at[k - 1],
                                              recv_sem=recv_sems.at[k - 1], device_id=peer, device_id_type=MESH)
            cp.start()
            sends.append(cp)
        for k in range(1, N_DEV):
            peer = (_flip(mx, k & 4), _flip(my, k & 2), _flip(mc, k & 1))
            src = 4 * peer[0] + 2 * peer[1] + peer[2]
            pltpu.make_async_remote_copy(src_ref=x_ref, dst_ref=out_ref.at[src], send_sem=send_sems.at[k - 1],
                                         recv_sem=recv_sems.at[k - 1], device_id=peer,
                                         device_id_type=MESH).wait_recv()
        for cp in sends:
            cp.wait_send()
        mine.wait()

    return pl.pallas_call(
        body, name=name,
        out_shape=jax.ShapeDtypeStruct((N_DEV, r, c), x.dtype),
        in_specs=[pl.BlockSpec(memory_space=pltpu.VMEM)],
        out_specs=pl.BlockSpec(memory_space=pltpu.VMEM),
        scratch_shapes=[pltpu.SemaphoreType.DMA((N_DEV - 1,)), pltpu.SemaphoreType.DMA((N_DEV - 1,)),
                        pltpu.SemaphoreType.DMA],
        compiler_params=pltpu.CompilerParams(vmem_limit_bytes=VMEM_LIMIT),
    )(x)


_HBM = pl.BlockSpec(memory_space=pltpu.HBM)
_SEM = pl.BlockSpec(memory_space=pltpu.SEMAPHORE)
_ANY = pl.BlockSpec(memory_space=pl.ANY)
_EFFECT = pltpu.SideEffectType.DATAFLOW_SIDE_EFFECTING


def _in_hbm(a):
    return pltpu.with_memory_space_constraint(a, pltpu.HBM)


def _place_w_in(w, layer, chip1, name):
    _, d, fc = w.shape
    tr = _tile(d, 512)

    def body(c_ref, w_ref, o_ref):
        o_ref[...] = w_ref[...].astype(BF16)

    return pl.pallas_call(
        body, name=name,
        grid_spec=pltpu.PrefetchScalarGridSpec(
            num_scalar_prefetch=1, grid=(d // tr,),
            in_specs=[pl.BlockSpec((None, tr, fc), lambda i, c: (layer, i, 0))],
            out_specs=pl.BlockSpec((tr, fc), lambda i, c: (i, c[0]))),
        out_shape=jax.ShapeDtypeStruct((d, N_CHIP * fc), BF16),
        compiler_params=_params(("parallel",)),
    )(chip1, w)


def _place_w_out(w, layer, chip1, name):
    _, rc, dd = w.shape
    tc = _tile(dd, 1024)

    def body(c_ref, w_ref, o_ref):
        o_ref[...] = w_ref[...].astype(BF16)

    return pl.pallas_call(
        body, name=name,
        grid_spec=pltpu.PrefetchScalarGridSpec(
            num_scalar_prefetch=1, grid=(dd // tc,),
            in_specs=[pl.BlockSpec((None, rc, tc), lambda j, c: (layer, 0, j))],
            out_specs=pl.BlockSpec((rc, tc), lambda j, c: (c[0], j))),
        out_shape=jax.ShapeDtypeStruct((N_CHIP * rc, dd), BF16),
        compiler_params=_params(("parallel",)),
    )(chip1, w)


def _weight_region(ref, tensor, chip):
    if tensor == 0:
        fc = ref.shape[1] // N_CHIP
        return ref.at[:, pl.ds(pl.multiple_of(chip * fc, LANES), fc)]
    rc = ref.shape[0] // N_CHIP
    return ref.at[pl.ds(pl.multiple_of(chip * rc, 8), rc), :]


def _gather_copies(ref, tensor, send_sems, recv_sems, landing):
    mx, my, mc = lax.axis_index("x"), lax.axis_index("y"), lax.axis_index("c")
    mine = _weight_region(ref, tensor, 2 * mx + my)
    copies = []
    for j in range(1, N_CHIP):
        peer = (_flip(mx, j & 2), _flip(my, j & 1), mc)
        dst = _weight_region(ref, tensor, 2 * peer[0] + peer[1]) if landing else mine
        idx = 2 * (j - 1) + tensor
        copies.append(pltpu.make_async_remote_copy(src_ref=mine, dst_ref=dst, send_sem=send_sems.at[idx],
                                                   recv_sem=recv_sems.at[idx], device_id=peer, device_id_type=MESH))
    return copies


def _gather_start(fi, fo, dep, name):
    ns = 2 * (N_CHIP - 1)

    def body(fi_ref, fo_ref, dep_ref, send_sems, recv_sems, fi_thru, fo_thru, token):
        for tensor, ref in enumerate((fi_ref, fo_ref)):
            for cp in _gather_copies(ref, tensor, send_sems, recv_sems, landing=False):
                cp.start()
        token[...] = jnp.zeros_like(token)

    return pl.pallas_call(
        body, name=name,
        out_shape=(pltpu.SemaphoreType.DMA((ns,)), pltpu.SemaphoreType.DMA((ns,)),
                   pltpu.HBM(fi.shape, fi.dtype), pltpu.HBM(fo.shape, fo.dtype),
                   jax.ShapeDtypeStruct((8, LANES), F32)),
        in_specs=(_HBM, _HBM, _ANY),
        out_specs=(_SEM, _SEM, _HBM, _HBM, pl.BlockSpec(memory_space=pltpu.VMEM)),
        input_output_aliases={0: 2, 1: 3},
        compiler_params=pltpu.CompilerParams(has_side_effects=_EFFECT),
    )(_in_hbm(fi), _in_hbm(fo), dep)


def _gather_wait(send_sems, recv_sems, buf, tensor, after, name):
    def body(buf_ref, send_sems, recv_sems, after_ref, buf_out):
        for cp in _gather_copies(buf_ref, tensor, send_sems, recv_sems, landing=True):
            cp.wait_send()
            cp.wait_recv()

    return pl.pallas_call(
        body, name=name,
        out_shape=pltpu.HBM(buf.shape, buf.dtype),
        in_specs=(_HBM, _SEM, _SEM, _ANY), out_specs=_HBM,
        input_output_aliases={0: 0},
        compiler_params=pltpu.CompilerParams(has_side_effects=_EFFECT),
    )(buf, send_sems, recv_sems, after)


def _w_in_half(ref, half, chip):
    hr = ref.shape[0] // 2
    fc = ref.shape[1] // N_CHIP
    return ref.at[pl.ds(pl.multiple_of(half * hr, 8), hr), pl.ds(pl.multiple_of(chip * fc, LANES), fc)]


def _half_copies_ici(ref, send_sems, recv_sems, landing):
    mx, my, mc = lax.axis_index("x"), lax.axis_index("y"), lax.axis_index("c")
    mine = _w_in_half(ref, mc, 2 * mx + my)
    copies = []
    for j in range(1, N_CHIP):
        peer = (_flip(mx, j & 2), _flip(my, j & 1), mc)
        dst = _w_in_half(ref, mc, 2 * peer[0] + peer[1]) if landing else mine
        copies.append(pltpu.make_async_remote_copy(src_ref=mine, dst_ref=dst, send_sem=send_sems.at[j - 1],
                                                   recv_sem=recv_sems.at[j - 1], device_id=peer, device_id_type=MESH))
    return copies


def _half_copies_d2d(ref, send_sems, recv_sems, landing):
    mx, my, mc = lax.axis_index("x"), lax.axis_index("y"), lax.axis_index("c")
    sib = (mx, my, 1 - mc)
    copies = []
    for j in range(1, N_CHIP):
        other = 2 * _flip(mx, j & 2) + _flip(my, j & 1)
        src = _w_in_half(ref, mc, other)
        dst = _w_in_half(ref, 1 - mc, other) if landing else src
        copies.append(pltpu.make_async_remote_copy(src_ref=src, dst_ref=dst, send_sem=send_sems.at[j - 1],
                                                   recv_sem=recv_sems.at[j - 1], device_id=sib, device_id_type=MESH))
    return copies


def _w_out_copies(ref, send_sems, recv_sems, landing):
    mx, my, mc = lax.axis_index("x"), lax.axis_index("y"), lax.axis_index("c")
    mine = _weight_region(ref, 1, 2 * mx + my)
    copies = []
    for j in range(1, N_CHIP):
        peer = (_flip(mx, j & 2), _flip(my, j & 1), mc)
        dst = _weight_region(ref, 1, 2 * peer[0] + peer[1]) if landing else mine
        copies.append(pltpu.make_async_remote_copy(src_ref=mine, dst_ref=dst, send_sem=send_sems.at[j - 1],
                                                   recv_sem=recv_sems.at[j - 1], device_id=peer, device_id_type=MESH))
    return copies


def _copies_start(buf, copies, dep, name):
    ns = N_CHIP - 1

    def body(buf_ref, dep_ref, send_sems, recv_sems, buf_thru, token):
        for cp in copies(buf_ref, send_sems, recv_sems, landing=False):
            cp.start()
        token[...] = jnp.zeros_like(token)

    return pl.pallas_call(
        body, name=name,
        out_shape=(pltpu.SemaphoreType.DMA((ns,)), pltpu.SemaphoreType.DMA((ns,)),
                   pltpu.HBM(buf.shape, buf.dtype), jax.ShapeDtypeStruct((8, LANES), F32)),
        in_specs=(_HBM, _ANY),
        out_specs=(_SEM, _SEM, _HBM, pl.BlockSpec(memory_space=pltpu.VMEM)),
        input_output_aliases={0: 2},
        compiler_params=pltpu.CompilerParams(has_side_effects=_EFFECT),
    )(_in_hbm(buf), dep)


def _copies_wait(send_sems, recv_sems, buf, copies, after, name):
    def body(buf_ref, send_sems, recv_sems, after_ref, buf_out):
        for cp in copies(buf_ref, send_sems, recv_sems, landing=True):
            cp.wait_send()
            cp.wait_recv()

    return pl.pallas_call(
        body, name=name,
        out_shape=pltpu.HBM(buf.shape, buf.dtype),
        in_specs=(_HBM, _SEM, _SEM, _ANY), out_specs=_HBM,
        input_output_aliases={0: 0},
        compiler_params=pltpu.CompilerParams(has_side_effects=_EFFECT),
    )(buf, send_sems, recv_sems, after)


def _scatter_copies(gi_ref, go_ref, pi_ref, po_ref, send_sems, recv_sems):
    mx, my, mc = lax.axis_index("x"), lax.axis_index("y"), lax.axis_index("c")
    hr, fc = pi_ref.shape[1:]
    ro = po_ref.shape[1]
    copies = []
    for k in range(1, N_DEV):
        peer = (_flip(mx, k & 4), _flip(my, k & 2), _flip(mc, k & 1))
        pchip = 2 * peer[0] + peer[1]
        src = (gi_ref.at[pl.ds(pl.multiple_of(peer[2] * hr, 8), hr), pl.ds(pl.multiple_of(pchip * fc, LANES), fc)],
               go_ref.at[pl.ds(pl.multiple_of((2 * pchip + peer[2]) * ro, 8), ro), :])
        dst = (pi_ref.at[k - 1], po_ref.at[k - 1])
        for t in range(2):
            idx = 2 * (k - 1) + t
            copies.append(pltpu.make_async_remote_copy(src_ref=src[t], dst_ref=dst[t], send_sem=send_sems.at[idx],
                                                       recv_sem=recv_sems.at[idx], device_id=peer,
                                                       device_id_type=MESH))
    return copies


def _scatter_start(gi, go, name):
    d, f = gi.shape
    dd = go.shape[1]
    ns = 2 * (N_DEV - 1)
    pi = lax.empty((N_DEV - 1, d // 2, f // N_CHIP), BF16)
    po = lax.empty((N_DEV - 1, d // N_DEV, dd), BF16)

    def body(gi_ref, go_ref, pi_ref, po_ref, send_sems, recv_sems, gi_thru, go_thru, pi_thru, po_thru, token):
        for cp in _scatter_copies(gi_ref, go_ref, pi_ref, po_ref, send_sems, recv_sems):
            cp.start()
        token[...] = jnp.zeros_like(token)

    return pl.pallas_call(
        body, name=name,
        out_shape=(pltpu.SemaphoreType.DMA((ns,)), pltpu.SemaphoreType.DMA((ns,)),
                   pltpu.HBM(gi.shape, gi.dtype), pltpu.HBM(go.shape, go.dtype),
                   pltpu.HBM(pi.shape, pi.dtype), pltpu.HBM(po.shape, po.dtype),
                   jax.ShapeDtypeStruct((8, LANES), F32)),
        in_specs=(_HBM, _HBM, _HBM, _HBM),
        out_specs=(_SEM, _SEM, _HBM, _HBM, _HBM, _HBM, pl.BlockSpec(memory_space=pltpu.VMEM)),
        input_output_aliases={0: 2, 1: 3, 2: 4, 3: 5},
        compiler_params=pltpu.CompilerParams(has_side_effects=_EFFECT),
    )(_in_hbm(gi), _in_hbm(go), _in_hbm(pi), _in_hbm(po))


def _scatter_wait(send_sems, recv_sems, gi, go, pi, po, after, name):
    def body(gi_ref, go_ref, pi_ref, po_ref, send_sems, recv_sems, *rest):
        for cp in _scatter_copies(gi_ref, go_ref, pi_ref, po_ref, send_sems, recv_sems):
            cp.wait_send()
            cp.wait_recv()

    return pl.pallas_call(
        body, name=name,
        out_shape=tuple(pltpu.HBM(a.shape, a.dtype) for a in (gi, go, pi, po)),
        in_specs=(_HBM, _HBM, _HBM, _HBM, _SEM, _SEM) + (_ANY,) * len(after), out_specs=(_HBM, _HBM, _HBM, _HBM),
        input_output_aliases={0: 0, 1: 1, 2: 2, 3: 3},
        compiler_params=pltpu.CompilerParams(has_side_effects=_EFFECT),
    )(gi, go, pi, po, send_sems, recv_sems, *after)


def _sum_into(buf, g, parts, where2, layer, row_blocks, dep, name):
    depth, r2, c = buf.shape
    r = r2 // 2
    tr, tc = _tile(r, 256), _tile(c, 1024)
    nr, nc = r // tr, c // tc
    col_blocks = (g.shape[1] // c) > 1

    def body(w_ref, buf_ref, g_ref, p_ref, dep_ref, o_ref):
        acc = g_ref[...].astype(F32)
        for e in range(N_DEV - 1):
            acc = acc + p_ref[e].astype(F32)
        o_ref[...] = acc

    return pl.pallas_call(
        body, name=name,
        grid_spec=pltpu.PrefetchScalarGridSpec(
            num_scalar_prefetch=1, grid=(nr, nc),
            in_specs=[_ANY,
                      pl.BlockSpec((tr, tc), lambda i, j, w: (row_blocks(w, nr) + i, (w[1] * nc if col_blocks else 0) + j)),
                      pl.BlockSpec((N_DEV - 1, tr, tc), lambda i, j, w: (0, i, j)), _ANY],
            out_specs=pl.BlockSpec((None, tr, tc), lambda i, j, w: (layer, w[0] * nr + i, j))),
        out_shape=jax.ShapeDtypeStruct(buf.shape, F32),
        input_output_aliases={1: 0},
        compiler_params=_params(("parallel", "parallel")),
    )(where2, buf, g, parts, dep)


def _exchange_halves(b_in, b_out, first, count, name):
    r_in = b_in.shape[1]
    r_out = b_out.shape[1]

    def body(bi_ref, bo_ref, oi_ref, oo_ref, send_sems, recv_sems):
        mx, my, mc = lax.axis_index("x"), lax.axis_index("y"), lax.axis_index("c")
        sib = (mx, my, 1 - mc)

        def half(ref, l, rows, which):
            return ref.at[l, pl.ds(pl.multiple_of(which * (rows // 2), 8), rows // 2), :]

        copies = []
        for l in range(first, first + count):
            for t, (src, dst, rows) in enumerate(((bi_ref, oi_ref, r_in), (bo_ref, oo_ref, r_out))):
                idx = 2 * (l - first) + t
                kw = dict(send_sem=send_sems.at[idx], recv_sem=recv_sems.at[idx], device_id=sib, device_id_type=MESH)
                cp = pltpu.make_async_remote_copy(src_ref=half(src, l, rows, mc), dst_ref=half(dst, l, rows, mc), **kw)
                cp.start()
                copies.append((cp, pltpu.make_async_remote_copy(src_ref=half(src, l, rows, mc),
                                                                dst_ref=half(dst, l, rows, 1 - mc), **kw)))
        for cp, landed in copies:
            landed.wait_recv()
        for cp, landed in copies:
            cp.wait_send()

    ns = 2 * count
    return pl.pallas_call(
        body, name=name,
        out_shape=[jax.ShapeDtypeStruct(b_in.shape, F32), jax.ShapeDtypeStruct(b_out.shape, F32)],
        in_specs=[_ANY, _ANY], out_specs=[_ANY, _ANY],
        input_output_aliases={0: 0, 1: 1},
        scratch_shapes=[pltpu.SemaphoreType.DMA((ns,)), pltpu.SemaphoreType.DMA((ns,))],
    )(b_in, b_out)


def _rows8(v):
    return jnp.pad(v, ((0, 8 - v.shape[0]), (0, 0)))


def kernel(x, c, norm_gain, w_ada, b_ada, w_in, w_out, ret_decay_logit_f, ret_decay_logit_b, final_gain, loss_target, m_norm_gain, m_w_ada, m_b_ada, m_w_in, m_w_out, m_ret_decay_logit_f, m_ret_decay_logit_b, m_final_gain, v_norm_gain, v_w_ada, v_b_ada, v_w_in, v_w_out, v_ret_decay_logit_f, v_ret_decay_logit_b, v_final_gain):
    cfg = _Cfg()
    depth, d = norm_gain.shape
    mx, my, mc = lax.axis_index("x"), lax.axis_index("y"), lax.axis_index("c")
    me = 4 * mx + 2 * my + mc
    chip = 2 * mx + my
    x0 = x[0]
    tgt = loss_target[0]
    ada_cols = w_ada.shape[2]

    chip1 = jnp.reshape(chip, (1,)).astype(jnp.int32)
    where2 = jnp.stack([mc, chip]).astype(jnp.int32)

    def start_gather(l, dep):
        return _gather_start(_place_w_in(w_in, l, chip1, f"place_w_in_{l}"),
                             _place_w_out(w_out, l, chip1, f"place_w_out_{l}"), dep, f"gather_start_{l}")

    c_all = _all_gather_small(_rows8(c), "gather_c")[:, 0, :]
    cact = _silu(c_all)
    ici = _copies_start(_place_w_in(w_in, 0, chip1, "place_w_in_0"), _half_copies_ici, c_all, "gather0_ici_start")
    cact16 = jnp.pad(cact, ((0, 8), (0, 0))) + ici[3][0:1, 0:1]
    mod_part = _ada_fwd(cact16.astype(BF16), w_ada, "ada_fwd")
    mod_all = _all_gather_small(mod_part.reshape(depth * 16, ada_cols), "gather_mod")
    mod_all = mod_all.reshape(N_CHIP, 2, depth, 16, ada_cols)[:, 0]
    mod_mine = lax.dynamic_index_in_dim(mod_all, me, axis=2, keepdims=False)
    mod = jnp.transpose(mod_mine, (1, 0, 2)).reshape(depth, 3, d)
    bias = b_ada.reshape(depth, 3, d)
    fi0 = _copies_wait(ici[0], ici[1], ici[2], _half_copies_ici, mod, "gather0_ici_wait")
    d2d = _copies_start(fi0, _half_copies_d2d, mod, "gather0_d2d_start")
    out0 = _copies_start(_place_w_out(w_out, 0, chip1, "place_w_out_0"), _w_out_copies, d2d[3], "gather0_out_start")
    w_in_0 = _copies_wait(d2d[0], d2d[1], d2d[2], _half_copies_d2d, out0[3], "gather0_d2d_wait")

    slopes = jnp.exp2(-8.0 * (jnp.arange(cfg.ha, dtype=F32) + 1.0) / cfg.ha)
    lg_f = jax.nn.log_sigmoid(ret_decay_logit_f)
    lg_b = jax.nn.log_sigmoid(ret_decay_logit_b)

    saved = []
    w_full = []
    h_x = x0
    pending = None
    for l in range(depth):
        if l == 0:
            w_in_l = w_in_0
        else:
            send_sems, recv_sems, fi, fo, _ = pending
            w_in_l = _gather_wait(send_sems, recv_sems, fi, 0, h_x, f"gather_wait_in_{l}")
        g8 = _rows8(norm_gain[l:l + 1])
        if l + 1 < depth:
            pending = start_gather(l + 1, w_in_l)
            g8 = g8 + pending[4][0:1, 0:1]
        mod3, b3 = _rows8(mod[l]), _rows8(bias[l])
        hb = _norm_mod_fwd(h_x, g8, mod3, b3, f"norm_mod_fwd_{l}")
        proj = _matmul(hb, w_in_l, "nn", BF16, f"in_proj_{l}")
        oa, lse, y = _attn_fwd(cfg, proj, slopes, f"attn_fwd_{l}")
        lgs = jnp.stack([lg_f[l], lg_b[l]])
        oret, y = _ret_fwd(cfg, proj, lgs, y, f"ret_fwd_{l}")
        if l == 0:
            w_out_l = _copies_wait(out0[0], out0[1], out0[2], _w_out_copies, y, "gather0_out_wait")
        else:
            w_out_l = _gather_wait(send_sems, recv_sems, fo, 1, y, f"gather_wait_out_{l}")
        w_full.append((w_in_l, w_out_l))
        x_next, out = _out_proj_fwd(y, w_out_l, h_x, mod3, b3, f"out_proj_{l}")
        saved.append((h_x, hb, proj, oret, y, oa, lse, out, g8, mod3, b3, lgs))
        h_x = x_next

    dx, loss8, fin_acc = _final_loss(h_x, tgt, _rows8(final_gain[None]), "final_loss")

    landed = [None] * depth
    d_mod, d_gain, d_lg = [None] * depth, [None] * depth, [None] * depth
    in_flight = None
    for l in reversed(range(depth)):
        x_l, hb, proj, oret, y, oa, lse, out, g8, mod3, b3, lgs = saved[l]
        w_in_l, w_out_l = w_full[l]
        douts, gate_acc = _out_proj_bwd_prep(dx, out, mod3, b3, f"out_proj_bwd_prep_{l}")
        dy = _matmul(douts, w_out_l, "nt", BF16, f"out_proj_dy_{l}")
        g_out_l = _matmul(y, douts, "tn", BF16, f"out_proj_dw_{l}", tk=4096)
        dproj = _attn_bwd(cfg, proj, slopes, dy, oa, lse, f"attn_bwd_{l}")
        dproj, dlg = _ret_bwd(cfg, proj, lgs, dy, oret, dproj, f"ret_bwd_{l}")
        g_in_l = _matmul(hb, dproj, "tn", BF16, f"in_proj_dw_{l}", tk=4096)
        started = _scatter_start(g_in_l, g_out_l, f"scatter_start_{l}")
        dh = _matmul(dproj, w_in_l, "nt", F32, f"in_proj_dh_{l}", tk=3584)
        g8 = g8 + started[-1][0:1, 0:1]
        dx, nm_acc = _norm_mod_bwd(dh, x_l, dx, g8, mod3, b3, f"norm_mod_bwd_{l}")
        d_mod[l] = jnp.concatenate([nm_acc[0], nm_acc[1], gate_acc[0]])
        d_gain[l] = nm_acc[2]
        d_lg[l] = dlg[:, 0:2, 0]
        if in_flight is not None:
            landed[l + 1] = _scatter_wait(*in_flight[:-1], (dx,), f"scatter_wait_{l + 1}")
        in_flight = started

    gw_in = lax.empty(w_in.shape, F32)
    gw_out = lax.empty(w_out.shape, F32)
    res_in = res_out = None
    for first, count in ((1, depth - 1), (0, 1)):
        if first == 0:
            after = (dx,) if res_out is None else (res_in[1], res_out[1])
            landed[0] = _scatter_wait(*in_flight[:-1], after, "scatter_wait_0")
        if count == 0:
            continue
        dep = in_flight[-1]
        for l in range(first, first + count):
            gi, go, pi, po = landed[l]
            gw_in = _sum_into(gw_in, gi, pi, where2, l, lambda w, nr: w[0] * nr, dep, f"sum_w_in_{l}")
            gw_out = _sum_into(gw_out, go, po, where2, l, lambda w, nr: (2 * w[1] + w[0]) * nr, dep,
                               f"sum_w_out_{l}")
        gw_in, gw_out = _exchange_halves(gw_in, gw_out, first, count, f"exchange_halves_{first}")
        res_in = _adamw_big(w_in, gw_in, m_w_in, v_w_in, first, count, res_in, f"adamw_w_in_{first}")
        res_out = _adamw_big(w_out, gw_out, m_w_out, v_w_out, first, count, res_out, f"adamw_w_out_{first}")
    grad_w_in, delta_w_in, new_m_w_in, new_v_w_in = res_in
    grad_w_out, delta_w_out, new_m_w_out, new_v_w_out = res_out

    dmod_mine = jnp.stack(d_mod)
    dmod_gathered = _all_gather_small(_rows8(dmod_mine), "gather_dmod")
    dmod_all = dmod_gathered[:, :depth, :]
    grad_b_ada = _sum_gathered(dmod_gathered, "sum_b_ada")[:depth]
    dmod_cols = lax.dynamic_slice_in_dim(dmod_all, chip * ada_cols, ada_cols, axis=2)
    dmod128 = jnp.pad(jnp.transpose(dmod_cols, (1, 0, 2)), ((0, 0), (0, 120), (0, 0))).astype(BF16)
    cact128 = jnp.pad(cact, ((0, 120), (0, 0))).astype(BF16)
    grad_w_ada, delta_w_ada, new_m_w_ada, new_v_w_ada = _adamw_ada(w_ada, m_w_ada, v_w_ada, cact128, dmod128,
                                                                  "adamw_w_ada")

    dlg_all = jnp.stack(d_lg)
    sig_f = jax.nn.sigmoid(-ret_decay_logit_f)
    sig_b = jax.nn.sigmoid(-ret_decay_logit_b)
    nlg = depth * cfg.hr
    pack = jnp.zeros((8, d), F32)
    for l in range(depth):
        pack = pack.at[l].set(d_gain[l])
    pack = pack.at[depth].set(fin_acc[0])
    pack = pack.at[depth + 1, 0].set(loss8[0, 0])
    pack = pack.at[depth + 1, LANES:LANES + nlg].set((dlg_all[:, :, 0] * sig_f).reshape(-1))
    pack = pack.at[depth + 1, 2 * LANES:2 * LANES + nlg].set((dlg_all[:, :, 1] * sig_b).reshape(-1))
    tot = _sum_gathered(_all_gather_small(pack, "gather_small"), "sum_small")
    grad_norm_gain = tot[:depth]
    grad_final_gain = tot[depth]
    loss = tot[depth + 1, 0]
    grad_lf = tot[depth + 1, LANES:LANES + nlg].reshape(depth, cfg.hr)
    grad_lb = tot[depth + 1, 2 * LANES:2 * LANES + nlg].reshape(depth, cfg.hr)

    d_ng, m_ng, v_ng = _adamw_small(norm_gain, grad_norm_gain, m_norm_gain, v_norm_gain, "adamw_norm_gain")
    d_ba, m_ba, v_ba = _adamw_small(b_ada, grad_b_ada, m_b_ada, v_b_ada, "adamw_b_ada")
    d_lf, m_lf, v_lf = _adamw_small(ret_decay_logit_f, grad_lf, m_ret_decay_logit_f, v_ret_decay_logit_f, "adamw_lf")
    d_lb, m_lb, v_lb = _adamw_small(ret_decay_logit_b, grad_lb, m_ret_decay_logit_b, v_ret_decay_logit_b, "adamw_lb")
    d_fg, m_fg, v_fg = _adamw_small(final_gain[None], grad_final_gain[None], m_final_gain[None], v_final_gain[None],
                                    "adamw_final_gain")

    return (loss, dx[None],
            grad_norm_gain, grad_w_ada, grad_b_ada, grad_w_in, grad_w_out, grad_lf, grad_lb, grad_final_gain,
            d_ng, delta_w_ada, d_ba, delta_w_in, delta_w_out, d_lf, d_lb, d_fg[0],
            m_ng, new_m_w_ada, m_ba, new_m_w_in, new_m_w_out, m_lf, m_lb, m_fg[0],
            v_ng, new_v_w_ada, v_ba, new_v_w_in, new_v_w_out, v_lf, v_lb, v_fg[0])
```

```python
import jax
import jax.numpy as jnp
from jax import lax
from jax.experimental import pallas as pl
from jax.experimental.pallas import tpu as pltpu

F32 = jnp.float32
BF16 = jnp.bfloat16

D_MODEL = 2048
SEQ = 4096
DEPTH = 4
HEAD_DIM = 128
DILATIONS = (1, 4, 16)
RADIUS = 64
N_HEADS_RET = 4
RET_QK = 128
RET_V = 256
RET_CHUNK = 256
NORM_EPS = 1e-6
MASK_VALUE = -1e30
N_DEV = 8
N_CHIP = 4
LANES = 128
VMEM_LIMIT = 56 * 1024 * 1024

ADAM_LR = 0.001
ADAM_B1 = 0.9
ADAM_B2 = 0.999
ADAM_EPS = 1e-08
ADAM_WD = 0.01
ADAM_STEP = 10

MESH = pl.DeviceIdType.MESH


class _Cfg:
    def __init__(self):
        self.d = D_MODEL
        self.s = SEQ
        self.aw = D_MODEL // 2
        self.ha = self.aw // HEAD_DIM
        self.rw = D_MODEL // 2
        self.hr = N_HEADS_RET
        self.rqk = self.hr * RET_QK
        self.f = 4 * self.aw + 2 * self.rqk + 2 * self.rw
        self.qa, self.ka, self.va, self.za = 0, self.aw, 2 * self.aw, 3 * self.aw
        self.qr = 4 * self.aw
        self.kr = self.qr + self.rqk
        self.vr = self.kr + self.rqk
        self.zr = self.vr + self.rw
        assert self.rw == self.hr * RET_V


def _tile(n, pref):
    t = min(n, pref)
    while n % t or t % LANES:
        t -= LANES
    return t


def _params(dims=None):
    return pltpu.CompilerParams(dimension_semantics=dims, vmem_limit_bytes=VMEM_LIMIT)


def _silu(z):
    return z * jax.nn.sigmoid(z)


def _dsilu(z):
    sg = jax.nn.sigmoid(z)
    return sg * (1.0 + z * (1.0 - sg))


_DN = {"nn": (((1,), (0,)), ((), ())), "nt": (((1,), (1,)), ((), ())), "tn": (((0,), (0,)), ((), ()))}


def _matmul(a, b, mode, out_dtype, name, tm=1024, tn=1024, tk=2048):
    if mode == "tn":
        kk, m = a.shape
    else:
        m, kk = a.shape
    n = b.shape[0] if mode == "nt" else b.shape[1]
    tm, tn, tk = _tile(m, tm), _tile(n, tn), _tile(kk, tk)
    nk = kk // tk
    a_spec = (pl.BlockSpec((tk, tm), lambda i, j, k: (k, i)) if mode == "tn"
              else pl.BlockSpec((tm, tk), lambda i, j, k: (i, k)))
    b_spec = (pl.BlockSpec((tn, tk), lambda i, j, k: (j, k)) if mode == "nt"
              else pl.BlockSpec((tk, tn), lambda i, j, k: (k, j)))
    dn = _DN[mode]

    def body(a_ref, b_ref, o_ref, *acc):
        p = lax.dot_general(a_ref[...], b_ref[...], dn, preferred_element_type=F32)
        if nk == 1:
            o_ref[...] = p.astype(out_dtype)
            return
        acc_ref, = acc
        k = pl.program_id(2)

        @pl.when(k == 0)
        def _():
            acc_ref[...] = p

        @pl.when(k > 0)
        def _():
            acc_ref[...] += p

        @pl.when(k == nk - 1)
        def _():
            o_ref[...] = acc_ref[...].astype(out_dtype)

    return pl.pallas_call(
        body, name=name, grid=(m // tm, n // tn, nk),
        in_specs=[a_spec, b_spec],
        out_specs=pl.BlockSpec((tm, tn), lambda i, j, k: (i, j)),
        out_shape=jax.ShapeDtypeStruct((m, n), out_dtype),
        scratch_shapes=[pltpu.VMEM((tm, tn), F32)] if nk > 1 else [],
        compiler_params=_params(("parallel", "parallel", "arbitrary")),
    )(a, b)


def _out_proj_fwd(y, w_out, x, mod3, b3, name):
    s, kk = y.shape
    d = w_out.shape[1]
    tm, tn = _tile(s, 256), d

    def body(y_ref, w_ref, x_ref, m_ref, b_ref, xn_ref, o_ref):
        out = jnp.dot(y_ref[...], w_ref[...], preferred_element_type=F32)
        gate = m_ref[2:3, :] + b_ref[2:3, :]
        xn_ref[...] = x_ref[...] + gate * out
        o_ref[...] = out.astype(BF16)

    vec = pl.BlockSpec((8, tn), lambda i, j: (0, j))
    return pl.pallas_call(
        body, name=name, grid=(s // tm, d // tn),
        in_specs=[pl.BlockSpec((tm, kk), lambda i, j: (i, 0)), pl.BlockSpec((kk, tn), lambda i, j: (0, j)),
                  pl.BlockSpec((tm, tn), lambda i, j: (i, j)), vec, vec],
        out_specs=[pl.BlockSpec((tm, tn), lambda i, j: (i, j)), pl.BlockSpec((tm, tn), lambda i, j: (i, j))],
        out_shape=[jax.ShapeDtypeStruct((s, d), F32), jax.ShapeDtypeStruct((s, d), BF16)],
        compiler_params=_params(("parallel", "parallel")),
    )(y, w_out, x, mod3, b3)


def _norm_mod_fwd(x, g8, mod3, b3, name):
    s, d = x.shape
    tr = _tile(s, 512)

    def body(x_ref, g_ref, m_ref, b_ref, h_ref):
        xv = x_ref[...]
        r = lax.rsqrt(jnp.mean(xv * xv, axis=-1, keepdims=True) + NORM_EPS)
        shift = m_ref[0:1, :] + b_ref[0:1, :]
        scale = m_ref[1:2, :] + b_ref[1:2, :]
        h_ref[...] = ((xv * r * g_ref[0:1, :]) * (1.0 + scale) + shift).astype(BF16)

    vec = pl.BlockSpec((8, d), lambda i: (0, 0))
    return pl.pallas_call(
        body, name=name, grid=(s // tr,),
        in_specs=[pl.BlockSpec((tr, d), lambda i: (i, 0)), vec, vec, vec],
        out_specs=pl.BlockSpec((tr, d), lambda i: (i, 0)),
        out_shape=jax.ShapeDtypeStruct((s, d), BF16),
        compiler_params=_params(("parallel",)),
    )(x, g8, mod3, b3)


def _norm_mod_bwd(dh, x, dx_next, g8, mod3, b3, name):
    s, d = x.shape
    tr = _tile(s, 256)

    def body(dh_ref, x_ref, dn_ref, g_ref, m_ref, b_ref, dx_ref, acc_ref):
        @pl.when(pl.program_id(0) == 0)
        def _():
            acc_ref[...] = jnp.zeros_like(acc_ref)

        xv = x_ref[...]
        dh_v = dh_ref[...]
        g = g_ref[0:1, :]
        r = lax.rsqrt(jnp.mean(xv * xv, axis=-1, keepdims=True) + NORM_EPS)
        xn = xv * r
        scale1 = 1.0 + m_ref[1:2, :] + b_ref[1:2, :]
        dhs = dh_v * scale1
        dxn = dhs * g
        dx_ref[...] = dn_ref[...] + r * (dxn - xn * jnp.mean(dxn * xn, axis=-1, keepdims=True))
        acc_ref[0:1, :] += jnp.sum(dh_v, axis=0, keepdims=True)
        acc_ref[1:2, :] += jnp.sum(dh_v * (xn * g), axis=0, keepdims=True)
        acc_ref[2:3, :] += jnp.sum(dhs * xn, axis=0, keepdims=True)

    vec = pl.BlockSpec((8, d), lambda i: (0, 0))
    row = pl.BlockSpec((tr, d), lambda i: (i, 0))
    return pl.pallas_call(
        body, name=name, grid=(s // tr,),
        in_specs=[row, row, row, vec, vec, vec],
        out_specs=[row, vec],
        out_shape=[jax.ShapeDtypeStruct((s, d), F32), jax.ShapeDtypeStruct((8, d), F32)],
        compiler_params=_params(("arbitrary",)),
    )(dh, x, dx_next, g8, mod3, b3)


def _final_loss(x, tgt, g8, name):
    s, d = x.shape
    tr = _tile(s, 256)

    def body(x_ref, t_ref, g_ref, dx_ref, loss_ref, acc_ref):
        @pl.when(pl.program_id(0) == 0)
        def _():
            acc_ref[...] = jnp.zeros_like(acc_ref)
            loss_ref[...] = jnp.zeros_like(loss_ref)

        xv = x_ref[...]
        g = g_ref[0:1, :]
        r = lax.rsqrt(jnp.mean(xv * xv, axis=-1, keepdims=True) + NORM_EPS)
        xn = xv * r
        err = xn * g - t_ref[...]
        loss_ref[...] += 0.5 * jnp.sum(jnp.mean(err * err, axis=-1, keepdims=True), axis=0, keepdims=True)
        dy = err * (1.0 / d)
        acc_ref[0:1, :] += jnp.sum(dy * xn, axis=0, keepdims=True)
        dxn = dy * g
        dx_ref[...] = r * (dxn - xn * jnp.mean(dxn * xn, axis=-1, keepdims=True))

    vec = pl.BlockSpec((8, d), lambda i: (0, 0))
    row = pl.BlockSpec((tr, d), lambda i: (i, 0))
    return pl.pallas_call(
        body, name=name, grid=(s // tr,),
        in_specs=[row, row, vec],
        out_specs=[row, pl.BlockSpec((8, LANES), lambda i: (0, 0)), vec],
        out_shape=[jax.ShapeDtypeStruct((s, d), F32), jax.ShapeDtypeStruct((8, LANES), F32),
                   jax.ShapeDtypeStruct((8, d), F32)],
        compiler_params=_params(("arbitrary",)),
    )(x, tgt, g8)


def _out_proj_bwd_prep(dxn, out, mod3, b3, name):
    s, d = dxn.shape
    tr = _tile(s, 512)

    def body(dx_ref, o_ref, m_ref, b_ref, do_ref, acc_ref):
        @pl.when(pl.program_id(0) == 0)
        def _():
            acc_ref[...] = jnp.zeros_like(acc_ref)

        dxv = dx_ref[...]
        gate = m_ref[2:3, :] + b_ref[2:3, :]
        do_ref[...] = (gate * dxv).astype(BF16)
        acc_ref[0:1, :] += jnp.sum(dxv * o_ref[...].astype(F32), axis=0, keepdims=True)

    vec = pl.BlockSpec((8, d), lambda i: (0, 0))
    row = pl.BlockSpec((tr, d), lambda i: (i, 0))
    return pl.pallas_call(
        body, name=name, grid=(s // tr,),
        in_specs=[row, row, vec, vec],
        out_specs=[row, vec],
        out_shape=[jax.ShapeDtypeStruct((s, d), BF16), jax.ShapeDtypeStruct((8, d), F32)],
        compiler_params=_params(("arbitrary",)),
    )(dxn, out, mod3, b3)


_COPY_ROWS = 512


_ATTN_WIN = 384


def _attn_geometry(cfg, dil):
    sub = cfg.s // dil
    if sub <= _ATTN_WIN:
        return sub, sub, sub
    return sub, _ATTN_WIN - 2 * RADIUS, _ATTN_WIN


_N_SHIFTS = 3


def _attn_rows(dil, r, i, sub, bq, win):
    margin = (win - bq) // 2
    ws = jnp.clip(i * bq - margin, 0, sub - win)
    shift = (i * bq - ws) // margin if margin else 0
    if dil == 1:
        return (shift, pl.ds(pl.multiple_of(i * bq, bq), bq), pl.ds(pl.multiple_of(ws, RADIUS), win))
    return (shift, pl.ds(r + i * (bq * dil), bq, stride=dil), pl.ds(r + ws * dil, win, stride=dil))


def _fill_bias_tables(cfg, bias_ref, slope):
    for pattern, dil in enumerate(DILATIONS):
        _, bq, win = _attn_geometry(cfg, dil)
        margin = (win - bq) // 2
        rel0 = lax.broadcasted_iota(jnp.int32, (bq, win), 1) - lax.broadcasted_iota(jnp.int32, (bq, win), 0)
        for shift in range(_N_SHIFTS if margin else 1):
            arel = jnp.abs(rel0 - shift * margin)
            bias_ref[pattern * _N_SHIFTS + shift, 0:bq, 0:win] = jnp.where(
                arel <= RADIUS, -(slope * dil) * arel.astype(F32), MASK_VALUE)


def _bias_scratch():
    return pltpu.VMEM((len(DILATIONS) * _N_SHIFTS, _ATTN_WIN - 2 * RADIUS, _ATTN_WIN), F32)


def _attn_scores(cfg, bias_ref, q, kw, pattern, shift):
    bq, win = q.shape[0], kw.shape[0]
    return (lax.dot_general(q, kw, _DN["nt"], preferred_element_type=F32)
            + bias_ref[pattern * _N_SHIFTS + shift, 0:bq, 0:win])


def _for_each_group(cfg, group, size):
    for pattern, dil in reversed(list(enumerate(DILATIONS))):
        first = pattern == len(DILATIONS) - 1
        sub, bq, _ = _attn_geometry(cfg, dil)
        nblk = sub // bq
        per = min(size, nblk)
        for r in range(dil):
            if nblk == per:
                group(pattern, dil, first, [(r, i) for i in range(nblk)])
            else:
                def step(g, carry, pattern=pattern, dil=dil, first=first, r=r, per=per):
                    group(pattern, dil, first, [(r, g * per + j) for j in range(per)])
                    return carry

                lax.fori_loop(0, nblk // per, step, 0)


def _attn_fwd(cfg, proj, slopes, name):
    s = cfg.s
    scale = HEAD_DIM ** -0.5

    def body(sl_ref, q_ref, k_ref, v_ref, z_ref, o_ref, lse_ref, y_ref, qf, kf, vf, acc, m_s, l_s, bias_s):
        slope = sl_ref[pl.program_id(0)]

        def to_f32(i, carry):
            rows = pl.ds(pl.multiple_of(i * _COPY_ROWS, _COPY_ROWS), _COPY_ROWS)
            qf[rows, :] = q_ref[rows, :].astype(F32) * scale
            kf[rows, :] = k_ref[rows, :].astype(F32)
            vf[rows, :] = v_ref[rows, :].astype(F32)
            return carry

        lax.fori_loop(0, s // _COPY_ROWS, to_f32, 0)

        _fill_bias_tables(cfg, bias_s, slope)

        def group(pattern, dil, first, blocks):
            sub, bq, win = _attn_geometry(cfg, dil)
            rep = win // HEAD_DIM
            work = []
            for r, i in blocks:
                shift, qrows, krows = _attn_rows(dil, r, i, sub, bq, win)
                old = None if first else (m_s[qrows, :], l_s[qrows, :], acc[qrows, :])
                work.append((shift, qrows, qf[qrows, :].astype(BF16), kf[krows, :].astype(BF16),
                             vf[krows, :].astype(BF16), old))
            new = []
            for shift, qrows, q, kw, vw, old in work:
                sc = _attn_scores(cfg, bias_s, q, kw, pattern, shift)
                m_blk = jnp.max(sc, axis=-1, keepdims=True)
                if first:
                    m_new = jnp.broadcast_to(m_blk, (bq, HEAD_DIM))
                    p = jnp.exp(sc - m_blk)
                    l_new = jnp.broadcast_to(jnp.sum(p, axis=-1, keepdims=True), (bq, HEAD_DIM))
                    a_new = jnp.dot(p.astype(BF16), vw, preferred_element_type=F32)
                else:
                    m_old, l_old, a_old = old
                    m_new = jnp.maximum(m_old, m_blk)
                    alpha = jnp.exp(m_old - m_new)
                    p = jnp.exp(sc - jnp.tile(m_new, (1, rep)))
                    l_new = alpha * l_old + jnp.sum(p, axis=-1, keepdims=True)
                    a_new = alpha * a_old + jnp.dot(p.astype(BF16), vw, preferred_element_type=F32)
                new.append((qrows, m_new, l_new, a_new))
            for qrows, m_new, l_new, a_new in new:
                m_s[qrows, :] = m_new
                l_s[qrows, :] = l_new
                acc[qrows, :] = a_new

        _for_each_group(cfg, group, size=8)

        def finish(i, carry):
            rows = pl.ds(pl.multiple_of(i * _COPY_ROWS, _COPY_ROWS), _COPY_ROWS)
            den = l_s[rows, :]
            o = (acc[rows, :] / den).astype(BF16)
            o_ref[rows, :] = o
            lse_ref[rows, :] = m_s[rows, :] + jnp.log(den)
            y_ref[rows, :] = (o.astype(F32) * _silu(z_ref[rows, :].astype(F32))).astype(BF16)
            return carry

        lax.fori_loop(0, s // _COPY_ROWS, finish, 0)

    def col(off):
        return pl.BlockSpec((s, HEAD_DIM), lambda h: (0, off // HEAD_DIM + h))

    head = pl.BlockSpec((s, HEAD_DIM), lambda h: (0, h))
    return pl.pallas_call(
        body, name=name, grid=(cfg.ha,),
        in_specs=[pl.BlockSpec(memory_space=pltpu.SMEM), col(cfg.qa), col(cfg.ka), col(cfg.va), col(cfg.za)],
        out_specs=[head, head, head],
        out_shape=[jax.ShapeDtypeStruct((s, cfg.aw), BF16), jax.ShapeDtypeStruct((s, cfg.aw), F32),
                   jax.ShapeDtypeStruct((s, cfg.d), BF16)],
        scratch_shapes=[pltpu.VMEM((s, HEAD_DIM), F32)] * 6 + [_bias_scratch()],
        compiler_params=_params(("parallel",)),
    )(slopes, proj, proj, proj, proj)


def _attn_bwd(cfg, proj, slopes, dy, oa, lse, name):
    s = cfg.s
    scale = HEAD_DIM ** -0.5
    dst_blocks = [c0 // HEAD_DIM for c0 in (cfg.za, cfg.qa, cfg.ka, cfg.va)]
    sub0, _, win0 = _attn_geometry(cfg, DILATIONS[-1])
    assign_first = sub0 == win0

    def body(sl_ref, q_ref, k_ref, v_ref, z_ref, dy_ref, o_ref, lse_ref, dproj_ref,
             qf, kf, vf, dof, dlt, dqa, dka, dva, bias_s, stage, sems):
        head = pl.program_id(0)
        slope = sl_ref[head]

        def out_copy(slot):
            cols = pl.ds(pl.multiple_of((dst_blocks[slot] + head) * HEAD_DIM, HEAD_DIM), HEAD_DIM)
            return pltpu.make_async_copy(stage.at[slot], dproj_ref.at[:, cols], sems.at[slot])

        @pl.when(head > 0)
        def _():
            out_copy(0).wait()

        def to_f32(i, carry):
            rows = pl.ds(pl.multiple_of(i * _COPY_ROWS, _COPY_ROWS), _COPY_ROWS)
            qf[rows, :] = q_ref[rows, :].astype(F32) * scale
            kf[rows, :] = k_ref[rows, :].astype(F32)
            vf[rows, :] = v_ref[rows, :].astype(F32)
            dyv = dy_ref[rows, :].astype(F32)
            zv = z_ref[rows, :].astype(F32)
            ov = o_ref[rows, :].astype(F32)
            dov = dyv * _silu(zv)
            dof[rows, :] = dov
            dlt[rows, :] = jnp.broadcast_to(jnp.sum(dov * ov, axis=-1, keepdims=True), (_COPY_ROWS, HEAD_DIM))
            stage[0, rows, :] = (dyv * ov * _dsilu(zv)).astype(BF16)
            if not assign_first:
                zero = jnp.zeros((_COPY_ROWS, HEAD_DIM), F32)
                dqa[rows, :] = zero
                dka[rows, :] = zero
                dva[rows, :] = zero
            return carry

        lax.fori_loop(0, s // _COPY_ROWS, to_f32, 0)
        out_copy(0).start()

        _fill_bias_tables(cfg, bias_s, slope)

        def group(pattern, dil, first, blocks):
            sub, bq, win = _attn_geometry(cfg, dil)
            rep = win // HEAD_DIM
            assign = first and assign_first
            work = []
            for r, i in blocks:
                shift, qrows, krows = _attn_rows(dil, r, i, sub, bq, win)
                work.append((shift, qrows, krows, qf[qrows, :].astype(BF16), kf[krows, :].astype(BF16),
                             vf[krows, :].astype(BF16), dof[qrows, :].astype(BF16),
                             lse_ref[qrows, :], dlt[qrows, :]))
            new = []
            for shift, qrows, krows, q, kw, vw, dob, lse_b, dlt_b in work:
                sc = _attn_scores(cfg, bias_s, q, kw, pattern, shift)
                p = jnp.exp(sc - jnp.tile(lse_b, (1, rep)))
                dp = lax.dot_general(dob, vw, _DN["nt"], preferred_element_type=F32)
                ds = (p * (dp - jnp.tile(dlt_b, (1, rep)))).astype(BF16)
                new.append((qrows, krows,
                            jnp.dot(ds, kw, preferred_element_type=F32) * scale,
                            lax.dot_general(ds, q, _DN["tn"], preferred_element_type=F32),
                            lax.dot_general(p.astype(BF16), dob, _DN["tn"], preferred_element_type=F32)))
            for qrows, krows, dq_b, dk_b, dv_b in new:
                if assign:
                    dqa[qrows, :] = dq_b
                    dka[krows, :] = dk_b
                    dva[krows, :] = dv_b
                else:
                    dqa[qrows, :] += dq_b
                    dka[krows, :] += dk_b
                    dva[krows, :] += dv_b

        _for_each_group(cfg, group, size=8)

        @pl.when(head > 0)
        def _():
            for slot in (1, 2, 3):
                out_copy(slot).wait()

        def emit(i, carry):
            rows = pl.ds(pl.multiple_of(i * _COPY_ROWS, _COPY_ROWS), _COPY_ROWS)
            stage[1, rows, :] = dqa[rows, :].astype(BF16)
            stage[2, rows, :] = dka[rows, :].astype(BF16)
            stage[3, rows, :] = dva[rows, :].astype(BF16)
            return carry

        lax.fori_loop(0, s // _COPY_ROWS, emit, 0)
        for slot in (1, 2, 3):
            out_copy(slot).start()

        @pl.when(head == pl.num_programs(0) - 1)
        def _():
            for slot in range(4):
                out_copy(slot).wait()

    def col(off):
        return pl.BlockSpec((s, HEAD_DIM), lambda h: (0, off // HEAD_DIM + h))

    head_cols = pl.BlockSpec((s, HEAD_DIM), lambda h: (0, h))
    return pl.pallas_call(
        body, name=name, grid=(cfg.ha,),
        in_specs=[pl.BlockSpec(memory_space=pltpu.SMEM), col(cfg.qa), col(cfg.ka), col(cfg.va), col(cfg.za),
                  head_cols, head_cols, head_cols],
        out_specs=_ANY,
        out_shape=jax.ShapeDtypeStruct((s, cfg.f), BF16),
        scratch_shapes=[pltpu.VMEM((s, HEAD_DIM), F32)] * 8
        + [_bias_scratch(), pltpu.VMEM((4, s, HEAD_DIM), BF16), pltpu.SemaphoreType.DMA((4,))],
        compiler_params=_params(("arbitrary",)),
    )(slopes, proj, proj, proj, proj, dy, oa, lse)


def _decay_tables(lg, backward):
    c = RET_CHUNK
    a = lax.broadcasted_iota(jnp.int32, (c, c), 0)
    b = lax.broadcasted_iota(jnp.int32, (c, c), 1)
    idx = lax.broadcasted_iota(jnp.int32, (c, 1), 0).astype(F32)
    if backward:
        rel = (b - a).astype(F32)
        ex_xi = c - idx
        ex_zeta = idx
    else:
        rel = (a - b).astype(F32)
        ex_xi = idx + 1.0
        ex_zeta = c - 1.0 - idx
    relc = jnp.maximum(rel, 0.0)
    dm = jnp.where(rel >= 0, jnp.exp(relc * lg), 0.0)
    xi = jnp.exp(ex_xi * lg)
    zeta = jnp.exp(ex_zeta * lg)
    gch = jnp.exp(jnp.full((1, 1), c, F32) * lg)
    return relc, dm, xi, zeta, ex_xi, ex_zeta, gch


def _ret_fwd(cfg, proj, lgs, y, name):
    s = cfg.s
    c = RET_CHUNK
    n = s // c
    kscale = RET_QK ** -0.5

    def body(lg_ref, q_ref, k_ref, v_ref, z_ref, y_in, o_ref, y_ref, st_ref):
        h = pl.program_id(0)
        tabs = [_decay_tables(lg_ref[dirn, h], dirn == 1) for dirn in range(2)]
        st_ref[...] = jnp.zeros_like(st_ref)
        o_ref[...] = jnp.zeros_like(o_ref)

        def step(t, carry):
            for dirn in range(2):
                _, dm, xi, zeta, _, _, gch = tabs[dirn]
                i = (n - 1 - t) if dirn == 1 else t
                rows = pl.ds(pl.multiple_of(i * c, c), c)
                qi = q_ref[rows, :]
                ks = k_ref[rows, :].astype(F32) * kscale
                vi = v_ref[rows, :]
                inner = lax.dot_general(qi, ks.astype(BF16), _DN["nt"], preferred_element_type=F32) * dm
                st = st_ref[dirn]
                o_ref[rows, :] += (jnp.dot(inner.astype(BF16), vi, preferred_element_type=F32)
                                   + jnp.dot(qi, st.astype(BF16), preferred_element_type=F32) * xi)
                st_ref[dirn] = st * gch + lax.dot_general((ks * zeta).astype(BF16), vi, _DN["tn"],
                                                          preferred_element_type=F32)
            return carry

        lax.fori_loop(0, n, step, 0, unroll=2)

        def gate(i, carry):
            rows = pl.ds(pl.multiple_of(i * _COPY_ROWS, _COPY_ROWS), _COPY_ROWS)
            oh = o_ref[rows, :]
            rr = lax.rsqrt(jnp.mean(oh * oh, axis=-1, keepdims=True) + NORM_EPS)
            y_ref[rows, :] = (oh * rr * _silu(z_ref[rows, :].astype(F32))).astype(BF16)
            return carry

        lax.fori_loop(0, s // _COPY_ROWS, gate, 0)

    return pl.pallas_call(
        body, name=name, grid=(cfg.hr,),
        in_specs=[pl.BlockSpec(memory_space=pltpu.SMEM),
                  pl.BlockSpec((s, RET_QK), lambda h: (0, cfg.qr // RET_QK + h)),
                  pl.BlockSpec((s, RET_QK), lambda h: (0, cfg.kr // RET_QK + h)),
                  pl.BlockSpec((s, RET_V), lambda h: (0, cfg.vr // RET_V + h)),
                  pl.BlockSpec((s, RET_V), lambda h: (0, cfg.zr // RET_V + h)), _ANY],
        out_specs=[pl.BlockSpec((s, RET_V), lambda h: (0, h)),
                   pl.BlockSpec((s, RET_V), lambda h: (0, cfg.aw // RET_V + h))],
        out_shape=[jax.ShapeDtypeStruct((s, cfg.rw), F32), jax.ShapeDtypeStruct(y.shape, BF16)],
        input_output_aliases={5: 1},
        scratch_shapes=[pltpu.VMEM((2, RET_QK, RET_V), F32)],
        compiler_params=_params(("parallel",)),
    )(lgs, proj, proj, proj, proj, y)


def _ret_bwd(cfg, proj, lgs, dy, oret, dproj, name):
    s = cfg.s
    c = RET_CHUNK
    n = s // c
    kscale = RET_QK ** -0.5

    def accumulate(lg_ref, q_ref, k_ref, v_ref, do_ref, dq_ref, dk_ref, dv_ref, dlg_ref, states, t_ref,
                   e_dm, e_xi, e_zeta, e_g):
        h = pl.program_id(0)
        tabs = [_decay_tables(lg_ref[dirn, h], dirn == 1) for dirn in range(2)]
        dlg_ref[...] = jnp.zeros_like(dlg_ref)
        dq_ref[...] = jnp.zeros_like(dq_ref)
        dk_ref[...] = jnp.zeros_like(dk_ref)
        dv_ref[...] = jnp.zeros_like(dv_ref)

        def chunk_rows(dirn, t):
            i = (n - 1 - t) if dirn == 1 else t
            return pl.ds(pl.multiple_of(i * c, c), c)

        t_ref[...] = jnp.zeros_like(t_ref)

        def fwd_step(t, carry):
            for dirn in range(2):
                _, _, _, zeta, _, _, gch = tabs[dirn]
                rows = chunk_rows(dirn, t)
                st = t_ref[dirn]
                states[dirn, t] = st
                ks = k_ref[rows, :].astype(F32) * kscale
                t_ref[dirn] = st * gch + lax.dot_general((ks * zeta).astype(BF16), v_ref[rows, :], _DN["tn"],
                                                         preferred_element_type=F32)
            return carry

        lax.fori_loop(0, n, fwd_step, 0, unroll=2)
        t_ref[...] = jnp.zeros_like(t_ref)

        for ref in (e_dm, e_xi, e_zeta, e_g):
            ref[...] = jnp.zeros_like(ref)

        def bwd_step(u, carry):
            t = n - 1 - u
            for dirn in range(2):
                relc, dm, xi, zeta, ex_xi, ex_zeta, gch = tabs[dirn]
                rows = chunk_rows(dirn, t)
                qi = q_ref[rows, :]
                ks = k_ref[rows, :].astype(F32) * kscale
                ksb = ks.astype(BF16)
                vi = v_ref[rows, :]
                doi = do_ref[rows, :]
                sn_f = states[dirn, t]
                sn = sn_f.astype(BF16)
                tt = t_ref[dirn]
                ttb = tt.astype(BF16)
                a_mat = lax.dot_general(qi, ksb, _DN["nt"], preferred_element_type=F32) * dm
                dov = lax.dot_general(doi, vi, _DN["nt"], preferred_element_type=F32)
                b_mat = (dov * dm).astype(BF16)
                kz = (ks * zeta).astype(BF16)
                d_v = (jnp.dot(kz, ttb, preferred_element_type=F32)
                       + lax.dot_general(a_mat.astype(BF16), doi, _DN["tn"], preferred_element_type=F32))
                dk_inter = lax.dot_general(vi, ttb, _DN["nt"], preferred_element_type=F32) * zeta
                d_k = lax.dot_general(b_mat, qi, _DN["tn"], preferred_element_type=F32) + dk_inter
                o_inter = jnp.dot(qi, sn, preferred_element_type=F32) * xi
                d_q = (jnp.dot(b_mat, ksb, preferred_element_type=F32)
                       + lax.dot_general(doi, sn, _DN["nt"], preferred_element_type=F32) * xi)
                e_dm[dirn] += relc * a_mat * dov
                e_xi[dirn] += ex_xi * (doi.astype(F32) * o_inter)
                e_zeta[dirn] += ex_zeta * (ks * dk_inter)
                e_g[dirn] += tt * sn_f
                t_ref[dirn] = tt * gch + lax.dot_general((qi.astype(F32) * xi).astype(BF16), doi, _DN["tn"],
                                                         preferred_element_type=F32)
                dq_ref[rows, :] += d_q
                dk_ref[rows, :] += d_k * kscale
                dv_ref[rows, :] += d_v
            return carry

        lax.fori_loop(0, n, bwd_step, 0, unroll=2)
        for dirn in range(2):
            total = (jnp.sum(e_dm[dirn], keepdims=True) + jnp.sum(e_xi[dirn], keepdims=True)
                     + jnp.sum(e_zeta[dirn], keepdims=True) + (c * tabs[dirn][6]) * jnp.sum(e_g[dirn], keepdims=True))
            dlg_ref[0, dirn:dirn + 1, :] = jnp.broadcast_to(total, (1, LANES))

    def body(lg_ref, q_ref, k_ref, v_ref, z_ref, dy_ref, or_ref, dproj_in, dproj_ref, dlg_ref,
             do_s, dq_s, dk_s, dv_s, stage_z, stage_q, stage_k, stage_v, sems, *scratch):
        head = pl.program_id(0)
        stages = (stage_z, stage_q, stage_k, stage_v)
        firsts = (cfg.zr, cfg.qr, cfg.kr, cfg.vr)

        def out_copy(slot):
            width = stages[slot].shape[1]
            cols = pl.ds(pl.multiple_of(firsts[slot] + head * width, LANES), width)
            return pltpu.make_async_copy(stages[slot], dproj_ref.at[:, cols], sems.at[slot])

        @pl.when(head > 0)
        def _():
            out_copy(0).wait()

        def gate_norm_bwd(i, carry):
            rows = pl.ds(pl.multiple_of(i * _COPY_ROWS, _COPY_ROWS), _COPY_ROWS)
            oh = or_ref[rows, :]
            zr = z_ref[rows, :].astype(F32)
            dyr = dy_ref[rows, :].astype(F32)
            rr = lax.rsqrt(jnp.mean(oh * oh, axis=-1, keepdims=True) + NORM_EPS)
            yn = oh * rr
            dyn = dyr * _silu(zr)
            stage_z[rows, :] = (dyr * yn * _dsilu(zr)).astype(BF16)
            do_s[rows, :] = (rr * (dyn - yn * jnp.mean(dyn * yn, axis=-1, keepdims=True))).astype(BF16)
            return carry

        lax.fori_loop(0, s // _COPY_ROWS, gate_norm_bwd, 0)
        out_copy(0).start()

        accumulate(lg_ref, q_ref, k_ref, v_ref, do_s, dq_s, dk_s, dv_s, dlg_ref, *scratch)

        @pl.when(head > 0)
        def _():
            for slot in (1, 2, 3):
                out_copy(slot).wait()

        def emit(i, carry):
            rows = pl.ds(pl.multiple_of(i * _COPY_ROWS, _COPY_ROWS), _COPY_ROWS)
            stage_q[rows, :] = dq_s[rows, :].astype(BF16)
            stage_k[rows, :] = dk_s[rows, :].astype(BF16)
            stage_v[rows, :] = dv_s[rows, :].astype(BF16)
            return carry

        lax.fori_loop(0, s // _COPY_ROWS, emit, 0)
        for slot in (1, 2, 3):
            out_copy(slot).start()

        @pl.when(head == pl.num_programs(0) - 1)
        def _():
            for slot in range(4):
                out_copy(slot).wait()

    return pl.pallas_call(
        body, name=name, grid=(cfg.hr,),
        in_specs=[pl.BlockSpec(memory_space=pltpu.SMEM),
                  pl.BlockSpec((s, RET_QK), lambda h: (0, cfg.qr // RET_QK + h)),
                  pl.BlockSpec((s, RET_QK), lambda h: (0, cfg.kr // RET_QK + h)),
                  pl.BlockSpec((s, RET_V), lambda h: (0, cfg.vr // RET_V + h)),
                  pl.BlockSpec((s, RET_V), lambda h: (0, cfg.zr // RET_V + h)),
                  pl.BlockSpec((s, RET_V), lambda h: (0, cfg.aw // RET_V + h)),
                  pl.BlockSpec((s, RET_V), lambda h: (0, h)), _ANY],
        out_specs=[_ANY, pl.BlockSpec((1, 8, LANES), lambda h: (h, 0, 0))],
        out_shape=[jax.ShapeDtypeStruct(dproj.shape, BF16), jax.ShapeDtypeStruct((cfg.hr, 8, LANES), F32)],
        input_output_aliases={7: 0},
        scratch_shapes=[pltpu.VMEM((s, RET_V), BF16),
                        pltpu.VMEM((s, RET_QK), F32), pltpu.VMEM((s, RET_QK), F32), pltpu.VMEM((s, RET_V), F32),
                        pltpu.VMEM((s, RET_V), BF16), pltpu.VMEM((s, RET_QK), BF16), pltpu.VMEM((s, RET_QK), BF16),
                        pltpu.VMEM((s, RET_V), BF16), pltpu.SemaphoreType.DMA((4,)),
                        pltpu.VMEM((2, n, RET_QK, RET_V), F32), pltpu.VMEM((2, RET_QK, RET_V), F32),
                        pltpu.VMEM((2, c, c), F32), pltpu.VMEM((2, c, RET_V), F32),
                        pltpu.VMEM((2, c, RET_QK), F32), pltpu.VMEM((2, RET_QK, RET_V), F32)],
        compiler_params=_params(("arbitrary",)),
    )(lgs, proj, proj, proj, proj, dy, oret, dproj)


_ADA_ROWS = 16


def _ada_fwd(cact, w_ada, name):
    depth, d, n = w_ada.shape
    rows = cact.shape[0]
    tn = _tile(n, 768)

    def body(c_ref, w_ref, o_ref):
        o_ref[0] = jnp.dot(c_ref[...], w_ref[0].astype(BF16), preferred_element_type=F32)

    return pl.pallas_call(
        body, name=name, grid=(depth, n // tn),
        in_specs=[pl.BlockSpec((rows, d), lambda l, j: (0, 0)), pl.BlockSpec((1, d, tn), lambda l, j: (l, 0, j))],
        out_specs=pl.BlockSpec((1, rows, tn), lambda l, j: (l, 0, j)),
        out_shape=jax.ShapeDtypeStruct((depth, rows, n), F32),
        compiler_params=_params(("parallel", "parallel")),
    )(cact, w_ada)


def _adam_math(w, g, m, v):
    m2 = ADAM_B1 * m + (1.0 - ADAM_B1) * g
    v2 = ADAM_B2 * v + (1.0 - ADAM_B2) * (g * g)
    m_hat = m2 / (1.0 - ADAM_B1 ** ADAM_STEP)
    v_hat = v2 / (1.0 - ADAM_B2 ** ADAM_STEP)
    delta = -ADAM_LR * (m_hat / (jnp.sqrt(v_hat) + ADAM_EPS) + ADAM_WD * w)
    return delta, m2, v2


def _adamw_big(w, g, m, v, first, count, prev, name):
    _, r, c = w.shape
    tr, tc = _tile(r, 512), _tile(c, 1024)

    def body(w_ref, g_ref, m_ref, v_ref, *rest):
        go_ref, d_ref, mo_ref, vo_ref = rest[-4:]
        gv = g_ref[...]
        delta, m2, v2 = _adam_math(w_ref[...], gv, m_ref[...], v_ref[...])
        go_ref[...] = gv
        d_ref[...] = delta
        mo_ref[...] = m2
        vo_ref[...] = v2

    spec = pl.BlockSpec((1, tr, tc), lambda l, i, j: (first + l, i, j))
    shp = jax.ShapeDtypeStruct(w.shape, F32)
    carried = [] if prev is None else list(prev)
    return pl.pallas_call(
        body, name=name, grid=(count, r // tr, c // tc),
        in_specs=[spec] * 4 + [_ANY] * len(carried), out_specs=[spec] * 4, out_shape=[shp] * 4,
        input_output_aliases={4 + k: k for k in range(len(carried))},
        compiler_params=_params(("parallel", "parallel", "parallel")),
    )(w, g, m, v, *carried)


def _adamw_ada(w, m, v, cact, dmod, name):
    depth, r, c = w.shape
    kk = cact.shape[0]
    tr, tc = _tile(r, 512), _tile(c, 768)

    def body(w_ref, m_ref, v_ref, c_ref, dm_ref, go_ref, d_ref, mo_ref, vo_ref):
        gv = lax.dot_general(c_ref[...], dm_ref[0], _DN["tn"], preferred_element_type=F32)
        delta, m2, v2 = _adam_math(w_ref[0], gv, m_ref[0], v_ref[0])
        go_ref[0] = gv
        d_ref[0] = delta
        mo_ref[0] = m2
        vo_ref[0] = v2

    spec = pl.BlockSpec((1, tr, tc), lambda l, i, j: (l, i, j))
    shp = jax.ShapeDtypeStruct(w.shape, F32)
    return pl.pallas_call(
        body, name=name, grid=(depth, r // tr, c // tc),
        in_specs=[spec] * 3 + [pl.BlockSpec((kk, tr), lambda l, i, j: (0, i)),
                               pl.BlockSpec((1, kk, tc), lambda l, i, j: (l, 0, j))],
        out_specs=[spec] * 4, out_shape=[shp] * 4,
        compiler_params=_params(("parallel", "parallel", "parallel")),
    )(w, m, v, cact, dmod)


def _adamw_small(w, g, m, v, name):
    def body(w_ref, g_ref, m_ref, v_ref, d_ref, mo_ref, vo_ref):
        delta, m2, v2 = _adam_math(w_ref[...], g_ref[...], m_ref[...], v_ref[...])
        d_ref[...] = delta
        mo_ref[...] = m2
        vo_ref[...] = v2

    shp = jax.ShapeDtypeStruct(w.shape, F32)
    return pl.pallas_call(body, name=name, out_shape=[shp] * 3)(w, g, m, v)


def _sum_gathered(parts, name):
    nd, r, c = parts.shape

    def body(p_ref, o_ref):
        acc = p_ref[0]
        for e in range(1, nd):
            acc = acc + p_ref[e]
        o_ref[...] = acc

    return pl.pallas_call(body, name=name, out_shape=jax.ShapeDtypeStruct((r, c), F32))(parts)


def _flip(v, bit):
    return 1 - v if bit else v


def _all_gather_small(x, name):
    r, c = x.shape

    def body(x_ref, out_ref, send_sems, recv_sems, local_sem):
        mx, my, mc = lax.axis_index("x"), lax.axis_index("y"), lax.axis_index("c")
        me = 4 * mx + 2 * my + mc
        mine = pltpu.make_async_copy(x_ref, out_ref.at[me], local_sem)
        mine.start()
        sends = []
        for k in range(1, N_DEV):
            peer = (_flip(mx, k & 4), _flip(my, k & 2), _flip(mc, k & 1))
            cp = pltpu.make_async_remote_copy(src_ref=x_ref, dst_ref=out_ref.at[me], send_sem=send_sems.at[k - 1],
                                              recv_sem=recv_sems.at[k - 1], device_id=peer, device_id_type=MESH)
            cp.start()
            sends.append(cp)
        for k in range(1, N_DEV):
            peer = (_flip(mx, k & 4), _flip(my, k & 2), _flip(mc, k & 1))
            src = 4 * peer[0] + 2 * peer[1] + peer[2]
            pltpu.make_async_remote_copy(src_ref=x_ref, dst_ref=out_ref.at[src], send_sem=send_sems.at[k - 1],
                                         recv_sem=recv_sems.at[k - 1], device_id=peer,
                                         device_id_type=MESH).wait_recv()
        for cp in sends:
            cp.wait_send()
        mine.wait()

    return pl.pallas_call(
        body, name=name,
        out_shape=jax.ShapeDtypeStruct((N_DEV, r, c), x.dtype),
        in_specs=[pl.BlockSpec(memory_space=pltpu.VMEM)],
        out_specs=pl.BlockSpec(memory_space=pltpu.VMEM),
        scratch_shapes=[pltpu.SemaphoreType.DMA((N_DEV - 1,)), pltpu.SemaphoreType.DMA((N_DEV - 1,)),
                        pltpu.SemaphoreType.DMA],
        compiler_params=pltpu.CompilerParams(vmem_limit_bytes=VMEM_LIMIT),
    )(x)


_HBM = pl.BlockSpec(memory_space=pltpu.HBM)
_SEM = pl.BlockSpec(memory_space=pltpu.SEMAPHORE)
_ANY = pl.BlockSpec(memory_space=pl.ANY)
_EFFECT = pltpu.SideEffectType.DATAFLOW_SIDE_EFFECTING


def _in_hbm(a):
    return pltpu.with_memory_space_constraint(a, pltpu.HBM)


def _place_w_in(w, layer, chip1, name):
    _, d, fc = w.shape
    tr = _tile(d, 512)

    def body(c_ref, w_ref, o_ref):
        o_ref[...] = w_ref[...].astype(BF16)

    return pl.pallas_call(
        body, name=name,
        grid_spec=pltpu.PrefetchScalarGridSpec(
            num_scalar_prefetch=1, grid=(d // tr,),
            in_specs=[pl.BlockSpec((None, tr, fc), lambda i, c: (layer, i, 0))],
            out_specs=pl.BlockSpec((tr, fc), lambda i, c: (i, c[0]))),
        out_shape=jax.ShapeDtypeStruct((d, N_CHIP * fc), BF16),
        compiler_params=_params(("parallel",)),
    )(chip1, w)


def _place_w_out(w, layer, chip1, name):
    _, rc, dd = w.shape
    tc = _tile(dd, 1024)

    def body(c_ref, w_ref, o_ref):
        o_ref[...] = w_ref[...].astype(BF16)

    return pl.pallas_call(
        body, name=name,
        grid_spec=pltpu.PrefetchScalarGridSpec(
            num_scalar_prefetch=1, grid=(dd // tc,),
            in_specs=[pl.BlockSpec((None, rc, tc), lambda j, c: (layer, 0, j))],
            out_specs=pl.BlockSpec((rc, tc), lambda j, c: (c[0], j))),
        out_shape=jax.ShapeDtypeStruct((N_CHIP * rc, dd), BF16),
        compiler_params=_params(("parallel",)),
    )(chip1, w)


def _weight_region(ref, tensor, chip):
    if tensor == 0:
        fc = ref.shape[1] // N_CHIP
        return ref.at[:, pl.ds(pl.multiple_of(chip * fc, LANES), fc)]
    rc = ref.shape[0] // N_CHIP
    return ref.at[pl.ds(pl.multiple_of(chip * rc, 8), rc), :]


def _gather_copies(ref, tensor, send_sems, recv_sems, landing):
    mx, my, mc = lax.axis_index("x"), lax.axis_index("y"), lax.axis_index("c")
    mine = _weight_region(ref, tensor, 2 * mx + my)
    copies = []
    for j in range(1, N_CHIP):
        peer = (_flip(mx, j & 2), _flip(my, j & 1), mc)
        dst = _weight_region(ref, tensor, 2 * peer[0] + peer[1]) if landing else mine
        idx = 2 * (j - 1) + tensor
        copies.append(pltpu.make_async_remote_copy(src_ref=mine, dst_ref=dst, send_sem=send_sems.at[idx],
                                                   recv_sem=recv_sems.at[idx], device_id=peer, device_id_type=MESH))
    return copies


def _gather_start(fi, fo, dep, name):
    ns = 2 * (N_CHIP - 1)

    def body(fi_ref, fo_ref, dep_ref, send_sems, recv_sems, fi_thru, fo_thru, token):
        for tensor, ref in enumerate((fi_ref, fo_ref)):
            for cp in _gather_copies(ref, tensor, send_sems, recv_sems, landing=False):
                cp.start()
        token[...] = jnp.zeros_like(token)

    return pl.pallas_call(
        body, name=name,
        out_shape=(pltpu.SemaphoreType.DMA((ns,)), pltpu.SemaphoreType.DMA((ns,)),
                   pltpu.HBM(fi.shape, fi.dtype), pltpu.HBM(fo.shape, fo.dtype),
                   jax.ShapeDtypeStruct((8, LANES), F32)),
        in_specs=(_HBM, _HBM, _ANY),
        out_specs=(_SEM, _SEM, _HBM, _HBM, pl.BlockSpec(memory_space=pltpu.VMEM)),
        input_output_aliases={0: 2, 1: 3},
        compiler_params=pltpu.CompilerParams(has_side_effects=_EFFECT),
    )(_in_hbm(fi), _in_hbm(fo), dep)


def _gather_wait(send_sems, recv_sems, buf, tensor, after, name):
    def body(buf_ref, send_sems, recv_sems, after_ref, buf_out):
        for cp in _gather_copies(buf_ref, tensor, send_sems, recv_sems, landing=True):
            cp.wait_send()
            cp.wait_recv()

    return pl.pallas_call(
        body, name=name,
        out_shape=pltpu.HBM(buf.shape, buf.dtype),
        in_specs=(_HBM, _SEM, _SEM, _ANY), out_specs=_HBM,
        input_output_aliases={0: 0},
        compiler_params=pltpu.CompilerParams(has_side_effects=_EFFECT),
    )(buf, send_sems, recv_sems, after)


def _w_in_half(ref, half, chip):
    hr = ref.shape[0] // 2
    fc = ref.shape[1] // N_CHIP
    return ref.at[pl.ds(pl.multiple_of(half * hr, 8), hr), pl.ds(pl.multiple_of(chip * fc, LANES), fc)]


def _half_copies_ici(ref, send_sems, recv_sems, landing):
    mx, my, mc = lax.axis_index("x"), lax.axis_index("y"), lax.axis_index("c")
    mine = _w_in_half(ref, mc, 2 * mx + my)
    copies = []
    for j in range(1, N_CHIP):
        peer = (_flip(mx, j & 2), _flip(my, j & 1), mc)
        dst = _w_in_half(ref, mc, 2 * peer[0] + peer[1]) if landing else mine
        copies.append(pltpu.make_async_remote_copy(src_ref=mine, dst_ref=dst, send_sem=send_sems.at[j - 1],
                                                   recv_sem=recv_sems.at[j - 1], device_id=peer, device_id_type=MESH))
    return copies


def _half_copies_d2d(ref, send_sems, recv_sems, landing):
    mx, my, mc = lax.axis_index("x"), lax.axis_index("y"), lax.axis_index("c")
    sib = (mx, my, 1 - mc)
    copies = []
    for j in range(1, N_CHIP):
        other = 2 * _flip(mx, j & 2) + _flip(my, j & 1)
        src = _w_in_half(ref, mc, other)
        dst = _w_in_half(ref, 1 - mc, other) if landing else src
        copies.append(pltpu.make_async_remote_copy(src_ref=src, dst_ref=dst, send_sem=send_sems.at[j - 1],
                                                   recv_sem=recv_sems.at[j - 1], device_id=sib, device_id_type=MESH))
    return copies


def _w_out_copies(ref, send_sems, recv_sems, landing):
    mx, my, mc = lax.axis_index("x"), lax.axis_index("y"), lax.axis_index("c")
    mine = _weight_region(ref, 1, 2 * mx + my)
    copies = []
    for j in range(1, N_CHIP):
        peer = (_flip(mx, j & 2), _flip(my, j & 1), mc)
        dst = _weight_region(ref, 1, 2 * peer[0] + peer[1]) if landing else mine
        copies.append(pltpu.make_async_remote_copy(src_ref=mine, dst_ref=dst, send_sem=send_sems.at[j - 1],
                                                   recv_sem=recv_sems.at[j - 1], device_id=peer, device_id_type=MESH))
    return copies


def _copies_start(buf, copies, dep, name):
    ns = N_CHIP - 1

    def body(buf_ref, dep_ref, send_sems, recv_sems, buf_thru, token):
        for cp in copies(buf_ref, send_sems, recv_sems, landing=False):
            cp.start()
        token[...] = jnp.zeros_like(token)

    return pl.pallas_call(
        body, name=name,
        out_shape=(pltpu.SemaphoreType.DMA((ns,)), pltpu.SemaphoreType.DMA((ns,)),
                   pltpu.HBM(buf.shape, buf.dtype), jax.ShapeDtypeStruct((8, LANES), F32)),
        in_specs=(_HBM, _ANY),
        out_specs=(_SEM, _SEM, _HBM, pl.BlockSpec(memory_space=pltpu.VMEM)),
        input_output_aliases={0: 2},
        compiler_params=pltpu.CompilerParams(has_side_effects=_EFFECT),
    )(_in_hbm(buf), dep)


def _copies_wait(send_sems, recv_sems, buf, copies, after, name):
    def body(buf_ref, send_sems, recv_sems, after_ref, buf_out):
        for cp in copies(buf_ref, send_sems, recv_sems, landing=True):
            cp.wait_send()
            cp.wait_recv()

    return pl.pallas_call(
        body, name=name,
        out_shape=pltpu.HBM(buf.shape, buf.dtype),
        in_specs=(_HBM, _SEM, _SEM, _ANY), out_specs=_HBM,
        input_output_aliases={0: 0},
        compiler_params=pltpu.CompilerParams(has_side_effects=_EFFECT),
    )(buf, send_sems, recv_sems, after)


def _scatter_copies(gi_ref, go_ref, pi_ref, po_ref, send_sems, recv_sems):
    mx, my, mc = lax.axis_index("x"), lax.axis_index("y"), lax.axis_index("c")
    hr, fc = pi_ref.shape[1:]
    ro = po_ref.shape[1]
    copies = []
    for k in range(1, N_DEV):
        peer = (_flip(mx, k & 4), _flip(my, k & 2), _flip(mc, k & 1))
        pchip = 2 * peer[0] + peer[1]
        src = (gi_ref.at[pl.ds(pl.multiple_of(peer[2] * hr, 8), hr), pl.ds(pl.multiple_of(pchip * fc, LANES), fc)],
               go_ref.at[pl.ds(pl.multiple_of((2 * pchip + peer[2]) * ro, 8), ro), :])
        dst = (pi_ref.at[k - 1], po_ref.at[k - 1])
        for t in range(2):
            idx = 2 * (k - 1) + t
            copies.append(pltpu.make_async_remote_copy(src_ref=src[t], dst_ref=dst[t], send_sem=send_sems.at[idx],
                                                       recv_sem=recv_sems.at[idx], device_id=peer,
                                                       device_id_type=MESH))
    return copies


def _scatter_start(gi, go, name):
    d, f = gi.shape
    dd = go.shape[1]
    ns = 2 * (N_DEV - 1)
    pi = lax.empty((N_DEV - 1, d // 2, f // N_CHIP), BF16)
    po = lax.empty((N_DEV - 1, d // N_DEV, dd), BF16)

    def body(gi_ref, go_ref, pi_ref, po_ref, send_sems, recv_sems, gi_thru, go_thru, pi_thru, po_thru, token):
        for cp in _scatter_copies(gi_ref, go_ref, pi_ref, po_ref, send_sems, recv_sems):
            cp.start()
        token[...] = jnp.zeros_like(token)

    return pl.pallas_call(
        body, name=name,
        out_shape=(pltpu.SemaphoreType.DMA((ns,)), pltpu.SemaphoreType.DMA((ns,)),
                   pltpu.HBM(gi.shape, gi.dtype), pltpu.HBM(go.shape, go.dtype),
                   pltpu.HBM(pi.shape, pi.dtype), pltpu.HBM(po.shape, po.dtype),
                   jax.ShapeDtypeStruct((8, LANES), F32)),
        in_specs=(_HBM, _HBM, _HBM, _HBM),
        out_specs=(_SEM, _SEM, _HBM, _HBM, _HBM, _HBM, pl.BlockSpec(memory_space=pltpu.VMEM)),
        input_output_aliases={0: 2, 1: 3, 2: 4, 3: 5},
        compiler_params=pltpu.CompilerParams(has_side_effects=_EFFECT),
    )(_in_hbm(gi), _in_hbm(go), _in_hbm(pi), _in_hbm(po))


def _scatter_wait(send_sems, recv_sems, gi, go, pi, po, after, name):
    def body(gi_ref, go_ref, pi_ref, po_ref, send_sems, recv_sems, *rest):
        for cp in _scatter_copies(gi_ref, go_ref, pi_ref, po_ref, send_sems, recv_sems):
            cp.wait_send()
            cp.wait_recv()

    return pl.pallas_call(
        body, name=name,
        out_shape=tuple(pltpu.HBM(a.shape, a.dtype) for a in (gi, go, pi, po)),
        in_specs=(_HBM, _HBM, _HBM, _HBM, _SEM, _SEM) + (_ANY,) * len(after), out_specs=(_HBM, _HBM, _HBM, _HBM),
        input_output_aliases={0: 0, 1: 1, 2: 2, 3: 3},
        compiler_params=pltpu.CompilerParams(has_side_effects=_EFFECT),
    )(gi, go, pi, po, send_sems, recv_sems, *after)


def _sum_into(buf, g, parts, where2, layer, row_blocks, dep, name):
    depth, r2, c = buf.shape
    r = r2 // 2
    tr, tc = _tile(r, 256), _tile(c, 1024)
    nr, nc = r // tr, c // tc
    col_blocks = (g.shape[1] // c) > 1

    def body(w_ref, buf_ref, g_ref, p_ref, dep_ref, o_ref):
        acc = g_ref[...].astype(F32)
        for e in range(N_DEV - 1):
            acc = acc + p_ref[e].astype(F32)
        o_ref[...] = acc

    return pl.pallas_call(
        body, name=name,
        grid_spec=pltpu.PrefetchScalarGridSpec(
            num_scalar_prefetch=1, grid=(nr, nc),
            in_specs=[_ANY,
                      pl.BlockSpec((tr, tc), lambda i, j, w: (row_blocks(w, nr) + i, (w[1] * nc if col_blocks else 0) + j)),
                      pl.BlockSpec((N_DEV - 1, tr, tc), lambda i, j, w: (0, i, j)), _ANY],
            out_specs=pl.BlockSpec((None, tr, tc), lambda i, j, w: (layer, w[0] * nr + i, j))),
        out_shape=jax.ShapeDtypeStruct(buf.shape, F32),
        input_output_aliases={1: 0},
        compiler_params=_params(("parallel", "parallel")),
    )(where2, buf, g, parts, dep)


def _exchange_halves(b_in, b_out, first, count, name):
    r_in = b_in.shape[1]
    r_out = b_out.shape[1]

    def body(bi_ref, bo_ref, oi_ref, oo_ref, send_sems, recv_sems):
        mx, my, mc = lax.axis_index("x"), lax.axis_index("y"), lax.axis_index("c")
        sib = (mx, my, 1 - mc)

        def half(ref, l, rows, which):
            return ref.at[l, pl.ds(pl.multiple_of(which * (rows // 2), 8), rows // 2), :]

        copies = []
        for l in range(first, first + count):
            for t, (src, dst, rows) in enumerate(((bi_ref, oi_ref, r_in), (bo_ref, oo_ref, r_out))):
                idx = 2 * (l - first) + t
                kw = dict(send_sem=send_sems.at[idx], recv_sem=recv_sems.at[idx], device_id=sib, device_id_type=MESH)
                cp = pltpu.make_async_remote_copy(src_ref=half(src, l, rows, mc), dst_ref=half(dst, l, rows, mc), **kw)
                cp.start()
                copies.append((cp, pltpu.make_async_remote_copy(src_ref=half(src, l, rows, mc),
                                                                dst_ref=half(dst, l, rows, 1 - mc), **kw)))
        for cp, landed in copies:
            landed.wait_recv()
        for cp, landed in copies:
            cp.wait_send()

    ns = 2 * count
    return pl.pallas_call(
        body, name=name,
        out_shape=[jax.ShapeDtypeStruct(b_in.shape, F32), jax.ShapeDtypeStruct(b_out.shape, F32)],
        in_specs=[_ANY, _ANY], out_specs=[_ANY, _ANY],
        input_output_aliases={0: 0, 1: 1},
        scratch_shapes=[pltpu.SemaphoreType.DMA((ns,)), pltpu.SemaphoreType.DMA((ns,))],
    )(b_in, b_out)


def _rows8(v):
    return jnp.pad(v, ((0, 8 - v.shape[0]), (0, 0)))


def kernel(x, c, norm_gain, w_ada, b_ada, w_in, w_out, ret_decay_logit_f, ret_decay_logit_b, final_gain, loss_target, m_norm_gain, m_w_ada, m_b_ada, m_w_in, m_w_out, m_ret_decay_logit_f, m_ret_decay_logit_b, m_final_gain, v_norm_gain, v_w_ada, v_b_ada, v_w_in, v_w_out, v_ret_decay_logit_f, v_ret_decay_logit_b, v_final_gain):
    cfg = _Cfg()
    depth, d = norm_gain.shape
    mx, my, mc = lax.axis_index("x"), lax.axis_index("y"), lax.axis_index("c")
    me = 4 * mx + 2 * my + mc
    chip = 2 * mx + my
    x0 = x[0]
    tgt = loss_target[0]
    ada_cols = w_ada.shape[2]

    chip1 = jnp.reshape(chip, (1,)).astype(jnp.int32)
    where2 = jnp.stack([mc, chip]).astype(jnp.int32)

    def start_gather(l, dep):
        return _gather_start(_place_w_in(w_in, l, chip1, f"place_w_in_{l}"),
                             _place_w_out(w_out, l, chip1, f"place_w_out_{l}"), dep, f"gather_start_{l}")

    c_all = _all_gather_small(_rows8(c), "gather_c")[:, 0, :]
    cact = _silu(c_all)
    ici = _copies_start(_place_w_in(w_in, 0, chip1, "place_w_in_0"), _half_copies_ici, c_all, "gather0_ici_start")
    cact_rows = jnp.pad(cact, ((0, _ADA_ROWS - N_DEV), (0, 0))) + ici[3][0:1, 0:1]
    mod_part = _ada_fwd(cact_rows.astype(BF16), w_ada, "ada_fwd")
    mod_all = _all_gather_small(mod_part.reshape(depth * _ADA_ROWS, ada_cols), "gather_mod")
    mod_all = mod_all.reshape(N_CHIP, 2, depth, _ADA_ROWS, ada_cols)[:, 0]
    mod_mine = lax.dynamic_index_in_dim(mod_all, me, axis=2, keepdims=False)
    mod = jnp.transpose(mod_mine, (1, 0, 2)).reshape(depth, 3, d)
    bias = b_ada.reshape(depth, 3, d)
    fi0 = _copies_wait(ici[0], ici[1], ici[2], _half_copies_ici, mod, "gather0_ici_wait")
    d2d = _copies_start(fi0, _half_copies_d2d, mod, "gather0_d2d_start")
    out0 = _copies_start(_place_w_out(w_out, 0, chip1, "place_w_out_0"), _w_out_copies, d2d[3], "gather0_out_start")
    w_in_0 = _copies_wait(d2d[0], d2d[1], d2d[2], _half_copies_d2d, out0[3], "gather0_d2d_wait")

    slopes = jnp.exp2(-8.0 * (jnp.arange(cfg.ha, dtype=F32) + 1.0) / cfg.ha)
    lg_f = jax.nn.log_sigmoid(ret_decay_logit_f)
    lg_b = jax.nn.log_sigmoid(ret_decay_logit_b)

    saved = []
    w_full = []
    h_x = x0
    pending = None
    for l in range(depth):
        if l == 0:
            w_in_l = w_in_0
        else:
            send_sems, recv_sems, fi, fo, _ = pending
            w_in_l = _gather_wait(send_sems, recv_sems, fi, 0, h_x, f"gather_wait_in_{l}")
        g8 = _rows8(norm_gain[l:l + 1])
        if l + 1 < depth:
            pending = start_gather(l + 1, w_in_l)
            g8 = g8 + pending[4][0:1, 0:1]
        mod3, b3 = _rows8(mod[l]), _rows8(bias[l])
        hb = _norm_mod_fwd(h_x, g8, mod3, b3, f"norm_mod_fwd_{l}")
        proj = _matmul(hb, w_in_l, "nn", BF16, f"in_proj_{l}")
        oa, lse, y = _attn_fwd(cfg, proj, slopes, f"attn_fwd_{l}")
        lgs = jnp.stack([lg_f[l], lg_b[l]])
        oret, y = _ret_fwd(cfg, proj, lgs, y, f"ret_fwd_{l}")
        if l == 0:
            w_out_l = _copies_wait(out0[0], out0[1], out0[2], _w_out_copies, y, "gather0_out_wait")
        else:
            w_out_l = _gather_wait(send_sems, recv_sems, fo, 1, y, f"gather_wait_out_{l}")
        w_full.append((w_in_l, w_out_l))
        x_next, out = _out_proj_fwd(y, w_out_l, h_x, mod3, b3, f"out_proj_{l}")
        saved.append((h_x, hb, proj, oret, y, oa, lse, out, g8, mod3, b3, lgs))
        h_x = x_next

    dx, loss8, fin_acc = _final_loss(h_x, tgt, _rows8(final_gain[None]), "final_loss")

    landed = [None] * depth
    d_mod, d_gain, d_lg = [None] * depth, [None] * depth, [None] * depth
    in_flight = None
    for l in reversed(range(depth)):
        x_l, hb, proj, oret, y, oa, lse, out, g8, mod3, b3, lgs = saved[l]
        w_in_l, w_out_l = w_full[l]
        douts, gate_acc = _out_proj_bwd_prep(dx, out, mod3, b3, f"out_proj_bwd_prep_{l}")
        dy = _matmul(douts, w_out_l, "nt", BF16, f"out_proj_dy_{l}")
        g_out_l = _matmul(y, douts, "tn", BF16, f"out_proj_dw_{l}", tk=4096)
        dproj = _attn_bwd(cfg, proj, slopes, dy, oa, lse, f"attn_bwd_{l}")
        dproj, dlg = _ret_bwd(cfg, proj, lgs, dy, oret, dproj, f"ret_bwd_{l}")
        g_in_l = _matmul(hb, dproj, "tn", BF16, f"in_proj_dw_{l}", tk=4096)
        started = _scatter_start(g_in_l, g_out_l, f"scatter_start_{l}")
        dh = _matmul(dproj, w_in_l, "nt", F32, f"in_proj_dh_{l}", tk=3584)
        g8 = g8 + started[-1][0:1, 0:1]
        dx, nm_acc = _norm_mod_bwd(dh, x_l, dx, g8, mod3, b3, f"norm_mod_bwd_{l}")
        d_mod[l] = jnp.concatenate([nm_acc[0], nm_acc[1], gate_acc[0]])
        d_gain[l] = nm_acc[2]
        d_lg[l] = dlg[:, 0:2, 0]
        if in_flight is not None:
            landed[l + 1] = _scatter_wait(*in_flight[:-1], (dx,), f"scatter_wait_{l + 1}")
        in_flight = started

    gw_in = lax.empty(w_in.shape, F32)
    gw_out = lax.empty(w_out.shape, F32)
    res_in = res_out = None
    for first, count in ((1, depth - 1), (0, 1)):
        if first == 0:
            after = (dx,) if res_out is None else (res_in[1], res_out[1])
            landed[0] = _scatter_wait(*in_flight[:-1], after, "scatter_wait_0")
        if count == 0:
            continue
        dep = in_flight[-1]
        for l in range(first, first + count):
            gi, go, pi, po = landed[l]
            gw_in = _sum_into(gw_in, gi, pi, where2, l, lambda w, nr: w[0] * nr, dep, f"sum_w_in_{l}")
            gw_out = _sum_into(gw_out, go, po, where2, l, lambda w, nr: (2 * w[1] + w[0]) * nr, dep,
                               f"sum_w_out_{l}")
        gw_in, gw_out = _exchange_halves(gw_in, gw_out, first, count, f"exchange_halves_{first}")
        res_in = _adamw_big(w_in, gw_in, m_w_in, v_w_in, first, count, res_in, f"adamw_w_in_{first}")
        res_out = _adamw_big(w_out, gw_out, m_w_out, v_w_out, first, count, res_out, f"adamw_w_out_{first}")
    grad_w_in, delta_w_in, new_m_w_in, new_v_w_in = res_in
    grad_w_out, delta_w_out, new_m_w_out, new_v_w_out = res_out

    dmod_mine = jnp.stack(d_mod)
    dmod_gathered = _all_gather_small(_rows8(dmod_mine), "gather_dmod")
    dmod_all = dmod_gathered[:, :depth, :]
    grad_b_ada = _sum_gathered(dmod_gathered, "sum_b_ada")[:depth]
    dmod_cols = lax.dynamic_slice_in_dim(dmod_all, chip * ada_cols, ada_cols, axis=2)
    dmod128 = jnp.pad(jnp.transpose(dmod_cols, (1, 0, 2)), ((0, 0), (0, LANES - N_DEV), (0, 0))).astype(BF16)
    cact128 = jnp.pad(cact, ((0, LANES - N_DEV), (0, 0))).astype(BF16)
    grad_w_ada, delta_w_ada, new_m_w_ada, new_v_w_ada = _adamw_ada(w_ada, m_w_ada, v_w_ada, cact128, dmod128,
                                                                  "adamw_w_ada")

    dlg_all = jnp.stack(d_lg)
    sig_f = jax.nn.sigmoid(-ret_decay_logit_f)
    sig_b = jax.nn.sigmoid(-ret_decay_logit_b)
    nlg = depth * cfg.hr
    pack = jnp.zeros((8, d), F32)
    for l in range(depth):
        pack = pack.at[l].set(d_gain[l])
    pack = pack.at[depth].set(fin_acc[0])
    pack = pack.at[depth + 1, 0].set(loss8[0, 0])
    pack = pack.at[depth + 1, LANES:LANES + nlg].set((dlg_all[:, :, 0] * sig_f).reshape(-1))
    pack = pack.at[depth + 1, 2 * LANES:2 * LANES + nlg].set((dlg_all[:, :, 1] * sig_b).reshape(-1))
    tot = _sum_gathered(_all_gather_small(pack, "gather_small"), "sum_small")
    grad_norm_gain = tot[:depth]
    grad_final_gain = tot[depth]
    loss = tot[depth + 1, 0]
    grad_lf = tot[depth + 1, LANES:LANES + nlg].reshape(depth, cfg.hr)
    grad_lb = tot[depth + 1, 2 * LANES:2 * LANES + nlg].reshape(depth, cfg.hr)

    d_ng, m_ng, v_ng = _adamw_small(norm_gain, grad_norm_gain, m_norm_gain, v_norm_gain, "adamw_norm_gain")
    d_ba, m_ba, v_ba = _adamw_small(b_ada, grad_b_ada, m_b_ada, v_b_ada, "adamw_b_ada")
    d_lf, m_lf, v_lf = _adamw_small(ret_decay_logit_f, grad_lf, m_ret_decay_logit_f, v_ret_decay_logit_f, "adamw_lf")
    d_lb, m_lb, v_lb = _adamw_small(ret_decay_logit_b, grad_lb, m_ret_decay_logit_b, v_ret_decay_logit_b, "adamw_lb")
    d_fg, m_fg, v_fg = _adamw_small(final_gain[None], grad_final_gain[None], m_final_gain[None], v_final_gain[None],
                                    "adamw_final_gain")

    return (loss, dx[None],
            grad_norm_gain, grad_w_ada, grad_b_ada, grad_w_in, grad_w_out, grad_lf, grad_lb, grad_final_gain,
            d_ng, delta_w_ada, d_ba, delta_w_in, delta_w_out, d_lf, d_lb, d_fg[0],
            m_ng, new_m_w_ada, m_ba, new_m_w_in, new_m_w_out, m_lf, m_lb, m_fg[0],
            v_ng, new_v_w_ada, v_ba, new_v_w_in, new_v_w_out, v_lf, v_lb, v_fg[0])
```

```python
import jax
import jax.numpy as jnp
from jax import lax
from jax.experimental import pallas as pl
from jax.experimental.pallas import tpu as pltpu

F32 = jnp.float32
BF16 = jnp.bfloat16

D_MODEL = 2048
SEQ = 4096
DEPTH = 4
HEAD_DIM = 128
DILATIONS = (1, 4, 16)
RADIUS = 64
N_HEADS_RET = 4
RET_QK = 128
RET_V = 256
RET_CHUNK = 256
NORM_EPS = 1e-6
MASK_VALUE = -1e30
N_DEV = 8
N_CHIP = 4
LANES = 128
VMEM_LIMIT = 56 * 1024 * 1024

ADAM_LR = 0.001
ADAM_B1 = 0.9
ADAM_B2 = 0.999
ADAM_EPS = 1e-08
ADAM_WD = 0.01
ADAM_STEP = 10

MESH = pl.DeviceIdType.MESH


class _Cfg:
    def __init__(self):
        self.d = D_MODEL
        self.s = SEQ
        self.aw = D_MODEL // 2
        self.ha = self.aw // HEAD_DIM
        self.rw = D_MODEL // 2
        self.hr = N_HEADS_RET
        self.rqk = self.hr * RET_QK
        self.f = 4 * self.aw + 2 * self.rqk + 2 * self.rw
        self.qa, self.ka, self.va, self.za = 0, self.aw, 2 * self.aw, 3 * self.aw
        self.qr = 4 * self.aw
        self.kr = self.qr + self.rqk
        self.vr = self.kr + self.rqk
        self.zr = self.vr + self.rw
        assert self.rw == self.hr * RET_V


def _tile(n, pref):
    t = min(n, pref)
    while n % t or t % LANES:
        t -= LANES
    return t


def _params(dims=None):
    return pltpu.CompilerParams(dimension_semantics=dims, vmem_limit_bytes=VMEM_LIMIT)


def _silu(z):
    return z * jax.nn.sigmoid(z)


def _dsilu(z):
    sg = jax.nn.sigmoid(z)
    return sg * (1.0 + z * (1.0 - sg))


_DN = {"nn": (((1,), (0,)), ((), ())), "nt": (((1,), (1,)), ((), ())), "tn": (((0,), (0,)), ((), ()))}


def _matmul(a, b, mode, out_dtype, name, tm=1024, tn=1024, tk=2048):
    if mode == "tn":
        kk, m = a.shape
    else:
        m, kk = a.shape
    n = b.shape[0] if mode == "nt" else b.shape[1]
    tm, tn, tk = _tile(m, tm), _tile(n, tn), _tile(kk, tk)
    nk = kk // tk
    a_spec = (pl.BlockSpec((tk, tm), lambda i, j, k: (k, i)) if mode == "tn"
              else pl.BlockSpec((tm, tk), lambda i, j, k: (i, k)))
    b_spec = (pl.BlockSpec((tn, tk), lambda i, j, k: (j, k)) if mode == "nt"
              else pl.BlockSpec((tk, tn), lambda i, j, k: (k, j)))
    dn = _DN[mode]

    def body(a_ref, b_ref, o_ref, *acc):
        p = lax.dot_general(a_ref[...], b_ref[...], dn, preferred_element_type=F32)
        if nk == 1:
            o_ref[...] = p.astype(out_dtype)
            return
        acc_ref, = acc
        k = pl.program_id(2)

        @pl.when(k == 0)
        def _():
            acc_ref[...] = p

        @pl.when(k > 0)
        def _():
            acc_ref[...] += p

        @pl.when(k == nk - 1)
        def _():
            o_ref[...] = acc_ref[...].astype(out_dtype)

    return pl.pallas_call(
        body, name=name, grid=(m // tm, n // tn, nk),
        in_specs=[a_spec, b_spec],
        out_specs=pl.BlockSpec((tm, tn), lambda i, j, k: (i, j)),
        out_shape=jax.ShapeDtypeStruct((m, n), out_dtype),
        scratch_shapes=[pltpu.VMEM((tm, tn), F32)] if nk > 1 else [],
        compiler_params=_params(("parallel", "parallel", "arbitrary")),
    )(a, b)


def _out_proj_fwd(y, w_out, x, mod3, b3, name):
    s, kk = y.shape
    d = w_out.shape[1]
    tm, tn = _tile(s, 256), d

    def body(y_ref, w_ref, x_ref, m_ref, b_ref, xn_ref, o_ref):
        out = jnp.dot(y_ref[...], w_ref[...], preferred_element_type=F32)
        gate = m_ref[2:3, :] + b_ref[2:3, :]
        xn_ref[...] = x_ref[...] + gate * out
        o_ref[...] = out.astype(BF16)

    vec = pl.BlockSpec((8, tn), lambda i, j: (0, j))
    return pl.pallas_call(
        body, name=name, grid=(s // tm, d // tn),
        in_specs=[pl.BlockSpec((tm, kk), lambda i, j: (i, 0)), pl.BlockSpec((kk, tn), lambda i, j: (0, j)),
                  pl.BlockSpec((tm, tn), lambda i, j: (i, j)), vec, vec],
        out_specs=[pl.BlockSpec((tm, tn), lambda i, j: (i, j)), pl.BlockSpec((tm, tn), lambda i, j: (i, j))],
        out_shape=[jax.ShapeDtypeStruct((s, d), F32), jax.ShapeDtypeStruct((s, d), BF16)],
        compiler_params=_params(("parallel", "parallel")),
    )(y, w_out, x, mod3, b3)


def _norm_mod_fwd(x, g8, mod3, b3, name):
    s, d = x.shape
    tr = _tile(s, 512)

    def body(x_ref, g_ref, m_ref, b_ref, h_ref):
        xv = x_ref[...]
        r = lax.rsqrt(jnp.mean(xv * xv, axis=-1, keepdims=True) + NORM_EPS)
        shift = m_ref[0:1, :] + b_ref[0:1, :]
        scale = m_ref[1:2, :] + b_ref[1:2, :]
        h_ref[...] = ((xv * r * g_ref[0:1, :]) * (1.0 + scale) + shift).astype(BF16)

    vec = pl.BlockSpec((8, d), lambda i: (0, 0))
    return pl.pallas_call(
        body, name=name, grid=(s // tr,),
        in_specs=[pl.BlockSpec((tr, d), lambda i: (i, 0)), vec, vec, vec],
        out_specs=pl.BlockSpec((tr, d), lambda i: (i, 0)),
        out_shape=jax.ShapeDtypeStruct((s, d), BF16),
        compiler_params=_params(("parallel",)),
    )(x, g8, mod3, b3)


def _norm_mod_bwd(dh, x, dx_next, g8, mod3, b3, name):
    s, d = x.shape
    tr = _tile(s, 256)

    def body(dh_ref, x_ref, dn_ref, g_ref, m_ref, b_ref, dx_ref, acc_ref):
        @pl.when(pl.program_id(0) == 0)
        def _():
            acc_ref[...] = jnp.zeros_like(acc_ref)

        xv = x_ref[...]
        dh_v = dh_ref[...]
        g = g_ref[0:1, :]
        r = lax.rsqrt(jnp.mean(xv * xv, axis=-1, keepdims=True) + NORM_EPS)
        xn = xv * r
        scale1 = 1.0 + m_ref[1:2, :] + b_ref[1:2, :]
        dhs = dh_v * scale1
        dxn = dhs * g
        dx_ref[...] = dn_ref[...] + r * (dxn - xn * jnp.mean(dxn * xn, axis=-1, keepdims=True))
        acc_ref[0:1, :] += jnp.sum(dh_v, axis=0, keepdims=True)
        acc_ref[1:2, :] += jnp.sum(dh_v * (xn * g), axis=0, keepdims=True)
        acc_ref[2:3, :] += jnp.sum(dhs * xn, axis=0, keepdims=True)

    vec = pl.BlockSpec((8, d), lambda i: (0, 0))
    row = pl.BlockSpec((tr, d), lambda i: (i, 0))
    return pl.pallas_call(
        body, name=name, grid=(s // tr,),
        in_specs=[row, row, row, vec, vec, vec],
        out_specs=[row, vec],
        out_shape=[jax.ShapeDtypeStruct((s, d), F32), jax.ShapeDtypeStruct((8, d), F32)],
        compiler_params=_params(("arbitrary",)),
    )(dh, x, dx_next, g8, mod3, b3)


def _final_loss(x, tgt, g8, name):
    s, d = x.shape
    tr = _tile(s, 256)

    def body(x_ref, t_ref, g_ref, dx_ref, loss_ref, acc_ref):
        @pl.when(pl.program_id(0) == 0)
        def _():
            acc_ref[...] = jnp.zeros_like(acc_ref)
            loss_ref[...] = jnp.zeros_like(loss_ref)

        xv = x_ref[...]
        g = g_ref[0:1, :]
        r = lax.rsqrt(jnp.mean(xv * xv, axis=-1, keepdims=True) + NORM_EPS)
        xn = xv * r
        err = xn * g - t_ref[...]
        loss_ref[...] += 0.5 * jnp.sum(jnp.mean(err * err, axis=-1, keepdims=True), axis=0, keepdims=True)
        dy = err * (1.0 / d)
        acc_ref[0:1, :] += jnp.sum(dy * xn, axis=0, keepdims=True)
        dxn = dy * g
        dx_ref[...] = r * (dxn - xn * jnp.mean(dxn * xn, axis=-1, keepdims=True))

    vec = pl.BlockSpec((8, d), lambda i: (0, 0))
    row = pl.BlockSpec((tr, d), lambda i: (i, 0))
    return pl.pallas_call(
        body, name=name, grid=(s // tr,),
        in_specs=[row, row, vec],
        out_specs=[row, pl.BlockSpec((8, LANES), lambda i: (0, 0)), vec],
        out_shape=[jax.ShapeDtypeStruct((s, d), F32), jax.ShapeDtypeStruct((8, LANES), F32),
                   jax.ShapeDtypeStruct((8, d), F32)],
        compiler_params=_params(("arbitrary",)),
    )(x, tgt, g8)


def _out_proj_bwd_prep(dxn, out, mod3, b3, name):
    s, d = dxn.shape
    tr = _tile(s, 512)

    def body(dx_ref, o_ref, m_ref, b_ref, do_ref, acc_ref):
        @pl.when(pl.program_id(0) == 0)
        def _():
            acc_ref[...] = jnp.zeros_like(acc_ref)

        dxv = dx_ref[...]
        gate = m_ref[2:3, :] + b_ref[2:3, :]
        do_ref[...] = (gate * dxv).astype(BF16)
        acc_ref[0:1, :] += jnp.sum(dxv * o_ref[...].astype(F32), axis=0, keepdims=True)

    vec = pl.BlockSpec((8, d), lambda i: (0, 0))
    row = pl.BlockSpec((tr, d), lambda i: (i, 0))
    return pl.pallas_call(
        body, name=name, grid=(s // tr,),
        in_specs=[row, row, vec, vec],
        out_specs=[row, vec],
        out_shape=[jax.ShapeDtypeStruct((s, d), BF16), jax.ShapeDtypeStruct((8, d), F32)],
        compiler_params=_params(("arbitrary",)),
    )(dxn, out, mod3, b3)


_COPY_ROWS = 512


_ATTN_WIN = 384


def _attn_geometry(cfg, dil):
    sub = cfg.s // dil
    if sub <= _ATTN_WIN:
        return sub, sub, sub
    return sub, _ATTN_WIN - 2 * RADIUS, _ATTN_WIN


_N_SHIFTS = 3


def _attn_rows(dil, r, i, sub, bq, win):
    margin = (win - bq) // 2
    ws = jnp.clip(i * bq - margin, 0, sub - win)
    shift = (i * bq - ws) // margin if margin else 0
    if dil == 1:
        return (shift, pl.ds(pl.multiple_of(i * bq, bq), bq), pl.ds(pl.multiple_of(ws, RADIUS), win))
    return (shift, pl.ds(r + i * (bq * dil), bq, stride=dil), pl.ds(r + ws * dil, win, stride=dil))


def _fill_bias_tables(cfg, bias_ref, slope):
    for pattern, dil in enumerate(DILATIONS):
        _, bq, win = _attn_geometry(cfg, dil)
        margin = (win - bq) // 2
        rel0 = lax.broadcasted_iota(jnp.int32, (bq, win), 1) - lax.broadcasted_iota(jnp.int32, (bq, win), 0)
        for shift in range(_N_SHIFTS if margin else 1):
            arel = jnp.abs(rel0 - shift * margin)
            bias_ref[pattern * _N_SHIFTS + shift, 0:bq, 0:win] = jnp.where(
                arel <= RADIUS, -(slope * dil) * arel.astype(F32), MASK_VALUE)


def _bias_scratch():
    return pltpu.VMEM((len(DILATIONS) * _N_SHIFTS, _ATTN_WIN - 2 * RADIUS, _ATTN_WIN), F32)


def _attn_scores(cfg, bias_ref, q, kw, pattern, shift):
    bq, win = q.shape[0], kw.shape[0]
    return (lax.dot_general(q, kw, _DN["nt"], preferred_element_type=F32)
            + bias_ref[pattern * _N_SHIFTS + shift, 0:bq, 0:win])


def _for_each_group(cfg, group, size):
    for pattern, dil in reversed(list(enumerate(DILATIONS))):
        first = pattern == len(DILATIONS) - 1
        sub, bq, _ = _attn_geometry(cfg, dil)
        nblk = sub // bq
        per = min(size, nblk)
        for r in range(dil):
            if nblk == per:
                group(pattern, dil, first, [(r, i) for i in range(nblk)])
            else:
                def step(g, carry, pattern=pattern, dil=dil, first=first, r=r, per=per):
                    group(pattern, dil, first, [(r, g * per + j) for j in range(per)])
                    return carry

                lax.fori_loop(0, nblk // per, step, 0)


def _attn_fwd(cfg, proj, slopes, name):
    s = cfg.s
    scale = HEAD_DIM ** -0.5

    def body(sl_ref, q_ref, k_ref, v_ref, z_ref, o_ref, lse_ref, y_ref, qf, kf, vf, acc, m_s, l_s, bias_s):
        slope = sl_ref[pl.program_id(0)]

        def to_f32(i, carry):
            rows = pl.ds(pl.multiple_of(i * _COPY_ROWS, _COPY_ROWS), _COPY_ROWS)
            qf[rows, :] = q_ref[rows, :].astype(F32) * scale
            kf[rows, :] = k_ref[rows, :].astype(F32)
            vf[rows, :] = v_ref[rows, :].astype(F32)
            return carry

        lax.fori_loop(0, s // _COPY_ROWS, to_f32, 0)

        _fill_bias_tables(cfg, bias_s, slope)

        def group(pattern, dil, first, blocks):
            sub, bq, win = _attn_geometry(cfg, dil)
            rep = win // HEAD_DIM
            work = []
            for r, i in blocks:
                shift, qrows, krows = _attn_rows(dil, r, i, sub, bq, win)
                old = None if first else (m_s[qrows, :], l_s[qrows, :], acc[qrows, :])
                work.append((shift, qrows, qf[qrows, :].astype(BF16), kf[krows, :].astype(BF16),
                             vf[krows, :].astype(BF16), old))
            new = []
            for shift, qrows, q, kw, vw, old in work:
                sc = _attn_scores(cfg, bias_s, q, kw, pattern, shift)
                m_blk = jnp.max(sc, axis=-1, keepdims=True)
                if first:
                    m_new = jnp.broadcast_to(m_blk, (bq, HEAD_DIM))
                    p = jnp.exp(sc - m_blk)
                    l_new = jnp.broadcast_to(jnp.sum(p, axis=-1, keepdims=True), (bq, HEAD_DIM))
                    a_new = jnp.dot(p.astype(BF16), vw, preferred_element_type=F32)
                else:
                    m_old, l_old, a_old = old
                    m_new = jnp.maximum(m_old, m_blk)
                    alpha = jnp.exp(m_old - m_new)
                    p = jnp.exp(sc - jnp.tile(m_new, (1, rep)))
                    l_new = alpha * l_old + jnp.sum(p, axis=-1, keepdims=True)
                    a_new = alpha * a_old + jnp.dot(p.astype(BF16), vw, preferred_element_type=F32)
                new.append((qrows, m_new, l_new, a_new))
            for qrows, m_new, l_new, a_new in new:
                m_s[qrows, :] = m_new
                l_s[qrows, :] = l_new
                acc[qrows, :] = a_new

        _for_each_group(cfg, group, size=8)

        def finish(i, carry):
            rows = pl.ds(pl.multiple_of(i * _COPY_ROWS, _COPY_ROWS), _COPY_ROWS)
            den = l_s[rows, :]
            o = (acc[rows, :] / den).astype(BF16)
            o_ref[rows, :] = o
            lse_ref[rows, :] = m_s[rows, :] + jnp.log(den)
            y_ref[rows, :] = (o.astype(F32) * _silu(z_ref[rows, :].astype(F32))).astype(BF16)
            return carry

        lax.fori_loop(0, s // _COPY_ROWS, finish, 0)

    def col(off):
        return pl.BlockSpec((s, HEAD_DIM), lambda h: (0, off // HEAD_DIM + h))

    head = pl.BlockSpec((s, HEAD_DIM), lambda h: (0, h))
    return pl.pallas_call(
        body, name=name, grid=(cfg.ha,),
        in_specs=[pl.BlockSpec(memory_space=pltpu.SMEM), col(cfg.qa), col(cfg.ka), col(cfg.va), col(cfg.za)],
        out_specs=[head, head, head],
        out_shape=[jax.ShapeDtypeStruct((s, cfg.aw), BF16), jax.ShapeDtypeStruct((s, cfg.aw), F32),
                   jax.ShapeDtypeStruct((s, cfg.d), BF16)],
        scratch_shapes=[pltpu.VMEM((s, HEAD_DIM), F32)] * 6 + [_bias_scratch()],
        compiler_params=_params(("parallel",)),
    )(slopes, proj, proj, proj, proj)


def _attn_bwd(cfg, proj, slopes, dy, oa, lse, name):
    s = cfg.s
    scale = HEAD_DIM ** -0.5
    dst_blocks = [c0 // HEAD_DIM for c0 in (cfg.za, cfg.qa, cfg.ka, cfg.va)]
    sub0, _, win0 = _attn_geometry(cfg, DILATIONS[-1])
    assign_first = sub0 == win0

    def body(sl_ref, q_ref, k_ref, v_ref, z_ref, dy_ref, o_ref, lse_ref, dproj_ref,
             qf, kf, vf, dof, dlt, dqa, dka, dva, bias_s, stage, sems):
        head = pl.program_id(0)
        slope = sl_ref[head]

        def out_copy(slot):
            cols = pl.ds(pl.multiple_of((dst_blocks[slot] + head) * HEAD_DIM, HEAD_DIM), HEAD_DIM)
            return pltpu.make_async_copy(stage.at[slot], dproj_ref.at[:, cols], sems.at[slot])

        @pl.when(head > 0)
        def _():
            out_copy(0).wait()

        def to_f32(i, carry):
            rows = pl.ds(pl.multiple_of(i * _COPY_ROWS, _COPY_ROWS), _COPY_ROWS)
            qf[rows, :] = q_ref[rows, :].astype(F32) * scale
            kf[rows, :] = k_ref[rows, :].astype(F32)
            vf[rows, :] = v_ref[rows, :].astype(F32)
            dyv = dy_ref[rows, :].astype(F32)
            zv = z_ref[rows, :].astype(F32)
            ov = o_ref[rows, :].astype(F32)
            dov = dyv * _silu(zv)
            dof[rows, :] = dov
            dlt[rows, :] = jnp.broadcast_to(jnp.sum(dov * ov, axis=-1, keepdims=True), (_COPY_ROWS, HEAD_DIM))
            stage[0, rows, :] = (dyv * ov * _dsilu(zv)).astype(BF16)
            if not assign_first:
                zero = jnp.zeros((_COPY_ROWS, HEAD_DIM), F32)
                dqa[rows, :] = zero
                dka[rows, :] = zero
                dva[rows, :] = zero
            return carry

        lax.fori_loop(0, s // _COPY_ROWS, to_f32, 0)
        out_copy(0).start()

        _fill_bias_tables(cfg, bias_s, slope)

        def group(pattern, dil, first, blocks):
            sub, bq, win = _attn_geometry(cfg, dil)
            rep = win // HEAD_DIM
            assign = first and assign_first
            work = []
            for r, i in blocks:
                shift, qrows, krows = _attn_rows(dil, r, i, sub, bq, win)
                work.append((shift, qrows, krows, qf[qrows, :].astype(BF16), kf[krows, :].astype(BF16),
                             vf[krows, :].astype(BF16), dof[qrows, :].astype(BF16),
                             lse_ref[qrows, :], dlt[qrows, :]))
            new = []
            for shift, qrows, krows, q, kw, vw, dob, lse_b, dlt_b in work:
                sc = _attn_scores(cfg, bias_s, q, kw, pattern, shift)
                p = jnp.exp(sc - jnp.tile(lse_b, (1, rep)))
                dp = lax.dot_general(dob, vw, _DN["nt"], preferred_element_type=F32)
                ds = (p * (dp - jnp.tile(dlt_b, (1, rep)))).astype(BF16)
                new.append((qrows, krows,
                            jnp.dot(ds, kw, preferred_element_type=F32) * scale,
                            lax.dot_general(ds, q, _DN["tn"], preferred_element_type=F32),
                            lax.dot_general(p.astype(BF16), dob, _DN["tn"], preferred_element_type=F32)))
            for qrows, krows, dq_b, dk_b, dv_b in new:
                if assign:
                    dqa[qrows, :] = dq_b
                    dka[krows, :] = dk_b
                    dva[krows, :] = dv_b
                else:
                    dqa[qrows, :] += dq_b
                    dka[krows, :] += dk_b
                    dva[krows, :] += dv_b

        _for_each_group(cfg, group, size=8)

        @pl.when(head > 0)
        def _():
            for slot in (1, 2, 3):
                out_copy(slot).wait()

        def emit(i, carry):
            rows = pl.ds(pl.multiple_of(i * _COPY_ROWS, _COPY_ROWS), _COPY_ROWS)
            stage[1, rows, :] = dqa[rows, :].astype(BF16)
            stage[2, rows, :] = dka[rows, :].astype(BF16)
            stage[3, rows, :] = dva[rows, :].astype(BF16)
            return carry

        lax.fori_loop(0, s // _COPY_ROWS, emit, 0)
        for slot in (1, 2, 3):
            out_copy(slot).start()

        @pl.when(head == pl.num_programs(0) - 1)
        def _():
            for slot in range(4):
                out_copy(slot).wait()

    def col(off):
        return pl.BlockSpec((s, HEAD_DIM), lambda h: (0, off // HEAD_DIM + h))

    head_cols = pl.BlockSpec((s, HEAD_DIM), lambda h: (0, h))
    return pl.pallas_call(
        body, name=name, grid=(cfg.ha,),
        in_specs=[pl.BlockSpec(memory_space=pltpu.SMEM), col(cfg.qa), col(cfg.ka), col(cfg.va), col(cfg.za),
                  head_cols, head_cols, head_cols],
        out_specs=_ANY,
        out_shape=jax.ShapeDtypeStruct((s, cfg.f), BF16),
        scratch_shapes=[pltpu.VMEM((s, HEAD_DIM), F32)] * 8
        + [_bias_scratch(), pltpu.VMEM((4, s, HEAD_DIM), BF16), pltpu.SemaphoreType.DMA((4,))],
        compiler_params=_params(("arbitrary",)),
    )(slopes, proj, proj, proj, proj, dy, oa, lse)


def _decay_tables(lg, backward):
    c = RET_CHUNK
    a = lax.broadcasted_iota(jnp.int32, (c, c), 0)
    b = lax.broadcasted_iota(jnp.int32, (c, c), 1)
    idx = lax.broadcasted_iota(jnp.int32, (c, 1), 0).astype(F32)
    if backward:
        rel = (b - a).astype(F32)
        ex_xi = c - idx
        ex_zeta = idx
    else:
        rel = (a - b).astype(F32)
        ex_xi = idx + 1.0
        ex_zeta = c - 1.0 - idx
    relc = jnp.maximum(rel, 0.0)
    dm = jnp.where(rel >= 0, jnp.exp(relc * lg), 0.0)
    xi = jnp.exp(ex_xi * lg)
    zeta = jnp.exp(ex_zeta * lg)
    gch = jnp.exp(jnp.full((1, 1), c, F32) * lg)
    return relc, dm, xi, zeta, ex_xi, ex_zeta, gch


def _ret_fwd(cfg, proj, lgs, y, name):
    s = cfg.s
    c = RET_CHUNK
    n = s // c
    kscale = RET_QK ** -0.5

    def body(lg_ref, q_ref, k_ref, v_ref, z_ref, y_in, o_ref, y_ref, st_ref):
        h = pl.program_id(0)
        tabs = [_decay_tables(lg_ref[dirn, h], dirn == 1) for dirn in range(2)]
        st_ref[...] = jnp.zeros_like(st_ref)
        o_ref[...] = jnp.zeros_like(o_ref)

        def step(t, carry):
            for dirn in range(2):
                _, dm, xi, zeta, _, _, gch = tabs[dirn]
                i = (n - 1 - t) if dirn == 1 else t
                rows = pl.ds(pl.multiple_of(i * c, c), c)
                qi = q_ref[rows, :]
                ks = k_ref[rows, :].astype(F32) * kscale
                vi = v_ref[rows, :]
                inner = lax.dot_general(qi, ks.astype(BF16), _DN["nt"], preferred_element_type=F32) * dm
                st = st_ref[dirn]
                o_ref[rows, :] += (jnp.dot(inner.astype(BF16), vi, preferred_element_type=F32)
                                   + jnp.dot(qi, st.astype(BF16), preferred_element_type=F32) * xi)
                st_ref[dirn] = st * gch + lax.dot_general((ks * zeta).astype(BF16), vi, _DN["tn"],
                                                          preferred_element_type=F32)
            return carry

        lax.fori_loop(0, n, step, 0, unroll=2)

        def gate(i, carry):
            rows = pl.ds(pl.multiple_of(i * _COPY_ROWS, _COPY_ROWS), _COPY_ROWS)
            oh = o_ref[rows, :]
            rr = lax.rsqrt(jnp.mean(oh * oh, axis=-1, keepdims=True) + NORM_EPS)
            y_ref[rows, :] = (oh * rr * _silu(z_ref[rows, :].astype(F32))).astype(BF16)
            return carry

        lax.fori_loop(0, s // _COPY_ROWS, gate, 0)

    return pl.pallas_call(
        body, name=name, grid=(cfg.hr,),
        in_specs=[pl.BlockSpec(memory_space=pltpu.SMEM),
                  pl.BlockSpec((s, RET_QK), lambda h: (0, cfg.qr // RET_QK + h)),
                  pl.BlockSpec((s, RET_QK), lambda h: (0, cfg.kr // RET_QK + h)),
                  pl.BlockSpec((s, RET_V), lambda h: (0, cfg.vr // RET_V + h)),
                  pl.BlockSpec((s, RET_V), lambda h: (0, cfg.zr // RET_V + h)), _ANY],
        out_specs=[pl.BlockSpec((s, RET_V), lambda h: (0, h)),
                   pl.BlockSpec((s, RET_V), lambda h: (0, cfg.aw // RET_V + h))],
        out_shape=[jax.ShapeDtypeStruct((s, cfg.rw), F32), jax.ShapeDtypeStruct(y.shape, BF16)],
        input_output_aliases={5: 1},
        scratch_shapes=[pltpu.VMEM((2, RET_QK, RET_V), F32)],
        compiler_params=_params(("parallel",)),
    )(lgs, proj, proj, proj, proj, y)


def _ret_bwd(cfg, proj, lgs, dy, oret, dproj, name):
    s = cfg.s
    c = RET_CHUNK
    n = s // c
    kscale = RET_QK ** -0.5

    def accumulate(lg_ref, q_ref, k_ref, v_ref, do_ref, dq_ref, dk_ref, dv_ref, dlg_ref, states, t_ref,
                   e_dm, e_xi, e_zeta, e_g):
        h = pl.program_id(0)
        tabs = [_decay_tables(lg_ref[dirn, h], dirn == 1) for dirn in range(2)]
        dlg_ref[...] = jnp.zeros_like(dlg_ref)
        dq_ref[...] = jnp.zeros_like(dq_ref)
        dk_ref[...] = jnp.zeros_like(dk_ref)
        dv_ref[...] = jnp.zeros_like(dv_ref)

        def chunk_rows(dirn, t):
            i = (n - 1 - t) if dirn == 1 else t
            return pl.ds(pl.multiple_of(i * c, c), c)

        t_ref[...] = jnp.zeros_like(t_ref)

        def fwd_step(t, carry):
            for dirn in range(2):
                _, _, _, zeta, _, _, gch = tabs[dirn]
                rows = chunk_rows(dirn, t)
                st = t_ref[dirn]
                states[dirn, t] = st
                ks = k_ref[rows, :].astype(F32) * kscale
                t_ref[dirn] = st * gch + lax.dot_general((ks * zeta).astype(BF16), v_ref[rows, :], _DN["tn"],
                                                         preferred_element_type=F32)
            return carry

        lax.fori_loop(0, n, fwd_step, 0, unroll=2)
        t_ref[...] = jnp.zeros_like(t_ref)

        for ref in (e_dm, e_xi, e_zeta, e_g):
            ref[...] = jnp.zeros_like(ref)

        def bwd_step(u, carry):
            t = n - 1 - u
            for dirn in range(2):
                relc, dm, xi, zeta, ex_xi, ex_zeta, gch = tabs[dirn]
                rows = chunk_rows(dirn, t)
                qi = q_ref[rows, :]
                ks = k_ref[rows, :].astype(F32) * kscale
                ksb = ks.astype(BF16)
                vi = v_ref[rows, :]
                doi = do_ref[rows, :]
                sn_f = states[dirn, t]
                sn = sn_f.astype(BF16)
                tt = t_ref[dirn]
                ttb = tt.astype(BF16)
                a_mat = lax.dot_general(qi, ksb, _DN["nt"], preferred_element_type=F32) * dm
                dov = lax.dot_general(doi, vi, _DN["nt"], preferred_element_type=F32)
                b_mat = (dov * dm).astype(BF16)
                kz = (ks * zeta).astype(BF16)
                d_v = (jnp.dot(kz, ttb, preferred_element_type=F32)
                       + lax.dot_general(a_mat.astype(BF16), doi, _DN["tn"], preferred_element_type=F32))
                dk_inter = lax.dot_general(vi, ttb, _DN["nt"], preferred_element_type=F32) * zeta
                d_k = lax.dot_general(b_mat, qi, _DN["tn"], preferred_element_type=F32) + dk_inter
                o_inter = jnp.dot(qi, sn, preferred_element_type=F32) * xi
                d_q = (jnp.dot(b_mat, ksb, preferred_element_type=F32)
                       + lax.dot_general(doi, sn, _DN["nt"], preferred_element_type=F32) * xi)
                e_dm[dirn] += relc * a_mat * dov
                e_xi[dirn] += ex_xi * (doi.astype(F32) * o_inter)
                e_zeta[dirn] += ex_zeta * (ks * dk_inter)
                e_g[dirn] += tt * sn_f
                t_ref[dirn] = tt * gch + lax.dot_general((qi.astype(F32) * xi).astype(BF16), doi, _DN["tn"],
                                                         preferred_element_type=F32)
                dq_ref[rows, :] += d_q
                dk_ref[rows, :] += d_k * kscale
                dv_ref[rows, :] += d_v
            return carry

        lax.fori_loop(0, n, bwd_step, 0, unroll=2)
        for dirn in range(2):
            total = (jnp.sum(e_dm[dirn], keepdims=True) + jnp.sum(e_xi[dirn], keepdims=True)
                     + jnp.sum(e_zeta[dirn], keepdims=True) + (c * tabs[dirn][6]) * jnp.sum(e_g[dirn], keepdims=True))
            dlg_ref[0, dirn:dirn + 1, :] = jnp.broadcast_to(total, (1, LANES))

    def body(lg_ref, q_ref, k_ref, v_ref, z_ref, dy_ref, or_ref, dproj_in, dproj_ref, dlg_ref,
             do_s, dq_s, dk_s, dv_s, stage_z, stage_q, stage_k, stage_v, sems, *scratch):
        head = pl.program_id(0)
        stages = (stage_z, stage_q, stage_k, stage_v)
        firsts = (cfg.zr, cfg.qr, cfg.kr, cfg.vr)

        def out_copy(slot):
            width = stages[slot].shape[1]
            cols = pl.ds(pl.multiple_of(firsts[slot] + head * width, LANES), width)
            return pltpu.make_async_copy(stages[slot], dproj_ref.at[:, cols], sems.at[slot])

        @pl.when(head > 0)
        def _():
            out_copy(0).wait()

        def gate_norm_bwd(i, carry):
            rows = pl.ds(pl.multiple_of(i * _COPY_ROWS, _COPY_ROWS), _COPY_ROWS)
            oh = or_ref[rows, :]
            zr = z_ref[rows, :].astype(F32)
            dyr = dy_ref[rows, :].astype(F32)
            rr = lax.rsqrt(jnp.mean(oh * oh, axis=-1, keepdims=True) + NORM_EPS)
            yn = oh * rr
            dyn = dyr * _silu(zr)
            stage_z[rows, :] = (dyr * yn * _dsilu(zr)).astype(BF16)
            do_s[rows, :] = (rr * (dyn - yn * jnp.mean(dyn * yn, axis=-1, keepdims=True))).astype(BF16)
            return carry

        lax.fori_loop(0, s // _COPY_ROWS, gate_norm_bwd, 0)
        out_copy(0).start()

        accumulate(lg_ref, q_ref, k_ref, v_ref, do_s, dq_s, dk_s, dv_s, dlg_ref, *scratch)

        @pl.when(head > 0)
        def _():
            for slot in (1, 2, 3):
                out_copy(slot).wait()

        def emit(i, carry):
            rows = pl.ds(pl.multiple_of(i * _COPY_ROWS, _COPY_ROWS), _COPY_ROWS)
            stage_q[rows, :] = dq_s[rows, :].astype(BF16)
            stage_k[rows, :] = dk_s[rows, :].astype(BF16)
            stage_v[rows, :] = dv_s[rows, :].astype(BF16)
            return carry

        lax.fori_loop(0, s // _COPY_ROWS, emit, 0)
        for slot in (1, 2, 3):
            out_copy(slot).start()

        @pl.when(head == pl.num_programs(0) - 1)
        def _():
            for slot in range(4):
                out_copy(slot).wait()

    return pl.pallas_call(
        body, name=name, grid=(cfg.hr,),
        in_specs=[pl.BlockSpec(memory_space=pltpu.SMEM),
                  pl.BlockSpec((s, RET_QK), lambda h: (0, cfg.qr // RET_QK + h)),
                  pl.BlockSpec((s, RET_QK), lambda h: (0, cfg.kr // RET_QK + h)),
                  pl.BlockSpec((s, RET_V), lambda h: (0, cfg.vr // RET_V + h)),
                  pl.BlockSpec((s, RET_V), lambda h: (0, cfg.zr // RET_V + h)),
                  pl.BlockSpec((s, RET_V), lambda h: (0, cfg.aw // RET_V + h)),
                  pl.BlockSpec((s, RET_V), lambda h: (0, h)), _ANY],
        out_specs=[_ANY, pl.BlockSpec((1, 8, LANES), lambda h: (h, 0, 0))],
        out_shape=[jax.ShapeDtypeStruct(dproj.shape, BF16), jax.ShapeDtypeStruct((cfg.hr, 8, LANES), F32)],
        input_output_aliases={7: 0},
        scratch_shapes=[pltpu.VMEM((s, RET_V), BF16),
                        pltpu.VMEM((s, RET_QK), F32), pltpu.VMEM((s, RET_QK), F32), pltpu.VMEM((s, RET_V), F32),
                        pltpu.VMEM((s, RET_V), BF16), pltpu.VMEM((s, RET_QK), BF16), pltpu.VMEM((s, RET_QK), BF16),
                        pltpu.VMEM((s, RET_V), BF16), pltpu.SemaphoreType.DMA((4,)),
                        pltpu.VMEM((2, n, RET_QK, RET_V), F32), pltpu.VMEM((2, RET_QK, RET_V), F32),
                        pltpu.VMEM((2, c, c), F32), pltpu.VMEM((2, c, RET_V), F32),
                        pltpu.VMEM((2, c, RET_QK), F32), pltpu.VMEM((2, RET_QK, RET_V), F32)],
        compiler_params=_params(("arbitrary",)),
    )(lgs, proj, proj, proj, proj, dy, oret, dproj)


_ADA_ROWS = 16


def _ada_fwd(cact, w_ada, name):
    depth, d, n = w_ada.shape
    rows = cact.shape[0]
    tn = _tile(n, 768)

    def body(c_ref, w_ref, o_ref):
        o_ref[0] = jnp.dot(c_ref[...], w_ref[0].astype(BF16), preferred_element_type=F32)

    return pl.pallas_call(
        body, name=name, grid=(depth, n // tn),
        in_specs=[pl.BlockSpec((rows, d), lambda l, j: (0, 0)), pl.BlockSpec((1, d, tn), lambda l, j: (l, 0, j))],
        out_specs=pl.BlockSpec((1, rows, tn), lambda l, j: (l, 0, j)),
        out_shape=jax.ShapeDtypeStruct((depth, rows, n), F32),
        compiler_params=_params(("parallel", "parallel")),
    )(cact, w_ada)


def _adam_math(w, g, m, v):
    m2 = ADAM_B1 * m + (1.0 - ADAM_B1) * g
    v2 = ADAM_B2 * v + (1.0 - ADAM_B2) * (g * g)
    m_hat = m2 / (1.0 - ADAM_B1 ** ADAM_STEP)
    v_hat = v2 / (1.0 - ADAM_B2 ** ADAM_STEP)
    delta = -ADAM_LR * (m_hat / (jnp.sqrt(v_hat) + ADAM_EPS) + ADAM_WD * w)
    return delta, m2, v2


def _adamw_big(w, g, m, v, first, count, prev, name):
    _, r, c = w.shape
    tr, tc = _tile(r, 512), _tile(c, 1024)

    def body(w_ref, g_ref, m_ref, v_ref, *rest):
        go_ref, d_ref, mo_ref, vo_ref = rest[-4:]
        gv = g_ref[...]
        delta, m2, v2 = _adam_math(w_ref[...], gv, m_ref[...], v_ref[...])
        go_ref[...] = gv
        d_ref[...] = delta
        mo_ref[...] = m2
        vo_ref[...] = v2

    spec = pl.BlockSpec((1, tr, tc), lambda l, i, j: (first + l, i, j))
    shp = jax.ShapeDtypeStruct(w.shape, F32)
    carried = [] if prev is None else list(prev)
    return pl.pallas_call(
        body, name=name, grid=(count, r // tr, c // tc),
        in_specs=[spec] * 4 + [_ANY] * len(carried), out_specs=[spec] * 4, out_shape=[shp] * 4,
        input_output_aliases={4 + k: k for k in range(len(carried))},
        compiler_params=_params(("parallel", "parallel", "parallel")),
    )(w, g, m, v, *carried)


def _adamw_ada(w, m, v, cact, dmod, name):
    depth, r, c = w.shape
    kk = cact.shape[0]
    tr, tc = _tile(r, 512), _tile(c, 768)

    def body(w_ref, m_ref, v_ref, c_ref, dm_ref, go_ref, d_ref, mo_ref, vo_ref):
        gv = lax.dot_general(c_ref[...], dm_ref[0], _DN["tn"], preferred_element_type=F32)
        delta, m2, v2 = _adam_math(w_ref[0], gv, m_ref[0], v_ref[0])
        go_ref[0] = gv
        d_ref[0] = delta
        mo_ref[0] = m2
        vo_ref[0] = v2

    spec = pl.BlockSpec((1, tr, tc), lambda l, i, j: (l, i, j))
    shp = jax.ShapeDtypeStruct(w.shape, F32)
    return pl.pallas_call(
        body, name=name, grid=(depth, r // tr, c // tc),
        in_specs=[spec] * 3 + [pl.BlockSpec((kk, tr), lambda l, i, j: (0, i)),
                               pl.BlockSpec((1, kk, tc), lambda l, i, j: (l, 0, j))],
        out_specs=[spec] * 4, out_shape=[shp] * 4,
        compiler_params=_params(("parallel", "parallel", "parallel")),
    )(w, m, v, cact, dmod)


def _adamw_small(w, g, m, v, name):
    def body(w_ref, g_ref, m_ref, v_ref, d_ref, mo_ref, vo_ref):
        delta, m2, v2 = _adam_math(w_ref[...], g_ref[...], m_ref[...], v_ref[...])
        d_ref[...] = delta
        mo_ref[...] = m2
        vo_ref[...] = v2

    shp = jax.ShapeDtypeStruct(w.shape, F32)
    return pl.pallas_call(body, name=name, out_shape=[shp] * 3)(w, g, m, v)


def _sum_gathered(parts, name):
    nd, r, c = parts.shape

    def body(p_ref, o_ref):
        acc = p_ref[0]
        for e in range(1, nd):
            acc = acc + p_ref[e]
        o_ref[...] = acc

    return pl.pallas_call(body, name=name, out_shape=jax.ShapeDtypeStruct((r, c), F32))(parts)


def _flip(v, bit):
    return 1 - v if bit else v


def _all_gather_small(x, name):
    r, c = x.shape

    def body(x_ref, out_ref, send_sems, recv_sems, local_sem):
        mx, my, mc = lax.axis_index("x"), lax.axis_index("y"), lax.axis_index("c")
        me = 4 * mx + 2 * my + mc
        mine = pltpu.make_async_copy(x_ref, out_ref.at[me], local_sem)
        mine.start()
        sends = []
        for k in range(1, N_DEV):
            peer = (_flip(mx, k & 4), _flip(my, k & 2), _flip(mc, k & 1))
            cp = pltpu.make_async_remote_copy(src_ref=x_ref, dst_ref=out_ref.at[me], send_sem=send_sems.at[k - 1],
                                              recv_sem=recv_sems.at[k - 1], device_id=peer, device_id_type=MESH)
            cp.start()
            sends.append(cp)
        for k in range(1, N_DEV):
            peer = (_flip(mx, k & 4), _flip(my, k & 2), _flip(mc, k & 1))
            src = 4 * peer[0] + 2 * peer[1] + peer[2]
            pltpu.make_async_remote_copy(src_ref=x_ref, dst_ref=out_ref.at[src], send_sem=send_sems.at[k - 1],
                                         recv_sem=recv_sems.at[k - 1], device_id=peer,
                                         device_id_type=MESH).wait_recv()
        for cp in sends:
            cp.wait_send()
        mine.wait()

    return pl.pallas_call(
        body, name=name,
        out_shape=jax.ShapeDtypeStruct((N_DEV, r, c), x.dtype),
        in_specs=[pl.BlockSpec(memory_space=pltpu.VMEM)],
        out_specs=pl.BlockSpec(memory_space=pltpu.VMEM),
        scratch_shapes=[pltpu.SemaphoreType.DMA((N_DEV - 1,)), pltpu.SemaphoreType.DMA((N_DEV - 1,)),
                        pltpu.SemaphoreType.DMA],
        compiler_params=pltpu.CompilerParams(vmem_limit_bytes=VMEM_LIMIT),
    )(x)


_HBM = pl.BlockSpec(memory_space=pltpu.HBM)
_SEM = pl.BlockSpec(memory_space=pltpu.SEMAPHORE)
_ANY = pl.BlockSpec(memory_space=pl.ANY)
_EFFECT = pltpu.SideEffectType.DATAFLOW_SIDE_EFFECTING


def _in_hbm(a):
    return pltpu.with_memory_space_constraint(a, pltpu.HBM)


def _place_w_in(w, layer, chip1, name):
    _, d, fc = w.shape
    tr = _tile(d, 512)

    def body(c_ref, w_ref, o_ref):
        o_ref[...] = w_ref[...].astype(BF16)

    return pl.pallas_call(
        body, name=name,
        grid_spec=pltpu.PrefetchScalarGridSpec(
            num_scalar_prefetch=1, grid=(d // tr,),
            in_specs=[pl.BlockSpec((None, tr, fc), lambda i, c: (layer, i, 0))],
            out_specs=pl.BlockSpec((tr, fc), lambda i, c: (i, c[0]))),
        out_shape=jax.ShapeDtypeStruct((d, N_CHIP * fc), BF16),
        compiler_params=_params(("parallel",)),
    )(chip1, w)


def _place_w_out(w, layer, chip1, name):
    _, rc, dd = w.shape
    tc = _tile(dd, 1024)

    def body(c_ref, w_ref, o_ref):
        o_ref[...] = w_ref[...].astype(BF16)

    return pl.pallas_call(
        body, name=name,
        grid_spec=pltpu.PrefetchScalarGridSpec(
            num_scalar_prefetch=1, grid=(dd // tc,),
            in_specs=[pl.BlockSpec((None, rc, tc), lambda j, c: (layer, 0, j))],
            out_specs=pl.BlockSpec((rc, tc), lambda j, c: (c[0], j))),
        out_shape=jax.ShapeDtypeStruct((N_CHIP * rc, dd), BF16),
        compiler_params=_params(("parallel",)),
    )(chip1, w)


def _weight_region(ref, tensor, chip):
    if tensor == 0:
        fc = ref.shape[1] // N_CHIP
        return ref.at[:, pl.ds(pl.multiple_of(chip * fc, LANES), fc)]
    rc = ref.shape[0] // N_CHIP
    return ref.at[pl.ds(pl.multiple_of(chip * rc, 8), rc), :]


def _gather_copies(ref, tensor, send_sems, recv_sems, landing):
    mx, my, mc = lax.axis_index("x"), lax.axis_index("y"), lax.axis_index("c")
    mine = _weight_region(ref, tensor, 2 * mx + my)
    copies = []
    for j in range(1, N_CHIP):
        peer = (_flip(mx, j & 2), _flip(my, j & 1), mc)
        dst = _weight_region(ref, tensor, 2 * peer[0] + peer[1]) if landing else mine
        idx = 2 * (j - 1) + tensor
        copies.append(pltpu.make_async_remote_copy(src_ref=mine, dst_ref=dst, send_sem=send_sems.at[idx],
                                                   recv_sem=recv_sems.at[idx], device_id=peer, device_id_type=MESH))
    return copies


def _gather_start(fi, fo, dep, name):
    ns = 2 * (N_CHIP - 1)

    def body(fi_ref, fo_ref, dep_ref, send_sems, recv_sems, fi_thru, fo_thru, token):
        for tensor, ref in enumerate((fi_ref, fo_ref)):
            for cp in _gather_copies(ref, tensor, send_sems, recv_sems, landing=False):
                cp.start()
        token[...] = jnp.zeros_like(token)

    return pl.pallas_call(
        body, name=name,
        out_shape=(pltpu.SemaphoreType.DMA((ns,)), pltpu.SemaphoreType.DMA((ns,)),
                   pltpu.HBM(fi.shape, fi.dtype), pltpu.HBM(fo.shape, fo.dtype),
                   jax.ShapeDtypeStruct((8, LANES), F32)),
        in_specs=(_HBM, _HBM, _ANY),
        out_specs=(_SEM, _SEM, _HBM, _HBM, pl.BlockSpec(memory_space=pltpu.VMEM)),
        input_output_aliases={0: 2, 1: 3},
        compiler_params=pltpu.CompilerParams(has_side_effects=_EFFECT),
    )(_in_hbm(fi), _in_hbm(fo), dep)


def _gather_wait(send_sems, recv_sems, buf, tensor, after, name):
    def body(buf_ref, send_sems, recv_sems, after_ref, buf_out):
        for cp in _gather_copies(buf_ref, tensor, send_sems, recv_sems, landing=True):
            cp.wait_send()
            cp.wait_recv()

    return pl.pallas_call(
        body, name=name,
        out_shape=pltpu.HBM(buf.shape, buf.dtype),
        in_specs=(_HBM, _SEM, _SEM, _ANY), out_specs=_HBM,
        input_output_aliases={0: 0},
        compiler_params=pltpu.CompilerParams(has_side_effects=_EFFECT),
    )(buf, send_sems, recv_sems, after)


def _w_in_half(ref, half, chip):
    hr = ref.shape[0] // 2
    fc = ref.shape[1] // N_CHIP
    return ref.at[pl.ds(pl.multiple_of(half * hr, 8), hr), pl.ds(pl.multiple_of(chip * fc, LANES), fc)]


def _half_copies_ici(ref, send_sems, recv_sems, landing):
    mx, my, mc = lax.axis_index("x"), lax.axis_index("y"), lax.axis_index("c")
    mine = _w_in_half(ref, mc, 2 * mx + my)
    copies = []
    for j in range(1, N_CHIP):
        peer = (_flip(mx, j & 2), _flip(my, j & 1), mc)
        dst = _w_in_half(ref, mc, 2 * peer[0] + peer[1]) if landing else mine
        copies.append(pltpu.make_async_remote_copy(src_ref=mine, dst_ref=dst, send_sem=send_sems.at[j - 1],
                                                   recv_sem=recv_sems.at[j - 1], device_id=peer, device_id_type=MESH))
    return copies


def _half_copies_d2d(ref, send_sems, recv_sems, landing):
    mx, my, mc = lax.axis_index("x"), lax.axis_index("y"), lax.axis_index("c")
    sib = (mx, my, 1 - mc)
    copies = []
    for j in range(1, N_CHIP):
        other = 2 * _flip(mx, j & 2) + _flip(my, j & 1)
        src = _w_in_half(ref, mc, other)
        dst = _w_in_half(ref, 1 - mc, other) if landing else src
        copies.append(pltpu.make_async_remote_copy(src_ref=src, dst_ref=dst, send_sem=send_sems.at[j - 1],
                                                   recv_sem=recv_sems.at[j - 1], device_id=sib, device_id_type=MESH))
    return copies


def _w_out_copies(ref, send_sems, recv_sems, landing):
    mx, my, mc = lax.axis_index("x"), lax.axis_index("y"), lax.axis_index("c")
    mine = _weight_region(ref, 1, 2 * mx + my)
    copies = []
    for j in range(1, N_CHIP):
        peer = (_flip(mx, j & 2), _flip(my, j & 1), mc)
        dst = _weight_region(ref, 1, 2 * peer[0] + peer[1]) if landing else mine
        copies.append(pltpu.make_async_remote_copy(src_ref=mine, dst_ref=dst, send_sem=send_sems.at[j - 1],
                                                   recv_sem=recv_sems.at[j - 1], device_id=peer, device_id_type=MESH))
    return copies


def _copies_start(buf, copies, dep, name):
    ns = N_CHIP - 1

    def body(buf_ref, dep_ref, send_sems, recv_sems, buf_thru, token):
        for cp in copies(buf_ref, send_sems, recv_sems, landing=False):
            cp.start()
        token[...] = jnp.zeros_like(token)

    return pl.pallas_call(
        body, name=name,
        out_shape=(pltpu.SemaphoreType.DMA((ns,)), pltpu.SemaphoreType.DMA((ns,)),
                   pltpu.HBM(buf.shape, buf.dtype), jax.ShapeDtypeStruct((8, LANES), F32)),
        in_specs=(_HBM, _ANY),
        out_specs=(_SEM, _SEM, _HBM, pl.BlockSpec(memory_space=pltpu.VMEM)),
        input_output_aliases={0: 2},
        compiler_params=pltpu.CompilerParams(has_side_effects=_EFFECT),
    )(_in_hbm(buf), dep)


def _copies_wait(send_sems, recv_sems, buf, copies, after, name):
    def body(buf_ref, send_sems, recv_sems, after_ref, buf_out):
        for cp in copies(buf_ref, send_sems, recv_sems, landing=True):
            cp.wait_send()
            cp.wait_recv()

    return pl.pallas_call(
        body, name=name,
        out_shape=pltpu.HBM(buf.shape, buf.dtype),
        in_specs=(_HBM, _SEM, _SEM, _ANY), out_specs=_HBM,
        input_output_aliases={0: 0},
        compiler_params=pltpu.CompilerParams(has_side_effects=_EFFECT),
    )(buf, send_sems, recv_sems, after)


def _scatter_copies(gi_ref, go_ref, pi_ref, po_ref, send_sems, recv_sems):
    mx, my, mc = lax.axis_index("x"), lax.axis_index("y"), lax.axis_index("c")
    hr, fc = pi_ref.shape[1:]
    ro = po_ref.shape[1]
    copies = []
    for k in range(1, N_DEV):
        peer = (_flip(mx, k & 4), _flip(my, k & 2), _flip(mc, k & 1))
        pchip = 2 * peer[0] + peer[1]
        src = (gi_ref.at[pl.ds(pl.multiple_of(peer[2] * hr, 8), hr), pl.ds(pl.multiple_of(pchip * fc, LANES), fc)],
               go_ref.at[pl.ds(pl.multiple_of((2 * pchip + peer[2]) * ro, 8), ro), :])
        dst = (pi_ref.at[k - 1], po_ref.at[k - 1])
        for t in range(2):
            idx = 2 * (k - 1) + t
            copies.append(pltpu.make_async_remote_copy(src_ref=src[t], dst_ref=dst[t], send_sem=send_sems.at[idx],
                                                       recv_sem=recv_sems.at[idx], device_id=peer,
                                                       device_id_type=MESH))
    return copies


def _scatter_start(gi, go, name):
    d, f = gi.shape
    dd = go.shape[1]
    ns = 2 * (N_DEV - 1)
    pi = lax.empty((N_DEV - 1, d // 2, f // N_CHIP), BF16)
    po = lax.empty((N_DEV - 1, d // N_DEV, dd), BF16)

    def body(gi_ref, go_ref, pi_ref, po_ref, send_sems, recv_sems, gi_thru, go_thru, pi_thru, po_thru, token):
        for cp in _scatter_copies(gi_ref, go_ref, pi_ref, po_ref, send_sems, recv_sems):
            cp.start()
        token[...] = jnp.zeros_like(token)

    return pl.pallas_call(
        body, name=name,
        out_shape=(pltpu.SemaphoreType.DMA((ns,)), pltpu.SemaphoreType.DMA((ns,)),
                   pltpu.HBM(gi.shape, gi.dtype), pltpu.HBM(go.shape, go.dtype),
                   pltpu.HBM(pi.shape, pi.dtype), pltpu.HBM(po.shape, po.dtype),
                   jax.ShapeDtypeStruct((8, LANES), F32)),
        in_specs=(_HBM, _HBM, _HBM, _HBM),
        out_specs=(_SEM, _SEM, _HBM, _HBM, _HBM, _HBM, pl.BlockSpec(memory_space=pltpu.VMEM)),
        input_output_aliases={0: 2, 1: 3, 2: 4, 3: 5},
        compiler_params=pltpu.CompilerParams(has_side_effects=_EFFECT),
    )(_in_hbm(gi), _in_hbm(go), _in_hbm(pi), _in_hbm(po))


def _scatter_wait(send_sems, recv_sems, gi, go, pi, po, after, name):
    def body(gi_ref, go_ref, pi_ref, po_ref, send_sems, recv_sems, *rest):
        for cp in _scatter_copies(gi_ref, go_ref, pi_ref, po_ref, send_sems, recv_sems):
            cp.wait_send()
            cp.wait_recv()

    return pl.pallas_call(
        body, name=name,
        out_shape=tuple(pltpu.HBM(a.shape, a.dtype) for a in (gi, go, pi, po)),
        in_specs=(_HBM, _HBM, _HBM, _HBM, _SEM, _SEM) + (_ANY,) * len(after), out_specs=(_HBM, _HBM, _HBM, _HBM),
        input_output_aliases={0: 0, 1: 1, 2: 2, 3: 3},
        compiler_params=pltpu.CompilerParams(has_side_effects=_EFFECT),
    )(gi, go, pi, po, send_sems, recv_sems, *after)


def _sum_into(buf, g, parts, where2, layer, row_blocks, dep, name):
    depth, r2, c = buf.shape
    r = r2 // 2
    tr, tc = _tile(r, 256), _tile(c, 1024)
    nr, nc = r // tr, c // tc
    col_blocks = (g.shape[1] // c) > 1

    def body(w_ref, buf_ref, g_ref, p_ref, dep_ref, o_ref):
        acc = g_ref[...].astype(F32)
        for e in range(N_DEV - 1):
            acc = acc + p_ref[e].astype(F32)
        o_ref[...] = acc

    return pl.pallas_call(
        body, name=name,
        grid_spec=pltpu.PrefetchScalarGridSpec(
            num_scalar_prefetch=1, grid=(nr, nc),
            in_specs=[_ANY,
                      pl.BlockSpec((tr, tc), lambda i, j, w: (row_blocks(w, nr) + i, (w[1] * nc if col_blocks else 0) + j)),
                      pl.BlockSpec((N_DEV - 1, tr, tc), lambda i, j, w: (0, i, j)), _ANY],
            out_specs=pl.BlockSpec((None, tr, tc), lambda i, j, w: (layer, w[0] * nr + i, j))),
        out_shape=jax.ShapeDtypeStruct(buf.shape, F32),
        input_output_aliases={1: 0},
        compiler_params=_params(("parallel", "parallel")),
    )(where2, buf, g, parts, dep)


def _exchange_halves(b_in, b_out, first, count, name):
    r_in = b_in.shape[1]
    r_out = b_out.shape[1]

    def body(bi_ref, bo_ref, oi_ref, oo_ref, send_sems, recv_sems):
        mx, my, mc = lax.axis_index("x"), lax.axis_index("y"), lax.axis_index("c")
        sib = (mx, my, 1 - mc)

        def half(ref, l, rows, which):
            return ref.at[l, pl.ds(pl.multiple_of(which * (rows // 2), 8), rows // 2), :]

        copies = []
        for l in range(first, first + count):
            for t, (src, dst, rows) in enumerate(((bi_ref, oi_ref, r_in), (bo_ref, oo_ref, r_out))):
                idx = 2 * (l - first) + t
                kw = dict(send_sem=send_sems.at[idx], recv_sem=recv_sems.at[idx], device_id=sib, device_id_type=MESH)
                cp = pltpu.make_async_remote_copy(src_ref=half(src, l, rows, mc), dst_ref=half(dst, l, rows, mc), **kw)
                cp.start()
                copies.append((cp, pltpu.make_async_remote_copy(src_ref=half(src, l, rows, mc),
                                                                dst_ref=half(dst, l, rows, 1 - mc), **kw)))
        for cp, landed in copies:
            landed.wait_recv()
        for cp, landed in copies:
            cp.wait_send()

    ns = 2 * count
    return pl.pallas_call(
        body, name=name,
        out_shape=[jax.ShapeDtypeStruct(b_in.shape, F32), jax.ShapeDtypeStruct(b_out.shape, F32)],
        in_specs=[_ANY, _ANY], out_specs=[_ANY, _ANY],
        input_output_aliases={0: 0, 1: 1},
        scratch_shapes=[pltpu.SemaphoreType.DMA((ns,)), pltpu.SemaphoreType.DMA((ns,))],
    )(b_in, b_out)


def _rows8(v):
    return jnp.pad(v, ((0, 8 - v.shape[0]), (0, 0)))


def kernel(x, c, norm_gain, w_ada, b_ada, w_in, w_out, ret_decay_logit_f, ret_decay_logit_b, final_gain, loss_target, m_norm_gain, m_w_ada, m_b_ada, m_w_in, m_w_out, m_ret_decay_logit_f, m_ret_decay_logit_b, m_final_gain, v_norm_gain, v_w_ada, v_b_ada, v_w_in, v_w_out, v_ret_decay_logit_f, v_ret_decay_logit_b, v_final_gain):
    cfg = _Cfg()
    depth, d = norm_gain.shape
    mx, my, mc = lax.axis_index("x"), lax.axis_index("y"), lax.axis_index("c")
    me = 4 * mx + 2 * my + mc
    chip = 2 * mx + my
    x0 = x[0]
    tgt = loss_target[0]
    ada_cols = w_ada.shape[2]

    chip1 = jnp.reshape(chip, (1,)).astype(jnp.int32)
    where2 = jnp.stack([mc, chip]).astype(jnp.int32)

    def start_gather(l, dep):
        return _gather_start(_place_w_in(w_in, l, chip1, f"place_w_in_{l}"),
                             _place_w_out(w_out, l, chip1, f"place_w_out_{l}"), dep, f"gather_start_{l}")

    c_all = _all_gather_small(_rows8(c), "gather_c")[:, 0, :]
    cact = _silu(c_all)
    ici = _copies_start(_place_w_in(w_in, 0, chip1, "place_w_in_0"), _half_copies_ici, c_all, "gather0_ici_start")
    cact_rows = jnp.pad(cact, ((0, _ADA_ROWS - N_DEV), (0, 0))) + ici[3][0:1, 0:1]
    mod_part = _ada_fwd(cact_rows.astype(BF16), w_ada, "ada_fwd")
    mod_all = _all_gather_small(mod_part.reshape(depth * _ADA_ROWS, ada_cols), "gather_mod")
    mod_all = mod_all.reshape(N_CHIP, 2, depth, _ADA_ROWS, ada_cols)[:, 0]
    mod_mine = lax.dynamic_index_in_dim(mod_all, me, axis=2, keepdims=False)
    mod = jnp.transpose(mod_mine, (1, 0, 2)).reshape(depth, 3, d)
    bias = b_ada.reshape(depth, 3, d)
    fi0 = _copies_wait(ici[0], ici[1], ici[2], _half_copies_ici, mod, "gather0_ici_wait")
    d2d = _copies_start(fi0, _half_copies_d2d, mod, "gather0_d2d_start")
    out0 = _copies_start(_place_w_out(w_out, 0, chip1, "place_w_out_0"), _w_out_copies, d2d[3], "gather0_out_start")
    w_in_0 = _copies_wait(d2d[0], d2d[1], d2d[2], _half_copies_d2d, out0[3], "gather0_d2d_wait")

    slopes = jnp.exp2(-8.0 * (jnp.arange(cfg.ha, dtype=F32) + 1.0) / cfg.ha)
    lg_f = jax.nn.log_sigmoid(ret_decay_logit_f)
    lg_b = jax.nn.log_sigmoid(ret_decay_logit_b)

    saved = []
    w_full = []
    h_x = x0
    pending = None
    for l in range(depth):
        if l == 0:
            w_in_l = w_in_0
        else:
            send_sems, recv_sems, fi, fo, _ = pending
            w_in_l = _gather_wait(send_sems, recv_sems, fi, 0, h_x, f"gather_wait_in_{l}")
        g8 = _rows8(norm_gain[l:l + 1])
        if l + 1 < depth:
            pending = start_gather(l + 1, w_in_l)
            g8 = g8 + pending[4][0:1, 0:1]
        mod3, b3 = _rows8(mod[l]), _rows8(bias[l])
        hb = _norm_mod_fwd(h_x, g8, mod3, b3, f"norm_mod_fwd_{l}")
        proj = _matmul(hb, w_in_l, "nn", BF16, f"in_proj_{l}")
        oa, lse, y = _attn_fwd(cfg, proj, slopes, f"attn_fwd_{l}")
        lgs = jnp.stack([lg_f[l], lg_b[l]])
        oret, y = _ret_fwd(cfg, proj, lgs, y, f"ret_fwd_{l}")
        if l == 0:
            w_out_l = _copies_wait(out0[0], out0[1], out0[2], _w_out_copies, y, "gather0_out_wait")
        else:
            w_out_l = _gather_wait(send_sems, recv_sems, fo, 1, y, f"gather_wait_out_{l}")
        w_full.append((w_in_l, w_out_l))
        x_next, out = _out_proj_fwd(y, w_out_l, h_x, mod3, b3, f"out_proj_{l}")
        saved.append((h_x, hb, proj, oret, y, oa, lse, out, g8, mod3, b3, lgs))
        h_x = x_next

    dx, loss8, fin_acc = _final_loss(h_x, tgt, _rows8(final_gain[None]), "final_loss")

    landed = [None] * depth
    d_mod, d_gain, d_lg = [None] * depth, [None] * depth, [None] * depth
    in_flight = None
    for l in reversed(range(depth)):
        x_l, hb, proj, oret, y, oa, lse, out, g8, mod3, b3, lgs = saved[l]
        w_in_l, w_out_l = w_full[l]
        douts, gate_acc = _out_proj_bwd_prep(dx, out, mod3, b3, f"out_proj_bwd_prep_{l}")
        dy = _matmul(douts, w_out_l, "nt", BF16, f"out_proj_dy_{l}")
        g_out_l = _matmul(y, douts, "tn", BF16, f"out_proj_dw_{l}", tk=4096)
        dproj = _attn_bwd(cfg, proj, slopes, dy, oa, lse, f"attn_bwd_{l}")
        dproj, dlg = _ret_bwd(cfg, proj, lgs, dy, oret, dproj, f"ret_bwd_{l}")
        g_in_l = _matmul(hb, dproj, "tn", BF16, f"in_proj_dw_{l}", tk=4096)
        started = _scatter_start(g_in_l, g_out_l, f"scatter_start_{l}")
        dh = _matmul(dproj, w_in_l, "nt", F32, f"in_proj_dh_{l}", tm=512, tk=7168)
        g8 = g8 + started[-1][0:1, 0:1]
        dx, nm_acc = _norm_mod_bwd(dh, x_l, dx, g8, mod3, b3, f"norm_mod_bwd_{l}")
        d_mod[l] = jnp.concatenate([nm_acc[0], nm_acc[1], gate_acc[0]])
        d_gain[l] = nm_acc[2]
        d_lg[l] = dlg[:, 0:2, 0]
        if in_flight is not None:
            landed[l + 1] = _scatter_wait(*in_flight[:-1], (dx,), f"scatter_wait_{l + 1}")
        in_flight = started

    gw_in = lax.empty(w_in.shape, F32)
    gw_out = lax.empty(w_out.shape, F32)
    res_in = res_out = None
    for first, count in ((1, depth - 1), (0, 1)):
        if first == 0:
            after = (dx,) if res_out is None else (res_in[1], res_out[1])
            landed[0] = _scatter_wait(*in_flight[:-1], after, "scatter_wait_0")
        if count == 0:
            continue
        dep = in_flight[-1]
        for l in range(first, first + count):
            gi, go, pi, po = landed[l]
            gw_in = _sum_into(gw_in, gi, pi, where2, l, lambda w, nr: w[0] * nr, dep, f"sum_w_in_{l}")
            gw_out = _sum_into(gw_out, go, po, where2, l, lambda w, nr: (2 * w[1] + w[0]) * nr, dep,
                               f"sum_w_out_{l}")
        gw_in, gw_out = _exchange_halves(gw_in, gw_out, first, count, f"exchange_halves_{first}")
        res_in = _adamw_big(w_in, gw_in, m_w_in, v_w_in, first, count, res_in, f"adamw_w_in_{first}")
        res_out = _adamw_big(w_out, gw_out, m_w_out, v_w_out, first, count, res_out, f"adamw_w_out_{first}")
    grad_w_in, delta_w_in, new_m_w_in, new_v_w_in = res_in
    grad_w_out, delta_w_out, new_m_w_out, new_v_w_out = res_out

    dmod_mine = jnp.stack(d_mod)
    dmod_gathered = _all_gather_small(_rows8(dmod_mine), "gather_dmod")
    dmod_all = dmod_gathered[:, :depth, :]
    grad_b_ada = _sum_gathered(dmod_gathered, "sum_b_ada")[:depth]
    dmod_cols = lax.dynamic_slice_in_dim(dmod_all, chip * ada_cols, ada_cols, axis=2)
    dmod128 = jnp.pad(jnp.transpose(dmod_cols, (1, 0, 2)), ((0, 0), (0, LANES - N_DEV), (0, 0))).astype(BF16)
    cact128 = jnp.pad(cact, ((0, LANES - N_DEV), (0, 0))).astype(BF16)
    grad_w_ada, delta_w_ada, new_m_w_ada, new_v_w_ada = _adamw_ada(w_ada, m_w_ada, v_w_ada, cact128, dmod128,
                                                                  "adamw_w_ada")

    dlg_all = jnp.stack(d_lg)
    sig_f = jax.nn.sigmoid(-ret_decay_logit_f)
    sig_b = jax.nn.sigmoid(-ret_decay_logit_b)
    nlg = depth * cfg.hr
    pack = jnp.zeros((8, d), F32)
    for l in range(depth):
        pack = pack.at[l].set(d_gain[l])
    pack = pack.at[depth].set(fin_acc[0])
    pack = pack.at[depth + 1, 0].set(loss8[0, 0])
    pack = pack.at[depth + 1, LANES:LANES + nlg].set((dlg_all[:, :, 0] * sig_f).reshape(-1))
    pack = pack.at[depth + 1, 2 * LANES:2 * LANES + nlg].set((dlg_all[:, :, 1] * sig_b).reshape(-1))
    tot = _sum_gathered(_all_gather_small(pack, "gather_small"), "sum_small")
    grad_norm_gain = tot[:depth]
    grad_final_gain = tot[depth]
    loss = tot[depth + 1, 0]
    grad_lf = tot[depth + 1, LANES:LANES + nlg].reshape(depth, cfg.hr)
    grad_lb = tot[depth + 1, 2 * LANES:2 * LANES + nlg].reshape(depth, cfg.hr)

    d_ng, m_ng, v_ng = _adamw_small(norm_gain, grad_norm_gain, m_norm_gain, v_norm_gain, "adamw_norm_gain")
    d_ba, m_ba, v_ba = _adamw_small(b_ada, grad_b_ada, m_b_ada, v_b_ada, "adamw_b_ada")
    d_lf, m_lf, v_lf = _adamw_small(ret_decay_logit_f, grad_lf, m_ret_decay_logit_f, v_ret_decay_logit_f, "adamw_lf")
    d_lb, m_lb, v_lb = _adamw_small(ret_decay_logit_b, grad_lb, m_ret_decay_logit_b, v_ret_decay_logit_b, "adamw_lb")
    d_fg, m_fg, v_fg = _adamw_small(final_gain[None], grad_final_gain[None], m_final_gain[None], v_final_gain[None],
                                    "adamw_final_gain")

    return (loss, dx[None],
            grad_norm_gain, grad_w_ada, grad_b_ada, grad_w_in, grad_w_out, grad_lf, grad_lb, grad_final_gain,
            d_ng, delta_w_ada, d_ba, delta_w_in, delta_w_out, d_lf, d_lb, d_fg[0],
            m_ng, new_m_w_ada, m_ba, new_m_w_in, new_m_w_out, m_lf, m_lb, m_fg[0],
            v_ng, new_v_w_ada, v_ba, new_v_w_in, new_v_w_out, v_lf, v_lb, v_fg[0])
```
